```python
import math
import jax, jax.numpy as jnp
from jax import lax
import numpy as np

D_MODEL = 2048
BATCH = 8
SEQ = 4096
DEPTH = 1

MEM_LEN = 256
LN_EPS = 1e-5
GLA_HEADS = 4
GLA_DV = (D_MODEL // 2) // GLA_HEADS
GLA_DK = GLA_DV // 2
GLA_GATE_RANK = 16
GLA_TAU = 16.0
GLA_CHUNK = 64
DIL_HD = 128
DIL_HEADS = (D_MODEL // 2) // DIL_HD
DIL_PATTERNS = ((128, 1), (512, 4), (2048, 16))
ROPE_THETA = 500000.0
ROPE_DIMS = DIL_HD // 4
CA_HEADS = 4
CA_HD = D_MODEL // CA_HEADS
D_FF = 5504
CONV_W = 3
DEEPNORM_ALPHA = (2.0 * DEPTH) ** 0.25
DEEPNORM_BETA = (8.0 * DEPTH) ** -0.25
IN_WIDTHS = (GLA_HEADS * GLA_DK, GLA_HEADS * GLA_DK, GLA_HEADS * GLA_DV, GLA_HEADS * GLA_DV,
             GLA_GATE_RANK, DIL_HEADS * DIL_HD, DIL_HEADS * DIL_HD, DIL_HEADS * DIL_HD)
IN_COLS = sum(IN_WIDTHS)
MIX_WIDTH = GLA_HEADS * GLA_DV + DIL_HEADS * DIL_HD

kernel_name = "hybrid_gla_dilated_attn_deepnorm_layer"


def split_cols(h, widths):
    outs, start = [], 0
    for w in widths:
        outs.append(h[..., start:start + w])
        start += w
    return outs


def layer_norm(x, g, b):
    xf = x.astype(jnp.float32)
    mu = jnp.mean(xf, axis=-1, keepdims=True)
    var = jnp.mean(jnp.square(xf - mu), axis=-1, keepdims=True)
    y = (xf - mu) * lax.rsqrt(var + LN_EPS)
    return (y * g.astype(jnp.float32) + b.astype(jnp.float32)).astype(x.dtype)


def partial_rotary(t, cos, sin):
    half = ROPE_DIMS // 2
    t1 = t[..., :half].astype(jnp.float32)
    t2 = t[..., half:ROPE_DIMS].astype(jnp.float32)
    rot = jnp.concatenate([t1 * cos - t2 * sin, t2 * cos + t1 * sin], axis=-1)
    return jnp.concatenate([rot.astype(t.dtype), t[..., ROPE_DIMS:]], axis=-1)


def gla_chunked(q, k, v, log_g):
    B, S, H, dk = q.shape
    dv = v.shape[-1]
    C = GLA_CHUNK
    N = S // C

    def chunk(t):
        return t.astype(jnp.float32).reshape(B, N, C, H, -1).transpose(0, 3, 1, 2, 4)

    qc = chunk(q) * (dk ** -0.5)
    kc, vc, gc = chunk(k), chunk(v), chunk(log_g)
    b = lax.cumsum(gc, axis=3)
    b_last = b[:, :, :, -1:, :]
    q_in = qc * jnp.exp(b)
    k_in = kc * jnp.exp(-b)
    k_end = kc * jnp.exp(b_last - b)
    causal = jnp.tril(jnp.ones((C, C), dtype=bool))
    A = jnp.where(causal, jnp.einsum('bhnik,bhnjk->bhnij', q_in, k_in), 0.0)
    o = jnp.einsum('bhnij,bhnjv->bhniv', A, vc)
    dS = jnp.einsum('bhnjk,bhnjv->bhnkv', k_end, vc)
    decay = jnp.exp(b_last[:, :, :, 0, :])

    def step(state, inp):
        dec, ds = inp
        return dec[..., None] * state + ds, state

    _, s_before = lax.scan(step, jnp.zeros((B, H, dk, dv), jnp.float32),
                           (jnp.moveaxis(decay, 2, 0), jnp.moveaxis(dS, 2, 0)))
    s_before = jnp.moveaxis(s_before, 0, 2)
    o = o + jnp.einsum('bhnik,bhnkv->bhniv', q_in, s_before)
    return o.transpose(0, 2, 3, 1, 4).reshape(B, S, H, dv)


def dilated_branch(q, k, v, window, dilation):
    B, S, H, hd = q.shape
    L = S // dilation
    band = window // dilation
    nb = -(-L // band)
    Lp = nb * band

    def to_sub(t):
        t = t.reshape(B, L, dilation, H, hd).transpose(0, 2, 3, 1, 4)
        t = jnp.pad(t, ((0, 0), (0, 0), (0, 0), (0, Lp - L), (0, 0)))
        return t.reshape(B, dilation, H, nb, band, hd)

    def with_prev(t):
        prev = jnp.pad(t[:, :, :, :-1], ((0, 0), (0, 0), (0, 0), (1, 0), (0, 0), (0, 0)))
        return jnp.concatenate([prev, t], axis=4)

    qb = to_sub(q)
    kk = with_prev(to_sub(k))
    vv = with_prev(to_sub(v))
    s = jnp.einsum('bdhnqc,bdhnkc->bdhnqk', qb, kk).astype(jnp.float32)
    qi = jnp.arange(band)[:, None] + band
    kj = jnp.arange(2 * band)[None, :]
    diff = qi - kj
    blk = jnp.arange(nb)[:, None, None]
    mask = (diff >= 0) & (diff <= band) & ((blk > 0) | (kj >= band))
    s = jnp.where(mask, s, -jnp.inf)
    m = jnp.max(s, axis=-1, keepdims=True)
    p = jnp.exp(s - m)
    den = jnp.sum(p, axis=-1, keepdims=True)
    o = jnp.einsum('bdhnqk,bdhnkc->bdhnqc', p.astype(v.dtype), vv).astype(jnp.float32) / den
    lse = (m + jnp.log(den))[..., 0]
    o = o.reshape(B, dilation, H, Lp, hd)[:, :, :, :L].transpose(0, 3, 1, 2, 4).reshape(B, S, H, hd)
    lse = lse.reshape(B, dilation, H, Lp)[:, :, :, :L].transpose(0, 3, 1, 2).reshape(B, S, H)
    return o, lse


def dilated_attention(q, k, v):
    outs, lses = [], []
    for window, dilation in DIL_PATTERNS:
        o, lse = dilated_branch(q, k, v, window, dilation)
        outs.append(o)
        lses.append(lse)
    w = jax.nn.softmax(jnp.stack(lses, axis=0), axis=0)
    return jnp.sum(w[..., None] * jnp.stack(outs, axis=0), axis=0)


def causal_dwconv(u, w, b):
    S = u.shape[1]
    up = jnp.pad(u, ((0, 0), (CONV_W - 1, 0), (0, 0)))
    y = b + w[0] * up[:, 0:S]
    for i in range(1, CONV_W):
        y = y + w[i] * up[:, i:i + S]
    return y


def _fwd_setup_inputs(seed: int = 0) -> dict:
    key = jax.random.key(seed)
    ks = jax.random.split(key, 24)
    f32 = jnp.float32
    nrm = lambda k, shape, std: jax.random.normal(k, shape, f32) * std
    beta = DEEPNORM_BETA
    x = jax.random.normal(ks[0], (BATCH, SEQ, D_MODEL), f32)
    mem = jax.random.normal(ks[1], (BATCH, MEM_LEN, D_MODEL), f32)
    offs = jax.random.randint(ks[2], (BATCH, 1), 0, 4096, dtype=jnp.int32)
    positions = offs + jnp.arange(SEQ, dtype=jnp.int32)[None, :]
    col_scale = jnp.concatenate([jnp.full((w,), beta if i in (2, 7) else 1.0, f32)
                                 for i, w in enumerate(IN_WIDTHS)])
    w_in = nrm(ks[3], (DEPTH, D_MODEL, IN_COLS), D_MODEL ** -0.5) * col_scale
    gla_gate_w2 = nrm(ks[4], (DEPTH, GLA_GATE_RANK, GLA_HEADS * GLA_DK), GLA_GATE_RANK ** -0.5)
    gla_gate_b = nrm(ks[5], (DEPTH, GLA_HEADS * GLA_DK), 0.01)
    gla_norm_g = 1.0 + nrm(ks[6], (DEPTH, GLA_HEADS * GLA_DV), 0.02)
    w_out = nrm(ks[7], (DEPTH, MIX_WIDTH, D_MODEL), MIX_WIDTH ** -0.5) * beta
    ln1_g = 1.0 + nrm(ks[8], (DEPTH, D_MODEL), 0.02)
    ln1_b = nrm(ks[9], (DEPTH, D_MODEL), 0.02)
    ca_wq = nrm(ks[10], (DEPTH, D_MODEL, D_MODEL), D_MODEL ** -0.5)
    kv_scale = jnp.concatenate([jnp.ones((D_MODEL,), f32), jnp.full((D_MODEL,), beta, f32)])
    ca_wkv = nrm(ks[11], (DEPTH, D_MODEL, 2 * D_MODEL), D_MODEL ** -0.5) * kv_scale
    ca_wo = nrm(ks[12], (DEPTH, D_MODEL, D_MODEL), D_MODEL ** -0.5) * beta
    ln2_g = 1.0 + nrm(ks[13], (DEPTH, D_MODEL), 0.02)
    ln2_b = nrm(ks[14], (DEPTH, D_MODEL), 0.02)
    ffn_w_in = nrm(ks[15], (DEPTH, D_MODEL, 2 * D_FF), D_MODEL ** -0.5) * beta
    ffn_conv_w = nrm(ks[16], (DEPTH, CONV_W, 2 * D_FF), CONV_W ** -0.5)
    ffn_conv_b = nrm(ks[17], (DEPTH, 2 * D_FF), 0.02)
    ffn_w_out = nrm(ks[18], (DEPTH, D_FF, D_MODEL), D_FF ** -0.5) * beta
    ln3_g = 1.0 + nrm(ks[19], (DEPTH, D_MODEL), 0.02)
    ln3_b = nrm(ks[20], (DEPTH, D_MODEL), 0.02)
    return {"x": x, "mem": mem, "positions": positions, "w_in": w_in,
            "gla_gate_w2": gla_gate_w2, "gla_gate_b": gla_gate_b, "gla_norm_g": gla_norm_g,
            "w_out": w_out, "ln1_g": ln1_g, "ln1_b": ln1_b,
            "ca_wq": ca_wq, "ca_wkv": ca_wkv, "ca_wo": ca_wo, "ln2_g": ln2_g, "ln2_b": ln2_b,
            "ffn_w_in": ffn_w_in, "ffn_conv_w": ffn_conv_w, "ffn_conv_b": ffn_conv_b,
            "ffn_w_out": ffn_w_out, "ln3_g": ln3_g, "ln3_b": ln3_b}


def _fwd_reference(x, mem, positions, w_in, gla_gate_w2, gla_gate_b, gla_norm_g, w_out, ln1_g, ln1_b,
              ca_wq, ca_wkv, ca_wo, ln2_g, ln2_b, ffn_w_in, ffn_conv_w, ffn_conv_b, ffn_w_out,
              ln3_g, ln3_b):
    B, S, D = x.shape
    M = mem.shape[1]
    inv_freq = ROPE_THETA ** (-jnp.arange(0, ROPE_DIMS, 2, dtype=jnp.float32) / ROPE_DIMS)
    ang = positions.astype(jnp.float32)[..., None] * inv_freq
    cos = jnp.cos(ang)[:, :, None, :]
    sin = jnp.sin(ang)[:, :, None, :]

    for l in range(DEPTH):
        h = x @ w_in[l]
        qg, kg, vg, rg, glr, qd, kd, vd = split_cols(h, IN_WIDTHS)
        log_g = jax.nn.log_sigmoid((glr @ gla_gate_w2[l] + gla_gate_b[l]).astype(jnp.float32)) / GLA_TAU
        o_g = gla_chunked(qg.reshape(B, S, GLA_HEADS, GLA_DK), kg.reshape(B, S, GLA_HEADS, GLA_DK),
                          vg.reshape(B, S, GLA_HEADS, GLA_DV), log_g.reshape(B, S, GLA_HEADS, GLA_DK))
        mu = jnp.mean(o_g, axis=-1, keepdims=True)
        var = jnp.mean(jnp.square(o_g - mu), axis=-1, keepdims=True)
        o_g = ((o_g - mu) * lax.rsqrt(var + LN_EPS)).reshape(B, S, GLA_HEADS * GLA_DV)
        o_g = (o_g * gla_norm_g[l].astype(jnp.float32) * jax.nn.silu(rg.astype(jnp.float32))).astype(x.dtype)
        qd = partial_rotary(qd.reshape(B, S, DIL_HEADS, DIL_HD) * (DIL_HD ** -0.5), cos, sin)
        kd = partial_rotary(kd.reshape(B, S, DIL_HEADS, DIL_HD), cos, sin)
        o_d = dilated_attention(qd, kd, vd.reshape(B, S, DIL_HEADS, DIL_HD))
        o_d = o_d.reshape(B, S, DIL_HEADS * DIL_HD).astype(x.dtype)
        mix = jnp.concatenate([o_g, o_d], axis=-1) @ w_out[l]
        x = layer_norm(DEEPNORM_ALPHA * x + mix, ln1_g[l], ln1_b[l])

        q = (x @ ca_wq[l]).reshape(B, S, CA_HEADS, CA_HD)
        mk, mv = split_cols(mem @ ca_wkv[l], (D_MODEL, D_MODEL))
        mk = mk.reshape(B, M, CA_HEADS, CA_HD)
        mv = mv.reshape(B, M, CA_HEADS, CA_HD)
        s = jnp.einsum('bshc,bmhc->bhsm', q, mk).astype(jnp.float32) * (CA_HD ** -0.5)
        p = jax.nn.softmax(s, axis=-1).astype(mv.dtype)
        o_c = jnp.einsum('bhsm,bmhc->bshc', p, mv).reshape(B, S, D_MODEL)
        x = layer_norm(DEEPNORM_ALPHA * x + o_c @ ca_wo[l], ln2_g[l], ln2_b[l])

        u = causal_dwconv(x @ ffn_w_in[l], ffn_conv_w[l], ffn_conv_b[l])
        gate, up = split_cols(u, (D_FF, D_FF))
        f = (jax.nn.silu(gate) * up) @ ffn_w_out[l]
        x = layer_norm(DEEPNORM_ALPHA * x + f, ln3_g[l], ln3_b[l])
    return x


import jax as _jax
import jax.numpy as _jnp

TWIN_FORMAT = 'train_step'
FWD_PARAMS = ['x', 'mem', 'positions', 'w_in', 'gla_gate_w2', 'gla_gate_b', 'gla_norm_g', 'w_out', 'ln1_g', 'ln1_b', 'ca_wq', 'ca_wkv', 'ca_wo', 'ln2_g', 'ln2_b', 'ffn_w_in', 'ffn_conv_w', 'ffn_conv_b', 'ffn_w_out', 'ln3_g', 'ln3_b']
TWIN_WEIGHTS = ['w_in', 'gla_gate_w2', 'gla_gate_b', 'gla_norm_g', 'w_out', 'ln1_g', 'ln1_b', 'ca_wq', 'ca_wkv', 'ca_wo', 'ln2_g', 'ln2_b', 'ffn_w_in', 'ffn_conv_w', 'ffn_conv_b', 'ffn_w_out', 'ln3_g', 'ln3_b']
TWIN_DIFF_INPUT = 'x'
TWIN_INPUTS = ['x', 'mem', 'positions', 'w_in', 'gla_gate_w2', 'gla_gate_b', 'gla_norm_g', 'w_out', 'ln1_g', 'ln1_b', 'ca_wq', 'ca_wkv', 'ca_wo', 'ln2_g', 'ln2_b', 'ffn_w_in', 'ffn_conv_w', 'ffn_conv_b', 'ffn_w_out', 'ln3_g', 'ln3_b', 'loss_target', 'm_w_in', 'm_gla_gate_w2', 'm_gla_gate_b', 'm_gla_norm_g', 'm_w_out', 'm_ln1_g', 'm_ln1_b', 'm_ca_wq', 'm_ca_wkv', 'm_ca_wo', 'm_ln2_g', 'm_ln2_b', 'm_ffn_w_in', 'm_ffn_conv_w', 'm_ffn_conv_b', 'm_ffn_w_out', 'm_ln3_g', 'm_ln3_b', 'v_w_in', 'v_gla_gate_w2', 'v_gla_gate_b', 'v_gla_norm_g', 'v_w_out', 'v_ln1_g', 'v_ln1_b', 'v_ca_wq', 'v_ca_wkv', 'v_ca_wo', 'v_ln2_g', 'v_ln2_b', 'v_ffn_w_in', 'v_ffn_conv_w', 'v_ffn_conv_b', 'v_ffn_w_out', 'v_ln3_g', 'v_ln3_b']
TWIN_OUTPUTS = ['loss', 'grad_x', 'grad_w_in', 'grad_gla_gate_w2', 'grad_gla_gate_b', 'grad_gla_norm_g', 'grad_w_out', 'grad_ln1_g', 'grad_ln1_b', 'grad_ca_wq', 'grad_ca_wkv', 'grad_ca_wo', 'grad_ln2_g', 'grad_ln2_b', 'grad_ffn_w_in', 'grad_ffn_conv_w', 'grad_ffn_conv_b', 'grad_ffn_w_out', 'grad_ln3_g', 'grad_ln3_b', 'delta_w_in', 'delta_gla_gate_w2', 'delta_gla_gate_b', 'delta_gla_norm_g', 'delta_w_out', 'delta_ln1_g', 'delta_ln1_b', 'delta_ca_wq', 'delta_ca_wkv', 'delta_ca_wo', 'delta_ln2_g', 'delta_ln2_b', 'delta_ffn_w_in', 'delta_ffn_conv_w', 'delta_ffn_conv_b', 'delta_ffn_w_out', 'delta_ln3_g', 'delta_ln3_b', 'new_m_w_in', 'new_m_gla_gate_w2', 'new_m_gla_gate_b', 'new_m_gla_norm_g', 'new_m_w_out', 'new_m_ln1_g', 'new_m_ln1_b', 'new_m_ca_wq', 'new_m_ca_wkv', 'new_m_ca_wo', 'new_m_ln2_g', 'new_m_ln2_b', 'new_m_ffn_w_in', 'new_m_ffn_conv_w', 'new_m_ffn_conv_b', 'new_m_ffn_w_out', 'new_m_ln3_g', 'new_m_ln3_b', 'new_v_w_in', 'new_v_gla_gate_w2', 'new_v_gla_gate_b', 'new_v_gla_norm_g', 'new_v_w_out', 'new_v_ln1_g', 'new_v_ln1_b', 'new_v_ca_wq', 'new_v_ca_wkv', 'new_v_ca_wo', 'new_v_ln2_g', 'new_v_ln2_b', 'new_v_ffn_w_in', 'new_v_ffn_conv_w', 'new_v_ffn_conv_b', 'new_v_ffn_w_out', 'new_v_ln3_g', 'new_v_ln3_b']
TWIN_LEAF_KINDS = {'loss': 'loss', 'grad_x': 'grad_x', 'grad_w_in': 'grad_w', 'grad_gla_gate_w2': 'grad_w', 'grad_gla_gate_b': 'grad_w', 'grad_gla_norm_g': 'grad_w', 'grad_w_out': 'grad_w', 'grad_ln1_g': 'grad_w', 'grad_ln1_b': 'grad_w', 'grad_ca_wq': 'grad_w', 'grad_ca_wkv': 'grad_w', 'grad_ca_wo': 'grad_w', 'grad_ln2_g': 'grad_w', 'grad_ln2_b': 'grad_w', 'grad_ffn_w_in': 'grad_w', 'grad_ffn_conv_w': 'grad_w', 'grad_ffn_conv_b': 'grad_w', 'grad_ffn_w_out': 'grad_w', 'grad_ln3_g': 'grad_w', 'grad_ln3_b': 'grad_w', 'delta_w_in': 'delta_w', 'delta_gla_gate_w2': 'delta_w', 'delta_gla_gate_b': 'delta_w', 'delta_gla_norm_g': 'delta_w', 'delta_w_out': 'delta_w', 'delta_ln1_g': 'delta_w', 'delta_ln1_b': 'delta_w', 'delta_ca_wq': 'delta_w', 'delta_ca_wkv': 'delta_w', 'delta_ca_wo': 'delta_w', 'delta_ln2_g': 'delta_w', 'delta_ln2_b': 'delta_w', 'delta_ffn_w_in': 'delta_w', 'delta_ffn_conv_w': 'delta_w', 'delta_ffn_conv_b': 'delta_w', 'delta_ffn_w_out': 'delta_w', 'delta_ln3_g': 'delta_w', 'delta_ln3_b': 'delta_w', 'new_m_w_in': 'new_m', 'new_m_gla_gate_w2': 'new_m', 'new_m_gla_gate_b': 'new_m', 'new_m_gla_norm_g': 'new_m', 'new_m_w_out': 'new_m', 'new_m_ln1_g': 'new_m', 'new_m_ln1_b': 'new_m', 'new_m_ca_wq': 'new_m', 'new_m_ca_wkv': 'new_m', 'new_m_ca_wo': 'new_m', 'new_m_ln2_g': 'new_m', 'new_m_ln2_b': 'new_m', 'new_m_ffn_w_in': 'new_m', 'new_m_ffn_conv_w': 'new_m', 'new_m_ffn_conv_b': 'new_m', 'new_m_ffn_w_out': 'new_m', 'new_m_ln3_g': 'new_m', 'new_m_ln3_b': 'new_m', 'new_v_w_in': 'new_v', 'new_v_gla_gate_w2': 'new_v', 'new_v_gla_gate_b': 'new_v', 'new_v_gla_norm_g': 'new_v', 'new_v_w_out': 'new_v', 'new_v_ln1_g': 'new_v', 'new_v_ln1_b': 'new_v', 'new_v_ca_wq': 'new_v', 'new_v_ca_wkv': 'new_v', 'new_v_ca_wo': 'new_v', 'new_v_ln2_g': 'new_v', 'new_v_ln2_b': 'new_v', 'new_v_ffn_w_in': 'new_v', 'new_v_ffn_conv_w': 'new_v', 'new_v_ffn_conv_b': 'new_v', 'new_v_ffn_w_out': 'new_v', 'new_v_ln3_g': 'new_v', 'new_v_ln3_b': 'new_v'}


def _forward(args):
    return _fwd_reference(*[args[k] for k in FWD_PARAMS])


def _output_shape():
    def fwd():
        inp = _fwd_setup_inputs(0)
        return _fwd_reference(*[inp[k] for k in FWD_PARAMS])
    out = _jax.eval_shape(fwd)
    return out.shape, out.dtype

N_MICROBATCH = 1
ADAM_LR = 0.001
ADAM_B1 = 0.9
ADAM_B2 = 0.999
ADAM_EPS = 1e-08
ADAM_WD = 0.01
ADAM_STEP = 10
PER_EXAMPLE_BATCH_AXIS = {'x': 0, 'mem': 0, 'positions': 0, 'loss_target': 0}
SHARED_INPUTS = []
_WEIGHT_DTYPES = {'w_in': _jnp.float32, 'gla_gate_w2': _jnp.float32, 'gla_gate_b': _jnp.float32, 'gla_norm_g': _jnp.float32, 'w_out': _jnp.float32, 'ln1_g': _jnp.float32, 'ln1_b': _jnp.float32, 'ca_wq': _jnp.float32, 'ca_wkv': _jnp.float32, 'ca_wo': _jnp.float32, 'ln2_g': _jnp.float32, 'ln2_b': _jnp.float32, 'ffn_w_in': _jnp.float32, 'ffn_conv_w': _jnp.float32, 'ffn_conv_b': _jnp.float32, 'ffn_w_out': _jnp.float32, 'ln3_g': _jnp.float32, 'ln3_b': _jnp.float32}
MOMENT_SCALE = {'w_in': 2.652733e-02, 'gla_gate_w2': 4.208145e-03, 'gla_gate_b': 1.634229e-02, 'gla_norm_g': 2.642142e-02, 'w_out': 3.158974e-02, 'ln1_g': 5.719499e-01, 'ln1_b': 2.816840e-01, 'ca_wq': 2.698769e-03, 'ca_wkv': 4.162579e-03, 'ca_wo': 5.241134e-03, 'ln2_g': 5.721544e-01, 'ln2_b': 2.820585e-01, 'ffn_w_in': 9.157542e-03, 'ffn_conv_w': 5.495926e-03, 'ffn_conv_b': 1.051643e-02, 'ffn_w_out': 1.480428e-02, 'ln3_g': 1.601401e+01, 'ln3_b': 4.951293e-01}


def _to_microbatches(a, axis):
    t = _jnp.moveaxis(a, axis, 0)
    t = t.reshape((N_MICROBATCH, t.shape[0] // N_MICROBATCH) + t.shape[1:])
    return _jnp.moveaxis(t, 1, axis + 1)


def setup_inputs(seed: int = 0) -> dict:
    inp = _fwd_setup_inputs(seed)
    key = _jax.random.fold_in(_jax.random.key(seed), 7919)
    shape, _ = _output_shape()
    out = dict(inp)
    out["loss_target"] = _jax.random.normal(_jax.random.fold_in(key, 0), shape, _jnp.float32)
    for i, name in enumerate(TWIN_WEIGHTS):
        w = inp[name].astype(_jnp.float32)
        if MOMENT_SCALE is None:
            s = _jnp.sqrt(_jnp.mean(_jnp.square(w)) + 1e-30)
        else:
            s = MOMENT_SCALE[name]
        km, kv = _jax.random.split(_jax.random.fold_in(key, i + 1))
        out[name] = w
        out["m_" + name] = s * _jax.random.normal(km, w.shape, _jnp.float32)
        out["v_" + name] = (s * s) * _jax.random.uniform(kv, w.shape, _jnp.float32, 0.5, 1.5)
    if N_MICROBATCH > 1:
        for name, axis in PER_EXAMPLE_BATCH_AXIS.items():
            out[name] = _to_microbatches(out[name], axis)
    return {'x': out['x'], 'mem': out['mem'], 'positions': out['positions'], 'w_in': out['w_in'], 'gla_gate_w2': out['gla_gate_w2'], 'gla_gate_b': out['gla_gate_b'], 'gla_norm_g': out['gla_norm_g'], 'w_out': out['w_out'], 'ln1_g': out['ln1_g'], 'ln1_b': out['ln1_b'], 'ca_wq': out['ca_wq'], 'ca_wkv': out['ca_wkv'], 'ca_wo': out['ca_wo'], 'ln2_g': out['ln2_g'], 'ln2_b': out['ln2_b'], 'ffn_w_in': out['ffn_w_in'], 'ffn_conv_w': out['ffn_conv_w'], 'ffn_conv_b': out['ffn_conv_b'], 'ffn_w_out': out['ffn_w_out'], 'ln3_g': out['ln3_g'], 'ln3_b': out['ln3_b'], 'loss_target': out['loss_target'], 'm_w_in': out['m_w_in'], 'm_gla_gate_w2': out['m_gla_gate_w2'], 'm_gla_gate_b': out['m_gla_gate_b'], 'm_gla_norm_g': out['m_gla_norm_g'], 'm_w_out': out['m_w_out'], 'm_ln1_g': out['m_ln1_g'], 'm_ln1_b': out['m_ln1_b'], 'm_ca_wq': out['m_ca_wq'], 'm_ca_wkv': out['m_ca_wkv'], 'm_ca_wo': out['m_ca_wo'], 'm_ln2_g': out['m_ln2_g'], 'm_ln2_b': out['m_ln2_b'], 'm_ffn_w_in': out['m_ffn_w_in'], 'm_ffn_conv_w': out['m_ffn_conv_w'], 'm_ffn_conv_b': out['m_ffn_conv_b'], 'm_ffn_w_out': out['m_ffn_w_out'], 'm_ln3_g': out['m_ln3_g'], 'm_ln3_b': out['m_ln3_b'], 'v_w_in': out['v_w_in'], 'v_gla_gate_w2': out['v_gla_gate_w2'], 'v_gla_gate_b': out['v_gla_gate_b'], 'v_gla_norm_g': out['v_gla_norm_g'], 'v_w_out': out['v_w_out'], 'v_ln1_g': out['v_ln1_g'], 'v_ln1_b': out['v_ln1_b'], 'v_ca_wq': out['v_ca_wq'], 'v_ca_wkv': out['v_ca_wkv'], 'v_ca_wo': out['v_ca_wo'], 'v_ln2_g': out['v_ln2_g'], 'v_ln2_b': out['v_ln2_b'], 'v_ffn_w_in': out['v_ffn_w_in'], 'v_ffn_conv_w': out['v_ffn_conv_w'], 'v_ffn_conv_b': out['v_ffn_conv_b'], 'v_ffn_w_out': out['v_ffn_w_out'], 'v_ln3_g': out['v_ln3_g'], 'v_ln3_b': out['v_ln3_b']}


def _loss(weights, diff, rest, loss_target):
    with _jax.named_scope("forward"):
        args = {**rest, TWIN_DIFF_INPUT: diff, **{k: w.astype(_WEIGHT_DTYPES[k]) for k, w in weights.items()}}
        y = _forward(args)
    with _jax.named_scope("loss_head"):
        err = _jnp.square(y.astype(_jnp.float32) - loss_target)
        return 0.5 * _jnp.sum(_jnp.mean(err, axis=-1)) if err.ndim else 0.5 * err


def _adamw(w, g, m, v):
    m = ADAM_B1 * m + (1.0 - ADAM_B1) * g
    v = ADAM_B2 * v + (1.0 - ADAM_B2) * _jnp.square(g)
    m_hat = m / (1.0 - ADAM_B1 ** ADAM_STEP)
    v_hat = v / (1.0 - ADAM_B2 ** ADAM_STEP)
    delta = -ADAM_LR * (m_hat / (_jnp.sqrt(v_hat) + ADAM_EPS) + ADAM_WD * w)
    return delta, m, v


def reference(x, mem, positions, w_in, gla_gate_w2, gla_gate_b, gla_norm_g, w_out, ln1_g, ln1_b, ca_wq, ca_wkv, ca_wo, ln2_g, ln2_b, ffn_w_in, ffn_conv_w, ffn_conv_b, ffn_w_out, ln3_g, ln3_b, loss_target, m_w_in, m_gla_gate_w2, m_gla_gate_b, m_gla_norm_g, m_w_out, m_ln1_g, m_ln1_b, m_ca_wq, m_ca_wkv, m_ca_wo, m_ln2_g, m_ln2_b, m_ffn_w_in, m_ffn_conv_w, m_ffn_conv_b, m_ffn_w_out, m_ln3_g, m_ln3_b, v_w_in, v_gla_gate_w2, v_gla_gate_b, v_gla_norm_g, v_w_out, v_ln1_g, v_ln1_b, v_ca_wq, v_ca_wkv, v_ca_wo, v_ln2_g, v_ln2_b, v_ffn_w_in, v_ffn_conv_w, v_ffn_conv_b, v_ffn_w_out, v_ln3_g, v_ln3_b):
    given = dict(x=x, mem=mem, positions=positions, w_in=w_in, gla_gate_w2=gla_gate_w2, gla_gate_b=gla_gate_b, gla_norm_g=gla_norm_g, w_out=w_out, ln1_g=ln1_g, ln1_b=ln1_b, ca_wq=ca_wq, ca_wkv=ca_wkv, ca_wo=ca_wo, ln2_g=ln2_g, ln2_b=ln2_b, ffn_w_in=ffn_w_in, ffn_conv_w=ffn_conv_w, ffn_conv_b=ffn_conv_b, ffn_w_out=ffn_w_out, ln3_g=ln3_g, ln3_b=ln3_b, loss_target=loss_target, m_w_in=m_w_in, m_gla_gate_w2=m_gla_gate_w2, m_gla_gate_b=m_gla_gate_b, m_gla_norm_g=m_gla_norm_g, m_w_out=m_w_out, m_ln1_g=m_ln1_g, m_ln1_b=m_ln1_b, m_ca_wq=m_ca_wq, m_ca_wkv=m_ca_wkv, m_ca_wo=m_ca_wo, m_ln2_g=m_ln2_g, m_ln2_b=m_ln2_b, m_ffn_w_in=m_ffn_w_in, m_ffn_conv_w=m_ffn_conv_w, m_ffn_conv_b=m_ffn_conv_b, m_ffn_w_out=m_ffn_w_out, m_ln3_g=m_ln3_g, m_ln3_b=m_ln3_b, v_w_in=v_w_in, v_gla_gate_w2=v_gla_gate_w2, v_gla_gate_b=v_gla_gate_b, v_gla_norm_g=v_gla_norm_g, v_w_out=v_w_out, v_ln1_g=v_ln1_g, v_ln1_b=v_ln1_b, v_ca_wq=v_ca_wq, v_ca_wkv=v_ca_wkv, v_ca_wo=v_ca_wo, v_ln2_g=v_ln2_g, v_ln2_b=v_ln2_b, v_ffn_w_in=v_ffn_w_in, v_ffn_conv_w=v_ffn_conv_w, v_ffn_conv_b=v_ffn_conv_b, v_ffn_w_out=v_ffn_w_out, v_ln3_g=v_ln3_g, v_ln3_b=v_ln3_b)
    weights = {n: given[n] for n in TWIN_WEIGHTS}
    shared = {n: given[n] for n in SHARED_INPUTS}
    per_example = {n: given[n] for n in ['x', 'mem', 'positions']}
    grad_fn = _jax.value_and_grad(_loss, argnums=(0, 1))

    def one_microbatch(ex, loss_target):
        ex = dict(ex)
        diff = ex.pop(TWIN_DIFF_INPUT)
        return grad_fn(weights, diff, {**shared, **ex}, loss_target)

    if N_MICROBATCH == 1:
        loss, (grad_w, grad_x) = one_microbatch(per_example, given["loss_target"])
    else:
        def body(carry, xs):
            loss_sum, grad_sum = carry
            l_k, (gw_k, gx_k) = one_microbatch(xs[0], xs[1])
            with _jax.named_scope("update"):
                return (loss_sum + l_k, _jax.tree.map(_jnp.add, grad_sum, gw_k)), gx_k

        init = (_jnp.zeros((), _jnp.float32), _jax.tree.map(_jnp.zeros_like, weights))
        (loss, grad_w), grad_x = _jax.lax.scan(body, init, (per_example, given["loss_target"]))
    with _jax.named_scope("update"):
        delta_w, new_m, new_v = {}, {}, {}
        for n in TWIN_WEIGHTS:
            delta_w[n], new_m[n], new_v[n] = _adamw(weights[n], grad_w[n], given["m_" + n], given["v_" + n])
    return (loss, grad_x, *[grad_w[n] for n in TWIN_WEIGHTS], *[delta_w[n] for n in TWIN_WEIGHTS],
            *[new_m[n] for n in TWIN_WEIGHTS], *[new_v[n] for n in TWIN_WEIGHTS])
```

```python
import functools
import math

import jax
import jax.numpy as jnp
from jax import lax
from jax.experimental import pallas as pl
from jax.experimental.pallas import tpu as pltpu

F32 = jnp.float32
BF16 = jnp.bfloat16
MESH = pl.DeviceIdType.MESH

D_MODEL = 2048
LN_EPS = 1e-5
GLA_HEADS = 4
GLA_DV = 256
GLA_DK = 128
GLA_RANK = 16
GLA_TAU = 16.0
GLA_CHUNK = 64
DIL_HD = 128
DIL_HEADS = 8
DIL_BAND = 128
DIL_DILATIONS = (1, 4, 16)
ROPE_THETA = 500000.0
ROPE_DIMS = 32
CA_HEADS = 4
CA_HD = 512
D_FF = 5504
ALPHA = 2.0 ** 0.25
ADAM_LR = 0.001
ADAM_B1 = 0.9
ADAM_B2 = 0.999
ADAM_EPS = 1e-08
ADAM_WD = 0.01
ADAM_STEP = 10

LANES = 128
SUBLANES = 8
VMEM_LIMIT = 56 * 1024 * 1024

GLA_W = 2 * GLA_HEADS * GLA_DK + 2 * GLA_HEADS * GLA_DV
HA_W = GLA_W + LANES
HB_W = 3 * DIL_HEADS * DIL_HD
FFP = 5632
NEG = -1e30


def _params(sem):
    return pltpu.CompilerParams(dimension_semantics=sem, vmem_limit_bytes=VMEM_LIMIT)


def _sigmoid(x):
    return 1.0 / (1.0 + jnp.exp(-x))


def _dot(a, b, dn, precision=None):
    return lax.dot_general(a, b, (dn, ((), ())), preferred_element_type=F32, precision=precision)


NN = ((1,), (0,))
NT = ((1,), (1,))
TN = ((0,), (0,))


def _bf(v):
    return v if v.dtype == BF16 else v.astype(BF16)


def _matmul(a, b, kind, out_dtype, tm, tn, tk, name, resid=None, resid_scale=1.0):
    if kind == "nn":
        (m, k), (k2, n) = a.shape, b.shape
    elif kind == "nt":
        (m, k), (n, k2) = a.shape, b.shape
    else:
        (k, m), (k2, n) = a.shape, b.shape
    assert k == k2 and m % tm == 0 and n % tn == 0 and k % tk == 0, (name, a.shape, b.shape, tm, tn, tk)
    nk = k // tk
    dn = {"nn": NN, "nt": NT, "tn": TN}[kind]
    a_spec = pl.BlockSpec((tk, tm), lambda i, j, kk: (kk, i)) if kind == "tn" else pl.BlockSpec((tm, tk), lambda i, j, kk: (i, kk))
    b_spec = pl.BlockSpec((tn, tk), lambda i, j, kk: (j, kk)) if kind == "nt" else pl.BlockSpec((tk, tn), lambda i, j, kk: (kk, j))
    o_spec = pl.BlockSpec((tm, tn), lambda i, j, kk: (i, j))
    has_resid = resid is not None

    def body(*refs):
        if has_resid:
            a_ref, b_ref, r_ref, o_ref = refs[:4]
        else:
            a_ref, b_ref, o_ref = refs[:3]
            r_ref = None
        part = _dot(_bf(a_ref[...]), _bf(b_ref[...]), dn)

        def finish(acc):
            if has_resid:
                acc = acc + resid_scale * r_ref[...].astype(F32)
            o_ref[...] = acc.astype(out_dtype)

        if nk == 1:
            finish(part)
        else:
            acc_ref = refs[-1]
            kk = pl.program_id(2)

            @pl.when(kk == 0)
            def _():
                acc_ref[...] = part

            @pl.when(kk > 0)
            def _():
                acc_ref[...] += part

            @pl.when(kk == nk - 1)
            def _():
                finish(acc_ref[...])

    in_specs = [a_spec, b_spec] + ([o_spec] if has_resid else [])
    args = (a, b) + ((resid,) if has_resid else ())
    return pl.pallas_call(
        body, name=name, out_shape=jax.ShapeDtypeStruct((m, n), out_dtype),
        grid=(m // tm, n // tn, nk), in_specs=in_specs, out_specs=o_spec,
        scratch_shapes=[pltpu.VMEM((tm, tn), F32)] if nk > 1 else [],
        compiler_params=_params(("parallel", "parallel", "arbitrary")),
    )(*args)


def _ln_core(xres, f):
    p = ALPHA * xres + f
    mu = jnp.mean(p, axis=-1, keepdims=True)
    xc = p - mu
    var = jnp.mean(xc * xc, axis=-1, keepdims=True)
    rstd = lax.rsqrt(var + LN_EPS)
    return xc * rstd, rstd


def _rows8(v):
    r, c = v.shape
    return jnp.sum(v.reshape(r // SUBLANES, SUBLANES, c), axis=0)


def _ln_fwd(xres, f, g, b, name, tr=256):
    t, d = xres.shape
    row = pl.BlockSpec((tr, d), lambda i: (i, 0))
    vec = pl.BlockSpec((1, d), lambda i: (0, 0))

    def body(x_ref, f_ref, g_ref, b_ref, y_ref, yb_ref):
        xhat, _ = _ln_core(x_ref[...], f_ref[...])
        y = xhat * g_ref[...] + b_ref[...]
        y_ref[...] = y
        yb_ref[...] = y.astype(BF16)

    return pl.pallas_call(
        body, name=name, grid=(t // tr,), in_specs=[row, row, vec, vec], out_specs=[row, row],
        out_shape=[jax.ShapeDtypeStruct((t, d), F32), jax.ShapeDtypeStruct((t, d), BF16)],
        compiler_params=_params(("parallel",)),
    )(xres, f, g, b)


def _ln_bwd(xres, f, g, b, dy_or_target, loss_head, name, tr=256):
    t, d = xres.shape
    row = pl.BlockSpec((tr, d), lambda i: (i, 0))
    vec = pl.BlockSpec((1, d), lambda i: (0, 0))
    acc = pl.BlockSpec((SUBLANES, d), lambda i: (0, 0))
    lacc = pl.BlockSpec((SUBLANES, LANES), lambda i: (0, 0))

    def body(x_ref, f_ref, g_ref, b_ref, t_ref, dp_ref, dpb_ref, dg_ref, db_ref, *rest):
        i = pl.program_id(0)
        xhat, rstd = _ln_core(x_ref[...], f_ref[...])
        if loss_head:
            err = xhat * g_ref[...] + b_ref[...] - t_ref[...]
            dy = err * (1.0 / d)
            sq = err * err
            lanes = sq[:, :LANES]
            for kk in range(1, d // LANES):
                lanes = lanes + sq[:, kk * LANES:(kk + 1) * LANES]
            lpart = _rows8(lanes) * (0.5 / d)
        else:
            dy = t_ref[...]
        dxh = dy * g_ref[...]
        m1 = jnp.mean(dxh, axis=-1, keepdims=True)
        m2 = jnp.mean(dxh * xhat, axis=-1, keepdims=True)
        dp = rstd * (dxh - m1 - xhat * m2)
        dp_ref[...] = dp
        dpb_ref[...] = dp.astype(BF16)
        dgp = _rows8(dy * xhat)
        dbp = _rows8(dy)

        @pl.when(i == 0)
        def _():
            dg_ref[...] = dgp
            db_ref[...] = dbp
            if loss_head:
                rest[0][...] = lpart

        @pl.when(i > 0)
        def _():
            dg_ref[...] += dgp
            db_ref[...] += dbp
            if loss_head:
                rest[0][...] += lpart

    out_shape = [jax.ShapeDtypeStruct((t, d), F32), jax.ShapeDtypeStruct((t, d), BF16),
                 jax.ShapeDtypeStruct((SUBLANES, d), F32), jax.ShapeDtypeStruct((SUBLANES, d), F32)]
    out_specs = [row, row, acc, acc]
    if loss_head:
        out_shape.append(jax.ShapeDtypeStruct((SUBLANES, LANES), F32))
        out_specs.append(lacc)
    return pl.pallas_call(
        body, name=name, grid=(t // tr,), in_specs=[row, row, vec, vec, row], out_specs=out_specs,
        out_shape=out_shape, compiler_params=_params(("arbitrary",)),
    )(xres, f, g, b, dy_or_target)


def _gla_gates(glr, w2, gb):
    z = _dot(_bf(glr), w2, NN) + gb
    lg = (jnp.minimum(z, 0.0) - jnp.log(1.0 + jnp.exp(-jnp.abs(z)))) * (1.0 / GLA_TAU)
    c = z.shape[0]
    ri = lax.broadcasted_iota(jnp.int32, (c, c), 0)
    ci = lax.broadcasted_iota(jnp.int32, (c, c), 1)
    tri = (ci <= ri).astype(F32)
    bcum = _dot(tri, lg, NN, precision=lax.Precision.HIGHEST)
    blast = jnp.sum(lg, axis=0, keepdims=True)
    return z, bcum, blast, tri


def _gla_specs(t):
    c = GLA_CHUNK
    return c, t // c


def _gla_fwd(h_a, w2p, gate_b, norm_g):
    t = h_a.shape[0]
    c, n = _gla_specs(t)
    hk, hv = GLA_HEADS * GLA_DK, GLA_HEADS * GLA_DV
    scale = GLA_DK ** -0.5

    def body(q_ref, k_ref, v_ref, r_ref, glr_ref, w2_ref, gb_ref, ng_ref, og_ref, oraw_ref, sb_ref, st_ref):
        i = pl.program_id(0)

        @pl.when(i == 0)
        def _():
            st_ref[...] = jnp.zeros_like(st_ref)

        _, bcum, blast, _ = _gla_gates(glr_ref[...], w2_ref[...], gb_ref[...])
        ri = lax.broadcasted_iota(jnp.int32, (c, c), 0)
        ci = lax.broadcasted_iota(jnp.int32, (c, c), 1)
        causal = ci <= ri
        for h in range(GLA_HEADS):
            ks = slice(h * GLA_DK, (h + 1) * GLA_DK)
            vs = slice(h * GLA_DV, (h + 1) * GLA_DV)
            b_h, bl_h = bcum[:, ks], blast[:, ks]
            q_h, k_h = q_ref[:, ks], k_ref[:, ks]
            v_h = _bf(v_ref[:, vs])
            qi = _bf(q_h * scale * jnp.exp(b_h))
            ki = _bf(k_h * jnp.exp(-b_h))
            ke = _bf(k_h * jnp.exp(bl_h - b_h))
            st = st_ref[h]
            sb_ref[0, h] = st
            a = jnp.where(causal, _dot(qi, ki, NT), 0.0)
            o = _dot(_bf(a), v_h, NN) + _dot(qi, _bf(st), NT)
            st_ref[h] = st * jnp.exp(bl_h) + _dot(v_h, ke, TN)
            oraw_ref[:, vs] = o
            mu = jnp.mean(o, axis=-1, keepdims=True)
            oc = o - mu
            var = jnp.mean(oc * oc, axis=-1, keepdims=True)
            xh = oc * lax.rsqrt(var + LN_EPS)
            r_h = r_ref[:, vs]
            og_ref[:, vs] = (xh * ng_ref[:, vs] * (r_h * _sigmoid(r_h))).astype(BF16)

    return pl.pallas_call(
        body, name="gla_fwd", grid=(n,),
        in_specs=[pl.BlockSpec((c, hk), lambda i: (i, 0)), pl.BlockSpec((c, hk), lambda i: (i, 1)),
                  pl.BlockSpec((c, hv), lambda i: (i, 1)), pl.BlockSpec((c, hv), lambda i: (i, 2)),
                  pl.BlockSpec((c, LANES), lambda i: (i, GLA_W // LANES)),
                  pl.BlockSpec((LANES, hk), lambda i: (0, 0)), pl.BlockSpec((1, hk), lambda i: (0, 0)),
                  pl.BlockSpec((1, hv), lambda i: (0, 0))],
        out_specs=[pl.BlockSpec((c, hv), lambda i: (i, 0)), pl.BlockSpec((c, hv), lambda i: (i, 0)),
                   pl.BlockSpec((1, GLA_HEADS, GLA_DV, GLA_DK), lambda i: (i, 0, 0, 0))],
        out_shape=[jax.ShapeDtypeStruct((t, hv), BF16), jax.ShapeDtypeStruct((t, hv), F32),
                   jax.ShapeDtypeStruct((n, GLA_HEADS, GLA_DV, GLA_DK), F32)],
        scratch_shapes=[pltpu.VMEM((GLA_HEADS, GLA_DV, GLA_DK), F32)],
        compiler_params=_params(("arbitrary",)),
    )(h_a, h_a, h_a, h_a, h_a, w2p, gate_b, norm_g)


def _gla_bwd(h_a, w2p, gate_b, norm_g, o_raw, s_before, dmix):
    t = h_a.shape[0]
    c, n = _gla_specs(t)
    hk, hv = GLA_HEADS * GLA_DK, GLA_HEADS * GLA_DV
    scale = GLA_DK ** -0.5
    rev = lambda i: n - 1 - i

    def body(q_ref, k_ref, v_ref, r_ref, glr_ref, w2_ref, gb_ref, ng_ref, oraw_ref, sb_ref, do_ref,
             dh_ref, dw2_ref, dgb_ref, dng_ref, dst_ref):
        i = pl.program_id(0)

        @pl.when(i == 0)
        def _():
            dst_ref[...] = jnp.zeros_like(dst_ref)

        glr = glr_ref[...]
        z, bcum, blast, tri = _gla_gates(glr, w2_ref[...], gb_ref[...])
        ri = lax.broadcasted_iota(jnp.int32, (c, c), 0)
        ci = lax.broadcasted_iota(jnp.int32, (c, c), 1)
        causal = ci <= ri
        dlg_parts = []
        dng_parts = []
        for h in range(GLA_HEADS):
            ks = slice(h * GLA_DK, (h + 1) * GLA_DK)
            vs = slice(h * GLA_DV, (h + 1) * GLA_DV)
            o = oraw_ref[:, vs]
            mu = jnp.mean(o, axis=-1, keepdims=True)
            oc = o - mu
            var = jnp.mean(oc * oc, axis=-1, keepdims=True)
            rstd = lax.rsqrt(var + LN_EPS)
            xh = oc * rstd
            r_h = r_ref[:, vs]
            sg = _sigmoid(r_h)
            silu = r_h * sg
            dout = do_ref[:, vs]
            ng = ng_ref[:, vs]
            dng_parts.append(_rows8(dout * xh * silu))
            dr = dout * xh * ng * (sg * (1.0 + r_h * (1.0 - sg)))
            dxh = dout * ng * silu
            m1 = jnp.mean(dxh, axis=-1, keepdims=True)
            m2 = jnp.mean(dxh * xh, axis=-1, keepdims=True)
            do_raw = _bf(rstd * (dxh - m1 - xh * m2))
            b_h, bl_h = bcum[:, ks], blast[:, ks]
            q_h, k_h = q_ref[:, ks], k_ref[:, ks]
            v_h = _bf(v_ref[:, vs])
            eb, enb, eend = jnp.exp(b_h), jnp.exp(-b_h), jnp.exp(bl_h - b_h)
            decay = jnp.exp(bl_h)
            qi_f, ki_f, ke_f = q_h * scale * eb, k_h * enb, k_h * eend
            qi, ki, ke = _bf(qi_f), _bf(ki_f), _bf(ke_f)
            st = sb_ref[0, h]
            dst = dst_ref[h]
            dst_b = _bf(dst)
            a = _bf(jnp.where(causal, _dot(qi, ki, NT), 0.0))
            da = _bf(jnp.where(causal, _dot(do_raw, v_h, NT), 0.0))
            dv = _dot(a, do_raw, TN) + _dot(ke, dst_b, NT)
            dqi = _dot(da, ki, NN) + _dot(do_raw, _bf(st), NN)
            dki = _dot(da, qi, TN)
            dke = _dot(v_h, dst_b, NN)
            dst_ref[h] = _dot(do_raw, qi, TN) + dst * decay
            dbl = decay * jnp.sum(st * dst, axis=0, keepdims=True) + jnp.sum(dke * ke_f, axis=0, keepdims=True)
            dbc = dqi * qi_f - dki * ki_f - dke * ke_f
            dlg_parts.append(_dot(tri, dbc, TN, precision=lax.Precision.HIGHEST) + dbl)
            dh_ref[:, ks] = (dqi * eb * scale).astype(BF16)
            dh_ref[:, hk + h * GLA_DK: hk + (h + 1) * GLA_DK] = (dki * enb + dke * eend).astype(BF16)
            dh_ref[:, 2 * hk + h * GLA_DV: 2 * hk + (h + 1) * GLA_DV] = dv.astype(BF16)
            dh_ref[:, 2 * hk + hv + h * GLA_DV: 2 * hk + hv + (h + 1) * GLA_DV] = dr.astype(BF16)
        dlg = jnp.concatenate(dlg_parts, axis=1)
        dz = dlg * (1.0 / GLA_TAU) * _sigmoid(-z)
        dz_b = _bf(dz)
        dh_ref[:, GLA_W:] = _dot(dz_b, w2_ref[...], NT).astype(BF16)
        dw2p = _dot(_bf(glr), dz_b, TN)
        dgbp = _rows8(dz)
        dngp = jnp.concatenate(dng_parts, axis=1)

        @pl.when(i == 0)
        def _():
            dw2_ref[...] = dw2p
            dgb_ref[...] = dgbp
            dng_ref[...] = dngp

        @pl.when(i > 0)
        def _():
            dw2_ref[...] += dw2p
            dgb_ref[...] += dgbp
            dng_ref[...] += dngp

    return pl.pallas_call(
        body, name="gla_bwd", grid=(n,),
        in_specs=[pl.BlockSpec((c, hk), lambda i: (rev(i), 0)), pl.BlockSpec((c, hk), lambda i: (rev(i), 1)),
                  pl.BlockSpec((c, hv), lambda i: (rev(i), 1)), pl.BlockSpec((c, hv), lambda i: (rev(i), 2)),
                  pl.BlockSpec((c, LANES), lambda i: (rev(i), GLA_W // LANES)),
                  pl.BlockSpec((LANES, hk), lambda i: (0, 0)), pl.BlockSpec((1, hk), lambda i: (0, 0)),
                  pl.BlockSpec((1, hv), lambda i: (0, 0)),
                  pl.BlockSpec((c, hv), lambda i: (rev(i), 0)),
                  pl.BlockSpec((1, GLA_HEADS, GLA_DV, GLA_DK), lambda i: (rev(i), 0, 0, 0)),
                  pl.BlockSpec((c, hv), lambda i: (rev(i), 0))],
        out_specs=[pl.BlockSpec((c, HA_W), lambda i: (rev(i), 0)),
                   pl.BlockSpec((LANES, hk), lambda i: (0, 0)),
                   pl.BlockSpec((SUBLANES, hk), lambda i: (0, 0)),
                   pl.BlockSpec((SUBLANES, hv), lambda i: (0, 0))],
        out_shape=[jax.ShapeDtypeStruct((t, HA_W), BF16), jax.ShapeDtypeStruct((LANES, hk), F32),
                   jax.ShapeDtypeStruct((SUBLANES, hk), F32), jax.ShapeDtypeStruct((SUBLANES, hv), F32)],
        scratch_shapes=[pltpu.VMEM((GLA_HEADS, GLA_DV, GLA_DK), F32)],
        compiler_params=_params(("arbitrary",)),
    )(h_a, h_a, h_a, h_a, h_a, w2p, gate_b, norm_g, o_raw, s_before, dmix)


def _rope_tables(positions):
    half = ROPE_DIMS // 2
    inv_freq = ROPE_THETA ** (-jnp.arange(0, ROPE_DIMS, 2, dtype=F32) / ROPE_DIMS)
    ang = positions.astype(F32).reshape(-1, 1) * inv_freq
    cos, sin = jnp.cos(ang), jnp.sin(ang)
    t = cos.shape[0]
    one = jnp.ones((t, DIL_HD - ROPE_DIMS), F32)
    zero = jnp.zeros((t, DIL_HD - ROPE_DIMS), F32)
    zh = jnp.zeros((t, half), F32)
    return (jnp.concatenate([cos, cos, one], axis=1), jnp.concatenate([-sin, zh, zero], axis=1),
            jnp.concatenate([zh, sin, zero], axis=1))


def _rope_apply(x, c, s1, s2):
    half = ROPE_DIMS // 2
    return x * c + pltpu.roll(x, DIL_HD - half, 1) * s1 + pltpu.roll(x, half, 1) * s2


def _rope_apply_t(dy, c, s1, s2):
    half = ROPE_DIMS // 2
    return dy * c + pltpu.roll(dy * s1, half, 1) + pltpu.roll(dy * s2, DIL_HD - half, 1)


def _rope_fwd(h_b, tabs, tr=256):
    t = h_b.shape[0]
    w = DIL_HEADS * DIL_HD
    scale = DIL_HD ** -0.5
    tab = pl.BlockSpec((tr, DIL_HD), lambda i: (i, 0))
    outb = pl.BlockSpec((tr, w), lambda i: (i, 0))

    def body(q_ref, k_ref, v_ref, c_ref, s1_ref, s2_ref, qo_ref, ko_ref, vo_ref):
        c, s1, s2 = c_ref[...], s1_ref[...], s2_ref[...]
        for h in range(DIL_HEADS):
            hs = slice(h * DIL_HD, (h + 1) * DIL_HD)
            qo_ref[:, hs] = _rope_apply(q_ref[:, hs] * scale, c, s1, s2).astype(BF16)
            ko_ref[:, hs] = _rope_apply(k_ref[:, hs], c, s1, s2).astype(BF16)
        vo_ref[...] = v_ref[...].astype(BF16)

    return pl.pallas_call(
        body, name="rope_fwd", grid=(t // tr,),
        in_specs=[pl.BlockSpec((tr, w), lambda i: (i, 0)), pl.BlockSpec((tr, w), lambda i: (i, 1)),
                  pl.BlockSpec((tr, w), lambda i: (i, 2)), tab, tab, tab],
        out_specs=[outb, outb, outb],
        out_shape=[jax.ShapeDtypeStruct((t, w), BF16)] * 3,
        compiler_params=_params(("parallel",)),
    )(h_b, h_b, h_b, *tabs)


def _dil_dh(dqs, dks, dvs, tabs, tr=128):
    t, w = dqs[0].shape
    scale = DIL_HD ** -0.5
    tab = pl.BlockSpec((tr, DIL_HD), lambda i: (i, 0))
    inb = pl.BlockSpec((tr, w), lambda i: (i, 0))

    def body(*refs):
        dq = refs[0][...] + refs[1][...] + refs[2][...]
        dk = refs[3][...] + refs[4][...] + refs[5][...]
        dv = refs[6][...] + refs[7][...] + refs[8][...]
        c, s1, s2 = refs[9][...], refs[10][...], refs[11][...]
        o_ref = refs[12]
        for h in range(DIL_HEADS):
            hs = slice(h * DIL_HD, (h + 1) * DIL_HD)
            o_ref[:, h * DIL_HD:(h + 1) * DIL_HD] = (_rope_apply_t(dq[:, hs], c, s1, s2) * scale).astype(BF16)
            o_ref[:, w + h * DIL_HD: w + (h + 1) * DIL_HD] = _rope_apply_t(dk[:, hs], c, s1, s2).astype(BF16)
        o_ref[:, 2 * w:] = dv.astype(BF16)

    return pl.pallas_call(
        body, name="dil_dh", grid=(t // tr,), in_specs=[inb] * 9 + [tab] * 3,
        out_specs=pl.BlockSpec((tr, 3 * w), lambda i: (i, 0)),
        out_shape=jax.ShapeDtypeStruct((t, 3 * w), BF16), compiler_params=_params(("parallel",)),
    )(*dqs, *dks, *dvs, *tabs)


BANDS = 8


def _to_branch(a, d):
    t, w = a.shape
    return a.reshape(t // d, d, w // DIL_HD, DIL_HD).transpose(1, 2, 0, 3).reshape(-1, DIL_HD)


def _from_branch(a, d, t):
    hds = a.shape[0] // t
    return a.reshape(d, hds, t // d, DIL_HD).transpose(2, 0, 1, 3).reshape(t, hds * DIL_HD)


def _band_masks(not_first):
    r = lax.broadcasted_iota(jnp.int32, (DIL_BAND, 2 * DIL_BAND), 0)
    c = lax.broadcasted_iota(jnp.int32, (DIL_BAND, 2 * DIL_BAND), 1)
    nf = jnp.full((DIL_BAND, 2 * DIL_BAND), not_first, jnp.int32)
    look_back = jnp.logical_and(jnp.logical_and(c < DIL_BAND, c >= r), nf > 0)
    own_band = jnp.logical_and(c >= DIL_BAND, (c - DIL_BAND) <= r)
    return jnp.logical_or(look_back, own_band)


def _dil_fwd(q, k, v, nb, name):
    rows = q.shape[0]
    blk = BANDS * DIL_BAND
    steps = rows // blk
    main = pl.BlockSpec((blk, DIL_HD), lambda i: (i, 0))
    prev = pl.BlockSpec((DIL_BAND, DIL_HD), lambda i: (jnp.maximum(i * BANDS - 1, 0), 0))

    def body(q_ref, k_ref, v_ref, kp_ref, vp_ref, o_ref, l_ref):
        i = pl.program_id(0)
        for j in range(BANDS):
            lo, hi = j * DIL_BAND, (j + 1) * DIL_BAND
            if j == 0:
                kcat = jnp.concatenate([kp_ref[...], k_ref[lo:hi, :]], axis=0)
                vcat = jnp.concatenate([vp_ref[...], v_ref[lo:hi, :]], axis=0)
            else:
                kcat = k_ref[lo - DIL_BAND:hi, :]
                vcat = v_ref[lo - DIL_BAND:hi, :]
            not_first = (((i * BANDS + j) % nb) != 0).astype(jnp.int32)
            s = jnp.where(_band_masks(not_first), _dot(q_ref[lo:hi, :], kcat, NT), NEG)
            m = jnp.max(s, axis=-1, keepdims=True)
            p = jnp.exp(s - m)
            den = jnp.sum(p, axis=-1, keepdims=True)
            o_ref[lo:hi, :] = _dot(_bf(p), vcat, NN) / den
            l_ref[lo:hi, :] = jnp.broadcast_to(m + jnp.log(den), (DIL_BAND, DIL_HD))

    return pl.pallas_call(
        body, name=name, grid=(steps,), in_specs=[main, main, main, prev, prev], out_specs=[main, main],
        out_shape=[jax.ShapeDtypeStruct((rows, DIL_HD), F32)] * 2, compiler_params=_params(("parallel",)),
    )(q, k, v, k, v)


def _dil_bwd(q, k, v, do, lse, dd, nb, name):
    rows = q.shape[0]
    blk = BANDS * DIL_BAND
    steps = rows // blk
    last_band = rows // DIL_BAND - 1
    main = pl.BlockSpec((blk, DIL_HD), lambda i: (i, 0))
    prev = pl.BlockSpec((DIL_BAND, DIL_HD), lambda i: (jnp.maximum(i * BANDS - 1, 0), 0))
    nxt = pl.BlockSpec((DIL_BAND, DIL_HD), lambda i: (jnp.minimum(i * BANDS + BANDS, last_band), 0))

    def body(q_ref, k_ref, v_ref, do_ref, l_ref, dd_ref, kp_ref, vp_ref, qn_ref, don_ref, ln_ref, ddn_ref,
             dq_ref, dk_ref, dv_ref, ak_ref, av_ref):
        i = pl.program_id(0)
        ak_ref[...] = jnp.zeros_like(ak_ref)
        av_ref[...] = jnp.zeros_like(av_ref)
        for j in range(BANDS + 1):
            lo, hi = j * DIL_BAND, (j + 1) * DIL_BAND
            if j == 0:
                kcat = jnp.concatenate([kp_ref[...], k_ref[lo:hi, :]], axis=0)
                vcat = jnp.concatenate([vp_ref[...], v_ref[lo:hi, :]], axis=0)
            elif j < BANDS:
                kcat = k_ref[lo - DIL_BAND:hi, :]
                vcat = v_ref[lo - DIL_BAND:hi, :]
            else:
                kcat = jnp.concatenate([k_ref[lo - DIL_BAND:lo, :], k_ref[lo - DIL_BAND:lo, :]], axis=0)
                vcat = jnp.concatenate([v_ref[lo - DIL_BAND:lo, :], v_ref[lo - DIL_BAND:lo, :]], axis=0)
            if j < BANDS:
                qj, doj, lj, ddj = q_ref[lo:hi, :], do_ref[lo:hi, :], l_ref[lo:hi, :], dd_ref[lo:hi, :]
            else:
                qj, doj, lj, ddj = qn_ref[...], don_ref[...], ln_ref[...], ddn_ref[...]
            not_first = (((i * BANDS + j) % nb) != 0).astype(jnp.int32)
            mask = _band_masks(not_first)
            if j == BANDS:
                cidx = lax.broadcasted_iota(jnp.int32, mask.shape, 1)
                mask = jnp.logical_and(mask, cidx < DIL_BAND)
            s = jnp.where(mask, _dot(qj, kcat, NT), NEG)
            p = jnp.exp(s - jnp.concatenate([lj, lj], axis=1))
            dp = _dot(doj, vcat, NT)
            ds = _bf(p * (dp - jnp.concatenate([ddj, ddj], axis=1)))
            if j < BANDS:
                dq_ref[lo:hi, :] = _dot(ds, kcat, NN)
            ak_ref[lo:hi + DIL_BAND, :] += _dot(ds, qj, TN)
            av_ref[lo:hi + DIL_BAND, :] += _dot(_bf(p), doj, TN)
        dk_ref[...] = ak_ref[DIL_BAND:DIL_BAND + blk, :]
        dv_ref[...] = av_ref[DIL_BAND:DIL_BAND + blk, :]

    return pl.pallas_call(
        body, name=name, grid=(steps,),
        in_specs=[main] * 6 + [prev, prev] + [nxt] * 4, out_specs=[main] * 3,
        out_shape=[jax.ShapeDtypeStruct((rows, DIL_HD), F32)] * 3,
        scratch_shapes=[pltpu.VMEM((blk + 2 * DIL_BAND, DIL_HD), F32)] * 2,
        compiler_params=_params(("parallel",)),
    )(q, k, v, do, lse, dd, k, v, q, do, lse, dd)


def _dil_merge(os_, ls_, tr=256):
    t, w = os_[0].shape
    blk = pl.BlockSpec((tr, w), lambda i: (i, 0))

    def body(o1, o2, o3, l1, l2, l3, ob_ref, of_ref, lt_ref):
        a, b, c = l1[...], l2[...], l3[...]
        m = jnp.maximum(jnp.maximum(a, b), c)
        ea, eb, ec = jnp.exp(a - m), jnp.exp(b - m), jnp.exp(c - m)
        den = ea + eb + ec
        o = (ea * o1[...] + eb * o2[...] + ec * o3[...]) / den
        ob_ref[...] = o.astype(BF16)
        of_ref[...] = o
        lt_ref[...] = m + jnp.log(den)

    return pl.pallas_call(
        body, name="dil_merge", grid=(t // tr,), in_specs=[blk] * 6, out_specs=[blk] * 3,
        out_shape=[jax.ShapeDtypeStruct((t, w), BF16), jax.ShapeDtypeStruct((t, w), F32),
                   jax.ShapeDtypeStruct((t, w), F32)],
        compiler_params=_params(("parallel",)),
    )(*os_, *ls_)


def _dil_bwd_prep(dmix, o_d, tr=256):
    t, w = o_d.shape
    blk = pl.BlockSpec((tr, w), lambda i: (i, 0))

    def body(do_ref, o_ref, dob_ref, dd_ref):
        do = do_ref[...]
        prod = do * o_ref[...]
        dob_ref[...] = do.astype(BF16)
        for h in range(DIL_HEADS):
            hs = slice(h * DIL_HD, (h + 1) * DIL_HD)
            dd_ref[:, hs] = jnp.broadcast_to(jnp.sum(prod[:, hs], axis=-1, keepdims=True), (tr, DIL_HD))

    return pl.pallas_call(
        body, name="dil_bwd_prep", grid=(t // tr,),
        in_specs=[pl.BlockSpec((tr, w), lambda i: (i, 1)), blk], out_specs=[blk, blk],
        out_shape=[jax.ShapeDtypeStruct((t, w), BF16), jax.ShapeDtypeStruct((t, w), F32)],
        compiler_params=_params(("parallel",)),
    )(dmix, o_d)


def _ca_fwd(q, memkv, tq=512):
    t, d = q.shape
    m = memkv.shape[0]
    scale = CA_HD ** -0.5

    def body(q_ref, k_ref, v_ref, o_ref):
        for h in range(CA_HEADS):
            hs = slice(h * CA_HD, (h + 1) * CA_HD)
            s = _dot(q_ref[:, hs], k_ref[:, hs], NT) * scale
            p = jnp.exp(s - jnp.max(s, axis=-1, keepdims=True))
            p = p / jnp.sum(p, axis=-1, keepdims=True)
            o_ref[:, hs] = _dot(_bf(p), v_ref[:, hs], NN).astype(BF16)

    return pl.pallas_call(
        body, name="ca_fwd", grid=(t // tq,),
        in_specs=[pl.BlockSpec((tq, d), lambda i: (i, 0)), pl.BlockSpec((m, d), lambda i: (0, 0)),
                  pl.BlockSpec((m, d), lambda i: (0, 1))],
        out_specs=pl.BlockSpec((tq, d), lambda i: (i, 0)),
        out_shape=jax.ShapeDtypeStruct((t, d), BF16), compiler_params=_params(("parallel",)),
    )(q, memkv, memkv)


def _ca_bwd(q, memkv, do, tq=512):
    t, d = q.shape
    m = memkv.shape[0]
    scale = CA_HD ** -0.5

    def body(q_ref, k_ref, v_ref, do_ref, dq_ref, dkv_ref):
        i = pl.program_id(0)

        @pl.when(i == 0)
        def _():
            dkv_ref[...] = jnp.zeros_like(dkv_ref)

        for h in range(CA_HEADS):
            hs = slice(h * CA_HD, (h + 1) * CA_HD)
            q_h, k_h, v_h, do_h = q_ref[:, hs], k_ref[:, hs], v_ref[:, hs], do_ref[:, hs]
            s = _dot(q_h, k_h, NT) * scale
            p = jnp.exp(s - jnp.max(s, axis=-1, keepdims=True))
            p = p / jnp.sum(p, axis=-1, keepdims=True)
            dp = _dot(do_h, v_h, NT)
            ds = _bf(p * (dp - jnp.sum(p * dp, axis=-1, keepdims=True)) * scale)
            dq_ref[:, hs] = _dot(ds, k_h, NN).astype(BF16)
            dkv_ref[:, hs] += _dot(ds, q_h, TN)
            dkv_ref[:, d + h * CA_HD: d + (h + 1) * CA_HD] += _dot(_bf(p), do_h, TN)

    return pl.pallas_call(
        body, name="ca_bwd", grid=(t // tq,),
        in_specs=[pl.BlockSpec((tq, d), lambda i: (i, 0)), pl.BlockSpec((m, d), lambda i: (0, 0)),
                  pl.BlockSpec((m, d), lambda i: (0, 1)), pl.BlockSpec((tq, d), lambda i: (i, 0))],
        out_specs=[pl.BlockSpec((tq, d), lambda i: (i, 0)), pl.BlockSpec((m, 2 * d), lambda i: (0, 0))],
        out_shape=[jax.ShapeDtypeStruct((t, d), BF16), jax.ShapeDtypeStruct((m, 2 * d), F32)],
        compiler_params=_params(("arbitrary",)),
    )(q, memkv, memkv, do)


STRIP = 256


def _shift_down(u, n, row):
    return jnp.where(row >= n, pltpu.roll(u, n, 0), 0.0)


def _shift_up(u, n, row):
    t = u.shape[0]
    return jnp.where(row < t - n, pltpu.roll(u, t - n, 0), 0.0)


def _conv(u, cw_ref, row):
    return ((cw_ref[3:4, :] + cw_ref[0:1, :] * _shift_down(u, 2, row)) + cw_ref[1:2, :] * _shift_down(u, 1, row)) \
        + cw_ref[2:3, :] * u


def _swiglu_fwd(ug0, uu0, cwg, cwu):
    t, w = ug0.shape
    col = pl.BlockSpec((t, STRIP), lambda j: (0, j))
    cws = pl.BlockSpec((SUBLANES, STRIP), lambda j: (0, j))

    def body(g_ref, u_ref, cg_ref, cu_ref, a_ref):
        row = lax.broadcasted_iota(jnp.int32, (t, STRIP), 0)
        gate = _conv(g_ref[...], cg_ref, row)
        up = _conv(u_ref[...], cu_ref, row)
        a_ref[...] = (gate * _sigmoid(gate) * up).astype(BF16)

    return pl.pallas_call(
        body, name="swiglu_fwd", grid=(w // STRIP,), in_specs=[col, col, cws, cws], out_specs=col,
        out_shape=jax.ShapeDtypeStruct((t, w), BF16), compiler_params=_params(("parallel",)),
    )(ug0, uu0, cwg, cwu)


def _swiglu_bwd(ug0, uu0, cwg, cwu, da):
    t, w = ug0.shape
    col = pl.BlockSpec((t, STRIP), lambda j: (0, j))
    cws = pl.BlockSpec((SUBLANES, STRIP), lambda j: (0, j))

    def conv_bwd(du, u0, cw_ref, row, du0_ref, dcw_ref):
        du0 = (cw_ref[2:3, :] * du + cw_ref[1:2, :] * _shift_up(du, 1, row)) + cw_ref[0:1, :] * _shift_up(du, 2, row)
        du0_ref[...] = du0.astype(BF16)
        dcw_ref[0:1, :] = jnp.sum(du * _shift_down(u0, 2, row), axis=0, keepdims=True)
        dcw_ref[1:2, :] = jnp.sum(du * _shift_down(u0, 1, row), axis=0, keepdims=True)
        dcw_ref[2:3, :] = jnp.sum(du * u0, axis=0, keepdims=True)
        dcw_ref[3:4, :] = jnp.sum(du, axis=0, keepdims=True)
        dcw_ref[4:8, :] = jnp.zeros((4, STRIP), F32)

    def body(g_ref, u_ref, cg_ref, cu_ref, da_ref, dg0_ref, du0_ref, dcg_ref, dcu_ref):
        row = lax.broadcasted_iota(jnp.int32, (t, STRIP), 0)
        g0, u0 = g_ref[...], u_ref[...]
        gate = _conv(g0, cg_ref, row)
        up = _conv(u0, cu_ref, row)
        sg = _sigmoid(gate)
        da = da_ref[...]
        dgate = da * up * (sg * (1.0 + gate * (1.0 - sg)))
        dup = da * (gate * sg)
        conv_bwd(dgate, g0, cg_ref, row, dg0_ref, dcg_ref)
        conv_bwd(dup, u0, cu_ref, row, du0_ref, dcu_ref)

    return pl.pallas_call(
        body, name="swiglu_bwd", grid=(w // STRIP,), in_specs=[col, col, cws, cws, col],
        out_specs=[col, col, cws, cws],
        out_shape=[jax.ShapeDtypeStruct((t, w), BF16), jax.ShapeDtypeStruct((t, w), BF16),
                   jax.ShapeDtypeStruct((SUBLANES, w), F32), jax.ShapeDtypeStruct((SUBLANES, w), F32)],
        compiler_params=_params(("parallel",)),
    )(ug0, uu0, cwg, cwu, da)


def _tile2d(r, c, limit=1 << 20):
    tr, tc = r, c
    while tr * tc * 4 > limit:
        if tr % (2 * SUBLANES) == 0:
            tr //= 2
        elif tc % (2 * LANES) == 0:
            tc //= 2
        else:
            break
    return tr, tc


def _adamw_math(w, m, v, g):
    c1 = 1.0 - ADAM_B1 ** ADAM_STEP
    c2 = 1.0 - ADAM_B2 ** ADAM_STEP
    mm = ADAM_B1 * m + (1.0 - ADAM_B1) * g
    vv = ADAM_B2 * v + (1.0 - ADAM_B2) * (g * g)
    delta = -ADAM_LR * ((mm / c1) / (jnp.sqrt(vv / c2) + ADAM_EPS) + ADAM_WD * w)
    return delta, mm, vv


def _adamw(w, m, v, g, name):
    r, c = w.shape
    blk = pl.BlockSpec((r, c), lambda i: (0, 0))

    def body(w_ref, m_ref, v_ref, gi_ref, g_ref, d_ref, nm_ref, nv_ref):
        g = gi_ref[...]
        d_ref[...], nm_ref[...], nv_ref[...] = _adamw_math(w_ref[...], m_ref[...], v_ref[...], g)
        g_ref[...] = g

    return pl.pallas_call(body, name=name, grid=(1,), in_specs=[blk] * 4, out_specs=[blk] * 4,
                          out_shape=[jax.ShapeDtypeStruct((r, c), F32)] * 4,
                          compiler_params=_params(("arbitrary",)))(w, m, v, g)


def _pair_add(gs, ra, core, name):
    _, _, r, c = gs.shape
    tr, tc = _tile2d(r, c)
    blk = pl.BlockSpec((None, tr, tc), lambda k, i, j, s: (k, i, j))

    def body(s_ref, g_ref, r_ref, o_ref, ob_ref):
        p = g_ref[...] + r_ref[...]
        o_ref[...] = p
        ob_ref[...] = p.astype(BF16)

    gspec = pltpu.PrefetchScalarGridSpec(
        num_scalar_prefetch=1, grid=(4, r // tr, c // tc),
        in_specs=[pl.BlockSpec((None, None, tr, tc), lambda k, i, j, s: (k, s[0], i, j)), blk], out_specs=[blk, blk])
    return pl.pallas_call(body, name=name, grid_spec=gspec,
                          out_shape=[jax.ShapeDtypeStruct((4, r, c), F32), jax.ShapeDtypeStruct((4, r, c), BF16)],
                          compiler_params=_params(("parallel", "parallel", "parallel")))(core, gs, ra)


def _small_reduce(gathered):
    nd, r, n = gathered.shape
    tn = 2048 if n % 2048 == 0 else n
    def body(g_ref, s_ref, t_ref):
        s = g_ref[0]
        for dv in range(1, nd):
            s = s + g_ref[dv]
        s_ref[...] = s
        t_ref[...] = jnp.broadcast_to(jnp.sum(s, axis=0, keepdims=True), (r, tn))

    return pl.pallas_call(
        body, name="small_reduce", grid=(n // tn,),
        in_specs=[pl.BlockSpec((nd, r, tn), lambda j: (0, 0, j))],
        out_specs=[pl.BlockSpec((r, tn), lambda j: (0, j))] * 2,
        out_shape=[jax.ShapeDtypeStruct((r, n), F32)] * 2, compiler_params=_params(("parallel",)),
    )(gathered)


HBM = pl.BlockSpec(memory_space=pltpu.HBM)


def _all_gather(arrs, name):
    n = len(arrs)

    def body(*refs):
        ins, outs = refs[:n], refs[n:2 * n]
        send, recv, lsem = refs[2 * n:]
        x, y, c = lax.axis_index("x"), lax.axis_index("y"), lax.axis_index("c")
        me, sib = (x, y, c), (x, y, 1 - c)
        chips = [(1 - x, y), (x, 1 - y), (1 - x, 1 - y)]

        def slot(w, p):
            return outs[w].at[4 * p[0] + 2 * p[1] + p[2]]

        def cp(w, k, block, to, src=None):
            return pltpu.make_async_remote_copy(
                src_ref=slot(w, block) if src is None else src, dst_ref=slot(w, block),
                send_sem=send.at[w * 7 + k], recv_sem=recv.at[w * 7 + k], device_id=to, device_id_type=MESH)

        mine = [pltpu.make_async_copy(ins[w], slot(w, me), lsem.at[w]) for w in range(n)]
        for m in mine:
            m.start()
        first = []
        for w in range(n):
            first.append(cp(w, 0, me, sib, src=ins[w]))
            first += [cp(w, 1 + j, me, (*chip, c), src=ins[w]) for j, chip in enumerate(chips)]
        for f in first:
            f.start()
        passed = []
        for j, chip in enumerate(chips):
            for w in range(n):
                cp(w, 1 + j, (*chip, c), me).wait_recv()
                fwd = cp(w, 4 + j, (*chip, c), sib)
                fwd.start()
                passed.append(fwd)
        for w in range(n):
            cp(w, 0, sib, me).wait_recv()
            for j, chip in enumerate(chips):
                cp(w, 4 + j, (*chip, 1 - c), me).wait_recv()
        for f in first + passed:
            f.wait_send()
        for m in mine:
            m.wait()

    return pl.pallas_call(
        body, name=name, in_specs=[HBM] * n, out_specs=[HBM] * n,
        out_shape=[jax.ShapeDtypeStruct((8,) + a.shape, a.dtype) for a in arrs],
        scratch_shapes=[pltpu.SemaphoreType.DMA((7 * n,)), pltpu.SemaphoreType.DMA((7 * n,)),
                        pltpu.SemaphoreType.DMA((n,))],
    )(*arrs)


def _sibling_exchange(arrs, name):
    n = len(arrs)

    def body(*refs):
        ins, outs = refs[:n], refs[n:2 * n]
        send, recv = refs[2 * n:]
        x, y, c = lax.axis_index("x"), lax.axis_index("y"), lax.axis_index("c")
        copies = [pltpu.make_async_remote_copy(
            src_ref=ins[w].at[:, 1 - c], dst_ref=outs[w], send_sem=send.at[w], recv_sem=recv.at[w],
            device_id=(x, y, 1 - c), device_id_type=MESH) for w in range(n)]
        for cpy in copies:
            cpy.start()
        for cpy in copies:
            cpy.wait()

    return pl.pallas_call(
        body, name=name, in_specs=[HBM] * n, out_specs=[HBM] * n,
        out_shape=[jax.ShapeDtypeStruct((a.shape[0],) + a.shape[2:], a.dtype) for a in arrs],
        scratch_shapes=[pltpu.SemaphoreType.DMA((n,)), pltpu.SemaphoreType.DMA((n,))],
    )(*arrs)


def _chip_exchange(arrs, name):
    n = len(arrs)

    def body(*refs):
        ins, outs = refs[:n], refs[n:2 * n]
        send, recv = refs[2 * n:]
        x, y, c = lax.axis_index("x"), lax.axis_index("y"), lax.axis_index("c")
        chips = [(1 - x, y), (x, 1 - y), (1 - x, 1 - y)]
        copies = []
        for w in range(n):
            for j, (cx, cy) in enumerate(chips):
                copies.append(pltpu.make_async_remote_copy(
                    src_ref=ins[w].at[2 * cx + cy], dst_ref=outs[w].at[j], send_sem=send.at[3 * w + j],
                    recv_sem=recv.at[3 * w + j], device_id=(cx, cy, c), device_id_type=MESH))
        for cpy in copies:
            cpy.start()
        for cpy in copies:
            cpy.wait()

    return pl.pallas_call(
        body, name=name, in_specs=[HBM] * n, out_specs=[HBM] * n,
        out_shape=[jax.ShapeDtypeStruct((3,) + a.shape[1:], a.dtype) for a in arrs],
        scratch_shapes=[pltpu.SemaphoreType.DMA((3 * n,)), pltpu.SemaphoreType.DMA((3 * n,))],
    )(*arrs)


def _pad_cols(a, to):
    return jnp.pad(a, ((0, 0), (0, to - a.shape[1])))


def _local_step(x, mem, positions, target, wts, small):
    t, d = x.shape
    w_in = wts["w_in"]
    n_glr = GLA_W + GLA_RANK
    w_a = _pad_cols(w_in[:, :n_glr], HA_W)
    w_b = w_in[:, n_glr:]
    w2p = jnp.pad(wts["gla_gate_w2"], ((0, LANES - GLA_RANK), (0, 0)))
    wg = _pad_cols(wts["ffn_w_in"][:, :D_FF], FFP)
    wu = _pad_cols(wts["ffn_w_in"][:, D_FF:], FFP)
    wo = jnp.pad(wts["ffn_w_out"], ((0, FFP - D_FF), (0, 0)))
    cw, cb = small["ffn_conv_w"], small["ffn_conv_b"]
    zrow = jnp.zeros((4, FFP), F32)
    cwg = jnp.concatenate([_pad_cols(cw[:, :D_FF], FFP), _pad_cols(cb[:, :D_FF], FFP), zrow], axis=0)
    cwu = jnp.concatenate([_pad_cols(cw[:, D_FF:], FFP), _pad_cols(cb[:, D_FF:], FFP), zrow], axis=0)
    tabs = _rope_tables(positions)
    xb = x.astype(BF16)
    memb = mem.astype(BF16)

    h_a = _matmul(xb, w_a, "nn", F32, 512, 640, d, "mm_h_a")
    h_b = _matmul(xb, w_b, "nn", F32, 512, 1024, d, "mm_h_b")
    o_g, o_raw, s_before = _gla_fwd(h_a, w2p, small["gla_gate_b"], small["gla_norm_g"])
    qr, kr, vr = _rope_fwd(h_b, tabs)
    branch_qkv, o_tok, l_tok = [], [], []
    for bi, dil in enumerate(DIL_DILATIONS):
        qb_, kb_, vb_ = _to_branch(qr, dil), _to_branch(kr, dil), _to_branch(vr, dil)
        nb = t // dil // DIL_BAND
        o_b, l_b = _dil_fwd(qb_, kb_, vb_, nb, f"dil_fwd{bi}")
        branch_qkv.append((qb_, kb_, vb_, nb))
        o_tok.append(_from_branch(o_b, dil, t))
        l_tok.append(_from_branch(l_b, dil, t))
    o_d_b, o_d, lse_tot = _dil_merge(o_tok, l_tok)
    mixin = jnp.concatenate([o_g, o_d_b], axis=1)
    mix = _matmul(mixin, wts["w_out"], "nn", F32, 512, 1024, d, "mm_mix")
    x1, x1b = _ln_fwd(x, mix, small["ln1_g"], small["ln1_b"], "ln1_fwd")

    q_ca = _matmul(x1b, wts["ca_wq"], "nn", BF16, 512, 1024, d, "mm_caq")
    memkv = _matmul(memb, wts["ca_wkv"], "nn", BF16, mem.shape[0], 1024, d, "mm_memkv")
    o_c = _ca_fwd(q_ca, memkv)
    ca_out = _matmul(o_c, wts["ca_wo"], "nn", F32, 512, 1024, d, "mm_cao")
    x2, x2b = _ln_fwd(x1, ca_out, small["ln2_g"], small["ln2_b"], "ln2_fwd")

    ug0 = _matmul(x2b, wg, "nn", F32, 512, 512, d, "mm_ug")
    uu0 = _matmul(x2b, wu, "nn", F32, 512, 512, d, "mm_uu")
    act = _swiglu_fwd(ug0, uu0, cwg, cwu)
    ffn = _matmul(act, wo, "nn", F32, 512, 512, FFP, "mm_ffn")

    dp3, dp3b, dg3, db3, loss_part = _ln_bwd(x2, ffn, small["ln3_g"], small["ln3_b"], target, True, "ln3_bwd")
    g_wo = _matmul(act, dp3b, "tn", F32, 512, 1024, 1024, "mm_g_wo")
    dact = _matmul(dp3b, wo, "nt", F32, 512, 512, d, "mm_dact")
    dug, duu, dcwg, dcwu = _swiglu_bwd(ug0, uu0, cwg, cwu, dact)
    g_wg = _matmul(x2b, dug, "tn", F32, 512, 512, 1024, "mm_g_wg")
    g_wu = _matmul(x2b, duu, "tn", F32, 512, 512, 1024, "mm_g_wu")
    dx2 = _matmul(dug, wg, "nt", F32, 512, 512, FFP // 2, "mm_dx2_g", resid=dp3, resid_scale=ALPHA)
    dx2 = _matmul(duu, wu, "nt", F32, 512, 512, FFP // 2, "mm_dx2_u", resid=dx2)

    dp2, dp2b, dg2, db2 = _ln_bwd(x1, ca_out, small["ln2_g"], small["ln2_b"], dx2, False, "ln2_bwd")
    g_cao = _matmul(o_c, dp2b, "tn", F32, 512, 1024, 1024, "mm_g_cao")
    do_c = _matmul(dp2b, wts["ca_wo"], "nt", BF16, 512, 1024, d, "mm_do_c")
    dq_ca, dmemkv = _ca_bwd(q_ca, memkv, do_c)
    g_caq = _matmul(x1b, dq_ca, "tn", F32, 512, 1024, 1024, "mm_g_caq")
    g_cakv = _matmul(memb, dmemkv.astype(BF16), "tn", F32, 512, 1024, mem.shape[0], "mm_g_cakv")
    dx1 = _matmul(dq_ca, wts["ca_wq"], "nt", F32, 512, 1024, d, "mm_dx1", resid=dp2, resid_scale=ALPHA)

    dp1, dp1b, dg1, db1 = _ln_bwd(x, mix, small["ln1_g"], small["ln1_b"], dx1, False, "ln1_bwd")
    g_wout = _matmul(mixin, dp1b, "tn", F32, 512, 1024, 1024, "mm_g_wout")
    dmix = _matmul(dp1b, wts["w_out"], "nt", F32, 512, 1024, d, "mm_dmix")
    dh_a, dw2, dgate_b, dnorm_g = _gla_bwd(h_a, w2p, small["gla_gate_b"], small["gla_norm_g"], o_raw, s_before, dmix)
    do_d, dd = _dil_bwd_prep(dmix, o_d)
    dqs, dks, dvs = [], [], []
    for bi, dil in enumerate(DIL_DILATIONS):
        qb_, kb_, vb_, nb = branch_qkv[bi]
        dq_b, dk_b, dv_b = _dil_bwd(qb_, kb_, vb_, _to_branch(do_d, dil), _to_branch(lse_tot, dil),
                                    _to_branch(dd, dil), nb, f"dil_bwd{bi}")
        dqs.append(_from_branch(dq_b, dil, t))
        dks.append(_from_branch(dk_b, dil, t))
        dvs.append(_from_branch(dv_b, dil, t))
    dh_b = _dil_dh(dqs, dks, dvs, tabs)
    xbt = xb
    g_wa = _matmul(xbt, dh_a, "tn", F32, 512, 640, 1024, "mm_g_wa")
    g_wb = _matmul(xbt, dh_b, "tn", F32, 512, 1024, 1024, "mm_g_wb")
    dx = _matmul(dh_a, w_a, "nt", F32, 512, 512, HA_W, "mm_dx_a", resid=dp1, resid_scale=ALPHA)
    dx = _matmul(dh_b, w_b, "nt", F32, 512, 512, HB_W, "mm_dx_b", resid=dx)

    grads = {
        "w_in": jnp.concatenate([g_wa[:, :n_glr], g_wb], axis=1),
        "w_out": g_wout, "ca_wq": g_caq, "ca_wkv": g_cakv, "ca_wo": g_cao,
        "ffn_w_in": jnp.concatenate([g_wg[:, :D_FF], g_wu[:, :D_FF]], axis=1),
        "ffn_w_out": g_wo[:D_FF],
    }
    small_parts = {
        "gla_gate_b": dgate_b, "gla_norm_g": dnorm_g, "ln1_g": dg1, "ln1_b": db1, "ln2_g": dg2, "ln2_b": db2,
        "ln3_g": dg3, "ln3_b": db3,
        "conv": jnp.concatenate([dcwg[:, :D_FF], dcwu[:, :D_FF]], axis=1),
        "gla_gate_w2": dw2[:GLA_RANK],
    }
    return loss_part, dx, grads, small_parts


BIG = ("w_in", "w_out", "ca_wq", "ca_wkv", "ca_wo", "ffn_w_in", "ffn_w_out")
COL_SHARDED = ("w_in", "ca_wkv", "ffn_w_in")
SMALL_ORDER = ("gla_gate_b", "gla_norm_g", "ln1_g", "ln1_b", "ln2_g", "ln2_b", "ln3_g", "ln3_b")


def _gathered_full(name, g):
    if name in COL_SHARDED:
        return g.transpose(1, 0, 2).reshape(g.shape[1], 8 * g.shape[2])
    return g.reshape(8 * g.shape[1], g.shape[2])


def _to_slabs(name, full):
    if name in COL_SHARDED:
        r, cc = full.shape
        s = full.reshape(r, 8, cc // 8).transpose(1, 0, 2)
    else:
        rr, c = full.shape
        s = full.reshape(8, rr // 8, c)
    return s.reshape((4, 2) + s.shape[1:])


def kernel(x, mem, positions, w_in, gla_gate_w2, gla_gate_b, gla_norm_g, w_out, ln1_g, ln1_b, ca_wq, ca_wkv, ca_wo, ln2_g, ln2_b, ffn_w_in, ffn_conv_w, ffn_conv_b, ffn_w_out, ln3_g, ln3_b, loss_target, m_w_in, m_gla_gate_w2, m_gla_gate_b, m_gla_norm_g, m_w_out, m_ln1_g, m_ln1_b, m_ca_wq, m_ca_wkv, m_ca_wo, m_ln2_g, m_ln2_b, m_ffn_w_in, m_ffn_conv_w, m_ffn_conv_b, m_ffn_w_out, m_ln3_g, m_ln3_b, v_w_in, v_gla_gate_w2, v_gla_gate_b, v_gla_norm_g, v_w_out, v_ln1_g, v_ln1_b, v_ca_wq, v_ca_wkv, v_ca_wo, v_ln2_g, v_ln2_b, v_ffn_w_in, v_ffn_conv_w, v_ffn_conv_b, v_ffn_w_out, v_ln3_g, v_ln3_b):
    weights = dict(w_in=w_in, gla_gate_w2=gla_gate_w2, gla_gate_b=gla_gate_b, gla_norm_g=gla_norm_g, w_out=w_out,
                   ln1_g=ln1_g, ln1_b=ln1_b, ca_wq=ca_wq, ca_wkv=ca_wkv, ca_wo=ca_wo, ln2_g=ln2_g, ln2_b=ln2_b,
                   ffn_w_in=ffn_w_in, ffn_conv_w=ffn_conv_w, ffn_conv_b=ffn_conv_b, ffn_w_out=ffn_w_out,
                   ln3_g=ln3_g, ln3_b=ln3_b)
    moms = dict(w_in=(m_w_in, v_w_in), gla_gate_w2=(m_gla_gate_w2, v_gla_gate_w2), gla_gate_b=(m_gla_gate_b, v_gla_gate_b),
                gla_norm_g=(m_gla_norm_g, v_gla_norm_g), w_out=(m_w_out, v_w_out), ln1_g=(m_ln1_g, v_ln1_g),
                ln1_b=(m_ln1_b, v_ln1_b), ca_wq=(m_ca_wq, v_ca_wq), ca_wkv=(m_ca_wkv, v_ca_wkv), ca_wo=(m_ca_wo, v_ca_wo),
                ln2_g=(m_ln2_g, v_ln2_g), ln2_b=(m_ln2_b, v_ln2_b), ffn_w_in=(m_ffn_w_in, v_ffn_w_in),
                ffn_conv_w=(m_ffn_conv_w, v_ffn_conv_w), ffn_conv_b=(m_ffn_conv_b, v_ffn_conv_b),
                ffn_w_out=(m_ffn_w_out, v_ffn_w_out), ln3_g=(m_ln3_g, v_ln3_g), ln3_b=(m_ln3_b, v_ln3_b))
    order = list(weights)
    xi, yi, ci = lax.axis_index("x"), lax.axis_index("y"), lax.axis_index("c")
    me = 4 * xi + 2 * yi + ci

    shards = [weights[n][0].astype(BF16) for n in BIG] + [gla_gate_w2[0].astype(BF16), ffn_conv_w[0]]
    gathered = _all_gather(shards, "ag_weights")
    wts = {n: _gathered_full(n, g) for n, g in zip(BIG, gathered[:len(BIG)])}
    wts["gla_gate_w2"] = gathered[-2].transpose(1, 0, 2).reshape(GLA_RANK, -1)
    conv_w_full = gathered[-1].transpose(1, 0, 2).reshape(3, -1)
    small = dict(gla_gate_b=gla_gate_b, gla_norm_g=gla_norm_g, ln1_g=ln1_g, ln1_b=ln1_b, ln2_g=ln2_g, ln2_b=ln2_b,
                 ln3_g=ln3_g, ln3_b=ln3_b, ffn_conv_w=conv_w_full, ffn_conv_b=ffn_conv_b)

    loss_part, dx, grads, small_parts = _local_step(x[0], mem[0], positions[0], loss_target[0], wts, small)
    loss = lax.psum(jnp.sum(loss_part), ("x", "y", "c"))

    slabs = [_to_slabs(n, grads[n]) for n in BIG]
    from_sib = _sibling_exchange(slabs, "rs_sibling")
    core = ci.reshape(1).astype(jnp.int32)
    pair32, pair16 = [], []
    for n, s, r in zip(BIG, slabs, from_sib):
        p32, p16 = _pair_add(s, r, core, f"pair_add_{n}")
        pair32.append(p32)
        pair16.append(p16)
    from_chips = _chip_exchange(pair16, "rs_chips")
    chip = (2 * xi + yi).reshape(1).astype(jnp.int32)
    out = {}
    for n, p32, rc in zip(BIG, pair32, from_chips):
        m_, v_ = moms[n]
        out[n] = _adamw_big(weights[n][0], m_[0], v_[0], p32, rc, chip, f"adamw_{n}")

    packed = jnp.concatenate([small_parts[n] for n in SMALL_ORDER] + [small_parts["conv"],
                             small_parts["gla_gate_w2"].reshape(SUBLANES, -1)], axis=1)
    pad = (-packed.shape[1]) % 2048
    packed = jnp.pad(packed, ((0, 0), (0, pad)))
    (allp,) = _all_gather([packed], "ag_small")
    dev_sum, row_sum = _small_reduce(allp)
    off = 0
    for n in SMALL_ORDER:
        width = weights[n].shape[1]
        g = row_sum[0:1, off:off + width]
        off += width
        m_, v_ = moms[n]
        out[n] = _adamw(weights[n], m_, v_, g, f"adamw_{n}")
    conv_g = dev_sum[:, off:off + 2 * D_FF]
    off += 2 * D_FF
    g_cb = conv_g[3:4]
    out["ffn_conv_b"] = _adamw(ffn_conv_b, m_ffn_conv_b, v_ffn_conv_b, g_cb, "adamw_ffn_conv_b")
    wsh = ffn_conv_w.shape[2]
    g_cw = lax.dynamic_slice_in_dim(conv_g[0:3], me * wsh, wsh, axis=1)
    out["ffn_conv_w"] = _adamw(ffn_conv_w[0], m_ffn_conv_w[0], v_ffn_conv_w[0], g_cw, "adamw_ffn_conv_w")
    w2_g = dev_sum[:, off:off + GLA_RANK * GLA_HEADS * GLA_DK // SUBLANES].reshape(GLA_RANK, GLA_HEADS * GLA_DK)
    wsh2 = gla_gate_w2.shape[2]
    g_w2 = lax.dynamic_slice_in_dim(w2_g, me * wsh2, wsh2, axis=1)
    out["gla_gate_w2"] = _adamw(gla_gate_w2[0], m_gla_gate_w2[0], v_gla_gate_w2[0], g_w2, "adamw_gla_gate_w2")

    def shaped(n, a):
        return a.reshape(weights[n].shape)

    res = [loss, dx[None]]
    for k in range(4):
        res += [shaped(n, out[n][k]) for n in order]
    return tuple(res)


def _adamw_big(w, m, v, p32, rc, chip, name):
    r, c = w.shape
    tr, tc = _tile2d(r, c)
    blk = pl.BlockSpec((tr, tc), lambda i, j, s: (i, j))
    own = pl.BlockSpec((None, tr, tc), lambda i, j, s: (s[0], i, j))
    others = [pl.BlockSpec((None, tr, tc), lambda i, j, s, k=k: (k, i, j)) for k in range(3)]

    def body(s_ref, w_ref, m_ref, v_ref, p_ref, r0_ref, r1_ref, r2_ref, g_ref, d_ref, nm_ref, nv_ref):
        g = ((p_ref[...] + r0_ref[...].astype(F32)) + r1_ref[...].astype(F32)) + r2_ref[...].astype(F32)
        d_ref[...], nm_ref[...], nv_ref[...] = _adamw_math(w_ref[...], m_ref[...], v_ref[...], g)
        g_ref[...] = g

    gs = pltpu.PrefetchScalarGridSpec(num_scalar_prefetch=1, grid=(r // tr, c // tc),
                                      in_specs=[blk, blk, blk, own] + others, out_specs=[blk] * 4)
    return pl.pallas_call(body, name=name, grid_spec=gs, out_shape=[jax.ShapeDtypeStruct((r, c), F32)] * 4,
                          compiler_params=_params(("parallel", "parallel")))(chip, w, m, v, p32, rc, rc, rc)
```

```python
import functools
import math

import jax
import jax.numpy as jnp
from jax import lax
from jax.experimental import pallas as pl
from jax.experimental.pallas import tpu as pltpu

F32 = jnp.float32
BF16 = jnp.bfloat16
MESH = pl.DeviceIdType.MESH

D_MODEL = 2048
LN_EPS = 1e-5
GLA_HEADS = 4
GLA_DV = 256
GLA_DK = 128
GLA_RANK = 16
GLA_TAU = 16.0
GLA_CHUNK = 64
DIL_HD = 128
DIL_HEADS = 8
DIL_BAND = 128
DIL_DILATIONS = (1, 4, 16)
ROPE_THETA = 500000.0
ROPE_DIMS = 32
CA_HEADS = 4
CA_HD = 512
D_FF = 5504
ALPHA = 2.0 ** 0.25
ADAM_LR = 0.001
ADAM_B1 = 0.9
ADAM_B2 = 0.999
ADAM_EPS = 1e-08
ADAM_WD = 0.01
ADAM_STEP = 10

LANES = 128
SUBLANES = 8
VMEM_LIMIT = 56 * 1024 * 1024

GLA_W = 2 * GLA_HEADS * GLA_DK + 2 * GLA_HEADS * GLA_DV
HA_W = GLA_W + LANES
HB_W = 3 * DIL_HEADS * DIL_HD
FFP = 5632
NEG = -1e30


def _params(sem):
    return pltpu.CompilerParams(dimension_semantics=sem, vmem_limit_bytes=VMEM_LIMIT)


def _sigmoid(x):
    return 1.0 / (1.0 + jnp.exp(-x))


def _dot(a, b, dn, precision=None):
    return lax.dot_general(a, b, (dn, ((), ())), preferred_element_type=F32, precision=precision)


NN = ((1,), (0,))
NT = ((1,), (1,))
TN = ((0,), (0,))


def _bf(v):
    return v if v.dtype == BF16 else v.astype(BF16)


def _matmul(a, b, kind, out_dtype, tm, tn, tk, name, resid=None, resid_scale=1.0):
    if kind == "nn":
        (m, k), (k2, n) = a.shape, b.shape
    elif kind == "nt":
        (m, k), (n, k2) = a.shape, b.shape
    else:
        (k, m), (k2, n) = a.shape, b.shape
    assert k == k2 and m % tm == 0 and n % tn == 0 and k % tk == 0, (name, a.shape, b.shape, tm, tn, tk)
    nk = k // tk
    dn = {"nn": NN, "nt": NT, "tn": TN}[kind]
    a_spec = pl.BlockSpec((tk, tm), lambda i, j, kk: (kk, i)) if kind == "tn" else pl.BlockSpec((tm, tk), lambda i, j, kk: (i, kk))
    b_spec = pl.BlockSpec((tn, tk), lambda i, j, kk: (j, kk)) if kind == "nt" else pl.BlockSpec((tk, tn), lambda i, j, kk: (kk, j))
    o_spec = pl.BlockSpec((tm, tn), lambda i, j, kk: (i, j))
    has_resid = resid is not None

    def body(*refs):
        if has_resid:
            a_ref, b_ref, r_ref, o_ref = refs[:4]
        else:
            a_ref, b_ref, o_ref = refs[:3]
            r_ref = None
        part = _dot(_bf(a_ref[...]), _bf(b_ref[...]), dn)

        def finish(acc):
            if has_resid:
                acc = acc + resid_scale * r_ref[...].astype(F32)
            o_ref[...] = acc.astype(out_dtype)

        if nk == 1:
            finish(part)
        else:
            acc_ref = refs[-1]
            kk = pl.program_id(2)

            @pl.when(kk == 0)
            def _():
                acc_ref[...] = part

            @pl.when(kk > 0)
            def _():
                acc_ref[...] += part

            @pl.when(kk == nk - 1)
            def _():
                finish(acc_ref[...])

    in_specs = [a_spec, b_spec] + ([o_spec] if has_resid else [])
    args = (a, b) + ((resid,) if has_resid else ())
    return pl.pallas_call(
        body, name=name, out_shape=jax.ShapeDtypeStruct((m, n), out_dtype),
        grid=(m // tm, n // tn, nk), in_specs=in_specs, out_specs=o_spec,
        scratch_shapes=[pltpu.VMEM((tm, tn), F32)] if nk > 1 else [],
        compiler_params=_params(("parallel", "parallel", "arbitrary")),
    )(*args)


def _ln_core(xres, f):
    p = ALPHA * xres + f
    mu = jnp.mean(p, axis=-1, keepdims=True)
    xc = p - mu
    var = jnp.mean(xc * xc, axis=-1, keepdims=True)
    rstd = lax.rsqrt(var + LN_EPS)
    return xc * rstd, rstd


def _rows8(v):
    r, c = v.shape
    return jnp.sum(v.reshape(r // SUBLANES, SUBLANES, c), axis=0)


def _ln_fwd(xres, f, g, b, name, tr=256):
    t, d = xres.shape
    row = pl.BlockSpec((tr, d), lambda i: (i, 0))
    vec = pl.BlockSpec((1, d), lambda i: (0, 0))

    def body(x_ref, f_ref, g_ref, b_ref, y_ref, yb_ref):
        xhat, _ = _ln_core(x_ref[...], f_ref[...])
        y = xhat * g_ref[...] + b_ref[...]
        y_ref[...] = y
        yb_ref[...] = y.astype(BF16)

    return pl.pallas_call(
        body, name=name, grid=(t // tr,), in_specs=[row, row, vec, vec], out_specs=[row, row],
        out_shape=[jax.ShapeDtypeStruct((t, d), F32), jax.ShapeDtypeStruct((t, d), BF16)],
        compiler_params=_params(("parallel",)),
    )(xres, f, g, b)


def _ln_bwd(xres, f, g, b, dy_or_target, loss_head, name, tr=256):
    t, d = xres.shape
    row = pl.BlockSpec((tr, d), lambda i: (i, 0))
    vec = pl.BlockSpec((1, d), lambda i: (0, 0))
    acc = pl.BlockSpec((SUBLANES, d), lambda i: (0, 0))
    lacc = pl.BlockSpec((SUBLANES, LANES), lambda i: (0, 0))

    def body(x_ref, f_ref, g_ref, b_ref, t_ref, dp_ref, dpb_ref, dg_ref, db_ref, *rest):
        i = pl.program_id(0)
        xhat, rstd = _ln_core(x_ref[...], f_ref[...])
        if loss_head:
            err = xhat * g_ref[...] + b_ref[...] - t_ref[...]
            dy = err * (1.0 / d)
            sq = err * err
            lanes = sq[:, :LANES]
            for kk in range(1, d // LANES):
                lanes = lanes + sq[:, kk * LANES:(kk + 1) * LANES]
            lpart = _rows8(lanes) * (0.5 / d)
        else:
            dy = t_ref[...]
        dxh = dy * g_ref[...]
        m1 = jnp.mean(dxh, axis=-1, keepdims=True)
        m2 = jnp.mean(dxh * xhat, axis=-1, keepdims=True)
        dp = rstd * (dxh - m1 - xhat * m2)
        dp_ref[...] = dp
        dpb_ref[...] = dp.astype(BF16)
        dgp = _rows8(dy * xhat)
        dbp = _rows8(dy)

        @pl.when(i == 0)
        def _():
            dg_ref[...] = dgp
            db_ref[...] = dbp
            if loss_head:
                rest[0][...] = lpart

        @pl.when(i > 0)
        def _():
            dg_ref[...] += dgp
            db_ref[...] += dbp
            if loss_head:
                rest[0][...] += lpart

    out_shape = [jax.ShapeDtypeStruct((t, d), F32), jax.ShapeDtypeStruct((t, d), BF16),
                 jax.ShapeDtypeStruct((SUBLANES, d), F32), jax.ShapeDtypeStruct((SUBLANES, d), F32)]
    out_specs = [row, row, acc, acc]
    if loss_head:
        out_shape.append(jax.ShapeDtypeStruct((SUBLANES, LANES), F32))
        out_specs.append(lacc)
    return pl.pallas_call(
        body, name=name, grid=(t // tr,), in_specs=[row, row, vec, vec, row], out_specs=out_specs,
        out_shape=out_shape, compiler_params=_params(("arbitrary",)),
    )(xres, f, g, b, dy_or_target)


def _gla_gates(glr, w2, gb):
    z = _dot(_bf(glr), w2, NN) + gb
    lg = (jnp.minimum(z, 0.0) - jnp.log(1.0 + jnp.exp(-jnp.abs(z)))) * (1.0 / GLA_TAU)
    c = z.shape[0]
    ri = lax.broadcasted_iota(jnp.int32, (c, c), 0)
    ci = lax.broadcasted_iota(jnp.int32, (c, c), 1)
    tri = (ci <= ri).astype(F32)
    bcum = _dot(tri, lg, NN, precision=lax.Precision.HIGHEST)
    blast = jnp.sum(lg, axis=0, keepdims=True)
    return z, bcum, blast, tri


def _gla_specs(t):
    c = GLA_CHUNK
    return c, t // c


def _gla_fwd(h_a, w2p, gate_b, norm_g):
    t = h_a.shape[0]
    c, n = _gla_specs(t)
    hk, hv = GLA_HEADS * GLA_DK, GLA_HEADS * GLA_DV
    scale = GLA_DK ** -0.5

    def body(q_ref, k_ref, v_ref, r_ref, glr_ref, w2_ref, gb_ref, ng_ref, og_ref, oraw_ref, sb_ref, st_ref):
        i = pl.program_id(0)

        @pl.when(i == 0)
        def _():
            st_ref[...] = jnp.zeros_like(st_ref)

        _, bcum, blast, _ = _gla_gates(glr_ref[...], w2_ref[...], gb_ref[...])
        ri = lax.broadcasted_iota(jnp.int32, (c, c), 0)
        ci = lax.broadcasted_iota(jnp.int32, (c, c), 1)
        causal = ci <= ri
        for h in range(GLA_HEADS):
            ks = slice(h * GLA_DK, (h + 1) * GLA_DK)
            vs = slice(h * GLA_DV, (h + 1) * GLA_DV)
            b_h, bl_h = bcum[:, ks], blast[:, ks]
            q_h, k_h = q_ref[:, ks], k_ref[:, ks]
            v_h = _bf(v_ref[:, vs])
            qi = _bf(q_h * scale * jnp.exp(b_h))
            ki = _bf(k_h * jnp.exp(-b_h))
            ke = _bf(k_h * jnp.exp(bl_h - b_h))
            st = st_ref[h]
            sb_ref[0, h] = st
            a = jnp.where(causal, _dot(qi, ki, NT), 0.0)
            o = _dot(_bf(a), v_h, NN) + _dot(qi, _bf(st), NT)
            st_ref[h] = st * jnp.exp(bl_h) + _dot(v_h, ke, TN)
            oraw_ref[:, vs] = o
            mu = jnp.mean(o, axis=-1, keepdims=True)
            oc = o - mu
            var = jnp.mean(oc * oc, axis=-1, keepdims=True)
            xh = oc * lax.rsqrt(var + LN_EPS)
            r_h = r_ref[:, vs]
            og_ref[:, vs] = (xh * ng_ref[:, vs] * (r_h * _sigmoid(r_h))).astype(BF16)

    return pl.pallas_call(
        body, name="gla_fwd", grid=(n,),
        in_specs=[pl.BlockSpec((c, hk), lambda i: (i, 0)), pl.BlockSpec((c, hk), lambda i: (i, 1)),
                  pl.BlockSpec((c, hv), lambda i: (i, 1)), pl.BlockSpec((c, hv), lambda i: (i, 2)),
                  pl.BlockSpec((c, LANES), lambda i: (i, GLA_W // LANES)),
                  pl.BlockSpec((LANES, hk), lambda i: (0, 0)), pl.BlockSpec((1, hk), lambda i: (0, 0)),
                  pl.BlockSpec((1, hv), lambda i: (0, 0))],
        out_specs=[pl.BlockSpec((c, hv), lambda i: (i, 0)), pl.BlockSpec((c, hv), lambda i: (i, 0)),
                   pl.BlockSpec((1, GLA_HEADS, GLA_DV, GLA_DK), lambda i: (i, 0, 0, 0))],
        out_shape=[jax.ShapeDtypeStruct((t, hv), BF16), jax.ShapeDtypeStruct((t, hv), F32),
                   jax.ShapeDtypeStruct((n, GLA_HEADS, GLA_DV, GLA_DK), F32)],
        scratch_shapes=[pltpu.VMEM((GLA_HEADS, GLA_DV, GLA_DK), F32)],
        compiler_params=_params(("arbitrary",)),
    )(h_a, h_a, h_a, h_a, h_a, w2p, gate_b, norm_g)


def _gla_bwd(h_a, w2p, gate_b, norm_g, o_raw, s_before, dmix):
    t = h_a.shape[0]
    c, n = _gla_specs(t)
    hk, hv = GLA_HEADS * GLA_DK, GLA_HEADS * GLA_DV
    scale = GLA_DK ** -0.5
    rev = lambda i: n - 1 - i

    def body(q_ref, k_ref, v_ref, r_ref, glr_ref, w2_ref, gb_ref, ng_ref, oraw_ref, sb_ref, do_ref,
             dh_ref, dw2_ref, dgb_ref, dng_ref, dst_ref):
        i = pl.program_id(0)

        @pl.when(i == 0)
        def _():
            dst_ref[...] = jnp.zeros_like(dst_ref)

        glr = glr_ref[...]
        z, bcum, blast, tri = _gla_gates(glr, w2_ref[...], gb_ref[...])
        ri = lax.broadcasted_iota(jnp.int32, (c, c), 0)
        ci = lax.broadcasted_iota(jnp.int32, (c, c), 1)
        causal = ci <= ri
        dlg_parts = []
        dng_parts = []
        for h in range(GLA_HEADS):
            ks = slice(h * GLA_DK, (h + 1) * GLA_DK)
            vs = slice(h * GLA_DV, (h + 1) * GLA_DV)
            o = oraw_ref[:, vs]
            mu = jnp.mean(o, axis=-1, keepdims=True)
            oc = o - mu
            var = jnp.mean(oc * oc, axis=-1, keepdims=True)
            rstd = lax.rsqrt(var + LN_EPS)
            xh = oc * rstd
            r_h = r_ref[:, vs]
            sg = _sigmoid(r_h)
            silu = r_h * sg
            dout = do_ref[:, vs]
            ng = ng_ref[:, vs]
            dng_parts.append(_rows8(dout * xh * silu))
            dr = dout * xh * ng * (sg * (1.0 + r_h * (1.0 - sg)))
            dxh = dout * ng * silu
            m1 = jnp.mean(dxh, axis=-1, keepdims=True)
            m2 = jnp.mean(dxh * xh, axis=-1, keepdims=True)
            do_raw = _bf(rstd * (dxh - m1 - xh * m2))
            b_h, bl_h = bcum[:, ks], blast[:, ks]
            q_h, k_h = q_ref[:, ks], k_ref[:, ks]
            v_h = _bf(v_ref[:, vs])
            eb, enb, eend = jnp.exp(b_h), jnp.exp(-b_h), jnp.exp(bl_h - b_h)
            decay = jnp.exp(bl_h)
            qi_f, ki_f, ke_f = q_h * scale * eb, k_h * enb, k_h * eend
            qi, ki, ke = _bf(qi_f), _bf(ki_f), _bf(ke_f)
            st = sb_ref[0, h]
            dst = dst_ref[h]
            dst_b = _bf(dst)
            a = _bf(jnp.where(causal, _dot(qi, ki, NT), 0.0))
            da = _bf(jnp.where(causal, _dot(do_raw, v_h, NT), 0.0))
            dv = _dot(a, do_raw, TN) + _dot(ke, dst_b, NT)
            dqi = _dot(da, ki, NN) + _dot(do_raw, _bf(st), NN)
            dki = _dot(da, qi, TN)
            dke = _dot(v_h, dst_b, NN)
            dst_ref[h] = _dot(do_raw, qi, TN) + dst * decay
            dbl = decay * jnp.sum(st * dst, axis=0, keepdims=True) + jnp.sum(dke * ke_f, axis=0, keepdims=True)
            dbc = dqi * qi_f - dki * ki_f - dke * ke_f
            dlg_parts.append(_dot(tri, dbc, TN, precision=lax.Precision.HIGHEST) + dbl)
            dh_ref[:, ks] = (dqi * eb * scale).astype(BF16)
            dh_ref[:, hk + h * GLA_DK: hk + (h + 1) * GLA_DK] = (dki * enb + dke * eend).astype(BF16)
            dh_ref[:, 2 * hk + h * GLA_DV: 2 * hk + (h + 1) * GLA_DV] = dv.astype(BF16)
            dh_ref[:, 2 * hk + hv + h * GLA_DV: 2 * hk + hv + (h + 1) * GLA_DV] = dr.astype(BF16)
        dlg = jnp.concatenate(dlg_parts, axis=1)
        dz = dlg * (1.0 / GLA_TAU) * _sigmoid(-z)
        dz_b = _bf(dz)
        dh_ref[:, GLA_W:] = _dot(dz_b, w2_ref[...], NT).astype(BF16)
        dw2p = _dot(_bf(glr), dz_b, TN)
        dgbp = _rows8(dz)
        dngp = jnp.concatenate(dng_parts, axis=1)

        @pl.when(i == 0)
        def _():
            dw2_ref[...] = dw2p
            dgb_ref[...] = dgbp
            dng_ref[...] = dngp

        @pl.when(i > 0)
        def _():
            dw2_ref[...] += dw2p
            dgb_ref[...] += dgbp
            dng_ref[...] += dngp

    return pl.pallas_call(
        body, name="gla_bwd", grid=(n,),
        in_specs=[pl.BlockSpec((c, hk), lambda i: (rev(i), 0)), pl.BlockSpec((c, hk), lambda i: (rev(i), 1)),
                  pl.BlockSpec((c, hv), lambda i: (rev(i), 1)), pl.BlockSpec((c, hv), lambda i: (rev(i), 2)),
                  pl.BlockSpec((c, LANES), lambda i: (rev(i), GLA_W // LANES)),
                  pl.BlockSpec((LANES, hk), lambda i: (0, 0)), pl.BlockSpec((1, hk), lambda i: (0, 0)),
                  pl.BlockSpec((1, hv), lambda i: (0, 0)),
                  pl.BlockSpec((c, hv), lambda i: (rev(i), 0)),
                  pl.BlockSpec((1, GLA_HEADS, GLA_DV, GLA_DK), lambda i: (rev(i), 0, 0, 0)),
                  pl.BlockSpec((c, hv), lambda i: (rev(i), 0))],
        out_specs=[pl.BlockSpec((c, HA_W), lambda i: (rev(i), 0)),
                   pl.BlockSpec((LANES, hk), lambda i: (0, 0)),
                   pl.BlockSpec((SUBLANES, hk), lambda i: (0, 0)),
                   pl.BlockSpec((SUBLANES, hv), lambda i: (0, 0))],
        out_shape=[jax.ShapeDtypeStruct((t, HA_W), BF16), jax.ShapeDtypeStruct((LANES, hk), F32),
                   jax.ShapeDtypeStruct((SUBLANES, hk), F32), jax.ShapeDtypeStruct((SUBLANES, hv), F32)],
        scratch_shapes=[pltpu.VMEM((GLA_HEADS, GLA_DV, GLA_DK), F32)],
        compiler_params=_params(("arbitrary",)),
    )(h_a, h_a, h_a, h_a, h_a, w2p, gate_b, norm_g, o_raw, s_before, dmix)


def _rope_tables(positions):
    half = ROPE_DIMS // 2
    inv_freq = ROPE_THETA ** (-jnp.arange(0, ROPE_DIMS, 2, dtype=F32) / ROPE_DIMS)
    ang = positions.astype(F32).reshape(-1, 1) * inv_freq
    cos, sin = jnp.cos(ang), jnp.sin(ang)
    t = cos.shape[0]
    one = jnp.ones((t, DIL_HD - ROPE_DIMS), F32)
    zero = jnp.zeros((t, DIL_HD - ROPE_DIMS), F32)
    zh = jnp.zeros((t, half), F32)
    return (jnp.concatenate([cos, cos, one], axis=1), jnp.concatenate([-sin, zh, zero], axis=1),
            jnp.concatenate([zh, sin, zero], axis=1))


def _rope_apply(x, c, s1, s2):
    half = ROPE_DIMS // 2
    return x * c + pltpu.roll(x, DIL_HD - half, 1) * s1 + pltpu.roll(x, half, 1) * s2


def _rope_apply_t(dy, c, s1, s2):
    half = ROPE_DIMS // 2
    return dy * c + pltpu.roll(dy * s1, half, 1) + pltpu.roll(dy * s2, DIL_HD - half, 1)


def _rope_fwd(h_b, tabs, tr=256):
    t = h_b.shape[0]
    w = DIL_HEADS * DIL_HD
    scale = DIL_HD ** -0.5
    tab = pl.BlockSpec((tr, DIL_HD), lambda i: (i, 0))
    outb = pl.BlockSpec((tr, w), lambda i: (i, 0))

    def body(q_ref, k_ref, v_ref, c_ref, s1_ref, s2_ref, qo_ref, ko_ref, vo_ref):
        c, s1, s2 = c_ref[...], s1_ref[...], s2_ref[...]
        for h in range(DIL_HEADS):
            hs = slice(h * DIL_HD, (h + 1) * DIL_HD)
            qo_ref[:, hs] = _rope_apply(q_ref[:, hs] * scale, c, s1, s2).astype(BF16)
            ko_ref[:, hs] = _rope_apply(k_ref[:, hs], c, s1, s2).astype(BF16)
        vo_ref[...] = v_ref[...].astype(BF16)

    return pl.pallas_call(
        body, name="rope_fwd", grid=(t // tr,),
        in_specs=[pl.BlockSpec((tr, w), lambda i: (i, 0)), pl.BlockSpec((tr, w), lambda i: (i, 1)),
                  pl.BlockSpec((tr, w), lambda i: (i, 2)), tab, tab, tab],
        out_specs=[outb, outb, outb],
        out_shape=[jax.ShapeDtypeStruct((t, w), BF16)] * 3,
        compiler_params=_params(("parallel",)),
    )(h_b, h_b, h_b, *tabs)


def _dil_dh(dqs, dks, dvs, tabs, tr=128):
    t, w = dqs[0].shape
    scale = DIL_HD ** -0.5
    tab = pl.BlockSpec((tr, DIL_HD), lambda i: (i, 0))
    inb = pl.BlockSpec((tr, w), lambda i: (i, 0))

    def body(*refs):
        dq = refs[0][...] + refs[1][...] + refs[2][...]
        dk = refs[3][...] + refs[4][...] + refs[5][...]
        dv = refs[6][...] + refs[7][...] + refs[8][...]
        c, s1, s2 = refs[9][...], refs[10][...], refs[11][...]
        o_ref = refs[12]
        for h in range(DIL_HEADS):
            hs = slice(h * DIL_HD, (h + 1) * DIL_HD)
            o_ref[:, h * DIL_HD:(h + 1) * DIL_HD] = (_rope_apply_t(dq[:, hs], c, s1, s2) * scale).astype(BF16)
            o_ref[:, w + h * DIL_HD: w + (h + 1) * DIL_HD] = _rope_apply_t(dk[:, hs], c, s1, s2).astype(BF16)
        o_ref[:, 2 * w:] = dv.astype(BF16)

    return pl.pallas_call(
        body, name="dil_dh", grid=(t // tr,), in_specs=[inb] * 9 + [tab] * 3,
        out_specs=pl.BlockSpec((tr, 3 * w), lambda i: (i, 0)),
        out_shape=jax.ShapeDtypeStruct((t, 3 * w), BF16), compiler_params=_params(("parallel",)),
    )(*dqs, *dks, *dvs, *tabs)


BANDS = 8


def _to_branch(a, d):
    t, w = a.shape
    return a.reshape(t // d, d, w // DIL_HD, DIL_HD).transpose(1, 2, 0, 3).reshape(-1, DIL_HD)


def _from_branch(a, d, t):
    hds = a.shape[0] // t
    return a.reshape(d, hds, t // d, DIL_HD).transpose(2, 0, 1, 3).reshape(t, hds * DIL_HD)


def _band_masks(not_first):
    r = lax.broadcasted_iota(jnp.int32, (DIL_BAND, 2 * DIL_BAND), 0)
    c = lax.broadcasted_iota(jnp.int32, (DIL_BAND, 2 * DIL_BAND), 1)
    nf = jnp.full((DIL_BAND, 2 * DIL_BAND), not_first, jnp.int32)
    look_back = jnp.logical_and(jnp.logical_and(c < DIL_BAND, c >= r), nf > 0)
    own_band = jnp.logical_and(c >= DIL_BAND, (c - DIL_BAND) <= r)
    return jnp.logical_or(look_back, own_band)


def _dil_fwd(q, k, v, nb, name):
    rows = q.shape[0]
    blk = BANDS * DIL_BAND
    steps = rows // blk
    main = pl.BlockSpec((blk, DIL_HD), lambda i: (i, 0))
    prev = pl.BlockSpec((DIL_BAND, DIL_HD), lambda i: (jnp.maximum(i * BANDS - 1, 0), 0))

    def body(q_ref, k_ref, v_ref, kp_ref, vp_ref, o_ref, l_ref):
        i = pl.program_id(0)
        for j in range(BANDS):
            lo, hi = j * DIL_BAND, (j + 1) * DIL_BAND
            if j == 0:
                kcat = jnp.concatenate([kp_ref[...], k_ref[lo:hi, :]], axis=0)
                vcat = jnp.concatenate([vp_ref[...], v_ref[lo:hi, :]], axis=0)
            else:
                kcat = k_ref[lo - DIL_BAND:hi, :]
                vcat = v_ref[lo - DIL_BAND:hi, :]
            not_first = (((i * BANDS + j) % nb) != 0).astype(jnp.int32)
            s = jnp.where(_band_masks(not_first), _dot(q_ref[lo:hi, :], kcat, NT), NEG)
            m = jnp.max(s, axis=-1, keepdims=True)
            p = jnp.exp(s - m)
            den = jnp.sum(p, axis=-1, keepdims=True)
            o_ref[lo:hi, :] = _dot(_bf(p), vcat, NN) / den
            l_ref[lo:hi, :] = jnp.broadcast_to(m + jnp.log(den), (DIL_BAND, DIL_HD))

    return pl.pallas_call(
        body, name=name, grid=(steps,), in_specs=[main, main, main, prev, prev], out_specs=[main, main],
        out_shape=[jax.ShapeDtypeStruct((rows, DIL_HD), F32)] * 2, compiler_params=_params(("parallel",)),
    )(q, k, v, k, v)


def _dil_bwd(q, k, v, do, lse, dd, nb, name):
    rows = q.shape[0]
    blk = BANDS * DIL_BAND
    steps = rows // blk
    last_band = rows // DIL_BAND - 1
    main = pl.BlockSpec((blk, DIL_HD), lambda i: (i, 0))
    prev = pl.BlockSpec((DIL_BAND, DIL_HD), lambda i: (jnp.maximum(i * BANDS - 1, 0), 0))
    nxt = pl.BlockSpec((DIL_BAND, DIL_HD), lambda i: (jnp.minimum(i * BANDS + BANDS, last_band), 0))

    def body(q_ref, k_ref, v_ref, do_ref, l_ref, dd_ref, kp_ref, vp_ref, qn_ref, don_ref, ln_ref, ddn_ref,
             dq_ref, dk_ref, dv_ref, ak_ref, av_ref):
        i = pl.program_id(0)
        ak_ref[...] = jnp.zeros_like(ak_ref)
        av_ref[...] = jnp.zeros_like(av_ref)
        for j in range(BANDS + 1):
            lo, hi = j * DIL_BAND, (j + 1) * DIL_BAND
            if j == 0:
                kcat = jnp.concatenate([kp_ref[...], k_ref[lo:hi, :]], axis=0)
                vcat = jnp.concatenate([vp_ref[...], v_ref[lo:hi, :]], axis=0)
            elif j < BANDS:
                kcat = k_ref[lo - DIL_BAND:hi, :]
                vcat = v_ref[lo - DIL_BAND:hi, :]
            else:
                kcat = jnp.concatenate([k_ref[lo - DIL_BAND:lo, :], k_ref[lo - DIL_BAND:lo, :]], axis=0)
                vcat = jnp.concatenate([v_ref[lo - DIL_BAND:lo, :], v_ref[lo - DIL_BAND:lo, :]], axis=0)
            if j < BANDS:
                qj, doj, lj, ddj = q_ref[lo:hi, :], do_ref[lo:hi, :], l_ref[lo:hi, :], dd_ref[lo:hi, :]
            else:
                qj, doj, lj, ddj = qn_ref[...], don_ref[...], ln_ref[...], ddn_ref[...]
            not_first = (((i * BANDS + j) % nb) != 0).astype(jnp.int32)
            mask = _band_masks(not_first)
            if j == BANDS:
                cidx = lax.broadcasted_iota(jnp.int32, mask.shape, 1)
                mask = jnp.logical_and(mask, cidx < DIL_BAND)
            s = jnp.where(mask, _dot(qj, kcat, NT), NEG)
            p = jnp.exp(s - jnp.concatenate([lj, lj], axis=1))
            dp = _dot(doj, vcat, NT)
            ds = _bf(p * (dp - jnp.concatenate([ddj, ddj], axis=1)))
            if j < BANDS:
                dq_ref[lo:hi, :] = _dot(ds, kcat, NN)
            ak_ref[lo:hi + DIL_BAND, :] += _dot(ds, qj, TN)
            av_ref[lo:hi + DIL_BAND, :] += _dot(_bf(p), doj, TN)
        dk_ref[...] = ak_ref[DIL_BAND:DIL_BAND + blk, :]
        dv_ref[...] = av_ref[DIL_BAND:DIL_BAND + blk, :]

    return pl.pallas_call(
        body, name=name, grid=(steps,),
        in_specs=[main] * 6 + [prev, prev] + [nxt] * 4, out_specs=[main] * 3,
        out_shape=[jax.ShapeDtypeStruct((rows, DIL_HD), F32)] * 3,
        scratch_shapes=[pltpu.VMEM((blk + 2 * DIL_BAND, DIL_HD), F32)] * 2,
        compiler_params=_params(("parallel",)),
    )(q, k, v, do, lse, dd, k, v, q, do, lse, dd)


def _dil_merge(os_, ls_, tr=256):
    t, w = os_[0].shape
    blk = pl.BlockSpec((tr, w), lambda i: (i, 0))

    def body(o1, o2, o3, l1, l2, l3, ob_ref, of_ref, lt_ref):
        a, b, c = l1[...], l2[...], l3[...]
        m = jnp.maximum(jnp.maximum(a, b), c)
        ea, eb, ec = jnp.exp(a - m), jnp.exp(b - m), jnp.exp(c - m)
        den = ea + eb + ec
        o = (ea * o1[...] + eb * o2[...] + ec * o3[...]) / den
        ob_ref[...] = o.astype(BF16)
        of_ref[...] = o
        lt_ref[...] = m + jnp.log(den)

    return pl.pallas_call(
        body, name="dil_merge", grid=(t // tr,), in_specs=[blk] * 6, out_specs=[blk] * 3,
        out_shape=[jax.ShapeDtypeStruct((t, w), BF16), jax.ShapeDtypeStruct((t, w), F32),
                   jax.ShapeDtypeStruct((t, w), F32)],
        compiler_params=_params(("parallel",)),
    )(*os_, *ls_)


def _dil_bwd_prep(dmix, o_d, tr=256):
    t, w = o_d.shape
    blk = pl.BlockSpec((tr, w), lambda i: (i, 0))

    def body(do_ref, o_ref, dob_ref, dd_ref):
        do = do_ref[...]
        prod = do * o_ref[...]
        dob_ref[...] = do.astype(BF16)
        for h in range(DIL_HEADS):
            hs = slice(h * DIL_HD, (h + 1) * DIL_HD)
            dd_ref[:, hs] = jnp.broadcast_to(jnp.sum(prod[:, hs], axis=-1, keepdims=True), (tr, DIL_HD))

    return pl.pallas_call(
        body, name="dil_bwd_prep", grid=(t // tr,),
        in_specs=[pl.BlockSpec((tr, w), lambda i: (i, 1)), blk], out_specs=[blk, blk],
        out_shape=[jax.ShapeDtypeStruct((t, w), BF16), jax.ShapeDtypeStruct((t, w), F32)],
        compiler_params=_params(("parallel",)),
    )(dmix, o_d)


def _ca_fwd(q, memkv, tq=512):
    t, d = q.shape
    m = memkv.shape[0]
    scale = CA_HD ** -0.5

    def body(q_ref, k_ref, v_ref, o_ref):
        for h in range(CA_HEADS):
            hs = slice(h * CA_HD, (h + 1) * CA_HD)
            s = _dot(q_ref[:, hs], k_ref[:, hs], NT) * scale
            p = jnp.exp(s - jnp.max(s, axis=-1, keepdims=True))
            p = p / jnp.sum(p, axis=-1, keepdims=True)
            o_ref[:, hs] = _dot(_bf(p), v_ref[:, hs], NN).astype(BF16)

    return pl.pallas_call(
        body, name="ca_fwd", grid=(t // tq,),
        in_specs=[pl.BlockSpec((tq, d), lambda i: (i, 0)), pl.BlockSpec((m, d), lambda i: (0, 0)),
                  pl.BlockSpec((m, d), lambda i: (0, 1))],
        out_specs=pl.BlockSpec((tq, d), lambda i: (i, 0)),
        out_shape=jax.ShapeDtypeStruct((t, d), BF16), compiler_params=_params(("parallel",)),
    )(q, memkv, memkv)


def _ca_bwd(q, memkv, do, tq=512):
    t, d = q.shape
    m = memkv.shape[0]
    scale = CA_HD ** -0.5

    def body(q_ref, k_ref, v_ref, do_ref, dq_ref, dkv_ref):
        i = pl.program_id(0)

        @pl.when(i == 0)
        def _():
            dkv_ref[...] = jnp.zeros_like(dkv_ref)

        for h in range(CA_HEADS):
            hs = slice(h * CA_HD, (h + 1) * CA_HD)
            q_h, k_h, v_h, do_h = q_ref[:, hs], k_ref[:, hs], v_ref[:, hs], do_ref[:, hs]
            s = _dot(q_h, k_h, NT) * scale
            p = jnp.exp(s - jnp.max(s, axis=-1, keepdims=True))
            p = p / jnp.sum(p, axis=-1, keepdims=True)
            dp = _dot(do_h, v_h, NT)
            ds = _bf(p * (dp - jnp.sum(p * dp, axis=-1, keepdims=True)) * scale)
            dq_ref[:, hs] = _dot(ds, k_h, NN).astype(BF16)
            dkv_ref[:, hs] += _dot(ds, q_h, TN)
            dkv_ref[:, d + h * CA_HD: d + (h + 1) * CA_HD] += _dot(_bf(p), do_h, TN)

    return pl.pallas_call(
        body, name="ca_bwd", grid=(t // tq,),
        in_specs=[pl.BlockSpec((tq, d), lambda i: (i, 0)), pl.BlockSpec((m, d), lambda i: (0, 0)),
                  pl.BlockSpec((m, d), lambda i: (0, 1)), pl.BlockSpec((tq, d), lambda i: (i, 0))],
        out_specs=[pl.BlockSpec((tq, d), lambda i: (i, 0)), pl.BlockSpec((m, 2 * d), lambda i: (0, 0))],
        out_shape=[jax.ShapeDtypeStruct((t, d), BF16), jax.ShapeDtypeStruct((m, 2 * d), F32)],
        compiler_params=_params(("arbitrary",)),
    )(q, memkv, memkv, do)


STRIP = 256


def _shift_down(u, n, row):
    return jnp.where(row >= n, pltpu.roll(u, n, 0), 0.0)


def _shift_up(u, n, row):
    t = u.shape[0]
    return jnp.where(row < t - n, pltpu.roll(u, t - n, 0), 0.0)


def _conv(u, cw_ref, row):
    return ((cw_ref[3:4, :] + cw_ref[0:1, :] * _shift_down(u, 2, row)) + cw_ref[1:2, :] * _shift_down(u, 1, row)) \
        + cw_ref[2:3, :] * u


def _swiglu_fwd(ug0, uu0, cwg, cwu):
    t, w = ug0.shape
    col = pl.BlockSpec((t, STRIP), lambda j: (0, j))
    cws = pl.BlockSpec((SUBLANES, STRIP), lambda j: (0, j))

    def body(g_ref, u_ref, cg_ref, cu_ref, a_ref):
        row = lax.broadcasted_iota(jnp.int32, (t, STRIP), 0)
        gate = _conv(g_ref[...], cg_ref, row)
        up = _conv(u_ref[...], cu_ref, row)
        a_ref[...] = (gate * _sigmoid(gate) * up).astype(BF16)

    return pl.pallas_call(
        body, name="swiglu_fwd", grid=(w // STRIP,), in_specs=[col, col, cws, cws], out_specs=col,
        out_shape=jax.ShapeDtypeStruct((t, w), BF16), compiler_params=_params(("parallel",)),
    )(ug0, uu0, cwg, cwu)


def _swiglu_bwd(ug0, uu0, cwg, cwu, da):
    t, w = ug0.shape
    col = pl.BlockSpec((t, STRIP), lambda j: (0, j))
    cws = pl.BlockSpec((SUBLANES, STRIP), lambda j: (0, j))

    def conv_bwd(du, u0, cw_ref, row, du0_ref, dcw_ref):
        du0 = (cw_ref[2:3, :] * du + cw_ref[1:2, :] * _shift_up(du, 1, row)) + cw_ref[0:1, :] * _shift_up(du, 2, row)
        du0_ref[...] = du0.astype(BF16)
        dcw_ref[0:1, :] = jnp.sum(du * _shift_down(u0, 2, row), axis=0, keepdims=True)
        dcw_ref[1:2, :] = jnp.sum(du * _shift_down(u0, 1, row), axis=0, keepdims=True)
        dcw_ref[2:3, :] = jnp.sum(du * u0, axis=0, keepdims=True)
        dcw_ref[3:4, :] = jnp.sum(du, axis=0, keepdims=True)
        dcw_ref[4:8, :] = jnp.zeros((4, STRIP), F32)

    def body(g_ref, u_ref, cg_ref, cu_ref, da_ref, dg0_ref, du0_ref, dcg_ref, dcu_ref):
        row = lax.broadcasted_iota(jnp.int32, (t, STRIP), 0)
        g0, u0 = g_ref[...], u_ref[...]
        gate = _conv(g0, cg_ref, row)
        up = _conv(u0, cu_ref, row)
        sg = _sigmoid(gate)
        da = da_ref[...]
        dgate = da * up * (sg * (1.0 + gate * (1.0 - sg)))
        dup = da * (gate * sg)
        conv_bwd(dgate, g0, cg_ref, row, dg0_ref, dcg_ref)
        conv_bwd(dup, u0, cu_ref, row, du0_ref, dcu_ref)

    return pl.pallas_call(
        body, name="swiglu_bwd", grid=(w // STRIP,), in_specs=[col, col, cws, cws, col],
        out_specs=[col, col, cws, cws],
        out_shape=[jax.ShapeDtypeStruct((t, w), BF16), jax.ShapeDtypeStruct((t, w), BF16),
                   jax.ShapeDtypeStruct((SUBLANES, w), F32), jax.ShapeDtypeStruct((SUBLANES, w), F32)],
        compiler_params=_params(("parallel",)),
    )(ug0, uu0, cwg, cwu, da)


def _tile2d(r, c, limit=1 << 20):
    tr, tc = r, c
    while tr * tc * 4 > limit:
        if tr % (2 * SUBLANES) == 0:
            tr //= 2
        elif tc % (2 * LANES) == 0:
            tc //= 2
        else:
            break
    return tr, tc


def _adamw_math(w, m, v, g):
    c1 = 1.0 - ADAM_B1 ** ADAM_STEP
    c2 = 1.0 - ADAM_B2 ** ADAM_STEP
    mm = ADAM_B1 * m + (1.0 - ADAM_B1) * g
    vv = ADAM_B2 * v + (1.0 - ADAM_B2) * (g * g)
    delta = -ADAM_LR * ((mm / c1) / (jnp.sqrt(vv / c2) + ADAM_EPS) + ADAM_WD * w)
    return delta, mm, vv


def _adamw(w, m, v, g, name):
    r, c = w.shape
    blk = pl.BlockSpec((r, c), lambda i: (0, 0))

    def body(w_ref, m_ref, v_ref, gi_ref, g_ref, d_ref, nm_ref, nv_ref):
        g = gi_ref[...]
        d_ref[...], nm_ref[...], nv_ref[...] = _adamw_math(w_ref[...], m_ref[...], v_ref[...], g)
        g_ref[...] = g

    return pl.pallas_call(body, name=name, grid=(1,), in_specs=[blk] * 4, out_specs=[blk] * 4,
                          out_shape=[jax.ShapeDtypeStruct((r, c), F32)] * 4,
                          compiler_params=_params(("arbitrary",)))(w, m, v, g)


def _pair_add(gs, ra, core, name):
    _, _, r, c = gs.shape
    tr, tc = _tile2d(r, c)
    blk = pl.BlockSpec((None, tr, tc), lambda k, i, j, s: (k, i, j))

    def body(s_ref, g_ref, r_ref, o_ref, ob_ref):
        p = g_ref[...] + r_ref[...]
        o_ref[...] = p
        ob_ref[...] = p.astype(BF16)

    gspec = pltpu.PrefetchScalarGridSpec(
        num_scalar_prefetch=1, grid=(4, r // tr, c // tc),
        in_specs=[pl.BlockSpec((None, None, tr, tc), lambda k, i, j, s: (k, s[0], i, j)), blk], out_specs=[blk, blk])
    return pl.pallas_call(body, name=name, grid_spec=gspec,
                          out_shape=[jax.ShapeDtypeStruct((4, r, c), F32), jax.ShapeDtypeStruct((4, r, c), BF16)],
                          compiler_params=_params(("parallel", "parallel", "parallel")))(core, gs, ra)


def _small_reduce(gathered):
    nd, r, n = gathered.shape
    tn = 2048 if n % 2048 == 0 else n
    def body(g_ref, s_ref, t_ref):
        s = g_ref[0]
        for dv in range(1, nd):
            s = s + g_ref[dv]
        s_ref[...] = s
        t_ref[...] = jnp.broadcast_to(jnp.sum(s, axis=0, keepdims=True), (r, tn))

    return pl.pallas_call(
        body, name="small_reduce", grid=(n // tn,),
        in_specs=[pl.BlockSpec((nd, r, tn), lambda j: (0, 0, j))],
        out_specs=[pl.BlockSpec((r, tn), lambda j: (0, j))] * 2,
        out_shape=[jax.ShapeDtypeStruct((r, n), F32)] * 2, compiler_params=_params(("parallel",)),
    )(gathered)


HBM = pl.BlockSpec(memory_space=pltpu.HBM)


def _all_gather(arrs, name):
    n = len(arrs)

    def body(*refs):
        ins, outs = refs[:n], refs[n:2 * n]
        send, recv, lsem = refs[2 * n:]
        x, y, c = lax.axis_index("x"), lax.axis_index("y"), lax.axis_index("c")
        me, sib = (x, y, c), (x, y, 1 - c)
        chips = [(1 - x, y), (x, 1 - y), (1 - x, 1 - y)]

        def slot(w, p):
            return outs[w].at[4 * p[0] + 2 * p[1] + p[2]]

        def cp(w, k, block, to, src=None):
            return pltpu.make_async_remote_copy(
                src_ref=slot(w, block) if src is None else src, dst_ref=slot(w, block),
                send_sem=send.at[w * 7 + k], recv_sem=recv.at[w * 7 + k], device_id=to, device_id_type=MESH)

        mine = [pltpu.make_async_copy(ins[w], slot(w, me), lsem.at[w]) for w in range(n)]
        for m in mine:
            m.start()
        first = []
        for w in range(n):
            first.append(cp(w, 0, me, sib, src=ins[w]))
            first += [cp(w, 1 + j, me, (*chip, c), src=ins[w]) for j, chip in enumerate(chips)]
        for f in first:
            f.start()
        passed = []
        for j, chip in enumerate(chips):
            for w in range(n):
                cp(w, 1 + j, (*chip, c), me).wait_recv()
                fwd = cp(w, 4 + j, (*chip, c), sib)
                fwd.start()
                passed.append(fwd)
        for w in range(n):
            cp(w, 0, sib, me).wait_recv()
            for j, chip in enumerate(chips):
                cp(w, 4 + j, (*chip, 1 - c), me).wait_recv()
        for f in first + passed:
            f.wait_send()
        for m in mine:
            m.wait()

    return pl.pallas_call(
        body, name=name, in_specs=[HBM] * n, out_specs=[HBM] * n,
        out_shape=[jax.ShapeDtypeStruct((8,) + a.shape, a.dtype) for a in arrs],
        scratch_shapes=[pltpu.SemaphoreType.DMA((7 * n,)), pltpu.SemaphoreType.DMA((7 * n,)),
                        pltpu.SemaphoreType.DMA((n,))],
    )(*arrs)


def _sibling_exchange(arrs, name):
    n = len(arrs)

    def body(*refs):
        ins, outs = refs[:n], refs[n:2 * n]
        send, recv = refs[2 * n:]
        x, y, c = lax.axis_index("x"), lax.axis_index("y"), lax.axis_index("c")
        copies = [pltpu.make_async_remote_copy(
            src_ref=ins[w].at[:, 1 - c], dst_ref=outs[w], send_sem=send.at[w], recv_sem=recv.at[w],
            device_id=(x, y, 1 - c), device_id_type=MESH) for w in range(n)]
        for cpy in copies:
            cpy.start()
        for cpy in copies:
            cpy.wait()

    return pl.pallas_call(
        body, name=name, in_specs=[HBM] * n, out_specs=[HBM] * n,
        out_shape=[jax.ShapeDtypeStruct((a.shape[0],) + a.shape[2:], a.dtype) for a in arrs],
        scratch_shapes=[pltpu.SemaphoreType.DMA((n,)), pltpu.SemaphoreType.DMA((n,))],
    )(*arrs)


def _chip_exchange(arrs, name):
    n = len(arrs)

    def body(*refs):
        ins, outs = refs[:n], refs[n:2 * n]
        send, recv = refs[2 * n:]
        x, y, c = lax.axis_index("x"), lax.axis_index("y"), lax.axis_index("c")
        chips = [(1 - x, y), (x, 1 - y), (1 - x, 1 - y)]
        copies = []
        for w in range(n):
            for j, (cx, cy) in enumerate(chips):
                copies.append(pltpu.make_async_remote_copy(
                    src_ref=ins[w].at[2 * cx + cy], dst_ref=outs[w].at[j], send_sem=send.at[3 * w + j],
                    recv_sem=recv.at[3 * w + j], device_id=(cx, cy, c), device_id_type=MESH))
        for cpy in copies:
            cpy.start()
        for cpy in copies:
            cpy.wait()

    return pl.pallas_call(
        body, name=name, in_specs=[HBM] * n, out_specs=[HBM] * n,
        out_shape=[jax.ShapeDtypeStruct((3,) + a.shape[1:], a.dtype) for a in arrs],
        scratch_shapes=[pltpu.SemaphoreType.DMA((3 * n,)), pltpu.SemaphoreType.DMA((3 * n,))],
    )(*arrs)


def _pad_cols(a, to):
    return jnp.pad(a, ((0, 0), (0, to - a.shape[1])))


N_GLR = GLA_W + GLA_RANK
FF_SLAB = D_FF // 4
FF_SLAB_P = FFP // 4


def _prepare_sub1(gath):
    w_in = _gathered_full("w_in", gath["w_in"])
    w2 = gath["gla_gate_w2"].transpose(1, 0, 2).reshape(GLA_RANK, -1)
    return {"w_a": _pad_cols(w_in[:, :N_GLR], HA_W), "w_b": w_in[:, N_GLR:],
            "w2p": jnp.pad(w2, ((0, LANES - GLA_RANK), (0, 0)))}


def _prepare_ffn(gath, conv_b):
    padc = FF_SLAB_P - FF_SLAB
    f = jnp.pad(gath["ffn_w_in"], ((0, 0), (0, 0), (0, padc)))
    wg = f[:4].transpose(1, 0, 2).reshape(f.shape[1], FFP)
    wu = f[4:].transpose(1, 0, 2).reshape(f.shape[1], FFP)
    wo = jnp.pad(gath["ffn_w_out"].reshape(4, FF_SLAB, -1), ((0, 0), (0, padc), (0, 0))).reshape(FFP, -1)
    cw = jnp.pad(gath["ffn_conv_w"], ((0, 0), (0, 0), (0, padc)))
    cb = jnp.pad(conv_b.reshape(8, FF_SLAB), ((0, 0), (0, padc)))
    zrow = jnp.zeros((4, FFP), F32)
    cwg = jnp.concatenate([cw[:4].transpose(1, 0, 2).reshape(3, FFP), cb[:4].reshape(1, FFP), zrow], axis=0)
    cwu = jnp.concatenate([cw[4:].transpose(1, 0, 2).reshape(3, FFP), cb[4:].reshape(1, FFP), zrow], axis=0)
    return {"wg": wg, "wu": wu, "wo": wo, "cwg": cwg, "cwu": cwu}


def _unpad_ff(a):
    r = a.shape[0]
    return a.reshape(r, 4, FF_SLAB_P)[:, :, :FF_SLAB].reshape(r, D_FF)


def _grad_slabs(g):
    s = {"w_in": _to_slabs("w_in", jnp.concatenate([g["w_a"][:, :N_GLR], g["w_b"]], axis=1))}
    for n in ("w_out", "ca_wq", "ca_wkv", "ca_wo"):
        s[n] = _to_slabs(n, g[n])
    d = g["wg"].shape[0]
    halves = [h.reshape(d, 4, FF_SLAB_P)[:, :, :FF_SLAB].transpose(1, 0, 2) for h in (g["wg"], g["wu"])]
    s["ffn_w_in"] = jnp.concatenate(halves, axis=0).reshape(4, 2, d, FF_SLAB)
    wo = g["wo"].reshape(4, FF_SLAB_P, -1)[:, :FF_SLAB]
    s["ffn_w_out"] = wo.reshape(4, 2, FF_SLAB // 2, wo.shape[-1])
    return s


def _local_step(x, mem, positions, target, p, small):
    t, d = x.shape
    w_a, w_b, w2p = p["w_a"], p["w_b"], p["w2p"]
    wg, wu, wo, cwg, cwu = p["wg"], p["wu"], p["wo"], p["cwg"], p["cwu"]
    wts = p
    tabs = _rope_tables(positions)
    xb = x.astype(BF16)
    memb = mem.astype(BF16)

    h_a = _matmul(xb, w_a, "nn", F32, 512, 640, d, "mm_h_a")
    h_b = _matmul(xb, w_b, "nn", F32, 512, 1024, d, "mm_h_b")
    o_g, o_raw, s_before = _gla_fwd(h_a, w2p, small["gla_gate_b"], small["gla_norm_g"])
    qr, kr, vr = _rope_fwd(h_b, tabs)
    branch_qkv, o_tok, l_tok = [], [], []
    for bi, dil in enumerate(DIL_DILATIONS):
        qb_, kb_, vb_ = _to_branch(qr, dil), _to_branch(kr, dil), _to_branch(vr, dil)
        nb = t // dil // DIL_BAND
        o_b, l_b = _dil_fwd(qb_, kb_, vb_, nb, f"dil_fwd{bi}")
        branch_qkv.append((qb_, kb_, vb_, nb))
        o_tok.append(_from_branch(o_b, dil, t))
        l_tok.append(_from_branch(l_b, dil, t))
    o_d_b, o_d, lse_tot = _dil_merge(o_tok, l_tok)
    mixin = jnp.concatenate([o_g, o_d_b], axis=1)
    mix = _matmul(mixin, wts["w_out"], "nn", F32, 512, 1024, d, "mm_mix")
    x1, x1b = _ln_fwd(x, mix, small["ln1_g"], small["ln1_b"], "ln1_fwd")

    q_ca = _matmul(x1b, wts["ca_wq"], "nn", BF16, 512, 1024, d, "mm_caq")
    memkv = _matmul(memb, wts["ca_wkv"], "nn", BF16, mem.shape[0], 1024, d, "mm_memkv")
    o_c = _ca_fwd(q_ca, memkv)
    ca_out = _matmul(o_c, wts["ca_wo"], "nn", F32, 512, 1024, d, "mm_cao")
    x2, x2b = _ln_fwd(x1, ca_out, small["ln2_g"], small["ln2_b"], "ln2_fwd")

    ug0 = _matmul(x2b, wg, "nn", F32, 512, 512, d, "mm_ug")
    uu0 = _matmul(x2b, wu, "nn", F32, 512, 512, d, "mm_uu")
    act = _swiglu_fwd(ug0, uu0, cwg, cwu)
    ffn = _matmul(act, wo, "nn", F32, 512, 512, FFP, "mm_ffn")

    dp3, dp3b, dg3, db3, loss_part = _ln_bwd(x2, ffn, small["ln3_g"], small["ln3_b"], target, True, "ln3_bwd")
    g_wo = _matmul(act.T, dp3b, "nn", F32, 512, 1024, t // 2, "mm_g_wo")
    dact = _matmul(dp3b, wo, "nt", F32, 512, 512, d, "mm_dact")
    dug, duu, dcwg, dcwu = _swiglu_bwd(ug0, uu0, cwg, cwu, dact)
    x2t = x2b.T
    g_wg = _matmul(x2t, dug, "nn", F32, 512, 512, t // 2, "mm_g_wg")
    g_wu = _matmul(x2t, duu, "nn", F32, 512, 512, t // 2, "mm_g_wu")
    dx2 = _matmul(dug, wg, "nt", F32, 512, 512, FFP // 2, "mm_dx2_g", resid=dp3, resid_scale=ALPHA)
    dx2 = _matmul(duu, wu, "nt", F32, 512, 512, FFP // 2, "mm_dx2_u", resid=dx2)

    dp2, dp2b, dg2, db2 = _ln_bwd(x1, ca_out, small["ln2_g"], small["ln2_b"], dx2, False, "ln2_bwd")
    g_cao = _matmul(o_c.T, dp2b, "nn", F32, 512, 1024, t // 2, "mm_g_cao")
    do_c = _matmul(dp2b, wts["ca_wo"], "nt", BF16, 512, 1024, d, "mm_do_c")
    dq_ca, dmemkv = _ca_bwd(q_ca, memkv, do_c)
    g_caq = _matmul(x1b.T, dq_ca, "nn", F32, 512, 1024, t // 2, "mm_g_caq")
    g_cakv = _matmul(memb.T, dmemkv.astype(BF16), "nn", F32, 512, 1024, mem.shape[0], "mm_g_cakv")
    dx1 = _matmul(dq_ca, wts["ca_wq"], "nt", F32, 512, 1024, d, "mm_dx1", resid=dp2, resid_scale=ALPHA)

    dp1, dp1b, dg1, db1 = _ln_bwd(x, mix, small["ln1_g"], small["ln1_b"], dx1, False, "ln1_bwd")
    g_wout = _matmul(mixin.T, dp1b, "nn", F32, 512, 1024, t // 2, "mm_g_wout")
    dmix = _matmul(dp1b, wts["w_out"], "nt", F32, 512, 1024, d, "mm_dmix")
    dh_a, dw2, dgate_b, dnorm_g = _gla_bwd(h_a, w2p, small["gla_gate_b"], small["gla_norm_g"], o_raw, s_before, dmix)
    do_d, dd = _dil_bwd_prep(dmix, o_d)
    dqs, dks, dvs = [], [], []
    for bi, dil in enumerate(DIL_DILATIONS):
        qb_, kb_, vb_, nb = branch_qkv[bi]
        dq_b, dk_b, dv_b = _dil_bwd(qb_, kb_, vb_, _to_branch(do_d, dil), _to_branch(lse_tot, dil),
                                    _to_branch(dd, dil), nb, f"dil_bwd{bi}")
        dqs.append(_from_branch(dq_b, dil, t))
        dks.append(_from_branch(dk_b, dil, t))
        dvs.append(_from_branch(dv_b, dil, t))
    dh_b = _dil_dh(dqs, dks, dvs, tabs)
    xt = xb.T
    g_wa = _matmul(xt, dh_a, "nn", F32, 512, 640, t // 2, "mm_g_wa")
    g_wb = _matmul(xt, dh_b, "nn", F32, 512, 1024, t // 2, "mm_g_wb")
    dx = _matmul(dh_a, w_a, "nt", F32, 512, 512, HA_W, "mm_dx_a", resid=dp1, resid_scale=ALPHA)
    dx = _matmul(dh_b, w_b, "nt", F32, 512, 512, HB_W, "mm_dx_b", resid=dx)

    grads = {"w_a": g_wa, "w_b": g_wb, "w_out": g_wout, "ca_wq": g_caq, "ca_wkv": g_cakv, "ca_wo": g_cao,
             "wg": g_wg, "wu": g_wu, "wo": g_wo}
    small_parts = {
        "gla_gate_b": dgate_b, "gla_norm_g": dnorm_g, "ln1_g": dg1, "ln1_b": db1, "ln2_g": dg2, "ln2_b": db2,
        "ln3_g": dg3, "ln3_b": db3,
        "conv": jnp.concatenate([_unpad_ff(dcwg), _unpad_ff(dcwu)], axis=1),
        "gla_gate_w2": dw2[:GLA_RANK],
    }
    return loss_part, dx, grads, small_parts


BIG = ("w_in", "w_out", "ca_wq", "ca_wkv", "ca_wo", "ffn_w_in", "ffn_w_out")
COL_SHARDED = ("w_in", "ca_wkv", "ffn_w_in")
SMALL_ORDER = ("gla_gate_b", "gla_norm_g", "ln1_g", "ln1_b", "ln2_g", "ln2_b", "ln3_g", "ln3_b")


def _gathered_full(name, g):
    if name in COL_SHARDED:
        return g.transpose(1, 0, 2).reshape(g.shape[1], 8 * g.shape[2])
    return g.reshape(8 * g.shape[1], g.shape[2])


def _to_slabs(name, full):
    if name in COL_SHARDED:
        r, cc = full.shape
        s = full.reshape(r, 8, cc // 8).transpose(1, 0, 2)
    else:
        rr, c = full.shape
        s = full.reshape(8, rr // 8, c)
    return s.reshape((4, 2) + s.shape[1:])


def kernel(x, mem, positions, w_in, gla_gate_w2, gla_gate_b, gla_norm_g, w_out, ln1_g, ln1_b, ca_wq, ca_wkv, ca_wo, ln2_g, ln2_b, ffn_w_in, ffn_conv_w, ffn_conv_b, ffn_w_out, ln3_g, ln3_b, loss_target, m_w_in, m_gla_gate_w2, m_gla_gate_b, m_gla_norm_g, m_w_out, m_ln1_g, m_ln1_b, m_ca_wq, m_ca_wkv, m_ca_wo, m_ln2_g, m_ln2_b, m_ffn_w_in, m_ffn_conv_w, m_ffn_conv_b, m_ffn_w_out, m_ln3_g, m_ln3_b, v_w_in, v_gla_gate_w2, v_gla_gate_b, v_gla_norm_g, v_w_out, v_ln1_g, v_ln1_b, v_ca_wq, v_ca_wkv, v_ca_wo, v_ln2_g, v_ln2_b, v_ffn_w_in, v_ffn_conv_w, v_ffn_conv_b, v_ffn_w_out, v_ln3_g, v_ln3_b):
    weights = dict(w_in=w_in, gla_gate_w2=gla_gate_w2, gla_gate_b=gla_gate_b, gla_norm_g=gla_norm_g, w_out=w_out,
                   ln1_g=ln1_g, ln1_b=ln1_b, ca_wq=ca_wq, ca_wkv=ca_wkv, ca_wo=ca_wo, ln2_g=ln2_g, ln2_b=ln2_b,
                   ffn_w_in=ffn_w_in, ffn_conv_w=ffn_conv_w, ffn_conv_b=ffn_conv_b, ffn_w_out=ffn_w_out,
                   ln3_g=ln3_g, ln3_b=ln3_b)
    moms = dict(w_in=(m_w_in, v_w_in), gla_gate_w2=(m_gla_gate_w2, v_gla_gate_w2), gla_gate_b=(m_gla_gate_b, v_gla_gate_b),
                gla_norm_g=(m_gla_norm_g, v_gla_norm_g), w_out=(m_w_out, v_w_out), ln1_g=(m_ln1_g, v_ln1_g),
                ln1_b=(m_ln1_b, v_ln1_b), ca_wq=(m_ca_wq, v_ca_wq), ca_wkv=(m_ca_wkv, v_ca_wkv), ca_wo=(m_ca_wo, v_ca_wo),
                ln2_g=(m_ln2_g, v_ln2_g), ln2_b=(m_ln2_b, v_ln2_b), ffn_w_in=(m_ffn_w_in, v_ffn_w_in),
                ffn_conv_w=(m_ffn_conv_w, v_ffn_conv_w), ffn_conv_b=(m_ffn_conv_b, v_ffn_conv_b),
                ffn_w_out=(m_ffn_w_out, v_ffn_w_out), ln3_g=(m_ln3_g, v_ln3_g), ln3_b=(m_ln3_b, v_ln3_b))
    order = list(weights)
    xi, yi, ci = lax.axis_index("x"), lax.axis_index("y"), lax.axis_index("c")
    me = 4 * xi + 2 * yi + ci

    shards = [weights[n][0].astype(BF16) for n in BIG] + [gla_gate_w2[0].astype(BF16), ffn_conv_w[0]]
    gathered = _all_gather(shards, "ag_weights")
    gath = dict(zip(BIG + ("gla_gate_w2", "ffn_conv_w"), gathered))
    p = _prepare_sub1(gath)
    p.update({n: _gathered_full(n, gath[n]) for n in ("w_out", "ca_wq", "ca_wkv", "ca_wo")})
    p.update(_prepare_ffn(gath, ffn_conv_b))
    small = dict(gla_gate_b=gla_gate_b, gla_norm_g=gla_norm_g, ln1_g=ln1_g, ln1_b=ln1_b, ln2_g=ln2_g, ln2_b=ln2_b,
                 ln3_g=ln3_g, ln3_b=ln3_b)

    loss_part, dx, grads, small_parts = _local_step(x[0], mem[0], positions[0], loss_target[0], p, small)
    loss = lax.psum(jnp.sum(loss_part), ("x", "y", "c"))

    slab_of = _grad_slabs(grads)
    slabs = [slab_of[n] for n in BIG]
    from_sib = _sibling_exchange(slabs, "rs_sibling")
    core = ci.reshape(1).astype(jnp.int32)
    pair32, pair16 = [], []
    for n, s, r in zip(BIG, slabs, from_sib):
        p32, p16 = _pair_add(s, r, core, f"pair_add_{n}")
        pair32.append(p32)
        pair16.append(p16)
    from_chips = _chip_exchange(pair16, "rs_chips")
    chip = (2 * xi + yi).reshape(1).astype(jnp.int32)
    out = {}
    for n, p32, rc in zip(BIG, pair32, from_chips):
        m_, v_ = moms[n]
        out[n] = _adamw_big(weights[n][0], m_[0], v_[0], p32, rc, chip, f"adamw_{n}")

    packed = jnp.concatenate([small_parts[n] for n in SMALL_ORDER] + [small_parts["conv"],
                             small_parts["gla_gate_w2"].reshape(SUBLANES, -1)], axis=1)
    pad = (-packed.shape[1]) % 2048
    packed = jnp.pad(packed, ((0, 0), (0, pad)))
    (allp,) = _all_gather([packed], "ag_small")
    dev_sum, row_sum = _small_reduce(allp)
    off = 0
    for n in SMALL_ORDER:
        width = weights[n].shape[1]
        g = row_sum[0:1, off:off + width]
        off += width
        m_, v_ = moms[n]
        out[n] = _adamw(weights[n], m_, v_, g, f"adamw_{n}")
    conv_g = dev_sum[:, off:off + 2 * D_FF]
    off += 2 * D_FF
    g_cb = conv_g[3:4]
    out["ffn_conv_b"] = _adamw(ffn_conv_b, m_ffn_conv_b, v_ffn_conv_b, g_cb, "adamw_ffn_conv_b")
    wsh = ffn_conv_w.shape[2]
    g_cw = lax.dynamic_slice_in_dim(conv_g[0:3], me * wsh, wsh, axis=1)
    out["ffn_conv_w"] = _adamw(ffn_conv_w[0], m_ffn_conv_w[0], v_ffn_conv_w[0], g_cw, "adamw_ffn_conv_w")
    w2_g = dev_sum[:, off:off + GLA_RANK * GLA_HEADS * GLA_DK // SUBLANES].reshape(GLA_RANK, GLA_HEADS * GLA_DK)
    wsh2 = gla_gate_w2.shape[2]
    g_w2 = lax.dynamic_slice_in_dim(w2_g, me * wsh2, wsh2, axis=1)
    out["gla_gate_w2"] = _adamw(gla_gate_w2[0], m_gla_gate_w2[0], v_gla_gate_w2[0], g_w2, "adamw_gla_gate_w2")

    def shaped(n, a):
        return a.reshape(weights[n].shape)

    res = [loss, dx[None]]
    for k in range(4):
        res += [shaped(n, out[n][k]) for n in order]
    return tuple(res)


def _adamw_big(w, m, v, p32, rc, chip, name):
    r, c = w.shape
    tr, tc = _tile2d(r, c)
    blk = pl.BlockSpec((tr, tc), lambda i, j, s: (i, j))
    own = pl.BlockSpec((None, tr, tc), lambda i, j, s: (s[0], i, j))
    others = [pl.BlockSpec((None, tr, tc), lambda i, j, s, k=k: (k, i, j)) for k in range(3)]

    def body(s_ref, w_ref, m_ref, v_ref, p_ref, r0_ref, r1_ref, r2_ref, g_ref, d_ref, nm_ref, nv_ref):
        g = ((p_ref[...] + r0_ref[...].astype(F32)) + r1_ref[...].astype(F32)) + r2_ref[...].astype(F32)
        d_ref[...], nm_ref[...], nv_ref[...] = _adamw_math(w_ref[...], m_ref[...], v_ref[...], g)
        g_ref[...] = g

    gs = pltpu.PrefetchScalarGridSpec(num_scalar_prefetch=1, grid=(r // tr, c // tc),
                                      in_specs=[blk, blk, blk, own] + others, out_specs=[blk] * 4)
    return pl.pallas_call(body, name=name, grid_spec=gs, out_shape=[jax.ShapeDtypeStruct((r, c), F32)] * 4,
                          compiler_params=_params(("parallel", "parallel")))(chip, w, m, v, p32, rc, rc, rc)
```

```python
import functools
import math

import jax
import jax.numpy as jnp
from jax import lax
from jax.experimental import pallas as pl
from jax.experimental.pallas import tpu as pltpu

F32 = jnp.float32
BF16 = jnp.bfloat16
MESH = pl.DeviceIdType.MESH

D_MODEL = 2048
LN_EPS = 1e-5
GLA_HEADS = 4
GLA_DV = 256
GLA_DK = 128
GLA_RANK = 16
GLA_TAU = 16.0
GLA_CHUNK = 64
DIL_HD = 128
DIL_HEADS = 8
DIL_BAND = 128
DIL_DILATIONS = (1, 4, 16)
ROPE_THETA = 500000.0
ROPE_DIMS = 32
CA_HEADS = 4
CA_HD = 512
D_FF = 5504
ALPHA = 2.0 ** 0.25
ADAM_LR = 0.001
ADAM_B1 = 0.9
ADAM_B2 = 0.999
ADAM_EPS = 1e-08
ADAM_WD = 0.01
ADAM_STEP = 10

LANES = 128
SUBLANES = 8
VMEM_LIMIT = 56 * 1024 * 1024

GLA_W = 2 * GLA_HEADS * GLA_DK + 2 * GLA_HEADS * GLA_DV
HA_W = GLA_W + LANES
HB_W = 3 * DIL_HEADS * DIL_HD
FFP = 5632
NEG = -1e30


def _params(sem):
    return pltpu.CompilerParams(dimension_semantics=sem, vmem_limit_bytes=VMEM_LIMIT)


def _sigmoid(x):
    return 1.0 / (1.0 + jnp.exp(-x))


def _dot(a, b, dn, precision=None):
    return lax.dot_general(a, b, (dn, ((), ())), preferred_element_type=F32, precision=precision)


NN = ((1,), (0,))
NT = ((1,), (1,))
TN = ((0,), (0,))


def _bf(v):
    return v if v.dtype == BF16 else v.astype(BF16)


def _matmul(a, b, kind, out_dtype, tm, tn, tk, name, resid=None, resid_scale=1.0, b_k_off=0):
    if kind == "nn":
        (m, k), (k2, n) = a.shape, b.shape
    elif kind == "nt":
        (m, k), (n, k2) = a.shape, b.shape
        k2 = k if b_k_off or k2 > k else k2
    else:
        (k, m), (k2, n) = a.shape, b.shape
    assert k == k2 and m % tm == 0 and n % tn == 0 and k % tk == 0, (name, a.shape, b.shape, tm, tn, tk)
    nk = k // tk
    dn = {"nn": NN, "nt": NT, "tn": TN}[kind]
    a_spec = pl.BlockSpec((tk, tm), lambda i, j, kk: (kk, i)) if kind == "tn" else pl.BlockSpec((tm, tk), lambda i, j, kk: (i, kk))
    b_spec = pl.BlockSpec((tn, tk), lambda i, j, kk: (j, kk + b_k_off)) if kind == "nt" else pl.BlockSpec((tk, tn), lambda i, j, kk: (kk, j))
    o_spec = pl.BlockSpec((tm, tn), lambda i, j, kk: (i, j))
    has_resid = resid is not None

    def body(*refs):
        if has_resid:
            a_ref, b_ref, r_ref, o_ref = refs[:4]
        else:
            a_ref, b_ref, o_ref = refs[:3]
            r_ref = None
        part = _dot(_bf(a_ref[...]), _bf(b_ref[...]), dn)

        def finish(acc):
            if has_resid:
                acc = acc + resid_scale * r_ref[...].astype(F32)
            o_ref[...] = acc.astype(out_dtype)

        if nk == 1:
            finish(part)
        else:
            acc_ref = refs[-1]
            kk = pl.program_id(2)

            @pl.when(kk == 0)
            def _():
                acc_ref[...] = part

            @pl.when(kk > 0)
            def _():
                acc_ref[...] += part

            @pl.when(kk == nk - 1)
            def _():
                finish(acc_ref[...])

    in_specs = [a_spec, b_spec] + ([o_spec] if has_resid else [])
    args = (a, b) + ((resid,) if has_resid else ())
    return pl.pallas_call(
        body, name=name, out_shape=jax.ShapeDtypeStruct((m, n), out_dtype),
        grid=(m // tm, n // tn, nk), in_specs=in_specs, out_specs=o_spec,
        scratch_shapes=[pltpu.VMEM((tm, tn), F32)] if nk > 1 else [],
        compiler_params=_params(("parallel", "parallel", "arbitrary")),
    )(*args)


def _ln_core(xres, f):
    p = ALPHA * xres + f
    mu = jnp.mean(p, axis=-1, keepdims=True)
    xc = p - mu
    var = jnp.mean(xc * xc, axis=-1, keepdims=True)
    rstd = lax.rsqrt(var + LN_EPS)
    return xc * rstd, rstd


def _rows8(v):
    r, c = v.shape
    return jnp.sum(v.reshape(r // SUBLANES, SUBLANES, c), axis=0)


def _ln_fwd(xres, f, g, b, name, tr=256):
    t, d = xres.shape
    row = pl.BlockSpec((tr, d), lambda i: (i, 0))
    vec = pl.BlockSpec((1, d), lambda i: (0, 0))

    def body(x_ref, f_ref, g_ref, b_ref, y_ref, yb_ref, yt_ref):
        xhat, _ = _ln_core(x_ref[...], f_ref[...])
        y = xhat * g_ref[...] + b_ref[...]
        y_ref[...] = y
        yb = y.astype(BF16)
        yb_ref[...] = yb
        yt_ref[...] = yb.T

    return pl.pallas_call(
        body, name=name, grid=(t // tr,), in_specs=[row, row, vec, vec],
        out_specs=[row, row, pl.BlockSpec((d, tr), lambda i: (0, i))],
        out_shape=[jax.ShapeDtypeStruct((t, d), F32), jax.ShapeDtypeStruct((t, d), BF16),
                   jax.ShapeDtypeStruct((d, t), BF16)],
        compiler_params=_params(("parallel",)),
    )(xres, f, g, b)


def _ln_bwd(xres, f, g, b, dy_or_target, loss_head, name, tr=256):
    t, d = xres.shape
    row = pl.BlockSpec((tr, d), lambda i: (i, 0))
    vec = pl.BlockSpec((1, d), lambda i: (0, 0))
    acc = pl.BlockSpec((SUBLANES, d), lambda i: (0, 0))
    lacc = pl.BlockSpec((SUBLANES, LANES), lambda i: (0, 0))

    def body(x_ref, f_ref, g_ref, b_ref, t_ref, dp_ref, dpb_ref, dg_ref, db_ref, *rest):
        i = pl.program_id(0)
        xhat, rstd = _ln_core(x_ref[...], f_ref[...])
        if loss_head:
            err = xhat * g_ref[...] + b_ref[...] - t_ref[...]
            dy = err * (1.0 / d)
            sq = err * err
            lanes = sq[:, :LANES]
            for kk in range(1, d // LANES):
                lanes = lanes + sq[:, kk * LANES:(kk + 1) * LANES]
            lpart = _rows8(lanes) * (0.5 / d)
        else:
            dy = t_ref[...]
        dxh = dy * g_ref[...]
        m1 = jnp.mean(dxh, axis=-1, keepdims=True)
        m2 = jnp.mean(dxh * xhat, axis=-1, keepdims=True)
        dp = rstd * (dxh - m1 - xhat * m2)
        dp_ref[...] = dp
        dpb_ref[...] = dp.astype(BF16)
        dgp = _rows8(dy * xhat)
        dbp = _rows8(dy)

        @pl.when(i == 0)
        def _():
            dg_ref[...] = dgp
            db_ref[...] = dbp
            if loss_head:
                rest[0][...] = lpart

        @pl.when(i > 0)
        def _():
            dg_ref[...] += dgp
            db_ref[...] += dbp
            if loss_head:
                rest[0][...] += lpart

    out_shape = [jax.ShapeDtypeStruct((t, d), F32), jax.ShapeDtypeStruct((t, d), BF16),
                 jax.ShapeDtypeStruct((SUBLANES, d), F32), jax.ShapeDtypeStruct((SUBLANES, d), F32)]
    out_specs = [row, row, acc, acc]
    if loss_head:
        out_shape.append(jax.ShapeDtypeStruct((SUBLANES, LANES), F32))
        out_specs.append(lacc)
    return pl.pallas_call(
        body, name=name, grid=(t // tr,), in_specs=[row, row, vec, vec, row], out_specs=out_specs,
        out_shape=out_shape, compiler_params=_params(("arbitrary",)),
    )(xres, f, g, b, dy_or_target)


def _gla_gates(glr, w2, gb):
    z = _dot(_bf(glr), w2, NN) + gb
    lg = (jnp.minimum(z, 0.0) - jnp.log(1.0 + jnp.exp(-jnp.abs(z)))) * (1.0 / GLA_TAU)
    c = z.shape[0]
    ri = lax.broadcasted_iota(jnp.int32, (c, c), 0)
    ci = lax.broadcasted_iota(jnp.int32, (c, c), 1)
    tri = (ci <= ri).astype(F32)
    bcum = _dot(tri, lg, NN, precision=lax.Precision.HIGHEST)
    blast = jnp.sum(lg, axis=0, keepdims=True)
    return z, bcum, blast, tri


def _gla_specs(t):
    c = GLA_CHUNK
    return c, t // c


def _gla_fwd(h_a, w2p, gate_b, norm_g):
    t = h_a.shape[0]
    c, n = _gla_specs(t)
    hk, hv = GLA_HEADS * GLA_DK, GLA_HEADS * GLA_DV
    scale = GLA_DK ** -0.5

    def body(q_ref, k_ref, v_ref, r_ref, glr_ref, w2_ref, gb_ref, ng_ref, og_ref, oraw_ref, sb_ref, st_ref):
        i = pl.program_id(0)

        @pl.when(i == 0)
        def _():
            st_ref[...] = jnp.zeros_like(st_ref)

        _, bcum, blast, _ = _gla_gates(glr_ref[...], w2_ref[...], gb_ref[...])
        ri = lax.broadcasted_iota(jnp.int32, (c, c), 0)
        ci = lax.broadcasted_iota(jnp.int32, (c, c), 1)
        causal = ci <= ri
        for h in range(GLA_HEADS):
            ks = slice(h * GLA_DK, (h + 1) * GLA_DK)
            vs = slice(h * GLA_DV, (h + 1) * GLA_DV)
            b_h, bl_h = bcum[:, ks], blast[:, ks]
            q_h, k_h = q_ref[:, ks], k_ref[:, ks]
            v_h = _bf(v_ref[:, vs])
            qi = _bf(q_h * scale * jnp.exp(b_h))
            ki = _bf(k_h * jnp.exp(-b_h))
            ke = _bf(k_h * jnp.exp(bl_h - b_h))
            st = st_ref[h]
            sb_ref[0, h] = st
            a = jnp.where(causal, _dot(qi, ki, NT), 0.0)
            o = _dot(_bf(a), v_h, NN) + _dot(qi, _bf(st), NT)
            st_ref[h] = st * jnp.exp(bl_h) + _dot(v_h, ke, TN)
            oraw_ref[:, vs] = o
            mu = jnp.mean(o, axis=-1, keepdims=True)
            oc = o - mu
            var = jnp.mean(oc * oc, axis=-1, keepdims=True)
            xh = oc * lax.rsqrt(var + LN_EPS)
            r_h = r_ref[:, vs]
            og_ref[:, vs] = (xh * ng_ref[:, vs] * (r_h * _sigmoid(r_h))).astype(BF16)

    return pl.pallas_call(
        body, name="gla_fwd", grid=(n,),
        in_specs=[pl.BlockSpec((c, hk), lambda i: (i, 0)), pl.BlockSpec((c, hk), lambda i: (i, 1)),
                  pl.BlockSpec((c, hv), lambda i: (i, 1)), pl.BlockSpec((c, hv), lambda i: (i, 2)),
                  pl.BlockSpec((c, LANES), lambda i: (i, GLA_W // LANES)),
                  pl.BlockSpec((LANES, hk), lambda i: (0, 0)), pl.BlockSpec((1, hk), lambda i: (0, 0)),
                  pl.BlockSpec((1, hv), lambda i: (0, 0))],
        out_specs=[pl.BlockSpec((c, hv), lambda i: (i, 0)), pl.BlockSpec((c, hv), lambda i: (i, 0)),
                   pl.BlockSpec((1, GLA_HEADS, GLA_DV, GLA_DK), lambda i: (i, 0, 0, 0))],
        out_shape=[jax.ShapeDtypeStruct((t, hv), BF16), jax.ShapeDtypeStruct((t, hv), F32),
                   jax.ShapeDtypeStruct((n, GLA_HEADS, GLA_DV, GLA_DK), F32)],
        scratch_shapes=[pltpu.VMEM((GLA_HEADS, GLA_DV, GLA_DK), F32)],
        compiler_params=_params(("arbitrary",)),
    )(h_a, h_a, h_a, h_a, h_a, w2p, gate_b, norm_g)


def _gla_bwd(h_a, w2p, gate_b, norm_g, o_raw, s_before, dmix):
    t = h_a.shape[0]
    c, n = _gla_specs(t)
    hk, hv = GLA_HEADS * GLA_DK, GLA_HEADS * GLA_DV
    scale = GLA_DK ** -0.5
    rev = lambda i: n - 1 - i

    def body(q_ref, k_ref, v_ref, r_ref, glr_ref, w2_ref, gb_ref, ng_ref, oraw_ref, sb_ref, do_ref,
             dh_ref, dw2_ref, dgb_ref, dng_ref, dst_ref):
        i = pl.program_id(0)

        @pl.when(i == 0)
        def _():
            dst_ref[...] = jnp.zeros_like(dst_ref)

        glr = glr_ref[...]
        z, bcum, blast, tri = _gla_gates(glr, w2_ref[...], gb_ref[...])
        ri = lax.broadcasted_iota(jnp.int32, (c, c), 0)
        ci = lax.broadcasted_iota(jnp.int32, (c, c), 1)
        causal = ci <= ri
        dlg_parts = []
        dng_parts = []
        for h in range(GLA_HEADS):
            ks = slice(h * GLA_DK, (h + 1) * GLA_DK)
            vs = slice(h * GLA_DV, (h + 1) * GLA_DV)
            o = oraw_ref[:, vs]
            mu = jnp.mean(o, axis=-1, keepdims=True)
            oc = o - mu
            var = jnp.mean(oc * oc, axis=-1, keepdims=True)
            rstd = lax.rsqrt(var + LN_EPS)
            xh = oc * rstd
            r_h = r_ref[:, vs]
            sg = _sigmoid(r_h)
            silu = r_h * sg
            dout = do_ref[:, vs]
            ng = ng_ref[:, vs]
            dng_parts.append(_rows8(dout * xh * silu))
            dr = dout * xh * ng * (sg * (1.0 + r_h * (1.0 - sg)))
            dxh = dout * ng * silu
            m1 = jnp.mean(dxh, axis=-1, keepdims=True)
            m2 = jnp.mean(dxh * xh, axis=-1, keepdims=True)
            do_raw = _bf(rstd * (dxh - m1 - xh * m2))
            b_h, bl_h = bcum[:, ks], blast[:, ks]
            q_h, k_h = q_ref[:, ks], k_ref[:, ks]
            v_h = _bf(v_ref[:, vs])
            eb, enb, eend = jnp.exp(b_h), jnp.exp(-b_h), jnp.exp(bl_h - b_h)
            decay = jnp.exp(bl_h)
            qi_f, ki_f, ke_f = q_h * scale * eb, k_h * enb, k_h * eend
            qi, ki, ke = _bf(qi_f), _bf(ki_f), _bf(ke_f)
            st = sb_ref[0, h]
            dst = dst_ref[h]
            dst_b = _bf(dst)
            a = _bf(jnp.where(causal, _dot(qi, ki, NT), 0.0))
            da = _bf(jnp.where(causal, _dot(do_raw, v_h, NT), 0.0))
            dv = _dot(a, do_raw, TN) + _dot(ke, dst_b, NT)
            dqi = _dot(da, ki, NN) + _dot(do_raw, _bf(st), NN)
            dki = _dot(da, qi, TN)
            dke = _dot(v_h, dst_b, NN)
            dst_ref[h] = _dot(do_raw, qi, TN) + dst * decay
            dbl = decay * jnp.sum(st * dst, axis=0, keepdims=True) + jnp.sum(dke * ke_f, axis=0, keepdims=True)
            dbc = dqi * qi_f - dki * ki_f - dke * ke_f
            dlg_parts.append(_dot(tri, dbc, TN, precision=lax.Precision.HIGHEST) + dbl)
            dh_ref[:, ks] = (dqi * eb * scale).astype(BF16)
            dh_ref[:, hk + h * GLA_DK: hk + (h + 1) * GLA_DK] = (dki * enb + dke * eend).astype(BF16)
            dh_ref[:, 2 * hk + h * GLA_DV: 2 * hk + (h + 1) * GLA_DV] = dv.astype(BF16)
            dh_ref[:, 2 * hk + hv + h * GLA_DV: 2 * hk + hv + (h + 1) * GLA_DV] = dr.astype(BF16)
        dlg = jnp.concatenate(dlg_parts, axis=1)
        dz = dlg * (1.0 / GLA_TAU) * _sigmoid(-z)
        dz_b = _bf(dz)
        dh_ref[:, GLA_W:] = _dot(dz_b, w2_ref[...], NT).astype(BF16)
        dw2p = _dot(_bf(glr), dz_b, TN)
        dgbp = _rows8(dz)
        dngp = jnp.concatenate(dng_parts, axis=1)

        @pl.when(i == 0)
        def _():
            dw2_ref[...] = dw2p
            dgb_ref[...] = dgbp
            dng_ref[...] = dngp

        @pl.when(i > 0)
        def _():
            dw2_ref[...] += dw2p
            dgb_ref[...] += dgbp
            dng_ref[...] += dngp

    return pl.pallas_call(
        body, name="gla_bwd", grid=(n,),
        in_specs=[pl.BlockSpec((c, hk), lambda i: (rev(i), 0)), pl.BlockSpec((c, hk), lambda i: (rev(i), 1)),
                  pl.BlockSpec((c, hv), lambda i: (rev(i), 1)), pl.BlockSpec((c, hv), lambda i: (rev(i), 2)),
                  pl.BlockSpec((c, LANES), lambda i: (rev(i), GLA_W // LANES)),
                  pl.BlockSpec((LANES, hk), lambda i: (0, 0)), pl.BlockSpec((1, hk), lambda i: (0, 0)),
                  pl.BlockSpec((1, hv), lambda i: (0, 0)),
                  pl.BlockSpec((c, hv), lambda i: (rev(i), 0)),
                  pl.BlockSpec((1, GLA_HEADS, GLA_DV, GLA_DK), lambda i: (rev(i), 0, 0, 0)),
                  pl.BlockSpec((c, hv), lambda i: (rev(i), 0))],
        out_specs=[pl.BlockSpec((c, HA_W), lambda i: (rev(i), 0)),
                   pl.BlockSpec((LANES, hk), lambda i: (0, 0)),
                   pl.BlockSpec((SUBLANES, hk), lambda i: (0, 0)),
                   pl.BlockSpec((SUBLANES, hv), lambda i: (0, 0))],
        out_shape=[jax.ShapeDtypeStruct((t, HA_W), BF16), jax.ShapeDtypeStruct((LANES, hk), F32),
                   jax.ShapeDtypeStruct((SUBLANES, hk), F32), jax.ShapeDtypeStruct((SUBLANES, hv), F32)],
        scratch_shapes=[pltpu.VMEM((GLA_HEADS, GLA_DV, GLA_DK), F32)],
        compiler_params=_params(("arbitrary",)),
    )(h_a, h_a, h_a, h_a, h_a, w2p, gate_b, norm_g, o_raw, s_before, dmix)


def _rope_tables(positions):
    half = ROPE_DIMS // 2
    inv_freq = ROPE_THETA ** (-jnp.arange(0, ROPE_DIMS, 2, dtype=F32) / ROPE_DIMS)
    ang = positions.astype(F32).reshape(-1, 1) * inv_freq
    cos, sin = jnp.cos(ang), jnp.sin(ang)
    t = cos.shape[0]
    one = jnp.ones((t, DIL_HD - ROPE_DIMS), F32)
    zero = jnp.zeros((t, DIL_HD - ROPE_DIMS), F32)
    zh = jnp.zeros((t, half), F32)
    return (jnp.concatenate([cos, cos, one], axis=1), jnp.concatenate([-sin, zh, zero], axis=1),
            jnp.concatenate([zh, sin, zero], axis=1))


def _rope_apply(x, c, s1, s2):
    half = ROPE_DIMS // 2
    return x * c + pltpu.roll(x, DIL_HD - half, 1) * s1 + pltpu.roll(x, half, 1) * s2


def _rope_apply_t(dy, c, s1, s2):
    half = ROPE_DIMS // 2
    return dy * c + pltpu.roll(dy * s1, half, 1) + pltpu.roll(dy * s2, DIL_HD - half, 1)


def _rope_fwd(h_b, tabs, tr=256):
    t = h_b.shape[0]
    w = DIL_HEADS * DIL_HD
    scale = DIL_HD ** -0.5
    tab = pl.BlockSpec((tr, DIL_HD), lambda i: (i, 0))
    outb = pl.BlockSpec((tr, w), lambda i: (i, 0))

    def body(q_ref, k_ref, c_ref, s1_ref, s2_ref, qo_ref, ko_ref):
        c, s1, s2 = c_ref[...], s1_ref[...], s2_ref[...]
        for h in range(DIL_HEADS):
            hs = slice(h * DIL_HD, (h + 1) * DIL_HD)
            qo_ref[:, hs] = _rope_apply(q_ref[:, hs] * scale, c, s1, s2)
            ko_ref[:, hs] = _rope_apply(k_ref[:, hs], c, s1, s2)

    return pl.pallas_call(
        body, name="rope_fwd", grid=(t // tr,),
        in_specs=[pl.BlockSpec((tr, w), lambda i: (i, 0)), pl.BlockSpec((tr, w), lambda i: (i, 1)), tab, tab, tab],
        out_specs=[outb, outb],
        out_shape=[jax.ShapeDtypeStruct((t, w), F32)] * 2,
        compiler_params=_params(("parallel",)),
    )(h_b, h_b, *tabs)


def _dil_dh(dq, dk, dv, tabs, tr=256):
    t, w = dq.shape
    scale = DIL_HD ** -0.5
    tab = pl.BlockSpec((tr, DIL_HD), lambda i: (i, 0))
    inb = pl.BlockSpec((tr, w), lambda i: (i, 0))

    def body(dq_ref, dk_ref, dv_ref, c_ref, s1_ref, s2_ref, o_ref):
        c, s1, s2 = c_ref[...], s1_ref[...], s2_ref[...]
        for h in range(DIL_HEADS):
            hs = slice(h * DIL_HD, (h + 1) * DIL_HD)
            o_ref[:, h * DIL_HD:(h + 1) * DIL_HD] = (_rope_apply_t(dq_ref[:, hs], c, s1, s2) * scale).astype(BF16)
            o_ref[:, w + h * DIL_HD: w + (h + 1) * DIL_HD] = _rope_apply_t(dk_ref[:, hs], c, s1, s2).astype(BF16)
        o_ref[:, 2 * w:] = dv_ref[...].astype(BF16)

    return pl.pallas_call(
        body, name="dil_dh", grid=(t // tr,), in_specs=[inb] * 3 + [tab] * 3,
        out_specs=pl.BlockSpec((tr, 3 * w), lambda i: (i, 0)),
        out_shape=jax.ShapeDtypeStruct((t, 3 * w), BF16), compiler_params=_params(("parallel",)),
    )(dq, dk, dv, *tabs)


BANDS = 8


def _to_branch(a, d):
    t, w = a.shape
    return a.reshape(t // d, d, w // DIL_HD, DIL_HD).transpose(1, 2, 0, 3).reshape(-1, DIL_HD)


def _from_branch(a, d, t):
    hds = a.shape[0] // t
    return a.reshape(d, hds, t // d, DIL_HD).transpose(2, 0, 1, 3).reshape(t, hds * DIL_HD)


def _band_masks(not_first):
    r = lax.broadcasted_iota(jnp.int32, (DIL_BAND, 2 * DIL_BAND), 0)
    c = lax.broadcasted_iota(jnp.int32, (DIL_BAND, 2 * DIL_BAND), 1)
    nf = jnp.full((DIL_BAND, 2 * DIL_BAND), not_first, jnp.int32)
    look_back = jnp.logical_and(jnp.logical_and(c < DIL_BAND, c >= r), nf > 0)
    own_band = jnp.logical_and(c >= DIL_BAND, (c - DIL_BAND) <= r)
    return jnp.logical_or(look_back, own_band)


def _dil_fwd(q, k, v, nb, name):
    rows = q.shape[0]
    blk = BANDS * DIL_BAND
    steps = rows // blk
    main = pl.BlockSpec((blk, DIL_HD), lambda i: (i, 0))
    prev = pl.BlockSpec((DIL_BAND, DIL_HD), lambda i: (jnp.maximum(i * BANDS - 1, 0), 0))

    def body(q_ref, k_ref, v_ref, kp_ref, vp_ref, o_ref, l_ref):
        i = pl.program_id(0)
        for j in range(BANDS):
            lo, hi = j * DIL_BAND, (j + 1) * DIL_BAND
            if j == 0:
                kcat = jnp.concatenate([kp_ref[...], k_ref[lo:hi, :]], axis=0)
                vcat = jnp.concatenate([vp_ref[...], v_ref[lo:hi, :]], axis=0)
            else:
                kcat = k_ref[lo - DIL_BAND:hi, :]
                vcat = v_ref[lo - DIL_BAND:hi, :]
            not_first = (((i * BANDS + j) % nb) != 0).astype(jnp.int32)
            s = jnp.where(_band_masks(not_first), _dot(q_ref[lo:hi, :], kcat, NT), NEG)
            m = jnp.max(s, axis=-1, keepdims=True)
            p = jnp.exp(s - m)
            den = jnp.sum(p, axis=-1, keepdims=True)
            o_ref[lo:hi, :] = _dot(_bf(p), vcat, NN) / den
            l_ref[lo:hi, :] = jnp.broadcast_to(m + jnp.log(den), (DIL_BAND, DIL_HD))

    return pl.pallas_call(
        body, name=name, grid=(steps,), in_specs=[main, main, main, prev, prev], out_specs=[main, main],
        out_shape=[jax.ShapeDtypeStruct((rows, DIL_HD), F32)] * 2, compiler_params=_params(("parallel",)),
    )(q, k, v, k, v)


def _dil_bwd(q, k, v, do, lse, dd, nb, name):
    rows = q.shape[0]
    blk = BANDS * DIL_BAND
    steps = rows // blk
    last_band = rows // DIL_BAND - 1
    main = pl.BlockSpec((blk, DIL_HD), lambda i: (i, 0))
    prev = pl.BlockSpec((DIL_BAND, DIL_HD), lambda i: (jnp.maximum(i * BANDS - 1, 0), 0))
    nxt = pl.BlockSpec((DIL_BAND, DIL_HD), lambda i: (jnp.minimum(i * BANDS + BANDS, last_band), 0))

    def body(q_ref, k_ref, v_ref, do_ref, l_ref, dd_ref, kp_ref, vp_ref, qn_ref, don_ref, ln_ref, ddn_ref,
             dq_ref, dk_ref, dv_ref, ak_ref, av_ref):
        i = pl.program_id(0)
        ak_ref[...] = jnp.zeros_like(ak_ref)
        av_ref[...] = jnp.zeros_like(av_ref)
        for j in range(BANDS + 1):
            lo, hi = j * DIL_BAND, (j + 1) * DIL_BAND
            if j == 0:
                kcat = jnp.concatenate([kp_ref[...], k_ref[lo:hi, :]], axis=0)
                vcat = jnp.concatenate([vp_ref[...], v_ref[lo:hi, :]], axis=0)
            elif j < BANDS:
                kcat = k_ref[lo - DIL_BAND:hi, :]
                vcat = v_ref[lo - DIL_BAND:hi, :]
            else:
                kcat = jnp.concatenate([k_ref[lo - DIL_BAND:lo, :], k_ref[lo - DIL_BAND:lo, :]], axis=0)
                vcat = jnp.concatenate([v_ref[lo - DIL_BAND:lo, :], v_ref[lo - DIL_BAND:lo, :]], axis=0)
            if j < BANDS:
                qj, doj, lj, ddj = q_ref[lo:hi, :], do_ref[lo:hi, :], l_ref[lo:hi, :], dd_ref[lo:hi, :]
            else:
                qj, doj, lj, ddj = qn_ref[...], don_ref[...], ln_ref[...], ddn_ref[...]
            not_first = (((i * BANDS + j) % nb) != 0).astype(jnp.int32)
            mask = _band_masks(not_first)
            if j == BANDS:
                cidx = lax.broadcasted_iota(jnp.int32, mask.shape, 1)
                mask = jnp.logical_and(mask, cidx < DIL_BAND)
            s = jnp.where(mask, _dot(qj, kcat, NT), NEG)
            p = jnp.exp(s - jnp.concatenate([lj, lj], axis=1))
            dp = _dot(doj, vcat, NT)
            ds = _bf(p * (dp - jnp.concatenate([ddj, ddj], axis=1)))
            if j < BANDS:
                dq_ref[lo:hi, :] = _dot(ds, kcat, NN)
            ak_ref[lo:hi + DIL_BAND, :] += _dot(ds, qj, TN)
            av_ref[lo:hi + DIL_BAND, :] += _dot(_bf(p), doj, TN)
        dk_ref[...] = ak_ref[DIL_BAND:DIL_BAND + blk, :]
        dv_ref[...] = av_ref[DIL_BAND:DIL_BAND + blk, :]

    return pl.pallas_call(
        body, name=name, grid=(steps,),
        in_specs=[main] * 6 + [prev, prev] + [nxt] * 4, out_specs=[main] * 3,
        out_shape=[jax.ShapeDtypeStruct((rows, DIL_HD), F32)] * 3,
        scratch_shapes=[pltpu.VMEM((blk + 2 * DIL_BAND, DIL_HD), F32)] * 2,
        compiler_params=_params(("parallel",)),
    )(q, k, v, do, lse, dd, k, v, q, do, lse, dd)


def _dil_merge(os_, ls_, tr=256):
    t, w = os_[0].shape
    blk = pl.BlockSpec((tr, w), lambda i: (i, 0))

    def body(o1, o2, o3, l1, l2, l3, ob_ref, of_ref, lt_ref):
        a, b, c = l1[...], l2[...], l3[...]
        m = jnp.maximum(jnp.maximum(a, b), c)
        ea, eb, ec = jnp.exp(a - m), jnp.exp(b - m), jnp.exp(c - m)
        den = ea + eb + ec
        o = (ea * o1[...] + eb * o2[...] + ec * o3[...]) / den
        ob_ref[...] = o.astype(BF16)
        of_ref[...] = o
        lt_ref[...] = m + jnp.log(den)

    return pl.pallas_call(
        body, name="dil_merge", grid=(t // tr,), in_specs=[blk] * 6, out_specs=[blk] * 3,
        out_shape=[jax.ShapeDtypeStruct((t, w), BF16), jax.ShapeDtypeStruct((t, w), F32),
                   jax.ShapeDtypeStruct((t, w), F32)],
        compiler_params=_params(("parallel",)),
    )(*os_, *ls_)


def _dil_bwd_prep(dmix, o_d, tr=256):
    t, w = o_d.shape
    blk = pl.BlockSpec((tr, w), lambda i: (i, 0))

    def body(do_ref, o_ref, dob_ref, dd_ref):
        do = do_ref[...]
        prod = do * o_ref[...]
        dob_ref[...] = do.astype(BF16)
        for h in range(DIL_HEADS):
            hs = slice(h * DIL_HD, (h + 1) * DIL_HD)
            dd_ref[:, hs] = jnp.broadcast_to(jnp.sum(prod[:, hs], axis=-1, keepdims=True), (tr, DIL_HD))

    return pl.pallas_call(
        body, name="dil_bwd_prep", grid=(t // tr,),
        in_specs=[pl.BlockSpec((tr, w), lambda i: (i, 1)), blk], out_specs=[blk, blk],
        out_shape=[jax.ShapeDtypeStruct((t, w), BF16), jax.ShapeDtypeStruct((t, w), F32)],
        compiler_params=_params(("parallel",)),
    )(dmix, o_d)


def _gather_rows(dst_ref, src_ref, t, d, cast=None):
    n = t // d
    for r in range(d):
        v = src_ref[pl.ds(r, n, stride=d), :] if d > 1 else src_ref[...]
        dst_ref[r * n:(r + 1) * n, :] = v if cast is None else v.astype(cast)


def _tri_mask():
    r = lax.broadcasted_iota(jnp.int32, (DIL_BAND, DIL_BAND), 0)
    c = lax.broadcasted_iota(jnp.int32, (DIL_BAND, DIL_BAND), 1)
    return c <= r


def _dil_fwd_all(qr, kr, h_b):
    t = qr.shape[0]
    nbands = t // DIL_BAND
    nbr = len(DIL_DILATIONS)
    hoff = DIL_HEADS

    def col(off):
        return pl.BlockSpec((t, DIL_HD), lambda h: (0, off + h), pipeline_mode=pl.Buffered(1))

    outb = pl.BlockSpec((t, DIL_HD), lambda h: (0, h))

    def body(q_ref, k_ref, v_ref, ob_ref, of_ref, lt_ref, qs, ks, vs, os_, ls_, *br):
        obr, lbr = br[:nbr], br[nbr:]
        for bi, d in enumerate(DIL_DILATIONS):
            n = t // d
            nb = n // DIL_BAND
            _gather_rows(qs, q_ref, t, d, BF16)
            _gather_rows(ks, k_ref, t, d, BF16)
            _gather_rows(vs, v_ref, t, d, BF16)
            s = jnp.where(_tri_mask(), _dot(qs[0:DIL_BAND, :], ks[0:DIL_BAND, :], NT), NEG)
            m = jnp.max(s, axis=-1, keepdims=True)
            pr = jnp.exp(s - m)
            den = jnp.sum(pr, axis=-1, keepdims=True)
            os_[0:DIL_BAND, :] = _dot(_bf(pr), vs[0:DIL_BAND, :], NN) / den
            ls_[0:DIL_BAND, :] = jnp.broadcast_to(m + jnp.log(den), (DIL_BAND, DIL_HD))

            def band(b, carry, nb=nb):
                st = pl.multiple_of((b - 1) * DIL_BAND, DIL_BAND)
                cur = pl.ds(st + DIL_BAND, DIL_BAND)
                both = pl.ds(st, 2 * DIL_BAND)
                not_first = ((b % nb) != 0).astype(jnp.int32)
                s = jnp.where(_band_masks(not_first), _dot(qs[cur, :], ks[both, :], NT), NEG)
                m = jnp.max(s, axis=-1, keepdims=True)
                pr = jnp.exp(s - m)
                den = jnp.sum(pr, axis=-1, keepdims=True)
                os_[cur, :] = _dot(_bf(pr), vs[both, :], NN) / den
                ls_[cur, :] = jnp.broadcast_to(m + jnp.log(den), (DIL_BAND, DIL_HD))
                return carry

            lax.fori_loop(1, nbands, band, 0)
            for r in range(d):
                dst = pl.ds(r, n, stride=d) if d > 1 else slice(None)
                obr[bi][dst, :] = os_[r * n:(r + 1) * n, :]
                lbr[bi][dst, :] = ls_[r * n:(r + 1) * n, :]
        rows = 512
        for c0 in range(0, t, rows):
            sl = slice(c0, c0 + rows)
            la, lb, lc = lbr[0][sl, :], lbr[1][sl, :], lbr[2][sl, :]
            m = jnp.maximum(jnp.maximum(la, lb), lc)
            ea, eb, ec = jnp.exp(la - m), jnp.exp(lb - m), jnp.exp(lc - m)
            den = ea + eb + ec
            o = (ea * obr[0][sl, :] + eb * obr[1][sl, :] + ec * obr[2][sl, :]) / den
            ob_ref[sl, :] = o.astype(BF16)
            of_ref[sl, :] = o
            lt_ref[sl, :] = m + jnp.log(den)

    w = DIL_HEADS * DIL_HD
    vm = lambda dt: pltpu.VMEM((t, DIL_HD), dt)
    return pl.pallas_call(
        body, name="dil_fwd", grid=(DIL_HEADS,), in_specs=[col(0), col(0), col(2 * hoff)],
        out_specs=[outb, outb, outb],
        out_shape=[jax.ShapeDtypeStruct((t, w), BF16), jax.ShapeDtypeStruct((t, w), F32),
                   jax.ShapeDtypeStruct((t, w), F32)],
        scratch_shapes=[vm(BF16)] * 3 + [vm(F32)] * 2 + [vm(F32)] * (2 * nbr),
        compiler_params=_params(("parallel",)),
    )(qr, kr, h_b)


def _dil_bwd_all(qr, kr, h_b, dmix, o_d, lse_tot):
    t = qr.shape[0]
    nbands = t // DIL_BAND
    hoff = DIL_HEADS

    def col(off):
        return pl.BlockSpec((t, DIL_HD), lambda h: (0, off + h), pipeline_mode=pl.Buffered(1))

    outb = pl.BlockSpec((t, DIL_HD), lambda h: (0, h))

    def body(q_ref, k_ref, v_ref, do_ref, o_ref, l_ref, dq_ref, dk_ref, dv_ref,
             qs, ks, vs, dos, lss, dds, dqs, acck, accv):
        for bi, d in enumerate(DIL_DILATIONS):
            n = t // d
            nb = n // DIL_BAND
            _gather_rows(qs, q_ref, t, d, BF16)
            _gather_rows(ks, k_ref, t, d, BF16)
            _gather_rows(vs, v_ref, t, d, BF16)
            _gather_rows(dos, do_ref, t, d, BF16)
            _gather_rows(lss, l_ref, t, d)
            for r in range(d):
                src = pl.ds(r, n, stride=d) if d > 1 else slice(None)
                prod = do_ref[src, :] * o_ref[src, :]
                dds[r * n:(r + 1) * n, :] = jnp.broadcast_to(jnp.sum(prod, axis=-1, keepdims=True), (n, DIL_HD))
            acck[...] = jnp.zeros_like(acck)
            accv[...] = jnp.zeros_like(accv)
            b0 = slice(0, DIL_BAND)
            s = jnp.where(_tri_mask(), _dot(qs[b0, :], ks[b0, :], NT), NEG)
            pr = jnp.exp(s - lss[b0, :])
            ds = _bf(pr * (_dot(dos[b0, :], vs[b0, :], NT) - dds[b0, :]))
            dqs[b0, :] = _dot(ds, ks[b0, :], NN)
            acck[DIL_BAND:2 * DIL_BAND, :] += _dot(ds, qs[b0, :], TN)
            accv[DIL_BAND:2 * DIL_BAND, :] += _dot(_bf(pr), dos[b0, :], TN)

            def band(b, carry, nb=nb):
                st = pl.multiple_of((b - 1) * DIL_BAND, DIL_BAND)
                cur = pl.ds(st + DIL_BAND, DIL_BAND)
                both = pl.ds(st, 2 * DIL_BAND)
                acc_rows = pl.ds(st + DIL_BAND, 2 * DIL_BAND)
                not_first = ((b % nb) != 0).astype(jnp.int32)
                qb, dob, lb, ddb = qs[cur, :], dos[cur, :], lss[cur, :], dds[cur, :]
                kcat, vcat = ks[both, :], vs[both, :]
                s = jnp.where(_band_masks(not_first), _dot(qb, kcat, NT), NEG)
                pr = jnp.exp(s - jnp.concatenate([lb, lb], axis=1))
                ds = _bf(pr * (_dot(dob, vcat, NT) - jnp.concatenate([ddb, ddb], axis=1)))
                dqs[cur, :] = _dot(ds, kcat, NN)
                acck[acc_rows, :] += _dot(ds, qb, TN)
                accv[acc_rows, :] += _dot(_bf(pr), dob, TN)
                return carry

            lax.fori_loop(1, nbands, band, 0)
            for r in range(d):
                lo = r * n
                if d == 1:
                    dq_ref[...] = dqs[...]
                    dk_ref[...] = acck[DIL_BAND:DIL_BAND + t, :]
                    dv_ref[...] = accv[DIL_BAND:DIL_BAND + t, :]
                else:
                    dst = pl.ds(r, n, stride=d)
                    dq_ref[dst, :] = dq_ref[dst, :] + dqs[lo:lo + n, :]
                    dk_ref[dst, :] = dk_ref[dst, :] + acck[DIL_BAND + lo:DIL_BAND + lo + n, :]
                    dv_ref[dst, :] = dv_ref[dst, :] + accv[DIL_BAND + lo:DIL_BAND + lo + n, :]

    w = DIL_HEADS * DIL_HD
    vm = lambda dt, extra=0: pltpu.VMEM((t + extra, DIL_HD), dt)
    return pl.pallas_call(
        body, name="dil_bwd", grid=(DIL_HEADS,),
        in_specs=[col(0), col(0), col(2 * hoff), col(hoff), col(0), col(0)], out_specs=[outb] * 3,
        out_shape=[jax.ShapeDtypeStruct((t, w), F32)] * 3,
        scratch_shapes=[vm(BF16)] * 4 + [vm(F32)] * 3 + [vm(F32, DIL_BAND)] * 2,
        compiler_params=_params(("parallel",)),
    )(qr, kr, h_b, dmix, o_d, lse_tot)


def _ca_fwd(q, memkv, tq=512):
    t, d = q.shape
    m = memkv.shape[0]
    scale = CA_HD ** -0.5

    def body(q_ref, k_ref, v_ref, o_ref, ot_ref):
        for h in range(CA_HEADS):
            hs = slice(h * CA_HD, (h + 1) * CA_HD)
            s = _dot(q_ref[:, hs], k_ref[:, hs], NT) * scale
            p = jnp.exp(s - jnp.max(s, axis=-1, keepdims=True))
            p = p / jnp.sum(p, axis=-1, keepdims=True)
            o = _dot(_bf(p), v_ref[:, hs], NN).astype(BF16)
            o_ref[:, hs] = o
            ot_ref[hs, :] = o.T

    return pl.pallas_call(
        body, name="ca_fwd", grid=(t // tq,),
        in_specs=[pl.BlockSpec((tq, d), lambda i: (i, 0)), pl.BlockSpec((m, d), lambda i: (0, 0)),
                  pl.BlockSpec((m, d), lambda i: (0, 1))],
        out_specs=[pl.BlockSpec((tq, d), lambda i: (i, 0)), pl.BlockSpec((d, tq), lambda i: (0, i))],
        out_shape=[jax.ShapeDtypeStruct((t, d), BF16), jax.ShapeDtypeStruct((d, t), BF16)],
        compiler_params=_params(("parallel",)),
    )(q, memkv, memkv)


def _ca_bwd(q, memkv, do, tq=512):
    t, d = q.shape
    m = memkv.shape[0]
    scale = CA_HD ** -0.5

    def body(q_ref, k_ref, v_ref, do_ref, dq_ref, dkv_ref):
        i = pl.program_id(0)

        @pl.when(i == 0)
        def _():
            dkv_ref[...] = jnp.zeros_like(dkv_ref)

        for h in range(CA_HEADS):
            hs = slice(h * CA_HD, (h + 1) * CA_HD)
            q_h, k_h, v_h, do_h = q_ref[:, hs], k_ref[:, hs], v_ref[:, hs], do_ref[:, hs]
            s = _dot(q_h, k_h, NT) * scale
            p = jnp.exp(s - jnp.max(s, axis=-1, keepdims=True))
            p = p / jnp.sum(p, axis=-1, keepdims=True)
            dp = _dot(do_h, v_h, NT)
            ds = _bf(p * (dp - jnp.sum(p * dp, axis=-1, keepdims=True)) * scale)
            dq_ref[:, hs] = _dot(ds, k_h, NN).astype(BF16)
            dkv_ref[:, hs] += _dot(ds, q_h, TN)
            dkv_ref[:, d + h * CA_HD: d + (h + 1) * CA_HD] += _dot(_bf(p), do_h, TN)

    return pl.pallas_call(
        body, name="ca_bwd", grid=(t // tq,),
        in_specs=[pl.BlockSpec((tq, d), lambda i: (i, 0)), pl.BlockSpec((m, d), lambda i: (0, 0)),
                  pl.BlockSpec((m, d), lambda i: (0, 1)), pl.BlockSpec((tq, d), lambda i: (i, 0))],
        out_specs=[pl.BlockSpec((tq, d), lambda i: (i, 0)), pl.BlockSpec((m, 2 * d), lambda i: (0, 0))],
        out_shape=[jax.ShapeDtypeStruct((t, d), BF16), jax.ShapeDtypeStruct((m, 2 * d), F32)],
        compiler_params=_params(("arbitrary",)),
    )(q, memkv, memkv, do)


STRIP = 256


def _shift_down(u, n, row):
    return jnp.where(row >= n, pltpu.roll(u, n, 0), 0.0)


def _shift_up(u, n, row):
    t = u.shape[0]
    return jnp.where(row < t - n, pltpu.roll(u, t - n, 0), 0.0)


def _conv(u, cw_ref, row):
    return ((cw_ref[3:4, :] + cw_ref[0:1, :] * _shift_down(u, 2, row)) + cw_ref[1:2, :] * _shift_down(u, 1, row)) \
        + cw_ref[2:3, :] * u


def _swiglu_fwd(u0, cw):
    t, w = u0.shape[0], u0.shape[1] // 2
    ns = w // STRIP
    col = pl.BlockSpec((t, STRIP), lambda j: (0, j))
    col_up = pl.BlockSpec((t, STRIP), lambda j: (0, ns + j))
    cws = pl.BlockSpec((SUBLANES, STRIP), lambda j: (0, j))
    cws_up = pl.BlockSpec((SUBLANES, STRIP), lambda j: (0, ns + j))

    def body(g_ref, u_ref, cg_ref, cu_ref, a_ref, at_ref):
        row = lax.broadcasted_iota(jnp.int32, (t, STRIP), 0)
        gate = _conv(g_ref[...].astype(F32), cg_ref, row)
        up = _conv(u_ref[...].astype(F32), cu_ref, row)
        act = (gate * _sigmoid(gate) * up).astype(BF16)
        a_ref[...] = act
        at_ref[...] = act.T

    return pl.pallas_call(
        body, name="swiglu_fwd", grid=(ns,), in_specs=[col, col_up, cws, cws_up],
        out_specs=[col, pl.BlockSpec((STRIP, t), lambda j: (j, 0))],
        out_shape=[jax.ShapeDtypeStruct((t, w), BF16), jax.ShapeDtypeStruct((w, t), BF16)],
        compiler_params=_params(("parallel",)),
    )(u0, u0, cw, cw)


def _swiglu_bwd(u0, cw, da):
    t, w = u0.shape[0], u0.shape[1] // 2
    ns = w // STRIP
    col = pl.BlockSpec((t, STRIP), lambda j: (0, j))
    col_up = pl.BlockSpec((t, STRIP), lambda j: (0, ns + j))
    cws = pl.BlockSpec((SUBLANES, STRIP), lambda j: (0, j))
    cws_up = pl.BlockSpec((SUBLANES, STRIP), lambda j: (0, ns + j))

    def conv_bwd(du, u0, cw_ref, row, du0_ref, dcw_ref):
        du0 = (cw_ref[2:3, :] * du + cw_ref[1:2, :] * _shift_up(du, 1, row)) + cw_ref[0:1, :] * _shift_up(du, 2, row)
        du0_ref[...] = du0.astype(BF16)
        dcw_ref[0:1, :] = jnp.sum(du * _shift_down(u0, 2, row), axis=0, keepdims=True)
        dcw_ref[1:2, :] = jnp.sum(du * _shift_down(u0, 1, row), axis=0, keepdims=True)
        dcw_ref[2:3, :] = jnp.sum(du * u0, axis=0, keepdims=True)
        dcw_ref[3:4, :] = jnp.sum(du, axis=0, keepdims=True)
        dcw_ref[4:8, :] = jnp.zeros((4, STRIP), F32)

    def body(g_ref, u_ref, cg_ref, cu_ref, da_ref, dg0_ref, du0_ref, dcg_ref, dcu_ref):
        row = lax.broadcasted_iota(jnp.int32, (t, STRIP), 0)
        g0, up0 = g_ref[...].astype(F32), u_ref[...].astype(F32)
        gate = _conv(g0, cg_ref, row)
        up = _conv(up0, cu_ref, row)
        sg = _sigmoid(gate)
        da = da_ref[...].astype(F32)
        dgate = da * up * (sg * (1.0 + gate * (1.0 - sg)))
        dup = da * (gate * sg)
        conv_bwd(dgate, g0, cg_ref, row, dg0_ref, dcg_ref)
        conv_bwd(dup, up0, cu_ref, row, du0_ref, dcu_ref)

    return pl.pallas_call(
        body, name="swiglu_bwd", grid=(ns,), in_specs=[col, col_up, cws, cws_up, col],
        out_specs=[col, col, cws, cws],
        out_shape=[jax.ShapeDtypeStruct((t, w), BF16), jax.ShapeDtypeStruct((t, w), BF16),
                   jax.ShapeDtypeStruct((SUBLANES, w), F32), jax.ShapeDtypeStruct((SUBLANES, w), F32)],
        compiler_params=_params(("parallel",)),
    )(u0, u0, cw, cw, da)


def _ffn_win_grad(x2t, dug, duu, tm=512):
    d, t = x2t.shape
    tk = t // 2
    nk = t // tk
    sp, sw = FF_SLAB_P, FF_SLAB

    def body(a_ref, bg_ref, bu_ref, o_ref, acc_ref):
        j, kk = pl.program_id(1), pl.program_id(2)

        @pl.when(kk == 0)
        def _():
            acc_ref[...] = jnp.zeros_like(acc_ref)

        @pl.when(j < 4)
        def _():
            acc_ref[...] += _dot(a_ref[...], bg_ref[...], NN)

        @pl.when(j >= 4)
        def _():
            acc_ref[...] += _dot(a_ref[...], bu_ref[...], NN)

        @pl.when(kk == nk - 1)
        def _():
            o_ref[...] = acc_ref[:, :sw]

    return pl.pallas_call(
        body, name="mm_g_ffn_in", grid=(d // tm, 8, nk),
        in_specs=[pl.BlockSpec((tm, tk), lambda i, j, kk: (i, kk)),
                  pl.BlockSpec((tk, sp), lambda i, j, kk: (kk, jnp.minimum(j, 3))),
                  pl.BlockSpec((tk, sp), lambda i, j, kk: (kk, jnp.maximum(j - 4, 0)))],
        out_specs=pl.BlockSpec((None, tm, sw), lambda i, j, kk: (j, i, 0)),
        out_shape=jax.ShapeDtypeStruct((8, d, sw), F32),
        scratch_shapes=[pltpu.VMEM((tm, sp), F32)],
        compiler_params=_params(("parallel", "parallel", "arbitrary")),
    )(x2t, dug, duu)


def _tile2d(r, c, limit=1 << 20):
    tr, tc = r, c
    while tr * tc * 4 > limit:
        if tr % (2 * SUBLANES) == 0:
            tr //= 2
        elif tc % (2 * LANES) == 0:
            tc //= 2
        else:
            break
    return tr, tc


def _adamw_math(w, m, v, g):
    c1 = 1.0 - ADAM_B1 ** ADAM_STEP
    c2 = 1.0 - ADAM_B2 ** ADAM_STEP
    mm = ADAM_B1 * m + (1.0 - ADAM_B1) * g
    vv = ADAM_B2 * v + (1.0 - ADAM_B2) * (g * g)
    delta = -ADAM_LR * ((mm / c1) / (jnp.sqrt(vv / c2) + ADAM_EPS) + ADAM_WD * w)
    return delta, mm, vv


def _adamw(w, m, v, g, name):
    r, c = w.shape
    blk = pl.BlockSpec((r, c), lambda i: (0, 0))

    def body(w_ref, m_ref, v_ref, gi_ref, g_ref, d_ref, nm_ref, nv_ref):
        g = gi_ref[...]
        d_ref[...], nm_ref[...], nv_ref[...] = _adamw_math(w_ref[...], m_ref[...], v_ref[...], g)
        g_ref[...] = g

    return pl.pallas_call(body, name=name, grid=(1,), in_specs=[blk] * 4, out_specs=[blk] * 4,
                          out_shape=[jax.ShapeDtypeStruct((r, c), F32)] * 4,
                          compiler_params=_params(("arbitrary",)))(w, m, v, g)


def _pair_add(gs, ra, core, name):
    _, _, r, c = gs.shape
    tr, tc = _tile2d(r, c)
    blk = pl.BlockSpec((None, tr, tc), lambda k, i, j, s: (k, i, j))

    def body(s_ref, g_ref, r_ref, o_ref, ob_ref):
        p = g_ref[...] + r_ref[...]
        o_ref[...] = p
        ob_ref[...] = p.astype(BF16)

    gspec = pltpu.PrefetchScalarGridSpec(
        num_scalar_prefetch=1, grid=(4, r // tr, c // tc),
        in_specs=[pl.BlockSpec((None, None, tr, tc), lambda k, i, j, s: (k, s[0], i, j)), blk], out_specs=[blk, blk])
    return pl.pallas_call(body, name=name, grid_spec=gspec,
                          out_shape=[jax.ShapeDtypeStruct((4, r, c), F32), jax.ShapeDtypeStruct((4, r, c), BF16)],
                          compiler_params=_params(("parallel", "parallel", "parallel")))(core, gs, ra)


def _small_reduce(gathered):
    nd, r, n = gathered.shape
    tn = 2048 if n % 2048 == 0 else n
    def body(g_ref, s_ref, t_ref):
        s = g_ref[0]
        for dv in range(1, nd):
            s = s + g_ref[dv]
        s_ref[...] = s
        t_ref[...] = jnp.broadcast_to(jnp.sum(s, axis=0, keepdims=True), (r, tn))

    return pl.pallas_call(
        body, name="small_reduce", grid=(n // tn,),
        in_specs=[pl.BlockSpec((nd, r, tn), lambda j: (0, 0, j))],
        out_specs=[pl.BlockSpec((r, tn), lambda j: (0, j))] * 2,
        out_shape=[jax.ShapeDtypeStruct((r, n), F32)] * 2, compiler_params=_params(("parallel",)),
    )(gathered)


HBM = pl.BlockSpec(memory_space=pltpu.HBM)


def _all_gather(arrs, name):
    n = len(arrs)

    def body(*refs):
        ins, outs = refs[:n], refs[n:2 * n]
        send, recv, lsem = refs[2 * n:]
        x, y, c = lax.axis_index("x"), lax.axis_index("y"), lax.axis_index("c")
        me, sib = (x, y, c), (x, y, 1 - c)
        chips = [(1 - x, y), (x, 1 - y), (1 - x, 1 - y)]

        def slot(w, p):
            return outs[w].at[4 * p[0] + 2 * p[1] + p[2]]

        def cp(w, k, block, to, src=None):
            return pltpu.make_async_remote_copy(
                src_ref=slot(w, block) if src is None else src, dst_ref=slot(w, block),
                send_sem=send.at[w * 7 + k], recv_sem=recv.at[w * 7 + k], device_id=to, device_id_type=MESH)

        mine = [pltpu.make_async_copy(ins[w], slot(w, me), lsem.at[w]) for w in range(n)]
        for m in mine:
            m.start()
        first = []
        for w in range(n):
            first.append(cp(w, 0, me, sib, src=ins[w]))
            first += [cp(w, 1 + j, me, (*chip, c), src=ins[w]) for j, chip in enumerate(chips)]
        for f in first:
            f.start()
        passed = []
        for j, chip in enumerate(chips):
            for w in range(n):
                cp(w, 1 + j, (*chip, c), me).wait_recv()
                fwd = cp(w, 4 + j, (*chip, c), sib)
                fwd.start()
                passed.append(fwd)
        for w in range(n):
            cp(w, 0, sib, me).wait_recv()
            for j, chip in enumerate(chips):
                cp(w, 4 + j, (*chip, 1 - c), me).wait_recv()
        for f in first + passed:
            f.wait_send()
        for m in mine:
            m.wait()

    return pl.pallas_call(
        body, name=name, in_specs=[HBM] * n, out_specs=[HBM] * n,
        out_shape=[jax.ShapeDtypeStruct((8,) + a.shape, a.dtype) for a in arrs],
        scratch_shapes=[pltpu.SemaphoreType.DMA((7 * n,)), pltpu.SemaphoreType.DMA((7 * n,)),
                        pltpu.SemaphoreType.DMA((n,))],
    )(*arrs)


def _sibling_exchange(arrs, name):
    n = len(arrs)

    def body(*refs):
        ins, outs = refs[:n], refs[n:2 * n]
        send, recv = refs[2 * n:]
        x, y, c = lax.axis_index("x"), lax.axis_index("y"), lax.axis_index("c")
        copies = [pltpu.make_async_remote_copy(
            src_ref=ins[w].at[:, 1 - c], dst_ref=outs[w], send_sem=send.at[w], recv_sem=recv.at[w],
            device_id=(x, y, 1 - c), device_id_type=MESH) for w in range(n)]
        for cpy in copies:
            cpy.start()
        for cpy in copies:
            cpy.wait()

    return pl.pallas_call(
        body, name=name, in_specs=[HBM] * n, out_specs=[HBM] * n,
        out_shape=[jax.ShapeDtypeStruct((a.shape[0],) + a.shape[2:], a.dtype) for a in arrs],
        scratch_shapes=[pltpu.SemaphoreType.DMA((n,)), pltpu.SemaphoreType.DMA((n,))],
    )(*arrs)


def _chip_exchange(arrs, name):
    n = len(arrs)

    def body(*refs):
        ins, outs = refs[:n], refs[n:2 * n]
        send, recv = refs[2 * n:]
        x, y, c = lax.axis_index("x"), lax.axis_index("y"), lax.axis_index("c")
        chips = [(1 - x, y), (x, 1 - y), (1 - x, 1 - y)]
        copies = []
        for w in range(n):
            for j, (cx, cy) in enumerate(chips):
                copies.append(pltpu.make_async_remote_copy(
                    src_ref=ins[w].at[2 * cx + cy], dst_ref=outs[w].at[j], send_sem=send.at[3 * w + j],
                    recv_sem=recv.at[3 * w + j], device_id=(cx, cy, c), device_id_type=MESH))
        for cpy in copies:
            cpy.start()
        for cpy in copies:
            cpy.wait()

    return pl.pallas_call(
        body, name=name, in_specs=[HBM] * n, out_specs=[HBM] * n,
        out_shape=[jax.ShapeDtypeStruct((3,) + a.shape[1:], a.dtype) for a in arrs],
        scratch_shapes=[pltpu.SemaphoreType.DMA((3 * n,)), pltpu.SemaphoreType.DMA((3 * n,))],
    )(*arrs)


def _pad_cols(a, to):
    return jnp.pad(a, ((0, 0), (0, to - a.shape[1])))


N_GLR = GLA_W + GLA_RANK
FF_SLAB = D_FF // 4
FF_SLAB_P = FFP // 4


def _prepare_sub1(gath):
    w_in = _gathered_full("w_in", gath["w_in"])
    w2 = gath["gla_gate_w2"].transpose(1, 0, 2).reshape(GLA_RANK, -1)
    return {"w_a": _pad_cols(w_in[:, :N_GLR], HA_W), "w_b": w_in[:, N_GLR:],
            "w2p": jnp.pad(w2, ((0, LANES - GLA_RANK), (0, 0)))}


def _prepare_ffn(gath, conv_b):
    padc = FF_SLAB_P - FF_SLAB
    f = jnp.pad(gath["ffn_w_in"], ((0, 0), (0, 0), (0, padc)))
    w_ffn = f.transpose(1, 0, 2).reshape(f.shape[1], 2 * FFP)
    wo = jnp.pad(gath["ffn_w_out"].reshape(4, FF_SLAB, -1), ((0, 0), (0, padc), (0, 0))).reshape(FFP, -1)
    cw = jnp.pad(gath["ffn_conv_w"], ((0, 0), (0, 0), (0, padc)))
    cb = jnp.pad(conv_b.reshape(8, FF_SLAB), ((0, 0), (0, padc)))
    cwb = jnp.concatenate([cw.transpose(1, 0, 2).reshape(3, 2 * FFP), cb.reshape(1, 2 * FFP),
                           jnp.zeros((4, 2 * FFP), F32)], axis=0)
    return {"w_ffn": w_ffn, "wo": wo, "cw": cwb}


def _unpad_ff(a):
    r = a.shape[0]
    return a.reshape(r, 4, FF_SLAB_P)[:, :, :FF_SLAB].reshape(r, D_FF)


def _grad_slabs(g):
    s = {"w_in": _to_slabs("w_in", jnp.concatenate([g["w_a"][:, :N_GLR], g["w_b"]], axis=1))}
    for n in ("w_out", "ca_wq", "ca_wkv", "ca_wo"):
        s[n] = _to_slabs(n, g[n])
    s["ffn_w_in"] = g["w_ffn_slabs"].reshape((4, 2) + g["w_ffn_slabs"].shape[1:])
    wo = g["wo"].reshape(4, FF_SLAB_P, -1)[:, :FF_SLAB]
    s["ffn_w_out"] = wo.reshape(4, 2, FF_SLAB // 2, wo.shape[-1])
    return s


def _local_step(x, mem, positions, target, p, small):
    t, d = x.shape
    w_a, w_b, w2p = p["w_a"], p["w_b"], p["w2p"]
    w_ffn, wo, cw = p["w_ffn"], p["wo"], p["cw"]
    wts = p
    tabs = _rope_tables(positions)
    xb = x.astype(BF16)
    memb = mem.astype(BF16)

    h_a = _matmul(xb, w_a, "nn", F32, 512, 640, d, "mm_h_a")
    h_b = _matmul(xb, w_b, "nn", F32, 512, 1024, d, "mm_h_b")
    o_g, o_raw, s_before = _gla_fwd(h_a, w2p, small["gla_gate_b"], small["gla_norm_g"])
    qr, kr = _rope_fwd(h_b, tabs)
    o_d_b, o_d, lse_tot = _dil_fwd_all(qr, kr, h_b)
    mixin = jnp.concatenate([o_g, o_d_b], axis=1)
    mix = _matmul(mixin, wts["w_out"], "nn", F32, 512, 1024, d, "mm_mix")
    x1, x1b, x1t = _ln_fwd(x, mix, small["ln1_g"], small["ln1_b"], "ln1_fwd")

    q_ca = _matmul(x1b, wts["ca_wq"], "nn", BF16, 512, 1024, d, "mm_caq")
    memkv = _matmul(memb, wts["ca_wkv"], "nn", BF16, mem.shape[0], 1024, d, "mm_memkv")
    o_c, o_ct = _ca_fwd(q_ca, memkv)
    ca_out = _matmul(o_c, wts["ca_wo"], "nn", F32, 512, 1024, d, "mm_cao")
    x2, x2b, x2t = _ln_fwd(x1, ca_out, small["ln2_g"], small["ln2_b"], "ln2_fwd")

    u0 = _matmul(x2b, w_ffn, "nn", BF16, 512, 512, d, "mm_u0")
    act, act_t = _swiglu_fwd(u0, cw)
    ffn = _matmul(act, wo, "nn", F32, 512, 512, FFP, "mm_ffn")

    dp3, dp3b, dg3, db3, loss_part = _ln_bwd(x2, ffn, small["ln3_g"], small["ln3_b"], target, True, "ln3_bwd")
    g_wo = _matmul(act_t, dp3b, "nn", F32, 512, 1024, t // 2, "mm_g_wo")
    dact = _matmul(dp3b, wo, "nt", BF16, 512, 512, d, "mm_dact")
    dug, duu, dcwg, dcwu = _swiglu_bwd(u0, cw, dact)
    g_ffn_in = _ffn_win_grad(x2t, dug, duu)
    dx2 = _matmul(dug, w_ffn, "nt", F32, 512, 512, FFP // 2, "mm_dx2_g", resid=dp3, resid_scale=ALPHA)
    dx2 = _matmul(duu, w_ffn, "nt", F32, 512, 512, FFP // 2, "mm_dx2_u", resid=dx2, b_k_off=2)

    dp2, dp2b, dg2, db2 = _ln_bwd(x1, ca_out, small["ln2_g"], small["ln2_b"], dx2, False, "ln2_bwd")
    g_cao = _matmul(o_ct, dp2b, "nn", F32, 512, 1024, t // 2, "mm_g_cao")
    do_c = _matmul(dp2b, wts["ca_wo"], "nt", BF16, 512, 1024, d, "mm_do_c")
    dq_ca, dmemkv = _ca_bwd(q_ca, memkv, do_c)
    g_caq = _matmul(x1t, dq_ca, "nn", F32, 512, 1024, t // 2, "mm_g_caq")
    g_cakv = _matmul(memb, dmemkv.astype(BF16), "tn", F32, 512, 1024, mem.shape[0], "mm_g_cakv")
    dx1 = _matmul(dq_ca, wts["ca_wq"], "nt", F32, 512, 1024, d, "mm_dx1", resid=dp2, resid_scale=ALPHA)

    dp1, dp1b, dg1, db1 = _ln_bwd(x, mix, small["ln1_g"], small["ln1_b"], dx1, False, "ln1_bwd")
    g_wout = _matmul(mixin, dp1b, "tn", F32, 512, 1024, 1024, "mm_g_wout")
    dmix = _matmul(dp1b, wts["w_out"], "nt", F32, 512, 1024, d, "mm_dmix")
    dh_a, dw2, dgate_b, dnorm_g = _gla_bwd(h_a, w2p, small["gla_gate_b"], small["gla_norm_g"], o_raw, s_before, dmix)
    dq_d, dk_d, dv_d = _dil_bwd_all(qr, kr, h_b, dmix, o_d, lse_tot)
    dh_b = _dil_dh(dq_d, dk_d, dv_d, tabs)
    g_wa = _matmul(xb, dh_a, "tn", F32, 512, 640, 1024, "mm_g_wa")
    g_wb = _matmul(xb, dh_b, "tn", F32, 512, 1024, 1024, "mm_g_wb")
    dx = _matmul(dh_a, w_a, "nt", F32, 512, 512, HA_W, "mm_dx_a", resid=dp1, resid_scale=ALPHA)
    dx = _matmul(dh_b, w_b, "nt", F32, 512, 512, HB_W, "mm_dx_b", resid=dx)

    grads = {"w_a": g_wa, "w_b": g_wb, "w_out": g_wout, "ca_wq": g_caq, "ca_wkv": g_cakv, "ca_wo": g_cao,
             "w_ffn_slabs": g_ffn_in, "wo": g_wo}
    small_parts = {
        "gla_gate_b": dgate_b, "gla_norm_g": dnorm_g, "ln1_g": dg1, "ln1_b": db1, "ln2_g": dg2, "ln2_b": db2,
        "ln3_g": dg3, "ln3_b": db3,
        "conv": jnp.concatenate([_unpad_ff(dcwg), _unpad_ff(dcwu)], axis=1),
        "gla_gate_w2": dw2[:GLA_RANK],
    }
    return loss_part, dx, grads, small_parts


BIG = ("w_in", "w_out", "ca_wq", "ca_wkv", "ca_wo", "ffn_w_in", "ffn_w_out")
COL_SHARDED = ("w_in", "ca_wkv", "ffn_w_in")
SMALL_ORDER = ("gla_gate_b", "gla_norm_g", "ln1_g", "ln1_b", "ln2_g", "ln2_b", "ln3_g", "ln3_b")


def _gathered_full(name, g):
    if name in COL_SHARDED:
        return g.transpose(1, 0, 2).reshape(g.shape[1], 8 * g.shape[2])
    return g.reshape(8 * g.shape[1], g.shape[2])


def _to_slabs(name, full):
    if name in COL_SHARDED:
        r, cc = full.shape
        s = full.reshape(r, 8, cc // 8).transpose(1, 0, 2)
    else:
        rr, c = full.shape
        s = full.reshape(8, rr // 8, c)
    return s.reshape((4, 2) + s.shape[1:])


def kernel(x, mem, positions, w_in, gla_gate_w2, gla_gate_b, gla_norm_g, w_out, ln1_g, ln1_b, ca_wq, ca_wkv, ca_wo, ln2_g, ln2_b, ffn_w_in, ffn_conv_w, ffn_conv_b, ffn_w_out, ln3_g, ln3_b, loss_target, m_w_in, m_gla_gate_w2, m_gla_gate_b, m_gla_norm_g, m_w_out, m_ln1_g, m_ln1_b, m_ca_wq, m_ca_wkv, m_ca_wo, m_ln2_g, m_ln2_b, m_ffn_w_in, m_ffn_conv_w, m_ffn_conv_b, m_ffn_w_out, m_ln3_g, m_ln3_b, v_w_in, v_gla_gate_w2, v_gla_gate_b, v_gla_norm_g, v_w_out, v_ln1_g, v_ln1_b, v_ca_wq, v_ca_wkv, v_ca_wo, v_ln2_g, v_ln2_b, v_ffn_w_in, v_ffn_conv_w, v_ffn_conv_b, v_ffn_w_out, v_ln3_g, v_ln3_b):
    weights = dict(w_in=w_in, gla_gate_w2=gla_gate_w2, gla_gate_b=gla_gate_b, gla_norm_g=gla_norm_g, w_out=w_out,
                   ln1_g=ln1_g, ln1_b=ln1_b, ca_wq=ca_wq, ca_wkv=ca_wkv, ca_wo=ca_wo, ln2_g=ln2_g, ln2_b=ln2_b,
                   ffn_w_in=ffn_w_in, ffn_conv_w=ffn_conv_w, ffn_conv_b=ffn_conv_b, ffn_w_out=ffn_w_out,
                   ln3_g=ln3_g, ln3_b=ln3_b)
    moms = dict(w_in=(m_w_in, v_w_in), gla_gate_w2=(m_gla_gate_w2, v_gla_gate_w2), gla_gate_b=(m_gla_gate_b, v_gla_gate_b),
                gla_norm_g=(m_gla_norm_g, v_gla_norm_g), w_out=(m_w_out, v_w_out), ln1_g=(m_ln1_g, v_ln1_g),
                ln1_b=(m_ln1_b, v_ln1_b), ca_wq=(m_ca_wq, v_ca_wq), ca_wkv=(m_ca_wkv, v_ca_wkv), ca_wo=(m_ca_wo, v_ca_wo),
                ln2_g=(m_ln2_g, v_ln2_g), ln2_b=(m_ln2_b, v_ln2_b), ffn_w_in=(m_ffn_w_in, v_ffn_w_in),
                ffn_conv_w=(m_ffn_conv_w, v_ffn_conv_w), ffn_conv_b=(m_ffn_conv_b, v_ffn_conv_b),
                ffn_w_out=(m_ffn_w_out, v_ffn_w_out), ln3_g=(m_ln3_g, v_ln3_g), ln3_b=(m_ln3_b, v_ln3_b))
    order = list(weights)
    xi, yi, ci = lax.axis_index("x"), lax.axis_index("y"), lax.axis_index("c")
    me = 4 * xi + 2 * yi + ci

    shards = [weights[n].astype(BF16)[0] for n in BIG] + [gla_gate_w2.astype(BF16)[0], ffn_conv_w[0]]
    gathered = _all_gather(shards, "ag_weights")
    gath = dict(zip(BIG + ("gla_gate_w2", "ffn_conv_w"), gathered))
    p = _prepare_sub1(gath)
    p.update({n: _gathered_full(n, gath[n]) for n in ("w_out", "ca_wq", "ca_wkv", "ca_wo")})
    p.update(_prepare_ffn(gath, ffn_conv_b))
    small = dict(gla_gate_b=gla_gate_b, gla_norm_g=gla_norm_g, ln1_g=ln1_g, ln1_b=ln1_b, ln2_g=ln2_g, ln2_b=ln2_b,
                 ln3_g=ln3_g, ln3_b=ln3_b)

    loss_part, dx, grads, small_parts = _local_step(x[0], mem[0], positions[0], loss_target[0], p, small)
    loss = lax.psum(jnp.sum(loss_part), ("x", "y", "c"))

    slab_of = _grad_slabs(grads)
    slabs = [slab_of[n] for n in BIG]
    from_sib = _sibling_exchange(slabs, "rs_sibling")
    core = ci.reshape(1).astype(jnp.int32)
    pair32, pair16 = [], []
    for n, s, r in zip(BIG, slabs, from_sib):
        p32, p16 = _pair_add(s, r, core, f"pair_add_{n}")
        pair32.append(p32)
        pair16.append(p16)
    from_chips = _chip_exchange(pair16, "rs_chips")
    chip = (2 * xi + yi).reshape(1).astype(jnp.int32)
    out = {}
    for n, p32, rc in zip(BIG, pair32, from_chips):
        m_, v_ = moms[n]
        out[n] = _adamw_big(weights[n], m_, v_, p32, rc, chip, f"adamw_{n}")

    packed = jnp.concatenate([small_parts[n] for n in SMALL_ORDER] + [small_parts["conv"],
                             small_parts["gla_gate_w2"].reshape(SUBLANES, -1)], axis=1)
    pad = (-packed.shape[1]) % 2048
    packed = jnp.pad(packed, ((0, 0), (0, pad)))
    (allp,) = _all_gather([packed], "ag_small")
    dev_sum, row_sum = _small_reduce(allp)
    off = 0
    for n in SMALL_ORDER:
        width = weights[n].shape[1]
        g = row_sum[0:1, off:off + width]
        off += width
        m_, v_ = moms[n]
        out[n] = _adamw(weights[n], m_, v_, g, f"adamw_{n}")
    conv_g = dev_sum[:, off:off + 2 * D_FF]
    off += 2 * D_FF
    g_cb = conv_g[3:4]
    out["ffn_conv_b"] = _adamw(ffn_conv_b, m_ffn_conv_b, v_ffn_conv_b, g_cb, "adamw_ffn_conv_b")
    wsh = ffn_conv_w.shape[2]
    g_cw = lax.dynamic_slice_in_dim(conv_g[0:3], me * wsh, wsh, axis=1)
    out["ffn_conv_w"] = _adamw(ffn_conv_w[0], m_ffn_conv_w[0], v_ffn_conv_w[0], g_cw, "adamw_ffn_conv_w")
    w2_g = dev_sum[:, off:off + GLA_RANK * GLA_HEADS * GLA_DK // SUBLANES].reshape(GLA_RANK, GLA_HEADS * GLA_DK)
    wsh2 = gla_gate_w2.shape[2]
    g_w2 = lax.dynamic_slice_in_dim(w2_g, me * wsh2, wsh2, axis=1)
    out["gla_gate_w2"] = _adamw(gla_gate_w2[0], m_gla_gate_w2[0], v_gla_gate_w2[0], g_w2, "adamw_gla_gate_w2")

    def shaped(n, a):
        return a.reshape(weights[n].shape)

    res = [loss, dx[None]]
    for k in range(4):
        res += [shaped(n, out[n][k]) for n in order]
    return tuple(res)


def _adamw_big(w, m, v, p32, rc, chip, name):
    _, r, c = w.shape
    tr, tc = _tile2d(r, c)
    blk = pl.BlockSpec((None, tr, tc), lambda i, j, s: (0, i, j))
    own = pl.BlockSpec((None, tr, tc), lambda i, j, s: (s[0], i, j))
    others = [pl.BlockSpec((None, tr, tc), lambda i, j, s, k=k: (k, i, j)) for k in range(3)]

    def body(s_ref, w_ref, m_ref, v_ref, p_ref, r0_ref, r1_ref, r2_ref, g_ref, d_ref, nm_ref, nv_ref):
        g = ((p_ref[...] + r0_ref[...].astype(F32)) + r1_ref[...].astype(F32)) + r2_ref[...].astype(F32)
        d_ref[...], nm_ref[...], nv_ref[...] = _adamw_math(w_ref[...], m_ref[...], v_ref[...], g)
        g_ref[...] = g

    gs = pltpu.PrefetchScalarGridSpec(num_scalar_prefetch=1, grid=(r // tr, c // tc),
                                      in_specs=[blk, blk, blk, own] + others, out_specs=[blk] * 4)
    return pl.pallas_call(body, name=name, grid_spec=gs, out_shape=[jax.ShapeDtypeStruct((1, r, c), F32)] * 4,
                          compiler_params=_params(("parallel", "parallel")))(chip, w, m, v, p32, rc, rc, rc)
```

```python
import functools
import math

import jax
import jax.numpy as jnp
from jax import lax
from jax.experimental import pallas as pl
from jax.experimental.pallas import tpu as pltpu

F32 = jnp.float32
BF16 = jnp.bfloat16
MESH = pl.DeviceIdType.MESH

D_MODEL = 2048
LN_EPS = 1e-5
GLA_HEADS = 4
GLA_DV = 256
GLA_DK = 128
GLA_RANK = 16
GLA_TAU = 16.0
GLA_CHUNK = 64
DIL_HD = 128
DIL_HEADS = 8
DIL_BAND = 128
DIL_DILATIONS = (1, 4, 16)
ROPE_THETA = 500000.0
ROPE_DIMS = 32
CA_HEADS = 4
CA_HD = 512
D_FF = 5504
ALPHA = 2.0 ** 0.25
ADAM_LR = 0.001
ADAM_B1 = 0.9
ADAM_B2 = 0.999
ADAM_EPS = 1e-08
ADAM_WD = 0.01
ADAM_STEP = 10

LANES = 128
SUBLANES = 8
VMEM_LIMIT = 56 * 1024 * 1024

GLA_W = 2 * GLA_HEADS * GLA_DK + 2 * GLA_HEADS * GLA_DV
HA_W = GLA_W + LANES
HB_W = 3 * DIL_HEADS * DIL_HD
FFP = 5632
NEG = -1e30


def _params(sem):
    return pltpu.CompilerParams(dimension_semantics=sem, vmem_limit_bytes=VMEM_LIMIT)


def _sigmoid(x):
    return 1.0 / (1.0 + jnp.exp(-x))


def _dot(a, b, dn, precision=None):
    return lax.dot_general(a, b, (dn, ((), ())), preferred_element_type=F32, precision=precision)


NN = ((1,), (0,))
NT = ((1,), (1,))
TN = ((0,), (0,))


def _bf(v):
    return v if v.dtype == BF16 else v.astype(BF16)


def _matmul(a, b, kind, out_dtype, tm, tn, tk, name, resid=None, resid_scale=1.0, b_k_off=0, b_slabs=False,
            out_slabs=False):
    if b_slabs:
        assert kind != "nt" and b.shape[2] == tn
        k2, n = b.shape[1], b.shape[0] * tn
    elif kind == "nt":
        n, k2 = b.shape
    else:
        k2, n = b.shape
    (k, m) = a.shape if kind == "tn" else a.shape[::-1]
    assert k2 >= k and (k2 == k or not b_slabs) and m % tm == 0 and n % tn == 0 and k % tk == 0, \
        (name, a.shape, b.shape, tm, tn, tk)
    nk = k // tk
    dn = {"nn": NN, "nt": NT, "tn": TN}[kind]
    a_spec = pl.BlockSpec((tk, tm), lambda i, j, kk: (kk, i)) if kind == "tn" else pl.BlockSpec((tm, tk), lambda i, j, kk: (i, kk))
    if b_slabs:
        b_spec = pl.BlockSpec((None, tk, tn), lambda i, j, kk: (j, kk, 0))
    elif kind == "nt":
        b_spec = pl.BlockSpec((tn, tk), lambda i, j, kk: (j, kk + b_k_off))
    else:
        b_spec = pl.BlockSpec((tk, tn), lambda i, j, kk: (kk + b_k_off, j))
    if out_slabs:
        o_spec = pl.BlockSpec((None, tm, tn), lambda i, j, kk: (j, i, 0))
        o_shape = (n // tn, m, tn)
    else:
        o_spec = pl.BlockSpec((tm, tn), lambda i, j, kk: (i, j))
        o_shape = (m, n)
    has_resid = resid is not None

    def body(*refs):
        if has_resid:
            a_ref, b_ref, r_ref, o_ref = refs[:4]
        else:
            a_ref, b_ref, o_ref = refs[:3]
            r_ref = None
        part = _dot(_bf(a_ref[...]), _bf(b_ref[...]), dn)

        def finish(acc):
            if has_resid:
                acc = acc + resid_scale * r_ref[...].astype(F32)
            o_ref[...] = acc.astype(out_dtype)

        if nk == 1:
            finish(part)
        else:
            acc_ref = refs[-1]
            kk = pl.program_id(2)

            @pl.when(kk == 0)
            def _():
                acc_ref[...] = part

            @pl.when(kk > 0)
            def _():
                acc_ref[...] += part

            @pl.when(kk == nk - 1)
            def _():
                finish(acc_ref[...])

    in_specs = [a_spec, b_spec] + ([o_spec] if has_resid else [])
    args = (a, b) + ((resid,) if has_resid else ())
    return pl.pallas_call(
        body, name=name, out_shape=jax.ShapeDtypeStruct(o_shape, out_dtype),
        grid=(m // tm, n // tn, nk), in_specs=in_specs, out_specs=o_spec,
        scratch_shapes=[pltpu.VMEM((tm, tn), F32)] if nk > 1 else [],
        compiler_params=_params(("parallel", "parallel", "arbitrary")),
    )(*args)


def _ln_core(xres, f):
    p = ALPHA * xres + f
    mu = jnp.mean(p, axis=-1, keepdims=True)
    xc = p - mu
    var = jnp.mean(xc * xc, axis=-1, keepdims=True)
    rstd = lax.rsqrt(var + LN_EPS)
    return xc * rstd, rstd


def _rows8(v):
    r, c = v.shape
    return jnp.sum(v.reshape(r // SUBLANES, SUBLANES, c), axis=0)


def _ln_fwd(xres, f, g, b, name, tr=256):
    t, d = xres.shape
    row = pl.BlockSpec((tr, d), lambda i: (i, 0))
    vec = pl.BlockSpec((1, d), lambda i: (0, 0))

    def body(x_ref, f_ref, g_ref, b_ref, y_ref, yb_ref, yt_ref):
        xhat, _ = _ln_core(x_ref[...], f_ref[...])
        y = xhat * g_ref[...] + b_ref[...]
        y_ref[...] = y
        yb = y.astype(BF16)
        yb_ref[...] = yb
        yt_ref[...] = yb.T

    return pl.pallas_call(
        body, name=name, grid=(t // tr,), in_specs=[row, row, vec, vec],
        out_specs=[row, row, pl.BlockSpec((d, tr), lambda i: (0, i))],
        out_shape=[jax.ShapeDtypeStruct((t, d), F32), jax.ShapeDtypeStruct((t, d), BF16),
                   jax.ShapeDtypeStruct((d, t), BF16)],
        compiler_params=_params(("parallel",)),
    )(xres, f, g, b)


def _ln_bwd(xres, f, g, b, dy_or_target, loss_head, name, tr=256):
    t, d = xres.shape
    row = pl.BlockSpec((tr, d), lambda i: (i, 0))
    vec = pl.BlockSpec((1, d), lambda i: (0, 0))
    acc = pl.BlockSpec((SUBLANES, d), lambda i: (0, 0))
    lacc = pl.BlockSpec((SUBLANES, LANES), lambda i: (0, 0))

    def body(x_ref, f_ref, g_ref, b_ref, t_ref, dp_ref, dpb_ref, dg_ref, db_ref, *rest):
        i = pl.program_id(0)
        xhat, rstd = _ln_core(x_ref[...], f_ref[...])
        if loss_head:
            err = xhat * g_ref[...] + b_ref[...] - t_ref[...]
            dy = err * (1.0 / d)
            sq = err * err
            lanes = sq[:, :LANES]
            for kk in range(1, d // LANES):
                lanes = lanes + sq[:, kk * LANES:(kk + 1) * LANES]
            lpart = _rows8(lanes) * (0.5 / d)
        else:
            dy = t_ref[...]
        dxh = dy * g_ref[...]
        m1 = jnp.mean(dxh, axis=-1, keepdims=True)
        m2 = jnp.mean(dxh * xhat, axis=-1, keepdims=True)
        dp = rstd * (dxh - m1 - xhat * m2)
        dp_ref[...] = dp
        dpb_ref[...] = dp.astype(BF16)
        dgp = _rows8(dy * xhat)
        dbp = _rows8(dy)

        @pl.when(i == 0)
        def _():
            dg_ref[...] = dgp
            db_ref[...] = dbp
            if loss_head:
                rest[0][...] = lpart

        @pl.when(i > 0)
        def _():
            dg_ref[...] += dgp
            db_ref[...] += dbp
            if loss_head:
                rest[0][...] += lpart

    out_shape = [jax.ShapeDtypeStruct((t, d), F32), jax.ShapeDtypeStruct((t, d), BF16),
                 jax.ShapeDtypeStruct((SUBLANES, d), F32), jax.ShapeDtypeStruct((SUBLANES, d), F32)]
    out_specs = [row, row, acc, acc]
    if loss_head:
        out_shape.append(jax.ShapeDtypeStruct((SUBLANES, LANES), F32))
        out_specs.append(lacc)
    return pl.pallas_call(
        body, name=name, grid=(t // tr,), in_specs=[row, row, vec, vec, row], out_specs=out_specs,
        out_shape=out_shape, compiler_params=_params(("arbitrary",)),
    )(xres, f, g, b, dy_or_target)


def _gla_gates(glr, w2, gb):
    z = _dot(_bf(glr), w2, NN) + gb
    lg = (jnp.minimum(z, 0.0) - jnp.log(1.0 + jnp.exp(-jnp.abs(z)))) * (1.0 / GLA_TAU)
    c = z.shape[0]
    ri = lax.broadcasted_iota(jnp.int32, (c, c), 0)
    ci = lax.broadcasted_iota(jnp.int32, (c, c), 1)
    tri = (ci <= ri).astype(F32)
    bcum = _dot(tri, lg, NN, precision=lax.Precision.HIGHEST)
    blast = jnp.sum(lg, axis=0, keepdims=True)
    return z, bcum, blast, tri


def _gla_specs(t):
    c = GLA_CHUNK
    return c, t // c


def _gla_fwd(h_a, w2p, gate_b, norm_g):
    t = h_a.shape[0]
    c, n = _gla_specs(t)
    hk, hv = GLA_HEADS * GLA_DK, GLA_HEADS * GLA_DV
    scale = GLA_DK ** -0.5

    def body(q_ref, k_ref, v_ref, r_ref, glr_ref, w2_ref, gb_ref, ng_ref, og_ref, oraw_ref, sb_ref, st_ref):
        i = pl.program_id(0)

        @pl.when(i == 0)
        def _():
            st_ref[...] = jnp.zeros_like(st_ref)

        _, bcum, blast, _ = _gla_gates(glr_ref[...], w2_ref[...], gb_ref[...])
        ri = lax.broadcasted_iota(jnp.int32, (c, c), 0)
        ci = lax.broadcasted_iota(jnp.int32, (c, c), 1)
        causal = ci <= ri
        for h in range(GLA_HEADS):
            ks = slice(h * GLA_DK, (h + 1) * GLA_DK)
            vs = slice(h * GLA_DV, (h + 1) * GLA_DV)
            b_h, bl_h = bcum[:, ks], blast[:, ks]
            q_h, k_h = q_ref[:, ks], k_ref[:, ks]
            v_h = _bf(v_ref[:, vs])
            qi = _bf(q_h * scale * jnp.exp(b_h))
            ki = _bf(k_h * jnp.exp(-b_h))
            ke = _bf(k_h * jnp.exp(bl_h - b_h))
            st = st_ref[h]
            sb_ref[0, h] = st
            a = jnp.where(causal, _dot(qi, ki, NT), 0.0)
            o = _dot(_bf(a), v_h, NN) + _dot(qi, _bf(st), NT)
            st_ref[h] = st * jnp.exp(bl_h) + _dot(v_h, ke, TN)
            oraw_ref[:, vs] = o
            mu = jnp.mean(o, axis=-1, keepdims=True)
            oc = o - mu
            var = jnp.mean(oc * oc, axis=-1, keepdims=True)
            xh = oc * lax.rsqrt(var + LN_EPS)
            r_h = r_ref[:, vs]
            og_ref[:, vs] = (xh * ng_ref[:, vs] * (r_h * _sigmoid(r_h))).astype(BF16)

    return pl.pallas_call(
        body, name="gla_fwd", grid=(n,),
        in_specs=[pl.BlockSpec((c, hk), lambda i: (i, 0)), pl.BlockSpec((c, hk), lambda i: (i, 1)),
                  pl.BlockSpec((c, hv), lambda i: (i, 1)), pl.BlockSpec((c, hv), lambda i: (i, 2)),
                  pl.BlockSpec((c, LANES), lambda i: (i, GLA_W // LANES)),
                  pl.BlockSpec((LANES, hk), lambda i: (0, 0)), pl.BlockSpec((1, hk), lambda i: (0, 0)),
                  pl.BlockSpec((1, hv), lambda i: (0, 0))],
        out_specs=[pl.BlockSpec((c, hv), lambda i: (i, 0)), pl.BlockSpec((c, hv), lambda i: (i, 0)),
                   pl.BlockSpec((1, GLA_HEADS, GLA_DV, GLA_DK), lambda i: (i, 0, 0, 0))],
        out_shape=[jax.ShapeDtypeStruct((t, hv), BF16), jax.ShapeDtypeStruct((t, hv), F32),
                   jax.ShapeDtypeStruct((n, GLA_HEADS, GLA_DV, GLA_DK), F32)],
        scratch_shapes=[pltpu.VMEM((GLA_HEADS, GLA_DV, GLA_DK), F32)],
        compiler_params=_params(("arbitrary",)),
    )(h_a, h_a, h_a, h_a, h_a, w2p, gate_b, norm_g)


def _gla_bwd(h_a, w2p, gate_b, norm_g, o_raw, s_before, dmix):
    t = h_a.shape[0]
    c, n = _gla_specs(t)
    hk, hv = GLA_HEADS * GLA_DK, GLA_HEADS * GLA_DV
    scale = GLA_DK ** -0.5
    rev = lambda i: n - 1 - i

    def body(q_ref, k_ref, v_ref, r_ref, glr_ref, w2_ref, gb_ref, ng_ref, oraw_ref, sb_ref, do_ref,
             dh_ref, dw2_ref, dgb_ref, dng_ref, dst_ref):
        i = pl.program_id(0)

        @pl.when(i == 0)
        def _():
            dst_ref[...] = jnp.zeros_like(dst_ref)

        glr = glr_ref[...]
        z, bcum, blast, tri = _gla_gates(glr, w2_ref[...], gb_ref[...])
        ri = lax.broadcasted_iota(jnp.int32, (c, c), 0)
        ci = lax.broadcasted_iota(jnp.int32, (c, c), 1)
        causal = ci <= ri
        dlg_parts = []
        dng_parts = []
        for h in range(GLA_HEADS):
            ks = slice(h * GLA_DK, (h + 1) * GLA_DK)
            vs = slice(h * GLA_DV, (h + 1) * GLA_DV)
            o = oraw_ref[:, vs]
            mu = jnp.mean(o, axis=-1, keepdims=True)
            oc = o - mu
            var = jnp.mean(oc * oc, axis=-1, keepdims=True)
            rstd = lax.rsqrt(var + LN_EPS)
            xh = oc * rstd
            r_h = r_ref[:, vs]
            sg = _sigmoid(r_h)
            silu = r_h * sg
            dout = do_ref[:, vs]
            ng = ng_ref[:, vs]
            dng_parts.append(_rows8(dout * xh * silu))
            dr = dout * xh * ng * (sg * (1.0 + r_h * (1.0 - sg)))
            dxh = dout * ng * silu
            m1 = jnp.mean(dxh, axis=-1, keepdims=True)
            m2 = jnp.mean(dxh * xh, axis=-1, keepdims=True)
            do_raw = _bf(rstd * (dxh - m1 - xh * m2))
            b_h, bl_h = bcum[:, ks], blast[:, ks]
            q_h, k_h = q_ref[:, ks], k_ref[:, ks]
            v_h = _bf(v_ref[:, vs])
            eb, enb, eend = jnp.exp(b_h), jnp.exp(-b_h), jnp.exp(bl_h - b_h)
            decay = jnp.exp(bl_h)
            qi_f, ki_f, ke_f = q_h * scale * eb, k_h * enb, k_h * eend
            qi, ki, ke = _bf(qi_f), _bf(ki_f), _bf(ke_f)
            st = sb_ref[0, h]
            dst = dst_ref[h]
            dst_b = _bf(dst)
            a = _bf(jnp.where(causal, _dot(qi, ki, NT), 0.0))
            da = _bf(jnp.where(causal, _dot(do_raw, v_h, NT), 0.0))
            dv = _dot(a, do_raw, TN) + _dot(ke, dst_b, NT)
            dqi = _dot(da, ki, NN) + _dot(do_raw, _bf(st), NN)
            dki = _dot(da, qi, TN)
            dke = _dot(v_h, dst_b, NN)
            dst_ref[h] = _dot(do_raw, qi, TN) + dst * decay
            dbl = decay * jnp.sum(st * dst, axis=0, keepdims=True) + jnp.sum(dke * ke_f, axis=0, keepdims=True)
            dbc = dqi * qi_f - dki * ki_f - dke * ke_f
            dlg_parts.append(_dot(tri, dbc, TN, precision=lax.Precision.HIGHEST) + dbl)
            dh_ref[:, ks] = (dqi * eb * scale).astype(BF16)
            dh_ref[:, hk + h * GLA_DK: hk + (h + 1) * GLA_DK] = (dki * enb + dke * eend).astype(BF16)
            dh_ref[:, 2 * hk + h * GLA_DV: 2 * hk + (h + 1) * GLA_DV] = dv.astype(BF16)
            dh_ref[:, 2 * hk + hv + h * GLA_DV: 2 * hk + hv + (h + 1) * GLA_DV] = dr.astype(BF16)
        dlg = jnp.concatenate(dlg_parts, axis=1)
        dz = dlg * (1.0 / GLA_TAU) * _sigmoid(-z)
        dz_b = _bf(dz)
        dh_ref[:, GLA_W:] = _dot(dz_b, w2_ref[...], NT).astype(BF16)
        dw2p = _dot(_bf(glr), dz_b, TN)
        dgbp = _rows8(dz)
        dngp = jnp.concatenate(dng_parts, axis=1)

        @pl.when(i == 0)
        def _():
            dw2_ref[...] = dw2p
            dgb_ref[...] = dgbp
            dng_ref[...] = dngp

        @pl.when(i > 0)
        def _():
            dw2_ref[...] += dw2p
            dgb_ref[...] += dgbp
            dng_ref[...] += dngp

    return pl.pallas_call(
        body, name="gla_bwd", grid=(n,),
        in_specs=[pl.BlockSpec((c, hk), lambda i: (rev(i), 0)), pl.BlockSpec((c, hk), lambda i: (rev(i), 1)),
                  pl.BlockSpec((c, hv), lambda i: (rev(i), 1)), pl.BlockSpec((c, hv), lambda i: (rev(i), 2)),
                  pl.BlockSpec((c, LANES), lambda i: (rev(i), GLA_W // LANES)),
                  pl.BlockSpec((LANES, hk), lambda i: (0, 0)), pl.BlockSpec((1, hk), lambda i: (0, 0)),
                  pl.BlockSpec((1, hv), lambda i: (0, 0)),
                  pl.BlockSpec((c, hv), lambda i: (rev(i), 0)),
                  pl.BlockSpec((1, GLA_HEADS, GLA_DV, GLA_DK), lambda i: (rev(i), 0, 0, 0)),
                  pl.BlockSpec((c, hv), lambda i: (rev(i), 0))],
        out_specs=[pl.BlockSpec((c, HA_W), lambda i: (rev(i), 0)),
                   pl.BlockSpec((LANES, hk), lambda i: (0, 0)),
                   pl.BlockSpec((SUBLANES, hk), lambda i: (0, 0)),
                   pl.BlockSpec((SUBLANES, hv), lambda i: (0, 0))],
        out_shape=[jax.ShapeDtypeStruct((t, HA_W), BF16), jax.ShapeDtypeStruct((LANES, hk), F32),
                   jax.ShapeDtypeStruct((SUBLANES, hk), F32), jax.ShapeDtypeStruct((SUBLANES, hv), F32)],
        scratch_shapes=[pltpu.VMEM((GLA_HEADS, GLA_DV, GLA_DK), F32)],
        compiler_params=_params(("arbitrary",)),
    )(h_a, h_a, h_a, h_a, h_a, w2p, gate_b, norm_g, o_raw, s_before, dmix)


def _rope_tables(positions):
    half = ROPE_DIMS // 2
    inv_freq = ROPE_THETA ** (-jnp.arange(0, ROPE_DIMS, 2, dtype=F32) / ROPE_DIMS)
    ang = positions.astype(F32).reshape(-1, 1) * inv_freq
    cos, sin = jnp.cos(ang), jnp.sin(ang)
    t = cos.shape[0]
    one = jnp.ones((t, DIL_HD - ROPE_DIMS), F32)
    zero = jnp.zeros((t, DIL_HD - ROPE_DIMS), F32)
    zh = jnp.zeros((t, half), F32)
    return (jnp.concatenate([cos, cos, one], axis=1), jnp.concatenate([-sin, zh, zero], axis=1),
            jnp.concatenate([zh, sin, zero], axis=1))


def _rope_apply(x, c, s1, s2):
    half = ROPE_DIMS // 2
    return x * c + pltpu.roll(x, DIL_HD - half, 1) * s1 + pltpu.roll(x, half, 1) * s2


def _rope_apply_t(dy, c, s1, s2):
    half = ROPE_DIMS // 2
    return dy * c + pltpu.roll(dy * s1, half, 1) + pltpu.roll(dy * s2, DIL_HD - half, 1)


def _rope_fwd(h_b, tabs, tr=256):
    t = h_b.shape[0]
    w = DIL_HEADS * DIL_HD
    scale = DIL_HD ** -0.5
    tab = pl.BlockSpec((tr, DIL_HD), lambda i: (i, 0))
    outb = pl.BlockSpec((tr, w), lambda i: (i, 0))

    def body(q_ref, k_ref, c_ref, s1_ref, s2_ref, qo_ref, ko_ref):
        c, s1, s2 = c_ref[...], s1_ref[...], s2_ref[...]
        for h in range(DIL_HEADS):
            hs = slice(h * DIL_HD, (h + 1) * DIL_HD)
            qo_ref[:, hs] = _rope_apply(q_ref[:, hs] * scale, c, s1, s2)
            ko_ref[:, hs] = _rope_apply(k_ref[:, hs], c, s1, s2)

    return pl.pallas_call(
        body, name="rope_fwd", grid=(t // tr,),
        in_specs=[pl.BlockSpec((tr, w), lambda i: (i, 0)), pl.BlockSpec((tr, w), lambda i: (i, 1)), tab, tab, tab],
        out_specs=[outb, outb],
        out_shape=[jax.ShapeDtypeStruct((t, w), F32)] * 2,
        compiler_params=_params(("parallel",)),
    )(h_b, h_b, *tabs)


def _dil_dh(dq, dk, dv, tabs, tr=256):
    t, w = dq.shape
    scale = DIL_HD ** -0.5
    tab = pl.BlockSpec((tr, DIL_HD), lambda i: (i, 0))
    inb = pl.BlockSpec((tr, w), lambda i: (i, 0))

    def body(dq_ref, dk_ref, dv_ref, c_ref, s1_ref, s2_ref, o_ref):
        c, s1, s2 = c_ref[...], s1_ref[...], s2_ref[...]
        for h in range(DIL_HEADS):
            hs = slice(h * DIL_HD, (h + 1) * DIL_HD)
            o_ref[:, h * DIL_HD:(h + 1) * DIL_HD] = (_rope_apply_t(dq_ref[:, hs], c, s1, s2) * scale).astype(BF16)
            o_ref[:, w + h * DIL_HD: w + (h + 1) * DIL_HD] = _rope_apply_t(dk_ref[:, hs], c, s1, s2).astype(BF16)
        o_ref[:, 2 * w:] = dv_ref[...].astype(BF16)

    return pl.pallas_call(
        body, name="dil_dh", grid=(t // tr,), in_specs=[inb] * 3 + [tab] * 3,
        out_specs=pl.BlockSpec((tr, 3 * w), lambda i: (i, 0)),
        out_shape=jax.ShapeDtypeStruct((t, 3 * w), BF16), compiler_params=_params(("parallel",)),
    )(dq, dk, dv, *tabs)


BANDS = 8


def _to_branch(a, d):
    t, w = a.shape
    return a.reshape(t // d, d, w // DIL_HD, DIL_HD).transpose(1, 2, 0, 3).reshape(-1, DIL_HD)


def _from_branch(a, d, t):
    hds = a.shape[0] // t
    return a.reshape(d, hds, t // d, DIL_HD).transpose(2, 0, 1, 3).reshape(t, hds * DIL_HD)


def _band_masks(not_first):
    r = lax.broadcasted_iota(jnp.int32, (DIL_BAND, 2 * DIL_BAND), 0)
    c = lax.broadcasted_iota(jnp.int32, (DIL_BAND, 2 * DIL_BAND), 1)
    nf = jnp.full((DIL_BAND, 2 * DIL_BAND), not_first, jnp.int32)
    look_back = jnp.logical_and(jnp.logical_and(c < DIL_BAND, c >= r), nf > 0)
    own_band = jnp.logical_and(c >= DIL_BAND, (c - DIL_BAND) <= r)
    return jnp.logical_or(look_back, own_band)


def _dil_fwd(q, k, v, nb, name):
    rows = q.shape[0]
    blk = BANDS * DIL_BAND
    steps = rows // blk
    main = pl.BlockSpec((blk, DIL_HD), lambda i: (i, 0))
    prev = pl.BlockSpec((DIL_BAND, DIL_HD), lambda i: (jnp.maximum(i * BANDS - 1, 0), 0))

    def body(q_ref, k_ref, v_ref, kp_ref, vp_ref, o_ref, l_ref):
        i = pl.program_id(0)
        for j in range(BANDS):
            lo, hi = j * DIL_BAND, (j + 1) * DIL_BAND
            if j == 0:
                kcat = jnp.concatenate([kp_ref[...], k_ref[lo:hi, :]], axis=0)
                vcat = jnp.concatenate([vp_ref[...], v_ref[lo:hi, :]], axis=0)
            else:
                kcat = k_ref[lo - DIL_BAND:hi, :]
                vcat = v_ref[lo - DIL_BAND:hi, :]
            not_first = (((i * BANDS + j) % nb) != 0).astype(jnp.int32)
            s = jnp.where(_band_masks(not_first), _dot(q_ref[lo:hi, :], kcat, NT), NEG)
            m = jnp.max(s, axis=-1, keepdims=True)
            p = jnp.exp(s - m)
            den = jnp.sum(p, axis=-1, keepdims=True)
            o_ref[lo:hi, :] = _dot(_bf(p), vcat, NN) / den
            l_ref[lo:hi, :] = jnp.broadcast_to(m + jnp.log(den), (DIL_BAND, DIL_HD))

    return pl.pallas_call(
        body, name=name, grid=(steps,), in_specs=[main, main, main, prev, prev], out_specs=[main, main],
        out_shape=[jax.ShapeDtypeStruct((rows, DIL_HD), F32)] * 2, compiler_params=_params(("parallel",)),
    )(q, k, v, k, v)


def _dil_bwd(q, k, v, do, lse, dd, nb, name):
    rows = q.shape[0]
    blk = BANDS * DIL_BAND
    steps = rows // blk
    last_band = rows // DIL_BAND - 1
    main = pl.BlockSpec((blk, DIL_HD), lambda i: (i, 0))
    prev = pl.BlockSpec((DIL_BAND, DIL_HD), lambda i: (jnp.maximum(i * BANDS - 1, 0), 0))
    nxt = pl.BlockSpec((DIL_BAND, DIL_HD), lambda i: (jnp.minimum(i * BANDS + BANDS, last_band), 0))

    def body(q_ref, k_ref, v_ref, do_ref, l_ref, dd_ref, kp_ref, vp_ref, qn_ref, don_ref, ln_ref, ddn_ref,
             dq_ref, dk_ref, dv_ref, ak_ref, av_ref):
        i = pl.program_id(0)
        ak_ref[...] = jnp.zeros_like(ak_ref)
        av_ref[...] = jnp.zeros_like(av_ref)
        for j in range(BANDS + 1):
            lo, hi = j * DIL_BAND, (j + 1) * DIL_BAND
            if j == 0:
                kcat = jnp.concatenate([kp_ref[...], k_ref[lo:hi, :]], axis=0)
                vcat = jnp.concatenate([vp_ref[...], v_ref[lo:hi, :]], axis=0)
            elif j < BANDS:
                kcat = k_ref[lo - DIL_BAND:hi, :]
                vcat = v_ref[lo - DIL_BAND:hi, :]
            else:
                kcat = jnp.concatenate([k_ref[lo - DIL_BAND:lo, :], k_ref[lo - DIL_BAND:lo, :]], axis=0)
                vcat = jnp.concatenate([v_ref[lo - DIL_BAND:lo, :], v_ref[lo - DIL_BAND:lo, :]], axis=0)
            if j < BANDS:
                qj, doj, lj, ddj = q_ref[lo:hi, :], do_ref[lo:hi, :], l_ref[lo:hi, :], dd_ref[lo:hi, :]
            else:
                qj, doj, lj, ddj = qn_ref[...], don_ref[...], ln_ref[...], ddn_ref[...]
            not_first = (((i * BANDS + j) % nb) != 0).astype(jnp.int32)
            mask = _band_masks(not_first)
            if j == BANDS:
                cidx = lax.broadcasted_iota(jnp.int32, mask.shape, 1)
                mask = jnp.logical_and(mask, cidx < DIL_BAND)
            s = jnp.where(mask, _dot(qj, kcat, NT), NEG)
            p = jnp.exp(s - jnp.concatenate([lj, lj], axis=1))
            dp = _dot(doj, vcat, NT)
            ds = _bf(p * (dp - jnp.concatenate([ddj, ddj], axis=1)))
            if j < BANDS:
                dq_ref[lo:hi, :] = _dot(ds, kcat, NN)
            ak_ref[lo:hi + DIL_BAND, :] += _dot(ds, qj, TN)
            av_ref[lo:hi + DIL_BAND, :] += _dot(_bf(p), doj, TN)
        dk_ref[...] = ak_ref[DIL_BAND:DIL_BAND + blk, :]
        dv_ref[...] = av_ref[DIL_BAND:DIL_BAND + blk, :]

    return pl.pallas_call(
        body, name=name, grid=(steps,),
        in_specs=[main] * 6 + [prev, prev] + [nxt] * 4, out_specs=[main] * 3,
        out_shape=[jax.ShapeDtypeStruct((rows, DIL_HD), F32)] * 3,
        scratch_shapes=[pltpu.VMEM((blk + 2 * DIL_BAND, DIL_HD), F32)] * 2,
        compiler_params=_params(("parallel",)),
    )(q, k, v, do, lse, dd, k, v, q, do, lse, dd)


def _dil_merge(os_, ls_, tr=256):
    t, w = os_[0].shape
    blk = pl.BlockSpec((tr, w), lambda i: (i, 0))

    def body(o1, o2, o3, l1, l2, l3, ob_ref, of_ref, lt_ref):
        a, b, c = l1[...], l2[...], l3[...]
        m = jnp.maximum(jnp.maximum(a, b), c)
        ea, eb, ec = jnp.exp(a - m), jnp.exp(b - m), jnp.exp(c - m)
        den = ea + eb + ec
        o = (ea * o1[...] + eb * o2[...] + ec * o3[...]) / den
        ob_ref[...] = o.astype(BF16)
        of_ref[...] = o
        lt_ref[...] = m + jnp.log(den)

    return pl.pallas_call(
        body, name="dil_merge", grid=(t // tr,), in_specs=[blk] * 6, out_specs=[blk] * 3,
        out_shape=[jax.ShapeDtypeStruct((t, w), BF16), jax.ShapeDtypeStruct((t, w), F32),
                   jax.ShapeDtypeStruct((t, w), F32)],
        compiler_params=_params(("parallel",)),
    )(*os_, *ls_)


def _dil_bwd_prep(dmix, o_d, tr=256):
    t, w = o_d.shape
    blk = pl.BlockSpec((tr, w), lambda i: (i, 0))

    def body(do_ref, o_ref, dob_ref, dd_ref):
        do = do_ref[...]
        prod = do * o_ref[...]
        dob_ref[...] = do.astype(BF16)
        for h in range(DIL_HEADS):
            hs = slice(h * DIL_HD, (h + 1) * DIL_HD)
            dd_ref[:, hs] = jnp.broadcast_to(jnp.sum(prod[:, hs], axis=-1, keepdims=True), (tr, DIL_HD))

    return pl.pallas_call(
        body, name="dil_bwd_prep", grid=(t // tr,),
        in_specs=[pl.BlockSpec((tr, w), lambda i: (i, 1)), blk], out_specs=[blk, blk],
        out_shape=[jax.ShapeDtypeStruct((t, w), BF16), jax.ShapeDtypeStruct((t, w), F32)],
        compiler_params=_params(("parallel",)),
    )(dmix, o_d)


def _gather_rows(dst_ref, src_ref, t, d, cast=None):
    n = t // d
    for r in range(d):
        v = src_ref[pl.ds(r, n, stride=d), :] if d > 1 else src_ref[...]
        dst_ref[r * n:(r + 1) * n, :] = v if cast is None else v.astype(cast)


def _tri_mask():
    r = lax.broadcasted_iota(jnp.int32, (DIL_BAND, DIL_BAND), 0)
    c = lax.broadcasted_iota(jnp.int32, (DIL_BAND, DIL_BAND), 1)
    return c <= r


def _dil_fwd_all(qr, kr, h_b):
    t = qr.shape[0]
    nbands = t // DIL_BAND
    nbr = len(DIL_DILATIONS)
    hoff = DIL_HEADS

    def col(off):
        return pl.BlockSpec((t, DIL_HD), lambda h: (0, off + h), pipeline_mode=pl.Buffered(1))

    outb = pl.BlockSpec((t, DIL_HD), lambda h: (0, h))

    def body(q_ref, k_ref, v_ref, ob_ref, of_ref, lt_ref, qs, ks, vs, os_, ls_, *br):
        obr, lbr = br[:nbr], br[nbr:]
        for bi, d in enumerate(DIL_DILATIONS):
            n = t // d
            nb = n // DIL_BAND
            _gather_rows(qs, q_ref, t, d, BF16)
            _gather_rows(ks, k_ref, t, d, BF16)
            _gather_rows(vs, v_ref, t, d, BF16)
            s = jnp.where(_tri_mask(), _dot(qs[0:DIL_BAND, :], ks[0:DIL_BAND, :], NT), NEG)
            m = jnp.max(s, axis=-1, keepdims=True)
            pr = jnp.exp(s - m)
            den = jnp.sum(pr, axis=-1, keepdims=True)
            os_[0:DIL_BAND, :] = _dot(_bf(pr), vs[0:DIL_BAND, :], NN) / den
            ls_[0:DIL_BAND, :] = jnp.broadcast_to(m + jnp.log(den), (DIL_BAND, DIL_HD))

            def band(b, carry, nb=nb):
                st = pl.multiple_of((b - 1) * DIL_BAND, DIL_BAND)
                cur = pl.ds(st + DIL_BAND, DIL_BAND)
                both = pl.ds(st, 2 * DIL_BAND)
                not_first = ((b % nb) != 0).astype(jnp.int32)
                s = jnp.where(_band_masks(not_first), _dot(qs[cur, :], ks[both, :], NT), NEG)
                m = jnp.max(s, axis=-1, keepdims=True)
                pr = jnp.exp(s - m)
                den = jnp.sum(pr, axis=-1, keepdims=True)
                os_[cur, :] = _dot(_bf(pr), vs[both, :], NN) / den
                ls_[cur, :] = jnp.broadcast_to(m + jnp.log(den), (DIL_BAND, DIL_HD))
                return carry

            lax.fori_loop(1, nbands, band, 0, unroll=4)
            for r in range(d):
                dst = pl.ds(r, n, stride=d) if d > 1 else slice(None)
                obr[bi][dst, :] = os_[r * n:(r + 1) * n, :]
                lbr[bi][dst, :] = ls_[r * n:(r + 1) * n, :]
        rows = 512
        for c0 in range(0, t, rows):
            sl = slice(c0, c0 + rows)
            la, lb, lc = lbr[0][sl, :], lbr[1][sl, :], lbr[2][sl, :]
            m = jnp.maximum(jnp.maximum(la, lb), lc)
            ea, eb, ec = jnp.exp(la - m), jnp.exp(lb - m), jnp.exp(lc - m)
            den = ea + eb + ec
            o = (ea * obr[0][sl, :] + eb * obr[1][sl, :] + ec * obr[2][sl, :]) / den
            ob_ref[sl, :] = o.astype(BF16)
            of_ref[sl, :] = o
            lt_ref[sl, :] = m + jnp.log(den)

    w = DIL_HEADS * DIL_HD
    vm = lambda dt: pltpu.VMEM((t, DIL_HD), dt)
    return pl.pallas_call(
        body, name="dil_fwd", grid=(DIL_HEADS,), in_specs=[col(0), col(0), col(2 * hoff)],
        out_specs=[outb, outb, outb],
        out_shape=[jax.ShapeDtypeStruct((t, w), BF16), jax.ShapeDtypeStruct((t, w), F32),
                   jax.ShapeDtypeStruct((t, w), F32)],
        scratch_shapes=[vm(BF16)] * 3 + [vm(F32)] * 2 + [vm(F32)] * (2 * nbr),
        compiler_params=_params(("parallel",)),
    )(qr, kr, h_b)


def _dil_bwd_all(qr, kr, h_b, dmix, o_d, lse_tot):
    t = qr.shape[0]
    nbands = t // DIL_BAND
    hoff = DIL_HEADS

    def col(off):
        return pl.BlockSpec((t, DIL_HD), lambda h: (0, off + h), pipeline_mode=pl.Buffered(1))

    outb = pl.BlockSpec((t, DIL_HD), lambda h: (0, h))

    def body(q_ref, k_ref, v_ref, do_ref, o_ref, l_ref, dq_ref, dk_ref, dv_ref,
             qs, ks, vs, dos, lss, dds, dqs, acck, accv):
        for bi, d in enumerate(DIL_DILATIONS):
            n = t // d
            nb = n // DIL_BAND
            _gather_rows(qs, q_ref, t, d, BF16)
            _gather_rows(ks, k_ref, t, d, BF16)
            _gather_rows(vs, v_ref, t, d, BF16)
            _gather_rows(dos, do_ref, t, d, BF16)
            _gather_rows(lss, l_ref, t, d)
            for r in range(d):
                src = pl.ds(r, n, stride=d) if d > 1 else slice(None)
                prod = do_ref[src, :] * o_ref[src, :]
                dds[r * n:(r + 1) * n, :] = jnp.broadcast_to(jnp.sum(prod, axis=-1, keepdims=True), (n, DIL_HD))
            acck[...] = jnp.zeros_like(acck)
            accv[...] = jnp.zeros_like(accv)
            b0 = slice(0, DIL_BAND)
            s = jnp.where(_tri_mask(), _dot(qs[b0, :], ks[b0, :], NT), NEG)
            pr = jnp.exp(s - lss[b0, :])
            ds = _bf(pr * (_dot(dos[b0, :], vs[b0, :], NT) - dds[b0, :]))
            dqs[b0, :] = _dot(ds, ks[b0, :], NN)
            acck[DIL_BAND:2 * DIL_BAND, :] += _dot(ds, qs[b0, :], TN)
            accv[DIL_BAND:2 * DIL_BAND, :] += _dot(_bf(pr), dos[b0, :], TN)

            def band(b, carry, nb=nb):
                st = pl.multiple_of((b - 1) * DIL_BAND, DIL_BAND)
                cur = pl.ds(st + DIL_BAND, DIL_BAND)
                both = pl.ds(st, 2 * DIL_BAND)
                acc_rows = pl.ds(st + DIL_BAND, 2 * DIL_BAND)
                not_first = ((b % nb) != 0).astype(jnp.int32)
                qb, dob, lb, ddb = qs[cur, :], dos[cur, :], lss[cur, :], dds[cur, :]
                kcat, vcat = ks[both, :], vs[both, :]
                s = jnp.where(_band_masks(not_first), _dot(qb, kcat, NT), NEG)
                pr = jnp.exp(s - jnp.concatenate([lb, lb], axis=1))
                ds = _bf(pr * (_dot(dob, vcat, NT) - jnp.concatenate([ddb, ddb], axis=1)))
                dqs[cur, :] = _dot(ds, kcat, NN)
                acck[acc_rows, :] += _dot(ds, qb, TN)
                accv[acc_rows, :] += _dot(_bf(pr), dob, TN)
                return carry

            lax.fori_loop(1, nbands, band, 0, unroll=2)
            for r in range(d):
                lo = r * n
                if d == 1:
                    dq_ref[...] = dqs[...]
                    dk_ref[...] = acck[DIL_BAND:DIL_BAND + t, :]
                    dv_ref[...] = accv[DIL_BAND:DIL_BAND + t, :]
                else:
                    dst = pl.ds(r, n, stride=d)
                    dq_ref[dst, :] = dq_ref[dst, :] + dqs[lo:lo + n, :]
                    dk_ref[dst, :] = dk_ref[dst, :] + acck[DIL_BAND + lo:DIL_BAND + lo + n, :]
                    dv_ref[dst, :] = dv_ref[dst, :] + accv[DIL_BAND + lo:DIL_BAND + lo + n, :]

    w = DIL_HEADS * DIL_HD
    vm = lambda dt, extra=0: pltpu.VMEM((t + extra, DIL_HD), dt)
    return pl.pallas_call(
        body, name="dil_bwd", grid=(DIL_HEADS,),
        in_specs=[col(0), col(0), col(2 * hoff), col(hoff), col(0), col(0)], out_specs=[outb] * 3,
        out_shape=[jax.ShapeDtypeStruct((t, w), F32)] * 3,
        scratch_shapes=[vm(BF16)] * 4 + [vm(F32)] * 3 + [vm(F32, DIL_BAND)] * 2,
        compiler_params=_params(("parallel",)),
    )(qr, kr, h_b, dmix, o_d, lse_tot)


def _ca_fwd(q, memkv, tq=512):
    t, d = q.shape
    m = memkv.shape[0]
    scale = CA_HD ** -0.5

    def body(q_ref, k_ref, v_ref, o_ref, ot_ref):
        for h in range(CA_HEADS):
            hs = slice(h * CA_HD, (h + 1) * CA_HD)
            s = _dot(q_ref[:, hs], k_ref[:, hs], NT) * scale
            p = jnp.exp(s - jnp.max(s, axis=-1, keepdims=True))
            p = p / jnp.sum(p, axis=-1, keepdims=True)
            o = _dot(_bf(p), v_ref[:, hs], NN).astype(BF16)
            o_ref[:, hs] = o
            ot_ref[hs, :] = o.T

    return pl.pallas_call(
        body, name="ca_fwd", grid=(t // tq,),
        in_specs=[pl.BlockSpec((tq, d), lambda i: (i, 0)), pl.BlockSpec((m, d), lambda i: (0, 0)),
                  pl.BlockSpec((m, d), lambda i: (0, 1))],
        out_specs=[pl.BlockSpec((tq, d), lambda i: (i, 0)), pl.BlockSpec((d, tq), lambda i: (0, i))],
        out_shape=[jax.ShapeDtypeStruct((t, d), BF16), jax.ShapeDtypeStruct((d, t), BF16)],
        compiler_params=_params(("parallel",)),
    )(q, memkv, memkv)


def _ca_bwd(q, memkv, do, tq=512):
    t, d = q.shape
    m = memkv.shape[0]
    scale = CA_HD ** -0.5

    def body(q_ref, k_ref, v_ref, do_ref, dq_ref, dkv_ref):
        i = pl.program_id(0)

        @pl.when(i == 0)
        def _():
            dkv_ref[...] = jnp.zeros_like(dkv_ref)

        for h in range(CA_HEADS):
            hs = slice(h * CA_HD, (h + 1) * CA_HD)
            q_h, k_h, v_h, do_h = q_ref[:, hs], k_ref[:, hs], v_ref[:, hs], do_ref[:, hs]
            s = _dot(q_h, k_h, NT) * scale
            p = jnp.exp(s - jnp.max(s, axis=-1, keepdims=True))
            p = p / jnp.sum(p, axis=-1, keepdims=True)
            dp = _dot(do_h, v_h, NT)
            ds = _bf(p * (dp - jnp.sum(p * dp, axis=-1, keepdims=True)) * scale)
            dq_ref[:, hs] = _dot(ds, k_h, NN).astype(BF16)
            dkv_ref[:, hs] += _dot(ds, q_h, TN)
            dkv_ref[:, d + h * CA_HD: d + (h + 1) * CA_HD] += _dot(_bf(p), do_h, TN)

    return pl.pallas_call(
        body, name="ca_bwd", grid=(t // tq,),
        in_specs=[pl.BlockSpec((tq, d), lambda i: (i, 0)), pl.BlockSpec((m, d), lambda i: (0, 0)),
                  pl.BlockSpec((m, d), lambda i: (0, 1)), pl.BlockSpec((tq, d), lambda i: (i, 0))],
        out_specs=[pl.BlockSpec((tq, d), lambda i: (i, 0)), pl.BlockSpec((m, 2 * d), lambda i: (0, 0))],
        out_shape=[jax.ShapeDtypeStruct((t, d), BF16), jax.ShapeDtypeStruct((m, 2 * d), F32)],
        compiler_params=_params(("arbitrary",)),
    )(q, memkv, memkv, do)


STRIP = 256


def _shift_down(u, n, row):
    return jnp.where(row >= n, pltpu.roll(u, n, 0), 0.0)


def _shift_up(u, n, row):
    t = u.shape[0]
    return jnp.where(row < t - n, pltpu.roll(u, t - n, 0), 0.0)


def _conv(u, cw_ref, row):
    return ((cw_ref[3:4, :] + cw_ref[0:1, :] * _shift_down(u, 2, row)) + cw_ref[1:2, :] * _shift_down(u, 1, row)) \
        + cw_ref[2:3, :] * u


def _swiglu_fwd(u0, cw):
    t, w = u0.shape[0], u0.shape[1] // 2
    ns = w // STRIP
    col = pl.BlockSpec((t, STRIP), lambda j: (0, j))
    col_up = pl.BlockSpec((t, STRIP), lambda j: (0, ns + j))
    cws = pl.BlockSpec((SUBLANES, STRIP), lambda j: (0, j))
    cws_up = pl.BlockSpec((SUBLANES, STRIP), lambda j: (0, ns + j))

    def body(g_ref, u_ref, cg_ref, cu_ref, a_ref, at_ref):
        row = lax.broadcasted_iota(jnp.int32, (t, STRIP), 0)
        gate = _conv(g_ref[...].astype(F32), cg_ref, row)
        up = _conv(u_ref[...].astype(F32), cu_ref, row)
        act = (gate * _sigmoid(gate) * up).astype(BF16)
        a_ref[...] = act
        at_ref[...] = act.T

    return pl.pallas_call(
        body, name="swiglu_fwd", grid=(ns,), in_specs=[col, col_up, cws, cws_up],
        out_specs=[col, pl.BlockSpec((STRIP, t), lambda j: (j, 0))],
        out_shape=[jax.ShapeDtypeStruct((t, w), BF16), jax.ShapeDtypeStruct((w, t), BF16)],
        compiler_params=_params(("parallel",)),
    )(u0, u0, cw, cw)


def _swiglu_bwd(u0, cw, da):
    t, w = u0.shape[0], u0.shape[1] // 2
    ns = w // STRIP
    col = pl.BlockSpec((t, STRIP), lambda j: (0, j))
    col_up = pl.BlockSpec((t, STRIP), lambda j: (0, ns + j))
    cws = pl.BlockSpec((SUBLANES, STRIP), lambda j: (0, j))
    cws_up = pl.BlockSpec((SUBLANES, STRIP), lambda j: (0, ns + j))

    def conv_bwd(du, u0, cw_ref, row, du0_ref, du0t_ref, dcw_ref):
        du0 = (cw_ref[2:3, :] * du + cw_ref[1:2, :] * _shift_up(du, 1, row)) + cw_ref[0:1, :] * _shift_up(du, 2, row)
        du0 = du0.astype(BF16)
        du0_ref[...] = du0
        du0t_ref[...] = du0.T
        dcw_ref[0:1, :] = jnp.sum(du * _shift_down(u0, 2, row), axis=0, keepdims=True)
        dcw_ref[1:2, :] = jnp.sum(du * _shift_down(u0, 1, row), axis=0, keepdims=True)
        dcw_ref[2:3, :] = jnp.sum(du * u0, axis=0, keepdims=True)
        dcw_ref[3:4, :] = jnp.sum(du, axis=0, keepdims=True)
        dcw_ref[4:8, :] = jnp.zeros((4, STRIP), F32)

    def body(g_ref, u_ref, cg_ref, cu_ref, da_ref, dg0_ref, du0_ref, dg0t_ref, du0t_ref, dcg_ref, dcu_ref):
        row = lax.broadcasted_iota(jnp.int32, (t, STRIP), 0)
        g0, up0 = g_ref[...].astype(F32), u_ref[...].astype(F32)
        gate = _conv(g0, cg_ref, row)
        up = _conv(up0, cu_ref, row)
        sg = _sigmoid(gate)
        da = da_ref[...].astype(F32)
        dgate = da * up * (sg * (1.0 + gate * (1.0 - sg)))
        dup = da * (gate * sg)
        conv_bwd(dgate, g0, cg_ref, row, dg0_ref, dg0t_ref, dcg_ref)
        conv_bwd(dup, up0, cu_ref, row, du0_ref, du0t_ref, dcu_ref)

    colt = pl.BlockSpec((STRIP, t), lambda j: (j, 0))
    return pl.pallas_call(
        body, name="swiglu_bwd", grid=(ns,), in_specs=[col, col_up, cws, cws_up, col],
        out_specs=[col, col, colt, colt, cws, cws],
        out_shape=[jax.ShapeDtypeStruct((t, w), BF16), jax.ShapeDtypeStruct((t, w), BF16),
                   jax.ShapeDtypeStruct((w, t), BF16), jax.ShapeDtypeStruct((w, t), BF16),
                   jax.ShapeDtypeStruct((SUBLANES, w), F32), jax.ShapeDtypeStruct((SUBLANES, w), F32)],
        compiler_params=_params(("parallel",)),
    )(u0, u0, cw, cw, da)


def _ffn_win_grad(dugt, duut, x2b, tn=512):
    t, d = x2b.shape
    tk = t // 2
    nk = t // tk
    sp, sw = FF_SLAB_P, FF_SLAB

    def body(ag_ref, au_ref, b_ref, o_ref, acc_ref):
        j, kk = pl.program_id(0), pl.program_id(2)

        @pl.when(kk == 0)
        def _():
            acc_ref[...] = jnp.zeros_like(acc_ref)

        @pl.when(j < 4)
        def _():
            acc_ref[...] += _dot(ag_ref[...], b_ref[...], NN)

        @pl.when(j >= 4)
        def _():
            acc_ref[...] += _dot(au_ref[...], b_ref[...], NN)

        @pl.when(kk == nk - 1)
        def _():
            o_ref[...] = acc_ref[:sw, :]

    return pl.pallas_call(
        body, name="mm_g_ffn_in", grid=(8, d // tn, nk),
        in_specs=[pl.BlockSpec((sp, tk), lambda j, n, kk: (jnp.minimum(j, 3), kk)),
                  pl.BlockSpec((sp, tk), lambda j, n, kk: (jnp.maximum(j - 4, 0), kk)),
                  pl.BlockSpec((tk, tn), lambda j, n, kk: (kk, n))],
        out_specs=pl.BlockSpec((None, sw, tn), lambda j, n, kk: (j, 0, n)),
        out_shape=jax.ShapeDtypeStruct((8, sw, d), F32),
        scratch_shapes=[pltpu.VMEM((sp, tn), F32)],
        compiler_params=_params(("parallel", "parallel", "arbitrary")),
    )(dugt, duut, x2b)


def _tile2d(r, c, limit=1 << 20):
    tr, tc = r, c
    while tr * tc * 4 > limit:
        if tr % (2 * SUBLANES) == 0:
            tr //= 2
        elif tc % (2 * LANES) == 0:
            tc //= 2
        else:
            break
    return tr, tc


def _adamw_math(w, m, v, g):
    c1 = 1.0 - ADAM_B1 ** ADAM_STEP
    c2 = 1.0 - ADAM_B2 ** ADAM_STEP
    mm = ADAM_B1 * m + (1.0 - ADAM_B1) * g
    vv = ADAM_B2 * v + (1.0 - ADAM_B2) * (g * g)
    delta = -ADAM_LR * ((mm / c1) / (jnp.sqrt(vv / c2) + ADAM_EPS) + ADAM_WD * w)
    return delta, mm, vv


def _adamw(w, m, v, g, name):
    r, c = w.shape
    blk = pl.BlockSpec((r, c), lambda i: (0, 0))

    def body(w_ref, m_ref, v_ref, gi_ref, g_ref, d_ref, nm_ref, nv_ref):
        g = gi_ref[...]
        d_ref[...], nm_ref[...], nv_ref[...] = _adamw_math(w_ref[...], m_ref[...], v_ref[...], g)
        g_ref[...] = g

    return pl.pallas_call(body, name=name, grid=(1,), in_specs=[blk] * 4, out_specs=[blk] * 4,
                          out_shape=[jax.ShapeDtypeStruct((r, c), F32)] * 4,
                          compiler_params=_params(("arbitrary",)))(w, m, v, g)


def _pair_add(gs, ra, core, name):
    _, _, r, c = gs.shape
    tr, tc = _tile2d(r, c)
    blk = pl.BlockSpec((None, tr, tc), lambda k, i, j, s: (k, i, j))

    def body(s_ref, g_ref, r_ref, o_ref, ob_ref):
        p = g_ref[...] + r_ref[...]
        o_ref[...] = p
        ob_ref[...] = p.astype(BF16)

    gspec = pltpu.PrefetchScalarGridSpec(
        num_scalar_prefetch=1, grid=(4, r // tr, c // tc),
        in_specs=[pl.BlockSpec((None, None, tr, tc), lambda k, i, j, s: (k, s[0], i, j)), blk], out_specs=[blk, blk])
    return pl.pallas_call(body, name=name, grid_spec=gspec,
                          out_shape=[jax.ShapeDtypeStruct((4, r, c), F32), jax.ShapeDtypeStruct((4, r, c), BF16)],
                          compiler_params=_params(("parallel", "parallel", "parallel")))(core, gs, ra)


def _small_reduce(gathered):
    nd, r, n = gathered.shape
    tn = 2048 if n % 2048 == 0 else n
    def body(g_ref, s_ref, t_ref):
        s = g_ref[0]
        for dv in range(1, nd):
            s = s + g_ref[dv]
        s_ref[...] = s
        t_ref[...] = jnp.broadcast_to(jnp.sum(s, axis=0, keepdims=True), (r, tn))

    return pl.pallas_call(
        body, name="small_reduce", grid=(n // tn,),
        in_specs=[pl.BlockSpec((nd, r, tn), lambda j: (0, 0, j))],
        out_specs=[pl.BlockSpec((r, tn), lambda j: (0, j))] * 2,
        out_shape=[jax.ShapeDtypeStruct((r, n), F32)] * 2, compiler_params=_params(("parallel",)),
    )(gathered)


HBM = pl.BlockSpec(memory_space=pltpu.HBM)


def _all_gather(arrs, name):
    n = len(arrs)

    def body(*refs):
        ins, outs = refs[:n], refs[n:2 * n]
        send, recv, lsem = refs[2 * n:]
        x, y, c = lax.axis_index("x"), lax.axis_index("y"), lax.axis_index("c")
        me, sib = (x, y, c), (x, y, 1 - c)
        chips = [(1 - x, y), (x, 1 - y), (1 - x, 1 - y)]

        def slot(w, p):
            return outs[w].at[4 * p[0] + 2 * p[1] + p[2]]

        def cp(w, k, block, to, src=None):
            return pltpu.make_async_remote_copy(
                src_ref=slot(w, block) if src is None else src, dst_ref=slot(w, block),
                send_sem=send.at[w * 7 + k], recv_sem=recv.at[w * 7 + k], device_id=to, device_id_type=MESH)

        mine = [pltpu.make_async_copy(ins[w], slot(w, me), lsem.at[w]) for w in range(n)]
        for m in mine:
            m.start()
        first = []
        for w in range(n):
            first.append(cp(w, 0, me, sib, src=ins[w]))
            first += [cp(w, 1 + j, me, (*chip, c), src=ins[w]) for j, chip in enumerate(chips)]
        for f in first:
            f.start()
        passed = []
        for j, chip in enumerate(chips):
            for w in range(n):
                cp(w, 1 + j, (*chip, c), me).wait_recv()
                fwd = cp(w, 4 + j, (*chip, c), sib)
                fwd.start()
                passed.append(fwd)
        for w in range(n):
            cp(w, 0, sib, me).wait_recv()
            for j, chip in enumerate(chips):
                cp(w, 4 + j, (*chip, 1 - c), me).wait_recv()
        for f in first + passed:
            f.wait_send()
        for m in mine:
            m.wait()

    return pl.pallas_call(
        body, name=name, in_specs=[HBM] * n, out_specs=[HBM] * n,
        out_shape=[jax.ShapeDtypeStruct((8,) + a.shape, a.dtype) for a in arrs],
        scratch_shapes=[pltpu.SemaphoreType.DMA((7 * n,)), pltpu.SemaphoreType.DMA((7 * n,)),
                        pltpu.SemaphoreType.DMA((n,))],
    )(*arrs)


def _sibling_exchange(arrs, name):
    n = len(arrs)

    def body(*refs):
        ins, outs = refs[:n], refs[n:2 * n]
        send, recv = refs[2 * n:]
        x, y, c = lax.axis_index("x"), lax.axis_index("y"), lax.axis_index("c")
        copies = [pltpu.make_async_remote_copy(
            src_ref=ins[w].at[:, 1 - c], dst_ref=outs[w], send_sem=send.at[w], recv_sem=recv.at[w],
            device_id=(x, y, 1 - c), device_id_type=MESH) for w in range(n)]
        for cpy in copies:
            cpy.start()
        for cpy in copies:
            cpy.wait()

    return pl.pallas_call(
        body, name=name, in_specs=[HBM] * n, out_specs=[HBM] * n,
        out_shape=[jax.ShapeDtypeStruct((a.shape[0],) + a.shape[2:], a.dtype) for a in arrs],
        scratch_shapes=[pltpu.SemaphoreType.DMA((n,)), pltpu.SemaphoreType.DMA((n,))],
    )(*arrs)


def _chip_exchange(arrs, name):
    n = len(arrs)

    def body(*refs):
        ins, outs = refs[:n], refs[n:2 * n]
        send, recv = refs[2 * n:]
        x, y, c = lax.axis_index("x"), lax.axis_index("y"), lax.axis_index("c")
        chips = [(1 - x, y), (x, 1 - y), (1 - x, 1 - y)]
        copies = []
        for w in range(n):
            for j, (cx, cy) in enumerate(chips):
                copies.append(pltpu.make_async_remote_copy(
                    src_ref=ins[w].at[2 * cx + cy], dst_ref=outs[w].at[j], send_sem=send.at[3 * w + j],
                    recv_sem=recv.at[3 * w + j], device_id=(cx, cy, c), device_id_type=MESH))
        for cpy in copies:
            cpy.start()
        for cpy in copies:
            cpy.wait()

    return pl.pallas_call(
        body, name=name, in_specs=[HBM] * n, out_specs=[HBM] * n,
        out_shape=[jax.ShapeDtypeStruct((3,) + a.shape[1:], a.dtype) for a in arrs],
        scratch_shapes=[pltpu.SemaphoreType.DMA((3 * n,)), pltpu.SemaphoreType.DMA((3 * n,))],
    )(*arrs)


def _pad_cols(a, to):
    return jnp.pad(a, ((0, 0), (0, to - a.shape[1])))


N_GLR = GLA_W + GLA_RANK
FF_SLAB = D_FF // 4
FF_SLAB_P = FFP // 4


TRANSPOSED = ("w_in", "ffn_w_in")


def _prepare_sub1(gath):
    w_in_t = gath["w_in"].reshape(-1, gath["w_in"].shape[2])
    w2 = jnp.concatenate([gath["gla_gate_w2"][s] for s in range(8)], axis=1)
    return {"w_a_t": jnp.pad(w_in_t[:N_GLR], ((0, HA_W - N_GLR), (0, 0))), "w_b_t": w_in_t[N_GLR:],
            "w2p": jnp.pad(w2, ((0, LANES - GLA_RANK), (0, 0)))}


def _prepare_ffn(gath, conv_b):
    padc = FF_SLAB_P - FF_SLAB
    f = jnp.pad(gath["ffn_w_in"], ((0, 0), (0, padc), (0, 0)))
    w_ffn_t = f.reshape(2 * FFP, f.shape[2])
    wo = jnp.pad(gath["ffn_w_out"].reshape(4, FF_SLAB, -1), ((0, 0), (0, padc), (0, 0))).reshape(FFP, -1)
    cw = jnp.pad(gath["ffn_conv_w"], ((0, 0), (0, 0), (0, padc)))
    cb = jnp.pad(conv_b.reshape(8, 1, FF_SLAB), ((0, 0), (0, 0), (0, padc)))
    rows = jnp.concatenate([cw, cb, jnp.zeros((8, 4, FF_SLAB_P), F32)], axis=1)
    cwb = jnp.concatenate([rows[s] for s in range(8)], axis=1)
    return {"w_ffn_t": w_ffn_t, "wo": wo, "cw": cwb}


def _unpad_ff(a):
    r = a.shape[0]
    return a.reshape(r, 4, FF_SLAB_P)[:, :, :FF_SLAB].reshape(r, D_FF)


def _grad_slabs(g):
    w_in_t = jnp.concatenate([g["w_a_t"][:N_GLR], g["w_b_t"]], axis=0)
    s = {"w_in": w_in_t.reshape(4, 2, w_in_t.shape[0] // 8, w_in_t.shape[1])}
    for n in ("w_out", "ca_wq", "ca_wo"):
        s[n] = _to_slabs(n, g[n])
    for n in ("ca_wkv", "ffn_w_in"):
        s[n] = g[n].reshape((4, 2) + g[n].shape[1:])
    wo = g["wo"].reshape(4, FF_SLAB_P, -1)[:, :FF_SLAB]
    s["ffn_w_out"] = wo.reshape(4, 2, FF_SLAB // 2, wo.shape[-1])
    return s


def _local_step(x, mem, positions, target, p, small):
    t, d = x.shape
    w_a_t, w_b_t, w2p = p["w_a_t"], p["w_b_t"], p["w2p"]
    w_ffn_t, wo, cw = p["w_ffn_t"], p["wo"], p["cw"]
    wts = p
    tabs = _rope_tables(positions)
    xb = x.astype(BF16)
    memb = mem.astype(BF16)

    h_a = _matmul(xb, w_a_t, "nt", F32, 512, 640, d, "mm_h_a")
    h_b = _matmul(xb, w_b_t, "nt", F32, 512, 1024, d, "mm_h_b")
    o_g, o_raw, s_before = _gla_fwd(h_a, w2p, small["gla_gate_b"], small["gla_norm_g"])
    qr, kr = _rope_fwd(h_b, tabs)
    o_d_b, o_d, lse_tot = _dil_fwd_all(qr, kr, h_b)
    mixin = jnp.concatenate([o_g, o_d_b], axis=1)
    mix = _matmul(mixin, wts["w_out"], "nn", F32, 512, 1024, d, "mm_mix")
    x1, x1b, x1t = _ln_fwd(x, mix, small["ln1_g"], small["ln1_b"], "ln1_fwd")

    q_ca = _matmul(x1b, wts["ca_wq"], "nn", BF16, 512, 1024, d, "mm_caq")
    kvw = wts["ca_wkv"].shape[2]
    memkv = _matmul(memb, wts["ca_wkv"], "nn", BF16, mem.shape[0], kvw, d, "mm_memkv", b_slabs=True)
    o_c, o_ct = _ca_fwd(q_ca, memkv)
    ca_out = _matmul(o_c, wts["ca_wo"], "nn", F32, 512, 1024, d, "mm_cao")
    x2, x2b, x2t = _ln_fwd(x1, ca_out, small["ln2_g"], small["ln2_b"], "ln2_fwd")

    u0 = _matmul(x2b, w_ffn_t, "nt", BF16, 512, 512, d, "mm_u0")
    act, act_t = _swiglu_fwd(u0, cw)
    ffn = _matmul(act, wo, "nn", F32, 512, 512, FFP, "mm_ffn")

    dp3, dp3b, dg3, db3, loss_part = _ln_bwd(x2, ffn, small["ln3_g"], small["ln3_b"], target, True, "ln3_bwd")
    g_wo = _matmul(act_t, dp3b, "nn", F32, 512, 1024, t // 2, "mm_g_wo")
    dact = _matmul(dp3b, wo, "nt", BF16, 512, 512, d, "mm_dact")
    dug, duu, dug_t, duu_t, dcwg, dcwu = _swiglu_bwd(u0, cw, dact)
    g_ffn_in = _ffn_win_grad(dug_t, duu_t, x2b)
    dx2 = _matmul(dug, w_ffn_t, "nn", F32, 512, 512, FFP // 2, "mm_dx2_g", resid=dp3, resid_scale=ALPHA)
    dx2 = _matmul(duu, w_ffn_t, "nn", F32, 512, 512, FFP // 2, "mm_dx2_u", resid=dx2, b_k_off=2)

    dp2, dp2b, dg2, db2 = _ln_bwd(x1, ca_out, small["ln2_g"], small["ln2_b"], dx2, False, "ln2_bwd")
    g_cao = _matmul(o_ct, dp2b, "nn", F32, 512, 1024, t // 2, "mm_g_cao")
    do_c = _matmul(dp2b, wts["ca_wo"], "nt", BF16, 512, 1024, d, "mm_do_c")
    dq_ca, dmemkv = _ca_bwd(q_ca, memkv, do_c)
    g_caq = _matmul(x1t, dq_ca, "nn", F32, 512, 1024, t // 2, "mm_g_caq")
    g_cakv = _matmul(memb, dmemkv.astype(BF16), "tn", F32, 512, kvw, mem.shape[0], "mm_g_cakv", out_slabs=True)
    dx1 = _matmul(dq_ca, wts["ca_wq"], "nt", F32, 512, 1024, d, "mm_dx1", resid=dp2, resid_scale=ALPHA)

    dp1, dp1b, dg1, db1 = _ln_bwd(x, mix, small["ln1_g"], small["ln1_b"], dx1, False, "ln1_bwd")
    g_wout = _matmul(mixin, dp1b, "tn", F32, 512, 1024, 1024, "mm_g_wout")
    dmix = _matmul(dp1b, wts["w_out"], "nt", F32, 512, 1024, d, "mm_dmix")
    dh_a, dw2, dgate_b, dnorm_g = _gla_bwd(h_a, w2p, small["gla_gate_b"], small["gla_norm_g"], o_raw, s_before, dmix)
    dq_d, dk_d, dv_d = _dil_bwd_all(qr, kr, h_b, dmix, o_d, lse_tot)
    dh_b = _dil_dh(dq_d, dk_d, dv_d, tabs)
    g_wa_t = _matmul(dh_a, xb, "tn", F32, 640, 1024, 1024, "mm_g_wa")
    g_wb_t = _matmul(dh_b, xb, "tn", F32, 512, 1024, 1024, "mm_g_wb")
    dx = _matmul(dh_a, w_a_t, "nn", F32, 512, 512, HA_W, "mm_dx_a", resid=dp1, resid_scale=ALPHA)
    dx = _matmul(dh_b, w_b_t, "nn", F32, 512, 512, HB_W, "mm_dx_b", resid=dx)

    grads = {"w_a_t": g_wa_t, "w_b_t": g_wb_t, "w_out": g_wout, "ca_wq": g_caq, "ca_wkv": g_cakv, "ca_wo": g_cao,
             "ffn_w_in": g_ffn_in, "wo": g_wo}
    small_parts = {
        "gla_gate_b": dgate_b, "gla_norm_g": dnorm_g, "ln1_g": dg1, "ln1_b": db1, "ln2_g": dg2, "ln2_b": db2,
        "ln3_g": dg3, "ln3_b": db3,
        "conv": jnp.concatenate([_unpad_ff(dcwg), _unpad_ff(dcwu)], axis=1),
        "gla_gate_w2": dw2[:GLA_RANK],
    }
    return loss_part, dx, grads, small_parts


BIG = ("w_in", "w_out", "ca_wq", "ca_wkv", "ca_wo", "ffn_w_in", "ffn_w_out")
COL_SHARDED = ("w_in", "ca_wkv", "ffn_w_in")
SMALL_ORDER = ("gla_gate_b", "gla_norm_g", "ln1_g", "ln1_b", "ln2_g", "ln2_b", "ln3_g", "ln3_b")


def _gathered_full(name, g):
    if name in COL_SHARDED:
        return g.transpose(1, 0, 2).reshape(g.shape[1], 8 * g.shape[2])
    return g.reshape(8 * g.shape[1], g.shape[2])


def _to_slabs(name, full):
    if name in COL_SHARDED:
        r, cc = full.shape
        s = full.reshape(r, 8, cc // 8).transpose(1, 0, 2)
    else:
        rr, c = full.shape
        s = full.reshape(8, rr // 8, c)
    return s.reshape((4, 2) + s.shape[1:])


def kernel(x, mem, positions, w_in, gla_gate_w2, gla_gate_b, gla_norm_g, w_out, ln1_g, ln1_b, ca_wq, ca_wkv, ca_wo, ln2_g, ln2_b, ffn_w_in, ffn_conv_w, ffn_conv_b, ffn_w_out, ln3_g, ln3_b, loss_target, m_w_in, m_gla_gate_w2, m_gla_gate_b, m_gla_norm_g, m_w_out, m_ln1_g, m_ln1_b, m_ca_wq, m_ca_wkv, m_ca_wo, m_ln2_g, m_ln2_b, m_ffn_w_in, m_ffn_conv_w, m_ffn_conv_b, m_ffn_w_out, m_ln3_g, m_ln3_b, v_w_in, v_gla_gate_w2, v_gla_gate_b, v_gla_norm_g, v_w_out, v_ln1_g, v_ln1_b, v_ca_wq, v_ca_wkv, v_ca_wo, v_ln2_g, v_ln2_b, v_ffn_w_in, v_ffn_conv_w, v_ffn_conv_b, v_ffn_w_out, v_ln3_g, v_ln3_b):
    weights = dict(w_in=w_in, gla_gate_w2=gla_gate_w2, gla_gate_b=gla_gate_b, gla_norm_g=gla_norm_g, w_out=w_out,
                   ln1_g=ln1_g, ln1_b=ln1_b, ca_wq=ca_wq, ca_wkv=ca_wkv, ca_wo=ca_wo, ln2_g=ln2_g, ln2_b=ln2_b,
                   ffn_w_in=ffn_w_in, ffn_conv_w=ffn_conv_w, ffn_conv_b=ffn_conv_b, ffn_w_out=ffn_w_out,
                   ln3_g=ln3_g, ln3_b=ln3_b)
    moms = dict(w_in=(m_w_in, v_w_in), gla_gate_w2=(m_gla_gate_w2, v_gla_gate_w2), gla_gate_b=(m_gla_gate_b, v_gla_gate_b),
                gla_norm_g=(m_gla_norm_g, v_gla_norm_g), w_out=(m_w_out, v_w_out), ln1_g=(m_ln1_g, v_ln1_g),
                ln1_b=(m_ln1_b, v_ln1_b), ca_wq=(m_ca_wq, v_ca_wq), ca_wkv=(m_ca_wkv, v_ca_wkv), ca_wo=(m_ca_wo, v_ca_wo),
                ln2_g=(m_ln2_g, v_ln2_g), ln2_b=(m_ln2_b, v_ln2_b), ffn_w_in=(m_ffn_w_in, v_ffn_w_in),
                ffn_conv_w=(m_ffn_conv_w, v_ffn_conv_w), ffn_conv_b=(m_ffn_conv_b, v_ffn_conv_b),
                ffn_w_out=(m_ffn_w_out, v_ffn_w_out), ln3_g=(m_ln3_g, v_ln3_g), ln3_b=(m_ln3_b, v_ln3_b))
    order = list(weights)
    xi, yi, ci = lax.axis_index("x"), lax.axis_index("y"), lax.axis_index("c")
    me = 4 * xi + 2 * yi + ci

    def travel(n, a):
        return jnp.swapaxes(a, 1, 2) if n in TRANSPOSED else a

    shards = [travel(n, weights[n]).astype(BF16)[0] for n in BIG] + [gla_gate_w2.astype(BF16)[0], ffn_conv_w[0]]
    gathered = _all_gather(shards, "ag_weights")
    gath = dict(zip(BIG + ("gla_gate_w2", "ffn_conv_w"), gathered))
    p = _prepare_sub1(gath)
    p.update({n: _gathered_full(n, gath[n]) for n in ("w_out", "ca_wq", "ca_wo")})
    p["ca_wkv"] = gath["ca_wkv"]
    p.update(_prepare_ffn(gath, ffn_conv_b))
    small = dict(gla_gate_b=gla_gate_b, gla_norm_g=gla_norm_g, ln1_g=ln1_g, ln1_b=ln1_b, ln2_g=ln2_g, ln2_b=ln2_b,
                 ln3_g=ln3_g, ln3_b=ln3_b)

    loss_part, dx, grads, small_parts = _local_step(x[0], mem[0], positions[0], loss_target[0], p, small)
    loss = lax.psum(jnp.sum(loss_part), ("x", "y", "c"))

    slab_of = _grad_slabs(grads)
    slabs = [slab_of[n] for n in BIG]
    from_sib = _sibling_exchange(slabs, "rs_sibling")
    core = ci.reshape(1).astype(jnp.int32)
    pair32, pair16 = [], []
    for n, s, r in zip(BIG, slabs, from_sib):
        p32, p16 = _pair_add(s, r, core, f"pair_add_{n}")
        pair32.append(p32)
        pair16.append(p16)
    from_chips = _chip_exchange(pair16, "rs_chips")
    chip = (2 * xi + yi).reshape(1).astype(jnp.int32)
    out = {}
    for n, p32, rc in zip(BIG, pair32, from_chips):
        m_, v_ = moms[n]
        res4 = _adamw_big(travel(n, weights[n]), travel(n, m_), travel(n, v_), p32, rc, chip, f"adamw_{n}")
        out[n] = [travel(n, a) for a in res4]

    packed = jnp.concatenate([small_parts[n] for n in SMALL_ORDER] + [small_parts["conv"],
                             small_parts["gla_gate_w2"].reshape(SUBLANES, -1)], axis=1)
    pad = (-packed.shape[1]) % 2048
    packed = jnp.pad(packed, ((0, 0), (0, pad)))
    (allp,) = _all_gather([packed], "ag_small")
    dev_sum, row_sum = _small_reduce(allp)
    off = 0
    for n in SMALL_ORDER:
        width = weights[n].shape[1]
        g = row_sum[0:1, off:off + width]
        off += width
        m_, v_ = moms[n]
        out[n] = _adamw(weights[n], m_, v_, g, f"adamw_{n}")
    conv_g = dev_sum[:, off:off + 2 * D_FF]
    off += 2 * D_FF
    g_cb = conv_g[3:4]
    out["ffn_conv_b"] = _adamw(ffn_conv_b, m_ffn_conv_b, v_ffn_conv_b, g_cb, "adamw_ffn_conv_b")
    wsh = ffn_conv_w.shape[2]
    g_cw = lax.dynamic_slice_in_dim(conv_g[0:3], me * wsh, wsh, axis=1)
    out["ffn_conv_w"] = _adamw(ffn_conv_w[0], m_ffn_conv_w[0], v_ffn_conv_w[0], g_cw, "adamw_ffn_conv_w")
    w2_g = dev_sum[:, off:off + GLA_RANK * GLA_HEADS * GLA_DK // SUBLANES].reshape(GLA_RANK, GLA_HEADS * GLA_DK)
    wsh2 = gla_gate_w2.shape[2]
    g_w2 = lax.dynamic_slice_in_dim(w2_g, me * wsh2, wsh2, axis=1)
    out["gla_gate_w2"] = _adamw(gla_gate_w2[0], m_gla_gate_w2[0], v_gla_gate_w2[0], g_w2, "adamw_gla_gate_w2")

    def shaped(n, a):
        return a.reshape(weights[n].shape)

    res = [loss, dx[None]]
    for k in range(4):
        res += [shaped(n, out[n][k]) for n in order]
    return tuple(res)


def _adamw_big(w, m, v, p32, rc, chip, name):
    _, r, c = w.shape
    tr, tc = _tile2d(r, c)
    blk = pl.BlockSpec((None, tr, tc), lambda i, j, s: (0, i, j))
    own = pl.BlockSpec((None, tr, tc), lambda i, j, s: (s[0], i, j))
    others = [pl.BlockSpec((None, tr, tc), lambda i, j, s, k=k: (k, i, j)) for k in range(3)]

    def body(s_ref, w_ref, m_ref, v_ref, p_ref, r0_ref, r1_ref, r2_ref, g_ref, d_ref, nm_ref, nv_ref):
        g = ((p_ref[...] + r0_ref[...].astype(F32)) + r1_ref[...].astype(F32)) + r2_ref[...].astype(F32)
        d_ref[...], nm_ref[...], nv_ref[...] = _adamw_math(w_ref[...], m_ref[...], v_ref[...], g)
        g_ref[...] = g

    gs = pltpu.PrefetchScalarGridSpec(num_scalar_prefetch=1, grid=(r // tr, c // tc),
                                      in_specs=[blk, blk, blk, own] + others, out_specs=[blk] * 4)
    return pl.pallas_call(body, name=name, grid_spec=gs, out_shape=[jax.ShapeDtypeStruct((1, r, c), F32)] * 4,
                          compiler_params=_params(("parallel", "parallel")))(chip, w, m, v, p32, rc, rc, rc)
```

```python
import functools
import math

import jax
import jax.numpy as jnp
from jax import lax
from jax.experimental import pallas as pl
from jax.experimental.pallas import tpu as pltpu

F32 = jnp.float32
BF16 = jnp.bfloat16
MESH = pl.DeviceIdType.MESH

D_MODEL = 2048
LN_EPS = 1e-5
GLA_HEADS = 4
GLA_DV = 256
GLA_DK = 128
GLA_RANK = 16
GLA_TAU = 16.0
GLA_CHUNK = 64
DIL_HD = 128
DIL_HEADS = 8
DIL_BAND = 128
DIL_DILATIONS = (1, 4, 16)
ROPE_THETA = 500000.0
ROPE_DIMS = 32
CA_HEADS = 4
CA_HD = 512
D_FF = 5504
ALPHA = 2.0 ** 0.25
ADAM_LR = 0.001
ADAM_B1 = 0.9
ADAM_B2 = 0.999
ADAM_EPS = 1e-08
ADAM_WD = 0.01
ADAM_STEP = 10

LANES = 128
SUBLANES = 8
VMEM_LIMIT = 56 * 1024 * 1024

GLA_W = 2 * GLA_HEADS * GLA_DK + 2 * GLA_HEADS * GLA_DV
HA_W = GLA_W + LANES
HB_W = 3 * DIL_HEADS * DIL_HD
FFP = 5632
NEG = -1e30


def _params(sem):
    return pltpu.CompilerParams(dimension_semantics=sem, vmem_limit_bytes=VMEM_LIMIT)


def _sigmoid(x):
    return 1.0 / (1.0 + jnp.exp(-x))


def _dot(a, b, dn, precision=None):
    return lax.dot_general(a, b, (dn, ((), ())), preferred_element_type=F32, precision=precision)


NN = ((1,), (0,))
NT = ((1,), (1,))
TN = ((0,), (0,))


def _bf(v):
    return v if v.dtype == BF16 else v.astype(BF16)


def _matmul(a, b, kind, out_dtype, tm, tn, tk, name, resid=None, resid_scale=1.0, b_k_off=0, b_slabs=False,
            out_slabs=False):
    if b_slabs:
        assert kind != "nt" and b.shape[2] == tn
        k2, n = b.shape[1], b.shape[0] * tn
    elif kind == "nt":
        n, k2 = b.shape
    else:
        k2, n = b.shape
    (k, m) = a.shape if kind == "tn" else a.shape[::-1]
    assert k2 >= k and (k2 == k or not b_slabs) and m % tm == 0 and n % tn == 0 and k % tk == 0, \
        (name, a.shape, b.shape, tm, tn, tk)
    nk = k // tk
    dn = {"nn": NN, "nt": NT, "tn": TN}[kind]
    a_spec = pl.BlockSpec((tk, tm), lambda i, j, kk: (kk, i)) if kind == "tn" else pl.BlockSpec((tm, tk), lambda i, j, kk: (i, kk))
    if b_slabs:
        b_spec = pl.BlockSpec((None, tk, tn), lambda i, j, kk: (j, kk, 0))
    elif kind == "nt":
        b_spec = pl.BlockSpec((tn, tk), lambda i, j, kk: (j, kk + b_k_off))
    else:
        b_spec = pl.BlockSpec((tk, tn), lambda i, j, kk: (kk + b_k_off, j))
    if out_slabs:
        o_spec = pl.BlockSpec((None, tm, tn), lambda i, j, kk: (j, i, 0))
        o_shape = (n // tn, m, tn)
    else:
        o_spec = pl.BlockSpec((tm, tn), lambda i, j, kk: (i, j))
        o_shape = (m, n)
    has_resid = resid is not None

    def body(*refs):
        if has_resid:
            a_ref, b_ref, r_ref, o_ref = refs[:4]
        else:
            a_ref, b_ref, o_ref = refs[:3]
            r_ref = None
        part = _dot(_bf(a_ref[...]), _bf(b_ref[...]), dn)

        def finish(acc):
            if has_resid:
                acc = acc + resid_scale * r_ref[...].astype(F32)
            o_ref[...] = acc.astype(out_dtype)

        if nk == 1:
            finish(part)
        else:
            acc_ref = refs[-1]
            kk = pl.program_id(2)

            @pl.when(kk == 0)
            def _():
                acc_ref[...] = part

            @pl.when(kk > 0)
            def _():
                acc_ref[...] += part

            @pl.when(kk == nk - 1)
            def _():
                finish(acc_ref[...])

    in_specs = [a_spec, b_spec] + ([o_spec] if has_resid else [])
    args = (a, b) + ((resid,) if has_resid else ())
    return pl.pallas_call(
        body, name=name, out_shape=jax.ShapeDtypeStruct(o_shape, out_dtype),
        grid=(m // tm, n // tn, nk), in_specs=in_specs, out_specs=o_spec,
        scratch_shapes=[pltpu.VMEM((tm, tn), F32)] if nk > 1 else [],
        compiler_params=_params(("parallel", "parallel", "arbitrary")),
    )(*args)


def _ln_core(xres, f):
    p = ALPHA * xres + f
    mu = jnp.mean(p, axis=-1, keepdims=True)
    xc = p - mu
    var = jnp.mean(xc * xc, axis=-1, keepdims=True)
    rstd = lax.rsqrt(var + LN_EPS)
    return xc * rstd, rstd


def _rows8(v):
    r, c = v.shape
    return jnp.sum(v.reshape(r // SUBLANES, SUBLANES, c), axis=0)


def _ln_fwd(xres, f, g, b, name, tr=256):
    t, d = xres.shape
    row = pl.BlockSpec((tr, d), lambda i: (i, 0))
    vec = pl.BlockSpec((1, d), lambda i: (0, 0))

    def body(x_ref, f_ref, g_ref, b_ref, y_ref, yb_ref, yt_ref):
        xhat, _ = _ln_core(x_ref[...], f_ref[...])
        y = xhat * g_ref[...] + b_ref[...]
        y_ref[...] = y
        yb = y.astype(BF16)
        yb_ref[...] = yb
        yt_ref[...] = yb.T

    return pl.pallas_call(
        body, name=name, grid=(t // tr,), in_specs=[row, row, vec, vec],
        out_specs=[row, row, pl.BlockSpec((d, tr), lambda i: (0, i))],
        out_shape=[jax.ShapeDtypeStruct((t, d), F32), jax.ShapeDtypeStruct((t, d), BF16),
                   jax.ShapeDtypeStruct((d, t), BF16)],
        compiler_params=_params(("parallel",)),
    )(xres, f, g, b)


def _ln_bwd(xres, f, g, b, dy_or_target, loss_head, name, tr=256):
    t, d = xres.shape
    row = pl.BlockSpec((tr, d), lambda i: (i, 0))
    vec = pl.BlockSpec((1, d), lambda i: (0, 0))
    acc = pl.BlockSpec((SUBLANES, d), lambda i: (0, 0))
    lacc = pl.BlockSpec((SUBLANES, LANES), lambda i: (0, 0))

    def body(x_ref, f_ref, g_ref, b_ref, t_ref, dp_ref, dpb_ref, dg_ref, db_ref, *rest):
        i = pl.program_id(0)
        xhat, rstd = _ln_core(x_ref[...], f_ref[...])
        if loss_head:
            err = xhat * g_ref[...] + b_ref[...] - t_ref[...]
            dy = err * (1.0 / d)
            sq = err * err
            lanes = sq[:, :LANES]
            for kk in range(1, d // LANES):
                lanes = lanes + sq[:, kk * LANES:(kk + 1) * LANES]
            lpart = _rows8(lanes) * (0.5 / d)
        else:
            dy = t_ref[...]
        dxh = dy * g_ref[...]
        m1 = jnp.mean(dxh, axis=-1, keepdims=True)
        m2 = jnp.mean(dxh * xhat, axis=-1, keepdims=True)
        dp = rstd * (dxh - m1 - xhat * m2)
        dp_ref[...] = dp
        dpb_ref[...] = dp.astype(BF16)
        dgp = _rows8(dy * xhat)
        dbp = _rows8(dy)

        @pl.when(i == 0)
        def _():
            dg_ref[...] = dgp
            db_ref[...] = dbp
            if loss_head:
                rest[0][...] = lpart

        @pl.when(i > 0)
        def _():
            dg_ref[...] += dgp
            db_ref[...] += dbp
            if loss_head:
                rest[0][...] += lpart

    out_shape = [jax.ShapeDtypeStruct((t, d), F32), jax.ShapeDtypeStruct((t, d), BF16),
                 jax.ShapeDtypeStruct((SUBLANES, d), F32), jax.ShapeDtypeStruct((SUBLANES, d), F32)]
    out_specs = [row, row, acc, acc]
    if loss_head:
        out_shape.append(jax.ShapeDtypeStruct((SUBLANES, LANES), F32))
        out_specs.append(lacc)
    return pl.pallas_call(
        body, name=name, grid=(t // tr,), in_specs=[row, row, vec, vec, row], out_specs=out_specs,
        out_shape=out_shape, compiler_params=_params(("arbitrary",)),
    )(xres, f, g, b, dy_or_target)


def _gla_gates(glr, w2, gb):
    z = _dot(_bf(glr), w2, NN) + gb
    lg = (jnp.minimum(z, 0.0) - jnp.log(1.0 + jnp.exp(-jnp.abs(z)))) * (1.0 / GLA_TAU)
    c = z.shape[0]
    ri = lax.broadcasted_iota(jnp.int32, (c, c), 0)
    ci = lax.broadcasted_iota(jnp.int32, (c, c), 1)
    tri = (ci <= ri).astype(F32)
    bcum = _dot(tri, lg, NN, precision=lax.Precision.HIGHEST)
    blast = jnp.sum(lg, axis=0, keepdims=True)
    return z, bcum, blast, tri


def _gla_specs(t):
    c = GLA_CHUNK
    return c, t // c


def _gla_fwd(h_a, w2p, gate_b, norm_g):
    t = h_a.shape[0]
    c, n = _gla_specs(t)
    hk, hv = GLA_HEADS * GLA_DK, GLA_HEADS * GLA_DV
    scale = GLA_DK ** -0.5

    def body(q_ref, k_ref, v_ref, r_ref, glr_ref, w2_ref, gb_ref, ng_ref, og_ref, oraw_ref, sb_ref, st_ref):
        i = pl.program_id(0)

        @pl.when(i == 0)
        def _():
            st_ref[...] = jnp.zeros_like(st_ref)

        _, bcum, blast, _ = _gla_gates(glr_ref[...], w2_ref[...], gb_ref[...])
        ri = lax.broadcasted_iota(jnp.int32, (c, c), 0)
        ci = lax.broadcasted_iota(jnp.int32, (c, c), 1)
        causal = ci <= ri
        for h in range(GLA_HEADS):
            ks = slice(h * GLA_DK, (h + 1) * GLA_DK)
            vs = slice(h * GLA_DV, (h + 1) * GLA_DV)
            b_h, bl_h = bcum[:, ks], blast[:, ks]
            q_h, k_h = q_ref[:, ks], k_ref[:, ks]
            v_h = _bf(v_ref[:, vs])
            qi = _bf(q_h * scale * jnp.exp(b_h))
            ki = _bf(k_h * jnp.exp(-b_h))
            ke = _bf(k_h * jnp.exp(bl_h - b_h))
            st = st_ref[h]
            sb_ref[0, h] = st
            a = jnp.where(causal, _dot(qi, ki, NT), 0.0)
            o = _dot(_bf(a), v_h, NN) + _dot(qi, _bf(st), NT)
            st_ref[h] = st * jnp.exp(bl_h) + _dot(v_h, ke, TN)
            oraw_ref[:, vs] = o
            mu = jnp.mean(o, axis=-1, keepdims=True)
            oc = o - mu
            var = jnp.mean(oc * oc, axis=-1, keepdims=True)
            xh = oc * lax.rsqrt(var + LN_EPS)
            r_h = r_ref[:, vs]
            og_ref[:, vs] = (xh * ng_ref[:, vs] * (r_h * _sigmoid(r_h))).astype(BF16)

    return pl.pallas_call(
        body, name="gla_fwd", grid=(n,),
        in_specs=[pl.BlockSpec((c, hk), lambda i: (i, 0)), pl.BlockSpec((c, hk), lambda i: (i, 1)),
                  pl.BlockSpec((c, hv), lambda i: (i, 1)), pl.BlockSpec((c, hv), lambda i: (i, 2)),
                  pl.BlockSpec((c, LANES), lambda i: (i, GLA_W // LANES)),
                  pl.BlockSpec((LANES, hk), lambda i: (0, 0)), pl.BlockSpec((1, hk), lambda i: (0, 0)),
                  pl.BlockSpec((1, hv), lambda i: (0, 0))],
        out_specs=[pl.BlockSpec((c, hv), lambda i: (i, 0)), pl.BlockSpec((c, hv), lambda i: (i, 0)),
                   pl.BlockSpec((1, GLA_HEADS, GLA_DV, GLA_DK), lambda i: (i, 0, 0, 0))],
        out_shape=[jax.ShapeDtypeStruct((t, hv), BF16), jax.ShapeDtypeStruct((t, hv), F32),
                   jax.ShapeDtypeStruct((n, GLA_HEADS, GLA_DV, GLA_DK), F32)],
        scratch_shapes=[pltpu.VMEM((GLA_HEADS, GLA_DV, GLA_DK), F32)],
        compiler_params=_params(("arbitrary",)),
    )(h_a, h_a, h_a, h_a, h_a, w2p, gate_b, norm_g)


def _gla_bwd(h_a, w2p, gate_b, norm_g, o_raw, s_before, dmix):
    t = h_a.shape[0]
    c, n = _gla_specs(t)
    hk, hv = GLA_HEADS * GLA_DK, GLA_HEADS * GLA_DV
    scale = GLA_DK ** -0.5
    rev = lambda i: n - 1 - i

    def body(q_ref, k_ref, v_ref, r_ref, glr_ref, w2_ref, gb_ref, ng_ref, oraw_ref, sb_ref, do_ref,
             dh_ref, dw2_ref, dgb_ref, dng_ref, dst_ref):
        i = pl.program_id(0)

        @pl.when(i == 0)
        def _():
            dst_ref[...] = jnp.zeros_like(dst_ref)

        glr = glr_ref[...]
        z, bcum, blast, tri = _gla_gates(glr, w2_ref[...], gb_ref[...])
        ri = lax.broadcasted_iota(jnp.int32, (c, c), 0)
        ci = lax.broadcasted_iota(jnp.int32, (c, c), 1)
        causal = ci <= ri
        dlg_parts = []
        dng_parts = []
        for h in range(GLA_HEADS):
            ks = slice(h * GLA_DK, (h + 1) * GLA_DK)
            vs = slice(h * GLA_DV, (h + 1) * GLA_DV)
            o = oraw_ref[:, vs]
            mu = jnp.mean(o, axis=-1, keepdims=True)
            oc = o - mu
            var = jnp.mean(oc * oc, axis=-1, keepdims=True)
            rstd = lax.rsqrt(var + LN_EPS)
            xh = oc * rstd
            r_h = r_ref[:, vs]
            sg = _sigmoid(r_h)
            silu = r_h * sg
            dout = do_ref[:, vs]
            ng = ng_ref[:, vs]
            dng_parts.append(_rows8(dout * xh * silu))
            dr = dout * xh * ng * (sg * (1.0 + r_h * (1.0 - sg)))
            dxh = dout * ng * silu
            m1 = jnp.mean(dxh, axis=-1, keepdims=True)
            m2 = jnp.mean(dxh * xh, axis=-1, keepdims=True)
            do_raw = _bf(rstd * (dxh - m1 - xh * m2))
            b_h, bl_h = bcum[:, ks], blast[:, ks]
            q_h, k_h = q_ref[:, ks], k_ref[:, ks]
            v_h = _bf(v_ref[:, vs])
            eb, enb, eend = jnp.exp(b_h), jnp.exp(-b_h), jnp.exp(bl_h - b_h)
            decay = jnp.exp(bl_h)
            qi_f, ki_f, ke_f = q_h * scale * eb, k_h * enb, k_h * eend
            qi, ki, ke = _bf(qi_f), _bf(ki_f), _bf(ke_f)
            st = sb_ref[0, h]
            dst = dst_ref[h]
            dst_b = _bf(dst)
            a = _bf(jnp.where(causal, _dot(qi, ki, NT), 0.0))
            da = _bf(jnp.where(causal, _dot(do_raw, v_h, NT), 0.0))
            dv = _dot(a, do_raw, TN) + _dot(ke, dst_b, NT)
            dqi = _dot(da, ki, NN) + _dot(do_raw, _bf(st), NN)
            dki = _dot(da, qi, TN)
            dke = _dot(v_h, dst_b, NN)
            dst_ref[h] = _dot(do_raw, qi, TN) + dst * decay
            dbl = decay * jnp.sum(st * dst, axis=0, keepdims=True) + jnp.sum(dke * ke_f, axis=0, keepdims=True)
            dbc = dqi * qi_f - dki * ki_f - dke * ke_f
            dlg_parts.append(_dot(tri, dbc, TN, precision=lax.Precision.HIGHEST) + dbl)
            dh_ref[:, ks] = (dqi * eb * scale).astype(BF16)
            dh_ref[:, hk + h * GLA_DK: hk + (h + 1) * GLA_DK] = (dki * enb + dke * eend).astype(BF16)
            dh_ref[:, 2 * hk + h * GLA_DV: 2 * hk + (h + 1) * GLA_DV] = dv.astype(BF16)
            dh_ref[:, 2 * hk + hv + h * GLA_DV: 2 * hk + hv + (h + 1) * GLA_DV] = dr.astype(BF16)
        dlg = jnp.concatenate(dlg_parts, axis=1)
        dz = dlg * (1.0 / GLA_TAU) * _sigmoid(-z)
        dz_b = _bf(dz)
        dh_ref[:, GLA_W:] = _dot(dz_b, w2_ref[...], NT).astype(BF16)
        dw2p = _dot(_bf(glr), dz_b, TN)
        dgbp = _rows8(dz)
        dngp = jnp.concatenate(dng_parts, axis=1)

        @pl.when(i == 0)
        def _():
            dw2_ref[...] = dw2p
            dgb_ref[...] = dgbp
            dng_ref[...] = dngp

        @pl.when(i > 0)
        def _():
            dw2_ref[...] += dw2p
            dgb_ref[...] += dgbp
            dng_ref[...] += dngp

    return pl.pallas_call(
        body, name="gla_bwd", grid=(n,),
        in_specs=[pl.BlockSpec((c, hk), lambda i: (rev(i), 0)), pl.BlockSpec((c, hk), lambda i: (rev(i), 1)),
                  pl.BlockSpec((c, hv), lambda i: (rev(i), 1)), pl.BlockSpec((c, hv), lambda i: (rev(i), 2)),
                  pl.BlockSpec((c, LANES), lambda i: (rev(i), GLA_W // LANES)),
                  pl.BlockSpec((LANES, hk), lambda i: (0, 0)), pl.BlockSpec((1, hk), lambda i: (0, 0)),
                  pl.BlockSpec((1, hv), lambda i: (0, 0)),
                  pl.BlockSpec((c, hv), lambda i: (rev(i), 0)),
                  pl.BlockSpec((1, GLA_HEADS, GLA_DV, GLA_DK), lambda i: (rev(i), 0, 0, 0)),
                  pl.BlockSpec((c, hv), lambda i: (rev(i), 0))],
        out_specs=[pl.BlockSpec((c, HA_W), lambda i: (rev(i), 0)),
                   pl.BlockSpec((LANES, hk), lambda i: (0, 0)),
                   pl.BlockSpec((SUBLANES, hk), lambda i: (0, 0)),
                   pl.BlockSpec((SUBLANES, hv), lambda i: (0, 0))],
        out_shape=[jax.ShapeDtypeStruct((t, HA_W), BF16), jax.ShapeDtypeStruct((LANES, hk), F32),
                   jax.ShapeDtypeStruct((SUBLANES, hk), F32), jax.ShapeDtypeStruct((SUBLANES, hv), F32)],
        scratch_shapes=[pltpu.VMEM((GLA_HEADS, GLA_DV, GLA_DK), F32)],
        compiler_params=_params(("arbitrary",)),
    )(h_a, h_a, h_a, h_a, h_a, w2p, gate_b, norm_g, o_raw, s_before, dmix)


def _rope_tables(positions):
    half = ROPE_DIMS // 2
    inv_freq = ROPE_THETA ** (-jnp.arange(0, ROPE_DIMS, 2, dtype=F32) / ROPE_DIMS)
    ang = positions.astype(F32).reshape(-1, 1) * inv_freq
    cos, sin = jnp.cos(ang), jnp.sin(ang)
    t = cos.shape[0]
    one = jnp.ones((t, DIL_HD - ROPE_DIMS), F32)
    zero = jnp.zeros((t, DIL_HD - ROPE_DIMS), F32)
    zh = jnp.zeros((t, half), F32)
    return (jnp.concatenate([cos, cos, one], axis=1), jnp.concatenate([-sin, zh, zero], axis=1),
            jnp.concatenate([zh, sin, zero], axis=1))


def _rope_apply(x, c, s1, s2):
    half = ROPE_DIMS // 2
    return x * c + pltpu.roll(x, DIL_HD - half, 1) * s1 + pltpu.roll(x, half, 1) * s2


def _rope_apply_t(dy, c, s1, s2):
    half = ROPE_DIMS // 2
    return dy * c + pltpu.roll(dy * s1, half, 1) + pltpu.roll(dy * s2, DIL_HD - half, 1)


def _rope_fwd(h_b, tabs, tr=256):
    t = h_b.shape[0]
    w = DIL_HEADS * DIL_HD
    scale = DIL_HD ** -0.5
    tab = pl.BlockSpec((tr, DIL_HD), lambda i: (i, 0))
    outb = pl.BlockSpec((tr, w), lambda i: (i, 0))

    def body(q_ref, k_ref, c_ref, s1_ref, s2_ref, qo_ref, ko_ref):
        c, s1, s2 = c_ref[...], s1_ref[...], s2_ref[...]
        for h in range(DIL_HEADS):
            hs = slice(h * DIL_HD, (h + 1) * DIL_HD)
            qo_ref[:, hs] = _rope_apply(q_ref[:, hs] * scale, c, s1, s2)
            ko_ref[:, hs] = _rope_apply(k_ref[:, hs], c, s1, s2)

    return pl.pallas_call(
        body, name="rope_fwd", grid=(t // tr,),
        in_specs=[pl.BlockSpec((tr, w), lambda i: (i, 0)), pl.BlockSpec((tr, w), lambda i: (i, 1)), tab, tab, tab],
        out_specs=[outb, outb],
        out_shape=[jax.ShapeDtypeStruct((t, w), F32)] * 2,
        compiler_params=_params(("parallel",)),
    )(h_b, h_b, *tabs)


def _dil_dh(dq, dk, dv, tabs, tr=256):
    t, w = dq.shape
    scale = DIL_HD ** -0.5
    tab = pl.BlockSpec((tr, DIL_HD), lambda i: (i, 0))
    inb = pl.BlockSpec((tr, w), lambda i: (i, 0))

    def body(dq_ref, dk_ref, dv_ref, c_ref, s1_ref, s2_ref, o_ref):
        c, s1, s2 = c_ref[...], s1_ref[...], s2_ref[...]
        for h in range(DIL_HEADS):
            hs = slice(h * DIL_HD, (h + 1) * DIL_HD)
            o_ref[:, h * DIL_HD:(h + 1) * DIL_HD] = (_rope_apply_t(dq_ref[:, hs], c, s1, s2) * scale).astype(BF16)
            o_ref[:, w + h * DIL_HD: w + (h + 1) * DIL_HD] = _rope_apply_t(dk_ref[:, hs], c, s1, s2).astype(BF16)
        o_ref[:, 2 * w:] = dv_ref[...].astype(BF16)

    return pl.pallas_call(
        body, name="dil_dh", grid=(t // tr,), in_specs=[inb] * 3 + [tab] * 3,
        out_specs=pl.BlockSpec((tr, 3 * w), lambda i: (i, 0)),
        out_shape=jax.ShapeDtypeStruct((t, 3 * w), BF16), compiler_params=_params(("parallel",)),
    )(dq, dk, dv, *tabs)


BANDS = 8


def _to_branch(a, d):
    t, w = a.shape
    return a.reshape(t // d, d, w // DIL_HD, DIL_HD).transpose(1, 2, 0, 3).reshape(-1, DIL_HD)


def _from_branch(a, d, t):
    hds = a.shape[0] // t
    return a.reshape(d, hds, t // d, DIL_HD).transpose(2, 0, 1, 3).reshape(t, hds * DIL_HD)


def _band_masks(not_first):
    r = lax.broadcasted_iota(jnp.int32, (DIL_BAND, 2 * DIL_BAND), 0)
    c = lax.broadcasted_iota(jnp.int32, (DIL_BAND, 2 * DIL_BAND), 1)
    nf = jnp.full((DIL_BAND, 2 * DIL_BAND), not_first, jnp.int32)
    look_back = jnp.logical_and(jnp.logical_and(c < DIL_BAND, c >= r), nf > 0)
    own_band = jnp.logical_and(c >= DIL_BAND, (c - DIL_BAND) <= r)
    return jnp.logical_or(look_back, own_band)


def _dil_fwd(q, k, v, nb, name):
    rows = q.shape[0]
    blk = BANDS * DIL_BAND
    steps = rows // blk
    main = pl.BlockSpec((blk, DIL_HD), lambda i: (i, 0))
    prev = pl.BlockSpec((DIL_BAND, DIL_HD), lambda i: (jnp.maximum(i * BANDS - 1, 0), 0))

    def body(q_ref, k_ref, v_ref, kp_ref, vp_ref, o_ref, l_ref):
        i = pl.program_id(0)
        for j in range(BANDS):
            lo, hi = j * DIL_BAND, (j + 1) * DIL_BAND
            if j == 0:
                kcat = jnp.concatenate([kp_ref[...], k_ref[lo:hi, :]], axis=0)
                vcat = jnp.concatenate([vp_ref[...], v_ref[lo:hi, :]], axis=0)
            else:
                kcat = k_ref[lo - DIL_BAND:hi, :]
                vcat = v_ref[lo - DIL_BAND:hi, :]
            not_first = (((i * BANDS + j) % nb) != 0).astype(jnp.int32)
            s = jnp.where(_band_masks(not_first), _dot(q_ref[lo:hi, :], kcat, NT), NEG)
            m = jnp.max(s, axis=-1, keepdims=True)
            p = jnp.exp(s - m)
            den = jnp.sum(p, axis=-1, keepdims=True)
            o_ref[lo:hi, :] = _dot(_bf(p), vcat, NN) / den
            l_ref[lo:hi, :] = jnp.broadcast_to(m + jnp.log(den), (DIL_BAND, DIL_HD))

    return pl.pallas_call(
        body, name=name, grid=(steps,), in_specs=[main, main, main, prev, prev], out_specs=[main, main],
        out_shape=[jax.ShapeDtypeStruct((rows, DIL_HD), F32)] * 2, compiler_params=_params(("parallel",)),
    )(q, k, v, k, v)


def _dil_bwd(q, k, v, do, lse, dd, nb, name):
    rows = q.shape[0]
    blk = BANDS * DIL_BAND
    steps = rows // blk
    last_band = rows // DIL_BAND - 1
    main = pl.BlockSpec((blk, DIL_HD), lambda i: (i, 0))
    prev = pl.BlockSpec((DIL_BAND, DIL_HD), lambda i: (jnp.maximum(i * BANDS - 1, 0), 0))
    nxt = pl.BlockSpec((DIL_BAND, DIL_HD), lambda i: (jnp.minimum(i * BANDS + BANDS, last_band), 0))

    def body(q_ref, k_ref, v_ref, do_ref, l_ref, dd_ref, kp_ref, vp_ref, qn_ref, don_ref, ln_ref, ddn_ref,
             dq_ref, dk_ref, dv_ref, ak_ref, av_ref):
        i = pl.program_id(0)
        ak_ref[...] = jnp.zeros_like(ak_ref)
        av_ref[...] = jnp.zeros_like(av_ref)
        for j in range(BANDS + 1):
            lo, hi = j * DIL_BAND, (j + 1) * DIL_BAND
            if j == 0:
                kcat = jnp.concatenate([kp_ref[...], k_ref[lo:hi, :]], axis=0)
                vcat = jnp.concatenate([vp_ref[...], v_ref[lo:hi, :]], axis=0)
            elif j < BANDS:
                kcat = k_ref[lo - DIL_BAND:hi, :]
                vcat = v_ref[lo - DIL_BAND:hi, :]
            else:
                kcat = jnp.concatenate([k_ref[lo - DIL_BAND:lo, :], k_ref[lo - DIL_BAND:lo, :]], axis=0)
                vcat = jnp.concatenate([v_ref[lo - DIL_BAND:lo, :], v_ref[lo - DIL_BAND:lo, :]], axis=0)
            if j < BANDS:
                qj, doj, lj, ddj = q_ref[lo:hi, :], do_ref[lo:hi, :], l_ref[lo:hi, :], dd_ref[lo:hi, :]
            else:
                qj, doj, lj, ddj = qn_ref[...], don_ref[...], ln_ref[...], ddn_ref[...]
            not_first = (((i * BANDS + j) % nb) != 0).astype(jnp.int32)
            mask = _band_masks(not_first)
            if j == BANDS:
                cidx = lax.broadcasted_iota(jnp.int32, mask.shape, 1)
                mask = jnp.logical_and(mask, cidx < DIL_BAND)
            s = jnp.where(mask, _dot(qj, kcat, NT), NEG)
            p = jnp.exp(s - jnp.concatenate([lj, lj], axis=1))
            dp = _dot(doj, vcat, NT)
            ds = _bf(p * (dp - jnp.concatenate([ddj, ddj], axis=1)))
            if j < BANDS:
                dq_ref[lo:hi, :] = _dot(ds, kcat, NN)
            ak_ref[lo:hi + DIL_BAND, :] += _dot(ds, qj, TN)
            av_ref[lo:hi + DIL_BAND, :] += _dot(_bf(p), doj, TN)
        dk_ref[...] = ak_ref[DIL_BAND:DIL_BAND + blk, :]
        dv_ref[...] = av_ref[DIL_BAND:DIL_BAND + blk, :]

    return pl.pallas_call(
        body, name=name, grid=(steps,),
        in_specs=[main] * 6 + [prev, prev] + [nxt] * 4, out_specs=[main] * 3,
        out_shape=[jax.ShapeDtypeStruct((rows, DIL_HD), F32)] * 3,
        scratch_shapes=[pltpu.VMEM((blk + 2 * DIL_BAND, DIL_HD), F32)] * 2,
        compiler_params=_params(("parallel",)),
    )(q, k, v, do, lse, dd, k, v, q, do, lse, dd)


def _dil_merge(os_, ls_, tr=256):
    t, w = os_[0].shape
    blk = pl.BlockSpec((tr, w), lambda i: (i, 0))

    def body(o1, o2, o3, l1, l2, l3, ob_ref, of_ref, lt_ref):
        a, b, c = l1[...], l2[...], l3[...]
        m = jnp.maximum(jnp.maximum(a, b), c)
        ea, eb, ec = jnp.exp(a - m), jnp.exp(b - m), jnp.exp(c - m)
        den = ea + eb + ec
        o = (ea * o1[...] + eb * o2[...] + ec * o3[...]) / den
        ob_ref[...] = o.astype(BF16)
        of_ref[...] = o
        lt_ref[...] = m + jnp.log(den)

    return pl.pallas_call(
        body, name="dil_merge", grid=(t // tr,), in_specs=[blk] * 6, out_specs=[blk] * 3,
        out_shape=[jax.ShapeDtypeStruct((t, w), BF16), jax.ShapeDtypeStruct((t, w), F32),
                   jax.ShapeDtypeStruct((t, w), F32)],
        compiler_params=_params(("parallel",)),
    )(*os_, *ls_)


def _dil_bwd_prep(dmix, o_d, tr=256):
    t, w = o_d.shape
    blk = pl.BlockSpec((tr, w), lambda i: (i, 0))

    def body(do_ref, o_ref, dob_ref, dd_ref):
        do = do_ref[...]
        prod = do * o_ref[...]
        dob_ref[...] = do.astype(BF16)
        for h in range(DIL_HEADS):
            hs = slice(h * DIL_HD, (h + 1) * DIL_HD)
            dd_ref[:, hs] = jnp.broadcast_to(jnp.sum(prod[:, hs], axis=-1, keepdims=True), (tr, DIL_HD))

    return pl.pallas_call(
        body, name="dil_bwd_prep", grid=(t // tr,),
        in_specs=[pl.BlockSpec((tr, w), lambda i: (i, 1)), blk], out_specs=[blk, blk],
        out_shape=[jax.ShapeDtypeStruct((t, w), BF16), jax.ShapeDtypeStruct((t, w), F32)],
        compiler_params=_params(("parallel",)),
    )(dmix, o_d)


def _gather_rows(dst_ref, src_ref, t, d, cast=None):
    n = t // d
    for r in range(d):
        v = src_ref[pl.ds(r, n, stride=d), :] if d > 1 else src_ref[...]
        dst_ref[r * n:(r + 1) * n, :] = v if cast is None else v.astype(cast)


def _tri_mask():
    r = lax.broadcasted_iota(jnp.int32, (DIL_BAND, DIL_BAND), 0)
    c = lax.broadcasted_iota(jnp.int32, (DIL_BAND, DIL_BAND), 1)
    return c <= r


def _dil_fwd_all(qr, kr, h_b):
    t = qr.shape[0]
    nbands = t // DIL_BAND
    nbr = len(DIL_DILATIONS)
    hoff = DIL_HEADS

    def col(off):
        return pl.BlockSpec((t, DIL_HD), lambda h: (0, off + h), pipeline_mode=pl.Buffered(1))

    outb = pl.BlockSpec((t, DIL_HD), lambda h: (0, h))

    def body(q_ref, k_ref, v_ref, ob_ref, of_ref, lt_ref, qs, ks, vs, os_, ls_, *br):
        obr, lbr = br[:nbr], br[nbr:]
        for bi, d in enumerate(DIL_DILATIONS):
            n = t // d
            nb = n // DIL_BAND
            _gather_rows(qs, q_ref, t, d, BF16)
            _gather_rows(ks, k_ref, t, d, BF16)
            _gather_rows(vs, v_ref, t, d, BF16)
            s = jnp.where(_tri_mask(), _dot(qs[0:DIL_BAND, :], ks[0:DIL_BAND, :], NT), NEG)
            m = jnp.max(s, axis=-1, keepdims=True)
            pr = jnp.exp(s - m)
            den = jnp.sum(pr, axis=-1, keepdims=True)
            os_[0:DIL_BAND, :] = _dot(_bf(pr), vs[0:DIL_BAND, :], NN) / den
            ls_[0:DIL_BAND, :] = jnp.broadcast_to(m + jnp.log(den), (DIL_BAND, DIL_HD))

            def band(b, carry, nb=nb):
                st = pl.multiple_of((b - 1) * DIL_BAND, DIL_BAND)
                cur = pl.ds(st + DIL_BAND, DIL_BAND)
                both = pl.ds(st, 2 * DIL_BAND)
                not_first = ((b % nb) != 0).astype(jnp.int32)
                s = jnp.where(_band_masks(not_first), _dot(qs[cur, :], ks[both, :], NT), NEG)
                m = jnp.max(s, axis=-1, keepdims=True)
                pr = jnp.exp(s - m)
                den = jnp.sum(pr, axis=-1, keepdims=True)
                os_[cur, :] = _dot(_bf(pr), vs[both, :], NN) / den
                ls_[cur, :] = jnp.broadcast_to(m + jnp.log(den), (DIL_BAND, DIL_HD))
                return carry

            lax.fori_loop(1, nbands, band, 0, unroll=4)
            for r in range(d):
                dst = pl.ds(r, n, stride=d) if d > 1 else slice(None)
                obr[bi][dst, :] = os_[r * n:(r + 1) * n, :]
                lbr[bi][dst, :] = ls_[r * n:(r + 1) * n, :]
        rows = 512
        for c0 in range(0, t, rows):
            sl = slice(c0, c0 + rows)
            la, lb, lc = lbr[0][sl, :], lbr[1][sl, :], lbr[2][sl, :]
            m = jnp.maximum(jnp.maximum(la, lb), lc)
            ea, eb, ec = jnp.exp(la - m), jnp.exp(lb - m), jnp.exp(lc - m)
            den = ea + eb + ec
            o = (ea * obr[0][sl, :] + eb * obr[1][sl, :] + ec * obr[2][sl, :]) / den
            ob_ref[sl, :] = o.astype(BF16)
            of_ref[sl, :] = o
            lt_ref[sl, :] = m + jnp.log(den)

    w = DIL_HEADS * DIL_HD
    vm = lambda dt: pltpu.VMEM((t, DIL_HD), dt)
    return pl.pallas_call(
        body, name="dil_fwd", grid=(DIL_HEADS,), in_specs=[col(0), col(0), col(2 * hoff)],
        out_specs=[outb, outb, outb],
        out_shape=[jax.ShapeDtypeStruct((t, w), BF16), jax.ShapeDtypeStruct((t, w), F32),
                   jax.ShapeDtypeStruct((t, w), F32)],
        scratch_shapes=[vm(BF16)] * 3 + [vm(F32)] * 2 + [vm(F32)] * (2 * nbr),
        compiler_params=_params(("parallel",)),
    )(qr, kr, h_b)


def _dil_bwd_all(qr, kr, h_b, dmix, o_d, lse_tot):
    t = qr.shape[0]
    nbands = t // DIL_BAND
    hoff = DIL_HEADS

    def col(off):
        return pl.BlockSpec((t, DIL_HD), lambda h: (0, off + h), pipeline_mode=pl.Buffered(1))

    outb = pl.BlockSpec((t, DIL_HD), lambda h: (0, h))

    def body(q_ref, k_ref, v_ref, do_ref, o_ref, l_ref, dq_ref, dk_ref, dv_ref,
             qs, ks, vs, dos, lss, dds, dqs, acck, accv):
        for bi, d in enumerate(DIL_DILATIONS):
            n = t // d
            nb = n // DIL_BAND
            _gather_rows(qs, q_ref, t, d, BF16)
            _gather_rows(ks, k_ref, t, d, BF16)
            _gather_rows(vs, v_ref, t, d, BF16)
            _gather_rows(dos, do_ref, t, d, BF16)
            _gather_rows(lss, l_ref, t, d)
            for r in range(d):
                src = pl.ds(r, n, stride=d) if d > 1 else slice(None)
                prod = do_ref[src, :] * o_ref[src, :]
                dds[r * n:(r + 1) * n, :] = jnp.broadcast_to(jnp.sum(prod, axis=-1, keepdims=True), (n, DIL_HD))
            acck[...] = jnp.zeros_like(acck)
            accv[...] = jnp.zeros_like(accv)
            b0 = slice(0, DIL_BAND)
            s = jnp.where(_tri_mask(), _dot(qs[b0, :], ks[b0, :], NT), NEG)
            pr = jnp.exp(s - lss[b0, :])
            ds = _bf(pr * (_dot(dos[b0, :], vs[b0, :], NT) - dds[b0, :]))
            dqs[b0, :] = _dot(ds, ks[b0, :], NN)
            acck[DIL_BAND:2 * DIL_BAND, :] += _dot(ds, qs[b0, :], TN)
            accv[DIL_BAND:2 * DIL_BAND, :] += _dot(_bf(pr), dos[b0, :], TN)

            def band(b, carry, nb=nb):
                st = pl.multiple_of((b - 1) * DIL_BAND, DIL_BAND)
                cur = pl.ds(st + DIL_BAND, DIL_BAND)
                both = pl.ds(st, 2 * DIL_BAND)
                acc_rows = pl.ds(st + DIL_BAND, 2 * DIL_BAND)
                not_first = ((b % nb) != 0).astype(jnp.int32)
                qb, dob, lb, ddb = qs[cur, :], dos[cur, :], lss[cur, :], dds[cur, :]
                kcat, vcat = ks[both, :], vs[both, :]
                s = jnp.where(_band_masks(not_first), _dot(qb, kcat, NT), NEG)
                pr = jnp.exp(s - jnp.concatenate([lb, lb], axis=1))
                ds = _bf(pr * (_dot(dob, vcat, NT) - jnp.concatenate([ddb, ddb], axis=1)))
                dqs[cur, :] = _dot(ds, kcat, NN)
                acck[acc_rows, :] += _dot(ds, qb, TN)
                accv[acc_rows, :] += _dot(_bf(pr), dob, TN)
                return carry

            lax.fori_loop(1, nbands, band, 0, unroll=2)
            for r in range(d):
                lo = r * n
                if d == 1:
                    dq_ref[...] = dqs[...]
                    dk_ref[...] = acck[DIL_BAND:DIL_BAND + t, :]
                    dv_ref[...] = accv[DIL_BAND:DIL_BAND + t, :]
                else:
                    dst = pl.ds(r, n, stride=d)
                    dq_ref[dst, :] = dq_ref[dst, :] + dqs[lo:lo + n, :]
                    dk_ref[dst, :] = dk_ref[dst, :] + acck[DIL_BAND + lo:DIL_BAND + lo + n, :]
                    dv_ref[dst, :] = dv_ref[dst, :] + accv[DIL_BAND + lo:DIL_BAND + lo + n, :]

    w = DIL_HEADS * DIL_HD
    vm = lambda dt, extra=0: pltpu.VMEM((t + extra, DIL_HD), dt)
    return pl.pallas_call(
        body, name="dil_bwd", grid=(DIL_HEADS,),
        in_specs=[col(0), col(0), col(2 * hoff), col(hoff), col(0), col(0)], out_specs=[outb] * 3,
        out_shape=[jax.ShapeDtypeStruct((t, w), F32)] * 3,
        scratch_shapes=[vm(BF16)] * 4 + [vm(F32)] * 3 + [vm(F32, DIL_BAND)] * 2,
        compiler_params=_params(("parallel",)),
    )(qr, kr, h_b, dmix, o_d, lse_tot)


def _ca_fwd(q, memkv, tq=512):
    t, d = q.shape
    m = memkv.shape[0]
    scale = CA_HD ** -0.5

    def body(q_ref, k_ref, v_ref, o_ref, ot_ref):
        for h in range(CA_HEADS):
            hs = slice(h * CA_HD, (h + 1) * CA_HD)
            s = _dot(q_ref[:, hs], k_ref[:, hs], NT) * scale
            p = jnp.exp(s - jnp.max(s, axis=-1, keepdims=True))
            p = p / jnp.sum(p, axis=-1, keepdims=True)
            o = _dot(_bf(p), v_ref[:, hs], NN).astype(BF16)
            o_ref[:, hs] = o
            ot_ref[hs, :] = o.T

    return pl.pallas_call(
        body, name="ca_fwd", grid=(t // tq,),
        in_specs=[pl.BlockSpec((tq, d), lambda i: (i, 0)), pl.BlockSpec((m, d), lambda i: (0, 0)),
                  pl.BlockSpec((m, d), lambda i: (0, 1))],
        out_specs=[pl.BlockSpec((tq, d), lambda i: (i, 0)), pl.BlockSpec((d, tq), lambda i: (0, i))],
        out_shape=[jax.ShapeDtypeStruct((t, d), BF16), jax.ShapeDtypeStruct((d, t), BF16)],
        compiler_params=_params(("parallel",)),
    )(q, memkv, memkv)


def _ca_bwd(q, memkv, do, tq=512):
    t, d = q.shape
    m = memkv.shape[0]
    scale = CA_HD ** -0.5

    def body(q_ref, k_ref, v_ref, do_ref, dq_ref, dkv_ref):
        i = pl.program_id(0)

        @pl.when(i == 0)
        def _():
            dkv_ref[...] = jnp.zeros_like(dkv_ref)

        for h in range(CA_HEADS):
            hs = slice(h * CA_HD, (h + 1) * CA_HD)
            q_h, k_h, v_h, do_h = q_ref[:, hs], k_ref[:, hs], v_ref[:, hs], do_ref[:, hs]
            s = _dot(q_h, k_h, NT) * scale
            p = jnp.exp(s - jnp.max(s, axis=-1, keepdims=True))
            p = p / jnp.sum(p, axis=-1, keepdims=True)
            dp = _dot(do_h, v_h, NT)
            ds = _bf(p * (dp - jnp.sum(p * dp, axis=-1, keepdims=True)) * scale)
            dq_ref[:, hs] = _dot(ds, k_h, NN).astype(BF16)
            dkv_ref[:, hs] += _dot(ds, q_h, TN)
            dkv_ref[:, d + h * CA_HD: d + (h + 1) * CA_HD] += _dot(_bf(p), do_h, TN)

    return pl.pallas_call(
        body, name="ca_bwd", grid=(t // tq,),
        in_specs=[pl.BlockSpec((tq, d), lambda i: (i, 0)), pl.BlockSpec((m, d), lambda i: (0, 0)),
                  pl.BlockSpec((m, d), lambda i: (0, 1)), pl.BlockSpec((tq, d), lambda i: (i, 0))],
        out_specs=[pl.BlockSpec((tq, d), lambda i: (i, 0)), pl.BlockSpec((m, 2 * d), lambda i: (0, 0))],
        out_shape=[jax.ShapeDtypeStruct((t, d), BF16), jax.ShapeDtypeStruct((m, 2 * d), F32)],
        compiler_params=_params(("arbitrary",)),
    )(q, memkv, memkv, do)


STRIP = 256


def _shift_down(u, n, row):
    return jnp.where(row >= n, pltpu.roll(u, n, 0), 0.0)


def _shift_up(u, n, row):
    t = u.shape[0]
    return jnp.where(row < t - n, pltpu.roll(u, t - n, 0), 0.0)


def _conv(u, cw_ref, row):
    return ((cw_ref[3:4, :] + cw_ref[0:1, :] * _shift_down(u, 2, row)) + cw_ref[1:2, :] * _shift_down(u, 1, row)) \
        + cw_ref[2:3, :] * u


def _swiglu_fwd(u0, cw):
    t, w = u0.shape[0], u0.shape[1] // 2
    ns = w // STRIP
    col = pl.BlockSpec((t, STRIP), lambda j: (0, j))
    col_up = pl.BlockSpec((t, STRIP), lambda j: (0, ns + j))
    cws = pl.BlockSpec((SUBLANES, STRIP), lambda j: (0, j))
    cws_up = pl.BlockSpec((SUBLANES, STRIP), lambda j: (0, ns + j))

    def body(g_ref, u_ref, cg_ref, cu_ref, a_ref, at_ref):
        row = lax.broadcasted_iota(jnp.int32, (t, STRIP), 0)
        gate = _conv(g_ref[...].astype(F32), cg_ref, row)
        up = _conv(u_ref[...].astype(F32), cu_ref, row)
        act = (gate * _sigmoid(gate) * up).astype(BF16)
        a_ref[...] = act
        at_ref[...] = act.T

    return pl.pallas_call(
        body, name="swiglu_fwd", grid=(ns,), in_specs=[col, col_up, cws, cws_up],
        out_specs=[col, pl.BlockSpec((STRIP, t), lambda j: (j, 0))],
        out_shape=[jax.ShapeDtypeStruct((t, w), BF16), jax.ShapeDtypeStruct((w, t), BF16)],
        compiler_params=_params(("parallel",)),
    )(u0, u0, cw, cw)


def _swiglu_bwd(u0, cw, da):
    t, w = u0.shape[0], u0.shape[1] // 2
    ns = w // STRIP
    col = pl.BlockSpec((t, STRIP), lambda j: (0, j))
    col_up = pl.BlockSpec((t, STRIP), lambda j: (0, ns + j))
    cws = pl.BlockSpec((SUBLANES, STRIP), lambda j: (0, j))
    cws_up = pl.BlockSpec((SUBLANES, STRIP), lambda j: (0, ns + j))

    def conv_bwd(du, u0, cw_ref, row, du0_ref, du0t_ref, dcw_ref):
        du0 = (cw_ref[2:3, :] * du + cw_ref[1:2, :] * _shift_up(du, 1, row)) + cw_ref[0:1, :] * _shift_up(du, 2, row)
        du0 = du0.astype(BF16)
        du0_ref[...] = du0
        du0t_ref[...] = du0.T
        dcw_ref[0:1, :] = jnp.sum(du * _shift_down(u0, 2, row), axis=0, keepdims=True)
        dcw_ref[1:2, :] = jnp.sum(du * _shift_down(u0, 1, row), axis=0, keepdims=True)
        dcw_ref[2:3, :] = jnp.sum(du * u0, axis=0, keepdims=True)
        dcw_ref[3:4, :] = jnp.sum(du, axis=0, keepdims=True)
        dcw_ref[4:8, :] = jnp.zeros((4, STRIP), F32)

    def body(g_ref, u_ref, cg_ref, cu_ref, da_ref, dg0_ref, du0_ref, dg0t_ref, du0t_ref, dcg_ref, dcu_ref):
        row = lax.broadcasted_iota(jnp.int32, (t, STRIP), 0)
        g0, up0 = g_ref[...].astype(F32), u_ref[...].astype(F32)
        gate = _conv(g0, cg_ref, row)
        up = _conv(up0, cu_ref, row)
        sg = _sigmoid(gate)
        da = da_ref[...].astype(F32)
        dgate = da * up * (sg * (1.0 + gate * (1.0 - sg)))
        dup = da * (gate * sg)
        conv_bwd(dgate, g0, cg_ref, row, dg0_ref, dg0t_ref, dcg_ref)
        conv_bwd(dup, up0, cu_ref, row, du0_ref, du0t_ref, dcu_ref)

    colt = pl.BlockSpec((STRIP, t), lambda j: (j, 0))
    return pl.pallas_call(
        body, name="swiglu_bwd", grid=(ns,), in_specs=[col, col_up, cws, cws_up, col],
        out_specs=[col, col, colt, colt, cws, cws],
        out_shape=[jax.ShapeDtypeStruct((t, w), BF16), jax.ShapeDtypeStruct((t, w), BF16),
                   jax.ShapeDtypeStruct((w, t), BF16), jax.ShapeDtypeStruct((w, t), BF16),
                   jax.ShapeDtypeStruct((SUBLANES, w), F32), jax.ShapeDtypeStruct((SUBLANES, w), F32)],
        compiler_params=_params(("parallel",)),
    )(u0, u0, cw, cw, da)


def _ffn_win_grad(dugt, duut, x2b, tn=512):
    t, d = x2b.shape
    tk = t // 2
    nk = t // tk
    sp, sw = FF_SLAB_P, FF_SLAB

    def body(ag_ref, au_ref, b_ref, o_ref, acc_ref):
        j, kk = pl.program_id(0), pl.program_id(2)

        @pl.when(kk == 0)
        def _():
            acc_ref[...] = jnp.zeros_like(acc_ref)

        @pl.when(j < 4)
        def _():
            acc_ref[...] += _dot(ag_ref[...], b_ref[...], NN)

        @pl.when(j >= 4)
        def _():
            acc_ref[...] += _dot(au_ref[...], b_ref[...], NN)

        @pl.when(kk == nk - 1)
        def _():
            o_ref[...] = acc_ref[:sw, :]

    return pl.pallas_call(
        body, name="mm_g_ffn_in", grid=(8, d // tn, nk),
        in_specs=[pl.BlockSpec((sp, tk), lambda j, n, kk: (jnp.minimum(j, 3), kk)),
                  pl.BlockSpec((sp, tk), lambda j, n, kk: (jnp.maximum(j - 4, 0), kk)),
                  pl.BlockSpec((tk, tn), lambda j, n, kk: (kk, n))],
        out_specs=pl.BlockSpec((None, sw, tn), lambda j, n, kk: (j, 0, n)),
        out_shape=jax.ShapeDtypeStruct((8, sw, d), F32),
        scratch_shapes=[pltpu.VMEM((sp, tn), F32)],
        compiler_params=_params(("parallel", "parallel", "arbitrary")),
    )(dugt, duut, x2b)


def _tile2d(r, c, limit=1 << 20):
    tr, tc = r, c
    while tr * tc * 4 > limit:
        if tr % (2 * SUBLANES) == 0:
            tr //= 2
        elif tc % (2 * LANES) == 0:
            tc //= 2
        else:
            break
    return tr, tc


def _adamw_math(w, m, v, g):
    c1 = 1.0 - ADAM_B1 ** ADAM_STEP
    c2 = 1.0 - ADAM_B2 ** ADAM_STEP
    mm = ADAM_B1 * m + (1.0 - ADAM_B1) * g
    vv = ADAM_B2 * v + (1.0 - ADAM_B2) * (g * g)
    delta = -ADAM_LR * ((mm / c1) / (jnp.sqrt(vv / c2) + ADAM_EPS) + ADAM_WD * w)
    return delta, mm, vv


def _adamw(w, m, v, g, name):
    r, c = w.shape
    blk = pl.BlockSpec((r, c), lambda i: (0, 0))

    def body(w_ref, m_ref, v_ref, gi_ref, g_ref, d_ref, nm_ref, nv_ref):
        g = gi_ref[...]
        d_ref[...], nm_ref[...], nv_ref[...] = _adamw_math(w_ref[...], m_ref[...], v_ref[...], g)
        g_ref[...] = g

    return pl.pallas_call(body, name=name, grid=(1,), in_specs=[blk] * 4, out_specs=[blk] * 4,
                          out_shape=[jax.ShapeDtypeStruct((r, c), F32)] * 4,
                          compiler_params=_params(("arbitrary",)))(w, m, v, g)


def _pair_add(gs, ra, core, name):
    _, _, r, c = gs.shape
    tr, tc = _tile2d(r, c)
    blk = pl.BlockSpec((None, tr, tc), lambda k, i, j, s: (k, i, j))

    def body(s_ref, g_ref, r_ref, o_ref, ob_ref):
        p = g_ref[...] + r_ref[...]
        o_ref[...] = p
        ob_ref[...] = p.astype(BF16)

    gspec = pltpu.PrefetchScalarGridSpec(
        num_scalar_prefetch=1, grid=(4, r // tr, c // tc),
        in_specs=[pl.BlockSpec((None, None, tr, tc), lambda k, i, j, s: (k, s[0], i, j)), blk], out_specs=[blk, blk])
    return pl.pallas_call(body, name=name, grid_spec=gspec,
                          out_shape=[jax.ShapeDtypeStruct((4, r, c), F32), jax.ShapeDtypeStruct((4, r, c), BF16)],
                          compiler_params=_params(("parallel", "parallel", "parallel")))(core, gs, ra)


def _small_reduce(gathered):
    nd, r, n = gathered.shape
    tn = 2048 if n % 2048 == 0 else n
    def body(g_ref, s_ref, t_ref):
        s = g_ref[0]
        for dv in range(1, nd):
            s = s + g_ref[dv]
        s_ref[...] = s
        t_ref[...] = jnp.broadcast_to(jnp.sum(s, axis=0, keepdims=True), (r, tn))

    return pl.pallas_call(
        body, name="small_reduce", grid=(n // tn,),
        in_specs=[pl.BlockSpec((nd, r, tn), lambda j: (0, 0, j))],
        out_specs=[pl.BlockSpec((r, tn), lambda j: (0, j))] * 2,
        out_shape=[jax.ShapeDtypeStruct((r, n), F32)] * 2, compiler_params=_params(("parallel",)),
    )(gathered)


HBM = pl.BlockSpec(memory_space=pltpu.HBM)


def _all_gather(arrs, name):
    n = len(arrs)

    def body(*refs):
        ins, outs = refs[:n], refs[n:2 * n]
        send, recv, lsem = refs[2 * n:]
        x, y, c = lax.axis_index("x"), lax.axis_index("y"), lax.axis_index("c")
        me, sib = (x, y, c), (x, y, 1 - c)
        chips = [(1 - x, y), (x, 1 - y), (1 - x, 1 - y)]

        def slot(w, p):
            return outs[w].at[4 * p[0] + 2 * p[1] + p[2]]

        def cp(w, k, block, to, src=None):
            return pltpu.make_async_remote_copy(
                src_ref=slot(w, block) if src is None else src, dst_ref=slot(w, block),
                send_sem=send.at[w * 7 + k], recv_sem=recv.at[w * 7 + k], device_id=to, device_id_type=MESH)

        mine = [pltpu.make_async_copy(ins[w], slot(w, me), lsem.at[w]) for w in range(n)]
        for m in mine:
            m.start()
        first = []
        for w in range(n):
            first.append(cp(w, 0, me, sib, src=ins[w]))
            first += [cp(w, 1 + j, me, (*chip, c), src=ins[w]) for j, chip in enumerate(chips)]
        for f in first:
            f.start()
        passed = []
        for j, chip in enumerate(chips):
            for w in range(n):
                cp(w, 1 + j, (*chip, c), me).wait_recv()
                fwd = cp(w, 4 + j, (*chip, c), sib)
                fwd.start()
                passed.append(fwd)
        for w in range(n):
            cp(w, 0, sib, me).wait_recv()
            for j, chip in enumerate(chips):
                cp(w, 4 + j, (*chip, 1 - c), me).wait_recv()
        for f in first + passed:
            f.wait_send()
        for m in mine:
            m.wait()

    return pl.pallas_call(
        body, name=name, in_specs=[HBM] * n, out_specs=[HBM] * n,
        out_shape=[jax.ShapeDtypeStruct((8,) + a.shape, a.dtype) for a in arrs],
        scratch_shapes=[pltpu.SemaphoreType.DMA((7 * n,)), pltpu.SemaphoreType.DMA((7 * n,)),
                        pltpu.SemaphoreType.DMA((n,))],
    )(*arrs)


def _sibling_exchange(arrs, name):
    n = len(arrs)

    def body(*refs):
        ins, outs = refs[:n], refs[n:2 * n]
        send, recv = refs[2 * n:]
        x, y, c = lax.axis_index("x"), lax.axis_index("y"), lax.axis_index("c")
        copies = [pltpu.make_async_remote_copy(
            src_ref=ins[w].at[:, 1 - c], dst_ref=outs[w], send_sem=send.at[w], recv_sem=recv.at[w],
            device_id=(x, y, 1 - c), device_id_type=MESH) for w in range(n)]
        for cpy in copies:
            cpy.start()
        for cpy in copies:
            cpy.wait()

    return pl.pallas_call(
        body, name=name, in_specs=[HBM] * n, out_specs=[HBM] * n,
        out_shape=[jax.ShapeDtypeStruct((a.shape[0],) + a.shape[2:], a.dtype) for a in arrs],
        scratch_shapes=[pltpu.SemaphoreType.DMA((n,)), pltpu.SemaphoreType.DMA((n,))],
    )(*arrs)


def _chip_exchange(arrs, name):
    n = len(arrs)

    def body(*refs):
        ins, outs = refs[:n], refs[n:2 * n]
        send, recv = refs[2 * n:]
        x, y, c = lax.axis_index("x"), lax.axis_index("y"), lax.axis_index("c")
        chips = [(1 - x, y), (x, 1 - y), (1 - x, 1 - y)]
        copies = []
        for w in range(n):
            for j, (cx, cy) in enumerate(chips):
                copies.append(pltpu.make_async_remote_copy(
                    src_ref=ins[w].at[2 * cx + cy], dst_ref=outs[w].at[j], send_sem=send.at[3 * w + j],
                    recv_sem=recv.at[3 * w + j], device_id=(cx, cy, c), device_id_type=MESH))
        for cpy in copies:
            cpy.start()
        for cpy in copies:
            cpy.wait()

    return pl.pallas_call(
        body, name=name, in_specs=[HBM] * n, out_specs=[HBM] * n,
        out_shape=[jax.ShapeDtypeStruct((3,) + a.shape[1:], a.dtype) for a in arrs],
        scratch_shapes=[pltpu.SemaphoreType.DMA((3 * n,)), pltpu.SemaphoreType.DMA((3 * n,))],
    )(*arrs)


SEM = pl.BlockSpec(memory_space=pltpu.SEMAPHORE)
ANY = pl.BlockSpec(memory_space=pl.ANY)
EFFECT = pltpu.SideEffectType.DATAFLOW_SIDE_EFFECTING
N_PEERS = 7


def _peers(x, y, c):
    return [((1 - x) if k & 4 else x, (1 - y) if k & 2 else y, (1 - c) if k & 1 else c) for k in range(1, 8)]


def _spread_copies(src_refs, land_refs, send, recv, gather):
    x, y, c = lax.axis_index("x"), lax.axis_index("y"), lax.axis_index("c")
    me = 4 * x + 2 * y + c
    copies = []
    for w in range(len(src_refs)):
        for k, (px, py, pc) in enumerate(_peers(x, y, c)):
            p = 4 * px + 2 * py + pc
            copies.append((pltpu.make_async_remote_copy(
                src_ref=src_refs[w] if gather else src_refs[w].at[p], dst_ref=land_refs[w].at[me],
                send_sem=send[w].at[k], recv_sem=recv[w].at[k], device_id=(px, py, pc), device_id_type=MESH),
                pltpu.make_async_remote_copy(
                src_ref=src_refs[w] if gather else src_refs[w].at[p], dst_ref=land_refs[w].at[p],
                send_sem=send[w].at[k], recv_sem=recv[w].at[k], device_id=(px, py, pc), device_id_type=MESH)))
    return copies


def _hbm(a):
    return pltpu.with_memory_space_constraint(a, pltpu.HBM)


def _spread_start(srcs, lands, after, gather, name):
    n = len(srcs)

    def body(*refs):
        src_refs, land_refs = refs[:n], refs[n:2 * n]
        outs = refs[2 * n + 1:]
        send, recv, token = outs[:n], outs[n:2 * n], outs[4 * n]
        for start, _ in _spread_copies(src_refs, land_refs, send, recv, gather):
            start.start()
        token[...] = jnp.zeros_like(token)

    res = pl.pallas_call(
        body, name=name,
        out_shape=tuple([pltpu.SemaphoreType.DMA((N_PEERS,))] * (2 * n)
                        + [pltpu.HBM(a.shape, a.dtype) for a in srcs] + [pltpu.HBM(a.shape, a.dtype) for a in lands]
                        + [jax.ShapeDtypeStruct((SUBLANES, LANES), F32)]),
        in_specs=[HBM] * (2 * n) + [ANY],
        out_specs=tuple([SEM] * (2 * n) + [HBM] * (2 * n) + [pl.BlockSpec(memory_space=pltpu.VMEM)]),
        input_output_aliases={i: 2 * n + i for i in range(2 * n)},
        compiler_params=pltpu.CompilerParams(has_side_effects=EFFECT),
    )(*[_hbm(a) for a in srcs], *[_hbm(a) for a in lands], after)
    return res[:n], res[n:2 * n], res[2 * n:3 * n], res[3 * n:4 * n], res[4 * n]


def _spread_wait(send, recv, srcs, lands, after, gather, name):
    n = len(srcs)

    def body(*refs):
        src_refs, land_refs = refs[:n], refs[n:2 * n]
        send_refs, recv_refs = refs[2 * n:3 * n], refs[3 * n:4 * n]
        for _, arrive in _spread_copies(src_refs, land_refs, send_refs, recv_refs, gather):
            arrive.wait_send()
            arrive.wait_recv()

    res = pl.pallas_call(
        body, name=name,
        out_shape=tuple([pltpu.HBM(a.shape, a.dtype) for a in srcs] + [pltpu.HBM(a.shape, a.dtype) for a in lands]),
        in_specs=[HBM] * (2 * n) + [SEM] * (2 * n) + [ANY],
        out_specs=tuple([HBM] * (2 * n)),
        input_output_aliases={i: i for i in range(2 * n)},
        compiler_params=pltpu.CompilerParams(has_side_effects=EFFECT),
    )(*srcs, *lands, *send, *recv, after)
    return res[n:]


def _landing(shape, dtype, own, me):
    return lax.dynamic_update_index_in_dim(lax.empty((8,) + shape, dtype), own, me, 0)


def _pad_cols(a, to):
    return jnp.pad(a, ((0, 0), (0, to - a.shape[1])))


N_GLR = GLA_W + GLA_RANK
FF_SLAB = D_FF // 4
FF_SLAB_P = FFP // 4


TRANSPOSED = ("w_in", "ffn_w_in")


def _prepare_sub1(gath):
    w_in_t = gath["w_in"].reshape(-1, gath["w_in"].shape[2])
    w2 = jnp.concatenate([gath["gla_gate_w2"][s] for s in range(8)], axis=1)
    return {"w_a_t": jnp.pad(w_in_t[:N_GLR], ((0, HA_W - N_GLR), (0, 0))), "w_b_t": w_in_t[N_GLR:],
            "w2p": jnp.pad(w2, ((0, LANES - GLA_RANK), (0, 0)))}


def _prepare_ffn_in(g):
    f = jnp.pad(g, ((0, 0), (0, FF_SLAB_P - FF_SLAB), (0, 0)))
    return f.reshape(2 * FFP, f.shape[2])


def _prepare_ffn_out(g):
    return jnp.pad(g.reshape(4, FF_SLAB, -1), ((0, 0), (0, FF_SLAB_P - FF_SLAB), (0, 0))).reshape(FFP, -1)


def _prepare_conv(g, conv_b):
    padc = FF_SLAB_P - FF_SLAB
    cw = jnp.pad(g, ((0, 0), (0, 0), (0, padc)))
    cb = jnp.pad(conv_b.reshape(8, 1, FF_SLAB), ((0, 0), (0, 0), (0, padc)))
    rows = jnp.concatenate([cw, cb, jnp.zeros((8, 4, FF_SLAB_P), F32)], axis=1)
    return jnp.concatenate([rows[s] for s in range(8)], axis=1)


def _prepare_ffn(gath, conv_b):
    return {"w_ffn_t": _prepare_ffn_in(gath["ffn_w_in"]), "wo": _prepare_ffn_out(gath["ffn_w_out"]),
            "cw": _prepare_conv(gath["ffn_conv_w"], conv_b)}


def _unpad_ff(a):
    r = a.shape[0]
    return a.reshape(r, 4, FF_SLAB_P)[:, :, :FF_SLAB].reshape(r, D_FF)


def _grad_slabs(g):
    w_in_t = jnp.concatenate([g["w_a_t"][:N_GLR], g["w_b_t"]], axis=0)
    s = {"w_in": w_in_t.reshape(4, 2, w_in_t.shape[0] // 8, w_in_t.shape[1])}
    for n in ("w_out", "ca_wq", "ca_wo"):
        s[n] = _to_slabs(n, g[n])
    for n in ("ca_wkv", "ffn_w_in"):
        s[n] = g[n].reshape((4, 2) + g[n].shape[1:])
    wo = g["wo"].reshape(4, FF_SLAB_P, -1)[:, :FF_SLAB]
    s["ffn_w_out"] = wo.reshape(4, 2, FF_SLAB // 2, wo.shape[-1])
    return s


class _AtHand:
    def __init__(self, p):
        self.p = p
        self.token = None

    def sub2(self, after):
        return self.p

    def ffn_in(self, after):
        return self.p["w_ffn_t"]

    def ffn_out(self, after):
        return self.p["wo"]


def _local_step(x, mem, positions, target, p, small, stages=None):
    t, d = x.shape
    stages = _AtHand(p) if stages is None else stages
    w_a_t, w_b_t, w2p, cw = p["w_a_t"], p["w_b_t"], p["w2p"], p["cw"]
    tabs = _rope_tables(positions)
    xb = x.astype(BF16) if stages.token is None else (x + stages.token[0, 0]).astype(BF16)
    memb = mem.astype(BF16)

    h_a = _matmul(xb, w_a_t, "nt", F32, 512, 640, d, "mm_h_a")
    h_b = _matmul(xb, w_b_t, "nt", F32, 512, 1024, d, "mm_h_b")
    o_g, o_raw, s_before = _gla_fwd(h_a, w2p, small["gla_gate_b"], small["gla_norm_g"])
    qr, kr = _rope_fwd(h_b, tabs)
    o_d_b, o_d, lse_tot = _dil_fwd_all(qr, kr, h_b)
    mixin = jnp.concatenate([o_g, o_d_b], axis=1)
    wts = stages.sub2(mixin)
    mix = _matmul(mixin, wts["w_out"], "nn", F32, 512, 1024, d, "mm_mix")
    x1, x1b, x1t = _ln_fwd(x, mix, small["ln1_g"], small["ln1_b"], "ln1_fwd")

    q_ca = _matmul(x1b, wts["ca_wq"], "nn", BF16, 512, 1024, d, "mm_caq")
    kvw = wts["ca_wkv"].shape[2]
    memkv = _matmul(memb, wts["ca_wkv"], "nn", BF16, mem.shape[0], kvw, d, "mm_memkv", b_slabs=True)
    o_c, o_ct = _ca_fwd(q_ca, memkv)
    ca_out = _matmul(o_c, wts["ca_wo"], "nn", F32, 512, 1024, d, "mm_cao")
    x2, x2b, x2t = _ln_fwd(x1, ca_out, small["ln2_g"], small["ln2_b"], "ln2_fwd")

    w_ffn_t = stages.ffn_in(x2b)
    u0 = _matmul(x2b, w_ffn_t, "nt", BF16, 512, 512, d, "mm_u0")
    act, act_t = _swiglu_fwd(u0, cw)
    wo = stages.ffn_out(act)
    ffn = _matmul(act, wo, "nn", F32, 512, 512, FFP, "mm_ffn")

    dp3, dp3b, dg3, db3, loss_part = _ln_bwd(x2, ffn, small["ln3_g"], small["ln3_b"], target, True, "ln3_bwd")
    g_wo = _matmul(act_t, dp3b, "nn", F32, 512, 1024, t // 2, "mm_g_wo")
    dact = _matmul(dp3b, wo, "nt", BF16, 512, 512, d, "mm_dact")
    dug, duu, dug_t, duu_t, dcwg, dcwu = _swiglu_bwd(u0, cw, dact)
    g_ffn_in = _ffn_win_grad(dug_t, duu_t, x2b)
    dx2 = _matmul(dug, w_ffn_t, "nn", F32, 512, 512, FFP // 2, "mm_dx2_g", resid=dp3, resid_scale=ALPHA)
    dx2 = _matmul(duu, w_ffn_t, "nn", F32, 512, 512, FFP // 2, "mm_dx2_u", resid=dx2, b_k_off=2)

    dp2, dp2b, dg2, db2 = _ln_bwd(x1, ca_out, small["ln2_g"], small["ln2_b"], dx2, False, "ln2_bwd")
    g_cao = _matmul(o_ct, dp2b, "nn", F32, 512, 1024, t // 2, "mm_g_cao")
    do_c = _matmul(dp2b, wts["ca_wo"], "nt", BF16, 512, 1024, d, "mm_do_c")
    dq_ca, dmemkv = _ca_bwd(q_ca, memkv, do_c)
    g_caq = _matmul(x1t, dq_ca, "nn", F32, 512, 1024, t // 2, "mm_g_caq")
    g_cakv = _matmul(memb, dmemkv.astype(BF16), "tn", F32, 512, kvw, mem.shape[0], "mm_g_cakv", out_slabs=True)
    dx1 = _matmul(dq_ca, wts["ca_wq"], "nt", F32, 512, 1024, d, "mm_dx1", resid=dp2, resid_scale=ALPHA)

    dp1, dp1b, dg1, db1 = _ln_bwd(x, mix, small["ln1_g"], small["ln1_b"], dx1, False, "ln1_bwd")
    g_wout = _matmul(mixin, dp1b, "tn", F32, 512, 1024, 1024, "mm_g_wout")
    dmix = _matmul(dp1b, wts["w_out"], "nt", F32, 512, 1024, d, "mm_dmix")
    dh_a, dw2, dgate_b, dnorm_g = _gla_bwd(h_a, w2p, small["gla_gate_b"], small["gla_norm_g"], o_raw, s_before, dmix)
    dq_d, dk_d, dv_d = _dil_bwd_all(qr, kr, h_b, dmix, o_d, lse_tot)
    dh_b = _dil_dh(dq_d, dk_d, dv_d, tabs)
    g_wa_t = _matmul(dh_a, xb, "tn", F32, 640, 1024, 1024, "mm_g_wa")
    g_wb_t = _matmul(dh_b, xb, "tn", F32, 512, 1024, 1024, "mm_g_wb")
    dx = _matmul(dh_a, w_a_t, "nn", F32, 512, 512, HA_W, "mm_dx_a", resid=dp1, resid_scale=ALPHA)
    dx = _matmul(dh_b, w_b_t, "nn", F32, 512, 512, HB_W, "mm_dx_b", resid=dx)

    grads = {"w_a_t": g_wa_t, "w_b_t": g_wb_t, "w_out": g_wout, "ca_wq": g_caq, "ca_wkv": g_cakv, "ca_wo": g_cao,
             "ffn_w_in": g_ffn_in, "wo": g_wo}
    small_parts = {
        "gla_gate_b": dgate_b, "gla_norm_g": dnorm_g, "ln1_g": dg1, "ln1_b": db1, "ln2_g": dg2, "ln2_b": db2,
        "ln3_g": dg3, "ln3_b": db3,
        "conv": jnp.concatenate([_unpad_ff(dcwg), _unpad_ff(dcwu)], axis=1),
        "gla_gate_w2": dw2[:GLA_RANK],
    }
    return loss_part, dx, grads, small_parts


BIG = ("w_in", "w_out", "ca_wq", "ca_wkv", "ca_wo", "ffn_w_in", "ffn_w_out")
COL_SHARDED = ("w_in", "ca_wkv", "ffn_w_in")
SMALL_ORDER = ("gla_gate_b", "gla_norm_g", "ln1_g", "ln1_b", "ln2_g", "ln2_b", "ln3_g", "ln3_b")


def _gathered_full(name, g):
    if name in COL_SHARDED:
        return g.transpose(1, 0, 2).reshape(g.shape[1], 8 * g.shape[2])
    return g.reshape(8 * g.shape[1], g.shape[2])


def _to_slabs(name, full):
    if name in COL_SHARDED:
        r, cc = full.shape
        s = full.reshape(r, 8, cc // 8).transpose(1, 0, 2)
    else:
        rr, c = full.shape
        s = full.reshape(8, rr // 8, c)
    return s.reshape((4, 2) + s.shape[1:])


def kernel(x, mem, positions, w_in, gla_gate_w2, gla_gate_b, gla_norm_g, w_out, ln1_g, ln1_b, ca_wq, ca_wkv, ca_wo, ln2_g, ln2_b, ffn_w_in, ffn_conv_w, ffn_conv_b, ffn_w_out, ln3_g, ln3_b, loss_target, m_w_in, m_gla_gate_w2, m_gla_gate_b, m_gla_norm_g, m_w_out, m_ln1_g, m_ln1_b, m_ca_wq, m_ca_wkv, m_ca_wo, m_ln2_g, m_ln2_b, m_ffn_w_in, m_ffn_conv_w, m_ffn_conv_b, m_ffn_w_out, m_ln3_g, m_ln3_b, v_w_in, v_gla_gate_w2, v_gla_gate_b, v_gla_norm_g, v_w_out, v_ln1_g, v_ln1_b, v_ca_wq, v_ca_wkv, v_ca_wo, v_ln2_g, v_ln2_b, v_ffn_w_in, v_ffn_conv_w, v_ffn_conv_b, v_ffn_w_out, v_ln3_g, v_ln3_b):
    weights = dict(w_in=w_in, gla_gate_w2=gla_gate_w2, gla_gate_b=gla_gate_b, gla_norm_g=gla_norm_g, w_out=w_out,
                   ln1_g=ln1_g, ln1_b=ln1_b, ca_wq=ca_wq, ca_wkv=ca_wkv, ca_wo=ca_wo, ln2_g=ln2_g, ln2_b=ln2_b,
                   ffn_w_in=ffn_w_in, ffn_conv_w=ffn_conv_w, ffn_conv_b=ffn_conv_b, ffn_w_out=ffn_w_out,
                   ln3_g=ln3_g, ln3_b=ln3_b)
    moms = dict(w_in=(m_w_in, v_w_in), gla_gate_w2=(m_gla_gate_w2, v_gla_gate_w2), gla_gate_b=(m_gla_gate_b, v_gla_gate_b),
                gla_norm_g=(m_gla_norm_g, v_gla_norm_g), w_out=(m_w_out, v_w_out), ln1_g=(m_ln1_g, v_ln1_g),
                ln1_b=(m_ln1_b, v_ln1_b), ca_wq=(m_ca_wq, v_ca_wq), ca_wkv=(m_ca_wkv, v_ca_wkv), ca_wo=(m_ca_wo, v_ca_wo),
                ln2_g=(m_ln2_g, v_ln2_g), ln2_b=(m_ln2_b, v_ln2_b), ffn_w_in=(m_ffn_w_in, v_ffn_w_in),
                ffn_conv_w=(m_ffn_conv_w, v_ffn_conv_w), ffn_conv_b=(m_ffn_conv_b, v_ffn_conv_b),
                ffn_w_out=(m_ffn_w_out, v_ffn_w_out), ln3_g=(m_ln3_g, v_ln3_g), ln3_b=(m_ln3_b, v_ln3_b))
    order = list(weights)
    xi, yi, ci = lax.axis_index("x"), lax.axis_index("y"), lax.axis_index("c")
    me = 4 * xi + 2 * yi + ci

    def travel(n, a):
        return jnp.swapaxes(a, 1, 2) if n in TRANSPOSED else a

    shard = {n: travel(n, weights[n]).astype(BF16)[0] for n in BIG}
    first = _all_gather([shard["w_in"], gla_gate_w2.astype(BF16)[0], ffn_conv_w[0]], "ag_first")
    p = _prepare_sub1({"w_in": first[0], "gla_gate_w2": first[1]})
    p["cw"] = _prepare_conv(first[2], ffn_conv_b)
    later = ("w_out", "ca_wq", "ca_wkv", "ca_wo", "ffn_w_in", "ffn_w_out")
    srcs = [shard[n] for n in later]
    lands = [_landing(shard[n].shape, BF16, shard[n], me) for n in later]
    send, recv, srcs, lands, token = _spread_start(srcs, lands, first[0], True, "ag_rest_start")

    class stages:
        pass

    stages.token = token

    def arrived(lo, hi, after, name):
        return _spread_wait(send[lo:hi], recv[lo:hi], srcs[lo:hi], lands[lo:hi], after, True, name)

    def sub2(after):
        g = dict(zip(later[:4], arrived(0, 4, after, "ag_wait_attn")))
        w = {n: _gathered_full(n, g[n]) for n in ("w_out", "ca_wq", "ca_wo")}
        w["ca_wkv"] = g["ca_wkv"]
        return w

    stages.sub2 = sub2
    stages.ffn_in = lambda after: _prepare_ffn_in(arrived(4, 5, after, "ag_wait_ffn_in")[0])
    stages.ffn_out = lambda after: _prepare_ffn_out(arrived(5, 6, after, "ag_wait_ffn_out")[0])
    small = dict(gla_gate_b=gla_gate_b, gla_norm_g=gla_norm_g, ln1_g=ln1_g, ln1_b=ln1_b, ln2_g=ln2_g, ln2_b=ln2_b,
                 ln3_g=ln3_g, ln3_b=ln3_b)

    loss_part, dx, grads, small_parts = _local_step(x[0], mem[0], positions[0], loss_target[0], p, small, stages)
    loss = lax.psum(jnp.sum(loss_part), ("x", "y", "c"))

    slab_of = _grad_slabs(grads)
    slabs = [slab_of[n] for n in BIG]
    from_sib = _sibling_exchange(slabs, "rs_sibling")
    core = ci.reshape(1).astype(jnp.int32)
    pair32, pair16 = [], []
    for n, s, r in zip(BIG, slabs, from_sib):
        p32, p16 = _pair_add(s, r, core, f"pair_add_{n}")
        pair32.append(p32)
        pair16.append(p16)
    from_chips = _chip_exchange(pair16, "rs_chips")
    chip = (2 * xi + yi).reshape(1).astype(jnp.int32)
    out = {}
    for n, p32, rc in zip(BIG, pair32, from_chips):
        m_, v_ = moms[n]
        res4 = _adamw_big(travel(n, weights[n]), travel(n, m_), travel(n, v_), p32, rc, chip, f"adamw_{n}")
        out[n] = [travel(n, a) for a in res4]

    packed = jnp.concatenate([small_parts[n] for n in SMALL_ORDER] + [small_parts["conv"],
                             small_parts["gla_gate_w2"].reshape(SUBLANES, -1)], axis=1)
    pad = (-packed.shape[1]) % 2048
    packed = jnp.pad(packed, ((0, 0), (0, pad)))
    (allp,) = _all_gather([packed], "ag_small")
    dev_sum, row_sum = _small_reduce(allp)
    off = 0
    for n in SMALL_ORDER:
        width = weights[n].shape[1]
        g = row_sum[0:1, off:off + width]
        off += width
        m_, v_ = moms[n]
        out[n] = _adamw(weights[n], m_, v_, g, f"adamw_{n}")
    conv_g = dev_sum[:, off:off + 2 * D_FF]
    off += 2 * D_FF
    g_cb = conv_g[3:4]
    out["ffn_conv_b"] = _adamw(ffn_conv_b, m_ffn_conv_b, v_ffn_conv_b, g_cb, "adamw_ffn_conv_b")
    wsh = ffn_conv_w.shape[2]
    g_cw = lax.dynamic_slice_in_dim(conv_g[0:3], me * wsh, wsh, axis=1)
    out["ffn_conv_w"] = _adamw(ffn_conv_w[0], m_ffn_conv_w[0], v_ffn_conv_w[0], g_cw, "adamw_ffn_conv_w")
    w2_g = dev_sum[:, off:off + GLA_RANK * GLA_HEADS * GLA_DK // SUBLANES].reshape(GLA_RANK, GLA_HEADS * GLA_DK)
    wsh2 = gla_gate_w2.shape[2]
    g_w2 = lax.dynamic_slice_in_dim(w2_g, me * wsh2, wsh2, axis=1)
    out["gla_gate_w2"] = _adamw(gla_gate_w2[0], m_gla_gate_w2[0], v_gla_gate_w2[0], g_w2, "adamw_gla_gate_w2")

    def shaped(n, a):
        return a.reshape(weights[n].shape)

    res = [loss, dx[None]]
    for k in range(4):
        res += [shaped(n, out[n][k]) for n in order]
    return tuple(res)


def _adamw_big(w, m, v, p32, rc, chip, name):
    _, r, c = w.shape
    tr, tc = _tile2d(r, c)
    blk = pl.BlockSpec((None, tr, tc), lambda i, j, s: (0, i, j))
    own = pl.BlockSpec((None, tr, tc), lambda i, j, s: (s[0], i, j))
    others = [pl.BlockSpec((None, tr, tc), lambda i, j, s, k=k: (k, i, j)) for k in range(3)]

    def body(s_ref, w_ref, m_ref, v_ref, p_ref, r0_ref, r1_ref, r2_ref, g_ref, d_ref, nm_ref, nv_ref):
        g = ((p_ref[...] + r0_ref[...].astype(F32)) + r1_ref[...].astype(F32)) + r2_ref[...].astype(F32)
        d_ref[...], nm_ref[...], nv_ref[...] = _adamw_math(w_ref[...], m_ref[...], v_ref[...], g)
        g_ref[...] = g

    gs = pltpu.PrefetchScalarGridSpec(num_scalar_prefetch=1, grid=(r // tr, c // tc),
                                      in_specs=[blk, blk, blk, own] + others, out_specs=[blk] * 4)
    return pl.pallas_call(body, name=name, grid_spec=gs, out_shape=[jax.ShapeDtypeStruct((1, r, c), F32)] * 4,
                          compiler_params=_params(("parallel", "parallel")))(chip, w, m, v, p32, rc, rc, rc)
```

```python
import functools
import math

import jax
import jax.numpy as jnp
from jax import lax
from jax.experimental import pallas as pl
from jax.experimental.pallas import tpu as pltpu

F32 = jnp.float32
BF16 = jnp.bfloat16
MESH = pl.DeviceIdType.MESH

D_MODEL = 2048
LN_EPS = 1e-5
GLA_HEADS = 4
GLA_DV = 256
GLA_DK = 128
GLA_RANK = 16
GLA_TAU = 16.0
GLA_CHUNK = 64
DIL_HD = 128
DIL_HEADS = 8
DIL_BAND = 128
DIL_DILATIONS = (1, 4, 16)
ROPE_THETA = 500000.0
ROPE_DIMS = 32
CA_HEADS = 4
CA_HD = 512
D_FF = 5504
ALPHA = 2.0 ** 0.25
ADAM_LR = 0.001
ADAM_B1 = 0.9
ADAM_B2 = 0.999
ADAM_EPS = 1e-08
ADAM_WD = 0.01
ADAM_STEP = 10

LANES = 128
SUBLANES = 8
VMEM_LIMIT = 56 * 1024 * 1024

GLA_W = 2 * GLA_HEADS * GLA_DK + 2 * GLA_HEADS * GLA_DV
HA_W = GLA_W + LANES
HB_W = 3 * DIL_HEADS * DIL_HD
FFP = 5632
NEG = -1e30


def _params(sem):
    return pltpu.CompilerParams(dimension_semantics=sem, vmem_limit_bytes=VMEM_LIMIT)


def _sigmoid(x):
    return 1.0 / (1.0 + jnp.exp(-x))


def _dot(a, b, dn, precision=None):
    return lax.dot_general(a, b, (dn, ((), ())), preferred_element_type=F32, precision=precision)


NN = ((1,), (0,))
NT = ((1,), (1,))
TN = ((0,), (0,))


def _bf(v):
    return v if v.dtype == BF16 else v.astype(BF16)


def _matmul(a, b, kind, out_dtype, tm, tn, tk, name, resid=None, resid_scale=1.0, b_k_off=0, b_slabs=False,
            out_slabs=False, also_bf16=False):
    if b_slabs:
        assert kind != "nt" and b.shape[2] == tn
        k2, n = b.shape[1], b.shape[0] * tn
    elif kind == "nt":
        n, k2 = b.shape
    else:
        k2, n = b.shape
    (k, m) = a.shape if kind == "tn" else a.shape[::-1]
    assert k2 >= k and (k2 == k or not b_slabs) and m % tm == 0 and n % tn == 0 and k % tk == 0, \
        (name, a.shape, b.shape, tm, tn, tk)
    nk = k // tk
    dn = {"nn": NN, "nt": NT, "tn": TN}[kind]
    a_spec = pl.BlockSpec((tk, tm), lambda i, j, kk: (kk, i)) if kind == "tn" else pl.BlockSpec((tm, tk), lambda i, j, kk: (i, kk))
    if b_slabs:
        b_spec = pl.BlockSpec((None, tk, tn), lambda i, j, kk: (j, kk, 0))
    elif kind == "nt":
        b_spec = pl.BlockSpec((tn, tk), lambda i, j, kk: (j, kk + b_k_off))
    else:
        b_spec = pl.BlockSpec((tk, tn), lambda i, j, kk: (kk + b_k_off, j))
    if out_slabs:
        o_spec = pl.BlockSpec((None, tm, tn), lambda i, j, kk: (j, i, 0))
        o_shape = (n // tn, m, tn)
    else:
        o_spec = pl.BlockSpec((tm, tn), lambda i, j, kk: (i, j))
        o_shape = (m, n)
    has_resid = resid is not None

    def body(*refs):
        if has_resid:
            a_ref, b_ref, r_ref, o_ref = refs[:4]
        else:
            a_ref, b_ref, o_ref = refs[:3]
            r_ref = None
        ob_ref = refs[4 if has_resid else 3] if also_bf16 else None
        part = _dot(_bf(a_ref[...]), _bf(b_ref[...]), dn)

        def finish(acc):
            if has_resid:
                acc = acc + resid_scale * r_ref[...].astype(F32)
            o_ref[...] = acc.astype(out_dtype)
            if also_bf16:
                ob_ref[...] = acc.astype(BF16)

        if nk == 1:
            finish(part)
        else:
            acc_ref = refs[-1]
            kk = pl.program_id(2)

            @pl.when(kk == 0)
            def _():
                acc_ref[...] = part

            @pl.when(kk > 0)
            def _():
                acc_ref[...] += part

            @pl.when(kk == nk - 1)
            def _():
                finish(acc_ref[...])

    in_specs = [a_spec, b_spec] + ([o_spec] if has_resid else [])
    args = (a, b) + ((resid,) if has_resid else ())
    o_struct = jax.ShapeDtypeStruct(o_shape, out_dtype)
    return pl.pallas_call(
        body, name=name, out_shape=[o_struct, jax.ShapeDtypeStruct(o_shape, BF16)] if also_bf16 else o_struct,
        grid=(m // tm, n // tn, nk), in_specs=in_specs, out_specs=[o_spec, o_spec] if also_bf16 else o_spec,
        scratch_shapes=[pltpu.VMEM((tm, tn), F32)] if nk > 1 else [],
        compiler_params=_params(("parallel", "parallel", "arbitrary")),
    )(*args)


def _ln_core(xres, f):
    p = ALPHA * xres + f
    mu = jnp.mean(p, axis=-1, keepdims=True)
    xc = p - mu
    var = jnp.mean(xc * xc, axis=-1, keepdims=True)
    rstd = lax.rsqrt(var + LN_EPS)
    return xc * rstd, rstd


def _rows8(v):
    r, c = v.shape
    return jnp.sum(v.reshape(r // SUBLANES, SUBLANES, c), axis=0)


def _ln_fwd(xres, f, g, b, name, tr=256):
    t, d = xres.shape
    row = pl.BlockSpec((tr, d), lambda i: (i, 0))
    vec = pl.BlockSpec((1, d), lambda i: (0, 0))

    def body(x_ref, f_ref, g_ref, b_ref, y_ref, yb_ref, yt_ref):
        xhat, _ = _ln_core(x_ref[...], f_ref[...])
        y = xhat * g_ref[...] + b_ref[...]
        y_ref[...] = y
        yb = y.astype(BF16)
        yb_ref[...] = yb
        yt_ref[...] = yb.T

    return pl.pallas_call(
        body, name=name, grid=(t // tr,), in_specs=[row, row, vec, vec],
        out_specs=[row, row, pl.BlockSpec((d, tr), lambda i: (0, i))],
        out_shape=[jax.ShapeDtypeStruct((t, d), F32), jax.ShapeDtypeStruct((t, d), BF16),
                   jax.ShapeDtypeStruct((d, t), BF16)],
        compiler_params=_params(("parallel",)),
    )(xres, f, g, b)


def _ln_bwd(xres, f, g, b, dy_or_target, loss_head, name, tr=256):
    t, d = xres.shape
    row = pl.BlockSpec((tr, d), lambda i: (i, 0))
    vec = pl.BlockSpec((1, d), lambda i: (0, 0))
    acc = pl.BlockSpec((SUBLANES, d), lambda i: (0, 0))
    lacc = pl.BlockSpec((SUBLANES, LANES), lambda i: (0, 0))

    def body(x_ref, f_ref, g_ref, b_ref, t_ref, dp_ref, dpb_ref, dg_ref, db_ref, *rest):
        i = pl.program_id(0)
        xhat, rstd = _ln_core(x_ref[...], f_ref[...])
        if loss_head:
            err = xhat * g_ref[...] + b_ref[...] - t_ref[...]
            dy = err * (1.0 / d)
            sq = err * err
            lanes = sq[:, :LANES]
            for kk in range(1, d // LANES):
                lanes = lanes + sq[:, kk * LANES:(kk + 1) * LANES]
            lpart = _rows8(lanes) * (0.5 / d)
        else:
            dy = t_ref[...]
        dxh = dy * g_ref[...]
        m1 = jnp.mean(dxh, axis=-1, keepdims=True)
        m2 = jnp.mean(dxh * xhat, axis=-1, keepdims=True)
        dp = rstd * (dxh - m1 - xhat * m2)
        dp_ref[...] = dp
        dpb_ref[...] = dp.astype(BF16)
        dgp = _rows8(dy * xhat)
        dbp = _rows8(dy)

        @pl.when(i == 0)
        def _():
            dg_ref[...] = dgp
            db_ref[...] = dbp
            if loss_head:
                rest[0][...] = lpart

        @pl.when(i > 0)
        def _():
            dg_ref[...] += dgp
            db_ref[...] += dbp
            if loss_head:
                rest[0][...] += lpart

    out_shape = [jax.ShapeDtypeStruct((t, d), F32), jax.ShapeDtypeStruct((t, d), BF16),
                 jax.ShapeDtypeStruct((SUBLANES, d), F32), jax.ShapeDtypeStruct((SUBLANES, d), F32)]
    out_specs = [row, row, acc, acc]
    if loss_head:
        out_shape.append(jax.ShapeDtypeStruct((SUBLANES, LANES), F32))
        out_specs.append(lacc)
    return pl.pallas_call(
        body, name=name, grid=(t // tr,), in_specs=[row, row, vec, vec, row], out_specs=out_specs,
        out_shape=out_shape, compiler_params=_params(("arbitrary",)),
    )(xres, f, g, b, dy_or_target)


def _gla_gates(glr, w2, gb):
    z = _dot(_bf(glr), w2, NN) + gb
    lg = (jnp.minimum(z, 0.0) - jnp.log(1.0 + jnp.exp(-jnp.abs(z)))) * (1.0 / GLA_TAU)
    c = z.shape[0]
    ri = lax.broadcasted_iota(jnp.int32, (c, c), 0)
    ci = lax.broadcasted_iota(jnp.int32, (c, c), 1)
    tri = (ci <= ri).astype(F32)
    bcum = _dot(tri, lg, NN, precision=lax.Precision.HIGHEST)
    blast = jnp.sum(lg, axis=0, keepdims=True)
    return z, bcum, blast, tri


def _gla_specs(t):
    c = GLA_CHUNK
    return c, t // c


def _gla_fwd(h_a, w2p, gate_b, norm_g):
    t = h_a.shape[0]
    c, n = _gla_specs(t)
    hk, hv = GLA_HEADS * GLA_DK, GLA_HEADS * GLA_DV
    scale = GLA_DK ** -0.5

    def body(q_ref, k_ref, v_ref, r_ref, glr_ref, w2_ref, gb_ref, ng_ref, og_ref, oraw_ref, sb_ref, st_ref):
        i = pl.program_id(0)

        @pl.when(i == 0)
        def _():
            st_ref[...] = jnp.zeros_like(st_ref)

        _, bcum, blast, _ = _gla_gates(glr_ref[...], w2_ref[...], gb_ref[...])
        ri = lax.broadcasted_iota(jnp.int32, (c, c), 0)
        ci = lax.broadcasted_iota(jnp.int32, (c, c), 1)
        causal = ci <= ri
        for h in range(GLA_HEADS):
            ks = slice(h * GLA_DK, (h + 1) * GLA_DK)
            vs = slice(h * GLA_DV, (h + 1) * GLA_DV)
            b_h, bl_h = bcum[:, ks], blast[:, ks]
            q_h, k_h = q_ref[:, ks], k_ref[:, ks]
            v_h = _bf(v_ref[:, vs])
            qi = _bf(q_h * scale * jnp.exp(b_h))
            ki = _bf(k_h * jnp.exp(-b_h))
            ke = _bf(k_h * jnp.exp(bl_h - b_h))
            st = st_ref[h]
            sb_ref[0, h] = st
            a = jnp.where(causal, _dot(qi, ki, NT), 0.0)
            o = _dot(_bf(a), v_h, NN) + _dot(qi, _bf(st), NT)
            st_ref[h] = st * jnp.exp(bl_h) + _dot(v_h, ke, TN)
            oraw_ref[:, vs] = o
            mu = jnp.mean(o, axis=-1, keepdims=True)
            oc = o - mu
            var = jnp.mean(oc * oc, axis=-1, keepdims=True)
            xh = oc * lax.rsqrt(var + LN_EPS)
            r_h = r_ref[:, vs]
            og_ref[:, vs] = (xh * ng_ref[:, vs] * (r_h * _sigmoid(r_h))).astype(BF16)

    return pl.pallas_call(
        body, name="gla_fwd", grid=(n,),
        in_specs=[pl.BlockSpec((c, hk), lambda i: (i, 0)), pl.BlockSpec((c, hk), lambda i: (i, 1)),
                  pl.BlockSpec((c, hv), lambda i: (i, 1)), pl.BlockSpec((c, hv), lambda i: (i, 2)),
                  pl.BlockSpec((c, LANES), lambda i: (i, GLA_W // LANES)),
                  pl.BlockSpec((LANES, hk), lambda i: (0, 0)), pl.BlockSpec((1, hk), lambda i: (0, 0)),
                  pl.BlockSpec((1, hv), lambda i: (0, 0))],
        out_specs=[pl.BlockSpec((c, hv), lambda i: (i, 0)), pl.BlockSpec((c, hv), lambda i: (i, 0)),
                   pl.BlockSpec((1, GLA_HEADS, GLA_DV, GLA_DK), lambda i: (i, 0, 0, 0))],
        out_shape=[jax.ShapeDtypeStruct((t, hv), BF16), jax.ShapeDtypeStruct((t, hv), F32),
                   jax.ShapeDtypeStruct((n, GLA_HEADS, GLA_DV, GLA_DK), F32)],
        scratch_shapes=[pltpu.VMEM((GLA_HEADS, GLA_DV, GLA_DK), F32)],
        compiler_params=_params(("arbitrary",)),
    )(h_a, h_a, h_a, h_a, h_a, w2p, gate_b, norm_g)


def _gla_bwd(h_a, w2p, gate_b, norm_g, o_raw, s_before, dmix):
    t = h_a.shape[0]
    c, n = _gla_specs(t)
    hk, hv = GLA_HEADS * GLA_DK, GLA_HEADS * GLA_DV
    scale = GLA_DK ** -0.5
    rev = lambda i: n - 1 - i

    def body(q_ref, k_ref, v_ref, r_ref, glr_ref, w2_ref, gb_ref, ng_ref, oraw_ref, sb_ref, do_ref,
             dh_ref, dw2_ref, dgb_ref, dng_ref, dst_ref):
        i = pl.program_id(0)

        @pl.when(i == 0)
        def _():
            dst_ref[...] = jnp.zeros_like(dst_ref)

        glr = glr_ref[...]
        z, bcum, blast, tri = _gla_gates(glr, w2_ref[...], gb_ref[...])
        ri = lax.broadcasted_iota(jnp.int32, (c, c), 0)
        ci = lax.broadcasted_iota(jnp.int32, (c, c), 1)
        causal = ci <= ri
        dlg_parts = []
        dng_parts = []
        for h in range(GLA_HEADS):
            ks = slice(h * GLA_DK, (h + 1) * GLA_DK)
            vs = slice(h * GLA_DV, (h + 1) * GLA_DV)
            o = oraw_ref[:, vs]
            mu = jnp.mean(o, axis=-1, keepdims=True)
            oc = o - mu
            var = jnp.mean(oc * oc, axis=-1, keepdims=True)
            rstd = lax.rsqrt(var + LN_EPS)
            xh = oc * rstd
            r_h = r_ref[:, vs]
            sg = _sigmoid(r_h)
            silu = r_h * sg
            dout = do_ref[:, vs]
            ng = ng_ref[:, vs]
            dng_parts.append(_rows8(dout * xh * silu))
            dr = dout * xh * ng * (sg * (1.0 + r_h * (1.0 - sg)))
            dxh = dout * ng * silu
            m1 = jnp.mean(dxh, axis=-1, keepdims=True)
            m2 = jnp.mean(dxh * xh, axis=-1, keepdims=True)
            do_raw = _bf(rstd * (dxh - m1 - xh * m2))
            b_h, bl_h = bcum[:, ks], blast[:, ks]
            q_h, k_h = q_ref[:, ks], k_ref[:, ks]
            v_h = _bf(v_ref[:, vs])
            eb, enb, eend = jnp.exp(b_h), jnp.exp(-b_h), jnp.exp(bl_h - b_h)
            decay = jnp.exp(bl_h)
            qi_f, ki_f, ke_f = q_h * scale * eb, k_h * enb, k_h * eend
            qi, ki, ke = _bf(qi_f), _bf(ki_f), _bf(ke_f)
            st = sb_ref[0, h]
            dst = dst_ref[h]
            dst_b = _bf(dst)
            a = _bf(jnp.where(causal, _dot(qi, ki, NT), 0.0))
            da = _bf(jnp.where(causal, _dot(do_raw, v_h, NT), 0.0))
            dv = _dot(a, do_raw, TN) + _dot(ke, dst_b, NT)
            dqi = _dot(da, ki, NN) + _dot(do_raw, _bf(st), NN)
            dki = _dot(da, qi, TN)
            dke = _dot(v_h, dst_b, NN)
            dst_ref[h] = _dot(do_raw, qi, TN) + dst * decay
            dbl = decay * jnp.sum(st * dst, axis=0, keepdims=True) + jnp.sum(dke * ke_f, axis=0, keepdims=True)
            dbc = dqi * qi_f - dki * ki_f - dke * ke_f
            dlg_parts.append(_dot(tri, dbc, TN, precision=lax.Precision.HIGHEST) + dbl)
            dh_ref[:, ks] = (dqi * eb * scale).astype(BF16)
            dh_ref[:, hk + h * GLA_DK: hk + (h + 1) * GLA_DK] = (dki * enb + dke * eend).astype(BF16)
            dh_ref[:, 2 * hk + h * GLA_DV: 2 * hk + (h + 1) * GLA_DV] = dv.astype(BF16)
            dh_ref[:, 2 * hk + hv + h * GLA_DV: 2 * hk + hv + (h + 1) * GLA_DV] = dr.astype(BF16)
        dlg = jnp.concatenate(dlg_parts, axis=1)
        dz = dlg * (1.0 / GLA_TAU) * _sigmoid(-z)
        dz_b = _bf(dz)
        dh_ref[:, GLA_W:] = _dot(dz_b, w2_ref[...], NT).astype(BF16)
        dw2p = _dot(_bf(glr), dz_b, TN)
        dgbp = _rows8(dz)
        dngp = jnp.concatenate(dng_parts, axis=1)

        @pl.when(i == 0)
        def _():
            dw2_ref[...] = dw2p
            dgb_ref[...] = dgbp
            dng_ref[...] = dngp

        @pl.when(i > 0)
        def _():
            dw2_ref[...] += dw2p
            dgb_ref[...] += dgbp
            dng_ref[...] += dngp

    return pl.pallas_call(
        body, name="gla_bwd", grid=(n,),
        in_specs=[pl.BlockSpec((c, hk), lambda i: (rev(i), 0)), pl.BlockSpec((c, hk), lambda i: (rev(i), 1)),
                  pl.BlockSpec((c, hv), lambda i: (rev(i), 1)), pl.BlockSpec((c, hv), lambda i: (rev(i), 2)),
                  pl.BlockSpec((c, LANES), lambda i: (rev(i), GLA_W // LANES)),
                  pl.BlockSpec((LANES, hk), lambda i: (0, 0)), pl.BlockSpec((1, hk), lambda i: (0, 0)),
                  pl.BlockSpec((1, hv), lambda i: (0, 0)),
                  pl.BlockSpec((c, hv), lambda i: (rev(i), 0)),
                  pl.BlockSpec((1, GLA_HEADS, GLA_DV, GLA_DK), lambda i: (rev(i), 0, 0, 0)),
                  pl.BlockSpec((c, hv), lambda i: (rev(i), 0))],
        out_specs=[pl.BlockSpec((c, HA_W), lambda i: (rev(i), 0)),
                   pl.BlockSpec((LANES, hk), lambda i: (0, 0)),
                   pl.BlockSpec((SUBLANES, hk), lambda i: (0, 0)),
                   pl.BlockSpec((SUBLANES, hv), lambda i: (0, 0))],
        out_shape=[jax.ShapeDtypeStruct((t, HA_W), BF16), jax.ShapeDtypeStruct((LANES, hk), F32),
                   jax.ShapeDtypeStruct((SUBLANES, hk), F32), jax.ShapeDtypeStruct((SUBLANES, hv), F32)],
        scratch_shapes=[pltpu.VMEM((GLA_HEADS, GLA_DV, GLA_DK), F32)],
        compiler_params=_params(("arbitrary",)),
    )(h_a, h_a, h_a, h_a, h_a, w2p, gate_b, norm_g, o_raw, s_before, dmix)


def _rope_tables(positions):
    half = ROPE_DIMS // 2
    inv_freq = ROPE_THETA ** (-jnp.arange(0, ROPE_DIMS, 2, dtype=F32) / ROPE_DIMS)
    ang = positions.astype(F32).reshape(-1, 1) * inv_freq
    cos, sin = jnp.cos(ang), jnp.sin(ang)
    t = cos.shape[0]
    one = jnp.ones((t, DIL_HD - ROPE_DIMS), F32)
    zero = jnp.zeros((t, DIL_HD - ROPE_DIMS), F32)
    zh = jnp.zeros((t, half), F32)
    return (jnp.concatenate([cos, cos, one], axis=1), jnp.concatenate([-sin, zh, zero], axis=1),
            jnp.concatenate([zh, sin, zero], axis=1))


def _rope_apply(x, c, s1, s2):
    half = ROPE_DIMS // 2
    return x * c + pltpu.roll(x, DIL_HD - half, 1) * s1 + pltpu.roll(x, half, 1) * s2


def _rope_apply_t(dy, c, s1, s2):
    half = ROPE_DIMS // 2
    return dy * c + pltpu.roll(dy * s1, half, 1) + pltpu.roll(dy * s2, DIL_HD - half, 1)


def _rope_fwd(h_b, tabs, tr=256):
    t = h_b.shape[0]
    w = DIL_HEADS * DIL_HD
    scale = DIL_HD ** -0.5
    tab = pl.BlockSpec((tr, DIL_HD), lambda i: (i, 0))
    outb = pl.BlockSpec((tr, w), lambda i: (i, 0))

    def body(q_ref, k_ref, c_ref, s1_ref, s2_ref, qo_ref, ko_ref):
        c, s1, s2 = c_ref[...], s1_ref[...], s2_ref[...]
        for h in range(DIL_HEADS):
            hs = slice(h * DIL_HD, (h + 1) * DIL_HD)
            qo_ref[:, hs] = _rope_apply(q_ref[:, hs] * scale, c, s1, s2)
            ko_ref[:, hs] = _rope_apply(k_ref[:, hs], c, s1, s2)

    return pl.pallas_call(
        body, name="rope_fwd", grid=(t // tr,),
        in_specs=[pl.BlockSpec((tr, w), lambda i: (i, 0)), pl.BlockSpec((tr, w), lambda i: (i, 1)), tab, tab, tab],
        out_specs=[outb, outb],
        out_shape=[jax.ShapeDtypeStruct((t, w), F32)] * 2,
        compiler_params=_params(("parallel",)),
    )(h_b, h_b, *tabs)


def _dil_dh(dq, dk, dv, tabs, tr=256):
    t, w = dq.shape
    scale = DIL_HD ** -0.5
    tab = pl.BlockSpec((tr, DIL_HD), lambda i: (i, 0))
    inb = pl.BlockSpec((tr, w), lambda i: (i, 0))

    def body(dq_ref, dk_ref, dv_ref, c_ref, s1_ref, s2_ref, o_ref):
        c, s1, s2 = c_ref[...], s1_ref[...], s2_ref[...]
        for h in range(DIL_HEADS):
            hs = slice(h * DIL_HD, (h + 1) * DIL_HD)
            o_ref[:, h * DIL_HD:(h + 1) * DIL_HD] = (_rope_apply_t(dq_ref[:, hs], c, s1, s2) * scale).astype(BF16)
            o_ref[:, w + h * DIL_HD: w + (h + 1) * DIL_HD] = _rope_apply_t(dk_ref[:, hs], c, s1, s2).astype(BF16)
        o_ref[:, 2 * w:] = dv_ref[...].astype(BF16)

    return pl.pallas_call(
        body, name="dil_dh", grid=(t // tr,), in_specs=[inb] * 3 + [tab] * 3,
        out_specs=pl.BlockSpec((tr, 3 * w), lambda i: (i, 0)),
        out_shape=jax.ShapeDtypeStruct((t, 3 * w), BF16), compiler_params=_params(("parallel",)),
    )(dq, dk, dv, *tabs)


BANDS = 8


def _to_branch(a, d):
    t, w = a.shape
    return a.reshape(t // d, d, w // DIL_HD, DIL_HD).transpose(1, 2, 0, 3).reshape(-1, DIL_HD)


def _from_branch(a, d, t):
    hds = a.shape[0] // t
    return a.reshape(d, hds, t // d, DIL_HD).transpose(2, 0, 1, 3).reshape(t, hds * DIL_HD)


def _band_masks(not_first):
    r = lax.broadcasted_iota(jnp.int32, (DIL_BAND, 2 * DIL_BAND), 0)
    c = lax.broadcasted_iota(jnp.int32, (DIL_BAND, 2 * DIL_BAND), 1)
    nf = jnp.full((DIL_BAND, 2 * DIL_BAND), not_first, jnp.int32)
    look_back = jnp.logical_and(jnp.logical_and(c < DIL_BAND, c >= r), nf > 0)
    own_band = jnp.logical_and(c >= DIL_BAND, (c - DIL_BAND) <= r)
    return jnp.logical_or(look_back, own_band)


def _dil_fwd(q, k, v, nb, name):
    rows = q.shape[0]
    blk = BANDS * DIL_BAND
    steps = rows // blk
    main = pl.BlockSpec((blk, DIL_HD), lambda i: (i, 0))
    prev = pl.BlockSpec((DIL_BAND, DIL_HD), lambda i: (jnp.maximum(i * BANDS - 1, 0), 0))

    def body(q_ref, k_ref, v_ref, kp_ref, vp_ref, o_ref, l_ref):
        i = pl.program_id(0)
        for j in range(BANDS):
            lo, hi = j * DIL_BAND, (j + 1) * DIL_BAND
            if j == 0:
                kcat = jnp.concatenate([kp_ref[...], k_ref[lo:hi, :]], axis=0)
                vcat = jnp.concatenate([vp_ref[...], v_ref[lo:hi, :]], axis=0)
            else:
                kcat = k_ref[lo - DIL_BAND:hi, :]
                vcat = v_ref[lo - DIL_BAND:hi, :]
            not_first = (((i * BANDS + j) % nb) != 0).astype(jnp.int32)
            s = jnp.where(_band_masks(not_first), _dot(q_ref[lo:hi, :], kcat, NT), NEG)
            m = jnp.max(s, axis=-1, keepdims=True)
            p = jnp.exp(s - m)
            den = jnp.sum(p, axis=-1, keepdims=True)
            o_ref[lo:hi, :] = _dot(_bf(p), vcat, NN) / den
            l_ref[lo:hi, :] = jnp.broadcast_to(m + jnp.log(den), (DIL_BAND, DIL_HD))

    return pl.pallas_call(
        body, name=name, grid=(steps,), in_specs=[main, main, main, prev, prev], out_specs=[main, main],
        out_shape=[jax.ShapeDtypeStruct((rows, DIL_HD), F32)] * 2, compiler_params=_params(("parallel",)),
    )(q, k, v, k, v)


def _dil_bwd(q, k, v, do, lse, dd, nb, name):
    rows = q.shape[0]
    blk = BANDS * DIL_BAND
    steps = rows // blk
    last_band = rows // DIL_BAND - 1
    main = pl.BlockSpec((blk, DIL_HD), lambda i: (i, 0))
    prev = pl.BlockSpec((DIL_BAND, DIL_HD), lambda i: (jnp.maximum(i * BANDS - 1, 0), 0))
    nxt = pl.BlockSpec((DIL_BAND, DIL_HD), lambda i: (jnp.minimum(i * BANDS + BANDS, last_band), 0))

    def body(q_ref, k_ref, v_ref, do_ref, l_ref, dd_ref, kp_ref, vp_ref, qn_ref, don_ref, ln_ref, ddn_ref,
             dq_ref, dk_ref, dv_ref, ak_ref, av_ref):
        i = pl.program_id(0)
        ak_ref[...] = jnp.zeros_like(ak_ref)
        av_ref[...] = jnp.zeros_like(av_ref)
        for j in range(BANDS + 1):
            lo, hi = j * DIL_BAND, (j + 1) * DIL_BAND
            if j == 0:
                kcat = jnp.concatenate([kp_ref[...], k_ref[lo:hi, :]], axis=0)
                vcat = jnp.concatenate([vp_ref[...], v_ref[lo:hi, :]], axis=0)
            elif j < BANDS:
                kcat = k_ref[lo - DIL_BAND:hi, :]
                vcat = v_ref[lo - DIL_BAND:hi, :]
            else:
                kcat = jnp.concatenate([k_ref[lo - DIL_BAND:lo, :], k_ref[lo - DIL_BAND:lo, :]], axis=0)
                vcat = jnp.concatenate([v_ref[lo - DIL_BAND:lo, :], v_ref[lo - DIL_BAND:lo, :]], axis=0)
            if j < BANDS:
                qj, doj, lj, ddj = q_ref[lo:hi, :], do_ref[lo:hi, :], l_ref[lo:hi, :], dd_ref[lo:hi, :]
            else:
                qj, doj, lj, ddj = qn_ref[...], don_ref[...], ln_ref[...], ddn_ref[...]
            not_first = (((i * BANDS + j) % nb) != 0).astype(jnp.int32)
            mask = _band_masks(not_first)
            if j == BANDS:
                cidx = lax.broadcasted_iota(jnp.int32, mask.shape, 1)
                mask = jnp.logical_and(mask, cidx < DIL_BAND)
            s = jnp.where(mask, _dot(qj, kcat, NT), NEG)
            p = jnp.exp(s - jnp.concatenate([lj, lj], axis=1))
            dp = _dot(doj, vcat, NT)
            ds = _bf(p * (dp - jnp.concatenate([ddj, ddj], axis=1)))
            if j < BANDS:
                dq_ref[lo:hi, :] = _dot(ds, kcat, NN)
            ak_ref[lo:hi + DIL_BAND, :] += _dot(ds, qj, TN)
            av_ref[lo:hi + DIL_BAND, :] += _dot(_bf(p), doj, TN)
        dk_ref[...] = ak_ref[DIL_BAND:DIL_BAND + blk, :]
        dv_ref[...] = av_ref[DIL_BAND:DIL_BAND + blk, :]

    return pl.pallas_call(
        body, name=name, grid=(steps,),
        in_specs=[main] * 6 + [prev, prev] + [nxt] * 4, out_specs=[main] * 3,
        out_shape=[jax.ShapeDtypeStruct((rows, DIL_HD), F32)] * 3,
        scratch_shapes=[pltpu.VMEM((blk + 2 * DIL_BAND, DIL_HD), F32)] * 2,
        compiler_params=_params(("parallel",)),
    )(q, k, v, do, lse, dd, k, v, q, do, lse, dd)


def _dil_merge(os_, ls_, tr=256):
    t, w = os_[0].shape
    blk = pl.BlockSpec((tr, w), lambda i: (i, 0))

    def body(o1, o2, o3, l1, l2, l3, ob_ref, of_ref, lt_ref):
        a, b, c = l1[...], l2[...], l3[...]
        m = jnp.maximum(jnp.maximum(a, b), c)
        ea, eb, ec = jnp.exp(a - m), jnp.exp(b - m), jnp.exp(c - m)
        den = ea + eb + ec
        o = (ea * o1[...] + eb * o2[...] + ec * o3[...]) / den
        ob_ref[...] = o.astype(BF16)
        of_ref[...] = o
        lt_ref[...] = m + jnp.log(den)

    return pl.pallas_call(
        body, name="dil_merge", grid=(t // tr,), in_specs=[blk] * 6, out_specs=[blk] * 3,
        out_shape=[jax.ShapeDtypeStruct((t, w), BF16), jax.ShapeDtypeStruct((t, w), F32),
                   jax.ShapeDtypeStruct((t, w), F32)],
        compiler_params=_params(("parallel",)),
    )(*os_, *ls_)


def _dil_bwd_prep(dmix, o_d, tr=256):
    t, w = o_d.shape
    blk = pl.BlockSpec((tr, w), lambda i: (i, 0))

    def body(do_ref, o_ref, dob_ref, dd_ref):
        do = do_ref[...]
        prod = do * o_ref[...]
        dob_ref[...] = do.astype(BF16)
        for h in range(DIL_HEADS):
            hs = slice(h * DIL_HD, (h + 1) * DIL_HD)
            dd_ref[:, hs] = jnp.broadcast_to(jnp.sum(prod[:, hs], axis=-1, keepdims=True), (tr, DIL_HD))

    return pl.pallas_call(
        body, name="dil_bwd_prep", grid=(t // tr,),
        in_specs=[pl.BlockSpec((tr, w), lambda i: (i, 1)), blk], out_specs=[blk, blk],
        out_shape=[jax.ShapeDtypeStruct((t, w), BF16), jax.ShapeDtypeStruct((t, w), F32)],
        compiler_params=_params(("parallel",)),
    )(dmix, o_d)


def _gather_rows(dst_ref, src_ref, t, d, cast=None):
    n = t // d
    for r in range(d):
        v = src_ref[pl.ds(r, n, stride=d), :] if d > 1 else src_ref[...]
        dst_ref[r * n:(r + 1) * n, :] = v if cast is None else v.astype(cast)


def _tri_mask():
    r = lax.broadcasted_iota(jnp.int32, (DIL_BAND, DIL_BAND), 0)
    c = lax.broadcasted_iota(jnp.int32, (DIL_BAND, DIL_BAND), 1)
    return c <= r


def _dil_fwd_all(qr, kr, h_b):
    t = qr.shape[0]
    nbands = t // DIL_BAND
    nbr = len(DIL_DILATIONS)
    hoff = DIL_HEADS

    def col(off):
        return pl.BlockSpec((t, DIL_HD), lambda h: (0, off + h), pipeline_mode=pl.Buffered(1))

    outb = pl.BlockSpec((t, DIL_HD), lambda h: (0, h))

    def body(q_ref, k_ref, v_ref, ob_ref, of_ref, lt_ref, qs, ks, vs, os_, ls_, *br):
        obr, lbr = br[:nbr], br[nbr:]
        for bi, d in enumerate(DIL_DILATIONS):
            n = t // d
            nb = n // DIL_BAND
            _gather_rows(qs, q_ref, t, d, BF16)
            _gather_rows(ks, k_ref, t, d, BF16)
            _gather_rows(vs, v_ref, t, d, BF16)
            s = jnp.where(_tri_mask(), _dot(qs[0:DIL_BAND, :], ks[0:DIL_BAND, :], NT), NEG)
            m = jnp.max(s, axis=-1, keepdims=True)
            pr = jnp.exp(s - m)
            den = jnp.sum(pr, axis=-1, keepdims=True)
            os_[0:DIL_BAND, :] = _dot(_bf(pr), vs[0:DIL_BAND, :], NN) / den
            ls_[0:DIL_BAND, :] = jnp.broadcast_to(m + jnp.log(den), (DIL_BAND, DIL_HD))

            def band(b, carry, nb=nb):
                st = pl.multiple_of((b - 1) * DIL_BAND, DIL_BAND)
                cur = pl.ds(st + DIL_BAND, DIL_BAND)
                both = pl.ds(st, 2 * DIL_BAND)
                not_first = ((b % nb) != 0).astype(jnp.int32)
                s = jnp.where(_band_masks(not_first), _dot(qs[cur, :], ks[both, :], NT), NEG)
                m = jnp.max(s, axis=-1, keepdims=True)
                pr = jnp.exp(s - m)
                den = jnp.sum(pr, axis=-1, keepdims=True)
                os_[cur, :] = _dot(_bf(pr), vs[both, :], NN) / den
                ls_[cur, :] = jnp.broadcast_to(m + jnp.log(den), (DIL_BAND, DIL_HD))
                return carry

            lax.fori_loop(1, nbands, band, 0, unroll=4)
            for r in range(d):
                dst = pl.ds(r, n, stride=d) if d > 1 else slice(None)
                obr[bi][dst, :] = os_[r * n:(r + 1) * n, :]
                lbr[bi][dst, :] = ls_[r * n:(r + 1) * n, :]
        rows = 512
        for c0 in range(0, t, rows):
            sl = slice(c0, c0 + rows)
            la, lb, lc = lbr[0][sl, :], lbr[1][sl, :], lbr[2][sl, :]
            m = jnp.maximum(jnp.maximum(la, lb), lc)
            ea, eb, ec = jnp.exp(la - m), jnp.exp(lb - m), jnp.exp(lc - m)
            den = ea + eb + ec
            o = (ea * obr[0][sl, :] + eb * obr[1][sl, :] + ec * obr[2][sl, :]) / den
            ob_ref[sl, :] = o.astype(BF16)
            of_ref[sl, :] = o
            lt_ref[sl, :] = m + jnp.log(den)

    w = DIL_HEADS * DIL_HD
    vm = lambda dt: pltpu.VMEM((t, DIL_HD), dt)
    return pl.pallas_call(
        body, name="dil_fwd", grid=(DIL_HEADS,), in_specs=[col(0), col(0), col(2 * hoff)],
        out_specs=[outb, outb, outb],
        out_shape=[jax.ShapeDtypeStruct((t, w), BF16), jax.ShapeDtypeStruct((t, w), F32),
                   jax.ShapeDtypeStruct((t, w), F32)],
        scratch_shapes=[vm(BF16)] * 3 + [vm(F32)] * 2 + [vm(F32)] * (2 * nbr),
        compiler_params=_params(("parallel",)),
    )(qr, kr, h_b)


def _dil_bwd_all(qr, kr, h_b, dmix, o_d, lse_tot):
    t = qr.shape[0]
    nbands = t // DIL_BAND
    hoff = DIL_HEADS

    def col(off):
        return pl.BlockSpec((t, DIL_HD), lambda h: (0, off + h), pipeline_mode=pl.Buffered(1))

    outb = pl.BlockSpec((t, DIL_HD), lambda h: (0, h))

    def body(q_ref, k_ref, v_ref, do_ref, o_ref, l_ref, dq_ref, dk_ref, dv_ref,
             qs, ks, vs, dos, lss, dds, dqs, acck, accv):
        for bi, d in enumerate(DIL_DILATIONS):
            n = t // d
            nb = n // DIL_BAND
            _gather_rows(qs, q_ref, t, d, BF16)
            _gather_rows(ks, k_ref, t, d, BF16)
            _gather_rows(vs, v_ref, t, d, BF16)
            _gather_rows(dos, do_ref, t, d, BF16)
            _gather_rows(lss, l_ref, t, d)
            for r in range(d):
                src = pl.ds(r, n, stride=d) if d > 1 else slice(None)
                prod = do_ref[src, :] * o_ref[src, :]
                dds[r * n:(r + 1) * n, :] = jnp.broadcast_to(jnp.sum(prod, axis=-1, keepdims=True), (n, DIL_HD))
            acck[...] = jnp.zeros_like(acck)
            accv[...] = jnp.zeros_like(accv)
            b0 = slice(0, DIL_BAND)
            s = jnp.where(_tri_mask(), _dot(qs[b0, :], ks[b0, :], NT), NEG)
            pr = jnp.exp(s - lss[b0, :])
            ds = _bf(pr * (_dot(dos[b0, :], vs[b0, :], NT) - dds[b0, :]))
            dqs[b0, :] = _dot(ds, ks[b0, :], NN)
            acck[DIL_BAND:2 * DIL_BAND, :] += _dot(ds, qs[b0, :], TN)
            accv[DIL_BAND:2 * DIL_BAND, :] += _dot(_bf(pr), dos[b0, :], TN)

            def band(b, carry, nb=nb):
                st = pl.multiple_of((b - 1) * DIL_BAND, DIL_BAND)
                cur = pl.ds(st + DIL_BAND, DIL_BAND)
                both = pl.ds(st, 2 * DIL_BAND)
                acc_rows = pl.ds(st + DIL_BAND, 2 * DIL_BAND)
                not_first = ((b % nb) != 0).astype(jnp.int32)
                qb, dob, lb, ddb = qs[cur, :], dos[cur, :], lss[cur, :], dds[cur, :]
                kcat, vcat = ks[both, :], vs[both, :]
                s = jnp.where(_band_masks(not_first), _dot(qb, kcat, NT), NEG)
                pr = jnp.exp(s - jnp.concatenate([lb, lb], axis=1))
                ds = _bf(pr * (_dot(dob, vcat, NT) - jnp.concatenate([ddb, ddb], axis=1)))
                dqs[cur, :] = _dot(ds, kcat, NN)
                acck[acc_rows, :] += _dot(ds, qb, TN)
                accv[acc_rows, :] += _dot(_bf(pr), dob, TN)
                return carry

            lax.fori_loop(1, nbands, band, 0, unroll=2)
            for r in range(d):
                lo = r * n
                if d == 1:
                    dq_ref[...] = dqs[...]
                    dk_ref[...] = acck[DIL_BAND:DIL_BAND + t, :]
                    dv_ref[...] = accv[DIL_BAND:DIL_BAND + t, :]
                else:
                    dst = pl.ds(r, n, stride=d)
                    dq_ref[dst, :] = dq_ref[dst, :] + dqs[lo:lo + n, :]
                    dk_ref[dst, :] = dk_ref[dst, :] + acck[DIL_BAND + lo:DIL_BAND + lo + n, :]
                    dv_ref[dst, :] = dv_ref[dst, :] + accv[DIL_BAND + lo:DIL_BAND + lo + n, :]

    w = DIL_HEADS * DIL_HD
    vm = lambda dt, extra=0: pltpu.VMEM((t + extra, DIL_HD), dt)
    return pl.pallas_call(
        body, name="dil_bwd", grid=(DIL_HEADS,),
        in_specs=[col(0), col(0), col(2 * hoff), col(hoff), col(0), col(0)], out_specs=[outb] * 3,
        out_shape=[jax.ShapeDtypeStruct((t, w), F32)] * 3,
        scratch_shapes=[vm(BF16)] * 4 + [vm(F32)] * 3 + [vm(F32, DIL_BAND)] * 2,
        compiler_params=_params(("parallel",)),
    )(qr, kr, h_b, dmix, o_d, lse_tot)


def _ca_fwd(q, memkv, tq=512):
    t, d = q.shape
    m = memkv.shape[0]
    scale = CA_HD ** -0.5

    def body(q_ref, k_ref, v_ref, o_ref, ot_ref):
        for h in range(CA_HEADS):
            hs = slice(h * CA_HD, (h + 1) * CA_HD)
            s = _dot(q_ref[:, hs], k_ref[:, hs], NT) * scale
            p = jnp.exp(s - jnp.max(s, axis=-1, keepdims=True))
            p = p / jnp.sum(p, axis=-1, keepdims=True)
            o = _dot(_bf(p), v_ref[:, hs], NN).astype(BF16)
            o_ref[:, hs] = o
            ot_ref[hs, :] = o.T

    return pl.pallas_call(
        body, name="ca_fwd", grid=(t // tq,),
        in_specs=[pl.BlockSpec((tq, d), lambda i: (i, 0)), pl.BlockSpec((m, d), lambda i: (0, 0)),
                  pl.BlockSpec((m, d), lambda i: (0, 1))],
        out_specs=[pl.BlockSpec((tq, d), lambda i: (i, 0)), pl.BlockSpec((d, tq), lambda i: (0, i))],
        out_shape=[jax.ShapeDtypeStruct((t, d), BF16), jax.ShapeDtypeStruct((d, t), BF16)],
        compiler_params=_params(("parallel",)),
    )(q, memkv, memkv)


def _ca_bwd(q, memkv, do, tq=512):
    t, d = q.shape
    m = memkv.shape[0]
    scale = CA_HD ** -0.5

    def body(q_ref, k_ref, v_ref, do_ref, dq_ref, dkv_ref):
        i = pl.program_id(0)

        @pl.when(i == 0)
        def _():
            dkv_ref[...] = jnp.zeros_like(dkv_ref)

        for h in range(CA_HEADS):
            hs = slice(h * CA_HD, (h + 1) * CA_HD)
            q_h, k_h, v_h, do_h = q_ref[:, hs], k_ref[:, hs], v_ref[:, hs], do_ref[:, hs]
            s = _dot(q_h, k_h, NT) * scale
            p = jnp.exp(s - jnp.max(s, axis=-1, keepdims=True))
            p = p / jnp.sum(p, axis=-1, keepdims=True)
            dp = _dot(do_h, v_h, NT)
            ds = _bf(p * (dp - jnp.sum(p * dp, axis=-1, keepdims=True)) * scale)
            dq_ref[:, hs] = _dot(ds, k_h, NN).astype(BF16)
            dkv_ref[:, hs] += _dot(ds, q_h, TN)
            dkv_ref[:, d + h * CA_HD: d + (h + 1) * CA_HD] += _dot(_bf(p), do_h, TN)

    return pl.pallas_call(
        body, name="ca_bwd", grid=(t // tq,),
        in_specs=[pl.BlockSpec((tq, d), lambda i: (i, 0)), pl.BlockSpec((m, d), lambda i: (0, 0)),
                  pl.BlockSpec((m, d), lambda i: (0, 1)), pl.BlockSpec((tq, d), lambda i: (i, 0))],
        out_specs=[pl.BlockSpec((tq, d), lambda i: (i, 0)), pl.BlockSpec((m, 2 * d), lambda i: (0, 0))],
        out_shape=[jax.ShapeDtypeStruct((t, d), BF16), jax.ShapeDtypeStruct((m, 2 * d), F32)],
        compiler_params=_params(("arbitrary",)),
    )(q, memkv, memkv, do)


STRIP = 256


def _shift_down(u, n, row):
    return jnp.where(row >= n, pltpu.roll(u, n, 0), 0.0)


def _shift_up(u, n, row):
    t = u.shape[0]
    return jnp.where(row < t - n, pltpu.roll(u, t - n, 0), 0.0)


def _conv(u, cw_ref, row):
    return ((cw_ref[3:4, :] + cw_ref[0:1, :] * _shift_down(u, 2, row)) + cw_ref[1:2, :] * _shift_down(u, 1, row)) \
        + cw_ref[2:3, :] * u


def _swiglu_fwd(u0, cw):
    t, w = u0.shape[0], u0.shape[1] // 2
    ns = w // STRIP
    col = pl.BlockSpec((t, STRIP), lambda j: (0, j))
    col_up = pl.BlockSpec((t, STRIP), lambda j: (0, ns + j))
    cws = pl.BlockSpec((SUBLANES, STRIP), lambda j: (0, j))
    cws_up = pl.BlockSpec((SUBLANES, STRIP), lambda j: (0, ns + j))

    def body(g_ref, u_ref, cg_ref, cu_ref, a_ref, at_ref):
        row = lax.broadcasted_iota(jnp.int32, (t, STRIP), 0)
        gate = _conv(g_ref[...].astype(F32), cg_ref, row)
        up = _conv(u_ref[...].astype(F32), cu_ref, row)
        act = (gate * _sigmoid(gate) * up).astype(BF16)
        a_ref[...] = act
        at_ref[...] = act.T

    return pl.pallas_call(
        body, name="swiglu_fwd", grid=(ns,), in_specs=[col, col_up, cws, cws_up],
        out_specs=[col, pl.BlockSpec((STRIP, t), lambda j: (j, 0))],
        out_shape=[jax.ShapeDtypeStruct((t, w), BF16), jax.ShapeDtypeStruct((w, t), BF16)],
        compiler_params=_params(("parallel",)),
    )(u0, u0, cw, cw)


def _swiglu_bwd(u0, cw, da):
    t, w = u0.shape[0], u0.shape[1] // 2
    ns = w // STRIP
    col = pl.BlockSpec((t, STRIP), lambda j: (0, j))
    col_up = pl.BlockSpec((t, STRIP), lambda j: (0, ns + j))
    cws = pl.BlockSpec((SUBLANES, STRIP), lambda j: (0, j))
    cws_up = pl.BlockSpec((SUBLANES, STRIP), lambda j: (0, ns + j))

    def conv_bwd(du, u0, cw_ref, row, du0_ref, du0t_ref, dcw_ref):
        du0 = (cw_ref[2:3, :] * du + cw_ref[1:2, :] * _shift_up(du, 1, row)) + cw_ref[0:1, :] * _shift_up(du, 2, row)
        du0 = du0.astype(BF16)
        du0_ref[...] = du0
        du0t_ref[...] = du0.T
        dcw_ref[0:1, :] = jnp.sum(du * _shift_down(u0, 2, row), axis=0, keepdims=True)
        dcw_ref[1:2, :] = jnp.sum(du * _shift_down(u0, 1, row), axis=0, keepdims=True)
        dcw_ref[2:3, :] = jnp.sum(du * u0, axis=0, keepdims=True)
        dcw_ref[3:4, :] = jnp.sum(du, axis=0, keepdims=True)
        dcw_ref[4:8, :] = jnp.zeros((4, STRIP), F32)

    def body(g_ref, u_ref, cg_ref, cu_ref, da_ref, dg0_ref, du0_ref, dg0t_ref, du0t_ref, dcg_ref, dcu_ref):
        row = lax.broadcasted_iota(jnp.int32, (t, STRIP), 0)
        g0, up0 = g_ref[...].astype(F32), u_ref[...].astype(F32)
        gate = _conv(g0, cg_ref, row)
        up = _conv(up0, cu_ref, row)
        sg = _sigmoid(gate)
        da = da_ref[...].astype(F32)
        dgate = da * up * (sg * (1.0 + gate * (1.0 - sg)))
        dup = da * (gate * sg)
        conv_bwd(dgate, g0, cg_ref, row, dg0_ref, dg0t_ref, dcg_ref)
        conv_bwd(dup, up0, cu_ref, row, du0_ref, du0t_ref, dcu_ref)

    colt = pl.BlockSpec((STRIP, t), lambda j: (j, 0))
    return pl.pallas_call(
        body, name="swiglu_bwd", grid=(ns,), in_specs=[col, col_up, cws, cws_up, col],
        out_specs=[col, col, colt, colt, cws, cws],
        out_shape=[jax.ShapeDtypeStruct((t, w), BF16), jax.ShapeDtypeStruct((t, w), BF16),
                   jax.ShapeDtypeStruct((w, t), BF16), jax.ShapeDtypeStruct((w, t), BF16),
                   jax.ShapeDtypeStruct((SUBLANES, w), F32), jax.ShapeDtypeStruct((SUBLANES, w), F32)],
        compiler_params=_params(("parallel",)),
    )(u0, u0, cw, cw, da)


def _ffn_win_grad(dugt, duut, x2b, tn=512):
    t, d = x2b.shape
    tk = t // 2
    nk = t // tk
    sp, sw = FF_SLAB_P, FF_SLAB

    def body(ag_ref, au_ref, b_ref, o_ref, ob_ref, acc_ref):
        j, kk = pl.program_id(0), pl.program_id(2)

        @pl.when(kk == 0)
        def _():
            acc_ref[...] = jnp.zeros_like(acc_ref)

        @pl.when(j < 4)
        def _():
            acc_ref[...] += _dot(ag_ref[...], b_ref[...], NN)

        @pl.when(j >= 4)
        def _():
            acc_ref[...] += _dot(au_ref[...], b_ref[...], NN)

        @pl.when(kk == nk - 1)
        def _():
            o_ref[...] = acc_ref[:sw, :]
            ob_ref[...] = acc_ref[:sw, :].astype(BF16)

    o_spec = pl.BlockSpec((None, sw, tn), lambda j, n, kk: (j, 0, n))
    return pl.pallas_call(
        body, name="mm_g_ffn_in", grid=(8, d // tn, nk),
        in_specs=[pl.BlockSpec((sp, tk), lambda j, n, kk: (jnp.minimum(j, 3), kk)),
                  pl.BlockSpec((sp, tk), lambda j, n, kk: (jnp.maximum(j - 4, 0), kk)),
                  pl.BlockSpec((tk, tn), lambda j, n, kk: (kk, n))],
        out_specs=[o_spec, o_spec],
        out_shape=[jax.ShapeDtypeStruct((8, sw, d), F32), jax.ShapeDtypeStruct((8, sw, d), BF16)],
        scratch_shapes=[pltpu.VMEM((sp, tn), F32)],
        compiler_params=_params(("parallel", "parallel", "arbitrary")),
    )(dugt, duut, x2b)


def _tile2d(r, c, limit=1 << 20):
    tr, tc = r, c
    while tr * tc * 4 > limit:
        if tr % (2 * SUBLANES) == 0:
            tr //= 2
        elif tc % (2 * LANES) == 0:
            tc //= 2
        else:
            break
    return tr, tc


def _adamw_math(w, m, v, g):
    c1 = 1.0 - ADAM_B1 ** ADAM_STEP
    c2 = 1.0 - ADAM_B2 ** ADAM_STEP
    mm = ADAM_B1 * m + (1.0 - ADAM_B1) * g
    vv = ADAM_B2 * v + (1.0 - ADAM_B2) * (g * g)
    delta = -ADAM_LR * ((mm / c1) / (jnp.sqrt(vv / c2) + ADAM_EPS) + ADAM_WD * w)
    return delta, mm, vv


def _adamw(w, m, v, g, name):
    r, c = w.shape
    blk = pl.BlockSpec((r, c), lambda i: (0, 0))

    def body(w_ref, m_ref, v_ref, gi_ref, g_ref, d_ref, nm_ref, nv_ref):
        g = gi_ref[...]
        d_ref[...], nm_ref[...], nv_ref[...] = _adamw_math(w_ref[...], m_ref[...], v_ref[...], g)
        g_ref[...] = g

    return pl.pallas_call(body, name=name, grid=(1,), in_specs=[blk] * 4, out_specs=[blk] * 4,
                          out_shape=[jax.ShapeDtypeStruct((r, c), F32)] * 4,
                          compiler_params=_params(("arbitrary",)))(w, m, v, g)


def _pair_add(gs, ra, core, name):
    _, _, r, c = gs.shape
    tr, tc = _tile2d(r, c)
    blk = pl.BlockSpec((None, tr, tc), lambda k, i, j, s: (k, i, j))

    def body(s_ref, g_ref, r_ref, o_ref, ob_ref):
        p = g_ref[...] + r_ref[...]
        o_ref[...] = p
        ob_ref[...] = p.astype(BF16)

    gspec = pltpu.PrefetchScalarGridSpec(
        num_scalar_prefetch=1, grid=(4, r // tr, c // tc),
        in_specs=[pl.BlockSpec((None, None, tr, tc), lambda k, i, j, s: (k, s[0], i, j)), blk], out_specs=[blk, blk])
    return pl.pallas_call(body, name=name, grid_spec=gspec,
                          out_shape=[jax.ShapeDtypeStruct((4, r, c), F32), jax.ShapeDtypeStruct((4, r, c), BF16)],
                          compiler_params=_params(("parallel", "parallel", "parallel")))(core, gs, ra)


def _small_reduce(gathered):
    nd, r, n = gathered.shape
    tn = 2048 if n % 2048 == 0 else n
    def body(g_ref, s_ref, t_ref):
        s = g_ref[0]
        for dv in range(1, nd):
            s = s + g_ref[dv]
        s_ref[...] = s
        t_ref[...] = jnp.broadcast_to(jnp.sum(s, axis=0, keepdims=True), (r, tn))

    return pl.pallas_call(
        body, name="small_reduce", grid=(n // tn,),
        in_specs=[pl.BlockSpec((nd, r, tn), lambda j: (0, 0, j))],
        out_specs=[pl.BlockSpec((r, tn), lambda j: (0, j))] * 2,
        out_shape=[jax.ShapeDtypeStruct((r, n), F32)] * 2, compiler_params=_params(("parallel",)),
    )(gathered)


HBM = pl.BlockSpec(memory_space=pltpu.HBM)


def _all_gather(arrs, name):
    n = len(arrs)

    def body(*refs):
        ins, outs = refs[:n], refs[n:2 * n]
        send, recv, lsem = refs[2 * n:]
        x, y, c = lax.axis_index("x"), lax.axis_index("y"), lax.axis_index("c")
        me, sib = (x, y, c), (x, y, 1 - c)
        chips = [(1 - x, y), (x, 1 - y), (1 - x, 1 - y)]

        def slot(w, p):
            return outs[w].at[4 * p[0] + 2 * p[1] + p[2]]

        def cp(w, k, block, to, src=None):
            return pltpu.make_async_remote_copy(
                src_ref=slot(w, block) if src is None else src, dst_ref=slot(w, block),
                send_sem=send.at[w * 7 + k], recv_sem=recv.at[w * 7 + k], device_id=to, device_id_type=MESH)

        mine = [pltpu.make_async_copy(ins[w], slot(w, me), lsem.at[w]) for w in range(n)]
        for m in mine:
            m.start()
        first = []
        for w in range(n):
            first.append(cp(w, 0, me, sib, src=ins[w]))
            first += [cp(w, 1 + j, me, (*chip, c), src=ins[w]) for j, chip in enumerate(chips)]
        for f in first:
            f.start()
        passed = []
        for j, chip in enumerate(chips):
            for w in range(n):
                cp(w, 1 + j, (*chip, c), me).wait_recv()
                fwd = cp(w, 4 + j, (*chip, c), sib)
                fwd.start()
                passed.append(fwd)
        for w in range(n):
            cp(w, 0, sib, me).wait_recv()
            for j, chip in enumerate(chips):
                cp(w, 4 + j, (*chip, 1 - c), me).wait_recv()
        for f in first + passed:
            f.wait_send()
        for m in mine:
            m.wait()

    return pl.pallas_call(
        body, name=name, in_specs=[HBM] * n, out_specs=[HBM] * n,
        out_shape=[jax.ShapeDtypeStruct((8,) + a.shape, a.dtype) for a in arrs],
        scratch_shapes=[pltpu.SemaphoreType.DMA((7 * n,)), pltpu.SemaphoreType.DMA((7 * n,)),
                        pltpu.SemaphoreType.DMA((n,))],
    )(*arrs)


def _sibling_exchange(arrs, name):
    n = len(arrs)

    def body(*refs):
        ins, outs = refs[:n], refs[n:2 * n]
        send, recv = refs[2 * n:]
        x, y, c = lax.axis_index("x"), lax.axis_index("y"), lax.axis_index("c")
        copies = [pltpu.make_async_remote_copy(
            src_ref=ins[w].at[:, 1 - c], dst_ref=outs[w], send_sem=send.at[w], recv_sem=recv.at[w],
            device_id=(x, y, 1 - c), device_id_type=MESH) for w in range(n)]
        for cpy in copies:
            cpy.start()
        for cpy in copies:
            cpy.wait()

    return pl.pallas_call(
        body, name=name, in_specs=[HBM] * n, out_specs=[HBM] * n,
        out_shape=[jax.ShapeDtypeStruct((a.shape[0],) + a.shape[2:], a.dtype) for a in arrs],
        scratch_shapes=[pltpu.SemaphoreType.DMA((n,)), pltpu.SemaphoreType.DMA((n,))],
    )(*arrs)


def _chip_exchange(arrs, name):
    n = len(arrs)

    def body(*refs):
        ins, outs = refs[:n], refs[n:2 * n]
        send, recv = refs[2 * n:]
        x, y, c = lax.axis_index("x"), lax.axis_index("y"), lax.axis_index("c")
        chips = [(1 - x, y), (x, 1 - y), (1 - x, 1 - y)]
        copies = []
        for w in range(n):
            for j, (cx, cy) in enumerate(chips):
                copies.append(pltpu.make_async_remote_copy(
                    src_ref=ins[w].at[2 * cx + cy], dst_ref=outs[w].at[j], send_sem=send.at[3 * w + j],
                    recv_sem=recv.at[3 * w + j], device_id=(cx, cy, c), device_id_type=MESH))
        for cpy in copies:
            cpy.start()
        for cpy in copies:
            cpy.wait()

    return pl.pallas_call(
        body, name=name, in_specs=[HBM] * n, out_specs=[HBM] * n,
        out_shape=[jax.ShapeDtypeStruct((3,) + a.shape[1:], a.dtype) for a in arrs],
        scratch_shapes=[pltpu.SemaphoreType.DMA((3 * n,)), pltpu.SemaphoreType.DMA((3 * n,))],
    )(*arrs)


SEM = pl.BlockSpec(memory_space=pltpu.SEMAPHORE)
ANY = pl.BlockSpec(memory_space=pl.ANY)
EFFECT = pltpu.SideEffectType.DATAFLOW_SIDE_EFFECTING
N_PEERS = 7


def _peers(x, y, c):
    return [((1 - x) if k & 4 else x, (1 - y) if k & 2 else y, (1 - c) if k & 1 else c) for k in range(1, 8)]


def _spread_copies(src_refs, land_refs, send, recv, gather):
    x, y, c = lax.axis_index("x"), lax.axis_index("y"), lax.axis_index("c")
    me = 4 * x + 2 * y + c
    copies = []
    for w in range(len(src_refs)):
        for k, (px, py, pc) in enumerate(_peers(x, y, c)):
            p = 4 * px + 2 * py + pc
            copies.append((pltpu.make_async_remote_copy(
                src_ref=src_refs[w] if gather else src_refs[w].at[p], dst_ref=land_refs[w].at[me],
                send_sem=send[w].at[k], recv_sem=recv[w].at[k], device_id=(px, py, pc), device_id_type=MESH),
                pltpu.make_async_remote_copy(
                src_ref=src_refs[w] if gather else src_refs[w].at[p], dst_ref=land_refs[w].at[p],
                send_sem=send[w].at[k], recv_sem=recv[w].at[k], device_id=(px, py, pc), device_id_type=MESH)))
    return copies


def _hbm(a):
    return pltpu.with_memory_space_constraint(a, pltpu.HBM)


def _spread_start(srcs, lands, after, gather, name):
    n = len(srcs)

    def body(*refs):
        src_refs, land_refs = refs[:n], refs[n:2 * n]
        outs = refs[2 * n + 1:]
        send, recv, token = outs[:n], outs[n:2 * n], outs[4 * n]
        for start, _ in _spread_copies(src_refs, land_refs, send, recv, gather):
            start.start()
        token[...] = jnp.zeros_like(token)

    res = pl.pallas_call(
        body, name=name,
        out_shape=tuple([pltpu.SemaphoreType.DMA((N_PEERS,))] * (2 * n)
                        + [pltpu.HBM(a.shape, a.dtype) for a in srcs] + [pltpu.HBM(a.shape, a.dtype) for a in lands]
                        + [jax.ShapeDtypeStruct((SUBLANES, LANES), F32)]),
        in_specs=[HBM] * (2 * n) + [ANY],
        out_specs=tuple([SEM] * (2 * n) + [HBM] * (2 * n) + [pl.BlockSpec(memory_space=pltpu.VMEM)]),
        input_output_aliases={i: 2 * n + i for i in range(2 * n)},
        compiler_params=pltpu.CompilerParams(has_side_effects=EFFECT),
    )(*[_hbm(a) for a in srcs], *[_hbm(a) for a in lands], after)
    return res[:n], res[n:2 * n], res[2 * n:3 * n], res[3 * n:4 * n], res[4 * n]


def _spread_wait(send, recv, srcs, lands, after, gather, name):
    n = len(srcs)

    def body(*refs):
        src_refs, land_refs = refs[:n], refs[n:2 * n]
        send_refs, recv_refs = refs[2 * n:3 * n], refs[3 * n:4 * n]
        for _, arrive in _spread_copies(src_refs, land_refs, send_refs, recv_refs, gather):
            arrive.wait_send()
            arrive.wait_recv()

    res = pl.pallas_call(
        body, name=name,
        out_shape=tuple([pltpu.HBM(a.shape, a.dtype) for a in srcs] + [pltpu.HBM(a.shape, a.dtype) for a in lands]),
        in_specs=[HBM] * (2 * n) + [SEM] * (2 * n) + [ANY],
        out_specs=tuple([HBM] * (2 * n)),
        input_output_aliases={i: i for i in range(2 * n)},
        compiler_params=pltpu.CompilerParams(has_side_effects=EFFECT),
    )(*srcs, *lands, *send, *recv, after)
    return res[n:]


def _landing(shape, dtype, own, me):
    return lax.dynamic_update_index_in_dim(lax.empty((8,) + shape, dtype), own, me, 0)


def _pad_cols(a, to):
    return jnp.pad(a, ((0, 0), (0, to - a.shape[1])))


N_GLR = GLA_W + GLA_RANK
FF_SLAB = D_FF // 4
FF_SLAB_P = FFP // 4


TRANSPOSED = ("w_in", "ffn_w_in")


def _prepare_sub1(gath):
    w_in_t = gath["w_in"].reshape(-1, gath["w_in"].shape[2])
    w2 = jnp.concatenate([gath["gla_gate_w2"][s] for s in range(8)], axis=1)
    return {"w_a_t": jnp.pad(w_in_t[:N_GLR], ((0, HA_W - N_GLR), (0, 0))), "w_b_t": w_in_t[N_GLR:],
            "w2p": jnp.pad(w2, ((0, LANES - GLA_RANK), (0, 0)))}


def _prepare_ffn_in(g):
    f = jnp.pad(g, ((0, 0), (0, FF_SLAB_P - FF_SLAB), (0, 0)))
    return f.reshape(2 * FFP, f.shape[2])


def _prepare_ffn_out(g):
    return jnp.pad(g.reshape(4, FF_SLAB, -1), ((0, 0), (0, FF_SLAB_P - FF_SLAB), (0, 0))).reshape(FFP, -1)


def _prepare_conv(g, conv_b):
    padc = FF_SLAB_P - FF_SLAB
    cw = jnp.pad(g, ((0, 0), (0, 0), (0, padc)))
    cb = jnp.pad(conv_b.reshape(8, 1, FF_SLAB), ((0, 0), (0, 0), (0, padc)))
    rows = jnp.concatenate([cw, cb, jnp.zeros((8, 4, FF_SLAB_P), F32)], axis=1)
    return jnp.concatenate([rows[s] for s in range(8)], axis=1)


def _prepare_ffn(gath, conv_b):
    return {"w_ffn_t": _prepare_ffn_in(gath["ffn_w_in"]), "wo": _prepare_ffn_out(gath["ffn_w_out"]),
            "cw": _prepare_conv(gath["ffn_conv_w"], conv_b)}


def _unpad_ff(a):
    r = a.shape[0]
    return a.reshape(r, 4, FF_SLAB_P)[:, :, :FF_SLAB].reshape(r, D_FF)


def _grad_slabs(g):
    w_in_t = jnp.concatenate([g["w_a_t"][:N_GLR], g["w_b_t"]], axis=0)
    s = {"w_in": w_in_t.reshape(4, 2, w_in_t.shape[0] // 8, w_in_t.shape[1])}
    for n in ("w_out", "ca_wq", "ca_wo"):
        s[n] = _to_slabs(n, g[n])
    for n in ("ca_wkv", "ffn_w_in"):
        s[n] = g[n].reshape((4, 2) + g[n].shape[1:])
    wo = g["wo"].reshape(4, FF_SLAB_P, -1)[:, :FF_SLAB]
    s["ffn_w_out"] = wo.reshape(4, 2, FF_SLAB // 2, wo.shape[-1])
    return s


class _AtHand:
    def __init__(self, p):
        self.p = p
        self.token = None

    def sub2(self, after):
        return self.p

    def ffn_in(self, after):
        return self.p["w_ffn_t"]

    def ffn_out(self, after):
        return self.p["wo"]

    def grads_out(self, group, slabs):
        pass


def _local_step(x, mem, positions, target, p, small, stages=None):
    t, d = x.shape
    stages = _AtHand(p) if stages is None else stages
    w_a_t, w_b_t, w2p, cw = p["w_a_t"], p["w_b_t"], p["w2p"], p["cw"]
    tabs = _rope_tables(positions)
    xb = x.astype(BF16) if stages.token is None else (x + stages.token[0, 0]).astype(BF16)
    memb = mem.astype(BF16)

    h_a = _matmul(xb, w_a_t, "nt", F32, 512, 640, d, "mm_h_a")
    h_b = _matmul(xb, w_b_t, "nt", F32, 512, 1024, d, "mm_h_b")
    o_g, o_raw, s_before = _gla_fwd(h_a, w2p, small["gla_gate_b"], small["gla_norm_g"])
    qr, kr = _rope_fwd(h_b, tabs)
    o_d_b, o_d, lse_tot = _dil_fwd_all(qr, kr, h_b)
    mixin = jnp.concatenate([o_g, o_d_b], axis=1)
    wts = stages.sub2(mixin)
    mix = _matmul(mixin, wts["w_out"], "nn", F32, 512, 1024, d, "mm_mix")
    x1, x1b, x1t = _ln_fwd(x, mix, small["ln1_g"], small["ln1_b"], "ln1_fwd")

    q_ca = _matmul(x1b, wts["ca_wq"], "nn", BF16, 512, 1024, d, "mm_caq")
    kvw = wts["ca_wkv"].shape[2]
    memkv = _matmul(memb, wts["ca_wkv"], "nn", BF16, mem.shape[0], kvw, d, "mm_memkv", b_slabs=True)
    o_c, o_ct = _ca_fwd(q_ca, memkv)
    ca_out = _matmul(o_c, wts["ca_wo"], "nn", F32, 512, 1024, d, "mm_cao")
    x2, x2b, x2t = _ln_fwd(x1, ca_out, small["ln2_g"], small["ln2_b"], "ln2_fwd")

    w_ffn_t = stages.ffn_in(x2b)
    u0 = _matmul(x2b, w_ffn_t, "nt", BF16, 512, 512, d, "mm_u0")
    act, act_t = _swiglu_fwd(u0, cw)
    wo = stages.ffn_out(act)
    ffn = _matmul(act, wo, "nn", F32, 512, 512, FFP, "mm_ffn")

    dp3, dp3b, dg3, db3, loss_part = _ln_bwd(x2, ffn, small["ln3_g"], small["ln3_b"], target, True, "ln3_bwd")
    g_wo, g_wo16 = _matmul(act_t, dp3b, "nn", F32, 512, 1024, t // 2, "mm_g_wo", also_bf16=True)
    dact = _matmul(dp3b, wo, "nt", BF16, 512, 512, d, "mm_dact")
    dug, duu, dug_t, duu_t, dcwg, dcwu = _swiglu_bwd(u0, cw, dact)
    g_ffn_in, g_ffn_in16 = _ffn_win_grad(dug_t, duu_t, x2b)

    def wo_slabs(a):
        a = a.reshape(4, FF_SLAB_P, -1)[:, :FF_SLAB]
        return a.reshape(8, FF_SLAB // 2, a.shape[-1])

    stages.grads_out("ffn", {"ffn_w_out": (wo_slabs(g_wo), wo_slabs(g_wo16)), "ffn_w_in": (g_ffn_in, g_ffn_in16)})
    dx2 = _matmul(dug, w_ffn_t, "nn", F32, 512, 512, FFP // 2, "mm_dx2_g", resid=dp3, resid_scale=ALPHA)
    dx2 = _matmul(duu, w_ffn_t, "nn", F32, 512, 512, FFP // 2, "mm_dx2_u", resid=dx2, b_k_off=2)

    dp2, dp2b, dg2, db2 = _ln_bwd(x1, ca_out, small["ln2_g"], small["ln2_b"], dx2, False, "ln2_bwd")
    g_cao, g_cao16 = _matmul(o_ct, dp2b, "nn", F32, 512, 1024, t // 2, "mm_g_cao", also_bf16=True)
    do_c = _matmul(dp2b, wts["ca_wo"], "nt", BF16, 512, 1024, d, "mm_do_c")
    dq_ca, dmemkv = _ca_bwd(q_ca, memkv, do_c)
    g_caq, g_caq16 = _matmul(x1t, dq_ca, "nn", F32, 512, 1024, t // 2, "mm_g_caq", also_bf16=True)
    g_cakv, g_cakv16 = _matmul(memb, dmemkv.astype(BF16), "tn", F32, 512, kvw, mem.shape[0], "mm_g_cakv",
                               out_slabs=True, also_bf16=True)
    dx1 = _matmul(dq_ca, wts["ca_wq"], "nt", F32, 512, 1024, d, "mm_dx1", resid=dp2, resid_scale=ALPHA)

    dp1, dp1b, dg1, db1 = _ln_bwd(x, mix, small["ln1_g"], small["ln1_b"], dx1, False, "ln1_bwd")
    g_wout, g_wout16 = _matmul(mixin, dp1b, "tn", F32, 512, 1024, 1024, "mm_g_wout", also_bf16=True)

    def row_slabs(a):
        return a.reshape(8, a.shape[0] // 8, a.shape[1])

    stages.grads_out("attn", {"ca_wo": (row_slabs(g_cao), row_slabs(g_cao16)), "ca_wq": (row_slabs(g_caq), row_slabs(g_caq16)),
                              "ca_wkv": (g_cakv, g_cakv16), "w_out": (row_slabs(g_wout), row_slabs(g_wout16))})
    dmix = _matmul(dp1b, wts["w_out"], "nt", F32, 512, 1024, d, "mm_dmix")
    dh_a, dw2, dgate_b, dnorm_g = _gla_bwd(h_a, w2p, small["gla_gate_b"], small["gla_norm_g"], o_raw, s_before, dmix)
    dq_d, dk_d, dv_d = _dil_bwd_all(qr, kr, h_b, dmix, o_d, lse_tot)
    dh_b = _dil_dh(dq_d, dk_d, dv_d, tabs)
    g_wa_t = _matmul(dh_a, xb, "tn", F32, 640, 1024, 1024, "mm_g_wa")
    g_wb_t = _matmul(dh_b, xb, "tn", F32, 512, 1024, 1024, "mm_g_wb")
    dx = _matmul(dh_a, w_a_t, "nn", F32, 512, 512, HA_W, "mm_dx_a", resid=dp1, resid_scale=ALPHA)
    dx = _matmul(dh_b, w_b_t, "nn", F32, 512, 512, HB_W, "mm_dx_b", resid=dx)

    grads = {"w_a_t": g_wa_t, "w_b_t": g_wb_t, "w_out": g_wout, "ca_wq": g_caq, "ca_wkv": g_cakv, "ca_wo": g_cao,
             "ffn_w_in": g_ffn_in, "wo": g_wo}
    small_parts = {
        "gla_gate_b": dgate_b, "gla_norm_g": dnorm_g, "ln1_g": dg1, "ln1_b": db1, "ln2_g": dg2, "ln2_b": db2,
        "ln3_g": dg3, "ln3_b": db3,
        "conv": jnp.concatenate([_unpad_ff(dcwg), _unpad_ff(dcwu)], axis=1),
        "gla_gate_w2": dw2[:GLA_RANK],
    }
    return loss_part, dx, grads, small_parts


BIG = ("w_in", "w_out", "ca_wq", "ca_wkv", "ca_wo", "ffn_w_in", "ffn_w_out")
COL_SHARDED = ("w_in", "ca_wkv", "ffn_w_in")
SMALL_ORDER = ("gla_gate_b", "gla_norm_g", "ln1_g", "ln1_b", "ln2_g", "ln2_b", "ln3_g", "ln3_b")


def _gathered_full(name, g):
    if name in COL_SHARDED:
        return g.transpose(1, 0, 2).reshape(g.shape[1], 8 * g.shape[2])
    return g.reshape(8 * g.shape[1], g.shape[2])


def _to_slabs(name, full):
    if name in COL_SHARDED:
        r, cc = full.shape
        s = full.reshape(r, 8, cc // 8).transpose(1, 0, 2)
    else:
        rr, c = full.shape
        s = full.reshape(8, rr // 8, c)
    return s.reshape((4, 2) + s.shape[1:])


def kernel(x, mem, positions, w_in, gla_gate_w2, gla_gate_b, gla_norm_g, w_out, ln1_g, ln1_b, ca_wq, ca_wkv, ca_wo, ln2_g, ln2_b, ffn_w_in, ffn_conv_w, ffn_conv_b, ffn_w_out, ln3_g, ln3_b, loss_target, m_w_in, m_gla_gate_w2, m_gla_gate_b, m_gla_norm_g, m_w_out, m_ln1_g, m_ln1_b, m_ca_wq, m_ca_wkv, m_ca_wo, m_ln2_g, m_ln2_b, m_ffn_w_in, m_ffn_conv_w, m_ffn_conv_b, m_ffn_w_out, m_ln3_g, m_ln3_b, v_w_in, v_gla_gate_w2, v_gla_gate_b, v_gla_norm_g, v_w_out, v_ln1_g, v_ln1_b, v_ca_wq, v_ca_wkv, v_ca_wo, v_ln2_g, v_ln2_b, v_ffn_w_in, v_ffn_conv_w, v_ffn_conv_b, v_ffn_w_out, v_ln3_g, v_ln3_b):
    weights = dict(w_in=w_in, gla_gate_w2=gla_gate_w2, gla_gate_b=gla_gate_b, gla_norm_g=gla_norm_g, w_out=w_out,
                   ln1_g=ln1_g, ln1_b=ln1_b, ca_wq=ca_wq, ca_wkv=ca_wkv, ca_wo=ca_wo, ln2_g=ln2_g, ln2_b=ln2_b,
                   ffn_w_in=ffn_w_in, ffn_conv_w=ffn_conv_w, ffn_conv_b=ffn_conv_b, ffn_w_out=ffn_w_out,
                   ln3_g=ln3_g, ln3_b=ln3_b)
    moms = dict(w_in=(m_w_in, v_w_in), gla_gate_w2=(m_gla_gate_w2, v_gla_gate_w2), gla_gate_b=(m_gla_gate_b, v_gla_gate_b),
                gla_norm_g=(m_gla_norm_g, v_gla_norm_g), w_out=(m_w_out, v_w_out), ln1_g=(m_ln1_g, v_ln1_g),
                ln1_b=(m_ln1_b, v_ln1_b), ca_wq=(m_ca_wq, v_ca_wq), ca_wkv=(m_ca_wkv, v_ca_wkv), ca_wo=(m_ca_wo, v_ca_wo),
                ln2_g=(m_ln2_g, v_ln2_g), ln2_b=(m_ln2_b, v_ln2_b), ffn_w_in=(m_ffn_w_in, v_ffn_w_in),
                ffn_conv_w=(m_ffn_conv_w, v_ffn_conv_w), ffn_conv_b=(m_ffn_conv_b, v_ffn_conv_b),
                ffn_w_out=(m_ffn_w_out, v_ffn_w_out), ln3_g=(m_ln3_g, v_ln3_g), ln3_b=(m_ln3_b, v_ln3_b))
    order = list(weights)
    xi, yi, ci = lax.axis_index("x"), lax.axis_index("y"), lax.axis_index("c")
    me = 4 * xi + 2 * yi + ci

    def travel(n, a):
        return jnp.swapaxes(a, 1, 2) if n in TRANSPOSED else a

    shard = {n: travel(n, weights[n]).astype(BF16)[0] for n in BIG}
    first = _all_gather([shard["w_in"], gla_gate_w2.astype(BF16)[0], ffn_conv_w[0]], "ag_first")
    p = _prepare_sub1({"w_in": first[0], "gla_gate_w2": first[1]})
    p["cw"] = _prepare_conv(first[2], ffn_conv_b)
    later = ("w_out", "ca_wq", "ca_wkv", "ca_wo", "ffn_w_in", "ffn_w_out")
    srcs = [shard[n] for n in later]
    lands = [_landing(shard[n].shape, BF16, shard[n], me) for n in later]
    send, recv, srcs, lands, token = _spread_start(srcs, lands, first[0], True, "ag_rest_start")

    class stages:
        pass

    stages.token = token

    def arrived(lo, hi, after, name):
        return _spread_wait(send[lo:hi], recv[lo:hi], srcs[lo:hi], lands[lo:hi], after, True, name)

    def sub2(after):
        g = dict(zip(later[:4], arrived(0, 4, after, "ag_wait_attn")))
        w = {n: _gathered_full(n, g[n]) for n in ("w_out", "ca_wq", "ca_wo")}
        w["ca_wkv"] = g["ca_wkv"]
        return w

    stages.sub2 = sub2
    stages.ffn_in = lambda after: _prepare_ffn_in(arrived(4, 5, after, "ag_wait_ffn_in")[0])
    stages.ffn_out = lambda after: _prepare_ffn_out(arrived(5, 6, after, "ag_wait_ffn_out")[0])
    sent = {}

    def grads_out(group, slabs):
        names = list(slabs)
        srcs16 = [slabs[n][1] for n in names]
        zones = [_landing(s.shape[1:], BF16, jnp.zeros(s.shape[1:], BF16), me) for s in srcs16]
        snd, rcv, s_thru, l_thru, _ = _spread_start(srcs16, zones, slabs[names[0]][0], False, f"rs_{group}_start")
        sent[group] = (names, [slabs[n][0] for n in names], (snd, rcv, s_thru, l_thru))

    stages.grads_out = grads_out
    small = dict(gla_gate_b=gla_gate_b, gla_norm_g=gla_norm_g, ln1_g=ln1_g, ln1_b=ln1_b, ln2_g=ln2_g, ln2_b=ln2_b,
                 ln3_g=ln3_g, ln3_b=ln3_b)

    loss_part, dx, grads, small_parts = _local_step(x[0], mem[0], positions[0], loss_target[0], p, small, stages)
    loss = lax.psum(jnp.sum(loss_part), ("x", "y", "c"))

    out = {}
    me1 = me.reshape(1).astype(jnp.int32)
    for group, (names, own32, handles) in sent.items():
        landed = _spread_wait(*handles, dx, False, f"rs_{group}_wait")
        for n, own, land in zip(names, own32, landed):
            m_, v_ = moms[n]
            res4 = _adamw_direct(travel(n, weights[n]), travel(n, m_), travel(n, v_), own, land, me1, f"adamw_{n}")
            out[n] = [travel(n, a) for a in res4]

    slabs = [_grad_slabs(grads)["w_in"]]
    from_sib = _sibling_exchange(slabs, "rs_sibling")
    core = ci.reshape(1).astype(jnp.int32)
    p32, p16 = _pair_add(slabs[0], from_sib[0], core, "pair_add_w_in")
    (from_chips,) = _chip_exchange([p16], "rs_chips")
    chip = (2 * xi + yi).reshape(1).astype(jnp.int32)
    res4 = _adamw_big(travel("w_in", w_in), travel("w_in", m_w_in), travel("w_in", v_w_in), p32, from_chips, chip,
                      "adamw_w_in")
    out["w_in"] = [travel("w_in", a) for a in res4]

    packed = jnp.concatenate([small_parts[n] for n in SMALL_ORDER] + [small_parts["conv"],
                             small_parts["gla_gate_w2"].reshape(SUBLANES, -1)], axis=1)
    pad = (-packed.shape[1]) % 2048
    packed = jnp.pad(packed, ((0, 0), (0, pad)))
    (allp,) = _all_gather([packed], "ag_small")
    dev_sum, row_sum = _small_reduce(allp)
    off = 0
    for n in SMALL_ORDER:
        width = weights[n].shape[1]
        g = row_sum[0:1, off:off + width]
        off += width
        m_, v_ = moms[n]
        out[n] = _adamw(weights[n], m_, v_, g, f"adamw_{n}")
    conv_g = dev_sum[:, off:off + 2 * D_FF]
    off += 2 * D_FF
    g_cb = conv_g[3:4]
    out["ffn_conv_b"] = _adamw(ffn_conv_b, m_ffn_conv_b, v_ffn_conv_b, g_cb, "adamw_ffn_conv_b")
    wsh = ffn_conv_w.shape[2]
    g_cw = lax.dynamic_slice_in_dim(conv_g[0:3], me * wsh, wsh, axis=1)
    out["ffn_conv_w"] = _adamw(ffn_conv_w[0], m_ffn_conv_w[0], v_ffn_conv_w[0], g_cw, "adamw_ffn_conv_w")
    w2_g = dev_sum[:, off:off + GLA_RANK * GLA_HEADS * GLA_DK // SUBLANES].reshape(GLA_RANK, GLA_HEADS * GLA_DK)
    wsh2 = gla_gate_w2.shape[2]
    g_w2 = lax.dynamic_slice_in_dim(w2_g, me * wsh2, wsh2, axis=1)
    out["gla_gate_w2"] = _adamw(gla_gate_w2[0], m_gla_gate_w2[0], v_gla_gate_w2[0], g_w2, "adamw_gla_gate_w2")

    def shaped(n, a):
        return a.reshape(weights[n].shape)

    res = [loss, dx[None]]
    for k in range(4):
        res += [shaped(n, out[n][k]) for n in order]
    return tuple(res)


def _adamw_direct(w, m, v, own, land, me, name):
    _, r, c = w.shape
    tr, tc = _tile2d(r, c)
    blk = pl.BlockSpec((None, tr, tc), lambda i, j, s: (0, i, j))
    mine = pl.BlockSpec((None, tr, tc), lambda i, j, s: (s[0], i, j))
    slots = [pl.BlockSpec((None, tr, tc), lambda i, j, s, k=k: (k, i, j)) for k in range(8)]

    def body(s_ref, w_ref, m_ref, v_ref, p_ref, *rest):
        slot_refs, (g_ref, d_ref, nm_ref, nv_ref) = rest[:8], rest[8:]
        g = p_ref[...]
        for sr in slot_refs:
            g = g + sr[...].astype(F32)
        d_ref[...], nm_ref[...], nv_ref[...] = _adamw_math(w_ref[...], m_ref[...], v_ref[...], g)
        g_ref[...] = g

    gs = pltpu.PrefetchScalarGridSpec(num_scalar_prefetch=1, grid=(r // tr, c // tc),
                                      in_specs=[blk, blk, blk, mine] + slots, out_specs=[blk] * 4)
    return pl.pallas_call(body, name=name, grid_spec=gs, out_shape=[jax.ShapeDtypeStruct((1, r, c), F32)] * 4,
                          compiler_params=_params(("parallel", "parallel")))(me, w, m, v, own, *([land] * 8))


def _adamw_big(w, m, v, p32, rc, chip, name):
    _, r, c = w.shape
    tr, tc = _tile2d(r, c)
    blk = pl.BlockSpec((None, tr, tc), lambda i, j, s: (0, i, j))
    own = pl.BlockSpec((None, tr, tc), lambda i, j, s: (s[0], i, j))
    others = [pl.BlockSpec((None, tr, tc), lambda i, j, s, k=k: (k, i, j)) for k in range(3)]

    def body(s_ref, w_ref, m_ref, v_ref, p_ref, r0_ref, r1_ref, r2_ref, g_ref, d_ref, nm_ref, nv_ref):
        g = ((p_ref[...] + r0_ref[...].astype(F32)) + r1_ref[...].astype(F32)) + r2_ref[...].astype(F32)
        d_ref[...], nm_ref[...], nv_ref[...] = _adamw_math(w_ref[...], m_ref[...], v_ref[...], g)
        g_ref[...] = g

    gs = pltpu.PrefetchScalarGridSpec(num_scalar_prefetch=1, grid=(r // tr, c // tc),
                                      in_specs=[blk, blk, blk, own] + others, out_specs=[blk] * 4)
    return pl.pallas_call(body, name=name, grid_spec=gs, out_shape=[jax.ShapeDtypeStruct((1, r, c), F32)] * 4,
                          compiler_params=_params(("parallel", "parallel")))(chip, w, m, v, p32, rc, rc, rc)
```

```python
import functools
import math

import jax
import jax.numpy as jnp
from jax import lax
from jax.experimental import pallas as pl
from jax.experimental.pallas import tpu as pltpu

F32 = jnp.float32
BF16 = jnp.bfloat16
MESH = pl.DeviceIdType.MESH

D_MODEL = 2048
LN_EPS = 1e-5
GLA_HEADS = 4
GLA_DV = 256
GLA_DK = 128
GLA_RANK = 16
GLA_TAU = 16.0
GLA_CHUNK = 64
DIL_HD = 128
DIL_HEADS = 8
DIL_BAND = 128
DIL_DILATIONS = (1, 4, 16)
ROPE_THETA = 500000.0
ROPE_DIMS = 32
CA_HEADS = 4
CA_HD = 512
D_FF = 5504
ALPHA = 2.0 ** 0.25
ADAM_LR = 0.001
ADAM_B1 = 0.9
ADAM_B2 = 0.999
ADAM_EPS = 1e-08
ADAM_WD = 0.01
ADAM_STEP = 10

LANES = 128
SUBLANES = 8
VMEM_LIMIT = 56 * 1024 * 1024

GLA_W = 2 * GLA_HEADS * GLA_DK + 2 * GLA_HEADS * GLA_DV
HA_W = GLA_W + LANES
HB_W = 3 * DIL_HEADS * DIL_HD
FFP = 5632
NEG = -1e30


def _params(sem):
    return pltpu.CompilerParams(dimension_semantics=sem, vmem_limit_bytes=VMEM_LIMIT)


def _sigmoid(x):
    return 1.0 / (1.0 + jnp.exp(-x))


def _dot(a, b, dn, precision=None):
    return lax.dot_general(a, b, (dn, ((), ())), preferred_element_type=F32, precision=precision)


NN = ((1,), (0,))
NT = ((1,), (1,))
TN = ((0,), (0,))


def _bf(v):
    return v if v.dtype == BF16 else v.astype(BF16)


def _matmul(a, b, kind, out_dtype, tm, tn, tk, name, resid=None, resid_scale=1.0, b_k_off=0, b_slabs=False,
            out_slabs=False, also_bf16=False, dep=None):
    if b_slabs:
        assert kind != "nt" and b.shape[2] == tn
        k2, n = b.shape[1], b.shape[0] * tn
    elif kind == "nt":
        n, k2 = b.shape
    else:
        k2, n = b.shape
    (k, m) = a.shape if kind == "tn" else a.shape[::-1]
    assert k2 >= k and (k2 == k or not b_slabs) and m % tm == 0 and n % tn == 0 and k % tk == 0, \
        (name, a.shape, b.shape, tm, tn, tk)
    nk = k // tk
    dn = {"nn": NN, "nt": NT, "tn": TN}[kind]
    a_spec = pl.BlockSpec((tk, tm), lambda i, j, kk: (kk, i)) if kind == "tn" else pl.BlockSpec((tm, tk), lambda i, j, kk: (i, kk))
    if b_slabs:
        b_spec = pl.BlockSpec((None, tk, tn), lambda i, j, kk: (j, kk, 0))
    elif kind == "nt":
        b_spec = pl.BlockSpec((tn, tk), lambda i, j, kk: (j, kk + b_k_off))
    else:
        b_spec = pl.BlockSpec((tk, tn), lambda i, j, kk: (kk + b_k_off, j))
    if out_slabs:
        o_spec = pl.BlockSpec((None, tm, tn), lambda i, j, kk: (j, i, 0))
        o_shape = (n // tn, m, tn)
    else:
        o_spec = pl.BlockSpec((tm, tn), lambda i, j, kk: (i, j))
        o_shape = (m, n)
    has_resid = resid is not None

    n_in = 2 + int(has_resid) + int(dep is not None)

    def body(*refs):
        a_ref, b_ref = refs[:2]
        r_ref = refs[2] if has_resid else None
        o_ref = refs[n_in]
        ob_ref = refs[n_in + 1] if also_bf16 else None
        part = _dot(_bf(a_ref[...]), _bf(b_ref[...]), dn)

        def finish(acc):
            if has_resid:
                acc = acc + resid_scale * r_ref[...].astype(F32)
            o_ref[...] = acc.astype(out_dtype)
            if also_bf16:
                ob_ref[...] = acc.astype(BF16)

        if nk == 1:
            finish(part)
        else:
            acc_ref = refs[-1]
            kk = pl.program_id(2)

            @pl.when(kk == 0)
            def _():
                acc_ref[...] = part

            @pl.when(kk > 0)
            def _():
                acc_ref[...] += part

            @pl.when(kk == nk - 1)
            def _():
                finish(acc_ref[...])

    in_specs = [a_spec, b_spec] + ([o_spec] if has_resid else [])
    args = (a, b) + ((resid,) if has_resid else ())
    if dep is not None:
        in_specs.append(pl.BlockSpec((SUBLANES, LANES), lambda i, j, kk: (0, 0)))
        args += (dep,)
    o_struct = jax.ShapeDtypeStruct(o_shape, out_dtype)
    return pl.pallas_call(
        body, name=name, out_shape=[o_struct, jax.ShapeDtypeStruct(o_shape, BF16)] if also_bf16 else o_struct,
        grid=(m // tm, n // tn, nk), in_specs=in_specs, out_specs=[o_spec, o_spec] if also_bf16 else o_spec,
        scratch_shapes=[pltpu.VMEM((tm, tn), F32)] if nk > 1 else [],
        compiler_params=_params(("parallel", "parallel", "arbitrary")),
    )(*args)


def _ln_core(xres, f):
    p = ALPHA * xres + f
    mu = jnp.mean(p, axis=-1, keepdims=True)
    xc = p - mu
    var = jnp.mean(xc * xc, axis=-1, keepdims=True)
    rstd = lax.rsqrt(var + LN_EPS)
    return xc * rstd, rstd


def _rows8(v):
    r, c = v.shape
    return jnp.sum(v.reshape(r // SUBLANES, SUBLANES, c), axis=0)


def _ln_fwd(xres, f, g, b, name, tr=256):
    t, d = xres.shape
    row = pl.BlockSpec((tr, d), lambda i: (i, 0))
    vec = pl.BlockSpec((1, d), lambda i: (0, 0))

    def body(x_ref, f_ref, g_ref, b_ref, y_ref, yb_ref, yt_ref):
        xhat, _ = _ln_core(x_ref[...], f_ref[...])
        y = xhat * g_ref[...] + b_ref[...]
        y_ref[...] = y
        yb = y.astype(BF16)
        yb_ref[...] = yb
        yt_ref[...] = yb.T

    return pl.pallas_call(
        body, name=name, grid=(t // tr,), in_specs=[row, row, vec, vec],
        out_specs=[row, row, pl.BlockSpec((d, tr), lambda i: (0, i))],
        out_shape=[jax.ShapeDtypeStruct((t, d), F32), jax.ShapeDtypeStruct((t, d), BF16),
                   jax.ShapeDtypeStruct((d, t), BF16)],
        compiler_params=_params(("parallel",)),
    )(xres, f, g, b)


def _ln_bwd(xres, f, g, b, dy_or_target, loss_head, name, tr=256):
    t, d = xres.shape
    row = pl.BlockSpec((tr, d), lambda i: (i, 0))
    vec = pl.BlockSpec((1, d), lambda i: (0, 0))
    acc = pl.BlockSpec((SUBLANES, d), lambda i: (0, 0))
    lacc = pl.BlockSpec((SUBLANES, LANES), lambda i: (0, 0))

    def body(x_ref, f_ref, g_ref, b_ref, t_ref, dp_ref, dpb_ref, dg_ref, db_ref, *rest):
        i = pl.program_id(0)
        xhat, rstd = _ln_core(x_ref[...], f_ref[...])
        if loss_head:
            err = xhat * g_ref[...] + b_ref[...] - t_ref[...]
            dy = err * (1.0 / d)
            sq = err * err
            lanes = sq[:, :LANES]
            for kk in range(1, d // LANES):
                lanes = lanes + sq[:, kk * LANES:(kk + 1) * LANES]
            lpart = _rows8(lanes) * (0.5 / d)
        else:
            dy = t_ref[...]
        dxh = dy * g_ref[...]
        m1 = jnp.mean(dxh, axis=-1, keepdims=True)
        m2 = jnp.mean(dxh * xhat, axis=-1, keepdims=True)
        dp = rstd * (dxh - m1 - xhat * m2)
        dp_ref[...] = dp
        dpb_ref[...] = dp.astype(BF16)
        dgp = _rows8(dy * xhat)
        dbp = _rows8(dy)

        @pl.when(i == 0)
        def _():
            dg_ref[...] = dgp
            db_ref[...] = dbp
            if loss_head:
                rest[0][...] = lpart

        @pl.when(i > 0)
        def _():
            dg_ref[...] += dgp
            db_ref[...] += dbp
            if loss_head:
                rest[0][...] += lpart

    out_shape = [jax.ShapeDtypeStruct((t, d), F32), jax.ShapeDtypeStruct((t, d), BF16),
                 jax.ShapeDtypeStruct((SUBLANES, d), F32), jax.ShapeDtypeStruct((SUBLANES, d), F32)]
    out_specs = [row, row, acc, acc]
    if loss_head:
        out_shape.append(jax.ShapeDtypeStruct((SUBLANES, LANES), F32))
        out_specs.append(lacc)
    return pl.pallas_call(
        body, name=name, grid=(t // tr,), in_specs=[row, row, vec, vec, row], out_specs=out_specs,
        out_shape=out_shape, compiler_params=_params(("arbitrary",)),
    )(xres, f, g, b, dy_or_target)


def _gla_gates(glr, w2, gb):
    z = _dot(_bf(glr), w2, NN) + gb
    lg = (jnp.minimum(z, 0.0) - jnp.log(1.0 + jnp.exp(-jnp.abs(z)))) * (1.0 / GLA_TAU)
    c = z.shape[0]
    ri = lax.broadcasted_iota(jnp.int32, (c, c), 0)
    ci = lax.broadcasted_iota(jnp.int32, (c, c), 1)
    tri = (ci <= ri).astype(F32)
    bcum = _dot(tri, lg, NN, precision=lax.Precision.HIGHEST)
    blast = jnp.sum(lg, axis=0, keepdims=True)
    return z, bcum, blast, tri


def _gla_specs(t):
    c = GLA_CHUNK
    return c, t // c


def _gla_fwd(h_a, w2p, gate_b, norm_g):
    t = h_a.shape[0]
    c, n = _gla_specs(t)
    hk, hv = GLA_HEADS * GLA_DK, GLA_HEADS * GLA_DV
    scale = GLA_DK ** -0.5

    def body(q_ref, k_ref, v_ref, r_ref, glr_ref, w2_ref, gb_ref, ng_ref, og_ref, oraw_ref, sb_ref, st_ref):
        i = pl.program_id(0)

        @pl.when(i == 0)
        def _():
            st_ref[...] = jnp.zeros_like(st_ref)

        _, bcum, blast, _ = _gla_gates(glr_ref[...], w2_ref[...], gb_ref[...])
        ri = lax.broadcasted_iota(jnp.int32, (c, c), 0)
        ci = lax.broadcasted_iota(jnp.int32, (c, c), 1)
        causal = ci <= ri
        for h in range(GLA_HEADS):
            ks = slice(h * GLA_DK, (h + 1) * GLA_DK)
            vs = slice(h * GLA_DV, (h + 1) * GLA_DV)
            b_h, bl_h = bcum[:, ks], blast[:, ks]
            q_h, k_h = q_ref[:, ks], k_ref[:, ks]
            v_h = _bf(v_ref[:, vs])
            qi = _bf(q_h * scale * jnp.exp(b_h))
            ki = _bf(k_h * jnp.exp(-b_h))
            ke = _bf(k_h * jnp.exp(bl_h - b_h))
            st = st_ref[h]
            sb_ref[0, h] = st
            a = jnp.where(causal, _dot(qi, ki, NT), 0.0)
            o = _dot(_bf(a), v_h, NN) + _dot(qi, _bf(st), NT)
            st_ref[h] = st * jnp.exp(bl_h) + _dot(v_h, ke, TN)
            oraw_ref[:, vs] = o
            mu = jnp.mean(o, axis=-1, keepdims=True)
            oc = o - mu
            var = jnp.mean(oc * oc, axis=-1, keepdims=True)
            xh = oc * lax.rsqrt(var + LN_EPS)
            r_h = r_ref[:, vs]
            og_ref[:, vs] = (xh * ng_ref[:, vs] * (r_h * _sigmoid(r_h))).astype(BF16)

    return pl.pallas_call(
        body, name="gla_fwd", grid=(n,),
        in_specs=[pl.BlockSpec((c, hk), lambda i: (i, 0)), pl.BlockSpec((c, hk), lambda i: (i, 1)),
                  pl.BlockSpec((c, hv), lambda i: (i, 1)), pl.BlockSpec((c, hv), lambda i: (i, 2)),
                  pl.BlockSpec((c, LANES), lambda i: (i, GLA_W // LANES)),
                  pl.BlockSpec((LANES, hk), lambda i: (0, 0)), pl.BlockSpec((1, hk), lambda i: (0, 0)),
                  pl.BlockSpec((1, hv), lambda i: (0, 0))],
        out_specs=[pl.BlockSpec((c, hv), lambda i: (i, 0)), pl.BlockSpec((c, hv), lambda i: (i, 0)),
                   pl.BlockSpec((1, GLA_HEADS, GLA_DV, GLA_DK), lambda i: (i, 0, 0, 0))],
        out_shape=[jax.ShapeDtypeStruct((t, hv), BF16), jax.ShapeDtypeStruct((t, hv), F32),
                   jax.ShapeDtypeStruct((n, GLA_HEADS, GLA_DV, GLA_DK), F32)],
        scratch_shapes=[pltpu.VMEM((GLA_HEADS, GLA_DV, GLA_DK), F32)],
        compiler_params=_params(("arbitrary",)),
    )(h_a, h_a, h_a, h_a, h_a, w2p, gate_b, norm_g)


def _gla_bwd(h_a, w2p, gate_b, norm_g, o_raw, s_before, dmix):
    t = h_a.shape[0]
    c, n = _gla_specs(t)
    hk, hv = GLA_HEADS * GLA_DK, GLA_HEADS * GLA_DV
    scale = GLA_DK ** -0.5
    rev = lambda i: n - 1 - i

    def body(q_ref, k_ref, v_ref, r_ref, glr_ref, w2_ref, gb_ref, ng_ref, oraw_ref, sb_ref, do_ref,
             dh_ref, dw2_ref, dgb_ref, dng_ref, dst_ref):
        i = pl.program_id(0)

        @pl.when(i == 0)
        def _():
            dst_ref[...] = jnp.zeros_like(dst_ref)

        glr = glr_ref[...]
        z, bcum, blast, tri = _gla_gates(glr, w2_ref[...], gb_ref[...])
        ri = lax.broadcasted_iota(jnp.int32, (c, c), 0)
        ci = lax.broadcasted_iota(jnp.int32, (c, c), 1)
        causal = ci <= ri
        dlg_parts = []
        dng_parts = []
        for h in range(GLA_HEADS):
            ks = slice(h * GLA_DK, (h + 1) * GLA_DK)
            vs = slice(h * GLA_DV, (h + 1) * GLA_DV)
            o = oraw_ref[:, vs]
            mu = jnp.mean(o, axis=-1, keepdims=True)
            oc = o - mu
            var = jnp.mean(oc * oc, axis=-1, keepdims=True)
            rstd = lax.rsqrt(var + LN_EPS)
            xh = oc * rstd
            r_h = r_ref[:, vs]
            sg = _sigmoid(r_h)
            silu = r_h * sg
            dout = do_ref[:, vs]
            ng = ng_ref[:, vs]
            dng_parts.append(_rows8(dout * xh * silu))
            dr = dout * xh * ng * (sg * (1.0 + r_h * (1.0 - sg)))
            dxh = dout * ng * silu
            m1 = jnp.mean(dxh, axis=-1, keepdims=True)
            m2 = jnp.mean(dxh * xh, axis=-1, keepdims=True)
            do_raw = _bf(rstd * (dxh - m1 - xh * m2))
            b_h, bl_h = bcum[:, ks], blast[:, ks]
            q_h, k_h = q_ref[:, ks], k_ref[:, ks]
            v_h = _bf(v_ref[:, vs])
            eb, enb, eend = jnp.exp(b_h), jnp.exp(-b_h), jnp.exp(bl_h - b_h)
            decay = jnp.exp(bl_h)
            qi_f, ki_f, ke_f = q_h * scale * eb, k_h * enb, k_h * eend
            qi, ki, ke = _bf(qi_f), _bf(ki_f), _bf(ke_f)
            st = sb_ref[0, h]
            dst = dst_ref[h]
            dst_b = _bf(dst)
            a = _bf(jnp.where(causal, _dot(qi, ki, NT), 0.0))
            da = _bf(jnp.where(causal, _dot(do_raw, v_h, NT), 0.0))
            dv = _dot(a, do_raw, TN) + _dot(ke, dst_b, NT)
            dqi = _dot(da, ki, NN) + _dot(do_raw, _bf(st), NN)
            dki = _dot(da, qi, TN)
            dke = _dot(v_h, dst_b, NN)
            dst_ref[h] = _dot(do_raw, qi, TN) + dst * decay
            dbl = decay * jnp.sum(st * dst, axis=0, keepdims=True) + jnp.sum(dke * ke_f, axis=0, keepdims=True)
            dbc = dqi * qi_f - dki * ki_f - dke * ke_f
            dlg_parts.append(_dot(tri, dbc, TN, precision=lax.Precision.HIGHEST) + dbl)
            dh_ref[:, ks] = (dqi * eb * scale).astype(BF16)
            dh_ref[:, hk + h * GLA_DK: hk + (h + 1) * GLA_DK] = (dki * enb + dke * eend).astype(BF16)
            dh_ref[:, 2 * hk + h * GLA_DV: 2 * hk + (h + 1) * GLA_DV] = dv.astype(BF16)
            dh_ref[:, 2 * hk + hv + h * GLA_DV: 2 * hk + hv + (h + 1) * GLA_DV] = dr.astype(BF16)
        dlg = jnp.concatenate(dlg_parts, axis=1)
        dz = dlg * (1.0 / GLA_TAU) * _sigmoid(-z)
        dz_b = _bf(dz)
        dh_ref[:, GLA_W:] = _dot(dz_b, w2_ref[...], NT).astype(BF16)
        dw2p = _dot(_bf(glr), dz_b, TN)
        dgbp = _rows8(dz)
        dngp = jnp.concatenate(dng_parts, axis=1)

        @pl.when(i == 0)
        def _():
            dw2_ref[...] = dw2p
            dgb_ref[...] = dgbp
            dng_ref[...] = dngp

        @pl.when(i > 0)
        def _():
            dw2_ref[...] += dw2p
            dgb_ref[...] += dgbp
            dng_ref[...] += dngp

    return pl.pallas_call(
        body, name="gla_bwd", grid=(n,),
        in_specs=[pl.BlockSpec((c, hk), lambda i: (rev(i), 0)), pl.BlockSpec((c, hk), lambda i: (rev(i), 1)),
                  pl.BlockSpec((c, hv), lambda i: (rev(i), 1)), pl.BlockSpec((c, hv), lambda i: (rev(i), 2)),
                  pl.BlockSpec((c, LANES), lambda i: (rev(i), GLA_W // LANES)),
                  pl.BlockSpec((LANES, hk), lambda i: (0, 0)), pl.BlockSpec((1, hk), lambda i: (0, 0)),
                  pl.BlockSpec((1, hv), lambda i: (0, 0)),
                  pl.BlockSpec((c, hv), lambda i: (rev(i), 0)),
                  pl.BlockSpec((1, GLA_HEADS, GLA_DV, GLA_DK), lambda i: (rev(i), 0, 0, 0)),
                  pl.BlockSpec((c, hv), lambda i: (rev(i), 0))],
        out_specs=[pl.BlockSpec((c, HA_W), lambda i: (rev(i), 0)),
                   pl.BlockSpec((LANES, hk), lambda i: (0, 0)),
                   pl.BlockSpec((SUBLANES, hk), lambda i: (0, 0)),
                   pl.BlockSpec((SUBLANES, hv), lambda i: (0, 0))],
        out_shape=[jax.ShapeDtypeStruct((t, HA_W), BF16), jax.ShapeDtypeStruct((LANES, hk), F32),
                   jax.ShapeDtypeStruct((SUBLANES, hk), F32), jax.ShapeDtypeStruct((SUBLANES, hv), F32)],
        scratch_shapes=[pltpu.VMEM((GLA_HEADS, GLA_DV, GLA_DK), F32)],
        compiler_params=_params(("arbitrary",)),
    )(h_a, h_a, h_a, h_a, h_a, w2p, gate_b, norm_g, o_raw, s_before, dmix)


def _rope_tables(positions):
    half = ROPE_DIMS // 2
    inv_freq = ROPE_THETA ** (-jnp.arange(0, ROPE_DIMS, 2, dtype=F32) / ROPE_DIMS)
    ang = positions.astype(F32).reshape(-1, 1) * inv_freq
    cos, sin = jnp.cos(ang), jnp.sin(ang)
    t = cos.shape[0]
    one = jnp.ones((t, DIL_HD - ROPE_DIMS), F32)
    zero = jnp.zeros((t, DIL_HD - ROPE_DIMS), F32)
    zh = jnp.zeros((t, half), F32)
    return (jnp.concatenate([cos, cos, one], axis=1), jnp.concatenate([-sin, zh, zero], axis=1),
            jnp.concatenate([zh, sin, zero], axis=1))


def _rope_apply(x, c, s1, s2):
    half = ROPE_DIMS // 2
    return x * c + pltpu.roll(x, DIL_HD - half, 1) * s1 + pltpu.roll(x, half, 1) * s2


def _rope_apply_t(dy, c, s1, s2):
    half = ROPE_DIMS // 2
    return dy * c + pltpu.roll(dy * s1, half, 1) + pltpu.roll(dy * s2, DIL_HD - half, 1)


def _rope_fwd(h_b, tabs, tr=256):
    t = h_b.shape[0]
    w = DIL_HEADS * DIL_HD
    scale = DIL_HD ** -0.5
    tab = pl.BlockSpec((tr, DIL_HD), lambda i: (i, 0))
    outb = pl.BlockSpec((tr, w), lambda i: (i, 0))

    def body(q_ref, k_ref, c_ref, s1_ref, s2_ref, qo_ref, ko_ref):
        c, s1, s2 = c_ref[...], s1_ref[...], s2_ref[...]
        for h in range(DIL_HEADS):
            hs = slice(h * DIL_HD, (h + 1) * DIL_HD)
            qo_ref[:, hs] = _rope_apply(q_ref[:, hs] * scale, c, s1, s2)
            ko_ref[:, hs] = _rope_apply(k_ref[:, hs], c, s1, s2)

    return pl.pallas_call(
        body, name="rope_fwd", grid=(t // tr,),
        in_specs=[pl.BlockSpec((tr, w), lambda i: (i, 0)), pl.BlockSpec((tr, w), lambda i: (i, 1)), tab, tab, tab],
        out_specs=[outb, outb],
        out_shape=[jax.ShapeDtypeStruct((t, w), F32)] * 2,
        compiler_params=_params(("parallel",)),
    )(h_b, h_b, *tabs)


def _dil_dh(dq, dk, dv, tabs, tr=256):
    t, w = dq.shape
    scale = DIL_HD ** -0.5
    tab = pl.BlockSpec((tr, DIL_HD), lambda i: (i, 0))
    inb = pl.BlockSpec((tr, w), lambda i: (i, 0))

    def body(dq_ref, dk_ref, dv_ref, c_ref, s1_ref, s2_ref, o_ref):
        c, s1, s2 = c_ref[...], s1_ref[...], s2_ref[...]
        for h in range(DIL_HEADS):
            hs = slice(h * DIL_HD, (h + 1) * DIL_HD)
            o_ref[:, h * DIL_HD:(h + 1) * DIL_HD] = (_rope_apply_t(dq_ref[:, hs], c, s1, s2) * scale).astype(BF16)
            o_ref[:, w + h * DIL_HD: w + (h + 1) * DIL_HD] = _rope_apply_t(dk_ref[:, hs], c, s1, s2).astype(BF16)
        o_ref[:, 2 * w:] = dv_ref[...].astype(BF16)

    return pl.pallas_call(
        body, name="dil_dh", grid=(t // tr,), in_specs=[inb] * 3 + [tab] * 3,
        out_specs=pl.BlockSpec((tr, 3 * w), lambda i: (i, 0)),
        out_shape=jax.ShapeDtypeStruct((t, 3 * w), BF16), compiler_params=_params(("parallel",)),
    )(dq, dk, dv, *tabs)


BANDS = 8


def _to_branch(a, d):
    t, w = a.shape
    return a.reshape(t // d, d, w // DIL_HD, DIL_HD).transpose(1, 2, 0, 3).reshape(-1, DIL_HD)


def _from_branch(a, d, t):
    hds = a.shape[0] // t
    return a.reshape(d, hds, t // d, DIL_HD).transpose(2, 0, 1, 3).reshape(t, hds * DIL_HD)


def _band_masks(not_first):
    r = lax.broadcasted_iota(jnp.int32, (DIL_BAND, 2 * DIL_BAND), 0)
    c = lax.broadcasted_iota(jnp.int32, (DIL_BAND, 2 * DIL_BAND), 1)
    nf = jnp.full((DIL_BAND, 2 * DIL_BAND), not_first, jnp.int32)
    look_back = jnp.logical_and(jnp.logical_and(c < DIL_BAND, c >= r), nf > 0)
    own_band = jnp.logical_and(c >= DIL_BAND, (c - DIL_BAND) <= r)
    return jnp.logical_or(look_back, own_band)


def _dil_fwd(q, k, v, nb, name):
    rows = q.shape[0]
    blk = BANDS * DIL_BAND
    steps = rows // blk
    main = pl.BlockSpec((blk, DIL_HD), lambda i: (i, 0))
    prev = pl.BlockSpec((DIL_BAND, DIL_HD), lambda i: (jnp.maximum(i * BANDS - 1, 0), 0))

    def body(q_ref, k_ref, v_ref, kp_ref, vp_ref, o_ref, l_ref):
        i = pl.program_id(0)
        for j in range(BANDS):
            lo, hi = j * DIL_BAND, (j + 1) * DIL_BAND
            if j == 0:
                kcat = jnp.concatenate([kp_ref[...], k_ref[lo:hi, :]], axis=0)
                vcat = jnp.concatenate([vp_ref[...], v_ref[lo:hi, :]], axis=0)
            else:
                kcat = k_ref[lo - DIL_BAND:hi, :]
                vcat = v_ref[lo - DIL_BAND:hi, :]
            not_first = (((i * BANDS + j) % nb) != 0).astype(jnp.int32)
            s = jnp.where(_band_masks(not_first), _dot(q_ref[lo:hi, :], kcat, NT), NEG)
            m = jnp.max(s, axis=-1, keepdims=True)
            p = jnp.exp(s - m)
            den = jnp.sum(p, axis=-1, keepdims=True)
            o_ref[lo:hi, :] = _dot(_bf(p), vcat, NN) / den
            l_ref[lo:hi, :] = jnp.broadcast_to(m + jnp.log(den), (DIL_BAND, DIL_HD))

    return pl.pallas_call(
        body, name=name, grid=(steps,), in_specs=[main, main, main, prev, prev], out_specs=[main, main],
        out_shape=[jax.ShapeDtypeStruct((rows, DIL_HD), F32)] * 2, compiler_params=_params(("parallel",)),
    )(q, k, v, k, v)


def _dil_bwd(q, k, v, do, lse, dd, nb, name):
    rows = q.shape[0]
    blk = BANDS * DIL_BAND
    steps = rows // blk
    last_band = rows // DIL_BAND - 1
    main = pl.BlockSpec((blk, DIL_HD), lambda i: (i, 0))
    prev = pl.BlockSpec((DIL_BAND, DIL_HD), lambda i: (jnp.maximum(i * BANDS - 1, 0), 0))
    nxt = pl.BlockSpec((DIL_BAND, DIL_HD), lambda i: (jnp.minimum(i * BANDS + BANDS, last_band), 0))

    def body(q_ref, k_ref, v_ref, do_ref, l_ref, dd_ref, kp_ref, vp_ref, qn_ref, don_ref, ln_ref, ddn_ref,
             dq_ref, dk_ref, dv_ref, ak_ref, av_ref):
        i = pl.program_id(0)
        ak_ref[...] = jnp.zeros_like(ak_ref)
        av_ref[...] = jnp.zeros_like(av_ref)
        for j in range(BANDS + 1):
            lo, hi = j * DIL_BAND, (j + 1) * DIL_BAND
            if j == 0:
                kcat = jnp.concatenate([kp_ref[...], k_ref[lo:hi, :]], axis=0)
                vcat = jnp.concatenate([vp_ref[...], v_ref[lo:hi, :]], axis=0)
            elif j < BANDS:
                kcat = k_ref[lo - DIL_BAND:hi, :]
                vcat = v_ref[lo - DIL_BAND:hi, :]
            else:
                kcat = jnp.concatenate([k_ref[lo - DIL_BAND:lo, :], k_ref[lo - DIL_BAND:lo, :]], axis=0)
                vcat = jnp.concatenate([v_ref[lo - DIL_BAND:lo, :], v_ref[lo - DIL_BAND:lo, :]], axis=0)
            if j < BANDS:
                qj, doj, lj, ddj = q_ref[lo:hi, :], do_ref[lo:hi, :], l_ref[lo:hi, :], dd_ref[lo:hi, :]
            else:
                qj, doj, lj, ddj = qn_ref[...], don_ref[...], ln_ref[...], ddn_ref[...]
            not_first = (((i * BANDS + j) % nb) != 0).astype(jnp.int32)
            mask = _band_masks(not_first)
            if j == BANDS:
                cidx = lax.broadcasted_iota(jnp.int32, mask.shape, 1)
                mask = jnp.logical_and(mask, cidx < DIL_BAND)
            s = jnp.where(mask, _dot(qj, kcat, NT), NEG)
            p = jnp.exp(s - jnp.concatenate([lj, lj], axis=1))
            dp = _dot(doj, vcat, NT)
            ds = _bf(p * (dp - jnp.concatenate([ddj, ddj], axis=1)))
            if j < BANDS:
                dq_ref[lo:hi, :] = _dot(ds, kcat, NN)
            ak_ref[lo:hi + DIL_BAND, :] += _dot(ds, qj, TN)
            av_ref[lo:hi + DIL_BAND, :] += _dot(_bf(p), doj, TN)
        dk_ref[...] = ak_ref[DIL_BAND:DIL_BAND + blk, :]
        dv_ref[...] = av_ref[DIL_BAND:DIL_BAND + blk, :]

    return pl.pallas_call(
        body, name=name, grid=(steps,),
        in_specs=[main] * 6 + [prev, prev] + [nxt] * 4, out_specs=[main] * 3,
        out_shape=[jax.ShapeDtypeStruct((rows, DIL_HD), F32)] * 3,
        scratch_shapes=[pltpu.VMEM((blk + 2 * DIL_BAND, DIL_HD), F32)] * 2,
        compiler_params=_params(("parallel",)),
    )(q, k, v, do, lse, dd, k, v, q, do, lse, dd)


def _dil_merge(os_, ls_, tr=256):
    t, w = os_[0].shape
    blk = pl.BlockSpec((tr, w), lambda i: (i, 0))

    def body(o1, o2, o3, l1, l2, l3, ob_ref, of_ref, lt_ref):
        a, b, c = l1[...], l2[...], l3[...]
        m = jnp.maximum(jnp.maximum(a, b), c)
        ea, eb, ec = jnp.exp(a - m), jnp.exp(b - m), jnp.exp(c - m)
        den = ea + eb + ec
        o = (ea * o1[...] + eb * o2[...] + ec * o3[...]) / den
        ob_ref[...] = o.astype(BF16)
        of_ref[...] = o
        lt_ref[...] = m + jnp.log(den)

    return pl.pallas_call(
        body, name="dil_merge", grid=(t // tr,), in_specs=[blk] * 6, out_specs=[blk] * 3,
        out_shape=[jax.ShapeDtypeStruct((t, w), BF16), jax.ShapeDtypeStruct((t, w), F32),
                   jax.ShapeDtypeStruct((t, w), F32)],
        compiler_params=_params(("parallel",)),
    )(*os_, *ls_)


def _dil_bwd_prep(dmix, o_d, tr=256):
    t, w = o_d.shape
    blk = pl.BlockSpec((tr, w), lambda i: (i, 0))

    def body(do_ref, o_ref, dob_ref, dd_ref):
        do = do_ref[...]
        prod = do * o_ref[...]
        dob_ref[...] = do.astype(BF16)
        for h in range(DIL_HEADS):
            hs = slice(h * DIL_HD, (h + 1) * DIL_HD)
            dd_ref[:, hs] = jnp.broadcast_to(jnp.sum(prod[:, hs], axis=-1, keepdims=True), (tr, DIL_HD))

    return pl.pallas_call(
        body, name="dil_bwd_prep", grid=(t // tr,),
        in_specs=[pl.BlockSpec((tr, w), lambda i: (i, 1)), blk], out_specs=[blk, blk],
        out_shape=[jax.ShapeDtypeStruct((t, w), BF16), jax.ShapeDtypeStruct((t, w), F32)],
        compiler_params=_params(("parallel",)),
    )(dmix, o_d)


def _gather_rows(dst_ref, src_ref, t, d, cast=None):
    n = t // d
    for r in range(d):
        v = src_ref[pl.ds(r, n, stride=d), :] if d > 1 else src_ref[...]
        dst_ref[r * n:(r + 1) * n, :] = v if cast is None else v.astype(cast)


def _tri_mask():
    r = lax.broadcasted_iota(jnp.int32, (DIL_BAND, DIL_BAND), 0)
    c = lax.broadcasted_iota(jnp.int32, (DIL_BAND, DIL_BAND), 1)
    return c <= r


def _dil_fwd_all(qr, kr, h_b):
    t = qr.shape[0]
    nbands = t // DIL_BAND
    nbr = len(DIL_DILATIONS)
    hoff = DIL_HEADS

    def col(off):
        return pl.BlockSpec((t, DIL_HD), lambda h: (0, off + h), pipeline_mode=pl.Buffered(1))

    outb = pl.BlockSpec((t, DIL_HD), lambda h: (0, h))

    def body(q_ref, k_ref, v_ref, ob_ref, of_ref, lt_ref, qs, ks, vs, os_, ls_, *br):
        obr, lbr = br[:nbr], br[nbr:]
        for bi, d in enumerate(DIL_DILATIONS):
            n = t // d
            nb = n // DIL_BAND
            _gather_rows(qs, q_ref, t, d, BF16)
            _gather_rows(ks, k_ref, t, d, BF16)
            _gather_rows(vs, v_ref, t, d, BF16)
            s = jnp.where(_tri_mask(), _dot(qs[0:DIL_BAND, :], ks[0:DIL_BAND, :], NT), NEG)
            m = jnp.max(s, axis=-1, keepdims=True)
            pr = jnp.exp(s - m)
            den = jnp.sum(pr, axis=-1, keepdims=True)
            os_[0:DIL_BAND, :] = _dot(_bf(pr), vs[0:DIL_BAND, :], NN) / den
            ls_[0:DIL_BAND, :] = jnp.broadcast_to(m + jnp.log(den), (DIL_BAND, DIL_HD))

            def band(b, carry, nb=nb):
                st = pl.multiple_of((b - 1) * DIL_BAND, DIL_BAND)
                cur = pl.ds(st + DIL_BAND, DIL_BAND)
                both = pl.ds(st, 2 * DIL_BAND)
                not_first = ((b % nb) != 0).astype(jnp.int32)
                s = jnp.where(_band_masks(not_first), _dot(qs[cur, :], ks[both, :], NT), NEG)
                m = jnp.max(s, axis=-1, keepdims=True)
                pr = jnp.exp(s - m)
                den = jnp.sum(pr, axis=-1, keepdims=True)
                os_[cur, :] = _dot(_bf(pr), vs[both, :], NN) / den
                ls_[cur, :] = jnp.broadcast_to(m + jnp.log(den), (DIL_BAND, DIL_HD))
                return carry

            lax.fori_loop(1, nbands, band, 0, unroll=4)
            for r in range(d):
                dst = pl.ds(r, n, stride=d) if d > 1 else slice(None)
                obr[bi][dst, :] = os_[r * n:(r + 1) * n, :]
                lbr[bi][dst, :] = ls_[r * n:(r + 1) * n, :]
        rows = 512
        for c0 in range(0, t, rows):
            sl = slice(c0, c0 + rows)
            la, lb, lc = lbr[0][sl, :], lbr[1][sl, :], lbr[2][sl, :]
            m = jnp.maximum(jnp.maximum(la, lb), lc)
            ea, eb, ec = jnp.exp(la - m), jnp.exp(lb - m), jnp.exp(lc - m)
            den = ea + eb + ec
            o = (ea * obr[0][sl, :] + eb * obr[1][sl, :] + ec * obr[2][sl, :]) / den
            ob_ref[sl, :] = o.astype(BF16)
            of_ref[sl, :] = o
            lt_ref[sl, :] = m + jnp.log(den)

    w = DIL_HEADS * DIL_HD
    vm = lambda dt: pltpu.VMEM((t, DIL_HD), dt)
    return pl.pallas_call(
        body, name="dil_fwd", grid=(DIL_HEADS,), in_specs=[col(0), col(0), col(2 * hoff)],
        out_specs=[outb, outb, outb],
        out_shape=[jax.ShapeDtypeStruct((t, w), BF16), jax.ShapeDtypeStruct((t, w), F32),
                   jax.ShapeDtypeStruct((t, w), F32)],
        scratch_shapes=[vm(BF16)] * 3 + [vm(F32)] * 2 + [vm(F32)] * (2 * nbr),
        compiler_params=_params(("parallel",)),
    )(qr, kr, h_b)


def _dil_bwd_all(qr, kr, h_b, dmix, o_d, lse_tot):
    t = qr.shape[0]
    nbands = t // DIL_BAND
    hoff = DIL_HEADS

    def col(off):
        return pl.BlockSpec((t, DIL_HD), lambda h: (0, off + h), pipeline_mode=pl.Buffered(1))

    outb = pl.BlockSpec((t, DIL_HD), lambda h: (0, h))

    def body(q_ref, k_ref, v_ref, do_ref, o_ref, l_ref, dq_ref, dk_ref, dv_ref,
             qs, ks, vs, dos, lss, dds, dqs, acck, accv):
        for bi, d in enumerate(DIL_DILATIONS):
            n = t // d
            nb = n // DIL_BAND
            _gather_rows(qs, q_ref, t, d, BF16)
            _gather_rows(ks, k_ref, t, d, BF16)
            _gather_rows(vs, v_ref, t, d, BF16)
            _gather_rows(dos, do_ref, t, d, BF16)
            _gather_rows(lss, l_ref, t, d)
            for r in range(d):
                src = pl.ds(r, n, stride=d) if d > 1 else slice(None)
                prod = do_ref[src, :] * o_ref[src, :]
                dds[r * n:(r + 1) * n, :] = jnp.broadcast_to(jnp.sum(prod, axis=-1, keepdims=True), (n, DIL_HD))
            acck[...] = jnp.zeros_like(acck)
            accv[...] = jnp.zeros_like(accv)
            b0 = slice(0, DIL_BAND)
            s = jnp.where(_tri_mask(), _dot(qs[b0, :], ks[b0, :], NT), NEG)
            pr = jnp.exp(s - lss[b0, :])
            ds = _bf(pr * (_dot(dos[b0, :], vs[b0, :], NT) - dds[b0, :]))
            dqs[b0, :] = _dot(ds, ks[b0, :], NN)
            acck[DIL_BAND:2 * DIL_BAND, :] += _dot(ds, qs[b0, :], TN)
            accv[DIL_BAND:2 * DIL_BAND, :] += _dot(_bf(pr), dos[b0, :], TN)

            def band(b, carry, nb=nb):
                st = pl.multiple_of((b - 1) * DIL_BAND, DIL_BAND)
                cur = pl.ds(st + DIL_BAND, DIL_BAND)
                both = pl.ds(st, 2 * DIL_BAND)
                acc_rows = pl.ds(st + DIL_BAND, 2 * DIL_BAND)
                not_first = ((b % nb) != 0).astype(jnp.int32)
                qb, dob, lb, ddb = qs[cur, :], dos[cur, :], lss[cur, :], dds[cur, :]
                kcat, vcat = ks[both, :], vs[both, :]
                s = jnp.where(_band_masks(not_first), _dot(qb, kcat, NT), NEG)
                pr = jnp.exp(s - jnp.concatenate([lb, lb], axis=1))
                ds = _bf(pr * (_dot(dob, vcat, NT) - jnp.concatenate([ddb, ddb], axis=1)))
                dqs[cur, :] = _dot(ds, kcat, NN)
                acck[acc_rows, :] += _dot(ds, qb, TN)
                accv[acc_rows, :] += _dot(_bf(pr), dob, TN)
                return carry

            lax.fori_loop(1, nbands, band, 0, unroll=2)
            for r in range(d):
                lo = r * n
                if d == 1:
                    dq_ref[...] = dqs[...]
                    dk_ref[...] = acck[DIL_BAND:DIL_BAND + t, :]
                    dv_ref[...] = accv[DIL_BAND:DIL_BAND + t, :]
                else:
                    dst = pl.ds(r, n, stride=d)
                    dq_ref[dst, :] = dq_ref[dst, :] + dqs[lo:lo + n, :]
                    dk_ref[dst, :] = dk_ref[dst, :] + acck[DIL_BAND + lo:DIL_BAND + lo + n, :]
                    dv_ref[dst, :] = dv_ref[dst, :] + accv[DIL_BAND + lo:DIL_BAND + lo + n, :]

    w = DIL_HEADS * DIL_HD
    vm = lambda dt, extra=0: pltpu.VMEM((t + extra, DIL_HD), dt)
    return pl.pallas_call(
        body, name="dil_bwd", grid=(DIL_HEADS,),
        in_specs=[col(0), col(0), col(2 * hoff), col(hoff), col(0), col(0)], out_specs=[outb] * 3,
        out_shape=[jax.ShapeDtypeStruct((t, w), F32)] * 3,
        scratch_shapes=[vm(BF16)] * 4 + [vm(F32)] * 3 + [vm(F32, DIL_BAND)] * 2,
        compiler_params=_params(("parallel",)),
    )(qr, kr, h_b, dmix, o_d, lse_tot)


def _ca_fwd(q, memkv, tq=512):
    t, d = q.shape
    m = memkv.shape[0]
    scale = CA_HD ** -0.5

    def body(q_ref, k_ref, v_ref, o_ref, ot_ref):
        for h in range(CA_HEADS):
            hs = slice(h * CA_HD, (h + 1) * CA_HD)
            s = _dot(q_ref[:, hs], k_ref[:, hs], NT) * scale
            p = jnp.exp(s - jnp.max(s, axis=-1, keepdims=True))
            p = p / jnp.sum(p, axis=-1, keepdims=True)
            o = _dot(_bf(p), v_ref[:, hs], NN).astype(BF16)
            o_ref[:, hs] = o
            ot_ref[hs, :] = o.T

    return pl.pallas_call(
        body, name="ca_fwd", grid=(t // tq,),
        in_specs=[pl.BlockSpec((tq, d), lambda i: (i, 0)), pl.BlockSpec((m, d), lambda i: (0, 0)),
                  pl.BlockSpec((m, d), lambda i: (0, 1))],
        out_specs=[pl.BlockSpec((tq, d), lambda i: (i, 0)), pl.BlockSpec((d, tq), lambda i: (0, i))],
        out_shape=[jax.ShapeDtypeStruct((t, d), BF16), jax.ShapeDtypeStruct((d, t), BF16)],
        compiler_params=_params(("parallel",)),
    )(q, memkv, memkv)


def _ca_bwd(q, memkv, do, tq=512):
    t, d = q.shape
    m = memkv.shape[0]
    scale = CA_HD ** -0.5

    def body(q_ref, k_ref, v_ref, do_ref, dq_ref, dkv_ref):
        i = pl.program_id(0)

        @pl.when(i == 0)
        def _():
            dkv_ref[...] = jnp.zeros_like(dkv_ref)

        for h in range(CA_HEADS):
            hs = slice(h * CA_HD, (h + 1) * CA_HD)
            q_h, k_h, v_h, do_h = q_ref[:, hs], k_ref[:, hs], v_ref[:, hs], do_ref[:, hs]
            s = _dot(q_h, k_h, NT) * scale
            p = jnp.exp(s - jnp.max(s, axis=-1, keepdims=True))
            p = p / jnp.sum(p, axis=-1, keepdims=True)
            dp = _dot(do_h, v_h, NT)
            ds = _bf(p * (dp - jnp.sum(p * dp, axis=-1, keepdims=True)) * scale)
            dq_ref[:, hs] = _dot(ds, k_h, NN).astype(BF16)
            dkv_ref[:, hs] += _dot(ds, q_h, TN)
            dkv_ref[:, d + h * CA_HD: d + (h + 1) * CA_HD] += _dot(_bf(p), do_h, TN)

    return pl.pallas_call(
        body, name="ca_bwd", grid=(t // tq,),
        in_specs=[pl.BlockSpec((tq, d), lambda i: (i, 0)), pl.BlockSpec((m, d), lambda i: (0, 0)),
                  pl.BlockSpec((m, d), lambda i: (0, 1)), pl.BlockSpec((tq, d), lambda i: (i, 0))],
        out_specs=[pl.BlockSpec((tq, d), lambda i: (i, 0)), pl.BlockSpec((m, 2 * d), lambda i: (0, 0))],
        out_shape=[jax.ShapeDtypeStruct((t, d), BF16), jax.ShapeDtypeStruct((m, 2 * d), F32)],
        compiler_params=_params(("arbitrary",)),
    )(q, memkv, memkv, do)


STRIP = 256


def _shift_down(u, n, row):
    return jnp.where(row >= n, pltpu.roll(u, n, 0), 0.0)


def _shift_up(u, n, row):
    t = u.shape[0]
    return jnp.where(row < t - n, pltpu.roll(u, t - n, 0), 0.0)


def _conv(u, cw_ref, row):
    return ((cw_ref[3:4, :] + cw_ref[0:1, :] * _shift_down(u, 2, row)) + cw_ref[1:2, :] * _shift_down(u, 1, row)) \
        + cw_ref[2:3, :] * u


def _swiglu_fwd(u0, cw):
    t, w = u0.shape[0], u0.shape[1] // 2
    ns = w // STRIP
    col = pl.BlockSpec((t, STRIP), lambda j: (0, j))
    col_up = pl.BlockSpec((t, STRIP), lambda j: (0, ns + j))
    cws = pl.BlockSpec((SUBLANES, STRIP), lambda j: (0, j))
    cws_up = pl.BlockSpec((SUBLANES, STRIP), lambda j: (0, ns + j))

    def body(g_ref, u_ref, cg_ref, cu_ref, a_ref, at_ref):
        row = lax.broadcasted_iota(jnp.int32, (t, STRIP), 0)
        gate = _conv(g_ref[...].astype(F32), cg_ref, row)
        up = _conv(u_ref[...].astype(F32), cu_ref, row)
        act = (gate * _sigmoid(gate) * up).astype(BF16)
        a_ref[...] = act
        at_ref[...] = act.T

    return pl.pallas_call(
        body, name="swiglu_fwd", grid=(ns,), in_specs=[col, col_up, cws, cws_up],
        out_specs=[col, pl.BlockSpec((STRIP, t), lambda j: (j, 0))],
        out_shape=[jax.ShapeDtypeStruct((t, w), BF16), jax.ShapeDtypeStruct((w, t), BF16)],
        compiler_params=_params(("parallel",)),
    )(u0, u0, cw, cw)


def _swiglu_bwd(u0, cw, da):
    t, w = u0.shape[0], u0.shape[1] // 2
    ns = w // STRIP
    col = pl.BlockSpec((t, STRIP), lambda j: (0, j))
    col_up = pl.BlockSpec((t, STRIP), lambda j: (0, ns + j))
    cws = pl.BlockSpec((SUBLANES, STRIP), lambda j: (0, j))
    cws_up = pl.BlockSpec((SUBLANES, STRIP), lambda j: (0, ns + j))

    def conv_bwd(du, u0, cw_ref, row, du0_ref, du0t_ref, dcw_ref):
        du0 = (cw_ref[2:3, :] * du + cw_ref[1:2, :] * _shift_up(du, 1, row)) + cw_ref[0:1, :] * _shift_up(du, 2, row)
        du0 = du0.astype(BF16)
        du0_ref[...] = du0
        du0t_ref[...] = du0.T
        dcw_ref[0:1, :] = jnp.sum(du * _shift_down(u0, 2, row), axis=0, keepdims=True)
        dcw_ref[1:2, :] = jnp.sum(du * _shift_down(u0, 1, row), axis=0, keepdims=True)
        dcw_ref[2:3, :] = jnp.sum(du * u0, axis=0, keepdims=True)
        dcw_ref[3:4, :] = jnp.sum(du, axis=0, keepdims=True)
        dcw_ref[4:8, :] = jnp.zeros((4, STRIP), F32)

    def body(g_ref, u_ref, cg_ref, cu_ref, da_ref, dg0_ref, du0_ref, dg0t_ref, du0t_ref, dcg_ref, dcu_ref):
        row = lax.broadcasted_iota(jnp.int32, (t, STRIP), 0)
        g0, up0 = g_ref[...].astype(F32), u_ref[...].astype(F32)
        gate = _conv(g0, cg_ref, row)
        up = _conv(up0, cu_ref, row)
        sg = _sigmoid(gate)
        da = da_ref[...].astype(F32)
        dgate = da * up * (sg * (1.0 + gate * (1.0 - sg)))
        dup = da * (gate * sg)
        conv_bwd(dgate, g0, cg_ref, row, dg0_ref, dg0t_ref, dcg_ref)
        conv_bwd(dup, up0, cu_ref, row, du0_ref, du0t_ref, dcu_ref)

    colt = pl.BlockSpec((STRIP, t), lambda j: (j, 0))
    return pl.pallas_call(
        body, name="swiglu_bwd", grid=(ns,), in_specs=[col, col_up, cws, cws_up, col],
        out_specs=[col, col, colt, colt, cws, cws],
        out_shape=[jax.ShapeDtypeStruct((t, w), BF16), jax.ShapeDtypeStruct((t, w), BF16),
                   jax.ShapeDtypeStruct((w, t), BF16), jax.ShapeDtypeStruct((w, t), BF16),
                   jax.ShapeDtypeStruct((SUBLANES, w), F32), jax.ShapeDtypeStruct((SUBLANES, w), F32)],
        compiler_params=_params(("parallel",)),
    )(u0, u0, cw, cw, da)


def _ffn_win_grad(dugt, duut, x2b, tn=512):
    t, d = x2b.shape
    tk = t // 2
    nk = t // tk
    sp, sw = FF_SLAB_P, FF_SLAB

    def body(ag_ref, au_ref, b_ref, o_ref, ob_ref, acc_ref):
        j, kk = pl.program_id(0), pl.program_id(2)

        @pl.when(kk == 0)
        def _():
            acc_ref[...] = jnp.zeros_like(acc_ref)

        @pl.when(j < 4)
        def _():
            acc_ref[...] += _dot(ag_ref[...], b_ref[...], NN)

        @pl.when(j >= 4)
        def _():
            acc_ref[...] += _dot(au_ref[...], b_ref[...], NN)

        @pl.when(kk == nk - 1)
        def _():
            o_ref[...] = acc_ref[:sw, :]
            ob_ref[...] = acc_ref[:sw, :].astype(BF16)

    o_spec = pl.BlockSpec((None, sw, tn), lambda j, n, kk: (j, 0, n))
    return pl.pallas_call(
        body, name="mm_g_ffn_in", grid=(8, d // tn, nk),
        in_specs=[pl.BlockSpec((sp, tk), lambda j, n, kk: (jnp.minimum(j, 3), kk)),
                  pl.BlockSpec((sp, tk), lambda j, n, kk: (jnp.maximum(j - 4, 0), kk)),
                  pl.BlockSpec((tk, tn), lambda j, n, kk: (kk, n))],
        out_specs=[o_spec, o_spec],
        out_shape=[jax.ShapeDtypeStruct((8, sw, d), F32), jax.ShapeDtypeStruct((8, sw, d), BF16)],
        scratch_shapes=[pltpu.VMEM((sp, tn), F32)],
        compiler_params=_params(("parallel", "parallel", "arbitrary")),
    )(dugt, duut, x2b)


def _tile2d(r, c, limit=1 << 20):
    tr, tc = r, c
    while tr * tc * 4 > limit:
        if tr % (2 * SUBLANES) == 0:
            tr //= 2
        elif tc % (2 * LANES) == 0:
            tc //= 2
        else:
            break
    return tr, tc


def _adamw_math(w, m, v, g):
    c1 = 1.0 - ADAM_B1 ** ADAM_STEP
    c2 = 1.0 - ADAM_B2 ** ADAM_STEP
    mm = ADAM_B1 * m + (1.0 - ADAM_B1) * g
    vv = ADAM_B2 * v + (1.0 - ADAM_B2) * (g * g)
    delta = -ADAM_LR * ((mm / c1) / (jnp.sqrt(vv / c2) + ADAM_EPS) + ADAM_WD * w)
    return delta, mm, vv


def _adamw(w, m, v, g, name):
    r, c = w.shape
    blk = pl.BlockSpec((r, c), lambda i: (0, 0))

    def body(w_ref, m_ref, v_ref, gi_ref, g_ref, d_ref, nm_ref, nv_ref):
        g = gi_ref[...]
        d_ref[...], nm_ref[...], nv_ref[...] = _adamw_math(w_ref[...], m_ref[...], v_ref[...], g)
        g_ref[...] = g

    return pl.pallas_call(body, name=name, grid=(1,), in_specs=[blk] * 4, out_specs=[blk] * 4,
                          out_shape=[jax.ShapeDtypeStruct((r, c), F32)] * 4,
                          compiler_params=_params(("arbitrary",)))(w, m, v, g)


def _pair_add(gs, ra, core, name):
    _, _, r, c = gs.shape
    tr, tc = _tile2d(r, c)
    blk = pl.BlockSpec((None, tr, tc), lambda k, i, j, s: (k, i, j))

    def body(s_ref, g_ref, r_ref, o_ref, ob_ref):
        p = g_ref[...] + r_ref[...]
        o_ref[...] = p
        ob_ref[...] = p.astype(BF16)

    gspec = pltpu.PrefetchScalarGridSpec(
        num_scalar_prefetch=1, grid=(4, r // tr, c // tc),
        in_specs=[pl.BlockSpec((None, None, tr, tc), lambda k, i, j, s: (k, s[0], i, j)), blk], out_specs=[blk, blk])
    return pl.pallas_call(body, name=name, grid_spec=gspec,
                          out_shape=[jax.ShapeDtypeStruct((4, r, c), F32), jax.ShapeDtypeStruct((4, r, c), BF16)],
                          compiler_params=_params(("parallel", "parallel", "parallel")))(core, gs, ra)


def _small_reduce(gathered):
    nd, r, n = gathered.shape
    tn = 2048 if n % 2048 == 0 else n
    def body(g_ref, s_ref, t_ref):
        s = g_ref[0]
        for dv in range(1, nd):
            s = s + g_ref[dv]
        s_ref[...] = s
        t_ref[...] = jnp.broadcast_to(jnp.sum(s, axis=0, keepdims=True), (r, tn))

    return pl.pallas_call(
        body, name="small_reduce", grid=(n // tn,),
        in_specs=[pl.BlockSpec((nd, r, tn), lambda j: (0, 0, j))],
        out_specs=[pl.BlockSpec((r, tn), lambda j: (0, j))] * 2,
        out_shape=[jax.ShapeDtypeStruct((r, n), F32)] * 2, compiler_params=_params(("parallel",)),
    )(gathered)


HBM = pl.BlockSpec(memory_space=pltpu.HBM)


def _all_gather(arrs, name):
    n = len(arrs)

    def body(*refs):
        ins, outs = refs[:n], refs[n:2 * n]
        send, recv, lsem = refs[2 * n:]
        x, y, c = lax.axis_index("x"), lax.axis_index("y"), lax.axis_index("c")
        me, sib = (x, y, c), (x, y, 1 - c)
        chips = [(1 - x, y), (x, 1 - y), (1 - x, 1 - y)]

        def slot(w, p):
            return outs[w].at[4 * p[0] + 2 * p[1] + p[2]]

        def cp(w, k, block, to, src=None):
            return pltpu.make_async_remote_copy(
                src_ref=slot(w, block) if src is None else src, dst_ref=slot(w, block),
                send_sem=send.at[w * 7 + k], recv_sem=recv.at[w * 7 + k], device_id=to, device_id_type=MESH)

        mine = [pltpu.make_async_copy(ins[w], slot(w, me), lsem.at[w]) for w in range(n)]
        for m in mine:
            m.start()
        first = []
        for w in range(n):
            first.append(cp(w, 0, me, sib, src=ins[w]))
            first += [cp(w, 1 + j, me, (*chip, c), src=ins[w]) for j, chip in enumerate(chips)]
        for f in first:
            f.start()
        passed = []
        for j, chip in enumerate(chips):
            for w in range(n):
                cp(w, 1 + j, (*chip, c), me).wait_recv()
                fwd = cp(w, 4 + j, (*chip, c), sib)
                fwd.start()
                passed.append(fwd)
        for w in range(n):
            cp(w, 0, sib, me).wait_recv()
            for j, chip in enumerate(chips):
                cp(w, 4 + j, (*chip, 1 - c), me).wait_recv()
        for f in first + passed:
            f.wait_send()
        for m in mine:
            m.wait()

    return pl.pallas_call(
        body, name=name, in_specs=[HBM] * n, out_specs=[HBM] * n,
        out_shape=[jax.ShapeDtypeStruct((8,) + a.shape, a.dtype) for a in arrs],
        scratch_shapes=[pltpu.SemaphoreType.DMA((7 * n,)), pltpu.SemaphoreType.DMA((7 * n,)),
                        pltpu.SemaphoreType.DMA((n,))],
    )(*arrs)


def _sibling_exchange(arrs, name):
    n = len(arrs)

    def body(*refs):
        ins, outs = refs[:n], refs[n:2 * n]
        send, recv = refs[2 * n:]
        x, y, c = lax.axis_index("x"), lax.axis_index("y"), lax.axis_index("c")
        copies = [pltpu.make_async_remote_copy(
            src_ref=ins[w].at[:, 1 - c], dst_ref=outs[w], send_sem=send.at[w], recv_sem=recv.at[w],
            device_id=(x, y, 1 - c), device_id_type=MESH) for w in range(n)]
        for cpy in copies:
            cpy.start()
        for cpy in copies:
            cpy.wait()

    return pl.pallas_call(
        body, name=name, in_specs=[HBM] * n, out_specs=[HBM] * n,
        out_shape=[jax.ShapeDtypeStruct((a.shape[0],) + a.shape[2:], a.dtype) for a in arrs],
        scratch_shapes=[pltpu.SemaphoreType.DMA((n,)), pltpu.SemaphoreType.DMA((n,))],
    )(*arrs)


def _chip_exchange(arrs, name):
    n = len(arrs)

    def body(*refs):
        ins, outs = refs[:n], refs[n:2 * n]
        send, recv = refs[2 * n:]
        x, y, c = lax.axis_index("x"), lax.axis_index("y"), lax.axis_index("c")
        chips = [(1 - x, y), (x, 1 - y), (1 - x, 1 - y)]
        copies = []
        for w in range(n):
            for j, (cx, cy) in enumerate(chips):
                copies.append(pltpu.make_async_remote_copy(
                    src_ref=ins[w].at[2 * cx + cy], dst_ref=outs[w].at[j], send_sem=send.at[3 * w + j],
                    recv_sem=recv.at[3 * w + j], device_id=(cx, cy, c), device_id_type=MESH))
        for cpy in copies:
            cpy.start()
        for cpy in copies:
            cpy.wait()

    return pl.pallas_call(
        body, name=name, in_specs=[HBM] * n, out_specs=[HBM] * n,
        out_shape=[jax.ShapeDtypeStruct((3,) + a.shape[1:], a.dtype) for a in arrs],
        scratch_shapes=[pltpu.SemaphoreType.DMA((3 * n,)), pltpu.SemaphoreType.DMA((3 * n,))],
    )(*arrs)


SEM = pl.BlockSpec(memory_space=pltpu.SEMAPHORE)
ANY = pl.BlockSpec(memory_space=pl.ANY)
EFFECT = pltpu.SideEffectType.DATAFLOW_SIDE_EFFECTING
N_PEERS = 7


def _peers(x, y, c):
    return [((1 - x) if k & 4 else x, (1 - y) if k & 2 else y, (1 - c) if k & 1 else c) for k in range(1, 8)]


def _spread_copies(src_refs, land_refs, send, recv, gather):
    x, y, c = lax.axis_index("x"), lax.axis_index("y"), lax.axis_index("c")
    me = 4 * x + 2 * y + c
    copies = []
    for w in range(len(src_refs)):
        for k, (px, py, pc) in enumerate(_peers(x, y, c)):
            p = 4 * px + 2 * py + pc
            copies.append((pltpu.make_async_remote_copy(
                src_ref=src_refs[w] if gather else src_refs[w].at[p], dst_ref=land_refs[w].at[me],
                send_sem=send[w].at[k], recv_sem=recv[w].at[k], device_id=(px, py, pc), device_id_type=MESH),
                pltpu.make_async_remote_copy(
                src_ref=src_refs[w] if gather else src_refs[w].at[p], dst_ref=land_refs[w].at[p],
                send_sem=send[w].at[k], recv_sem=recv[w].at[k], device_id=(px, py, pc), device_id_type=MESH)))
    return copies


def _hbm(a):
    return pltpu.with_memory_space_constraint(a, pltpu.HBM)


def _spread_start(srcs, lands, after, gather, name):
    n = len(srcs)

    def body(*refs):
        src_refs, land_refs = refs[:n], refs[n:2 * n]
        outs = refs[2 * n + 1:]
        send, recv, token = outs[:n], outs[n:2 * n], outs[4 * n]
        for start, _ in _spread_copies(src_refs, land_refs, send, recv, gather):
            start.start()
        token[...] = jnp.zeros_like(token)

    res = pl.pallas_call(
        body, name=name,
        out_shape=tuple([pltpu.SemaphoreType.DMA((N_PEERS,))] * (2 * n)
                        + [pltpu.HBM(a.shape, a.dtype) for a in srcs] + [pltpu.HBM(a.shape, a.dtype) for a in lands]
                        + [jax.ShapeDtypeStruct((SUBLANES, LANES), F32)]),
        in_specs=[HBM] * (2 * n) + [ANY],
        out_specs=tuple([SEM] * (2 * n) + [HBM] * (2 * n) + [pl.BlockSpec(memory_space=pltpu.VMEM)]),
        input_output_aliases={i: 2 * n + i for i in range(2 * n)},
        compiler_params=pltpu.CompilerParams(has_side_effects=EFFECT),
    )(*[_hbm(a) for a in srcs], *[_hbm(a) for a in lands], after)
    return res[:n], res[n:2 * n], res[2 * n:3 * n], res[3 * n:4 * n], res[4 * n]


def _spread_wait(send, recv, srcs, lands, after, gather, name):
    n = len(srcs)

    def body(*refs):
        src_refs, land_refs = refs[:n], refs[n:2 * n]
        send_refs, recv_refs = refs[2 * n:3 * n], refs[3 * n:4 * n]
        for _, arrive in _spread_copies(src_refs, land_refs, send_refs, recv_refs, gather):
            arrive.wait_send()
            arrive.wait_recv()

    res = pl.pallas_call(
        body, name=name,
        out_shape=tuple([pltpu.HBM(a.shape, a.dtype) for a in srcs] + [pltpu.HBM(a.shape, a.dtype) for a in lands]),
        in_specs=[HBM] * (2 * n) + [SEM] * (2 * n) + [ANY],
        out_specs=tuple([HBM] * (2 * n)),
        input_output_aliases={i: i for i in range(2 * n)},
        compiler_params=pltpu.CompilerParams(has_side_effects=EFFECT),
    )(*srcs, *lands, *send, *recv, after)
    return res[n:]


def _landing(shape, dtype, own, me):
    return lax.dynamic_update_index_in_dim(lax.empty((8,) + shape, dtype), own, me, 0)


def _pad_cols(a, to):
    return jnp.pad(a, ((0, 0), (0, to - a.shape[1])))


N_GLR = GLA_W + GLA_RANK
FF_SLAB = D_FF // 4
FF_SLAB_P = FFP // 4


TRANSPOSED = ("w_in", "ffn_w_in")


def _prepare_sub1(gath):
    w_in_t = gath["w_in"].reshape(-1, gath["w_in"].shape[2])
    w2 = jnp.concatenate([gath["gla_gate_w2"][s] for s in range(8)], axis=1)
    return {"w_a_t": jnp.pad(w_in_t[:N_GLR], ((0, HA_W - N_GLR), (0, 0))), "w_b_t": w_in_t[N_GLR:],
            "w2p": jnp.pad(w2, ((0, LANES - GLA_RANK), (0, 0)))}


def _prepare_ffn_in(g):
    f = jnp.pad(g, ((0, 0), (0, FF_SLAB_P - FF_SLAB), (0, 0)))
    return f.reshape(2 * FFP, f.shape[2])


def _prepare_ffn_out(g):
    return jnp.pad(g.reshape(4, FF_SLAB, -1), ((0, 0), (0, FF_SLAB_P - FF_SLAB), (0, 0))).reshape(FFP, -1)


def _prepare_conv(g, conv_b):
    padc = FF_SLAB_P - FF_SLAB
    cw = jnp.pad(g, ((0, 0), (0, 0), (0, padc)))
    cb = jnp.pad(conv_b.reshape(8, 1, FF_SLAB), ((0, 0), (0, 0), (0, padc)))
    rows = jnp.concatenate([cw, cb, jnp.zeros((8, 4, FF_SLAB_P), F32)], axis=1)
    return jnp.concatenate([rows[s] for s in range(8)], axis=1)


def _prepare_ffn(gath, conv_b):
    return {"w_ffn_t": _prepare_ffn_in(gath["ffn_w_in"]), "wo": _prepare_ffn_out(gath["ffn_w_out"]),
            "cw": _prepare_conv(gath["ffn_conv_w"], conv_b)}


def _unpad_ff(a):
    r = a.shape[0]
    return a.reshape(r, 4, FF_SLAB_P)[:, :, :FF_SLAB].reshape(r, D_FF)


def _grad_slabs(g):
    w_in_t = jnp.concatenate([g["w_a_t"][:N_GLR], g["w_b_t"]], axis=0)
    s = {"w_in": w_in_t.reshape(4, 2, w_in_t.shape[0] // 8, w_in_t.shape[1])}
    for n in ("w_out", "ca_wq", "ca_wo"):
        s[n] = _to_slabs(n, g[n])
    for n in ("ca_wkv", "ffn_w_in"):
        s[n] = g[n].reshape((4, 2) + g[n].shape[1:])
    wo = g["wo"].reshape(4, FF_SLAB_P, -1)[:, :FF_SLAB]
    s["ffn_w_out"] = wo.reshape(4, 2, FF_SLAB // 2, wo.shape[-1])
    return s


class _AtHand:
    def __init__(self, p):
        self.p = p
        self.token = None

    def sub2(self, after):
        return self.p

    def ffn_in(self, after):
        return self.p["w_ffn_t"]

    def ffn_out(self, after):
        return self.p["wo"]

    def grads_out(self, group, slabs):
        pass


def _local_step(x, mem, positions, target, p, small, stages=None):
    t, d = x.shape
    stages = _AtHand(p) if stages is None else stages
    w_a_t, w_b_t, w2p, cw = p["w_a_t"], p["w_b_t"], p["w2p"], p["cw"]
    tabs = _rope_tables(positions)
    xb = x.astype(BF16) if stages.token is None else (x + stages.token[0, 0]).astype(BF16)
    memb = mem.astype(BF16)

    h_a = _matmul(xb, w_a_t, "nt", F32, 512, 640, d, "mm_h_a")
    h_b = _matmul(xb, w_b_t, "nt", F32, 512, 1024, d, "mm_h_b")
    o_g, o_raw, s_before = _gla_fwd(h_a, w2p, small["gla_gate_b"], small["gla_norm_g"])
    qr, kr = _rope_fwd(h_b, tabs)
    o_d_b, o_d, lse_tot = _dil_fwd_all(qr, kr, h_b)
    mixin = jnp.concatenate([o_g, o_d_b], axis=1)
    wts = stages.sub2(mixin)
    mix = _matmul(mixin, wts["w_out"], "nn", F32, 512, 1024, d, "mm_mix")
    x1, x1b, x1t = _ln_fwd(x, mix, small["ln1_g"], small["ln1_b"], "ln1_fwd")

    q_ca = _matmul(x1b, wts["ca_wq"], "nn", BF16, 512, 1024, d, "mm_caq")
    kvw = wts["ca_wkv"].shape[2]
    memkv = _matmul(memb, wts["ca_wkv"], "nn", BF16, mem.shape[0], kvw, d, "mm_memkv", b_slabs=True)
    o_c, o_ct = _ca_fwd(q_ca, memkv)
    ca_out = _matmul(o_c, wts["ca_wo"], "nn", F32, 512, 1024, d, "mm_cao")
    x2, x2b, x2t = _ln_fwd(x1, ca_out, small["ln2_g"], small["ln2_b"], "ln2_fwd")

    w_ffn_t = stages.ffn_in(x2b)
    u0 = _matmul(x2b, w_ffn_t, "nt", BF16, 512, 512, d, "mm_u0")
    act, act_t = _swiglu_fwd(u0, cw)
    wo = stages.ffn_out(act)
    ffn = _matmul(act, wo, "nn", F32, 512, 512, FFP, "mm_ffn")

    dp3, dp3b, dg3, db3, loss_part = _ln_bwd(x2, ffn, small["ln3_g"], small["ln3_b"], target, True, "ln3_bwd")
    g_wo, g_wo16 = _matmul(act_t, dp3b, "nn", F32, 512, 1024, t // 2, "mm_g_wo", also_bf16=True)
    dact = _matmul(dp3b, wo, "nt", BF16, 512, 512, d, "mm_dact")
    dug, duu, dug_t, duu_t, dcwg, dcwu = _swiglu_bwd(u0, cw, dact)
    g_ffn_in, g_ffn_in16 = _ffn_win_grad(dug_t, duu_t, x2b)

    def wo_slabs(a):
        a = a.reshape(4, FF_SLAB_P, -1)[:, :FF_SLAB]
        return a.reshape(8, FF_SLAB // 2, a.shape[-1])

    sent = stages.grads_out("ffn", {"ffn_w_out": (wo_slabs(g_wo), wo_slabs(g_wo16)), "ffn_w_in": (g_ffn_in, g_ffn_in16)})
    dx2 = _matmul(dug, w_ffn_t, "nn", F32, 512, 512, FFP // 2, "mm_dx2_g", resid=dp3, resid_scale=ALPHA, dep=sent)
    dx2 = _matmul(duu, w_ffn_t, "nn", F32, 512, 512, FFP // 2, "mm_dx2_u", resid=dx2, b_k_off=2)

    dp2, dp2b, dg2, db2 = _ln_bwd(x1, ca_out, small["ln2_g"], small["ln2_b"], dx2, False, "ln2_bwd")
    g_cao, g_cao16 = _matmul(o_ct, dp2b, "nn", F32, 512, 1024, t // 2, "mm_g_cao", also_bf16=True)
    do_c = _matmul(dp2b, wts["ca_wo"], "nt", BF16, 512, 1024, d, "mm_do_c")
    dq_ca, dmemkv = _ca_bwd(q_ca, memkv, do_c)
    g_caq, g_caq16 = _matmul(x1t, dq_ca, "nn", F32, 512, 1024, t // 2, "mm_g_caq", also_bf16=True)
    g_cakv, g_cakv16 = _matmul(memb, dmemkv.astype(BF16), "tn", F32, 512, kvw, mem.shape[0], "mm_g_cakv",
                               out_slabs=True, also_bf16=True)
    dx1 = _matmul(dq_ca, wts["ca_wq"], "nt", F32, 512, 1024, d, "mm_dx1", resid=dp2, resid_scale=ALPHA)

    dp1, dp1b, dg1, db1 = _ln_bwd(x, mix, small["ln1_g"], small["ln1_b"], dx1, False, "ln1_bwd")
    g_wout, g_wout16 = _matmul(mixin, dp1b, "tn", F32, 512, 1024, 1024, "mm_g_wout", also_bf16=True)

    def row_slabs(a):
        return a.reshape(8, a.shape[0] // 8, a.shape[1])

    sent = stages.grads_out("attn", {"ca_wo": (row_slabs(g_cao), row_slabs(g_cao16)),
                                     "ca_wq": (row_slabs(g_caq), row_slabs(g_caq16)), "ca_wkv": (g_cakv, g_cakv16),
                                     "w_out": (row_slabs(g_wout), row_slabs(g_wout16))})
    dmix = _matmul(dp1b, wts["w_out"], "nt", F32, 512, 1024, d, "mm_dmix", dep=sent)
    dh_a, dw2, dgate_b, dnorm_g = _gla_bwd(h_a, w2p, small["gla_gate_b"], small["gla_norm_g"], o_raw, s_before, dmix)
    dq_d, dk_d, dv_d = _dil_bwd_all(qr, kr, h_b, dmix, o_d, lse_tot)
    dh_b = _dil_dh(dq_d, dk_d, dv_d, tabs)
    g_wa_t = _matmul(dh_a, xb, "tn", F32, 640, 1024, 1024, "mm_g_wa")
    g_wb_t = _matmul(dh_b, xb, "tn", F32, 512, 1024, 1024, "mm_g_wb")
    dx = _matmul(dh_a, w_a_t, "nn", F32, 512, 512, HA_W, "mm_dx_a", resid=dp1, resid_scale=ALPHA)
    dx = _matmul(dh_b, w_b_t, "nn", F32, 512, 512, HB_W, "mm_dx_b", resid=dx)

    grads = {"w_a_t": g_wa_t, "w_b_t": g_wb_t, "w_out": g_wout, "ca_wq": g_caq, "ca_wkv": g_cakv, "ca_wo": g_cao,
             "ffn_w_in": g_ffn_in, "wo": g_wo}
    small_parts = {
        "gla_gate_b": dgate_b, "gla_norm_g": dnorm_g, "ln1_g": dg1, "ln1_b": db1, "ln2_g": dg2, "ln2_b": db2,
        "ln3_g": dg3, "ln3_b": db3,
        "conv": jnp.concatenate([_unpad_ff(dcwg), _unpad_ff(dcwu)], axis=1),
        "gla_gate_w2": dw2[:GLA_RANK],
    }
    return loss_part, dx, grads, small_parts


BIG = ("w_in", "w_out", "ca_wq", "ca_wkv", "ca_wo", "ffn_w_in", "ffn_w_out")
COL_SHARDED = ("w_in", "ca_wkv", "ffn_w_in")
SMALL_ORDER = ("gla_gate_b", "gla_norm_g", "ln1_g", "ln1_b", "ln2_g", "ln2_b", "ln3_g", "ln3_b")


def _gathered_full(name, g):
    if name in COL_SHARDED:
        return g.transpose(1, 0, 2).reshape(g.shape[1], 8 * g.shape[2])
    return g.reshape(8 * g.shape[1], g.shape[2])


def _to_slabs(name, full):
    if name in COL_SHARDED:
        r, cc = full.shape
        s = full.reshape(r, 8, cc // 8).transpose(1, 0, 2)
    else:
        rr, c = full.shape
        s = full.reshape(8, rr // 8, c)
    return s.reshape((4, 2) + s.shape[1:])


def kernel(x, mem, positions, w_in, gla_gate_w2, gla_gate_b, gla_norm_g, w_out, ln1_g, ln1_b, ca_wq, ca_wkv, ca_wo, ln2_g, ln2_b, ffn_w_in, ffn_conv_w, ffn_conv_b, ffn_w_out, ln3_g, ln3_b, loss_target, m_w_in, m_gla_gate_w2, m_gla_gate_b, m_gla_norm_g, m_w_out, m_ln1_g, m_ln1_b, m_ca_wq, m_ca_wkv, m_ca_wo, m_ln2_g, m_ln2_b, m_ffn_w_in, m_ffn_conv_w, m_ffn_conv_b, m_ffn_w_out, m_ln3_g, m_ln3_b, v_w_in, v_gla_gate_w2, v_gla_gate_b, v_gla_norm_g, v_w_out, v_ln1_g, v_ln1_b, v_ca_wq, v_ca_wkv, v_ca_wo, v_ln2_g, v_ln2_b, v_ffn_w_in, v_ffn_conv_w, v_ffn_conv_b, v_ffn_w_out, v_ln3_g, v_ln3_b):
    weights = dict(w_in=w_in, gla_gate_w2=gla_gate_w2, gla_gate_b=gla_gate_b, gla_norm_g=gla_norm_g, w_out=w_out,
                   ln1_g=ln1_g, ln1_b=ln1_b, ca_wq=ca_wq, ca_wkv=ca_wkv, ca_wo=ca_wo, ln2_g=ln2_g, ln2_b=ln2_b,
                   ffn_w_in=ffn_w_in, ffn_conv_w=ffn_conv_w, ffn_conv_b=ffn_conv_b, ffn_w_out=ffn_w_out,
                   ln3_g=ln3_g, ln3_b=ln3_b)
    moms = dict(w_in=(m_w_in, v_w_in), gla_gate_w2=(m_gla_gate_w2, v_gla_gate_w2), gla_gate_b=(m_gla_gate_b, v_gla_gate_b),
                gla_norm_g=(m_gla_norm_g, v_gla_norm_g), w_out=(m_w_out, v_w_out), ln1_g=(m_ln1_g, v_ln1_g),
                ln1_b=(m_ln1_b, v_ln1_b), ca_wq=(m_ca_wq, v_ca_wq), ca_wkv=(m_ca_wkv, v_ca_wkv), ca_wo=(m_ca_wo, v_ca_wo),
                ln2_g=(m_ln2_g, v_ln2_g), ln2_b=(m_ln2_b, v_ln2_b), ffn_w_in=(m_ffn_w_in, v_ffn_w_in),
                ffn_conv_w=(m_ffn_conv_w, v_ffn_conv_w), ffn_conv_b=(m_ffn_conv_b, v_ffn_conv_b),
                ffn_w_out=(m_ffn_w_out, v_ffn_w_out), ln3_g=(m_ln3_g, v_ln3_g), ln3_b=(m_ln3_b, v_ln3_b))
    order = list(weights)
    xi, yi, ci = lax.axis_index("x"), lax.axis_index("y"), lax.axis_index("c")
    me = 4 * xi + 2 * yi + ci

    def travel(n, a):
        return jnp.swapaxes(a, 1, 2) if n in TRANSPOSED else a

    shard = {n: travel(n, weights[n]).astype(BF16)[0] for n in BIG}
    first = _all_gather([shard["w_in"], gla_gate_w2.astype(BF16)[0], ffn_conv_w[0]], "ag_first")
    p = _prepare_sub1({"w_in": first[0], "gla_gate_w2": first[1]})
    p["cw"] = _prepare_conv(first[2], ffn_conv_b)
    later = ("w_out", "ca_wq", "ca_wkv", "ca_wo", "ffn_w_in", "ffn_w_out")
    srcs = [shard[n] for n in later]
    lands = [_landing(shard[n].shape, BF16, shard[n], me) for n in later]
    send, recv, srcs, lands, token = _spread_start(srcs, lands, first[0], True, "ag_rest_start")

    class stages:
        pass

    stages.token = token

    def arrived(lo, hi, after, name):
        return _spread_wait(send[lo:hi], recv[lo:hi], srcs[lo:hi], lands[lo:hi], after, True, name)

    def sub2(after):
        g = dict(zip(later[:4], arrived(0, 4, after, "ag_wait_attn")))
        w = {n: _gathered_full(n, g[n]) for n in ("w_out", "ca_wq", "ca_wo")}
        w["ca_wkv"] = g["ca_wkv"]
        return w

    stages.sub2 = sub2
    stages.ffn_in = lambda after: _prepare_ffn_in(arrived(4, 5, after, "ag_wait_ffn_in")[0])
    stages.ffn_out = lambda after: _prepare_ffn_out(arrived(5, 6, after, "ag_wait_ffn_out")[0])
    sent = {}

    def grads_out(group, slabs):
        names = list(slabs)
        srcs16 = [slabs[n][1] for n in names]
        zones = [_landing(s.shape[1:], BF16, jnp.zeros(s.shape[1:], BF16), me) for s in srcs16]
        snd, rcv, s_thru, l_thru, tok = _spread_start(srcs16, zones, slabs[names[0]][0], False, f"rs_{group}_start")
        sent[group] = (names, [slabs[n][0] for n in names], (snd, rcv, s_thru, l_thru))
        return tok

    stages.grads_out = grads_out
    small = dict(gla_gate_b=gla_gate_b, gla_norm_g=gla_norm_g, ln1_g=ln1_g, ln1_b=ln1_b, ln2_g=ln2_g, ln2_b=ln2_b,
                 ln3_g=ln3_g, ln3_b=ln3_b)

    loss_part, dx, grads, small_parts = _local_step(x[0], mem[0], positions[0], loss_target[0], p, small, stages)
    loss = lax.psum(jnp.sum(loss_part), ("x", "y", "c"))

    out = {}
    me1 = me.reshape(1).astype(jnp.int32)
    for group, (names, own32, handles) in sent.items():
        landed = _spread_wait(*handles, dx, False, f"rs_{group}_wait")
        for n, own, land in zip(names, own32, landed):
            m_, v_ = moms[n]
            res4 = _adamw_direct(travel(n, weights[n]), travel(n, m_), travel(n, v_), own, land, me1, f"adamw_{n}")
            out[n] = [travel(n, a) for a in res4]

    slabs = [_grad_slabs(grads)["w_in"]]
    from_sib = _sibling_exchange(slabs, "rs_sibling")
    core = ci.reshape(1).astype(jnp.int32)
    p32, p16 = _pair_add(slabs[0], from_sib[0], core, "pair_add_w_in")
    (from_chips,) = _chip_exchange([p16], "rs_chips")
    chip = (2 * xi + yi).reshape(1).astype(jnp.int32)
    res4 = _adamw_big(travel("w_in", w_in), travel("w_in", m_w_in), travel("w_in", v_w_in), p32, from_chips, chip,
                      "adamw_w_in")
    out["w_in"] = [travel("w_in", a) for a in res4]

    packed = jnp.concatenate([small_parts[n] for n in SMALL_ORDER] + [small_parts["conv"],
                             small_parts["gla_gate_w2"].reshape(SUBLANES, -1)], axis=1)
    pad = (-packed.shape[1]) % 2048
    packed = jnp.pad(packed, ((0, 0), (0, pad)))
    (allp,) = _all_gather([packed], "ag_small")
    dev_sum, row_sum = _small_reduce(allp)
    off = 0
    for n in SMALL_ORDER:
        width = weights[n].shape[1]
        g = row_sum[0:1, off:off + width]
        off += width
        m_, v_ = moms[n]
        out[n] = _adamw(weights[n], m_, v_, g, f"adamw_{n}")
    conv_g = dev_sum[:, off:off + 2 * D_FF]
    off += 2 * D_FF
    g_cb = conv_g[3:4]
    out["ffn_conv_b"] = _adamw(ffn_conv_b, m_ffn_conv_b, v_ffn_conv_b, g_cb, "adamw_ffn_conv_b")
    wsh = ffn_conv_w.shape[2]
    g_cw = lax.dynamic_slice_in_dim(conv_g[0:3], me * wsh, wsh, axis=1)
    out["ffn_conv_w"] = _adamw(ffn_conv_w[0], m_ffn_conv_w[0], v_ffn_conv_w[0], g_cw, "adamw_ffn_conv_w")
    w2_g = dev_sum[:, off:off + GLA_RANK * GLA_HEADS * GLA_DK // SUBLANES].reshape(GLA_RANK, GLA_HEADS * GLA_DK)
    wsh2 = gla_gate_w2.shape[2]
    g_w2 = lax.dynamic_slice_in_dim(w2_g, me * wsh2, wsh2, axis=1)
    out["gla_gate_w2"] = _adamw(gla_gate_w2[0], m_gla_gate_w2[0], v_gla_gate_w2[0], g_w2, "adamw_gla_gate_w2")

    def shaped(n, a):
        return a.reshape(weights[n].shape)

    res = [loss, dx[None]]
    for k in range(4):
        res += [shaped(n, out[n][k]) for n in order]
    return tuple(res)


def _adamw_direct(w, m, v, own, land, me, name):
    _, r, c = w.shape
    tr, tc = _tile2d(r, c)
    blk = pl.BlockSpec((None, tr, tc), lambda i, j, s: (0, i, j))
    mine = pl.BlockSpec((None, tr, tc), lambda i, j, s: (s[0], i, j))
    slots = [pl.BlockSpec((None, tr, tc), lambda i, j, s, k=k: (k, i, j)) for k in range(8)]

    def body(s_ref, w_ref, m_ref, v_ref, p_ref, *rest):
        slot_refs, (g_ref, d_ref, nm_ref, nv_ref) = rest[:8], rest[8:]
        g = p_ref[...]
        for sr in slot_refs:
            g = g + sr[...].astype(F32)
        d_ref[...], nm_ref[...], nv_ref[...] = _adamw_math(w_ref[...], m_ref[...], v_ref[...], g)
        g_ref[...] = g

    gs = pltpu.PrefetchScalarGridSpec(num_scalar_prefetch=1, grid=(r // tr, c // tc),
                                      in_specs=[blk, blk, blk, mine] + slots, out_specs=[blk] * 4)
    return pl.pallas_call(body, name=name, grid_spec=gs, out_shape=[jax.ShapeDtypeStruct((1, r, c), F32)] * 4,
                          compiler_params=_params(("parallel", "parallel")))(me, w, m, v, own, *([land] * 8))


def _adamw_big(w, m, v, p32, rc, chip, name):
    _, r, c = w.shape
    tr, tc = _tile2d(r, c)
    blk = pl.BlockSpec((None, tr, tc), lambda i, j, s: (0, i, j))
    own = pl.BlockSpec((None, tr, tc), lambda i, j, s: (s[0], i, j))
    others = [pl.BlockSpec((None, tr, tc), lambda i, j, s, k=k: (k, i, j)) for k in range(3)]

    def body(s_ref, w_ref, m_ref, v_ref, p_ref, r0_ref, r1_ref, r2_ref, g_ref, d_ref, nm_ref, nv_ref):
        g = ((p_ref[...] + r0_ref[...].astype(F32)) + r1_ref[...].astype(F32)) + r2_ref[...].astype(F32)
        d_ref[...], nm_ref[...], nv_ref[...] = _adamw_math(w_ref[...], m_ref[...], v_ref[...], g)
        g_ref[...] = g

    gs = pltpu.PrefetchScalarGridSpec(num_scalar_prefetch=1, grid=(r // tr, c // tc),
                                      in_specs=[blk, blk, blk, own] + others, out_specs=[blk] * 4)
    return pl.pallas_call(body, name=name, grid_spec=gs, out_shape=[jax.ShapeDtypeStruct((1, r, c), F32)] * 4,
                          compiler_params=_params(("parallel", "parallel")))(chip, w, m, v, p32, rc, rc, rc)
```

```python
import functools
import math

import jax
import jax.numpy as jnp
from jax import lax
from jax.experimental import pallas as pl
from jax.experimental.pallas import tpu as pltpu

F32 = jnp.float32
BF16 = jnp.bfloat16
MESH = pl.DeviceIdType.MESH

D_MODEL = 2048
LN_EPS = 1e-5
GLA_HEADS = 4
GLA_DV = 256
GLA_DK = 128
GLA_RANK = 16
GLA_TAU = 16.0
GLA_CHUNK = 64
DIL_HD = 128
DIL_HEADS = 8
DIL_BAND = 128
DIL_DILATIONS = (1, 4, 16)
ROPE_THETA = 500000.0
ROPE_DIMS = 32
CA_HEADS = 4
CA_HD = 512
D_FF = 5504
ALPHA = 2.0 ** 0.25
ADAM_LR = 0.001
ADAM_B1 = 0.9
ADAM_B2 = 0.999
ADAM_EPS = 1e-08
ADAM_WD = 0.01
ADAM_STEP = 10

LANES = 128
SUBLANES = 8
VMEM_LIMIT = 56 * 1024 * 1024

GLA_W = 2 * GLA_HEADS * GLA_DK + 2 * GLA_HEADS * GLA_DV
HA_W = GLA_W + LANES
HB_W = 3 * DIL_HEADS * DIL_HD
FFP = 5632
NEG = -1e30


def _params(sem):
    return pltpu.CompilerParams(dimension_semantics=sem, vmem_limit_bytes=VMEM_LIMIT)


def _sigmoid(x):
    return 1.0 / (1.0 + jnp.exp(-x))


def _dot(a, b, dn, precision=None):
    return lax.dot_general(a, b, (dn, ((), ())), preferred_element_type=F32, precision=precision)


NN = ((1,), (0,))
NT = ((1,), (1,))
TN = ((0,), (0,))


def _bf(v):
    return v if v.dtype == BF16 else v.astype(BF16)


def _matmul(a, b, kind, out_dtype, tm, tn, tk, name, resid=None, resid_scale=1.0, b_k_off=0, b_slabs=False,
            out_slabs=False, also_bf16=False, dep=None):
    if b_slabs:
        assert kind != "nt" and b.shape[2] == tn
        k2, n = b.shape[1], b.shape[0] * tn
    elif kind == "nt":
        n, k2 = b.shape
    else:
        k2, n = b.shape
    (k, m) = a.shape if kind == "tn" else a.shape[::-1]
    assert k2 >= k and (k2 == k or not b_slabs) and m % tm == 0 and n % tn == 0 and k % tk == 0, \
        (name, a.shape, b.shape, tm, tn, tk)
    nk = k // tk
    dn = {"nn": NN, "nt": NT, "tn": TN}[kind]
    a_spec = pl.BlockSpec((tk, tm), lambda i, j, kk: (kk, i)) if kind == "tn" else pl.BlockSpec((tm, tk), lambda i, j, kk: (i, kk))
    if b_slabs:
        b_spec = pl.BlockSpec((None, tk, tn), lambda i, j, kk: (j, kk, 0))
    elif kind == "nt":
        b_spec = pl.BlockSpec((tn, tk), lambda i, j, kk: (j, kk + b_k_off))
    else:
        b_spec = pl.BlockSpec((tk, tn), lambda i, j, kk: (kk + b_k_off, j))
    if out_slabs:
        o_spec = pl.BlockSpec((None, tm, tn), lambda i, j, kk: (j, i, 0))
        o_shape = (n // tn, m, tn)
    else:
        o_spec = pl.BlockSpec((tm, tn), lambda i, j, kk: (i, j))
        o_shape = (m, n)
    has_resid = resid is not None

    n_in = 2 + int(has_resid) + int(dep is not None)

    def body(*refs):
        a_ref, b_ref = refs[:2]
        r_ref = refs[2] if has_resid else None
        o_ref = refs[n_in]
        ob_ref = refs[n_in + 1] if also_bf16 else None
        part = _dot(_bf(a_ref[...]), _bf(b_ref[...]), dn)

        def finish(acc):
            if has_resid:
                acc = acc + resid_scale * r_ref[...].astype(F32)
            o_ref[...] = acc.astype(out_dtype)
            if also_bf16:
                ob_ref[...] = acc.astype(BF16)

        if nk == 1:
            finish(part)
        else:
            acc_ref = refs[-1]
            kk = pl.program_id(2)

            @pl.when(kk == 0)
            def _():
                acc_ref[...] = part

            @pl.when(kk > 0)
            def _():
                acc_ref[...] += part

            @pl.when(kk == nk - 1)
            def _():
                finish(acc_ref[...])

    in_specs = [a_spec, b_spec] + ([o_spec] if has_resid else [])
    args = (a, b) + ((resid,) if has_resid else ())
    if dep is not None:
        in_specs.append(pl.BlockSpec((SUBLANES, LANES), lambda i, j, kk: (0, 0)))
        args += (dep,)
    o_struct = jax.ShapeDtypeStruct(o_shape, out_dtype)
    return pl.pallas_call(
        body, name=name, out_shape=[o_struct, jax.ShapeDtypeStruct(o_shape, BF16)] if also_bf16 else o_struct,
        grid=(m // tm, n // tn, nk), in_specs=in_specs, out_specs=[o_spec, o_spec] if also_bf16 else o_spec,
        scratch_shapes=[pltpu.VMEM((tm, tn), F32)] if nk > 1 else [],
        compiler_params=_params(("parallel", "parallel", "arbitrary")),
    )(*args)


def _ln_core(xres, f):
    p = ALPHA * xres + f
    mu = jnp.mean(p, axis=-1, keepdims=True)
    xc = p - mu
    var = jnp.mean(xc * xc, axis=-1, keepdims=True)
    rstd = lax.rsqrt(var + LN_EPS)
    return xc * rstd, rstd


def _rows8(v):
    r, c = v.shape
    return jnp.sum(v.reshape(r // SUBLANES, SUBLANES, c), axis=0)


def _ln_fwd(xres, f, g, b, name, tr=256):
    t, d = xres.shape
    row = pl.BlockSpec((tr, d), lambda i: (i, 0))
    vec = pl.BlockSpec((1, d), lambda i: (0, 0))

    def body(x_ref, f_ref, g_ref, b_ref, y_ref, yb_ref, yt_ref):
        xhat, _ = _ln_core(x_ref[...], f_ref[...])
        y = xhat * g_ref[...] + b_ref[...]
        y_ref[...] = y
        yb = y.astype(BF16)
        yb_ref[...] = yb
        yt_ref[...] = yb.T

    return pl.pallas_call(
        body, name=name, grid=(t // tr,), in_specs=[row, row, vec, vec],
        out_specs=[row, row, pl.BlockSpec((d, tr), lambda i: (0, i))],
        out_shape=[jax.ShapeDtypeStruct((t, d), F32), jax.ShapeDtypeStruct((t, d), BF16),
                   jax.ShapeDtypeStruct((d, t), BF16)],
        compiler_params=_params(("parallel",)),
    )(xres, f, g, b)


def _ln_bwd(xres, f, g, b, dy_or_target, loss_head, name, tr=256):
    t, d = xres.shape
    row = pl.BlockSpec((tr, d), lambda i: (i, 0))
    vec = pl.BlockSpec((1, d), lambda i: (0, 0))
    acc = pl.BlockSpec((SUBLANES, d), lambda i: (0, 0))
    lacc = pl.BlockSpec((SUBLANES, LANES), lambda i: (0, 0))

    def body(x_ref, f_ref, g_ref, b_ref, t_ref, dp_ref, dpb_ref, dg_ref, db_ref, *rest):
        i = pl.program_id(0)
        xhat, rstd = _ln_core(x_ref[...], f_ref[...])
        if loss_head:
            err = xhat * g_ref[...] + b_ref[...] - t_ref[...]
            dy = err * (1.0 / d)
            sq = err * err
            lanes = sq[:, :LANES]
            for kk in range(1, d // LANES):
                lanes = lanes + sq[:, kk * LANES:(kk + 1) * LANES]
            lpart = _rows8(lanes) * (0.5 / d)
        else:
            dy = t_ref[...]
        dxh = dy * g_ref[...]
        m1 = jnp.mean(dxh, axis=-1, keepdims=True)
        m2 = jnp.mean(dxh * xhat, axis=-1, keepdims=True)
        dp = rstd * (dxh - m1 - xhat * m2)
        dp_ref[...] = dp
        dpb_ref[...] = dp.astype(BF16)
        dgp = _rows8(dy * xhat)
        dbp = _rows8(dy)

        @pl.when(i == 0)
        def _():
            dg_ref[...] = dgp
            db_ref[...] = dbp
            if loss_head:
                rest[0][...] = lpart

        @pl.when(i > 0)
        def _():
            dg_ref[...] += dgp
            db_ref[...] += dbp
            if loss_head:
                rest[0][...] += lpart

    out_shape = [jax.ShapeDtypeStruct((t, d), F32), jax.ShapeDtypeStruct((t, d), BF16),
                 jax.ShapeDtypeStruct((SUBLANES, d), F32), jax.ShapeDtypeStruct((SUBLANES, d), F32)]
    out_specs = [row, row, acc, acc]
    if loss_head:
        out_shape.append(jax.ShapeDtypeStruct((SUBLANES, LANES), F32))
        out_specs.append(lacc)
    return pl.pallas_call(
        body, name=name, grid=(t // tr,), in_specs=[row, row, vec, vec, row], out_specs=out_specs,
        out_shape=out_shape, compiler_params=_params(("arbitrary",)),
    )(xres, f, g, b, dy_or_target)


def _gla_gates(glr, w2, gb):
    z = _dot(_bf(glr), w2, NN) + gb
    lg = (jnp.minimum(z, 0.0) - jnp.log(1.0 + jnp.exp(-jnp.abs(z)))) * (1.0 / GLA_TAU)
    c = z.shape[0]
    ri = lax.broadcasted_iota(jnp.int32, (c, c), 0)
    ci = lax.broadcasted_iota(jnp.int32, (c, c), 1)
    tri = (ci <= ri).astype(F32)
    bcum = _dot(tri, lg, NN, precision=lax.Precision.HIGHEST)
    blast = jnp.sum(lg, axis=0, keepdims=True)
    return z, bcum, blast, tri


def _gla_specs(t):
    c = GLA_CHUNK
    return c, t // c


def _gla_fwd(h_a, w2p, gate_b, norm_g):
    t = h_a.shape[0]
    c, n = _gla_specs(t)
    hk, hv = GLA_HEADS * GLA_DK, GLA_HEADS * GLA_DV
    scale = GLA_DK ** -0.5

    def body(q_ref, k_ref, v_ref, r_ref, glr_ref, w2_ref, gb_ref, ng_ref, og_ref, oraw_ref, sb_ref, st_ref):
        i = pl.program_id(0)

        @pl.when(i == 0)
        def _():
            st_ref[...] = jnp.zeros_like(st_ref)

        _, bcum, blast, _ = _gla_gates(glr_ref[...], w2_ref[...], gb_ref[...])
        ri = lax.broadcasted_iota(jnp.int32, (c, c), 0)
        ci = lax.broadcasted_iota(jnp.int32, (c, c), 1)
        causal = ci <= ri
        for h in range(GLA_HEADS):
            ks = slice(h * GLA_DK, (h + 1) * GLA_DK)
            vs = slice(h * GLA_DV, (h + 1) * GLA_DV)
            b_h, bl_h = bcum[:, ks], blast[:, ks]
            q_h, k_h = q_ref[:, ks], k_ref[:, ks]
            v_h = _bf(v_ref[:, vs])
            qi = _bf(q_h * scale * jnp.exp(b_h))
            ki = _bf(k_h * jnp.exp(-b_h))
            ke = _bf(k_h * jnp.exp(bl_h - b_h))
            st = st_ref[h]
            sb_ref[0, h] = st
            a = jnp.where(causal, _dot(qi, ki, NT), 0.0)
            o = _dot(_bf(a), v_h, NN) + _dot(qi, _bf(st), NT)
            st_ref[h] = st * jnp.exp(bl_h) + _dot(v_h, ke, TN)
            oraw_ref[:, vs] = o
            mu = jnp.mean(o, axis=-1, keepdims=True)
            oc = o - mu
            var = jnp.mean(oc * oc, axis=-1, keepdims=True)
            xh = oc * lax.rsqrt(var + LN_EPS)
            r_h = r_ref[:, vs]
            og_ref[:, vs] = (xh * ng_ref[:, vs] * (r_h * _sigmoid(r_h))).astype(BF16)

    return pl.pallas_call(
        body, name="gla_fwd", grid=(n,),
        in_specs=[pl.BlockSpec((c, hk), lambda i: (i, 0)), pl.BlockSpec((c, hk), lambda i: (i, 1)),
                  pl.BlockSpec((c, hv), lambda i: (i, 1)), pl.BlockSpec((c, hv), lambda i: (i, 2)),
                  pl.BlockSpec((c, LANES), lambda i: (i, GLA_W // LANES)),
                  pl.BlockSpec((LANES, hk), lambda i: (0, 0)), pl.BlockSpec((1, hk), lambda i: (0, 0)),
                  pl.BlockSpec((1, hv), lambda i: (0, 0))],
        out_specs=[pl.BlockSpec((c, hv), lambda i: (i, 0)), pl.BlockSpec((c, hv), lambda i: (i, 0)),
                   pl.BlockSpec((1, GLA_HEADS, GLA_DV, GLA_DK), lambda i: (i, 0, 0, 0))],
        out_shape=[jax.ShapeDtypeStruct((t, hv), BF16), jax.ShapeDtypeStruct((t, hv), F32),
                   jax.ShapeDtypeStruct((n, GLA_HEADS, GLA_DV, GLA_DK), F32)],
        scratch_shapes=[pltpu.VMEM((GLA_HEADS, GLA_DV, GLA_DK), F32)],
        compiler_params=_params(("arbitrary",)),
    )(h_a, h_a, h_a, h_a, h_a, w2p, gate_b, norm_g)


def _gla_bwd(h_a, w2p, gate_b, norm_g, o_raw, s_before, dmix):
    t = h_a.shape[0]
    c, n = _gla_specs(t)
    hk, hv = GLA_HEADS * GLA_DK, GLA_HEADS * GLA_DV
    scale = GLA_DK ** -0.5
    rev = lambda i: n - 1 - i

    def body(q_ref, k_ref, v_ref, r_ref, glr_ref, w2_ref, gb_ref, ng_ref, oraw_ref, sb_ref, do_ref,
             dh_ref, dw2_ref, dgb_ref, dng_ref, dst_ref):
        i = pl.program_id(0)

        @pl.when(i == 0)
        def _():
            dst_ref[...] = jnp.zeros_like(dst_ref)

        glr = glr_ref[...]
        z, bcum, blast, tri = _gla_gates(glr, w2_ref[...], gb_ref[...])
        ri = lax.broadcasted_iota(jnp.int32, (c, c), 0)
        ci = lax.broadcasted_iota(jnp.int32, (c, c), 1)
        causal = ci <= ri
        dlg_parts = []
        dng_parts = []
        for h in range(GLA_HEADS):
            ks = slice(h * GLA_DK, (h + 1) * GLA_DK)
            vs = slice(h * GLA_DV, (h + 1) * GLA_DV)
            o = oraw_ref[:, vs]
            mu = jnp.mean(o, axis=-1, keepdims=True)
            oc = o - mu
            var = jnp.mean(oc * oc, axis=-1, keepdims=True)
            rstd = lax.rsqrt(var + LN_EPS)
            xh = oc * rstd
            r_h = r_ref[:, vs]
            sg = _sigmoid(r_h)
            silu = r_h * sg
            dout = do_ref[:, vs]
            ng = ng_ref[:, vs]
            dng_parts.append(_rows8(dout * xh * silu))
            dr = dout * xh * ng * (sg * (1.0 + r_h * (1.0 - sg)))
            dxh = dout * ng * silu
            m1 = jnp.mean(dxh, axis=-1, keepdims=True)
            m2 = jnp.mean(dxh * xh, axis=-1, keepdims=True)
            do_raw = _bf(rstd * (dxh - m1 - xh * m2))
            b_h, bl_h = bcum[:, ks], blast[:, ks]
            q_h, k_h = q_ref[:, ks], k_ref[:, ks]
            v_h = _bf(v_ref[:, vs])
            eb, enb, eend = jnp.exp(b_h), jnp.exp(-b_h), jnp.exp(bl_h - b_h)
            decay = jnp.exp(bl_h)
            qi_f, ki_f, ke_f = q_h * scale * eb, k_h * enb, k_h * eend
            qi, ki, ke = _bf(qi_f), _bf(ki_f), _bf(ke_f)
            st = sb_ref[0, h]
            dst = dst_ref[h]
            dst_b = _bf(dst)
            a = _bf(jnp.where(causal, _dot(qi, ki, NT), 0.0))
            da = _bf(jnp.where(causal, _dot(do_raw, v_h, NT), 0.0))
            dv = _dot(a, do_raw, TN) + _dot(ke, dst_b, NT)
            dqi = _dot(da, ki, NN) + _dot(do_raw, _bf(st), NN)
            dki = _dot(da, qi, TN)
            dke = _dot(v_h, dst_b, NN)
            dst_ref[h] = _dot(do_raw, qi, TN) + dst * decay
            dbl = decay * jnp.sum(st * dst, axis=0, keepdims=True) + jnp.sum(dke * ke_f, axis=0, keepdims=True)
            dbc = dqi * qi_f - dki * ki_f - dke * ke_f
            dlg_parts.append(_dot(tri, dbc, TN, precision=lax.Precision.HIGHEST) + dbl)
            dh_ref[:, ks] = (dqi * eb * scale).astype(BF16)
            dh_ref[:, hk + h * GLA_DK: hk + (h + 1) * GLA_DK] = (dki * enb + dke * eend).astype(BF16)
            dh_ref[:, 2 * hk + h * GLA_DV: 2 * hk + (h + 1) * GLA_DV] = dv.astype(BF16)
            dh_ref[:, 2 * hk + hv + h * GLA_DV: 2 * hk + hv + (h + 1) * GLA_DV] = dr.astype(BF16)
        dlg = jnp.concatenate(dlg_parts, axis=1)
        dz = dlg * (1.0 / GLA_TAU) * _sigmoid(-z)
        dz_b = _bf(dz)
        dh_ref[:, GLA_W:] = _dot(dz_b, w2_ref[...], NT).astype(BF16)
        dw2p = _dot(_bf(glr), dz_b, TN)
        dgbp = _rows8(dz)
        dngp = jnp.concatenate(dng_parts, axis=1)

        @pl.when(i == 0)
        def _():
            dw2_ref[...] = dw2p
            dgb_ref[...] = dgbp
            dng_ref[...] = dngp

        @pl.when(i > 0)
        def _():
            dw2_ref[...] += dw2p
            dgb_ref[...] += dgbp
            dng_ref[...] += dngp

    return pl.pallas_call(
        body, name="gla_bwd", grid=(n,),
        in_specs=[pl.BlockSpec((c, hk), lambda i: (rev(i), 0)), pl.BlockSpec((c, hk), lambda i: (rev(i), 1)),
                  pl.BlockSpec((c, hv), lambda i: (rev(i), 1)), pl.BlockSpec((c, hv), lambda i: (rev(i), 2)),
                  pl.BlockSpec((c, LANES), lambda i: (rev(i), GLA_W // LANES)),
                  pl.BlockSpec((LANES, hk), lambda i: (0, 0)), pl.BlockSpec((1, hk), lambda i: (0, 0)),
                  pl.BlockSpec((1, hv), lambda i: (0, 0)),
                  pl.BlockSpec((c, hv), lambda i: (rev(i), 0)),
                  pl.BlockSpec((1, GLA_HEADS, GLA_DV, GLA_DK), lambda i: (rev(i), 0, 0, 0)),
                  pl.BlockSpec((c, hv), lambda i: (rev(i), 0))],
        out_specs=[pl.BlockSpec((c, HA_W), lambda i: (rev(i), 0)),
                   pl.BlockSpec((LANES, hk), lambda i: (0, 0)),
                   pl.BlockSpec((SUBLANES, hk), lambda i: (0, 0)),
                   pl.BlockSpec((SUBLANES, hv), lambda i: (0, 0))],
        out_shape=[jax.ShapeDtypeStruct((t, HA_W), BF16), jax.ShapeDtypeStruct((LANES, hk), F32),
                   jax.ShapeDtypeStruct((SUBLANES, hk), F32), jax.ShapeDtypeStruct((SUBLANES, hv), F32)],
        scratch_shapes=[pltpu.VMEM((GLA_HEADS, GLA_DV, GLA_DK), F32)],
        compiler_params=_params(("arbitrary",)),
    )(h_a, h_a, h_a, h_a, h_a, w2p, gate_b, norm_g, o_raw, s_before, dmix)


def _rope_tables(positions):
    half = ROPE_DIMS // 2
    inv_freq = ROPE_THETA ** (-jnp.arange(0, ROPE_DIMS, 2, dtype=F32) / ROPE_DIMS)
    ang = positions.astype(F32).reshape(-1, 1) * inv_freq
    cos, sin = jnp.cos(ang), jnp.sin(ang)
    t = cos.shape[0]
    one = jnp.ones((t, DIL_HD - ROPE_DIMS), F32)
    zero = jnp.zeros((t, DIL_HD - ROPE_DIMS), F32)
    zh = jnp.zeros((t, half), F32)
    return (jnp.concatenate([cos, cos, one], axis=1), jnp.concatenate([-sin, zh, zero], axis=1),
            jnp.concatenate([zh, sin, zero], axis=1))


def _rope_apply(x, c, s1, s2):
    half = ROPE_DIMS // 2
    return x * c + pltpu.roll(x, DIL_HD - half, 1) * s1 + pltpu.roll(x, half, 1) * s2


def _rope_apply_t(dy, c, s1, s2):
    half = ROPE_DIMS // 2
    return dy * c + pltpu.roll(dy * s1, half, 1) + pltpu.roll(dy * s2, DIL_HD - half, 1)


def _rope_fwd(h_b, tabs, tr=256):
    t = h_b.shape[0]
    w = DIL_HEADS * DIL_HD
    scale = DIL_HD ** -0.5
    tab = pl.BlockSpec((tr, DIL_HD), lambda i: (i, 0))
    outb = pl.BlockSpec((tr, w), lambda i: (i, 0))

    def body(q_ref, k_ref, c_ref, s1_ref, s2_ref, qo_ref, ko_ref):
        c, s1, s2 = c_ref[...], s1_ref[...], s2_ref[...]
        for h in range(DIL_HEADS):
            hs = slice(h * DIL_HD, (h + 1) * DIL_HD)
            qo_ref[:, hs] = _rope_apply(q_ref[:, hs] * scale, c, s1, s2)
            ko_ref[:, hs] = _rope_apply(k_ref[:, hs], c, s1, s2)

    return pl.pallas_call(
        body, name="rope_fwd", grid=(t // tr,),
        in_specs=[pl.BlockSpec((tr, w), lambda i: (i, 0)), pl.BlockSpec((tr, w), lambda i: (i, 1)), tab, tab, tab],
        out_specs=[outb, outb],
        out_shape=[jax.ShapeDtypeStruct((t, w), F32)] * 2,
        compiler_params=_params(("parallel",)),
    )(h_b, h_b, *tabs)


def _dil_dh(dq, dk, dv, tabs, tr=256):
    t, w = dq.shape
    scale = DIL_HD ** -0.5
    tab = pl.BlockSpec((tr, DIL_HD), lambda i: (i, 0))
    inb = pl.BlockSpec((tr, w), lambda i: (i, 0))

    def body(dq_ref, dk_ref, dv_ref, c_ref, s1_ref, s2_ref, o_ref):
        c, s1, s2 = c_ref[...], s1_ref[...], s2_ref[...]
        for h in range(DIL_HEADS):
            hs = slice(h * DIL_HD, (h + 1) * DIL_HD)
            o_ref[:, h * DIL_HD:(h + 1) * DIL_HD] = (_rope_apply_t(dq_ref[:, hs], c, s1, s2) * scale).astype(BF16)
            o_ref[:, w + h * DIL_HD: w + (h + 1) * DIL_HD] = _rope_apply_t(dk_ref[:, hs], c, s1, s2).astype(BF16)
        o_ref[:, 2 * w:] = dv_ref[...].astype(BF16)

    return pl.pallas_call(
        body, name="dil_dh", grid=(t // tr,), in_specs=[inb] * 3 + [tab] * 3,
        out_specs=pl.BlockSpec((tr, 3 * w), lambda i: (i, 0)),
        out_shape=jax.ShapeDtypeStruct((t, 3 * w), BF16), compiler_params=_params(("parallel",)),
    )(dq, dk, dv, *tabs)


BANDS = 8


def _to_branch(a, d):
    t, w = a.shape
    return a.reshape(t // d, d, w // DIL_HD, DIL_HD).transpose(1, 2, 0, 3).reshape(-1, DIL_HD)


def _from_branch(a, d, t):
    hds = a.shape[0] // t
    return a.reshape(d, hds, t // d, DIL_HD).transpose(2, 0, 1, 3).reshape(t, hds * DIL_HD)


def _band_masks(not_first):
    r = lax.broadcasted_iota(jnp.int32, (DIL_BAND, 2 * DIL_BAND), 0)
    c = lax.broadcasted_iota(jnp.int32, (DIL_BAND, 2 * DIL_BAND), 1)
    nf = jnp.full((DIL_BAND, 2 * DIL_BAND), not_first, jnp.int32)
    look_back = jnp.logical_and(jnp.logical_and(c < DIL_BAND, c >= r), nf > 0)
    own_band = jnp.logical_and(c >= DIL_BAND, (c - DIL_BAND) <= r)
    return jnp.logical_or(look_back, own_band)


def _dil_fwd(q, k, v, nb, name):
    rows = q.shape[0]
    blk = BANDS * DIL_BAND
    steps = rows // blk
    main = pl.BlockSpec((blk, DIL_HD), lambda i: (i, 0))
    prev = pl.BlockSpec((DIL_BAND, DIL_HD), lambda i: (jnp.maximum(i * BANDS - 1, 0), 0))

    def body(q_ref, k_ref, v_ref, kp_ref, vp_ref, o_ref, l_ref):
        i = pl.program_id(0)
        for j in range(BANDS):
            lo, hi = j * DIL_BAND, (j + 1) * DIL_BAND
            if j == 0:
                kcat = jnp.concatenate([kp_ref[...], k_ref[lo:hi, :]], axis=0)
                vcat = jnp.concatenate([vp_ref[...], v_ref[lo:hi, :]], axis=0)
            else:
                kcat = k_ref[lo - DIL_BAND:hi, :]
                vcat = v_ref[lo - DIL_BAND:hi, :]
            not_first = (((i * BANDS + j) % nb) != 0).astype(jnp.int32)
            s = jnp.where(_band_masks(not_first), _dot(q_ref[lo:hi, :], kcat, NT), NEG)
            m = jnp.max(s, axis=-1, keepdims=True)
            p = jnp.exp(s - m)
            den = jnp.sum(p, axis=-1, keepdims=True)
            o_ref[lo:hi, :] = _dot(_bf(p), vcat, NN) / den
            l_ref[lo:hi, :] = jnp.broadcast_to(m + jnp.log(den), (DIL_BAND, DIL_HD))

    return pl.pallas_call(
        body, name=name, grid=(steps,), in_specs=[main, main, main, prev, prev], out_specs=[main, main],
        out_shape=[jax.ShapeDtypeStruct((rows, DIL_HD), F32)] * 2, compiler_params=_params(("parallel",)),
    )(q, k, v, k, v)


def _dil_bwd(q, k, v, do, lse, dd, nb, name):
    rows = q.shape[0]
    blk = BANDS * DIL_BAND
    steps = rows // blk
    last_band = rows // DIL_BAND - 1
    main = pl.BlockSpec((blk, DIL_HD), lambda i: (i, 0))
    prev = pl.BlockSpec((DIL_BAND, DIL_HD), lambda i: (jnp.maximum(i * BANDS - 1, 0), 0))
    nxt = pl.BlockSpec((DIL_BAND, DIL_HD), lambda i: (jnp.minimum(i * BANDS + BANDS, last_band), 0))

    def body(q_ref, k_ref, v_ref, do_ref, l_ref, dd_ref, kp_ref, vp_ref, qn_ref, don_ref, ln_ref, ddn_ref,
             dq_ref, dk_ref, dv_ref, ak_ref, av_ref):
        i = pl.program_id(0)
        ak_ref[...] = jnp.zeros_like(ak_ref)
        av_ref[...] = jnp.zeros_like(av_ref)
        for j in range(BANDS + 1):
            lo, hi = j * DIL_BAND, (j + 1) * DIL_BAND
            if j == 0:
                kcat = jnp.concatenate([kp_ref[...], k_ref[lo:hi, :]], axis=0)
                vcat = jnp.concatenate([vp_ref[...], v_ref[lo:hi, :]], axis=0)
            elif j < BANDS:
                kcat = k_ref[lo - DIL_BAND:hi, :]
                vcat = v_ref[lo - DIL_BAND:hi, :]
            else:
                kcat = jnp.concatenate([k_ref[lo - DIL_BAND:lo, :], k_ref[lo - DIL_BAND:lo, :]], axis=0)
                vcat = jnp.concatenate([v_ref[lo - DIL_BAND:lo, :], v_ref[lo - DIL_BAND:lo, :]], axis=0)
            if j < BANDS:
                qj, doj, lj, ddj = q_ref[lo:hi, :], do_ref[lo:hi, :], l_ref[lo:hi, :], dd_ref[lo:hi, :]
            else:
                qj, doj, lj, ddj = qn_ref[...], don_ref[...], ln_ref[...], ddn_ref[...]
            not_first = (((i * BANDS + j) % nb) != 0).astype(jnp.int32)
            mask = _band_masks(not_first)
            if j == BANDS:
                cidx = lax.broadcasted_iota(jnp.int32, mask.shape, 1)
                mask = jnp.logical_and(mask, cidx < DIL_BAND)
            s = jnp.where(mask, _dot(qj, kcat, NT), NEG)
            p = jnp.exp(s - jnp.concatenate([lj, lj], axis=1))
            dp = _dot(doj, vcat, NT)
            ds = _bf(p * (dp - jnp.concatenate([ddj, ddj], axis=1)))
            if j < BANDS:
                dq_ref[lo:hi, :] = _dot(ds, kcat, NN)
            ak_ref[lo:hi + DIL_BAND, :] += _dot(ds, qj, TN)
            av_ref[lo:hi + DIL_BAND, :] += _dot(_bf(p), doj, TN)
        dk_ref[...] = ak_ref[DIL_BAND:DIL_BAND + blk, :]
        dv_ref[...] = av_ref[DIL_BAND:DIL_BAND + blk, :]

    return pl.pallas_call(
        body, name=name, grid=(steps,),
        in_specs=[main] * 6 + [prev, prev] + [nxt] * 4, out_specs=[main] * 3,
        out_shape=[jax.ShapeDtypeStruct((rows, DIL_HD), F32)] * 3,
        scratch_shapes=[pltpu.VMEM((blk + 2 * DIL_BAND, DIL_HD), F32)] * 2,
        compiler_params=_params(("parallel",)),
    )(q, k, v, do, lse, dd, k, v, q, do, lse, dd)


def _dil_merge(os_, ls_, tr=256):
    t, w = os_[0].shape
    blk = pl.BlockSpec((tr, w), lambda i: (i, 0))

    def body(o1, o2, o3, l1, l2, l3, ob_ref, of_ref, lt_ref):
        a, b, c = l1[...], l2[...], l3[...]
        m = jnp.maximum(jnp.maximum(a, b), c)
        ea, eb, ec = jnp.exp(a - m), jnp.exp(b - m), jnp.exp(c - m)
        den = ea + eb + ec
        o = (ea * o1[...] + eb * o2[...] + ec * o3[...]) / den
        ob_ref[...] = o.astype(BF16)
        of_ref[...] = o
        lt_ref[...] = m + jnp.log(den)

    return pl.pallas_call(
        body, name="dil_merge", grid=(t // tr,), in_specs=[blk] * 6, out_specs=[blk] * 3,
        out_shape=[jax.ShapeDtypeStruct((t, w), BF16), jax.ShapeDtypeStruct((t, w), F32),
                   jax.ShapeDtypeStruct((t, w), F32)],
        compiler_params=_params(("parallel",)),
    )(*os_, *ls_)


def _dil_bwd_prep(dmix, o_d, tr=256):
    t, w = o_d.shape
    blk = pl.BlockSpec((tr, w), lambda i: (i, 0))

    def body(do_ref, o_ref, dob_ref, dd_ref):
        do = do_ref[...]
        prod = do * o_ref[...]
        dob_ref[...] = do.astype(BF16)
        for h in range(DIL_HEADS):
            hs = slice(h * DIL_HD, (h + 1) * DIL_HD)
            dd_ref[:, hs] = jnp.broadcast_to(jnp.sum(prod[:, hs], axis=-1, keepdims=True), (tr, DIL_HD))

    return pl.pallas_call(
        body, name="dil_bwd_prep", grid=(t // tr,),
        in_specs=[pl.BlockSpec((tr, w), lambda i: (i, 1)), blk], out_specs=[blk, blk],
        out_shape=[jax.ShapeDtypeStruct((t, w), BF16), jax.ShapeDtypeStruct((t, w), F32)],
        compiler_params=_params(("parallel",)),
    )(dmix, o_d)


def _gather_rows(dst_ref, src_ref, t, d, cast=None):
    n = t // d
    for r in range(d):
        v = src_ref[pl.ds(r, n, stride=d), :] if d > 1 else src_ref[...]
        dst_ref[r * n:(r + 1) * n, :] = v if cast is None else v.astype(cast)


def _tri_mask():
    r = lax.broadcasted_iota(jnp.int32, (DIL_BAND, DIL_BAND), 0)
    c = lax.broadcasted_iota(jnp.int32, (DIL_BAND, DIL_BAND), 1)
    return c <= r


def _dil_fwd_all(qr, kr, h_b):
    t = qr.shape[0]
    nbands = t // DIL_BAND
    nbr = len(DIL_DILATIONS)
    hoff = DIL_HEADS

    def col(off):
        return pl.BlockSpec((t, DIL_HD), lambda h: (0, off + h), pipeline_mode=pl.Buffered(1))

    outb = pl.BlockSpec((t, DIL_HD), lambda h: (0, h))

    def body(q_ref, k_ref, v_ref, ob_ref, of_ref, lt_ref, qs, ks, vs, os_, ls_, *br):
        obr, lbr = br[:nbr], br[nbr:]
        for bi, d in enumerate(DIL_DILATIONS):
            n = t // d
            nb = n // DIL_BAND
            _gather_rows(qs, q_ref, t, d, BF16)
            _gather_rows(ks, k_ref, t, d, BF16)
            _gather_rows(vs, v_ref, t, d, BF16)
            s = jnp.where(_tri_mask(), _dot(qs[0:DIL_BAND, :], ks[0:DIL_BAND, :], NT), NEG)
            m = jnp.max(s, axis=-1, keepdims=True)
            pr = jnp.exp(s - m)
            den = jnp.sum(pr, axis=-1, keepdims=True)
            os_[0:DIL_BAND, :] = _dot(_bf(pr), vs[0:DIL_BAND, :], NN) / den
            ls_[0:DIL_BAND, :] = jnp.broadcast_to(m + jnp.log(den), (DIL_BAND, DIL_HD))

            def band(b, carry, nb=nb):
                st = pl.multiple_of((b - 1) * DIL_BAND, DIL_BAND)
                cur = pl.ds(st + DIL_BAND, DIL_BAND)
                both = pl.ds(st, 2 * DIL_BAND)
                not_first = ((b % nb) != 0).astype(jnp.int32)
                s = jnp.where(_band_masks(not_first), _dot(qs[cur, :], ks[both, :], NT), NEG)
                m = jnp.max(s, axis=-1, keepdims=True)
                pr = jnp.exp(s - m)
                den = jnp.sum(pr, axis=-1, keepdims=True)
                os_[cur, :] = _dot(_bf(pr), vs[both, :], NN) / den
                ls_[cur, :] = jnp.broadcast_to(m + jnp.log(den), (DIL_BAND, DIL_HD))
                return carry

            lax.fori_loop(1, nbands, band, 0, unroll=4)
            for r in range(d):
                dst = pl.ds(r, n, stride=d) if d > 1 else slice(None)
                obr[bi][dst, :] = os_[r * n:(r + 1) * n, :]
                lbr[bi][dst, :] = ls_[r * n:(r + 1) * n, :]
        rows = 512
        for c0 in range(0, t, rows):
            sl = slice(c0, c0 + rows)
            la, lb, lc = lbr[0][sl, :], lbr[1][sl, :], lbr[2][sl, :]
            m = jnp.maximum(jnp.maximum(la, lb), lc)
            ea, eb, ec = jnp.exp(la - m), jnp.exp(lb - m), jnp.exp(lc - m)
            den = ea + eb + ec
            o = (ea * obr[0][sl, :] + eb * obr[1][sl, :] + ec * obr[2][sl, :]) / den
            ob_ref[sl, :] = o.astype(BF16)
            of_ref[sl, :] = o
            lt_ref[sl, :] = m + jnp.log(den)

    w = DIL_HEADS * DIL_HD
    vm = lambda dt: pltpu.VMEM((t, DIL_HD), dt)
    return pl.pallas_call(
        body, name="dil_fwd", grid=(DIL_HEADS,), in_specs=[col(0), col(0), col(2 * hoff)],
        out_specs=[outb, outb, outb],
        out_shape=[jax.ShapeDtypeStruct((t, w), BF16), jax.ShapeDtypeStruct((t, w), F32),
                   jax.ShapeDtypeStruct((t, w), F32)],
        scratch_shapes=[vm(BF16)] * 3 + [vm(F32)] * 2 + [vm(F32)] * (2 * nbr),
        compiler_params=_params(("parallel",)),
    )(qr, kr, h_b)


def _dil_bwd_all(qr, kr, h_b, dmix, o_d, lse_tot):
    t = qr.shape[0]
    nbands = t // DIL_BAND
    hoff = DIL_HEADS

    def col(off):
        return pl.BlockSpec((t, DIL_HD), lambda h: (0, off + h), pipeline_mode=pl.Buffered(1))

    outb = pl.BlockSpec((t, DIL_HD), lambda h: (0, h))

    def body(q_ref, k_ref, v_ref, do_ref, o_ref, l_ref, dq_ref, dk_ref, dv_ref,
             qs, ks, vs, dos, lss, dds, dqs, acck, accv):
        for bi, d in enumerate(DIL_DILATIONS):
            n = t // d
            nb = n // DIL_BAND
            _gather_rows(qs, q_ref, t, d, BF16)
            _gather_rows(ks, k_ref, t, d, BF16)
            _gather_rows(vs, v_ref, t, d, BF16)
            _gather_rows(dos, do_ref, t, d, BF16)
            _gather_rows(lss, l_ref, t, d)
            for r in range(d):
                src = pl.ds(r, n, stride=d) if d > 1 else slice(None)
                prod = do_ref[src, :] * o_ref[src, :]
                dds[r * n:(r + 1) * n, :] = jnp.broadcast_to(jnp.sum(prod, axis=-1, keepdims=True), (n, DIL_HD))
            acck[...] = jnp.zeros_like(acck)
            accv[...] = jnp.zeros_like(accv)
            b0 = slice(0, DIL_BAND)
            s = jnp.where(_tri_mask(), _dot(qs[b0, :], ks[b0, :], NT), NEG)
            pr = jnp.exp(s - lss[b0, :])
            ds = _bf(pr * (_dot(dos[b0, :], vs[b0, :], NT) - dds[b0, :]))
            dqs[b0, :] = _dot(ds, ks[b0, :], NN)
            acck[DIL_BAND:2 * DIL_BAND, :] += _dot(ds, qs[b0, :], TN)
            accv[DIL_BAND:2 * DIL_BAND, :] += _dot(_bf(pr), dos[b0, :], TN)

            def band(b, carry, nb=nb):
                st = pl.multiple_of((b - 1) * DIL_BAND, DIL_BAND)
                cur = pl.ds(st + DIL_BAND, DIL_BAND)
                both = pl.ds(st, 2 * DIL_BAND)
                acc_rows = pl.ds(st + DIL_BAND, 2 * DIL_BAND)
                not_first = ((b % nb) != 0).astype(jnp.int32)
                qb, dob, lb, ddb = qs[cur, :], dos[cur, :], lss[cur, :], dds[cur, :]
                kcat, vcat = ks[both, :], vs[both, :]
                s = jnp.where(_band_masks(not_first), _dot(qb, kcat, NT), NEG)
                pr = jnp.exp(s - jnp.concatenate([lb, lb], axis=1))
                ds = _bf(pr * (_dot(dob, vcat, NT) - jnp.concatenate([ddb, ddb], axis=1)))
                dqs[cur, :] = _dot(ds, kcat, NN)
                acck[acc_rows, :] += _dot(ds, qb, TN)
                accv[acc_rows, :] += _dot(_bf(pr), dob, TN)
                return carry

            lax.fori_loop(1, nbands, band, 0, unroll=2)
            for r in range(d):
                lo = r * n
                if d == 1:
                    dq_ref[...] = dqs[...]
                    dk_ref[...] = acck[DIL_BAND:DIL_BAND + t, :]
                    dv_ref[...] = accv[DIL_BAND:DIL_BAND + t, :]
                else:
                    dst = pl.ds(r, n, stride=d)
                    dq_ref[dst, :] = dq_ref[dst, :] + dqs[lo:lo + n, :]
                    dk_ref[dst, :] = dk_ref[dst, :] + acck[DIL_BAND + lo:DIL_BAND + lo + n, :]
                    dv_ref[dst, :] = dv_ref[dst, :] + accv[DIL_BAND + lo:DIL_BAND + lo + n, :]

    w = DIL_HEADS * DIL_HD
    vm = lambda dt, extra=0: pltpu.VMEM((t + extra, DIL_HD), dt)
    return pl.pallas_call(
        body, name="dil_bwd", grid=(DIL_HEADS,),
        in_specs=[col(0), col(0), col(2 * hoff), col(hoff), col(0), col(0)], out_specs=[outb] * 3,
        out_shape=[jax.ShapeDtypeStruct((t, w), F32)] * 3,
        scratch_shapes=[vm(BF16)] * 4 + [vm(F32)] * 3 + [vm(F32, DIL_BAND)] * 2,
        compiler_params=_params(("parallel",)),
    )(qr, kr, h_b, dmix, o_d, lse_tot)


def _ca_fwd(q, memkv, tq=512):
    t, d = q.shape
    m = memkv.shape[0]
    scale = CA_HD ** -0.5

    def body(q_ref, k_ref, v_ref, o_ref, ot_ref):
        for h in range(CA_HEADS):
            hs = slice(h * CA_HD, (h + 1) * CA_HD)
            s = _dot(q_ref[:, hs], k_ref[:, hs], NT) * scale
            p = jnp.exp(s - jnp.max(s, axis=-1, keepdims=True))
            p = p / jnp.sum(p, axis=-1, keepdims=True)
            o = _dot(_bf(p), v_ref[:, hs], NN).astype(BF16)
            o_ref[:, hs] = o
            ot_ref[hs, :] = o.T

    return pl.pallas_call(
        body, name="ca_fwd", grid=(t // tq,),
        in_specs=[pl.BlockSpec((tq, d), lambda i: (i, 0)), pl.BlockSpec((m, d), lambda i: (0, 0)),
                  pl.BlockSpec((m, d), lambda i: (0, 1))],
        out_specs=[pl.BlockSpec((tq, d), lambda i: (i, 0)), pl.BlockSpec((d, tq), lambda i: (0, i))],
        out_shape=[jax.ShapeDtypeStruct((t, d), BF16), jax.ShapeDtypeStruct((d, t), BF16)],
        compiler_params=_params(("parallel",)),
    )(q, memkv, memkv)


def _ca_bwd(q, memkv, do, tq=512):
    t, d = q.shape
    m = memkv.shape[0]
    scale = CA_HD ** -0.5

    def body(q_ref, k_ref, v_ref, do_ref, dq_ref, dkv_ref):
        i = pl.program_id(0)

        @pl.when(i == 0)
        def _():
            dkv_ref[...] = jnp.zeros_like(dkv_ref)

        for h in range(CA_HEADS):
            hs = slice(h * CA_HD, (h + 1) * CA_HD)
            q_h, k_h, v_h, do_h = q_ref[:, hs], k_ref[:, hs], v_ref[:, hs], do_ref[:, hs]
            s = _dot(q_h, k_h, NT) * scale
            p = jnp.exp(s - jnp.max(s, axis=-1, keepdims=True))
            p = p / jnp.sum(p, axis=-1, keepdims=True)
            dp = _dot(do_h, v_h, NT)
            ds = _bf(p * (dp - jnp.sum(p * dp, axis=-1, keepdims=True)) * scale)
            dq_ref[:, hs] = _dot(ds, k_h, NN).astype(BF16)
            dkv_ref[:, hs] += _dot(ds, q_h, TN)
            dkv_ref[:, d + h * CA_HD: d + (h + 1) * CA_HD] += _dot(_bf(p), do_h, TN)

    return pl.pallas_call(
        body, name="ca_bwd", grid=(t // tq,),
        in_specs=[pl.BlockSpec((tq, d), lambda i: (i, 0)), pl.BlockSpec((m, d), lambda i: (0, 0)),
                  pl.BlockSpec((m, d), lambda i: (0, 1)), pl.BlockSpec((tq, d), lambda i: (i, 0))],
        out_specs=[pl.BlockSpec((tq, d), lambda i: (i, 0)), pl.BlockSpec((m, 2 * d), lambda i: (0, 0))],
        out_shape=[jax.ShapeDtypeStruct((t, d), BF16), jax.ShapeDtypeStruct((m, 2 * d), F32)],
        compiler_params=_params(("arbitrary",)),
    )(q, memkv, memkv, do)


STRIP = 256


def _shift_down(u, n, row):
    return jnp.where(row >= n, pltpu.roll(u, n, 0), 0.0)


def _shift_up(u, n, row):
    t = u.shape[0]
    return jnp.where(row < t - n, pltpu.roll(u, t - n, 0), 0.0)


def _conv(u, cw_ref, row):
    return ((cw_ref[3:4, :] + cw_ref[0:1, :] * _shift_down(u, 2, row)) + cw_ref[1:2, :] * _shift_down(u, 1, row)) \
        + cw_ref[2:3, :] * u


def _swiglu_fwd(u0, cw):
    t, w = u0.shape[0], u0.shape[1] // 2
    ns = w // STRIP
    col = pl.BlockSpec((t, STRIP), lambda j: (0, j))
    col_up = pl.BlockSpec((t, STRIP), lambda j: (0, ns + j))
    cws = pl.BlockSpec((SUBLANES, STRIP), lambda j: (0, j))
    cws_up = pl.BlockSpec((SUBLANES, STRIP), lambda j: (0, ns + j))

    def body(g_ref, u_ref, cg_ref, cu_ref, a_ref, at_ref):
        row = lax.broadcasted_iota(jnp.int32, (t, STRIP), 0)
        gate = _conv(g_ref[...].astype(F32), cg_ref, row)
        up = _conv(u_ref[...].astype(F32), cu_ref, row)
        act = (gate * _sigmoid(gate) * up).astype(BF16)
        a_ref[...] = act
        at_ref[...] = act.T

    return pl.pallas_call(
        body, name="swiglu_fwd", grid=(ns,), in_specs=[col, col_up, cws, cws_up],
        out_specs=[col, pl.BlockSpec((STRIP, t), lambda j: (j, 0))],
        out_shape=[jax.ShapeDtypeStruct((t, w), BF16), jax.ShapeDtypeStruct((w, t), BF16)],
        compiler_params=_params(("parallel",)),
    )(u0, u0, cw, cw)


def _swiglu_bwd(u0, cw, da):
    t, w = u0.shape[0], u0.shape[1] // 2
    ns = w // STRIP
    col = pl.BlockSpec((t, STRIP), lambda j: (0, j))
    col_up = pl.BlockSpec((t, STRIP), lambda j: (0, ns + j))
    cws = pl.BlockSpec((SUBLANES, STRIP), lambda j: (0, j))
    cws_up = pl.BlockSpec((SUBLANES, STRIP), lambda j: (0, ns + j))

    def conv_bwd(du, u0, cw_ref, row, du0_ref, du0t_ref, dcw_ref):
        du0 = (cw_ref[2:3, :] * du + cw_ref[1:2, :] * _shift_up(du, 1, row)) + cw_ref[0:1, :] * _shift_up(du, 2, row)
        du0 = du0.astype(BF16)
        du0_ref[...] = du0
        du0t_ref[...] = du0.T
        dcw_ref[0:1, :] = jnp.sum(du * _shift_down(u0, 2, row), axis=0, keepdims=True)
        dcw_ref[1:2, :] = jnp.sum(du * _shift_down(u0, 1, row), axis=0, keepdims=True)
        dcw_ref[2:3, :] = jnp.sum(du * u0, axis=0, keepdims=True)
        dcw_ref[3:4, :] = jnp.sum(du, axis=0, keepdims=True)
        dcw_ref[4:8, :] = jnp.zeros((4, STRIP), F32)

    def body(g_ref, u_ref, cg_ref, cu_ref, da_ref, dg0_ref, du0_ref, dut_ref, dcg_ref, dcu_ref):
        row = lax.broadcasted_iota(jnp.int32, (t, STRIP), 0)
        g0, up0 = g_ref[...].astype(F32), u_ref[...].astype(F32)
        gate = _conv(g0, cg_ref, row)
        up = _conv(up0, cu_ref, row)
        sg = _sigmoid(gate)
        da = da_ref[...].astype(F32)
        dgate = da * up * (sg * (1.0 + gate * (1.0 - sg)))
        dup = da * (gate * sg)
        conv_bwd(dgate, g0, cg_ref, row, dg0_ref, dut_ref.at[0], dcg_ref)
        conv_bwd(dup, up0, cu_ref, row, du0_ref, dut_ref.at[1], dcu_ref)

    return pl.pallas_call(
        body, name="swiglu_bwd", grid=(ns,), in_specs=[col, col_up, cws, cws_up, col],
        out_specs=[col, col, pl.BlockSpec((2, STRIP, t), lambda j: (0, j, 0)), cws, cws],
        out_shape=[jax.ShapeDtypeStruct((t, w), BF16), jax.ShapeDtypeStruct((t, w), BF16),
                   jax.ShapeDtypeStruct((2, w, t), BF16),
                   jax.ShapeDtypeStruct((SUBLANES, w), F32), jax.ShapeDtypeStruct((SUBLANES, w), F32)],
        compiler_params=_params(("parallel",)),
    )(u0, u0, cw, cw, da)


def _ffn_win_grad(dut, x2b, tn=512):
    t, d = x2b.shape
    sp, sw = FF_SLAB_P, FF_SLAB

    def body(a_ref, b_ref, o_ref, ob_ref):
        res = _dot(a_ref[...], b_ref[...], NN)
        o_ref[...] = res[:sw, :]
        ob_ref[...] = res[:sw, :].astype(BF16)

    o_spec = pl.BlockSpec((None, sw, tn), lambda j, n: (j, 0, n))
    return pl.pallas_call(
        body, name="mm_g_ffn_in", grid=(8, d // tn),
        in_specs=[pl.BlockSpec((None, sp, t), lambda j, n: (j // 4, j % 4, 0)),
                  pl.BlockSpec((t, tn), lambda j, n: (0, n))],
        out_specs=[o_spec, o_spec],
        out_shape=[jax.ShapeDtypeStruct((8, sw, d), F32), jax.ShapeDtypeStruct((8, sw, d), BF16)],
        compiler_params=_params(("parallel", "parallel")),
    )(dut, x2b)


def _tile2d(r, c, limit=1 << 20):
    tr, tc = r, c
    while tr * tc * 4 > limit:
        if tr % (2 * SUBLANES) == 0:
            tr //= 2
        elif tc % (2 * LANES) == 0:
            tc //= 2
        else:
            break
    return tr, tc


def _adamw_math(w, m, v, g):
    c1 = 1.0 - ADAM_B1 ** ADAM_STEP
    c2 = 1.0 - ADAM_B2 ** ADAM_STEP
    mm = ADAM_B1 * m + (1.0 - ADAM_B1) * g
    vv = ADAM_B2 * v + (1.0 - ADAM_B2) * (g * g)
    delta = -ADAM_LR * ((mm / c1) / (jnp.sqrt(vv / c2) + ADAM_EPS) + ADAM_WD * w)
    return delta, mm, vv


def _adamw(w, m, v, g, name):
    r, c = w.shape
    blk = pl.BlockSpec((r, c), lambda i: (0, 0))

    def body(w_ref, m_ref, v_ref, gi_ref, g_ref, d_ref, nm_ref, nv_ref):
        g = gi_ref[...]
        d_ref[...], nm_ref[...], nv_ref[...] = _adamw_math(w_ref[...], m_ref[...], v_ref[...], g)
        g_ref[...] = g

    return pl.pallas_call(body, name=name, grid=(1,), in_specs=[blk] * 4, out_specs=[blk] * 4,
                          out_shape=[jax.ShapeDtypeStruct((r, c), F32)] * 4,
                          compiler_params=_params(("arbitrary",)))(w, m, v, g)


def _pair_add(gs, ra, core, name):
    _, _, r, c = gs.shape
    tr, tc = _tile2d(r, c)
    blk = pl.BlockSpec((None, tr, tc), lambda k, i, j, s: (k, i, j))

    def body(s_ref, g_ref, r_ref, o_ref, ob_ref):
        p = g_ref[...] + r_ref[...]
        o_ref[...] = p
        ob_ref[...] = p.astype(BF16)

    gspec = pltpu.PrefetchScalarGridSpec(
        num_scalar_prefetch=1, grid=(4, r // tr, c // tc),
        in_specs=[pl.BlockSpec((None, None, tr, tc), lambda k, i, j, s: (k, s[0], i, j)), blk], out_specs=[blk, blk])
    return pl.pallas_call(body, name=name, grid_spec=gspec,
                          out_shape=[jax.ShapeDtypeStruct((4, r, c), F32), jax.ShapeDtypeStruct((4, r, c), BF16)],
                          compiler_params=_params(("parallel", "parallel", "parallel")))(core, gs, ra)


def _small_reduce(gathered):
    nd, r, n = gathered.shape
    tn = 2048 if n % 2048 == 0 else n
    def body(g_ref, s_ref, t_ref):
        s = g_ref[0]
        for dv in range(1, nd):
            s = s + g_ref[dv]
        s_ref[...] = s
        t_ref[...] = jnp.broadcast_to(jnp.sum(s, axis=0, keepdims=True), (r, tn))

    return pl.pallas_call(
        body, name="small_reduce", grid=(n // tn,),
        in_specs=[pl.BlockSpec((nd, r, tn), lambda j: (0, 0, j))],
        out_specs=[pl.BlockSpec((r, tn), lambda j: (0, j))] * 2,
        out_shape=[jax.ShapeDtypeStruct((r, n), F32)] * 2, compiler_params=_params(("parallel",)),
    )(gathered)


HBM = pl.BlockSpec(memory_space=pltpu.HBM)


def _all_gather(arrs, name):
    n = len(arrs)

    def body(*refs):
        ins, outs = refs[:n], refs[n:2 * n]
        send, recv, lsem = refs[2 * n:]
        x, y, c = lax.axis_index("x"), lax.axis_index("y"), lax.axis_index("c")
        me, sib = (x, y, c), (x, y, 1 - c)
        chips = [(1 - x, y), (x, 1 - y), (1 - x, 1 - y)]

        def slot(w, p):
            return outs[w].at[4 * p[0] + 2 * p[1] + p[2]]

        def cp(w, k, block, to, src=None):
            return pltpu.make_async_remote_copy(
                src_ref=slot(w, block) if src is None else src, dst_ref=slot(w, block),
                send_sem=send.at[w * 7 + k], recv_sem=recv.at[w * 7 + k], device_id=to, device_id_type=MESH)

        mine = [pltpu.make_async_copy(ins[w], slot(w, me), lsem.at[w]) for w in range(n)]
        for m in mine:
            m.start()
        first = []
        for w in range(n):
            first.append(cp(w, 0, me, sib, src=ins[w]))
            first += [cp(w, 1 + j, me, (*chip, c), src=ins[w]) for j, chip in enumerate(chips)]
        for f in first:
            f.start()
        passed = []
        for j, chip in enumerate(chips):
            for w in range(n):
                cp(w, 1 + j, (*chip, c), me).wait_recv()
                fwd = cp(w, 4 + j, (*chip, c), sib)
                fwd.start()
                passed.append(fwd)
        for w in range(n):
            cp(w, 0, sib, me).wait_recv()
            for j, chip in enumerate(chips):
                cp(w, 4 + j, (*chip, 1 - c), me).wait_recv()
        for f in first + passed:
            f.wait_send()
        for m in mine:
            m.wait()

    return pl.pallas_call(
        body, name=name, in_specs=[HBM] * n, out_specs=[HBM] * n,
        out_shape=[jax.ShapeDtypeStruct((8,) + a.shape, a.dtype) for a in arrs],
        scratch_shapes=[pltpu.SemaphoreType.DMA((7 * n,)), pltpu.SemaphoreType.DMA((7 * n,)),
                        pltpu.SemaphoreType.DMA((n,))],
    )(*arrs)


def _sibling_exchange(arrs, name):
    n = len(arrs)

    def body(*refs):
        ins, outs = refs[:n], refs[n:2 * n]
        send, recv = refs[2 * n:]
        x, y, c = lax.axis_index("x"), lax.axis_index("y"), lax.axis_index("c")
        copies = [pltpu.make_async_remote_copy(
            src_ref=ins[w].at[:, 1 - c], dst_ref=outs[w], send_sem=send.at[w], recv_sem=recv.at[w],
            device_id=(x, y, 1 - c), device_id_type=MESH) for w in range(n)]
        for cpy in copies:
            cpy.start()
        for cpy in copies:
            cpy.wait()

    return pl.pallas_call(
        body, name=name, in_specs=[HBM] * n, out_specs=[HBM] * n,
        out_shape=[jax.ShapeDtypeStruct((a.shape[0],) + a.shape[2:], a.dtype) for a in arrs],
        scratch_shapes=[pltpu.SemaphoreType.DMA((n,)), pltpu.SemaphoreType.DMA((n,))],
    )(*arrs)


def _chip_exchange(arrs, name):
    n = len(arrs)

    def body(*refs):
        ins, outs = refs[:n], refs[n:2 * n]
        send, recv = refs[2 * n:]
        x, y, c = lax.axis_index("x"), lax.axis_index("y"), lax.axis_index("c")
        chips = [(1 - x, y), (x, 1 - y), (1 - x, 1 - y)]
        copies = []
        for w in range(n):
            for j, (cx, cy) in enumerate(chips):
                copies.append(pltpu.make_async_remote_copy(
                    src_ref=ins[w].at[2 * cx + cy], dst_ref=outs[w].at[j], send_sem=send.at[3 * w + j],
                    recv_sem=recv.at[3 * w + j], device_id=(cx, cy, c), device_id_type=MESH))
        for cpy in copies:
            cpy.start()
        for cpy in copies:
            cpy.wait()

    return pl.pallas_call(
        body, name=name, in_specs=[HBM] * n, out_specs=[HBM] * n,
        out_shape=[jax.ShapeDtypeStruct((3,) + a.shape[1:], a.dtype) for a in arrs],
        scratch_shapes=[pltpu.SemaphoreType.DMA((3 * n,)), pltpu.SemaphoreType.DMA((3 * n,))],
    )(*arrs)


SEM = pl.BlockSpec(memory_space=pltpu.SEMAPHORE)
ANY = pl.BlockSpec(memory_space=pl.ANY)
EFFECT = pltpu.SideEffectType.DATAFLOW_SIDE_EFFECTING
N_PEERS = 7


def _peers(x, y, c):
    return [((1 - x) if k & 4 else x, (1 - y) if k & 2 else y, (1 - c) if k & 1 else c) for k in range(1, 8)]


def _spread_copies(src_refs, land_refs, send, recv, gather):
    x, y, c = lax.axis_index("x"), lax.axis_index("y"), lax.axis_index("c")
    me = 4 * x + 2 * y + c
    copies = []
    for w in range(len(src_refs)):
        for k, (px, py, pc) in enumerate(_peers(x, y, c)):
            p = 4 * px + 2 * py + pc
            copies.append((pltpu.make_async_remote_copy(
                src_ref=src_refs[w] if gather else src_refs[w].at[p], dst_ref=land_refs[w].at[me],
                send_sem=send[w].at[k], recv_sem=recv[w].at[k], device_id=(px, py, pc), device_id_type=MESH),
                pltpu.make_async_remote_copy(
                src_ref=src_refs[w] if gather else src_refs[w].at[p], dst_ref=land_refs[w].at[p],
                send_sem=send[w].at[k], recv_sem=recv[w].at[k], device_id=(px, py, pc), device_id_type=MESH)))
    return copies


def _hbm(a):
    return pltpu.with_memory_space_constraint(a, pltpu.HBM)


def _spread_start(srcs, lands, after, gather, name):
    n = len(srcs)

    def body(*refs):
        src_refs, land_refs = refs[:n], refs[n:2 * n]
        outs = refs[2 * n + 1:]
        send, recv, token = outs[:n], outs[n:2 * n], outs[4 * n]
        for start, _ in _spread_copies(src_refs, land_refs, send, recv, gather):
            start.start()
        token[...] = jnp.zeros_like(token)

    res = pl.pallas_call(
        body, name=name,
        out_shape=tuple([pltpu.SemaphoreType.DMA((N_PEERS,))] * (2 * n)
                        + [pltpu.HBM(a.shape, a.dtype) for a in srcs] + [pltpu.HBM(a.shape, a.dtype) for a in lands]
                        + [jax.ShapeDtypeStruct((SUBLANES, LANES), F32)]),
        in_specs=[HBM] * (2 * n) + [ANY],
        out_specs=tuple([SEM] * (2 * n) + [HBM] * (2 * n) + [pl.BlockSpec(memory_space=pltpu.VMEM)]),
        input_output_aliases={i: 2 * n + i for i in range(2 * n)},
        compiler_params=pltpu.CompilerParams(has_side_effects=EFFECT),
    )(*[_hbm(a) for a in srcs], *[_hbm(a) for a in lands], after)
    return res[:n], res[n:2 * n], res[2 * n:3 * n], res[3 * n:4 * n], res[4 * n]


def _spread_wait(send, recv, srcs, lands, after, gather, name):
    n = len(srcs)

    def body(*refs):
        src_refs, land_refs = refs[:n], refs[n:2 * n]
        send_refs, recv_refs = refs[2 * n:3 * n], refs[3 * n:4 * n]
        for _, arrive in _spread_copies(src_refs, land_refs, send_refs, recv_refs, gather):
            arrive.wait_send()
            arrive.wait_recv()

    res = pl.pallas_call(
        body, name=name,
        out_shape=tuple([pltpu.HBM(a.shape, a.dtype) for a in srcs] + [pltpu.HBM(a.shape, a.dtype) for a in lands]),
        in_specs=[HBM] * (2 * n) + [SEM] * (2 * n) + [ANY],
        out_specs=tuple([HBM] * (2 * n)),
        input_output_aliases={i: i for i in range(2 * n)},
        compiler_params=pltpu.CompilerParams(has_side_effects=EFFECT),
    )(*srcs, *lands, *send, *recv, after)
    return res[n:]


def _landing(shape, dtype, own, me):
    return lax.dynamic_update_index_in_dim(lax.empty((8,) + shape, dtype), own, me, 0)


def _pad_cols(a, to):
    return jnp.pad(a, ((0, 0), (0, to - a.shape[1])))


N_GLR = GLA_W + GLA_RANK
FF_SLAB = D_FF // 4
FF_SLAB_P = FFP // 4


TRANSPOSED = ("w_in", "ffn_w_in")


def _prepare_sub1(gath):
    w_in_t = gath["w_in"].reshape(-1, gath["w_in"].shape[2])
    w2 = jnp.concatenate([gath["gla_gate_w2"][s] for s in range(8)], axis=1)
    return {"w_a_t": jnp.pad(w_in_t[:N_GLR], ((0, HA_W - N_GLR), (0, 0))), "w_b_t": w_in_t[N_GLR:],
            "w2p": jnp.pad(w2, ((0, LANES - GLA_RANK), (0, 0)))}


def _prepare_ffn_in(g):
    f = jnp.pad(g, ((0, 0), (0, FF_SLAB_P - FF_SLAB), (0, 0)))
    return f.reshape(2 * FFP, f.shape[2])


def _prepare_ffn_out(g):
    return jnp.pad(g.reshape(4, FF_SLAB, -1), ((0, 0), (0, FF_SLAB_P - FF_SLAB), (0, 0))).reshape(FFP, -1)


def _prepare_conv(g, conv_b):
    padc = FF_SLAB_P - FF_SLAB
    cw = jnp.pad(g, ((0, 0), (0, 0), (0, padc)))
    cb = jnp.pad(conv_b.reshape(8, 1, FF_SLAB), ((0, 0), (0, 0), (0, padc)))
    rows = jnp.concatenate([cw, cb, jnp.zeros((8, 4, FF_SLAB_P), F32)], axis=1)
    return jnp.concatenate([rows[s] for s in range(8)], axis=1)


def _prepare_ffn(gath, conv_b):
    return {"w_ffn_t": _prepare_ffn_in(gath["ffn_w_in"]), "wo": _prepare_ffn_out(gath["ffn_w_out"]),
            "cw": _prepare_conv(gath["ffn_conv_w"], conv_b)}


def _unpad_ff(a):
    r = a.shape[0]
    return a.reshape(r, 4, FF_SLAB_P)[:, :, :FF_SLAB].reshape(r, D_FF)


def _grad_slabs(g):
    w_in_t = jnp.concatenate([g["w_a_t"][:N_GLR], g["w_b_t"]], axis=0)
    s = {"w_in": w_in_t.reshape(4, 2, w_in_t.shape[0] // 8, w_in_t.shape[1])}
    for n in ("w_out", "ca_wq", "ca_wo"):
        s[n] = _to_slabs(n, g[n])
    for n in ("ca_wkv", "ffn_w_in"):
        s[n] = g[n].reshape((4, 2) + g[n].shape[1:])
    wo = g["wo"].reshape(4, FF_SLAB_P, -1)[:, :FF_SLAB]
    s["ffn_w_out"] = wo.reshape(4, 2, FF_SLAB // 2, wo.shape[-1])
    return s


class _AtHand:
    def __init__(self, p):
        self.p = p
        self.token = None

    def sub2(self, after):
        return self.p

    def ffn_in(self, after):
        return self.p["w_ffn_t"]

    def ffn_out(self, after):
        return self.p["wo"]

    def grads_out(self, group, slabs):
        pass


def _local_step(x, mem, positions, target, p, small, stages=None):
    t, d = x.shape
    stages = _AtHand(p) if stages is None else stages
    w_a_t, w_b_t, w2p, cw = p["w_a_t"], p["w_b_t"], p["w2p"], p["cw"]
    tabs = _rope_tables(positions)
    xb = x.astype(BF16) if stages.token is None else (x + stages.token[0, 0]).astype(BF16)
    memb = mem.astype(BF16)

    h_a = _matmul(xb, w_a_t, "nt", F32, 512, 640, d, "mm_h_a")
    h_b = _matmul(xb, w_b_t, "nt", F32, 512, 1024, d, "mm_h_b")
    o_g, o_raw, s_before = _gla_fwd(h_a, w2p, small["gla_gate_b"], small["gla_norm_g"])
    qr, kr = _rope_fwd(h_b, tabs)
    o_d_b, o_d, lse_tot = _dil_fwd_all(qr, kr, h_b)
    mixin = jnp.concatenate([o_g, o_d_b], axis=1)
    wts = stages.sub2(mixin)
    mix = _matmul(mixin, wts["w_out"], "nn", F32, 512, 1024, d, "mm_mix")
    x1, x1b, x1t = _ln_fwd(x, mix, small["ln1_g"], small["ln1_b"], "ln1_fwd")

    q_ca = _matmul(x1b, wts["ca_wq"], "nn", BF16, 512, 1024, d, "mm_caq")
    kvw = wts["ca_wkv"].shape[2]
    memkv = _matmul(memb, wts["ca_wkv"], "nn", BF16, mem.shape[0], kvw, d, "mm_memkv", b_slabs=True)
    o_c, o_ct = _ca_fwd(q_ca, memkv)
    ca_out = _matmul(o_c, wts["ca_wo"], "nn", F32, 512, 1024, d, "mm_cao")
    x2, x2b, x2t = _ln_fwd(x1, ca_out, small["ln2_g"], small["ln2_b"], "ln2_fwd")

    w_ffn_t = stages.ffn_in(x2b)
    u0 = _matmul(x2b, w_ffn_t, "nt", BF16, 512, 512, d, "mm_u0")
    act, act_t = _swiglu_fwd(u0, cw)
    wo = stages.ffn_out(act)
    ffn = _matmul(act, wo, "nn", F32, 512, 512, FFP, "mm_ffn")

    dp3, dp3b, dg3, db3, loss_part = _ln_bwd(x2, ffn, small["ln3_g"], small["ln3_b"], target, True, "ln3_bwd")
    g_wo, g_wo16 = _matmul(act_t, dp3b, "nn", F32, 512, 1024, t // 2, "mm_g_wo", also_bf16=True)
    dact = _matmul(dp3b, wo, "nt", BF16, 512, 512, d, "mm_dact")
    dug, duu, du_t, dcwg, dcwu = _swiglu_bwd(u0, cw, dact)
    g_ffn_in, g_ffn_in16 = _ffn_win_grad(du_t, x2b)

    def wo_slabs(a):
        a = a.reshape(4, FF_SLAB_P, -1)[:, :FF_SLAB]
        return a.reshape(8, FF_SLAB // 2, a.shape[-1])

    sent = stages.grads_out("ffn", {"ffn_w_out": (wo_slabs(g_wo), wo_slabs(g_wo16)), "ffn_w_in": (g_ffn_in, g_ffn_in16)})
    dx2 = _matmul(dug, w_ffn_t, "nn", F32, 512, 512, FFP // 2, "mm_dx2_g", resid=dp3, resid_scale=ALPHA, dep=sent)
    dx2 = _matmul(duu, w_ffn_t, "nn", F32, 512, 512, FFP // 2, "mm_dx2_u", resid=dx2, b_k_off=2)

    dp2, dp2b, dg2, db2 = _ln_bwd(x1, ca_out, small["ln2_g"], small["ln2_b"], dx2, False, "ln2_bwd")
    g_cao, g_cao16 = _matmul(o_ct, dp2b, "nn", F32, 512, 1024, t // 2, "mm_g_cao", also_bf16=True)
    do_c = _matmul(dp2b, wts["ca_wo"], "nt", BF16, 512, 1024, d, "mm_do_c")
    dq_ca, dmemkv = _ca_bwd(q_ca, memkv, do_c)
    g_caq, g_caq16 = _matmul(x1t, dq_ca, "nn", F32, 512, 1024, t // 2, "mm_g_caq", also_bf16=True)
    g_cakv, g_cakv16 = _matmul(memb, dmemkv.astype(BF16), "tn", F32, 512, kvw, mem.shape[0], "mm_g_cakv",
                               out_slabs=True, also_bf16=True)
    dx1 = _matmul(dq_ca, wts["ca_wq"], "nt", F32, 512, 1024, d, "mm_dx1", resid=dp2, resid_scale=ALPHA)

    dp1, dp1b, dg1, db1 = _ln_bwd(x, mix, small["ln1_g"], small["ln1_b"], dx1, False, "ln1_bwd")
    g_wout, g_wout16 = _matmul(mixin, dp1b, "tn", F32, 512, 1024, 1024, "mm_g_wout", also_bf16=True)

    def row_slabs(a):
        return a.reshape(8, a.shape[0] // 8, a.shape[1])

    sent = stages.grads_out("attn", {"ca_wo": (row_slabs(g_cao), row_slabs(g_cao16)),
                                     "ca_wq": (row_slabs(g_caq), row_slabs(g_caq16)), "ca_wkv": (g_cakv, g_cakv16),
                                     "w_out": (row_slabs(g_wout), row_slabs(g_wout16))})
    dmix = _matmul(dp1b, wts["w_out"], "nt", F32, 512, 1024, d, "mm_dmix", dep=sent)
    dh_a, dw2, dgate_b, dnorm_g = _gla_bwd(h_a, w2p, small["gla_gate_b"], small["gla_norm_g"], o_raw, s_before, dmix)
    dq_d, dk_d, dv_d = _dil_bwd_all(qr, kr, h_b, dmix, o_d, lse_tot)
    dh_b = _dil_dh(dq_d, dk_d, dv_d, tabs)
    g_wa_t, g_wa16 = _matmul(dh_a, xb, "tn", F32, 640, 1024, 1024, "mm_g_wa", also_bf16=True)
    g_wb_t, g_wb16 = _matmul(dh_b, xb, "tn", F32, 512, 1024, 1024, "mm_g_wb", also_bf16=True)

    def w_in_slabs(a, b):
        full = jnp.concatenate([a[:N_GLR], b], axis=0)
        return full.reshape(8, full.shape[0] // 8, full.shape[1])

    sent = stages.grads_out("w_in", {"w_in": (w_in_slabs(g_wa_t, g_wb_t), w_in_slabs(g_wa16, g_wb16))})
    dx = _matmul(dh_a, w_a_t, "nn", F32, 512, 512, HA_W, "mm_dx_a", resid=dp1, resid_scale=ALPHA, dep=sent)
    dx = _matmul(dh_b, w_b_t, "nn", F32, 512, 512, HB_W, "mm_dx_b", resid=dx)

    grads = {"w_a_t": g_wa_t, "w_b_t": g_wb_t, "w_out": g_wout, "ca_wq": g_caq, "ca_wkv": g_cakv, "ca_wo": g_cao,
             "ffn_w_in": g_ffn_in, "wo": g_wo}
    small_parts = {
        "gla_gate_b": dgate_b, "gla_norm_g": dnorm_g, "ln1_g": dg1, "ln1_b": db1, "ln2_g": dg2, "ln2_b": db2,
        "ln3_g": dg3, "ln3_b": db3,
        "conv": jnp.concatenate([_unpad_ff(dcwg), _unpad_ff(dcwu)], axis=1),
        "gla_gate_w2": dw2[:GLA_RANK],
    }
    return loss_part, dx, grads, small_parts


BIG = ("w_in", "w_out", "ca_wq", "ca_wkv", "ca_wo", "ffn_w_in", "ffn_w_out")
COL_SHARDED = ("w_in", "ca_wkv", "ffn_w_in")
SMALL_ORDER = ("gla_gate_b", "gla_norm_g", "ln1_g", "ln1_b", "ln2_g", "ln2_b", "ln3_g", "ln3_b")


def _gathered_full(name, g):
    if name in COL_SHARDED:
        return g.transpose(1, 0, 2).reshape(g.shape[1], 8 * g.shape[2])
    return g.reshape(8 * g.shape[1], g.shape[2])


def _to_slabs(name, full):
    if name in COL_SHARDED:
        r, cc = full.shape
        s = full.reshape(r, 8, cc // 8).transpose(1, 0, 2)
    else:
        rr, c = full.shape
        s = full.reshape(8, rr // 8, c)
    return s.reshape((4, 2) + s.shape[1:])


def kernel(x, mem, positions, w_in, gla_gate_w2, gla_gate_b, gla_norm_g, w_out, ln1_g, ln1_b, ca_wq, ca_wkv, ca_wo, ln2_g, ln2_b, ffn_w_in, ffn_conv_w, ffn_conv_b, ffn_w_out, ln3_g, ln3_b, loss_target, m_w_in, m_gla_gate_w2, m_gla_gate_b, m_gla_norm_g, m_w_out, m_ln1_g, m_ln1_b, m_ca_wq, m_ca_wkv, m_ca_wo, m_ln2_g, m_ln2_b, m_ffn_w_in, m_ffn_conv_w, m_ffn_conv_b, m_ffn_w_out, m_ln3_g, m_ln3_b, v_w_in, v_gla_gate_w2, v_gla_gate_b, v_gla_norm_g, v_w_out, v_ln1_g, v_ln1_b, v_ca_wq, v_ca_wkv, v_ca_wo, v_ln2_g, v_ln2_b, v_ffn_w_in, v_ffn_conv_w, v_ffn_conv_b, v_ffn_w_out, v_ln3_g, v_ln3_b):
    weights = dict(w_in=w_in, gla_gate_w2=gla_gate_w2, gla_gate_b=gla_gate_b, gla_norm_g=gla_norm_g, w_out=w_out,
                   ln1_g=ln1_g, ln1_b=ln1_b, ca_wq=ca_wq, ca_wkv=ca_wkv, ca_wo=ca_wo, ln2_g=ln2_g, ln2_b=ln2_b,
                   ffn_w_in=ffn_w_in, ffn_conv_w=ffn_conv_w, ffn_conv_b=ffn_conv_b, ffn_w_out=ffn_w_out,
                   ln3_g=ln3_g, ln3_b=ln3_b)
    moms = dict(w_in=(m_w_in, v_w_in), gla_gate_w2=(m_gla_gate_w2, v_gla_gate_w2), gla_gate_b=(m_gla_gate_b, v_gla_gate_b),
                gla_norm_g=(m_gla_norm_g, v_gla_norm_g), w_out=(m_w_out, v_w_out), ln1_g=(m_ln1_g, v_ln1_g),
                ln1_b=(m_ln1_b, v_ln1_b), ca_wq=(m_ca_wq, v_ca_wq), ca_wkv=(m_ca_wkv, v_ca_wkv), ca_wo=(m_ca_wo, v_ca_wo),
                ln2_g=(m_ln2_g, v_ln2_g), ln2_b=(m_ln2_b, v_ln2_b), ffn_w_in=(m_ffn_w_in, v_ffn_w_in),
                ffn_conv_w=(m_ffn_conv_w, v_ffn_conv_w), ffn_conv_b=(m_ffn_conv_b, v_ffn_conv_b),
                ffn_w_out=(m_ffn_w_out, v_ffn_w_out), ln3_g=(m_ln3_g, v_ln3_g), ln3_b=(m_ln3_b, v_ln3_b))
    order = list(weights)
    xi, yi, ci = lax.axis_index("x"), lax.axis_index("y"), lax.axis_index("c")
    me = 4 * xi + 2 * yi + ci

    def travel(n, a):
        return jnp.swapaxes(a, 1, 2) if n in TRANSPOSED else a

    shard = {n: travel(n, weights[n]).astype(BF16)[0] for n in BIG}
    first = _all_gather([shard["w_in"], gla_gate_w2.astype(BF16)[0], ffn_conv_w[0]], "ag_first")
    p = _prepare_sub1({"w_in": first[0], "gla_gate_w2": first[1]})
    p["cw"] = _prepare_conv(first[2], ffn_conv_b)
    later = ("w_out", "ca_wq", "ca_wkv", "ca_wo", "ffn_w_in", "ffn_w_out")
    srcs = [shard[n] for n in later]
    lands = [_landing(shard[n].shape, BF16, shard[n], me) for n in later]
    send, recv, srcs, lands, token = _spread_start(srcs, lands, first[0], True, "ag_rest_start")

    class stages:
        pass

    stages.token = token

    def arrived(lo, hi, after, name):
        return _spread_wait(send[lo:hi], recv[lo:hi], srcs[lo:hi], lands[lo:hi], after, True, name)

    def sub2(after):
        g = dict(zip(later[:4], arrived(0, 4, after, "ag_wait_attn")))
        w = {n: _gathered_full(n, g[n]) for n in ("w_out", "ca_wq", "ca_wo")}
        w["ca_wkv"] = g["ca_wkv"]
        return w

    stages.sub2 = sub2
    stages.ffn_in = lambda after: _prepare_ffn_in(arrived(4, 5, after, "ag_wait_ffn_in")[0])
    stages.ffn_out = lambda after: _prepare_ffn_out(arrived(5, 6, after, "ag_wait_ffn_out")[0])
    sent = {}

    def grads_out(group, slabs):
        names = list(slabs)
        srcs16 = [slabs[n][1] for n in names]
        zones = [_landing(s.shape[1:], BF16, jnp.zeros(s.shape[1:], BF16), me) for s in srcs16]
        snd, rcv, s_thru, l_thru, tok = _spread_start(srcs16, zones, slabs[names[0]][0], False, f"rs_{group}_start")
        sent[group] = (names, [slabs[n][0] for n in names], (snd, rcv, s_thru, l_thru))
        return tok

    stages.grads_out = grads_out
    small = dict(gla_gate_b=gla_gate_b, gla_norm_g=gla_norm_g, ln1_g=ln1_g, ln1_b=ln1_b, ln2_g=ln2_g, ln2_b=ln2_b,
                 ln3_g=ln3_g, ln3_b=ln3_b)

    loss_part, dx, grads, small_parts = _local_step(x[0], mem[0], positions[0], loss_target[0], p, small, stages)
    loss = lax.psum(jnp.sum(loss_part), ("x", "y", "c"))

    out = {}
    me1 = me.reshape(1).astype(jnp.int32)
    for group, (names, own32, handles) in sent.items():
        landed = _spread_wait(*handles, dx, False, f"rs_{group}_wait")
        for n, own, land in zip(names, own32, landed):
            m_, v_ = moms[n]
            res4 = _adamw_direct(travel(n, weights[n]), travel(n, m_), travel(n, v_), own, land, me1, f"adamw_{n}")
            out[n] = [travel(n, a) for a in res4]


    packed = jnp.concatenate([small_parts[n] for n in SMALL_ORDER] + [small_parts["conv"],
                             small_parts["gla_gate_w2"].reshape(SUBLANES, -1)], axis=1)
    pad = (-packed.shape[1]) % 2048
    packed = jnp.pad(packed, ((0, 0), (0, pad)))
    (allp,) = _all_gather([packed], "ag_small")
    dev_sum, row_sum = _small_reduce(allp)
    off = 0
    for n in SMALL_ORDER:
        width = weights[n].shape[1]
        g = row_sum[0:1, off:off + width]
        off += width
        m_, v_ = moms[n]
        out[n] = _adamw(weights[n], m_, v_, g, f"adamw_{n}")
    conv_g = dev_sum[:, off:off + 2 * D_FF]
    off += 2 * D_FF
    g_cb = conv_g[3:4]
    out["ffn_conv_b"] = _adamw(ffn_conv_b, m_ffn_conv_b, v_ffn_conv_b, g_cb, "adamw_ffn_conv_b")
    wsh = ffn_conv_w.shape[2]
    g_cw = lax.dynamic_slice_in_dim(conv_g[0:3], me * wsh, wsh, axis=1)
    out["ffn_conv_w"] = _adamw(ffn_conv_w[0], m_ffn_conv_w[0], v_ffn_conv_w[0], g_cw, "adamw_ffn_conv_w")
    w2_g = dev_sum[:, off:off + GLA_RANK * GLA_HEADS * GLA_DK // SUBLANES].reshape(GLA_RANK, GLA_HEADS * GLA_DK)
    wsh2 = gla_gate_w2.shape[2]
    g_w2 = lax.dynamic_slice_in_dim(w2_g, me * wsh2, wsh2, axis=1)
    out["gla_gate_w2"] = _adamw(gla_gate_w2[0], m_gla_gate_w2[0], v_gla_gate_w2[0], g_w2, "adamw_gla_gate_w2")

    def shaped(n, a):
        return a.reshape(weights[n].shape)

    res = [loss, dx[None]]
    for k in range(4):
        res += [shaped(n, out[n][k]) for n in order]
    return tuple(res)


def _adamw_direct(w, m, v, own, land, me, name):
    _, r, c = w.shape
    tr, tc = _tile2d(r, c)
    blk = pl.BlockSpec((None, tr, tc), lambda i, j, s: (0, i, j))
    mine = pl.BlockSpec((None, tr, tc), lambda i, j, s: (s[0], i, j))
    slots = [pl.BlockSpec((None, tr, tc), lambda i, j, s, k=k: (k, i, j)) for k in range(8)]

    def body(s_ref, w_ref, m_ref, v_ref, p_ref, *rest):
        slot_refs, (g_ref, d_ref, nm_ref, nv_ref) = rest[:8], rest[8:]
        g = p_ref[...]
        for sr in slot_refs:
            g = g + sr[...].astype(F32)
        d_ref[...], nm_ref[...], nv_ref[...] = _adamw_math(w_ref[...], m_ref[...], v_ref[...], g)
        g_ref[...] = g

    gs = pltpu.PrefetchScalarGridSpec(num_scalar_prefetch=1, grid=(r // tr, c // tc),
                                      in_specs=[blk, blk, blk, mine] + slots, out_specs=[blk] * 4)
    return pl.pallas_call(body, name=name, grid_spec=gs, out_shape=[jax.ShapeDtypeStruct((1, r, c), F32)] * 4,
                          compiler_params=_params(("parallel", "parallel")))(me, w, m, v, own, *([land] * 8))


def _adamw_big(w, m, v, p32, rc, chip, name):
    _, r, c = w.shape
    tr, tc = _tile2d(r, c)
    blk = pl.BlockSpec((None, tr, tc), lambda i, j, s: (0, i, j))
    own = pl.BlockSpec((None, tr, tc), lambda i, j, s: (s[0], i, j))
    others = [pl.BlockSpec((None, tr, tc), lambda i, j, s, k=k: (k, i, j)) for k in range(3)]

    def body(s_ref, w_ref, m_ref, v_ref, p_ref, r0_ref, r1_ref, r2_ref, g_ref, d_ref, nm_ref, nv_ref):
        g = ((p_ref[...] + r0_ref[...].astype(F32)) + r1_ref[...].astype(F32)) + r2_ref[...].astype(F32)
        d_ref[...], nm_ref[...], nv_ref[...] = _adamw_math(w_ref[...], m_ref[...], v_ref[...], g)
        g_ref[...] = g

    gs = pltpu.PrefetchScalarGridSpec(num_scalar_prefetch=1, grid=(r // tr, c // tc),
                                      in_specs=[blk, blk, blk, own] + others, out_specs=[blk] * 4)
    return pl.pallas_call(body, name=name, grid_spec=gs, out_shape=[jax.ShapeDtypeStruct((1, r, c), F32)] * 4,
                          compiler_params=_params(("parallel", "parallel")))(chip, w, m, v, p32, rc, rc, rc)
```

```python
import functools
import math

import jax
import jax.numpy as jnp
from jax import lax
from jax.experimental import pallas as pl
from jax.experimental.pallas import tpu as pltpu

F32 = jnp.float32
BF16 = jnp.bfloat16
MESH = pl.DeviceIdType.MESH

D_MODEL = 2048
LN_EPS = 1e-5
GLA_HEADS = 4
GLA_DV = 256
GLA_DK = 128
GLA_RANK = 16
GLA_TAU = 16.0
GLA_CHUNK = 64
DIL_HD = 128
DIL_HEADS = 8
DIL_BAND = 128
DIL_DILATIONS = (1, 4, 16)
ROPE_THETA = 500000.0
ROPE_DIMS = 32
CA_HEADS = 4
CA_HD = 512
D_FF = 5504
ALPHA = 2.0 ** 0.25
ADAM_LR = 0.001
ADAM_B1 = 0.9
ADAM_B2 = 0.999
ADAM_EPS = 1e-08
ADAM_WD = 0.01
ADAM_STEP = 10

LANES = 128
SUBLANES = 8
VMEM_LIMIT = 56 * 1024 * 1024

GLA_W = 2 * GLA_HEADS * GLA_DK + 2 * GLA_HEADS * GLA_DV
HA_W = GLA_W + LANES
HB_W = 3 * DIL_HEADS * DIL_HD
FFP = 5632
NEG = -1e30


def _params(sem):
    return pltpu.CompilerParams(dimension_semantics=sem, vmem_limit_bytes=VMEM_LIMIT)


def _sigmoid(x):
    return 1.0 / (1.0 + jnp.exp(-x))


def _dot(a, b, dn, precision=None):
    return lax.dot_general(a, b, (dn, ((), ())), preferred_element_type=F32, precision=precision)


NN = ((1,), (0,))
NT = ((1,), (1,))
TN = ((0,), (0,))


def _bf(v):
    return v if v.dtype == BF16 else v.astype(BF16)


def _matmul(a, b, kind, out_dtype, tm, tn, tk, name, resid=None, resid_scale=1.0, b_k_off=0, b_slabs=False,
            out_slabs=False, also_bf16=False, dep=None):
    if b_slabs:
        assert kind != "nt" and b.shape[2] == tn
        k2, n = b.shape[1], b.shape[0] * tn
    elif kind == "nt":
        n, k2 = b.shape
    else:
        k2, n = b.shape
    (k, m) = a.shape if kind == "tn" else a.shape[::-1]
    assert k2 >= k and (k2 == k or not b_slabs) and m % tm == 0 and n % tn == 0 and k % tk == 0, \
        (name, a.shape, b.shape, tm, tn, tk)
    nk = k // tk
    dn = {"nn": NN, "nt": NT, "tn": TN}[kind]
    a_spec = pl.BlockSpec((tk, tm), lambda i, j, kk: (kk, i)) if kind == "tn" else pl.BlockSpec((tm, tk), lambda i, j, kk: (i, kk))
    if b_slabs:
        b_spec = pl.BlockSpec((None, tk, tn), lambda i, j, kk: (j, kk, 0))
    elif kind == "nt":
        b_spec = pl.BlockSpec((tn, tk), lambda i, j, kk: (j, kk + b_k_off))
    else:
        b_spec = pl.BlockSpec((tk, tn), lambda i, j, kk: (kk + b_k_off, j))
    if out_slabs:
        o_spec = pl.BlockSpec((None, tm, tn), lambda i, j, kk: (j, i, 0))
        o_shape = (n // tn, m, tn)
    else:
        o_spec = pl.BlockSpec((tm, tn), lambda i, j, kk: (i, j))
        o_shape = (m, n)
    has_resid = resid is not None

    n_in = 2 + int(has_resid) + int(dep is not None)

    def body(*refs):
        a_ref, b_ref = refs[:2]
        r_ref = refs[2] if has_resid else None
        o_ref = refs[n_in]
        ob_ref = refs[n_in + 1] if also_bf16 else None
        part = _dot(_bf(a_ref[...]), _bf(b_ref[...]), dn)

        def finish(acc):
            if has_resid:
                acc = acc + resid_scale * r_ref[...].astype(F32)
            o_ref[...] = acc.astype(out_dtype)
            if also_bf16:
                ob_ref[...] = acc.astype(BF16)

        if nk == 1:
            finish(part)
        else:
            acc_ref = refs[-1]
            kk = pl.program_id(2)

            @pl.when(kk == 0)
            def _():
                acc_ref[...] = part

            @pl.when(kk > 0)
            def _():
                acc_ref[...] += part

            @pl.when(kk == nk - 1)
            def _():
                finish(acc_ref[...])

    in_specs = [a_spec, b_spec] + ([o_spec] if has_resid else [])
    args = (a, b) + ((resid,) if has_resid else ())
    if dep is not None:
        in_specs.append(pl.BlockSpec((SUBLANES, LANES), lambda i, j, kk: (0, 0)))
        args += (dep,)
    o_struct = jax.ShapeDtypeStruct(o_shape, out_dtype)
    return pl.pallas_call(
        body, name=name, out_shape=[o_struct, jax.ShapeDtypeStruct(o_shape, BF16)] if also_bf16 else o_struct,
        grid=(m // tm, n // tn, nk), in_specs=in_specs, out_specs=[o_spec, o_spec] if also_bf16 else o_spec,
        scratch_shapes=[pltpu.VMEM((tm, tn), F32)] if nk > 1 else [],
        compiler_params=_params(("parallel", "parallel", "arbitrary")),
    )(*args)


def _ln_core(xres, f):
    p = ALPHA * xres + f
    mu = jnp.mean(p, axis=-1, keepdims=True)
    xc = p - mu
    var = jnp.mean(xc * xc, axis=-1, keepdims=True)
    rstd = lax.rsqrt(var + LN_EPS)
    return xc * rstd, rstd


def _rows8(v):
    r, c = v.shape
    return jnp.sum(v.reshape(r // SUBLANES, SUBLANES, c), axis=0)


def _ln_fwd(xres, f, g, b, name, tr=256):
    t, d = xres.shape
    row = pl.BlockSpec((tr, d), lambda i: (i, 0))
    vec = pl.BlockSpec((1, d), lambda i: (0, 0))

    def body(x_ref, f_ref, g_ref, b_ref, y_ref, yb_ref, yt_ref):
        xhat, _ = _ln_core(x_ref[...], f_ref[...])
        y = xhat * g_ref[...] + b_ref[...]
        y_ref[...] = y
        yb = y.astype(BF16)
        yb_ref[...] = yb
        yt_ref[...] = yb.T

    return pl.pallas_call(
        body, name=name, grid=(t // tr,), in_specs=[row, row, vec, vec],
        out_specs=[row, row, pl.BlockSpec((d, tr), lambda i: (0, i))],
        out_shape=[jax.ShapeDtypeStruct((t, d), F32), jax.ShapeDtypeStruct((t, d), BF16),
                   jax.ShapeDtypeStruct((d, t), BF16)],
        compiler_params=_params(("parallel",)),
    )(xres, f, g, b)


def _ln_bwd(xres, f, g, b, dy_or_target, loss_head, name, tr=256):
    t, d = xres.shape
    row = pl.BlockSpec((tr, d), lambda i: (i, 0))
    vec = pl.BlockSpec((1, d), lambda i: (0, 0))
    acc = pl.BlockSpec((SUBLANES, d), lambda i: (0, 0))
    lacc = pl.BlockSpec((SUBLANES, LANES), lambda i: (0, 0))

    def body(x_ref, f_ref, g_ref, b_ref, t_ref, dp_ref, dpb_ref, dg_ref, db_ref, *rest):
        i = pl.program_id(0)
        xhat, rstd = _ln_core(x_ref[...], f_ref[...])
        if loss_head:
            err = xhat * g_ref[...] + b_ref[...] - t_ref[...]
            dy = err * (1.0 / d)
            sq = err * err
            lanes = sq[:, :LANES]
            for kk in range(1, d // LANES):
                lanes = lanes + sq[:, kk * LANES:(kk + 1) * LANES]
            lpart = _rows8(lanes) * (0.5 / d)
        else:
            dy = t_ref[...]
        dxh = dy * g_ref[...]
        m1 = jnp.mean(dxh, axis=-1, keepdims=True)
        m2 = jnp.mean(dxh * xhat, axis=-1, keepdims=True)
        dp = rstd * (dxh - m1 - xhat * m2)
        dp_ref[...] = dp
        dpb_ref[...] = dp.astype(BF16)
        dgp = _rows8(dy * xhat)
        dbp = _rows8(dy)

        @pl.when(i == 0)
        def _():
            dg_ref[...] = dgp
            db_ref[...] = dbp
            if loss_head:
                rest[0][...] = lpart

        @pl.when(i > 0)
        def _():
            dg_ref[...] += dgp
            db_ref[...] += dbp
            if loss_head:
                rest[0][...] += lpart

    out_shape = [jax.ShapeDtypeStruct((t, d), F32), jax.ShapeDtypeStruct((t, d), BF16),
                 jax.ShapeDtypeStruct((SUBLANES, d), F32), jax.ShapeDtypeStruct((SUBLANES, d), F32)]
    out_specs = [row, row, acc, acc]
    if loss_head:
        out_shape.append(jax.ShapeDtypeStruct((SUBLANES, LANES), F32))
        out_specs.append(lacc)
    return pl.pallas_call(
        body, name=name, grid=(t // tr,), in_specs=[row, row, vec, vec, row], out_specs=out_specs,
        out_shape=out_shape, compiler_params=_params(("arbitrary",)),
    )(xres, f, g, b, dy_or_target)


def _gla_gates(glr, w2, gb):
    z = _dot(_bf(glr), w2, NN) + gb
    lg = (jnp.minimum(z, 0.0) - jnp.log(1.0 + jnp.exp(-jnp.abs(z)))) * (1.0 / GLA_TAU)
    c = z.shape[0]
    ri = lax.broadcasted_iota(jnp.int32, (c, c), 0)
    ci = lax.broadcasted_iota(jnp.int32, (c, c), 1)
    tri = (ci <= ri).astype(F32)
    bcum = _dot(tri, lg, NN, precision=lax.Precision.HIGHEST)
    blast = jnp.sum(lg, axis=0, keepdims=True)
    return z, bcum, blast, tri


def _gla_specs(t):
    c = GLA_CHUNK
    return c, t // c


def _gla_fwd(h_a, w2p, gate_b, norm_g):
    t = h_a.shape[0]
    c, n = _gla_specs(t)
    hk, hv = GLA_HEADS * GLA_DK, GLA_HEADS * GLA_DV
    scale = GLA_DK ** -0.5

    def body(q_ref, k_ref, v_ref, r_ref, glr_ref, w2_ref, gb_ref, ng_ref, og_ref, oraw_ref, sb_ref, st_ref):
        i = pl.program_id(0)

        @pl.when(i == 0)
        def _():
            st_ref[...] = jnp.zeros_like(st_ref)

        _, bcum, blast, _ = _gla_gates(glr_ref[...], w2_ref[...], gb_ref[...])
        ri = lax.broadcasted_iota(jnp.int32, (c, c), 0)
        ci = lax.broadcasted_iota(jnp.int32, (c, c), 1)
        causal = ci <= ri
        for h in range(GLA_HEADS):
            ks = slice(h * GLA_DK, (h + 1) * GLA_DK)
            vs = slice(h * GLA_DV, (h + 1) * GLA_DV)
            b_h, bl_h = bcum[:, ks], blast[:, ks]
            q_h, k_h = q_ref[:, ks], k_ref[:, ks]
            v_h = _bf(v_ref[:, vs])
            qi = _bf(q_h * scale * jnp.exp(b_h))
            ki = _bf(k_h * jnp.exp(-b_h))
            ke = _bf(k_h * jnp.exp(bl_h - b_h))
            st = st_ref[h]
            sb_ref[0, h] = st
            a = jnp.where(causal, _dot(qi, ki, NT), 0.0)
            o = _dot(_bf(a), v_h, NN) + _dot(qi, _bf(st), NT)
            st_ref[h] = st * jnp.exp(bl_h) + _dot(v_h, ke, TN)
            oraw_ref[:, vs] = o
            mu = jnp.mean(o, axis=-1, keepdims=True)
            oc = o - mu
            var = jnp.mean(oc * oc, axis=-1, keepdims=True)
            xh = oc * lax.rsqrt(var + LN_EPS)
            r_h = r_ref[:, vs]
            og_ref[:, vs] = (xh * ng_ref[:, vs] * (r_h * _sigmoid(r_h))).astype(BF16)

    return pl.pallas_call(
        body, name="gla_fwd", grid=(n,),
        in_specs=[pl.BlockSpec((c, hk), lambda i: (i, 0)), pl.BlockSpec((c, hk), lambda i: (i, 1)),
                  pl.BlockSpec((c, hv), lambda i: (i, 1)), pl.BlockSpec((c, hv), lambda i: (i, 2)),
                  pl.BlockSpec((c, LANES), lambda i: (i, GLA_W // LANES)),
                  pl.BlockSpec((LANES, hk), lambda i: (0, 0)), pl.BlockSpec((1, hk), lambda i: (0, 0)),
                  pl.BlockSpec((1, hv), lambda i: (0, 0))],
        out_specs=[pl.BlockSpec((c, hv), lambda i: (i, 0)), pl.BlockSpec((c, hv), lambda i: (i, 0)),
                   pl.BlockSpec((1, GLA_HEADS, GLA_DV, GLA_DK), lambda i: (i, 0, 0, 0))],
        out_shape=[jax.ShapeDtypeStruct((t, hv), BF16), jax.ShapeDtypeStruct((t, hv), F32),
                   jax.ShapeDtypeStruct((n, GLA_HEADS, GLA_DV, GLA_DK), F32)],
        scratch_shapes=[pltpu.VMEM((GLA_HEADS, GLA_DV, GLA_DK), F32)],
        compiler_params=_params(("arbitrary",)),
    )(h_a, h_a, h_a, h_a, h_a, w2p, gate_b, norm_g)


def _gla_bwd(h_a, w2p, gate_b, norm_g, o_raw, s_before, dmix):
    t = h_a.shape[0]
    c, n = _gla_specs(t)
    hk, hv = GLA_HEADS * GLA_DK, GLA_HEADS * GLA_DV
    scale = GLA_DK ** -0.5
    rev = lambda i: n - 1 - i

    def body(q_ref, k_ref, v_ref, r_ref, glr_ref, w2_ref, gb_ref, ng_ref, oraw_ref, sb_ref, do_ref,
             dh_ref, dw2_ref, dgb_ref, dng_ref, dst_ref):
        i = pl.program_id(0)

        @pl.when(i == 0)
        def _():
            dst_ref[...] = jnp.zeros_like(dst_ref)

        glr = glr_ref[...]
        z, bcum, blast, tri = _gla_gates(glr, w2_ref[...], gb_ref[...])
        ri = lax.broadcasted_iota(jnp.int32, (c, c), 0)
        ci = lax.broadcasted_iota(jnp.int32, (c, c), 1)
        causal = ci <= ri
        dlg_parts = []
        dng_parts = []
        for h in range(GLA_HEADS):
            ks = slice(h * GLA_DK, (h + 1) * GLA_DK)
            vs = slice(h * GLA_DV, (h + 1) * GLA_DV)
            o = oraw_ref[:, vs]
            mu = jnp.mean(o, axis=-1, keepdims=True)
            oc = o - mu
            var = jnp.mean(oc * oc, axis=-1, keepdims=True)
            rstd = lax.rsqrt(var + LN_EPS)
            xh = oc * rstd
            r_h = r_ref[:, vs]
            sg = _sigmoid(r_h)
            silu = r_h * sg
            dout = do_ref[:, vs]
            ng = ng_ref[:, vs]
            dng_parts.append(_rows8(dout * xh * silu))
            dr = dout * xh * ng * (sg * (1.0 + r_h * (1.0 - sg)))
            dxh = dout * ng * silu
            m1 = jnp.mean(dxh, axis=-1, keepdims=True)
            m2 = jnp.mean(dxh * xh, axis=-1, keepdims=True)
            do_raw = _bf(rstd * (dxh - m1 - xh * m2))
            b_h, bl_h = bcum[:, ks], blast[:, ks]
            q_h, k_h = q_ref[:, ks], k_ref[:, ks]
            v_h = _bf(v_ref[:, vs])
            eb, enb, eend = jnp.exp(b_h), jnp.exp(-b_h), jnp.exp(bl_h - b_h)
            decay = jnp.exp(bl_h)
            qi_f, ki_f, ke_f = q_h * scale * eb, k_h * enb, k_h * eend
            qi, ki, ke = _bf(qi_f), _bf(ki_f), _bf(ke_f)
            st = sb_ref[0, h]
            dst = dst_ref[h]
            dst_b = _bf(dst)
            a = _bf(jnp.where(causal, _dot(qi, ki, NT), 0.0))
            da = _bf(jnp.where(causal, _dot(do_raw, v_h, NT), 0.0))
            dv = _dot(a, do_raw, TN) + _dot(ke, dst_b, NT)
            dqi = _dot(da, ki, NN) + _dot(do_raw, _bf(st), NN)
            dki = _dot(da, qi, TN)
            dke = _dot(v_h, dst_b, NN)
            dst_ref[h] = _dot(do_raw, qi, TN) + dst * decay
            dbl = decay * jnp.sum(st * dst, axis=0, keepdims=True) + jnp.sum(dke * ke_f, axis=0, keepdims=True)
            dbc = dqi * qi_f - dki * ki_f - dke * ke_f
            dlg_parts.append(_dot(tri, dbc, TN, precision=lax.Precision.HIGHEST) + dbl)
            dh_ref[:, ks] = (dqi * eb * scale).astype(BF16)
            dh_ref[:, hk + h * GLA_DK: hk + (h + 1) * GLA_DK] = (dki * enb + dke * eend).astype(BF16)
            dh_ref[:, 2 * hk + h * GLA_DV: 2 * hk + (h + 1) * GLA_DV] = dv.astype(BF16)
            dh_ref[:, 2 * hk + hv + h * GLA_DV: 2 * hk + hv + (h + 1) * GLA_DV] = dr.astype(BF16)
        dlg = jnp.concatenate(dlg_parts, axis=1)
        dz = dlg * (1.0 / GLA_TAU) * _sigmoid(-z)
        dz_b = _bf(dz)
        dh_ref[:, GLA_W:] = _dot(dz_b, w2_ref[...], NT).astype(BF16)
        dw2p = _dot(_bf(glr), dz_b, TN)
        dgbp = _rows8(dz)
        dngp = jnp.concatenate(dng_parts, axis=1)

        @pl.when(i == 0)
        def _():
            dw2_ref[...] = dw2p
            dgb_ref[...] = dgbp
            dng_ref[...] = dngp

        @pl.when(i > 0)
        def _():
            dw2_ref[...] += dw2p
            dgb_ref[...] += dgbp
            dng_ref[...] += dngp

    return pl.pallas_call(
        body, name="gla_bwd", grid=(n,),
        in_specs=[pl.BlockSpec((c, hk), lambda i: (rev(i), 0)), pl.BlockSpec((c, hk), lambda i: (rev(i), 1)),
                  pl.BlockSpec((c, hv), lambda i: (rev(i), 1)), pl.BlockSpec((c, hv), lambda i: (rev(i), 2)),
                  pl.BlockSpec((c, LANES), lambda i: (rev(i), GLA_W // LANES)),
                  pl.BlockSpec((LANES, hk), lambda i: (0, 0)), pl.BlockSpec((1, hk), lambda i: (0, 0)),
                  pl.BlockSpec((1, hv), lambda i: (0, 0)),
                  pl.BlockSpec((c, hv), lambda i: (rev(i), 0)),
                  pl.BlockSpec((1, GLA_HEADS, GLA_DV, GLA_DK), lambda i: (rev(i), 0, 0, 0)),
                  pl.BlockSpec((c, hv), lambda i: (rev(i), 0))],
        out_specs=[pl.BlockSpec((c, HA_W), lambda i: (rev(i), 0)),
                   pl.BlockSpec((LANES, hk), lambda i: (0, 0)),
                   pl.BlockSpec((SUBLANES, hk), lambda i: (0, 0)),
                   pl.BlockSpec((SUBLANES, hv), lambda i: (0, 0))],
        out_shape=[jax.ShapeDtypeStruct((t, HA_W), BF16), jax.ShapeDtypeStruct((LANES, hk), F32),
                   jax.ShapeDtypeStruct((SUBLANES, hk), F32), jax.ShapeDtypeStruct((SUBLANES, hv), F32)],
        scratch_shapes=[pltpu.VMEM((GLA_HEADS, GLA_DV, GLA_DK), F32)],
        compiler_params=_params(("arbitrary",)),
    )(h_a, h_a, h_a, h_a, h_a, w2p, gate_b, norm_g, o_raw, s_before, dmix)


def _rope_tables(positions):
    half = ROPE_DIMS // 2
    inv_freq = ROPE_THETA ** (-jnp.arange(0, ROPE_DIMS, 2, dtype=F32) / ROPE_DIMS)
    ang = positions.astype(F32).reshape(-1, 1) * inv_freq
    cos, sin = jnp.cos(ang), jnp.sin(ang)
    t = cos.shape[0]
    one = jnp.ones((t, DIL_HD - ROPE_DIMS), F32)
    zero = jnp.zeros((t, DIL_HD - ROPE_DIMS), F32)
    zh = jnp.zeros((t, half), F32)
    return (jnp.concatenate([cos, cos, one], axis=1), jnp.concatenate([-sin, zh, zero], axis=1),
            jnp.concatenate([zh, sin, zero], axis=1))


def _rope_apply(x, c, s1, s2):
    half = ROPE_DIMS // 2
    return x * c + pltpu.roll(x, DIL_HD - half, 1) * s1 + pltpu.roll(x, half, 1) * s2


def _rope_apply_t(dy, c, s1, s2):
    half = ROPE_DIMS // 2
    return dy * c + pltpu.roll(dy * s1, half, 1) + pltpu.roll(dy * s2, DIL_HD - half, 1)


def _rope_fwd(h_b, tabs, tr=256):
    t = h_b.shape[0]
    w = DIL_HEADS * DIL_HD
    scale = DIL_HD ** -0.5
    tab = pl.BlockSpec((tr, DIL_HD), lambda i: (i, 0))
    outb = pl.BlockSpec((tr, w), lambda i: (i, 0))

    def body(q_ref, k_ref, c_ref, s1_ref, s2_ref, qo_ref, ko_ref):
        c, s1, s2 = c_ref[...], s1_ref[...], s2_ref[...]
        for h in range(DIL_HEADS):
            hs = slice(h * DIL_HD, (h + 1) * DIL_HD)
            qo_ref[:, hs] = _rope_apply(q_ref[:, hs] * scale, c, s1, s2)
            ko_ref[:, hs] = _rope_apply(k_ref[:, hs], c, s1, s2)

    return pl.pallas_call(
        body, name="rope_fwd", grid=(t // tr,),
        in_specs=[pl.BlockSpec((tr, w), lambda i: (i, 0)), pl.BlockSpec((tr, w), lambda i: (i, 1)), tab, tab, tab],
        out_specs=[outb, outb],
        out_shape=[jax.ShapeDtypeStruct((t, w), F32)] * 2,
        compiler_params=_params(("parallel",)),
    )(h_b, h_b, *tabs)


def _dil_dh(dq, dk, dv, tabs, tr=256):
    t, w = dq.shape
    scale = DIL_HD ** -0.5
    tab = pl.BlockSpec((tr, DIL_HD), lambda i: (i, 0))
    inb = pl.BlockSpec((tr, w), lambda i: (i, 0))

    def body(dq_ref, dk_ref, dv_ref, c_ref, s1_ref, s2_ref, o_ref):
        c, s1, s2 = c_ref[...], s1_ref[...], s2_ref[...]
        for h in range(DIL_HEADS):
            hs = slice(h * DIL_HD, (h + 1) * DIL_HD)
            o_ref[:, h * DIL_HD:(h + 1) * DIL_HD] = (_rope_apply_t(dq_ref[:, hs], c, s1, s2) * scale).astype(BF16)
            o_ref[:, w + h * DIL_HD: w + (h + 1) * DIL_HD] = _rope_apply_t(dk_ref[:, hs], c, s1, s2).astype(BF16)
        o_ref[:, 2 * w:] = dv_ref[...].astype(BF16)

    return pl.pallas_call(
        body, name="dil_dh", grid=(t // tr,), in_specs=[inb] * 3 + [tab] * 3,
        out_specs=pl.BlockSpec((tr, 3 * w), lambda i: (i, 0)),
        out_shape=jax.ShapeDtypeStruct((t, 3 * w), BF16), compiler_params=_params(("parallel",)),
    )(dq, dk, dv, *tabs)


BANDS = 8


def _to_branch(a, d):
    t, w = a.shape
    return a.reshape(t // d, d, w // DIL_HD, DIL_HD).transpose(1, 2, 0, 3).reshape(-1, DIL_HD)


def _from_branch(a, d, t):
    hds = a.shape[0] // t
    return a.reshape(d, hds, t // d, DIL_HD).transpose(2, 0, 1, 3).reshape(t, hds * DIL_HD)


def _band_masks(not_first):
    r = lax.broadcasted_iota(jnp.int32, (DIL_BAND, 2 * DIL_BAND), 0)
    c = lax.broadcasted_iota(jnp.int32, (DIL_BAND, 2 * DIL_BAND), 1)
    nf = jnp.full((DIL_BAND, 2 * DIL_BAND), not_first, jnp.int32)
    look_back = jnp.logical_and(jnp.logical_and(c < DIL_BAND, c >= r), nf > 0)
    own_band = jnp.logical_and(c >= DIL_BAND, (c - DIL_BAND) <= r)
    return jnp.logical_or(look_back, own_band)


def _dil_fwd(q, k, v, nb, name):
    rows = q.shape[0]
    blk = BANDS * DIL_BAND
    steps = rows // blk
    main = pl.BlockSpec((blk, DIL_HD), lambda i: (i, 0))
    prev = pl.BlockSpec((DIL_BAND, DIL_HD), lambda i: (jnp.maximum(i * BANDS - 1, 0), 0))

    def body(q_ref, k_ref, v_ref, kp_ref, vp_ref, o_ref, l_ref):
        i = pl.program_id(0)
        for j in range(BANDS):
            lo, hi = j * DIL_BAND, (j + 1) * DIL_BAND
            if j == 0:
                kcat = jnp.concatenate([kp_ref[...], k_ref[lo:hi, :]], axis=0)
                vcat = jnp.concatenate([vp_ref[...], v_ref[lo:hi, :]], axis=0)
            else:
                kcat = k_ref[lo - DIL_BAND:hi, :]
                vcat = v_ref[lo - DIL_BAND:hi, :]
            not_first = (((i * BANDS + j) % nb) != 0).astype(jnp.int32)
            s = jnp.where(_band_masks(not_first), _dot(q_ref[lo:hi, :], kcat, NT), NEG)
            m = jnp.max(s, axis=-1, keepdims=True)
            p = jnp.exp(s - m)
            den = jnp.sum(p, axis=-1, keepdims=True)
            o_ref[lo:hi, :] = _dot(_bf(p), vcat, NN) / den
            l_ref[lo:hi, :] = jnp.broadcast_to(m + jnp.log(den), (DIL_BAND, DIL_HD))

    return pl.pallas_call(
        body, name=name, grid=(steps,), in_specs=[main, main, main, prev, prev], out_specs=[main, main],
        out_shape=[jax.ShapeDtypeStruct((rows, DIL_HD), F32)] * 2, compiler_params=_params(("parallel",)),
    )(q, k, v, k, v)


def _dil_bwd(q, k, v, do, lse, dd, nb, name):
    rows = q.shape[0]
    blk = BANDS * DIL_BAND
    steps = rows // blk
    last_band = rows // DIL_BAND - 1
    main = pl.BlockSpec((blk, DIL_HD), lambda i: (i, 0))
    prev = pl.BlockSpec((DIL_BAND, DIL_HD), lambda i: (jnp.maximum(i * BANDS - 1, 0), 0))
    nxt = pl.BlockSpec((DIL_BAND, DIL_HD), lambda i: (jnp.minimum(i * BANDS + BANDS, last_band), 0))

    def body(q_ref, k_ref, v_ref, do_ref, l_ref, dd_ref, kp_ref, vp_ref, qn_ref, don_ref, ln_ref, ddn_ref,
             dq_ref, dk_ref, dv_ref, ak_ref, av_ref):
        i = pl.program_id(0)
        ak_ref[...] = jnp.zeros_like(ak_ref)
        av_ref[...] = jnp.zeros_like(av_ref)
        for j in range(BANDS + 1):
            lo, hi = j * DIL_BAND, (j + 1) * DIL_BAND
            if j == 0:
                kcat = jnp.concatenate([kp_ref[...], k_ref[lo:hi, :]], axis=0)
                vcat = jnp.concatenate([vp_ref[...], v_ref[lo:hi, :]], axis=0)
            elif j < BANDS:
                kcat = k_ref[lo - DIL_BAND:hi, :]
                vcat = v_ref[lo - DIL_BAND:hi, :]
            else:
                kcat = jnp.concatenate([k_ref[lo - DIL_BAND:lo, :], k_ref[lo - DIL_BAND:lo, :]], axis=0)
                vcat = jnp.concatenate([v_ref[lo - DIL_BAND:lo, :], v_ref[lo - DIL_BAND:lo, :]], axis=0)
            if j < BANDS:
                qj, doj, lj, ddj = q_ref[lo:hi, :], do_ref[lo:hi, :], l_ref[lo:hi, :], dd_ref[lo:hi, :]
            else:
                qj, doj, lj, ddj = qn_ref[...], don_ref[...], ln_ref[...], ddn_ref[...]
            not_first = (((i * BANDS + j) % nb) != 0).astype(jnp.int32)
            mask = _band_masks(not_first)
            if j == BANDS:
                cidx = lax.broadcasted_iota(jnp.int32, mask.shape, 1)
                mask = jnp.logical_and(mask, cidx < DIL_BAND)
            s = jnp.where(mask, _dot(qj, kcat, NT), NEG)
            p = jnp.exp(s - jnp.concatenate([lj, lj], axis=1))
            dp = _dot(doj, vcat, NT)
            ds = _bf(p * (dp - jnp.concatenate([ddj, ddj], axis=1)))
            if j < BANDS:
                dq_ref[lo:hi, :] = _dot(ds, kcat, NN)
            ak_ref[lo:hi + DIL_BAND, :] += _dot(ds, qj, TN)
            av_ref[lo:hi + DIL_BAND, :] += _dot(_bf(p), doj, TN)
        dk_ref[...] = ak_ref[DIL_BAND:DIL_BAND + blk, :]
        dv_ref[...] = av_ref[DIL_BAND:DIL_BAND + blk, :]

    return pl.pallas_call(
        body, name=name, grid=(steps,),
        in_specs=[main] * 6 + [prev, prev] + [nxt] * 4, out_specs=[main] * 3,
        out_shape=[jax.ShapeDtypeStruct((rows, DIL_HD), F32)] * 3,
        scratch_shapes=[pltpu.VMEM((blk + 2 * DIL_BAND, DIL_HD), F32)] * 2,
        compiler_params=_params(("parallel",)),
    )(q, k, v, do, lse, dd, k, v, q, do, lse, dd)


def _dil_merge(os_, ls_, tr=256):
    t, w = os_[0].shape
    blk = pl.BlockSpec((tr, w), lambda i: (i, 0))

    def body(o1, o2, o3, l1, l2, l3, ob_ref, of_ref, lt_ref):
        a, b, c = l1[...], l2[...], l3[...]
        m = jnp.maximum(jnp.maximum(a, b), c)
        ea, eb, ec = jnp.exp(a - m), jnp.exp(b - m), jnp.exp(c - m)
        den = ea + eb + ec
        o = (ea * o1[...] + eb * o2[...] + ec * o3[...]) / den
        ob_ref[...] = o.astype(BF16)
        of_ref[...] = o
        lt_ref[...] = m + jnp.log(den)

    return pl.pallas_call(
        body, name="dil_merge", grid=(t // tr,), in_specs=[blk] * 6, out_specs=[blk] * 3,
        out_shape=[jax.ShapeDtypeStruct((t, w), BF16), jax.ShapeDtypeStruct((t, w), F32),
                   jax.ShapeDtypeStruct((t, w), F32)],
        compiler_params=_params(("parallel",)),
    )(*os_, *ls_)


def _dil_bwd_prep(dmix, o_d, tr=256):
    t, w = o_d.shape
    blk = pl.BlockSpec((tr, w), lambda i: (i, 0))

    def body(do_ref, o_ref, dob_ref, dd_ref):
        do = do_ref[...]
        prod = do * o_ref[...]
        dob_ref[...] = do.astype(BF16)
        for h in range(DIL_HEADS):
            hs = slice(h * DIL_HD, (h + 1) * DIL_HD)
            dd_ref[:, hs] = jnp.broadcast_to(jnp.sum(prod[:, hs], axis=-1, keepdims=True), (tr, DIL_HD))

    return pl.pallas_call(
        body, name="dil_bwd_prep", grid=(t // tr,),
        in_specs=[pl.BlockSpec((tr, w), lambda i: (i, 1)), blk], out_specs=[blk, blk],
        out_shape=[jax.ShapeDtypeStruct((t, w), BF16), jax.ShapeDtypeStruct((t, w), F32)],
        compiler_params=_params(("parallel",)),
    )(dmix, o_d)


def _gather_rows(dst_ref, src_ref, t, d, cast=None):
    n = t // d
    for r in range(d):
        v = src_ref[pl.ds(r, n, stride=d), :] if d > 1 else src_ref[...]
        dst_ref[r * n:(r + 1) * n, :] = v if cast is None else v.astype(cast)


def _tri_mask():
    r = lax.broadcasted_iota(jnp.int32, (DIL_BAND, DIL_BAND), 0)
    c = lax.broadcasted_iota(jnp.int32, (DIL_BAND, DIL_BAND), 1)
    return c <= r


def _dil_fwd_all(qr, kr, h_b):
    t = qr.shape[0]
    nbands = t // DIL_BAND
    nbr = len(DIL_DILATIONS)
    hoff = DIL_HEADS

    def col(off):
        return pl.BlockSpec((t, DIL_HD), lambda h: (0, off + h), pipeline_mode=pl.Buffered(1))

    outb = pl.BlockSpec((t, DIL_HD), lambda h: (0, h))

    def body(q_ref, k_ref, v_ref, ob_ref, of_ref, lt_ref, qs, ks, vs, os_, ls_, *br):
        obr, lbr = br[:nbr], br[nbr:]
        for bi, d in enumerate(DIL_DILATIONS):
            n = t // d
            nb = n // DIL_BAND
            _gather_rows(qs, q_ref, t, d, BF16)
            _gather_rows(ks, k_ref, t, d, BF16)
            _gather_rows(vs, v_ref, t, d, BF16)
            s = jnp.where(_tri_mask(), _dot(qs[0:DIL_BAND, :], ks[0:DIL_BAND, :], NT), NEG)
            m = jnp.max(s, axis=-1, keepdims=True)
            pr = jnp.exp(s - m)
            den = jnp.sum(pr, axis=-1, keepdims=True)
            os_[0:DIL_BAND, :] = _dot(_bf(pr), vs[0:DIL_BAND, :], NN) / den
            ls_[0:DIL_BAND, :] = jnp.broadcast_to(m + jnp.log(den), (DIL_BAND, DIL_HD))

            def band(b, carry, nb=nb):
                st = pl.multiple_of((b - 1) * DIL_BAND, DIL_BAND)
                cur = pl.ds(st + DIL_BAND, DIL_BAND)
                both = pl.ds(st, 2 * DIL_BAND)
                not_first = ((b % nb) != 0).astype(jnp.int32)
                s = jnp.where(_band_masks(not_first), _dot(qs[cur, :], ks[both, :], NT), NEG)
                m = jnp.max(s, axis=-1, keepdims=True)
                pr = jnp.exp(s - m)
                den = jnp.sum(pr, axis=-1, keepdims=True)
                os_[cur, :] = _dot(_bf(pr), vs[both, :], NN) / den
                ls_[cur, :] = jnp.broadcast_to(m + jnp.log(den), (DIL_BAND, DIL_HD))
                return carry

            lax.fori_loop(1, nbands, band, 0, unroll=4)
            for r in range(d):
                dst = pl.ds(r, n, stride=d) if d > 1 else slice(None)
                obr[bi][dst, :] = os_[r * n:(r + 1) * n, :]
                lbr[bi][dst, :] = ls_[r * n:(r + 1) * n, :]
        rows = 512
        for c0 in range(0, t, rows):
            sl = slice(c0, c0 + rows)
            la, lb, lc = lbr[0][sl, :], lbr[1][sl, :], lbr[2][sl, :]
            m = jnp.maximum(jnp.maximum(la, lb), lc)
            ea, eb, ec = jnp.exp(la - m), jnp.exp(lb - m), jnp.exp(lc - m)
            den = ea + eb + ec
            o = (ea * obr[0][sl, :] + eb * obr[1][sl, :] + ec * obr[2][sl, :]) / den
            ob_ref[sl, :] = o.astype(BF16)
            of_ref[sl, :] = o
            lt_ref[sl, :] = m + jnp.log(den)

    w = DIL_HEADS * DIL_HD
    vm = lambda dt: pltpu.VMEM((t, DIL_HD), dt)
    return pl.pallas_call(
        body, name="dil_fwd", grid=(DIL_HEADS,), in_specs=[col(0), col(0), col(2 * hoff)],
        out_specs=[outb, outb, outb],
        out_shape=[jax.ShapeDtypeStruct((t, w), BF16), jax.ShapeDtypeStruct((t, w), F32),
                   jax.ShapeDtypeStruct((t, w), F32)],
        scratch_shapes=[vm(BF16)] * 3 + [vm(F32)] * 2 + [vm(F32)] * (2 * nbr),
        compiler_params=_params(("parallel",)),
    )(qr, kr, h_b)


def _dil_bwd_all(qr, kr, h_b, dmix, o_d, lse_tot):
    t = qr.shape[0]
    nbands = t // DIL_BAND
    hoff = DIL_HEADS

    def col(off):
        return pl.BlockSpec((t, DIL_HD), lambda h: (0, off + h), pipeline_mode=pl.Buffered(1))

    outb = pl.BlockSpec((t, DIL_HD), lambda h: (0, h))

    def body(q_ref, k_ref, v_ref, do_ref, o_ref, l_ref, dq_ref, dk_ref, dv_ref,
             qs, ks, vs, dos, lss, dds, dqs, acck, accv):
        for bi, d in enumerate(DIL_DILATIONS):
            n = t // d
            nb = n // DIL_BAND
            _gather_rows(qs, q_ref, t, d, BF16)
            _gather_rows(ks, k_ref, t, d, BF16)
            _gather_rows(vs, v_ref, t, d, BF16)
            _gather_rows(dos, do_ref, t, d, BF16)
            _gather_rows(lss, l_ref, t, d)
            for r in range(d):
                src = pl.ds(r, n, stride=d) if d > 1 else slice(None)
                prod = do_ref[src, :] * o_ref[src, :]
                dds[r * n:(r + 1) * n, :] = jnp.broadcast_to(jnp.sum(prod, axis=-1, keepdims=True), (n, DIL_HD))
            acck[...] = jnp.zeros_like(acck)
            accv[...] = jnp.zeros_like(accv)
            b0 = slice(0, DIL_BAND)
            s = jnp.where(_tri_mask(), _dot(qs[b0, :], ks[b0, :], NT), NEG)
            pr = jnp.exp(s - lss[b0, :])
            ds = _bf(pr * (_dot(dos[b0, :], vs[b0, :], NT) - dds[b0, :]))
            dqs[b0, :] = _dot(ds, ks[b0, :], NN)
            acck[DIL_BAND:2 * DIL_BAND, :] += _dot(ds, qs[b0, :], TN)
            accv[DIL_BAND:2 * DIL_BAND, :] += _dot(_bf(pr), dos[b0, :], TN)

            def band(b, carry, nb=nb):
                st = pl.multiple_of((b - 1) * DIL_BAND, DIL_BAND)
                cur = pl.ds(st + DIL_BAND, DIL_BAND)
                both = pl.ds(st, 2 * DIL_BAND)
                acc_rows = pl.ds(st + DIL_BAND, 2 * DIL_BAND)
                not_first = ((b % nb) != 0).astype(jnp.int32)
                qb, dob, lb, ddb = qs[cur, :], dos[cur, :], lss[cur, :], dds[cur, :]
                kcat, vcat = ks[both, :], vs[both, :]
                s = jnp.where(_band_masks(not_first), _dot(qb, kcat, NT), NEG)
                pr = jnp.exp(s - jnp.concatenate([lb, lb], axis=1))
                ds = _bf(pr * (_dot(dob, vcat, NT) - jnp.concatenate([ddb, ddb], axis=1)))
                dqs[cur, :] = _dot(ds, kcat, NN)
                acck[acc_rows, :] += _dot(ds, qb, TN)
                accv[acc_rows, :] += _dot(_bf(pr), dob, TN)
                return carry

            lax.fori_loop(1, nbands, band, 0, unroll=4)
            for r in range(d):
                lo = r * n
                if d == 1:
                    dq_ref[...] = dqs[...]
                    dk_ref[...] = acck[DIL_BAND:DIL_BAND + t, :]
                    dv_ref[...] = accv[DIL_BAND:DIL_BAND + t, :]
                else:
                    dst = pl.ds(r, n, stride=d)
                    dq_ref[dst, :] = dq_ref[dst, :] + dqs[lo:lo + n, :]
                    dk_ref[dst, :] = dk_ref[dst, :] + acck[DIL_BAND + lo:DIL_BAND + lo + n, :]
                    dv_ref[dst, :] = dv_ref[dst, :] + accv[DIL_BAND + lo:DIL_BAND + lo + n, :]

    w = DIL_HEADS * DIL_HD
    vm = lambda dt, extra=0: pltpu.VMEM((t + extra, DIL_HD), dt)
    return pl.pallas_call(
        body, name="dil_bwd", grid=(DIL_HEADS,),
        in_specs=[col(0), col(0), col(2 * hoff), col(hoff), col(0), col(0)], out_specs=[outb] * 3,
        out_shape=[jax.ShapeDtypeStruct((t, w), F32)] * 3,
        scratch_shapes=[vm(BF16)] * 4 + [vm(F32)] * 3 + [vm(F32, DIL_BAND)] * 2,
        compiler_params=_params(("parallel",)),
    )(qr, kr, h_b, dmix, o_d, lse_tot)


def _ca_fwd(q, memkv, tq=512):
    t, d = q.shape
    m = memkv.shape[0]
    scale = CA_HD ** -0.5

    def body(q_ref, k_ref, v_ref, o_ref, ot_ref):
        for h in range(CA_HEADS):
            hs = slice(h * CA_HD, (h + 1) * CA_HD)
            s = _dot(q_ref[:, hs], k_ref[:, hs], NT) * scale
            p = jnp.exp(s - jnp.max(s, axis=-1, keepdims=True))
            p = p / jnp.sum(p, axis=-1, keepdims=True)
            o = _dot(_bf(p), v_ref[:, hs], NN).astype(BF16)
            o_ref[:, hs] = o
            ot_ref[hs, :] = o.T

    return pl.pallas_call(
        body, name="ca_fwd", grid=(t // tq,),
        in_specs=[pl.BlockSpec((tq, d), lambda i: (i, 0)), pl.BlockSpec((m, d), lambda i: (0, 0)),
                  pl.BlockSpec((m, d), lambda i: (0, 1))],
        out_specs=[pl.BlockSpec((tq, d), lambda i: (i, 0)), pl.BlockSpec((d, tq), lambda i: (0, i))],
        out_shape=[jax.ShapeDtypeStruct((t, d), BF16), jax.ShapeDtypeStruct((d, t), BF16)],
        compiler_params=_params(("parallel",)),
    )(q, memkv, memkv)


def _ca_bwd(q, memkv, do, tq=512):
    t, d = q.shape
    m = memkv.shape[0]
    scale = CA_HD ** -0.5

    def body(q_ref, k_ref, v_ref, do_ref, dq_ref, dkv_ref):
        i = pl.program_id(0)

        @pl.when(i == 0)
        def _():
            dkv_ref[...] = jnp.zeros_like(dkv_ref)

        for h in range(CA_HEADS):
            hs = slice(h * CA_HD, (h + 1) * CA_HD)
            q_h, k_h, v_h, do_h = q_ref[:, hs], k_ref[:, hs], v_ref[:, hs], do_ref[:, hs]
            s = _dot(q_h, k_h, NT) * scale
            p = jnp.exp(s - jnp.max(s, axis=-1, keepdims=True))
            p = p / jnp.sum(p, axis=-1, keepdims=True)
            dp = _dot(do_h, v_h, NT)
            ds = _bf(p * (dp - jnp.sum(p * dp, axis=-1, keepdims=True)) * scale)
            dq_ref[:, hs] = _dot(ds, k_h, NN).astype(BF16)
            dkv_ref[:, hs] += _dot(ds, q_h, TN)
            dkv_ref[:, d + h * CA_HD: d + (h + 1) * CA_HD] += _dot(_bf(p), do_h, TN)

    return pl.pallas_call(
        body, name="ca_bwd", grid=(t // tq,),
        in_specs=[pl.BlockSpec((tq, d), lambda i: (i, 0)), pl.BlockSpec((m, d), lambda i: (0, 0)),
                  pl.BlockSpec((m, d), lambda i: (0, 1)), pl.BlockSpec((tq, d), lambda i: (i, 0))],
        out_specs=[pl.BlockSpec((tq, d), lambda i: (i, 0)), pl.BlockSpec((m, 2 * d), lambda i: (0, 0))],
        out_shape=[jax.ShapeDtypeStruct((t, d), BF16), jax.ShapeDtypeStruct((m, 2 * d), F32)],
        compiler_params=_params(("arbitrary",)),
    )(q, memkv, memkv, do)


STRIP = 256


def _shift_down(u, n, row):
    return jnp.where(row >= n, pltpu.roll(u, n, 0), 0.0)


def _shift_up(u, n, row):
    t = u.shape[0]
    return jnp.where(row < t - n, pltpu.roll(u, t - n, 0), 0.0)


def _conv(u, cw_ref, row):
    return ((cw_ref[3:4, :] + cw_ref[0:1, :] * _shift_down(u, 2, row)) + cw_ref[1:2, :] * _shift_down(u, 1, row)) \
        + cw_ref[2:3, :] * u


def _swiglu_fwd(u0, cw):
    t, w = u0.shape[0], u0.shape[1] // 2
    ns = w // STRIP
    col = pl.BlockSpec((t, STRIP), lambda j: (0, j))
    col_up = pl.BlockSpec((t, STRIP), lambda j: (0, ns + j))
    cws = pl.BlockSpec((SUBLANES, STRIP), lambda j: (0, j))
    cws_up = pl.BlockSpec((SUBLANES, STRIP), lambda j: (0, ns + j))

    def body(g_ref, u_ref, cg_ref, cu_ref, a_ref, at_ref):
        row = lax.broadcasted_iota(jnp.int32, (t, STRIP), 0)
        gate = _conv(g_ref[...].astype(F32), cg_ref, row)
        up = _conv(u_ref[...].astype(F32), cu_ref, row)
        act = (gate * _sigmoid(gate) * up).astype(BF16)
        a_ref[...] = act
        at_ref[...] = act.T

    return pl.pallas_call(
        body, name="swiglu_fwd", grid=(ns,), in_specs=[col, col_up, cws, cws_up],
        out_specs=[col, pl.BlockSpec((STRIP, t), lambda j: (j, 0))],
        out_shape=[jax.ShapeDtypeStruct((t, w), BF16), jax.ShapeDtypeStruct((w, t), BF16)],
        compiler_params=_params(("parallel",)),
    )(u0, u0, cw, cw)


def _swiglu_bwd(u0, cw, da):
    t, w = u0.shape[0], u0.shape[1] // 2
    ns = w // STRIP
    col = pl.BlockSpec((t, STRIP), lambda j: (0, j))
    col_up = pl.BlockSpec((t, STRIP), lambda j: (0, ns + j))
    cws = pl.BlockSpec((SUBLANES, STRIP), lambda j: (0, j))
    cws_up = pl.BlockSpec((SUBLANES, STRIP), lambda j: (0, ns + j))

    def conv_bwd(du, u0, cw_ref, row, du0_ref, du0t_ref, dcw_ref):
        du0 = (cw_ref[2:3, :] * du + cw_ref[1:2, :] * _shift_up(du, 1, row)) + cw_ref[0:1, :] * _shift_up(du, 2, row)
        du0 = du0.astype(BF16)
        du0_ref[...] = du0
        du0t_ref[...] = du0.T
        dcw_ref[0:1, :] = jnp.sum(du * _shift_down(u0, 2, row), axis=0, keepdims=True)
        dcw_ref[1:2, :] = jnp.sum(du * _shift_down(u0, 1, row), axis=0, keepdims=True)
        dcw_ref[2:3, :] = jnp.sum(du * u0, axis=0, keepdims=True)
        dcw_ref[3:4, :] = jnp.sum(du, axis=0, keepdims=True)
        dcw_ref[4:8, :] = jnp.zeros((4, STRIP), F32)

    def body(g_ref, u_ref, cg_ref, cu_ref, da_ref, dg0_ref, du0_ref, dut_ref, dcg_ref, dcu_ref):
        row = lax.broadcasted_iota(jnp.int32, (t, STRIP), 0)
        g0, up0 = g_ref[...].astype(F32), u_ref[...].astype(F32)
        gate = _conv(g0, cg_ref, row)
        up = _conv(up0, cu_ref, row)
        sg = _sigmoid(gate)
        da = da_ref[...].astype(F32)
        dgate = da * up * (sg * (1.0 + gate * (1.0 - sg)))
        dup = da * (gate * sg)
        conv_bwd(dgate, g0, cg_ref, row, dg0_ref, dut_ref.at[0], dcg_ref)
        conv_bwd(dup, up0, cu_ref, row, du0_ref, dut_ref.at[1], dcu_ref)

    return pl.pallas_call(
        body, name="swiglu_bwd", grid=(ns,), in_specs=[col, col_up, cws, cws_up, col],
        out_specs=[col, col, pl.BlockSpec((2, STRIP, t), lambda j: (0, j, 0)), cws, cws],
        out_shape=[jax.ShapeDtypeStruct((t, w), BF16), jax.ShapeDtypeStruct((t, w), BF16),
                   jax.ShapeDtypeStruct((2, w, t), BF16),
                   jax.ShapeDtypeStruct((SUBLANES, w), F32), jax.ShapeDtypeStruct((SUBLANES, w), F32)],
        compiler_params=_params(("parallel",)),
    )(u0, u0, cw, cw, da)


def _ffn_win_grad(dut, x2b, tn=512):
    t, d = x2b.shape
    sp, sw = FF_SLAB_P, FF_SLAB

    def body(a_ref, b_ref, o_ref, ob_ref):
        res = _dot(a_ref[...], b_ref[...], NN)
        o_ref[...] = res[:sw, :]
        ob_ref[...] = res[:sw, :].astype(BF16)

    o_spec = pl.BlockSpec((None, sw, tn), lambda j, n: (j, 0, n))
    return pl.pallas_call(
        body, name="mm_g_ffn_in", grid=(8, d // tn),
        in_specs=[pl.BlockSpec((None, sp, t), lambda j, n: (j // 4, j % 4, 0)),
                  pl.BlockSpec((t, tn), lambda j, n: (0, n))],
        out_specs=[o_spec, o_spec],
        out_shape=[jax.ShapeDtypeStruct((8, sw, d), F32), jax.ShapeDtypeStruct((8, sw, d), BF16)],
        compiler_params=_params(("parallel", "parallel")),
    )(dut, x2b)


def _tile2d(r, c, limit=1 << 20):
    tr, tc = r, c
    while tr * tc * 4 > limit:
        if tr % (2 * SUBLANES) == 0:
            tr //= 2
        elif tc % (2 * LANES) == 0:
            tc //= 2
        else:
            break
    return tr, tc


def _adamw_math(w, m, v, g):
    c1 = 1.0 - ADAM_B1 ** ADAM_STEP
    c2 = 1.0 - ADAM_B2 ** ADAM_STEP
    mm = ADAM_B1 * m + (1.0 - ADAM_B1) * g
    vv = ADAM_B2 * v + (1.0 - ADAM_B2) * (g * g)
    delta = -ADAM_LR * ((mm / c1) / (jnp.sqrt(vv / c2) + ADAM_EPS) + ADAM_WD * w)
    return delta, mm, vv


def _adamw(w, m, v, g, name):
    r, c = w.shape
    blk = pl.BlockSpec((r, c), lambda i: (0, 0))

    def body(w_ref, m_ref, v_ref, gi_ref, g_ref, d_ref, nm_ref, nv_ref):
        g = gi_ref[...]
        d_ref[...], nm_ref[...], nv_ref[...] = _adamw_math(w_ref[...], m_ref[...], v_ref[...], g)
        g_ref[...] = g

    return pl.pallas_call(body, name=name, grid=(1,), in_specs=[blk] * 4, out_specs=[blk] * 4,
                          out_shape=[jax.ShapeDtypeStruct((r, c), F32)] * 4,
                          compiler_params=_params(("arbitrary",)))(w, m, v, g)


def _pair_add(gs, ra, core, name):
    _, _, r, c = gs.shape
    tr, tc = _tile2d(r, c)
    blk = pl.BlockSpec((None, tr, tc), lambda k, i, j, s: (k, i, j))

    def body(s_ref, g_ref, r_ref, o_ref, ob_ref):
        p = g_ref[...] + r_ref[...]
        o_ref[...] = p
        ob_ref[...] = p.astype(BF16)

    gspec = pltpu.PrefetchScalarGridSpec(
        num_scalar_prefetch=1, grid=(4, r // tr, c // tc),
        in_specs=[pl.BlockSpec((None, None, tr, tc), lambda k, i, j, s: (k, s[0], i, j)), blk], out_specs=[blk, blk])
    return pl.pallas_call(body, name=name, grid_spec=gspec,
                          out_shape=[jax.ShapeDtypeStruct((4, r, c), F32), jax.ShapeDtypeStruct((4, r, c), BF16)],
                          compiler_params=_params(("parallel", "parallel", "parallel")))(core, gs, ra)


def _small_reduce(gathered):
    nd, r, n = gathered.shape
    tn = 2048 if n % 2048 == 0 else n
    def body(g_ref, s_ref, t_ref):
        s = g_ref[0]
        for dv in range(1, nd):
            s = s + g_ref[dv]
        s_ref[...] = s
        t_ref[...] = jnp.broadcast_to(jnp.sum(s, axis=0, keepdims=True), (r, tn))

    return pl.pallas_call(
        body, name="small_reduce", grid=(n // tn,),
        in_specs=[pl.BlockSpec((nd, r, tn), lambda j: (0, 0, j))],
        out_specs=[pl.BlockSpec((r, tn), lambda j: (0, j))] * 2,
        out_shape=[jax.ShapeDtypeStruct((r, n), F32)] * 2, compiler_params=_params(("parallel",)),
    )(gathered)


HBM = pl.BlockSpec(memory_space=pltpu.HBM)


def _all_gather(arrs, name):
    n = len(arrs)

    def body(*refs):
        ins, outs = refs[:n], refs[n:2 * n]
        send, recv, lsem = refs[2 * n:]
        x, y, c = lax.axis_index("x"), lax.axis_index("y"), lax.axis_index("c")
        me, sib = (x, y, c), (x, y, 1 - c)
        chips = [(1 - x, y), (x, 1 - y), (1 - x, 1 - y)]

        def slot(w, p):
            return outs[w].at[4 * p[0] + 2 * p[1] + p[2]]

        def cp(w, k, block, to, src=None):
            return pltpu.make_async_remote_copy(
                src_ref=slot(w, block) if src is None else src, dst_ref=slot(w, block),
                send_sem=send.at[w * 7 + k], recv_sem=recv.at[w * 7 + k], device_id=to, device_id_type=MESH)

        mine = [pltpu.make_async_copy(ins[w], slot(w, me), lsem.at[w]) for w in range(n)]
        for m in mine:
            m.start()
        first = []
        for w in range(n):
            first.append(cp(w, 0, me, sib, src=ins[w]))
            first += [cp(w, 1 + j, me, (*chip, c), src=ins[w]) for j, chip in enumerate(chips)]
        for f in first:
            f.start()
        passed = []
        for j, chip in enumerate(chips):
            for w in range(n):
                cp(w, 1 + j, (*chip, c), me).wait_recv()
                fwd = cp(w, 4 + j, (*chip, c), sib)
                fwd.start()
                passed.append(fwd)
        for w in range(n):
            cp(w, 0, sib, me).wait_recv()
            for j, chip in enumerate(chips):
                cp(w, 4 + j, (*chip, 1 - c), me).wait_recv()
        for f in first + passed:
            f.wait_send()
        for m in mine:
            m.wait()

    return pl.pallas_call(
        body, name=name, in_specs=[HBM] * n, out_specs=[HBM] * n,
        out_shape=[jax.ShapeDtypeStruct((8,) + a.shape, a.dtype) for a in arrs],
        scratch_shapes=[pltpu.SemaphoreType.DMA((7 * n,)), pltpu.SemaphoreType.DMA((7 * n,)),
                        pltpu.SemaphoreType.DMA((n,))],
    )(*arrs)


def _sibling_exchange(arrs, name):
    n = len(arrs)

    def body(*refs):
        ins, outs = refs[:n], refs[n:2 * n]
        send, recv = refs[2 * n:]
        x, y, c = lax.axis_index("x"), lax.axis_index("y"), lax.axis_index("c")
        copies = [pltpu.make_async_remote_copy(
            src_ref=ins[w].at[:, 1 - c], dst_ref=outs[w], send_sem=send.at[w], recv_sem=recv.at[w],
            device_id=(x, y, 1 - c), device_id_type=MESH) for w in range(n)]
        for cpy in copies:
            cpy.start()
        for cpy in copies:
            cpy.wait()

    return pl.pallas_call(
        body, name=name, in_specs=[HBM] * n, out_specs=[HBM] * n,
        out_shape=[jax.ShapeDtypeStruct((a.shape[0],) + a.shape[2:], a.dtype) for a in arrs],
        scratch_shapes=[pltpu.SemaphoreType.DMA((n,)), pltpu.SemaphoreType.DMA((n,))],
    )(*arrs)


def _chip_exchange(arrs, name):
    n = len(arrs)

    def body(*refs):
        ins, outs = refs[:n], refs[n:2 * n]
        send, recv = refs[2 * n:]
        x, y, c = lax.axis_index("x"), lax.axis_index("y"), lax.axis_index("c")
        chips = [(1 - x, y), (x, 1 - y), (1 - x, 1 - y)]
        copies = []
        for w in range(n):
            for j, (cx, cy) in enumerate(chips):
                copies.append(pltpu.make_async_remote_copy(
                    src_ref=ins[w].at[2 * cx + cy], dst_ref=outs[w].at[j], send_sem=send.at[3 * w + j],
                    recv_sem=recv.at[3 * w + j], device_id=(cx, cy, c), device_id_type=MESH))
        for cpy in copies:
            cpy.start()
        for cpy in copies:
            cpy.wait()

    return pl.pallas_call(
        body, name=name, in_specs=[HBM] * n, out_specs=[HBM] * n,
        out_shape=[jax.ShapeDtypeStruct((3,) + a.shape[1:], a.dtype) for a in arrs],
        scratch_shapes=[pltpu.SemaphoreType.DMA((3 * n,)), pltpu.SemaphoreType.DMA((3 * n,))],
    )(*arrs)


SEM = pl.BlockSpec(memory_space=pltpu.SEMAPHORE)
ANY = pl.BlockSpec(memory_space=pl.ANY)
EFFECT = pltpu.SideEffectType.DATAFLOW_SIDE_EFFECTING
N_PEERS = 7


def _peers(x, y, c):
    return [((1 - x) if k & 4 else x, (1 - y) if k & 2 else y, (1 - c) if k & 1 else c) for k in range(1, 8)]


def _spread_copies(src_refs, land_refs, send, recv, gather):
    x, y, c = lax.axis_index("x"), lax.axis_index("y"), lax.axis_index("c")
    me = 4 * x + 2 * y + c
    copies = []
    for w in range(len(src_refs)):
        for k, (px, py, pc) in enumerate(_peers(x, y, c)):
            p = 4 * px + 2 * py + pc
            copies.append((pltpu.make_async_remote_copy(
                src_ref=src_refs[w] if gather else src_refs[w].at[p], dst_ref=land_refs[w].at[me],
                send_sem=send[w].at[k], recv_sem=recv[w].at[k], device_id=(px, py, pc), device_id_type=MESH),
                pltpu.make_async_remote_copy(
                src_ref=src_refs[w] if gather else src_refs[w].at[p], dst_ref=land_refs[w].at[p],
                send_sem=send[w].at[k], recv_sem=recv[w].at[k], device_id=(px, py, pc), device_id_type=MESH)))
    return copies


def _hbm(a):
    return pltpu.with_memory_space_constraint(a, pltpu.HBM)


def _spread_start(srcs, lands, after, gather, name):
    n = len(srcs)

    def body(*refs):
        src_refs, land_refs = refs[:n], refs[n:2 * n]
        outs = refs[2 * n + 1:]
        send, recv, token = outs[:n], outs[n:2 * n], outs[4 * n]
        for start, _ in _spread_copies(src_refs, land_refs, send, recv, gather):
            start.start()
        token[...] = jnp.zeros_like(token)

    res = pl.pallas_call(
        body, name=name,
        out_shape=tuple([pltpu.SemaphoreType.DMA((N_PEERS,))] * (2 * n)
                        + [pltpu.HBM(a.shape, a.dtype) for a in srcs] + [pltpu.HBM(a.shape, a.dtype) for a in lands]
                        + [jax.ShapeDtypeStruct((SUBLANES, LANES), F32)]),
        in_specs=[HBM] * (2 * n) + [ANY],
        out_specs=tuple([SEM] * (2 * n) + [HBM] * (2 * n) + [pl.BlockSpec(memory_space=pltpu.VMEM)]),
        input_output_aliases={i: 2 * n + i for i in range(2 * n)},
        compiler_params=pltpu.CompilerParams(has_side_effects=EFFECT),
    )(*[_hbm(a) for a in srcs], *[_hbm(a) for a in lands], after)
    return res[:n], res[n:2 * n], res[2 * n:3 * n], res[3 * n:4 * n], res[4 * n]


def _spread_wait(send, recv, srcs, lands, after, gather, name):
    n = len(srcs)

    def body(*refs):
        src_refs, land_refs = refs[:n], refs[n:2 * n]
        send_refs, recv_refs = refs[2 * n:3 * n], refs[3 * n:4 * n]
        for _, arrive in _spread_copies(src_refs, land_refs, send_refs, recv_refs, gather):
            arrive.wait_send()
            arrive.wait_recv()

    res = pl.pallas_call(
        body, name=name,
        out_shape=tuple([pltpu.HBM(a.shape, a.dtype) for a in srcs] + [pltpu.HBM(a.shape, a.dtype) for a in lands]),
        in_specs=[HBM] * (2 * n) + [SEM] * (2 * n) + [ANY],
        out_specs=tuple([HBM] * (2 * n)),
        input_output_aliases={i: i for i in range(2 * n)},
        compiler_params=pltpu.CompilerParams(has_side_effects=EFFECT),
    )(*srcs, *lands, *send, *recv, after)
    return res[n:]


def _landing(shape, dtype, own, me):
    return lax.dynamic_update_index_in_dim(lax.empty((8,) + shape, dtype), own, me, 0)


def _pad_cols(a, to):
    return jnp.pad(a, ((0, 0), (0, to - a.shape[1])))


N_GLR = GLA_W + GLA_RANK
FF_SLAB = D_FF // 4
FF_SLAB_P = FFP // 4


TRANSPOSED = ("w_in", "ffn_w_in")


def _prepare_sub1(gath):
    w_in_t = gath["w_in"].reshape(-1, gath["w_in"].shape[2])
    w2 = jnp.concatenate([gath["gla_gate_w2"][s] for s in range(8)], axis=1)
    return {"w_a_t": jnp.pad(w_in_t[:N_GLR], ((0, HA_W - N_GLR), (0, 0))), "w_b_t": w_in_t[N_GLR:],
            "w2p": jnp.pad(w2, ((0, LANES - GLA_RANK), (0, 0)))}


def _prepare_ffn_in(g):
    f = jnp.pad(g, ((0, 0), (0, FF_SLAB_P - FF_SLAB), (0, 0)))
    return f.reshape(2 * FFP, f.shape[2])


def _prepare_ffn_out(g):
    return jnp.pad(g.reshape(4, FF_SLAB, -1), ((0, 0), (0, FF_SLAB_P - FF_SLAB), (0, 0))).reshape(FFP, -1)


def _prepare_conv(g, conv_b):
    padc = FF_SLAB_P - FF_SLAB
    cw = jnp.pad(g, ((0, 0), (0, 0), (0, padc)))
    cb = jnp.pad(conv_b.reshape(8, 1, FF_SLAB), ((0, 0), (0, 0), (0, padc)))
    rows = jnp.concatenate([cw, cb, jnp.zeros((8, 4, FF_SLAB_P), F32)], axis=1)
    return jnp.concatenate([rows[s] for s in range(8)], axis=1)


def _prepare_ffn(gath, conv_b):
    return {"w_ffn_t": _prepare_ffn_in(gath["ffn_w_in"]), "wo": _prepare_ffn_out(gath["ffn_w_out"]),
            "cw": _prepare_conv(gath["ffn_conv_w"], conv_b)}


def _unpad_ff(a):
    r = a.shape[0]
    return a.reshape(r, 4, FF_SLAB_P)[:, :, :FF_SLAB].reshape(r, D_FF)


def _grad_slabs(g):
    w_in_t = jnp.concatenate([g["w_a_t"][:N_GLR], g["w_b_t"]], axis=0)
    s = {"w_in": w_in_t.reshape(4, 2, w_in_t.shape[0] // 8, w_in_t.shape[1])}
    for n in ("w_out", "ca_wq", "ca_wo"):
        s[n] = _to_slabs(n, g[n])
    for n in ("ca_wkv", "ffn_w_in"):
        s[n] = g[n].reshape((4, 2) + g[n].shape[1:])
    wo = g["wo"].reshape(4, FF_SLAB_P, -1)[:, :FF_SLAB]
    s["ffn_w_out"] = wo.reshape(4, 2, FF_SLAB // 2, wo.shape[-1])
    return s


class _AtHand:
    def __init__(self, p):
        self.p = p
        self.token = None

    def sub2(self, after):
        return self.p

    def ffn_in(self, after):
        return self.p["w_ffn_t"]

    def ffn_out(self, after):
        return self.p["wo"]

    def grads_out(self, group, slabs):
        pass


def _local_step(x, mem, positions, target, p, small, stages=None):
    t, d = x.shape
    stages = _AtHand(p) if stages is None else stages
    w_a_t, w_b_t, w2p, cw = p["w_a_t"], p["w_b_t"], p["w2p"], p["cw"]
    tabs = _rope_tables(positions)
    xb = x.astype(BF16) if stages.token is None else (x + stages.token[0, 0]).astype(BF16)
    memb = mem.astype(BF16)

    h_a = _matmul(xb, w_a_t, "nt", F32, 1024, 640, d, "mm_h_a")
    h_b = _matmul(xb, w_b_t, "nt", F32, 1024, 1024, d, "mm_h_b")
    o_g, o_raw, s_before = _gla_fwd(h_a, w2p, small["gla_gate_b"], small["gla_norm_g"])
    qr, kr = _rope_fwd(h_b, tabs)
    o_d_b, o_d, lse_tot = _dil_fwd_all(qr, kr, h_b)
    mixin = jnp.concatenate([o_g, o_d_b], axis=1)
    wts = stages.sub2(mixin)
    mix = _matmul(mixin, wts["w_out"], "nn", F32, 1024, 1024, d, "mm_mix")
    x1, x1b, x1t = _ln_fwd(x, mix, small["ln1_g"], small["ln1_b"], "ln1_fwd")

    q_ca = _matmul(x1b, wts["ca_wq"], "nn", BF16, 1024, 1024, d, "mm_caq")
    kvw = wts["ca_wkv"].shape[2]
    memkv = _matmul(memb, wts["ca_wkv"], "nn", BF16, mem.shape[0], kvw, d, "mm_memkv", b_slabs=True)
    o_c, o_ct = _ca_fwd(q_ca, memkv)
    ca_out = _matmul(o_c, wts["ca_wo"], "nn", F32, 1024, 1024, d, "mm_cao")
    x2, x2b, x2t = _ln_fwd(x1, ca_out, small["ln2_g"], small["ln2_b"], "ln2_fwd")

    w_ffn_t = stages.ffn_in(x2b)
    u0 = _matmul(x2b, w_ffn_t, "nt", BF16, 1024, 1024, d, "mm_u0")
    act, act_t = _swiglu_fwd(u0, cw)
    wo = stages.ffn_out(act)
    ffn = _matmul(act, wo, "nn", F32, 512, 512, FFP, "mm_ffn")

    dp3, dp3b, dg3, db3, loss_part = _ln_bwd(x2, ffn, small["ln3_g"], small["ln3_b"], target, True, "ln3_bwd")
    g_wo, g_wo16 = _matmul(act_t, dp3b, "nn", F32, 512, 1024, t, "mm_g_wo", also_bf16=True)
    dact = _matmul(dp3b, wo, "nt", BF16, 1024, 512, d, "mm_dact")
    dug, duu, du_t, dcwg, dcwu = _swiglu_bwd(u0, cw, dact)
    g_ffn_in, g_ffn_in16 = _ffn_win_grad(du_t, x2b)

    def wo_slabs(a):
        a = a.reshape(4, FF_SLAB_P, -1)[:, :FF_SLAB]
        return a.reshape(8, FF_SLAB // 2, a.shape[-1])

    sent = stages.grads_out("ffn", {"ffn_w_out": (wo_slabs(g_wo), wo_slabs(g_wo16)), "ffn_w_in": (g_ffn_in, g_ffn_in16)})
    dx2 = _matmul(dug, w_ffn_t, "nn", F32, 512, 1024, FFP, "mm_dx2_g", resid=dp3, resid_scale=ALPHA, dep=sent)
    dx2 = _matmul(duu, w_ffn_t, "nn", F32, 512, 1024, FFP, "mm_dx2_u", resid=dx2, b_k_off=1)

    dp2, dp2b, dg2, db2 = _ln_bwd(x1, ca_out, small["ln2_g"], small["ln2_b"], dx2, False, "ln2_bwd")
    g_cao, g_cao16 = _matmul(o_ct, dp2b, "nn", F32, 512, 1024, t, "mm_g_cao", also_bf16=True)
    do_c = _matmul(dp2b, wts["ca_wo"], "nt", BF16, 1024, 1024, d, "mm_do_c")
    dq_ca, dmemkv = _ca_bwd(q_ca, memkv, do_c)
    g_caq, g_caq16 = _matmul(x1t, dq_ca, "nn", F32, 512, 1024, t, "mm_g_caq", also_bf16=True)
    g_cakv, g_cakv16 = _matmul(memb, dmemkv.astype(BF16), "tn", F32, 512, kvw, mem.shape[0], "mm_g_cakv",
                               out_slabs=True, also_bf16=True)
    dx1 = _matmul(dq_ca, wts["ca_wq"], "nt", F32, 1024, 1024, d, "mm_dx1", resid=dp2, resid_scale=ALPHA)

    dp1, dp1b, dg1, db1 = _ln_bwd(x, mix, small["ln1_g"], small["ln1_b"], dx1, False, "ln1_bwd")
    g_wout, g_wout16 = _matmul(mixin, dp1b, "tn", F32, 512, 1024, t, "mm_g_wout", also_bf16=True)

    def row_slabs(a):
        return a.reshape(8, a.shape[0] // 8, a.shape[1])

    sent = stages.grads_out("attn", {"ca_wo": (row_slabs(g_cao), row_slabs(g_cao16)),
                                     "ca_wq": (row_slabs(g_caq), row_slabs(g_caq16)), "ca_wkv": (g_cakv, g_cakv16),
                                     "w_out": (row_slabs(g_wout), row_slabs(g_wout16))})
    dmix = _matmul(dp1b, wts["w_out"], "nt", F32, 1024, 1024, d, "mm_dmix", dep=sent)
    dh_a, dw2, dgate_b, dnorm_g = _gla_bwd(h_a, w2p, small["gla_gate_b"], small["gla_norm_g"], o_raw, s_before, dmix)
    dq_d, dk_d, dv_d = _dil_bwd_all(qr, kr, h_b, dmix, o_d, lse_tot)
    dh_b = _dil_dh(dq_d, dk_d, dv_d, tabs)
    g_wa_t, g_wa16 = _matmul(dh_a, xb, "tn", F32, 640, 1024, t, "mm_g_wa", also_bf16=True)
    g_wb_t, g_wb16 = _matmul(dh_b, xb, "tn", F32, 512, 1024, t, "mm_g_wb", also_bf16=True)

    def w_in_slabs(a, b):
        full = jnp.concatenate([a[:N_GLR], b], axis=0)
        return full.reshape(8, full.shape[0] // 8, full.shape[1])

    sent = stages.grads_out("w_in", {"w_in": (w_in_slabs(g_wa_t, g_wb_t), w_in_slabs(g_wa16, g_wb16))})
    dx = _matmul(dh_a, w_a_t, "nn", F32, 512, 1024, HA_W, "mm_dx_a", resid=dp1, resid_scale=ALPHA, dep=sent)
    dx = _matmul(dh_b, w_b_t, "nn", F32, 512, 1024, HB_W, "mm_dx_b", resid=dx)

    grads = {"w_a_t": g_wa_t, "w_b_t": g_wb_t, "w_out": g_wout, "ca_wq": g_caq, "ca_wkv": g_cakv, "ca_wo": g_cao,
             "ffn_w_in": g_ffn_in, "wo": g_wo}
    small_parts = {
        "gla_gate_b": dgate_b, "gla_norm_g": dnorm_g, "ln1_g": dg1, "ln1_b": db1, "ln2_g": dg2, "ln2_b": db2,
        "ln3_g": dg3, "ln3_b": db3,
        "conv": jnp.concatenate([_unpad_ff(dcwg), _unpad_ff(dcwu)], axis=1),
        "gla_gate_w2": dw2[:GLA_RANK],
    }
    return loss_part, dx, grads, small_parts


BIG = ("w_in", "w_out", "ca_wq", "ca_wkv", "ca_wo", "ffn_w_in", "ffn_w_out")
COL_SHARDED = ("w_in", "ca_wkv", "ffn_w_in")
SMALL_ORDER = ("gla_gate_b", "gla_norm_g", "ln1_g", "ln1_b", "ln2_g", "ln2_b", "ln3_g", "ln3_b")


def _gathered_full(name, g):
    if name in COL_SHARDED:
        return g.transpose(1, 0, 2).reshape(g.shape[1], 8 * g.shape[2])
    return g.reshape(8 * g.shape[1], g.shape[2])


def _to_slabs(name, full):
    if name in COL_SHARDED:
        r, cc = full.shape
        s = full.reshape(r, 8, cc // 8).transpose(1, 0, 2)
    else:
        rr, c = full.shape
        s = full.reshape(8, rr // 8, c)
    return s.reshape((4, 2) + s.shape[1:])


def kernel(x, mem, positions, w_in, gla_gate_w2, gla_gate_b, gla_norm_g, w_out, ln1_g, ln1_b, ca_wq, ca_wkv, ca_wo, ln2_g, ln2_b, ffn_w_in, ffn_conv_w, ffn_conv_b, ffn_w_out, ln3_g, ln3_b, loss_target, m_w_in, m_gla_gate_w2, m_gla_gate_b, m_gla_norm_g, m_w_out, m_ln1_g, m_ln1_b, m_ca_wq, m_ca_wkv, m_ca_wo, m_ln2_g, m_ln2_b, m_ffn_w_in, m_ffn_conv_w, m_ffn_conv_b, m_ffn_w_out, m_ln3_g, m_ln3_b, v_w_in, v_gla_gate_w2, v_gla_gate_b, v_gla_norm_g, v_w_out, v_ln1_g, v_ln1_b, v_ca_wq, v_ca_wkv, v_ca_wo, v_ln2_g, v_ln2_b, v_ffn_w_in, v_ffn_conv_w, v_ffn_conv_b, v_ffn_w_out, v_ln3_g, v_ln3_b):
    weights = dict(w_in=w_in, gla_gate_w2=gla_gate_w2, gla_gate_b=gla_gate_b, gla_norm_g=gla_norm_g, w_out=w_out,
                   ln1_g=ln1_g, ln1_b=ln1_b, ca_wq=ca_wq, ca_wkv=ca_wkv, ca_wo=ca_wo, ln2_g=ln2_g, ln2_b=ln2_b,
                   ffn_w_in=ffn_w_in, ffn_conv_w=ffn_conv_w, ffn_conv_b=ffn_conv_b, ffn_w_out=ffn_w_out,
                   ln3_g=ln3_g, ln3_b=ln3_b)
    moms = dict(w_in=(m_w_in, v_w_in), gla_gate_w2=(m_gla_gate_w2, v_gla_gate_w2), gla_gate_b=(m_gla_gate_b, v_gla_gate_b),
                gla_norm_g=(m_gla_norm_g, v_gla_norm_g), w_out=(m_w_out, v_w_out), ln1_g=(m_ln1_g, v_ln1_g),
                ln1_b=(m_ln1_b, v_ln1_b), ca_wq=(m_ca_wq, v_ca_wq), ca_wkv=(m_ca_wkv, v_ca_wkv), ca_wo=(m_ca_wo, v_ca_wo),
                ln2_g=(m_ln2_g, v_ln2_g), ln2_b=(m_ln2_b, v_ln2_b), ffn_w_in=(m_ffn_w_in, v_ffn_w_in),
                ffn_conv_w=(m_ffn_conv_w, v_ffn_conv_w), ffn_conv_b=(m_ffn_conv_b, v_ffn_conv_b),
                ffn_w_out=(m_ffn_w_out, v_ffn_w_out), ln3_g=(m_ln3_g, v_ln3_g), ln3_b=(m_ln3_b, v_ln3_b))
    order = list(weights)
    xi, yi, ci = lax.axis_index("x"), lax.axis_index("y"), lax.axis_index("c")
    me = 4 * xi + 2 * yi + ci

    def travel(n, a):
        return jnp.swapaxes(a, 1, 2) if n in TRANSPOSED else a

    shard = {n: travel(n, weights[n]).astype(BF16)[0] for n in BIG}
    first = _all_gather([shard["w_in"], gla_gate_w2.astype(BF16)[0], ffn_conv_w[0]], "ag_first")
    p = _prepare_sub1({"w_in": first[0], "gla_gate_w2": first[1]})
    p["cw"] = _prepare_conv(first[2], ffn_conv_b)
    later = ("w_out", "ca_wq", "ca_wkv", "ca_wo", "ffn_w_in", "ffn_w_out")
    srcs = [shard[n] for n in later]
    lands = [_landing(shard[n].shape, BF16, shard[n], me) for n in later]
    send, recv, srcs, lands, token = _spread_start(srcs, lands, first[0], True, "ag_rest_start")

    class stages:
        pass

    stages.token = token

    def arrived(lo, hi, after, name):
        return _spread_wait(send[lo:hi], recv[lo:hi], srcs[lo:hi], lands[lo:hi], after, True, name)

    def sub2(after):
        g = dict(zip(later[:4], arrived(0, 4, after, "ag_wait_attn")))
        w = {n: _gathered_full(n, g[n]) for n in ("w_out", "ca_wq", "ca_wo")}
        w["ca_wkv"] = g["ca_wkv"]
        return w

    stages.sub2 = sub2
    stages.ffn_in = lambda after: _prepare_ffn_in(arrived(4, 5, after, "ag_wait_ffn_in")[0])
    stages.ffn_out = lambda after: _prepare_ffn_out(arrived(5, 6, after, "ag_wait_ffn_out")[0])
    sent = {}

    def grads_out(group, slabs):
        names = list(slabs)
        srcs16 = [slabs[n][1] for n in names]
        zones = [_landing(s.shape[1:], BF16, jnp.zeros(s.shape[1:], BF16), me) for s in srcs16]
        snd, rcv, s_thru, l_thru, tok = _spread_start(srcs16, zones, slabs[names[0]][0], False, f"rs_{group}_start")
        sent[group] = (names, [slabs[n][0] for n in names], (snd, rcv, s_thru, l_thru))
        return tok

    stages.grads_out = grads_out
    small = dict(gla_gate_b=gla_gate_b, gla_norm_g=gla_norm_g, ln1_g=ln1_g, ln1_b=ln1_b, ln2_g=ln2_g, ln2_b=ln2_b,
                 ln3_g=ln3_g, ln3_b=ln3_b)

    loss_part, dx, grads, small_parts = _local_step(x[0], mem[0], positions[0], loss_target[0], p, small, stages)
    loss = lax.psum(jnp.sum(loss_part), ("x", "y", "c"))

    out = {}
    me1 = me.reshape(1).astype(jnp.int32)
    for group, (names, own32, handles) in sent.items():
        landed = _spread_wait(*handles, dx, False, f"rs_{group}_wait")
        for n, own, land in zip(names, own32, landed):
            m_, v_ = moms[n]
            res4 = _adamw_direct(travel(n, weights[n]), travel(n, m_), travel(n, v_), own, land, me1, f"adamw_{n}")
            out[n] = [travel(n, a) for a in res4]


    packed = jnp.concatenate([small_parts[n] for n in SMALL_ORDER] + [small_parts["conv"],
                             small_parts["gla_gate_w2"].reshape(SUBLANES, -1)], axis=1)
    pad = (-packed.shape[1]) % 2048
    packed = jnp.pad(packed, ((0, 0), (0, pad)))
    (allp,) = _all_gather([packed], "ag_small")
    dev_sum, row_sum = _small_reduce(allp)
    off = 0
    for n in SMALL_ORDER:
        width = weights[n].shape[1]
        g = row_sum[0:1, off:off + width]
        off += width
        m_, v_ = moms[n]
        out[n] = _adamw(weights[n], m_, v_, g, f"adamw_{n}")
    conv_g = dev_sum[:, off:off + 2 * D_FF]
    off += 2 * D_FF
    g_cb = conv_g[3:4]
    out["ffn_conv_b"] = _adamw(ffn_conv_b, m_ffn_conv_b, v_ffn_conv_b, g_cb, "adamw_ffn_conv_b")
    wsh = ffn_conv_w.shape[2]
    g_cw = lax.dynamic_slice_in_dim(conv_g[0:3], me * wsh, wsh, axis=1)
    out["ffn_conv_w"] = _adamw(ffn_conv_w[0], m_ffn_conv_w[0], v_ffn_conv_w[0], g_cw, "adamw_ffn_conv_w")
    w2_g = dev_sum[:, off:off + GLA_RANK * GLA_HEADS * GLA_DK // SUBLANES].reshape(GLA_RANK, GLA_HEADS * GLA_DK)
    wsh2 = gla_gate_w2.shape[2]
    g_w2 = lax.dynamic_slice_in_dim(w2_g, me * wsh2, wsh2, axis=1)
    out["gla_gate_w2"] = _adamw(gla_gate_w2[0], m_gla_gate_w2[0], v_gla_gate_w2[0], g_w2, "adamw_gla_gate_w2")

    def shaped(n, a):
        return a.reshape(weights[n].shape)

    res = [loss, dx[None]]
    for k in range(4):
        res += [shaped(n, out[n][k]) for n in order]
    return tuple(res)


def _adamw_direct(w, m, v, own, land, me, name):
    _, r, c = w.shape
    tr, tc = _tile2d(r, c)
    blk = pl.BlockSpec((None, tr, tc), lambda i, j, s: (0, i, j))
    mine = pl.BlockSpec((None, tr, tc), lambda i, j, s: (s[0], i, j))
    slots = [pl.BlockSpec((None, tr, tc), lambda i, j, s, k=k: (k, i, j)) for k in range(8)]

    def body(s_ref, w_ref, m_ref, v_ref, p_ref, *rest):
        slot_refs, (g_ref, d_ref, nm_ref, nv_ref) = rest[:8], rest[8:]
        g = p_ref[...]
        for sr in slot_refs:
            g = g + sr[...].astype(F32)
        d_ref[...], nm_ref[...], nv_ref[...] = _adamw_math(w_ref[...], m_ref[...], v_ref[...], g)
        g_ref[...] = g

    gs = pltpu.PrefetchScalarGridSpec(num_scalar_prefetch=1, grid=(r // tr, c // tc),
                                      in_specs=[blk, blk, blk, mine] + slots, out_specs=[blk] * 4)
    return pl.pallas_call(body, name=name, grid_spec=gs, out_shape=[jax.ShapeDtypeStruct((1, r, c), F32)] * 4,
                          compiler_params=_params(("parallel", "parallel")))(me, w, m, v, own, *([land] * 8))


def _adamw_big(w, m, v, p32, rc, chip, name):
    _, r, c = w.shape
    tr, tc = _tile2d(r, c)
    blk = pl.BlockSpec((None, tr, tc), lambda i, j, s: (0, i, j))
    own = pl.BlockSpec((None, tr, tc), lambda i, j, s: (s[0], i, j))
    others = [pl.BlockSpec((None, tr, tc), lambda i, j, s, k=k: (k, i, j)) for k in range(3)]

    def body(s_ref, w_ref, m_ref, v_ref, p_ref, r0_ref, r1_ref, r2_ref, g_ref, d_ref, nm_ref, nv_ref):
        g = ((p_ref[...] + r0_ref[...].astype(F32)) + r1_ref[...].astype(F32)) + r2_ref[...].astype(F32)
        d_ref[...], nm_ref[...], nv_ref[...] = _adamw_math(w_ref[...], m_ref[...], v_ref[...], g)
        g_ref[...] = g

    gs = pltpu.PrefetchScalarGridSpec(num_scalar_prefetch=1, grid=(r // tr, c // tc),
                                      in_specs=[blk, blk, blk, own] + others, out_specs=[blk] * 4)
    return pl.pallas_call(body, name=name, grid_spec=gs, out_shape=[jax.ShapeDtypeStruct((1, r, c), F32)] * 4,
                          compiler_params=_params(("parallel", "parallel")))(chip, w, m, v, p32, rc, rc, rc)
```

```python
import functools
import math

import jax
import jax.numpy as jnp
from jax import lax
from jax.experimental import pallas as pl
from jax.experimental.pallas import tpu as pltpu

F32 = jnp.float32
BF16 = jnp.bfloat16
MESH = pl.DeviceIdType.MESH

D_MODEL = 2048
LN_EPS = 1e-5
GLA_HEADS = 4
GLA_DV = 256
GLA_DK = 128
GLA_RANK = 16
GLA_TAU = 16.0
GLA_CHUNK = 64
DIL_HD = 128
DIL_HEADS = 8
DIL_BAND = 128
DIL_DILATIONS = (1, 4, 16)
ROPE_THETA = 500000.0
ROPE_DIMS = 32
CA_HEADS = 4
CA_HD = 512
D_FF = 5504
ALPHA = 2.0 ** 0.25
ADAM_LR = 0.001
ADAM_B1 = 0.9
ADAM_B2 = 0.999
ADAM_EPS = 1e-08
ADAM_WD = 0.01
ADAM_STEP = 10

LANES = 128
SUBLANES = 8
VMEM_LIMIT = 56 * 1024 * 1024

GLA_W = 2 * GLA_HEADS * GLA_DK + 2 * GLA_HEADS * GLA_DV
HA_W = GLA_W + LANES
HB_W = 3 * DIL_HEADS * DIL_HD
FFP = 5632
NEG = -1e30


def _params(sem):
    return pltpu.CompilerParams(dimension_semantics=sem, vmem_limit_bytes=VMEM_LIMIT)


def _sigmoid(x):
    return 1.0 / (1.0 + jnp.exp(-x))


def _dot(a, b, dn, precision=None):
    return lax.dot_general(a, b, (dn, ((), ())), preferred_element_type=F32, precision=precision)


NN = ((1,), (0,))
NT = ((1,), (1,))
TN = ((0,), (0,))


def _bf(v):
    return v if v.dtype == BF16 else v.astype(BF16)


def _matmul(a, b, kind, out_dtype, tm, tn, tk, name, resid=None, resid_scale=1.0, b_k_off=0, b_slabs=False,
            out_slabs=False, also_bf16=False, dep=None):
    if b_slabs:
        assert kind != "nt" and b.shape[2] == tn
        k2, n = b.shape[1], b.shape[0] * tn
    elif kind == "nt":
        n, k2 = b.shape
    else:
        k2, n = b.shape
    (k, m) = a.shape if kind == "tn" else a.shape[::-1]
    assert k2 >= k and (k2 == k or not b_slabs) and m % tm == 0 and n % tn == 0 and k % tk == 0, \
        (name, a.shape, b.shape, tm, tn, tk)
    nk = k // tk
    dn = {"nn": NN, "nt": NT, "tn": TN}[kind]
    a_spec = pl.BlockSpec((tk, tm), lambda i, j, kk: (kk, i)) if kind == "tn" else pl.BlockSpec((tm, tk), lambda i, j, kk: (i, kk))
    if b_slabs:
        b_spec = pl.BlockSpec((None, tk, tn), lambda i, j, kk: (j, kk, 0))
    elif kind == "nt":
        b_spec = pl.BlockSpec((tn, tk), lambda i, j, kk: (j, kk + b_k_off))
    else:
        b_spec = pl.BlockSpec((tk, tn), lambda i, j, kk: (kk + b_k_off, j))
    if out_slabs:
        o_spec = pl.BlockSpec((None, tm, tn), lambda i, j, kk: (j, i, 0))
        o_shape = (n // tn, m, tn)
    else:
        o_spec = pl.BlockSpec((tm, tn), lambda i, j, kk: (i, j))
        o_shape = (m, n)
    has_resid = resid is not None

    n_in = 2 + int(has_resid) + int(dep is not None)

    def body(*refs):
        a_ref, b_ref = refs[:2]
        r_ref = refs[2] if has_resid else None
        o_ref = refs[n_in]
        ob_ref = refs[n_in + 1] if also_bf16 else None
        part = _dot(_bf(a_ref[...]), _bf(b_ref[...]), dn)

        def finish(acc):
            if has_resid:
                acc = acc + resid_scale * r_ref[...].astype(F32)
            o_ref[...] = acc.astype(out_dtype)
            if also_bf16:
                ob_ref[...] = acc.astype(BF16)

        if nk == 1:
            finish(part)
        else:
            acc_ref = refs[-1]
            kk = pl.program_id(2)

            @pl.when(kk == 0)
            def _():
                acc_ref[...] = part

            @pl.when(kk > 0)
            def _():
                acc_ref[...] += part

            @pl.when(kk == nk - 1)
            def _():
                finish(acc_ref[...])

    in_specs = [a_spec, b_spec] + ([o_spec] if has_resid else [])
    args = (a, b) + ((resid,) if has_resid else ())
    if dep is not None:
        in_specs.append(pl.BlockSpec((SUBLANES, LANES), lambda i, j, kk: (0, 0)))
        args += (dep,)
    o_struct = jax.ShapeDtypeStruct(o_shape, out_dtype)
    return pl.pallas_call(
        body, name=name, out_shape=[o_struct, jax.ShapeDtypeStruct(o_shape, BF16)] if also_bf16 else o_struct,
        grid=(m // tm, n // tn, nk), in_specs=in_specs, out_specs=[o_spec, o_spec] if also_bf16 else o_spec,
        scratch_shapes=[pltpu.VMEM((tm, tn), F32)] if nk > 1 else [],
        compiler_params=_params(("parallel", "parallel", "arbitrary")),
    )(*args)


def _ln_core(xres, f):
    p = ALPHA * xres + f
    mu = jnp.mean(p, axis=-1, keepdims=True)
    xc = p - mu
    var = jnp.mean(xc * xc, axis=-1, keepdims=True)
    rstd = lax.rsqrt(var + LN_EPS)
    return xc * rstd, rstd


def _rows8(v):
    r, c = v.shape
    return jnp.sum(v.reshape(r // SUBLANES, SUBLANES, c), axis=0)


def _ln_fwd(xres, f, g, b, name, tr=256):
    t, d = xres.shape
    row = pl.BlockSpec((tr, d), lambda i: (i, 0))
    vec = pl.BlockSpec((1, d), lambda i: (0, 0))

    def body(x_ref, f_ref, g_ref, b_ref, y_ref, yb_ref, yt_ref):
        xhat, _ = _ln_core(x_ref[...], f_ref[...])
        y = xhat * g_ref[...] + b_ref[...]
        y_ref[...] = y
        yb = y.astype(BF16)
        yb_ref[...] = yb
        yt_ref[...] = yb.T

    return pl.pallas_call(
        body, name=name, grid=(t // tr,), in_specs=[row, row, vec, vec],
        out_specs=[row, row, pl.BlockSpec((d, tr), lambda i: (0, i))],
        out_shape=[jax.ShapeDtypeStruct((t, d), F32), jax.ShapeDtypeStruct((t, d), BF16),
                   jax.ShapeDtypeStruct((d, t), BF16)],
        compiler_params=_params(("parallel",)),
    )(xres, f, g, b)


def _ln_bwd(xres, f, g, b, dy_or_target, loss_head, name, tr=256):
    t, d = xres.shape
    row = pl.BlockSpec((tr, d), lambda i: (i, 0))
    vec = pl.BlockSpec((1, d), lambda i: (0, 0))
    acc = pl.BlockSpec((SUBLANES, d), lambda i: (0, 0))
    lacc = pl.BlockSpec((SUBLANES, LANES), lambda i: (0, 0))

    def body(x_ref, f_ref, g_ref, b_ref, t_ref, dp_ref, dpb_ref, dg_ref, db_ref, *rest):
        i = pl.program_id(0)
        xhat, rstd = _ln_core(x_ref[...], f_ref[...])
        if loss_head:
            err = xhat * g_ref[...] + b_ref[...] - t_ref[...]
            dy = err * (1.0 / d)
            sq = err * err
            lanes = sq[:, :LANES]
            for kk in range(1, d // LANES):
                lanes = lanes + sq[:, kk * LANES:(kk + 1) * LANES]
            lpart = _rows8(lanes) * (0.5 / d)
        else:
            dy = t_ref[...]
        dxh = dy * g_ref[...]
        m1 = jnp.mean(dxh, axis=-1, keepdims=True)
        m2 = jnp.mean(dxh * xhat, axis=-1, keepdims=True)
        dp = rstd * (dxh - m1 - xhat * m2)
        dp_ref[...] = dp
        dpb_ref[...] = dp.astype(BF16)
        dgp = _rows8(dy * xhat)
        dbp = _rows8(dy)

        @pl.when(i == 0)
        def _():
            dg_ref[...] = dgp
            db_ref[...] = dbp
            if loss_head:
                rest[0][...] = lpart

        @pl.when(i > 0)
        def _():
            dg_ref[...] += dgp
            db_ref[...] += dbp
            if loss_head:
                rest[0][...] += lpart

    out_shape = [jax.ShapeDtypeStruct((t, d), F32), jax.ShapeDtypeStruct((t, d), BF16),
                 jax.ShapeDtypeStruct((SUBLANES, d), F32), jax.ShapeDtypeStruct((SUBLANES, d), F32)]
    out_specs = [row, row, acc, acc]
    if loss_head:
        out_shape.append(jax.ShapeDtypeStruct((SUBLANES, LANES), F32))
        out_specs.append(lacc)
    return pl.pallas_call(
        body, name=name, grid=(t // tr,), in_specs=[row, row, vec, vec, row], out_specs=out_specs,
        out_shape=out_shape, compiler_params=_params(("arbitrary",)),
    )(xres, f, g, b, dy_or_target)


def _gla_gates(glr, w2, gb):
    z = _dot(_bf(glr), w2, NN) + gb
    lg = (jnp.minimum(z, 0.0) - jnp.log(1.0 + jnp.exp(-jnp.abs(z)))) * (1.0 / GLA_TAU)
    c = z.shape[0]
    ri = lax.broadcasted_iota(jnp.int32, (c, c), 0)
    ci = lax.broadcasted_iota(jnp.int32, (c, c), 1)
    tri = (ci <= ri).astype(F32)
    bcum = _dot(tri, lg, NN, precision=lax.Precision.HIGHEST)
    blast = jnp.sum(lg, axis=0, keepdims=True)
    return z, bcum, blast, tri


def _gla_specs(t):
    c = GLA_CHUNK
    return c, t // c


def _gla_fwd(h_a, w2p, gate_b, norm_g):
    t = h_a.shape[0]
    c, n = _gla_specs(t)
    hk, hv = GLA_HEADS * GLA_DK, GLA_HEADS * GLA_DV
    scale = GLA_DK ** -0.5

    def body(q_ref, k_ref, v_ref, r_ref, glr_ref, w2_ref, gb_ref, ng_ref, og_ref, oraw_ref, sb_ref, st_ref):
        i = pl.program_id(0)

        @pl.when(i == 0)
        def _():
            st_ref[...] = jnp.zeros_like(st_ref)

        _, bcum, blast, _ = _gla_gates(glr_ref[...], w2_ref[...], gb_ref[...])
        ri = lax.broadcasted_iota(jnp.int32, (c, c), 0)
        ci = lax.broadcasted_iota(jnp.int32, (c, c), 1)
        causal = ci <= ri
        for h in range(GLA_HEADS):
            ks = slice(h * GLA_DK, (h + 1) * GLA_DK)
            vs = slice(h * GLA_DV, (h + 1) * GLA_DV)
            b_h, bl_h = bcum[:, ks], blast[:, ks]
            q_h, k_h = q_ref[:, ks], k_ref[:, ks]
            v_h = _bf(v_ref[:, vs])
            qi = _bf(q_h * scale * jnp.exp(b_h))
            ki = _bf(k_h * jnp.exp(-b_h))
            ke = _bf(k_h * jnp.exp(bl_h - b_h))
            st = st_ref[h]
            sb_ref[0, h] = st
            a = jnp.where(causal, _dot(qi, ki, NT), 0.0)
            o = _dot(_bf(a), v_h, NN) + _dot(qi, _bf(st), NT)
            st_ref[h] = st * jnp.exp(bl_h) + _dot(v_h, ke, TN)
            oraw_ref[:, vs] = o
            mu = jnp.mean(o, axis=-1, keepdims=True)
            oc = o - mu
            var = jnp.mean(oc * oc, axis=-1, keepdims=True)
            xh = oc * lax.rsqrt(var + LN_EPS)
            r_h = r_ref[:, vs]
            og_ref[:, vs] = (xh * ng_ref[:, vs] * (r_h * _sigmoid(r_h))).astype(BF16)

    return pl.pallas_call(
        body, name="gla_fwd", grid=(n,),
        in_specs=[pl.BlockSpec((c, hk), lambda i: (i, 0)), pl.BlockSpec((c, hk), lambda i: (i, 1)),
                  pl.BlockSpec((c, hv), lambda i: (i, 1)), pl.BlockSpec((c, hv), lambda i: (i, 2)),
                  pl.BlockSpec((c, LANES), lambda i: (i, GLA_W // LANES)),
                  pl.BlockSpec((LANES, hk), lambda i: (0, 0)), pl.BlockSpec((1, hk), lambda i: (0, 0)),
                  pl.BlockSpec((1, hv), lambda i: (0, 0))],
        out_specs=[pl.BlockSpec((c, hv), lambda i: (i, 0)), pl.BlockSpec((c, hv), lambda i: (i, 0)),
                   pl.BlockSpec((1, GLA_HEADS, GLA_DV, GLA_DK), lambda i: (i, 0, 0, 0))],
        out_shape=[jax.ShapeDtypeStruct((t, hv), BF16), jax.ShapeDtypeStruct((t, hv), F32),
                   jax.ShapeDtypeStruct((n, GLA_HEADS, GLA_DV, GLA_DK), F32)],
        scratch_shapes=[pltpu.VMEM((GLA_HEADS, GLA_DV, GLA_DK), F32)],
        compiler_params=_params(("arbitrary",)),
    )(h_a, h_a, h_a, h_a, h_a, w2p, gate_b, norm_g)


def _gla_bwd(h_a, w2p, gate_b, norm_g, o_raw, s_before, dmix):
    t = h_a.shape[0]
    c, n = _gla_specs(t)
    hk, hv = GLA_HEADS * GLA_DK, GLA_HEADS * GLA_DV
    scale = GLA_DK ** -0.5
    rev = lambda i: n - 1 - i

    def body(q_ref, k_ref, v_ref, r_ref, glr_ref, w2_ref, gb_ref, ng_ref, oraw_ref, sb_ref, do_ref,
             dh_ref, dw2_ref, dgb_ref, dng_ref, dst_ref):
        i = pl.program_id(0)

        @pl.when(i == 0)
        def _():
            dst_ref[...] = jnp.zeros_like(dst_ref)

        glr = glr_ref[...]
        z, bcum, blast, tri = _gla_gates(glr, w2_ref[...], gb_ref[...])
        ri = lax.broadcasted_iota(jnp.int32, (c, c), 0)
        ci = lax.broadcasted_iota(jnp.int32, (c, c), 1)
        causal = ci <= ri
        dlg_parts = []
        dng_parts = []
        for h in range(GLA_HEADS):
            ks = slice(h * GLA_DK, (h + 1) * GLA_DK)
            vs = slice(h * GLA_DV, (h + 1) * GLA_DV)
            o = oraw_ref[:, vs]
            mu = jnp.mean(o, axis=-1, keepdims=True)
            oc = o - mu
            var = jnp.mean(oc * oc, axis=-1, keepdims=True)
            rstd = lax.rsqrt(var + LN_EPS)
            xh = oc * rstd
            r_h = r_ref[:, vs]
            sg = _sigmoid(r_h)
            silu = r_h * sg
            dout = do_ref[:, vs]
            ng = ng_ref[:, vs]
            dng_parts.append(_rows8(dout * xh * silu))
            dr = dout * xh * ng * (sg * (1.0 + r_h * (1.0 - sg)))
            dxh = dout * ng * silu
            m1 = jnp.mean(dxh, axis=-1, keepdims=True)
            m2 = jnp.mean(dxh * xh, axis=-1, keepdims=True)
            do_raw = _bf(rstd * (dxh - m1 - xh * m2))
            b_h, bl_h = bcum[:, ks], blast[:, ks]
            q_h, k_h = q_ref[:, ks], k_ref[:, ks]
            v_h = _bf(v_ref[:, vs])
            eb, enb, eend = jnp.exp(b_h), jnp.exp(-b_h), jnp.exp(bl_h - b_h)
            decay = jnp.exp(bl_h)
            qi_f, ki_f, ke_f = q_h * scale * eb, k_h * enb, k_h * eend
            qi, ki, ke = _bf(qi_f), _bf(ki_f), _bf(ke_f)
            st = sb_ref[0, h]
            dst = dst_ref[h]
            dst_b = _bf(dst)
            a = _bf(jnp.where(causal, _dot(qi, ki, NT), 0.0))
            da = _bf(jnp.where(causal, _dot(do_raw, v_h, NT), 0.0))
            dv = _dot(a, do_raw, TN) + _dot(ke, dst_b, NT)
            dqi = _dot(da, ki, NN) + _dot(do_raw, _bf(st), NN)
            dki = _dot(da, qi, TN)
            dke = _dot(v_h, dst_b, NN)
            dst_ref[h] = _dot(do_raw, qi, TN) + dst * decay
            dbl = decay * jnp.sum(st * dst, axis=0, keepdims=True) + jnp.sum(dke * ke_f, axis=0, keepdims=True)
            dbc = dqi * qi_f - dki * ki_f - dke * ke_f
            dlg_parts.append(_dot(tri, dbc, TN, precision=lax.Precision.HIGHEST) + dbl)
            dh_ref[:, ks] = (dqi * eb * scale).astype(BF16)
            dh_ref[:, hk + h * GLA_DK: hk + (h + 1) * GLA_DK] = (dki * enb + dke * eend).astype(BF16)
            dh_ref[:, 2 * hk + h * GLA_DV: 2 * hk + (h + 1) * GLA_DV] = dv.astype(BF16)
            dh_ref[:, 2 * hk + hv + h * GLA_DV: 2 * hk + hv + (h + 1) * GLA_DV] = dr.astype(BF16)
        dlg = jnp.concatenate(dlg_parts, axis=1)
        dz = dlg * (1.0 / GLA_TAU) * _sigmoid(-z)
        dz_b = _bf(dz)
        dh_ref[:, GLA_W:] = _dot(dz_b, w2_ref[...], NT).astype(BF16)
        dw2p = _dot(_bf(glr), dz_b, TN)
        dgbp = _rows8(dz)
        dngp = jnp.concatenate(dng_parts, axis=1)

        @pl.when(i == 0)
        def _():
            dw2_ref[...] = dw2p
            dgb_ref[...] = dgbp
            dng_ref[...] = dngp

        @pl.when(i > 0)
        def _():
            dw2_ref[...] += dw2p
            dgb_ref[...] += dgbp
            dng_ref[...] += dngp

    return pl.pallas_call(
        body, name="gla_bwd", grid=(n,),
        in_specs=[pl.BlockSpec((c, hk), lambda i: (rev(i), 0)), pl.BlockSpec((c, hk), lambda i: (rev(i), 1)),
                  pl.BlockSpec((c, hv), lambda i: (rev(i), 1)), pl.BlockSpec((c, hv), lambda i: (rev(i), 2)),
                  pl.BlockSpec((c, LANES), lambda i: (rev(i), GLA_W // LANES)),
                  pl.BlockSpec((LANES, hk), lambda i: (0, 0)), pl.BlockSpec((1, hk), lambda i: (0, 0)),
                  pl.BlockSpec((1, hv), lambda i: (0, 0)),
                  pl.BlockSpec((c, hv), lambda i: (rev(i), 0)),
                  pl.BlockSpec((1, GLA_HEADS, GLA_DV, GLA_DK), lambda i: (rev(i), 0, 0, 0)),
                  pl.BlockSpec((c, hv), lambda i: (rev(i), 0))],
        out_specs=[pl.BlockSpec((c, HA_W), lambda i: (rev(i), 0)),
                   pl.BlockSpec((LANES, hk), lambda i: (0, 0)),
                   pl.BlockSpec((SUBLANES, hk), lambda i: (0, 0)),
                   pl.BlockSpec((SUBLANES, hv), lambda i: (0, 0))],
        out_shape=[jax.ShapeDtypeStruct((t, HA_W), BF16), jax.ShapeDtypeStruct((LANES, hk), F32),
                   jax.ShapeDtypeStruct((SUBLANES, hk), F32), jax.ShapeDtypeStruct((SUBLANES, hv), F32)],
        scratch_shapes=[pltpu.VMEM((GLA_HEADS, GLA_DV, GLA_DK), F32)],
        compiler_params=_params(("arbitrary",)),
    )(h_a, h_a, h_a, h_a, h_a, w2p, gate_b, norm_g, o_raw, s_before, dmix)


def _rope_tables(positions):
    half = ROPE_DIMS // 2
    inv_freq = ROPE_THETA ** (-jnp.arange(0, ROPE_DIMS, 2, dtype=F32) / ROPE_DIMS)
    ang = positions.astype(F32).reshape(-1, 1) * inv_freq
    cos, sin = jnp.cos(ang), jnp.sin(ang)
    t = cos.shape[0]
    one = jnp.ones((t, DIL_HD - ROPE_DIMS), F32)
    zero = jnp.zeros((t, DIL_HD - ROPE_DIMS), F32)
    zh = jnp.zeros((t, half), F32)
    return (jnp.concatenate([cos, cos, one], axis=1), jnp.concatenate([-sin, zh, zero], axis=1),
            jnp.concatenate([zh, sin, zero], axis=1))


def _rope_apply(x, c, s1, s2):
    half = ROPE_DIMS // 2
    return x * c + pltpu.roll(x, DIL_HD - half, 1) * s1 + pltpu.roll(x, half, 1) * s2


def _rope_apply_t(dy, c, s1, s2):
    half = ROPE_DIMS // 2
    return dy * c + pltpu.roll(dy * s1, half, 1) + pltpu.roll(dy * s2, DIL_HD - half, 1)


def _rope_fwd(h_b, tabs, tr=256):
    t = h_b.shape[0]
    w = DIL_HEADS * DIL_HD
    scale = DIL_HD ** -0.5
    tab = pl.BlockSpec((tr, DIL_HD), lambda i: (i, 0))
    outb = pl.BlockSpec((tr, w), lambda i: (i, 0))

    def body(q_ref, k_ref, c_ref, s1_ref, s2_ref, qo_ref, ko_ref):
        c, s1, s2 = c_ref[...], s1_ref[...], s2_ref[...]
        for h in range(DIL_HEADS):
            hs = slice(h * DIL_HD, (h + 1) * DIL_HD)
            qo_ref[:, hs] = _rope_apply(q_ref[:, hs] * scale, c, s1, s2)
            ko_ref[:, hs] = _rope_apply(k_ref[:, hs], c, s1, s2)

    return pl.pallas_call(
        body, name="rope_fwd", grid=(t // tr,),
        in_specs=[pl.BlockSpec((tr, w), lambda i: (i, 0)), pl.BlockSpec((tr, w), lambda i: (i, 1)), tab, tab, tab],
        out_specs=[outb, outb],
        out_shape=[jax.ShapeDtypeStruct((t, w), F32)] * 2,
        compiler_params=_params(("parallel",)),
    )(h_b, h_b, *tabs)


def _dil_dh(dq, dk, dv, tabs, tr=256):
    t, w = dq.shape
    scale = DIL_HD ** -0.5
    tab = pl.BlockSpec((tr, DIL_HD), lambda i: (i, 0))
    inb = pl.BlockSpec((tr, w), lambda i: (i, 0))

    def body(dq_ref, dk_ref, dv_ref, c_ref, s1_ref, s2_ref, o_ref):
        c, s1, s2 = c_ref[...], s1_ref[...], s2_ref[...]
        for h in range(DIL_HEADS):
            hs = slice(h * DIL_HD, (h + 1) * DIL_HD)
            o_ref[:, h * DIL_HD:(h + 1) * DIL_HD] = (_rope_apply_t(dq_ref[:, hs], c, s1, s2) * scale).astype(BF16)
            o_ref[:, w + h * DIL_HD: w + (h + 1) * DIL_HD] = _rope_apply_t(dk_ref[:, hs], c, s1, s2).astype(BF16)
        o_ref[:, 2 * w:] = dv_ref[...].astype(BF16)

    return pl.pallas_call(
        body, name="dil_dh", grid=(t // tr,), in_specs=[inb] * 3 + [tab] * 3,
        out_specs=pl.BlockSpec((tr, 3 * w), lambda i: (i, 0)),
        out_shape=jax.ShapeDtypeStruct((t, 3 * w), BF16), compiler_params=_params(("parallel",)),
    )(dq, dk, dv, *tabs)


BANDS = 8


def _to_branch(a, d):
    t, w = a.shape
    return a.reshape(t // d, d, w // DIL_HD, DIL_HD).transpose(1, 2, 0, 3).reshape(-1, DIL_HD)


def _from_branch(a, d, t):
    hds = a.shape[0] // t
    return a.reshape(d, hds, t // d, DIL_HD).transpose(2, 0, 1, 3).reshape(t, hds * DIL_HD)


def _band_masks(not_first):
    r = lax.broadcasted_iota(jnp.int32, (DIL_BAND, 2 * DIL_BAND), 0)
    c = lax.broadcasted_iota(jnp.int32, (DIL_BAND, 2 * DIL_BAND), 1)
    nf = jnp.full((DIL_BAND, 2 * DIL_BAND), not_first, jnp.int32)
    look_back = jnp.logical_and(jnp.logical_and(c < DIL_BAND, c >= r), nf > 0)
    own_band = jnp.logical_and(c >= DIL_BAND, (c - DIL_BAND) <= r)
    return jnp.logical_or(look_back, own_band)


def _dil_fwd(q, k, v, nb, name):
    rows = q.shape[0]
    blk = BANDS * DIL_BAND
    steps = rows // blk
    main = pl.BlockSpec((blk, DIL_HD), lambda i: (i, 0))
    prev = pl.BlockSpec((DIL_BAND, DIL_HD), lambda i: (jnp.maximum(i * BANDS - 1, 0), 0))

    def body(q_ref, k_ref, v_ref, kp_ref, vp_ref, o_ref, l_ref):
        i = pl.program_id(0)
        for j in range(BANDS):
            lo, hi = j * DIL_BAND, (j + 1) * DIL_BAND
            if j == 0:
                kcat = jnp.concatenate([kp_ref[...], k_ref[lo:hi, :]], axis=0)
                vcat = jnp.concatenate([vp_ref[...], v_ref[lo:hi, :]], axis=0)
            else:
                kcat = k_ref[lo - DIL_BAND:hi, :]
                vcat = v_ref[lo - DIL_BAND:hi, :]
            not_first = (((i * BANDS + j) % nb) != 0).astype(jnp.int32)
            s = jnp.where(_band_masks(not_first), _dot(q_ref[lo:hi, :], kcat, NT), NEG)
            m = jnp.max(s, axis=-1, keepdims=True)
            p = jnp.exp(s - m)
            den = jnp.sum(p, axis=-1, keepdims=True)
            o_ref[lo:hi, :] = _dot(_bf(p), vcat, NN) / den
            l_ref[lo:hi, :] = jnp.broadcast_to(m + jnp.log(den), (DIL_BAND, DIL_HD))

    return pl.pallas_call(
        body, name=name, grid=(steps,), in_specs=[main, main, main, prev, prev], out_specs=[main, main],
        out_shape=[jax.ShapeDtypeStruct((rows, DIL_HD), F32)] * 2, compiler_params=_params(("parallel",)),
    )(q, k, v, k, v)


def _dil_bwd(q, k, v, do, lse, dd, nb, name):
    rows = q.shape[0]
    blk = BANDS * DIL_BAND
    steps = rows // blk
    last_band = rows // DIL_BAND - 1
    main = pl.BlockSpec((blk, DIL_HD), lambda i: (i, 0))
    prev = pl.BlockSpec((DIL_BAND, DIL_HD), lambda i: (jnp.maximum(i * BANDS - 1, 0), 0))
    nxt = pl.BlockSpec((DIL_BAND, DIL_HD), lambda i: (jnp.minimum(i * BANDS + BANDS, last_band), 0))

    def body(q_ref, k_ref, v_ref, do_ref, l_ref, dd_ref, kp_ref, vp_ref, qn_ref, don_ref, ln_ref, ddn_ref,
             dq_ref, dk_ref, dv_ref, ak_ref, av_ref):
        i = pl.program_id(0)
        ak_ref[...] = jnp.zeros_like(ak_ref)
        av_ref[...] = jnp.zeros_like(av_ref)
        for j in range(BANDS + 1):
            lo, hi = j * DIL_BAND, (j + 1) * DIL_BAND
            if j == 0:
                kcat = jnp.concatenate([kp_ref[...], k_ref[lo:hi, :]], axis=0)
                vcat = jnp.concatenate([vp_ref[...], v_ref[lo:hi, :]], axis=0)
            elif j < BANDS:
                kcat = k_ref[lo - DIL_BAND:hi, :]
                vcat = v_ref[lo - DIL_BAND:hi, :]
            else:
                kcat = jnp.concatenate([k_ref[lo - DIL_BAND:lo, :], k_ref[lo - DIL_BAND:lo, :]], axis=0)
                vcat = jnp.concatenate([v_ref[lo - DIL_BAND:lo, :], v_ref[lo - DIL_BAND:lo, :]], axis=0)
            if j < BANDS:
                qj, doj, lj, ddj = q_ref[lo:hi, :], do_ref[lo:hi, :], l_ref[lo:hi, :], dd_ref[lo:hi, :]
            else:
                qj, doj, lj, ddj = qn_ref[...], don_ref[...], ln_ref[...], ddn_ref[...]
            not_first = (((i * BANDS + j) % nb) != 0).astype(jnp.int32)
            mask = _band_masks(not_first)
            if j == BANDS:
                cidx = lax.broadcasted_iota(jnp.int32, mask.shape, 1)
                mask = jnp.logical_and(mask, cidx < DIL_BAND)
            s = jnp.where(mask, _dot(qj, kcat, NT), NEG)
            p = jnp.exp(s - jnp.concatenate([lj, lj], axis=1))
            dp = _dot(doj, vcat, NT)
            ds = _bf(p * (dp - jnp.concatenate([ddj, ddj], axis=1)))
            if j < BANDS:
                dq_ref[lo:hi, :] = _dot(ds, kcat, NN)
            ak_ref[lo:hi + DIL_BAND, :] += _dot(ds, qj, TN)
            av_ref[lo:hi + DIL_BAND, :] += _dot(_bf(p), doj, TN)
        dk_ref[...] = ak_ref[DIL_BAND:DIL_BAND + blk, :]
        dv_ref[...] = av_ref[DIL_BAND:DIL_BAND + blk, :]

    return pl.pallas_call(
        body, name=name, grid=(steps,),
        in_specs=[main] * 6 + [prev, prev] + [nxt] * 4, out_specs=[main] * 3,
        out_shape=[jax.ShapeDtypeStruct((rows, DIL_HD), F32)] * 3,
        scratch_shapes=[pltpu.VMEM((blk + 2 * DIL_BAND, DIL_HD), F32)] * 2,
        compiler_params=_params(("parallel",)),
    )(q, k, v, do, lse, dd, k, v, q, do, lse, dd)


def _dil_merge(os_, ls_, tr=256):
    t, w = os_[0].shape
    blk = pl.BlockSpec((tr, w), lambda i: (i, 0))

    def body(o1, o2, o3, l1, l2, l3, ob_ref, of_ref, lt_ref):
        a, b, c = l1[...], l2[...], l3[...]
        m = jnp.maximum(jnp.maximum(a, b), c)
        ea, eb, ec = jnp.exp(a - m), jnp.exp(b - m), jnp.exp(c - m)
        den = ea + eb + ec
        o = (ea * o1[...] + eb * o2[...] + ec * o3[...]) / den
        ob_ref[...] = o.astype(BF16)
        of_ref[...] = o
        lt_ref[...] = m + jnp.log(den)

    return pl.pallas_call(
        body, name="dil_merge", grid=(t // tr,), in_specs=[blk] * 6, out_specs=[blk] * 3,
        out_shape=[jax.ShapeDtypeStruct((t, w), BF16), jax.ShapeDtypeStruct((t, w), F32),
                   jax.ShapeDtypeStruct((t, w), F32)],
        compiler_params=_params(("parallel",)),
    )(*os_, *ls_)


def _dil_bwd_prep(dmix, o_d, tr=256):
    t, w = o_d.shape
    blk = pl.BlockSpec((tr, w), lambda i: (i, 0))

    def body(do_ref, o_ref, dob_ref, dd_ref):
        do = do_ref[...]
        prod = do * o_ref[...]
        dob_ref[...] = do.astype(BF16)
        for h in range(DIL_HEADS):
            hs = slice(h * DIL_HD, (h + 1) * DIL_HD)
            dd_ref[:, hs] = jnp.broadcast_to(jnp.sum(prod[:, hs], axis=-1, keepdims=True), (tr, DIL_HD))

    return pl.pallas_call(
        body, name="dil_bwd_prep", grid=(t // tr,),
        in_specs=[pl.BlockSpec((tr, w), lambda i: (i, 1)), blk], out_specs=[blk, blk],
        out_shape=[jax.ShapeDtypeStruct((t, w), BF16), jax.ShapeDtypeStruct((t, w), F32)],
        compiler_params=_params(("parallel",)),
    )(dmix, o_d)


def _gather_rows(dst_ref, src_ref, t, d, cast=None):
    n = t // d
    for r in range(d):
        v = src_ref[pl.ds(r, n, stride=d), :] if d > 1 else src_ref[...]
        dst_ref[r * n:(r + 1) * n, :] = v if cast is None else v.astype(cast)


def _tri_mask():
    r = lax.broadcasted_iota(jnp.int32, (DIL_BAND, DIL_BAND), 0)
    c = lax.broadcasted_iota(jnp.int32, (DIL_BAND, DIL_BAND), 1)
    return c <= r


def _dil_fwd_all(qr, kr, h_b):
    t = qr.shape[0]
    nbands = t // DIL_BAND
    nbr = len(DIL_DILATIONS)
    hoff = DIL_HEADS

    def col(off):
        return pl.BlockSpec((t, DIL_HD), lambda h: (0, off + h), pipeline_mode=pl.Buffered(1))

    outb = pl.BlockSpec((t, DIL_HD), lambda h: (0, h))

    def body(q_ref, k_ref, v_ref, ob_ref, of_ref, lt_ref, qs, ks, vs, os_, ls_, *br):
        obr, lbr = br[:nbr], br[nbr:]
        for bi, d in enumerate(DIL_DILATIONS):
            n = t // d
            nb = n // DIL_BAND
            _gather_rows(qs, q_ref, t, d, BF16)
            _gather_rows(ks, k_ref, t, d, BF16)
            _gather_rows(vs, v_ref, t, d, BF16)
            s = jnp.where(_tri_mask(), _dot(qs[0:DIL_BAND, :], ks[0:DIL_BAND, :], NT), NEG)
            m = jnp.max(s, axis=-1, keepdims=True)
            pr = jnp.exp(s - m)
            den = jnp.sum(pr, axis=-1, keepdims=True)
            os_[0:DIL_BAND, :] = _dot(_bf(pr), vs[0:DIL_BAND, :], NN) / den
            ls_[0:DIL_BAND, :] = jnp.broadcast_to(m + jnp.log(den), (DIL_BAND, DIL_HD))

            def band(b, carry, nb=nb):
                st = pl.multiple_of((b - 1) * DIL_BAND, DIL_BAND)
                cur = pl.ds(st + DIL_BAND, DIL_BAND)
                both = pl.ds(st, 2 * DIL_BAND)
                not_first = ((b % nb) != 0).astype(jnp.int32)
                s = jnp.where(_band_masks(not_first), _dot(qs[cur, :], ks[both, :], NT), NEG)
                m = jnp.max(s, axis=-1, keepdims=True)
                pr = jnp.exp(s - m)
                den = jnp.sum(pr, axis=-1, keepdims=True)
                os_[cur, :] = _dot(_bf(pr), vs[both, :], NN) / den
                ls_[cur, :] = jnp.broadcast_to(m + jnp.log(den), (DIL_BAND, DIL_HD))
                return carry

            lax.fori_loop(1, nbands, band, 0, unroll=4)
            for r in range(d):
                dst = pl.ds(r, n, stride=d) if d > 1 else slice(None)
                obr[bi][dst, :] = os_[r * n:(r + 1) * n, :]
                lbr[bi][dst, :] = ls_[r * n:(r + 1) * n, :]
        rows = 512
        for c0 in range(0, t, rows):
            sl = slice(c0, c0 + rows)
            la, lb, lc = lbr[0][sl, :], lbr[1][sl, :], lbr[2][sl, :]
            m = jnp.maximum(jnp.maximum(la, lb), lc)
            ea, eb, ec = jnp.exp(la - m), jnp.exp(lb - m), jnp.exp(lc - m)
            den = ea + eb + ec
            o = (ea * obr[0][sl, :] + eb * obr[1][sl, :] + ec * obr[2][sl, :]) / den
            ob_ref[sl, :] = o.astype(BF16)
            of_ref[sl, :] = o
            lt_ref[sl, :] = m + jnp.log(den)

    w = DIL_HEADS * DIL_HD
    vm = lambda dt: pltpu.VMEM((t, DIL_HD), dt)
    return pl.pallas_call(
        body, name="dil_fwd", grid=(DIL_HEADS,), in_specs=[col(0), col(0), col(2 * hoff)],
        out_specs=[outb, outb, outb],
        out_shape=[jax.ShapeDtypeStruct((t, w), BF16), jax.ShapeDtypeStruct((t, w), F32),
                   jax.ShapeDtypeStruct((t, w), F32)],
        scratch_shapes=[vm(BF16)] * 3 + [vm(F32)] * 2 + [vm(F32)] * (2 * nbr),
        compiler_params=_params(("parallel",)),
    )(qr, kr, h_b)


def _dil_bwd_all(qr, kr, h_b, dmix, o_d, lse_tot):
    t = qr.shape[0]
    nbands = t // DIL_BAND
    hoff = DIL_HEADS

    def col(off):
        return pl.BlockSpec((t, DIL_HD), lambda h: (0, off + h), pipeline_mode=pl.Buffered(1))

    outb = pl.BlockSpec((t, DIL_HD), lambda h: (0, h))

    def body(q_ref, k_ref, v_ref, do_ref, o_ref, l_ref, dq_ref, dk_ref, dv_ref,
             qs, ks, vs, dos, lss, dds, dqs, acck, accv):
        for bi, d in enumerate(DIL_DILATIONS):
            n = t // d
            nb = n // DIL_BAND
            _gather_rows(qs, q_ref, t, d, BF16)
            _gather_rows(ks, k_ref, t, d, BF16)
            _gather_rows(vs, v_ref, t, d, BF16)
            _gather_rows(dos, do_ref, t, d, BF16)
            _gather_rows(lss, l_ref, t, d)
            for r in range(d):
                src = pl.ds(r, n, stride=d) if d > 1 else slice(None)
                prod = do_ref[src, :] * o_ref[src, :]
                dds[r * n:(r + 1) * n, :] = jnp.broadcast_to(jnp.sum(prod, axis=-1, keepdims=True), (n, DIL_HD))
            acck[...] = jnp.zeros_like(acck)
            accv[...] = jnp.zeros_like(accv)
            b0 = slice(0, DIL_BAND)
            s = jnp.where(_tri_mask(), _dot(qs[b0, :], ks[b0, :], NT), NEG)
            pr = jnp.exp(s - lss[b0, :])
            ds = _bf(pr * (_dot(dos[b0, :], vs[b0, :], NT) - dds[b0, :]))
            dqs[b0, :] = _dot(ds, ks[b0, :], NN)
            acck[DIL_BAND:2 * DIL_BAND, :] += _dot(ds, qs[b0, :], TN)
            accv[DIL_BAND:2 * DIL_BAND, :] += _dot(_bf(pr), dos[b0, :], TN)

            def band(b, carry, nb=nb):
                st = pl.multiple_of((b - 1) * DIL_BAND, DIL_BAND)
                cur = pl.ds(st + DIL_BAND, DIL_BAND)
                both = pl.ds(st, 2 * DIL_BAND)
                acc_rows = pl.ds(st + DIL_BAND, 2 * DIL_BAND)
                not_first = ((b % nb) != 0).astype(jnp.int32)
                qb, dob, lb, ddb = qs[cur, :], dos[cur, :], lss[cur, :], dds[cur, :]
                kcat, vcat = ks[both, :], vs[both, :]
                s = jnp.where(_band_masks(not_first), _dot(qb, kcat, NT), NEG)
                pr = jnp.exp(s - jnp.concatenate([lb, lb], axis=1))
                ds = _bf(pr * (_dot(dob, vcat, NT) - jnp.concatenate([ddb, ddb], axis=1)))
                dqs[cur, :] = _dot(ds, kcat, NN)
                acck[acc_rows, :] += _dot(ds, qb, TN)
                accv[acc_rows, :] += _dot(_bf(pr), dob, TN)
                return carry

            lax.fori_loop(1, nbands, band, 0, unroll=4)
            for r in range(d):
                lo = r * n
                if d == 1:
                    dq_ref[...] = dqs[...]
                    dk_ref[...] = acck[DIL_BAND:DIL_BAND + t, :]
                    dv_ref[...] = accv[DIL_BAND:DIL_BAND + t, :]
                else:
                    dst = pl.ds(r, n, stride=d)
                    dq_ref[dst, :] = dq_ref[dst, :] + dqs[lo:lo + n, :]
                    dk_ref[dst, :] = dk_ref[dst, :] + acck[DIL_BAND + lo:DIL_BAND + lo + n, :]
                    dv_ref[dst, :] = dv_ref[dst, :] + accv[DIL_BAND + lo:DIL_BAND + lo + n, :]

    w = DIL_HEADS * DIL_HD
    vm = lambda dt, extra=0: pltpu.VMEM((t + extra, DIL_HD), dt)
    return pl.pallas_call(
        body, name="dil_bwd", grid=(DIL_HEADS,),
        in_specs=[col(0), col(0), col(2 * hoff), col(hoff), col(0), col(0)], out_specs=[outb] * 3,
        out_shape=[jax.ShapeDtypeStruct((t, w), F32)] * 3,
        scratch_shapes=[vm(BF16)] * 4 + [vm(F32)] * 3 + [vm(F32, DIL_BAND)] * 2,
        compiler_params=_params(("parallel",)),
    )(qr, kr, h_b, dmix, o_d, lse_tot)


def _ca_fwd(q, memkv, tq=512):
    t, d = q.shape
    m = memkv.shape[0]
    scale = CA_HD ** -0.5

    def body(q_ref, k_ref, v_ref, o_ref, ot_ref):
        for h in range(CA_HEADS):
            hs = slice(h * CA_HD, (h + 1) * CA_HD)
            s = _dot(q_ref[:, hs], k_ref[:, hs], NT) * scale
            p = jnp.exp(s - jnp.max(s, axis=-1, keepdims=True))
            p = p / jnp.sum(p, axis=-1, keepdims=True)
            o = _dot(_bf(p), v_ref[:, hs], NN).astype(BF16)
            o_ref[:, hs] = o
            ot_ref[hs, :] = o.T

    return pl.pallas_call(
        body, name="ca_fwd", grid=(t // tq,),
        in_specs=[pl.BlockSpec((tq, d), lambda i: (i, 0)), pl.BlockSpec((m, d), lambda i: (0, 0)),
                  pl.BlockSpec((m, d), lambda i: (0, 1))],
        out_specs=[pl.BlockSpec((tq, d), lambda i: (i, 0)), pl.BlockSpec((d, tq), lambda i: (0, i))],
        out_shape=[jax.ShapeDtypeStruct((t, d), BF16), jax.ShapeDtypeStruct((d, t), BF16)],
        compiler_params=_params(("parallel",)),
    )(q, memkv, memkv)


def _ca_bwd(q, memkv, do, tq=512):
    t, d = q.shape
    m = memkv.shape[0]
    scale = CA_HD ** -0.5

    def body(q_ref, k_ref, v_ref, do_ref, dq_ref, dkv_ref):
        i = pl.program_id(0)

        @pl.when(i == 0)
        def _():
            dkv_ref[...] = jnp.zeros_like(dkv_ref)

        for h in range(CA_HEADS):
            hs = slice(h * CA_HD, (h + 1) * CA_HD)
            q_h, k_h, v_h, do_h = q_ref[:, hs], k_ref[:, hs], v_ref[:, hs], do_ref[:, hs]
            s = _dot(q_h, k_h, NT) * scale
            p = jnp.exp(s - jnp.max(s, axis=-1, keepdims=True))
            p = p / jnp.sum(p, axis=-1, keepdims=True)
            dp = _dot(do_h, v_h, NT)
            ds = _bf(p * (dp - jnp.sum(p * dp, axis=-1, keepdims=True)) * scale)
            dq_ref[:, hs] = _dot(ds, k_h, NN).astype(BF16)
            dkv_ref[:, hs] += _dot(ds, q_h, TN)
            dkv_ref[:, d + h * CA_HD: d + (h + 1) * CA_HD] += _dot(_bf(p), do_h, TN)

    return pl.pallas_call(
        body, name="ca_bwd", grid=(t // tq,),
        in_specs=[pl.BlockSpec((tq, d), lambda i: (i, 0)), pl.BlockSpec((m, d), lambda i: (0, 0)),
                  pl.BlockSpec((m, d), lambda i: (0, 1)), pl.BlockSpec((tq, d), lambda i: (i, 0))],
        out_specs=[pl.BlockSpec((tq, d), lambda i: (i, 0)), pl.BlockSpec((m, 2 * d), lambda i: (0, 0))],
        out_shape=[jax.ShapeDtypeStruct((t, d), BF16), jax.ShapeDtypeStruct((m, 2 * d), F32)],
        compiler_params=_params(("arbitrary",)),
    )(q, memkv, memkv, do)


STRIP = 256


def _shift_down(u, n, row):
    return jnp.where(row >= n, pltpu.roll(u, n, 0), 0.0)


def _shift_up(u, n, row):
    t = u.shape[0]
    return jnp.where(row < t - n, pltpu.roll(u, t - n, 0), 0.0)


def _conv(u, cw_ref, row):
    return ((cw_ref[3:4, :] + cw_ref[0:1, :] * _shift_down(u, 2, row)) + cw_ref[1:2, :] * _shift_down(u, 1, row)) \
        + cw_ref[2:3, :] * u


def _swiglu_fwd(u0, cw):
    t, w = u0.shape[0], u0.shape[1] // 2
    ns = w // STRIP
    col = pl.BlockSpec((t, STRIP), lambda j: (0, j))
    col_up = pl.BlockSpec((t, STRIP), lambda j: (0, ns + j))
    cws = pl.BlockSpec((SUBLANES, STRIP), lambda j: (0, j))
    cws_up = pl.BlockSpec((SUBLANES, STRIP), lambda j: (0, ns + j))

    def body(g_ref, u_ref, cg_ref, cu_ref, a_ref, at_ref):
        row = lax.broadcasted_iota(jnp.int32, (t, STRIP), 0)
        gate = _conv(g_ref[...].astype(F32), cg_ref, row)
        up = _conv(u_ref[...].astype(F32), cu_ref, row)
        act = (gate * _sigmoid(gate) * up).astype(BF16)
        a_ref[...] = act
        at_ref[...] = act.T

    return pl.pallas_call(
        body, name="swiglu_fwd", grid=(ns,), in_specs=[col, col_up, cws, cws_up],
        out_specs=[col, pl.BlockSpec((STRIP, t), lambda j: (j, 0))],
        out_shape=[jax.ShapeDtypeStruct((t, w), BF16), jax.ShapeDtypeStruct((w, t), BF16)],
        compiler_params=_params(("parallel",)),
    )(u0, u0, cw, cw)


def _swiglu_bwd(u0, cw, da):
    t, w = u0.shape[0], u0.shape[1] // 2
    ns = w // STRIP
    col = pl.BlockSpec((t, STRIP), lambda j: (0, j))
    col_up = pl.BlockSpec((t, STRIP), lambda j: (0, ns + j))
    cws = pl.BlockSpec((SUBLANES, STRIP), lambda j: (0, j))
    cws_up = pl.BlockSpec((SUBLANES, STRIP), lambda j: (0, ns + j))

    def conv_bwd(du, u0, cw_ref, row, du0_ref, du0t_ref, dcw_ref):
        du0 = (cw_ref[2:3, :] * du + cw_ref[1:2, :] * _shift_up(du, 1, row)) + cw_ref[0:1, :] * _shift_up(du, 2, row)
        du0 = du0.astype(BF16)
        du0_ref[...] = du0
        du0t_ref[...] = du0.T
        dcw_ref[0:1, :] = jnp.sum(du * _shift_down(u0, 2, row), axis=0, keepdims=True)
        dcw_ref[1:2, :] = jnp.sum(du * _shift_down(u0, 1, row), axis=0, keepdims=True)
        dcw_ref[2:3, :] = jnp.sum(du * u0, axis=0, keepdims=True)
        dcw_ref[3:4, :] = jnp.sum(du, axis=0, keepdims=True)
        dcw_ref[4:8, :] = jnp.zeros((4, STRIP), F32)

    def body(g_ref, u_ref, cg_ref, cu_ref, da_ref, dg0_ref, du0_ref, dut_ref, dcg_ref, dcu_ref):
        row = lax.broadcasted_iota(jnp.int32, (t, STRIP), 0)
        g0, up0 = g_ref[...].astype(F32), u_ref[...].astype(F32)
        gate = _conv(g0, cg_ref, row)
        up = _conv(up0, cu_ref, row)
        sg = _sigmoid(gate)
        da = da_ref[...].astype(F32)
        dgate = da * up * (sg * (1.0 + gate * (1.0 - sg)))
        dup = da * (gate * sg)
        conv_bwd(dgate, g0, cg_ref, row, dg0_ref, dut_ref.at[0], dcg_ref)
        conv_bwd(dup, up0, cu_ref, row, du0_ref, dut_ref.at[1], dcu_ref)

    return pl.pallas_call(
        body, name="swiglu_bwd", grid=(ns,), in_specs=[col, col_up, cws, cws_up, col],
        out_specs=[col, col, pl.BlockSpec((2, STRIP, t), lambda j: (0, j, 0)), cws, cws],
        out_shape=[jax.ShapeDtypeStruct((t, w), BF16), jax.ShapeDtypeStruct((t, w), BF16),
                   jax.ShapeDtypeStruct((2, w, t), BF16),
                   jax.ShapeDtypeStruct((SUBLANES, w), F32), jax.ShapeDtypeStruct((SUBLANES, w), F32)],
        compiler_params=_params(("parallel",)),
    )(u0, u0, cw, cw, da)


def _ffn_win_grad(dut, x2b, tn=512):
    t, d = x2b.shape
    sp, sw = FF_SLAB_P, FF_SLAB

    def body(a_ref, b_ref, o_ref, ob_ref):
        res = _dot(a_ref[...], b_ref[...], NN)
        o_ref[...] = res[:sw, :]
        ob_ref[...] = res[:sw, :].astype(BF16)

    o_spec = pl.BlockSpec((None, sw, tn), lambda j, n: (j, 0, n))
    return pl.pallas_call(
        body, name="mm_g_ffn_in", grid=(8, d // tn),
        in_specs=[pl.BlockSpec((None, sp, t), lambda j, n: (j // 4, j % 4, 0)),
                  pl.BlockSpec((t, tn), lambda j, n: (0, n))],
        out_specs=[o_spec, o_spec],
        out_shape=[jax.ShapeDtypeStruct((8, sw, d), F32), jax.ShapeDtypeStruct((8, sw, d), BF16)],
        compiler_params=_params(("parallel", "parallel")),
    )(dut, x2b)


def _tile2d(r, c, limit=1 << 20):
    tr, tc = r, c
    while tr * tc * 4 > limit:
        if tr % (2 * SUBLANES) == 0:
            tr //= 2
        elif tc % (2 * LANES) == 0:
            tc //= 2
        else:
            break
    return tr, tc


def _adamw_math(w, m, v, g):
    c1 = 1.0 - ADAM_B1 ** ADAM_STEP
    c2 = 1.0 - ADAM_B2 ** ADAM_STEP
    mm = ADAM_B1 * m + (1.0 - ADAM_B1) * g
    vv = ADAM_B2 * v + (1.0 - ADAM_B2) * (g * g)
    delta = -ADAM_LR * ((mm / c1) / (jnp.sqrt(vv / c2) + ADAM_EPS) + ADAM_WD * w)
    return delta, mm, vv


def _adamw(w, m, v, g, name):
    r, c = w.shape
    blk = pl.BlockSpec((r, c), lambda i: (0, 0))

    def body(w_ref, m_ref, v_ref, gi_ref, g_ref, d_ref, nm_ref, nv_ref):
        g = gi_ref[...]
        d_ref[...], nm_ref[...], nv_ref[...] = _adamw_math(w_ref[...], m_ref[...], v_ref[...], g)
        g_ref[...] = g

    return pl.pallas_call(body, name=name, grid=(1,), in_specs=[blk] * 4, out_specs=[blk] * 4,
                          out_shape=[jax.ShapeDtypeStruct((r, c), F32)] * 4,
                          compiler_params=_params(("arbitrary",)))(w, m, v, g)


def _pair_add(gs, ra, core, name):
    _, _, r, c = gs.shape
    tr, tc = _tile2d(r, c)
    blk = pl.BlockSpec((None, tr, tc), lambda k, i, j, s: (k, i, j))

    def body(s_ref, g_ref, r_ref, o_ref, ob_ref):
        p = g_ref[...] + r_ref[...]
        o_ref[...] = p
        ob_ref[...] = p.astype(BF16)

    gspec = pltpu.PrefetchScalarGridSpec(
        num_scalar_prefetch=1, grid=(4, r // tr, c // tc),
        in_specs=[pl.BlockSpec((None, None, tr, tc), lambda k, i, j, s: (k, s[0], i, j)), blk], out_specs=[blk, blk])
    return pl.pallas_call(body, name=name, grid_spec=gspec,
                          out_shape=[jax.ShapeDtypeStruct((4, r, c), F32), jax.ShapeDtypeStruct((4, r, c), BF16)],
                          compiler_params=_params(("parallel", "parallel", "parallel")))(core, gs, ra)


def _small_reduce(gathered):
    nd, r, n = gathered.shape
    tn = 2048 if n % 2048 == 0 else n
    def body(g_ref, s_ref, t_ref):
        s = g_ref[0]
        for dv in range(1, nd):
            s = s + g_ref[dv]
        s_ref[...] = s
        t_ref[...] = jnp.broadcast_to(jnp.sum(s, axis=0, keepdims=True), (r, tn))

    return pl.pallas_call(
        body, name="small_reduce", grid=(n // tn,),
        in_specs=[pl.BlockSpec((nd, r, tn), lambda j: (0, 0, j))],
        out_specs=[pl.BlockSpec((r, tn), lambda j: (0, j))] * 2,
        out_shape=[jax.ShapeDtypeStruct((r, n), F32)] * 2, compiler_params=_params(("parallel",)),
    )(gathered)


HBM = pl.BlockSpec(memory_space=pltpu.HBM)


def _all_gather(arrs, name):
    n = len(arrs)

    def body(*refs):
        ins, outs = refs[:n], refs[n:2 * n]
        send, recv, lsem = refs[2 * n:]
        x, y, c = lax.axis_index("x"), lax.axis_index("y"), lax.axis_index("c")
        me, sib = (x, y, c), (x, y, 1 - c)
        chips = [(1 - x, y), (x, 1 - y), (1 - x, 1 - y)]

        def slot(w, p):
            return outs[w].at[4 * p[0] + 2 * p[1] + p[2]]

        def cp(w, k, block, to, src=None):
            return pltpu.make_async_remote_copy(
                src_ref=slot(w, block) if src is None else src, dst_ref=slot(w, block),
                send_sem=send.at[w * 7 + k], recv_sem=recv.at[w * 7 + k], device_id=to, device_id_type=MESH)

        mine = [pltpu.make_async_copy(ins[w], slot(w, me), lsem.at[w]) for w in range(n)]
        for m in mine:
            m.start()
        first = []
        for w in range(n):
            first.append(cp(w, 0, me, sib, src=ins[w]))
            first += [cp(w, 1 + j, me, (*chip, c), src=ins[w]) for j, chip in enumerate(chips)]
        for f in first:
            f.start()
        passed = []
        for j, chip in enumerate(chips):
            for w in range(n):
                cp(w, 1 + j, (*chip, c), me).wait_recv()
                fwd = cp(w, 4 + j, (*chip, c), sib)
                fwd.start()
                passed.append(fwd)
        for w in range(n):
            cp(w, 0, sib, me).wait_recv()
            for j, chip in enumerate(chips):
                cp(w, 4 + j, (*chip, 1 - c), me).wait_recv()
        for f in first + passed:
            f.wait_send()
        for m in mine:
            m.wait()

    return pl.pallas_call(
        body, name=name, in_specs=[HBM] * n, out_specs=[HBM] * n,
        out_shape=[jax.ShapeDtypeStruct((8,) + a.shape, a.dtype) for a in arrs],
        scratch_shapes=[pltpu.SemaphoreType.DMA((7 * n,)), pltpu.SemaphoreType.DMA((7 * n,)),
                        pltpu.SemaphoreType.DMA((n,))],
    )(*arrs)


def _sibling_exchange(arrs, name):
    n = len(arrs)

    def body(*refs):
        ins, outs = refs[:n], refs[n:2 * n]
        send, recv = refs[2 * n:]
        x, y, c = lax.axis_index("x"), lax.axis_index("y"), lax.axis_index("c")
        copies = [pltpu.make_async_remote_copy(
            src_ref=ins[w].at[:, 1 - c], dst_ref=outs[w], send_sem=send.at[w], recv_sem=recv.at[w],
            device_id=(x, y, 1 - c), device_id_type=MESH) for w in range(n)]
        for cpy in copies:
            cpy.start()
        for cpy in copies:
            cpy.wait()

    return pl.pallas_call(
        body, name=name, in_specs=[HBM] * n, out_specs=[HBM] * n,
        out_shape=[jax.ShapeDtypeStruct((a.shape[0],) + a.shape[2:], a.dtype) for a in arrs],
        scratch_shapes=[pltpu.SemaphoreType.DMA((n,)), pltpu.SemaphoreType.DMA((n,))],
    )(*arrs)


def _chip_exchange(arrs, name):
    n = len(arrs)

    def body(*refs):
        ins, outs = refs[:n], refs[n:2 * n]
        send, recv = refs[2 * n:]
        x, y, c = lax.axis_index("x"), lax.axis_index("y"), lax.axis_index("c")
        chips = [(1 - x, y), (x, 1 - y), (1 - x, 1 - y)]
        copies = []
        for w in range(n):
            for j, (cx, cy) in enumerate(chips):
                copies.append(pltpu.make_async_remote_copy(
                    src_ref=ins[w].at[2 * cx + cy], dst_ref=outs[w].at[j], send_sem=send.at[3 * w + j],
                    recv_sem=recv.at[3 * w + j], device_id=(cx, cy, c), device_id_type=MESH))
        for cpy in copies:
            cpy.start()
        for cpy in copies:
            cpy.wait()

    return pl.pallas_call(
        body, name=name, in_specs=[HBM] * n, out_specs=[HBM] * n,
        out_shape=[jax.ShapeDtypeStruct((3,) + a.shape[1:], a.dtype) for a in arrs],
        scratch_shapes=[pltpu.SemaphoreType.DMA((3 * n,)), pltpu.SemaphoreType.DMA((3 * n,))],
    )(*arrs)


SEM = pl.BlockSpec(memory_space=pltpu.SEMAPHORE)
ANY = pl.BlockSpec(memory_space=pl.ANY)
EFFECT = pltpu.SideEffectType.DATAFLOW_SIDE_EFFECTING
N_PEERS = 7


def _peers(x, y, c):
    return [((1 - x) if k & 4 else x, (1 - y) if k & 2 else y, (1 - c) if k & 1 else c) for k in range(1, 8)]


def _spread_copies(src_refs, land_refs, send, recv, gather):
    x, y, c = lax.axis_index("x"), lax.axis_index("y"), lax.axis_index("c")
    me = 4 * x + 2 * y + c
    copies = []
    for w in range(len(src_refs)):
        for k, (px, py, pc) in enumerate(_peers(x, y, c)):
            p = 4 * px + 2 * py + pc
            copies.append((pltpu.make_async_remote_copy(
                src_ref=src_refs[w] if gather else src_refs[w].at[p], dst_ref=land_refs[w].at[me],
                send_sem=send[w].at[k], recv_sem=recv[w].at[k], device_id=(px, py, pc), device_id_type=MESH),
                pltpu.make_async_remote_copy(
                src_ref=src_refs[w] if gather else src_refs[w].at[p], dst_ref=land_refs[w].at[p],
                send_sem=send[w].at[k], recv_sem=recv[w].at[k], device_id=(px, py, pc), device_id_type=MESH)))
    return copies


def _hbm(a):
    return pltpu.with_memory_space_constraint(a, pltpu.HBM)


def _spread_start(srcs, lands, after, gather, name):
    n = len(srcs)

    def body(*refs):
        src_refs, land_refs = refs[:n], refs[n:2 * n]
        outs = refs[2 * n + 1:]
        send, recv, token = outs[:n], outs[n:2 * n], outs[4 * n]
        for start, _ in _spread_copies(src_refs, land_refs, send, recv, gather):
            start.start()
        token[...] = jnp.zeros_like(token)

    res = pl.pallas_call(
        body, name=name,
        out_shape=tuple([pltpu.SemaphoreType.DMA((N_PEERS,))] * (2 * n)
                        + [pltpu.HBM(a.shape, a.dtype) for a in srcs] + [pltpu.HBM(a.shape, a.dtype) for a in lands]
                        + [jax.ShapeDtypeStruct((SUBLANES, LANES), F32)]),
        in_specs=[HBM] * (2 * n) + [ANY],
        out_specs=tuple([SEM] * (2 * n) + [HBM] * (2 * n) + [pl.BlockSpec(memory_space=pltpu.VMEM)]),
        input_output_aliases={i: 2 * n + i for i in range(2 * n)},
        compiler_params=pltpu.CompilerParams(has_side_effects=EFFECT),
    )(*[_hbm(a) for a in srcs], *[_hbm(a) for a in lands], after)
    return res[:n], res[n:2 * n], res[2 * n:3 * n], res[3 * n:4 * n], res[4 * n]


def _spread_wait(send, recv, srcs, lands, after, gather, name):
    n = len(srcs)

    def body(*refs):
        src_refs, land_refs = refs[:n], refs[n:2 * n]
        send_refs, recv_refs = refs[2 * n:3 * n], refs[3 * n:4 * n]
        for _, arrive in _spread_copies(src_refs, land_refs, send_refs, recv_refs, gather):
            arrive.wait_send()
            arrive.wait_recv()

    res = pl.pallas_call(
        body, name=name,
        out_shape=tuple([pltpu.HBM(a.shape, a.dtype) for a in srcs] + [pltpu.HBM(a.shape, a.dtype) for a in lands]),
        in_specs=[HBM] * (2 * n) + [SEM] * (2 * n) + [ANY],
        out_specs=tuple([HBM] * (2 * n)),
        input_output_aliases={i: i for i in range(2 * n)},
        compiler_params=pltpu.CompilerParams(has_side_effects=EFFECT),
    )(*srcs, *lands, *send, *recv, after)
    return res[n:]


def _landing(shape, dtype, own, me):
    return lax.dynamic_update_index_in_dim(lax.empty((8,) + shape, dtype), own, me, 0)


def _pad_cols(a, to):
    return jnp.pad(a, ((0, 0), (0, to - a.shape[1])))


N_GLR = GLA_W + GLA_RANK
FF_SLAB = D_FF // 4
FF_SLAB_P = FFP // 4


TRANSPOSED = ("w_in", "ffn_w_in")


def _prepare_sub1(gath):
    w_in_t = gath["w_in"].reshape(-1, gath["w_in"].shape[2])
    w2 = jnp.concatenate([gath["gla_gate_w2"][s] for s in range(8)], axis=1)
    return {"w_a_t": jnp.pad(w_in_t[:N_GLR], ((0, HA_W - N_GLR), (0, 0))), "w_b_t": w_in_t[N_GLR:],
            "w2p": jnp.pad(w2, ((0, LANES - GLA_RANK), (0, 0)))}


def _prepare_ffn_in(g):
    f = jnp.pad(g, ((0, 0), (0, FF_SLAB_P - FF_SLAB), (0, 0)))
    return f.reshape(2 * FFP, f.shape[2])


def _prepare_ffn_out(g):
    return jnp.pad(g.reshape(4, FF_SLAB, -1), ((0, 0), (0, FF_SLAB_P - FF_SLAB), (0, 0))).reshape(FFP, -1)


def _prepare_conv(g, conv_b):
    padc = FF_SLAB_P - FF_SLAB
    cw = jnp.pad(g, ((0, 0), (0, 0), (0, padc)))
    cb = jnp.pad(conv_b.reshape(8, 1, FF_SLAB), ((0, 0), (0, 0), (0, padc)))
    rows = jnp.concatenate([cw, cb, jnp.zeros((8, 4, FF_SLAB_P), F32)], axis=1)
    return jnp.concatenate([rows[s] for s in range(8)], axis=1)


def _prepare_ffn(gath, conv_b):
    return {"w_ffn_t": _prepare_ffn_in(gath["ffn_w_in"]), "wo": _prepare_ffn_out(gath["ffn_w_out"]),
            "cw": _prepare_conv(gath["ffn_conv_w"], conv_b)}


def _unpad_ff(a):
    r = a.shape[0]
    return a.reshape(r, 4, FF_SLAB_P)[:, :, :FF_SLAB].reshape(r, D_FF)


def _grad_slabs(g):
    w_in_t = jnp.concatenate([g["w_a_t"][:N_GLR], g["w_b_t"]], axis=0)
    s = {"w_in": w_in_t.reshape(4, 2, w_in_t.shape[0] // 8, w_in_t.shape[1])}
    for n in ("w_out", "ca_wq", "ca_wo"):
        s[n] = _to_slabs(n, g[n])
    for n in ("ca_wkv", "ffn_w_in"):
        s[n] = g[n].reshape((4, 2) + g[n].shape[1:])
    wo = g["wo"].reshape(4, FF_SLAB_P, -1)[:, :FF_SLAB]
    s["ffn_w_out"] = wo.reshape(4, 2, FF_SLAB // 2, wo.shape[-1])
    return s


class _AtHand:
    def __init__(self, p):
        self.p = p
        self.token = None

    def sub2(self, after):
        return self.p

    def ffn_in(self, after):
        return self.p["w_ffn_t"]

    def ffn_out(self, after):
        return self.p["wo"]

    def grads_out(self, group, slabs):
        pass

    def small_out(self, parts):
        pass


def _local_step(x, mem, positions, target, p, small, stages=None):
    t, d = x.shape
    stages = _AtHand(p) if stages is None else stages
    w_a_t, w_b_t, w2p, cw = p["w_a_t"], p["w_b_t"], p["w2p"], p["cw"]
    tabs = _rope_tables(positions)
    xb = x.astype(BF16) if stages.token is None else (x + stages.token[0, 0]).astype(BF16)
    memb = mem.astype(BF16)

    h_a = _matmul(xb, w_a_t, "nt", F32, 1024, 640, d, "mm_h_a")
    h_b = _matmul(xb, w_b_t, "nt", F32, 1024, 1024, d, "mm_h_b")
    o_g, o_raw, s_before = _gla_fwd(h_a, w2p, small["gla_gate_b"], small["gla_norm_g"])
    qr, kr = _rope_fwd(h_b, tabs)
    o_d_b, o_d, lse_tot = _dil_fwd_all(qr, kr, h_b)
    mixin = jnp.concatenate([o_g, o_d_b], axis=1)
    wts = stages.sub2(mixin)
    mix = _matmul(mixin, wts["w_out"], "nn", F32, 1024, 1024, d, "mm_mix")
    x1, x1b, x1t = _ln_fwd(x, mix, small["ln1_g"], small["ln1_b"], "ln1_fwd")

    q_ca = _matmul(x1b, wts["ca_wq"], "nn", BF16, 1024, 1024, d, "mm_caq")
    kvw = wts["ca_wkv"].shape[2]
    memkv = _matmul(memb, wts["ca_wkv"], "nn", BF16, mem.shape[0], kvw, d, "mm_memkv", b_slabs=True)
    o_c, o_ct = _ca_fwd(q_ca, memkv)
    ca_out = _matmul(o_c, wts["ca_wo"], "nn", F32, 1024, 1024, d, "mm_cao")
    x2, x2b, x2t = _ln_fwd(x1, ca_out, small["ln2_g"], small["ln2_b"], "ln2_fwd")

    w_ffn_t = stages.ffn_in(x2b)
    u0 = _matmul(x2b, w_ffn_t, "nt", BF16, 1024, 1024, d, "mm_u0")
    act, act_t = _swiglu_fwd(u0, cw)
    wo = stages.ffn_out(act)
    ffn = _matmul(act, wo, "nn", F32, 512, 512, FFP, "mm_ffn")

    dp3, dp3b, dg3, db3, loss_part = _ln_bwd(x2, ffn, small["ln3_g"], small["ln3_b"], target, True, "ln3_bwd")
    g_wo, g_wo16 = _matmul(act_t, dp3b, "nn", F32, 512, 1024, t, "mm_g_wo", also_bf16=True)
    dact = _matmul(dp3b, wo, "nt", BF16, 1024, 512, d, "mm_dact")
    dug, duu, du_t, dcwg, dcwu = _swiglu_bwd(u0, cw, dact)
    g_ffn_in, g_ffn_in16 = _ffn_win_grad(du_t, x2b)

    def wo_slabs(a):
        a = a.reshape(4, FF_SLAB_P, -1)[:, :FF_SLAB]
        return a.reshape(8, FF_SLAB // 2, a.shape[-1])

    sent = stages.grads_out("ffn", {"ffn_w_out": (wo_slabs(g_wo), wo_slabs(g_wo16)), "ffn_w_in": (g_ffn_in, g_ffn_in16)})
    dx2 = _matmul(dug, w_ffn_t, "nn", F32, 512, 1024, FFP, "mm_dx2_g", resid=dp3, resid_scale=ALPHA, dep=sent)
    dx2 = _matmul(duu, w_ffn_t, "nn", F32, 512, 1024, FFP, "mm_dx2_u", resid=dx2, b_k_off=1)

    dp2, dp2b, dg2, db2 = _ln_bwd(x1, ca_out, small["ln2_g"], small["ln2_b"], dx2, False, "ln2_bwd")
    g_cao, g_cao16 = _matmul(o_ct, dp2b, "nn", F32, 512, 1024, t, "mm_g_cao", also_bf16=True)
    do_c = _matmul(dp2b, wts["ca_wo"], "nt", BF16, 1024, 1024, d, "mm_do_c")
    dq_ca, dmemkv = _ca_bwd(q_ca, memkv, do_c)
    g_caq, g_caq16 = _matmul(x1t, dq_ca, "nn", F32, 512, 1024, t, "mm_g_caq", also_bf16=True)
    g_cakv, g_cakv16 = _matmul(memb, dmemkv.astype(BF16), "tn", F32, 512, kvw, mem.shape[0], "mm_g_cakv",
                               out_slabs=True, also_bf16=True)
    dx1 = _matmul(dq_ca, wts["ca_wq"], "nt", F32, 1024, 1024, d, "mm_dx1", resid=dp2, resid_scale=ALPHA)

    dp1, dp1b, dg1, db1 = _ln_bwd(x, mix, small["ln1_g"], small["ln1_b"], dx1, False, "ln1_bwd")
    g_wout, g_wout16 = _matmul(mixin, dp1b, "tn", F32, 512, 1024, t, "mm_g_wout", also_bf16=True)

    def row_slabs(a):
        return a.reshape(8, a.shape[0] // 8, a.shape[1])

    sent = stages.grads_out("attn", {"ca_wo": (row_slabs(g_cao), row_slabs(g_cao16)),
                                     "ca_wq": (row_slabs(g_caq), row_slabs(g_caq16)), "ca_wkv": (g_cakv, g_cakv16),
                                     "w_out": (row_slabs(g_wout), row_slabs(g_wout16))})
    dmix = _matmul(dp1b, wts["w_out"], "nt", F32, 1024, 1024, d, "mm_dmix", dep=sent)
    dh_a, dw2, dgate_b, dnorm_g = _gla_bwd(h_a, w2p, small["gla_gate_b"], small["gla_norm_g"], o_raw, s_before, dmix)
    small_parts = {
        "gla_gate_b": dgate_b, "gla_norm_g": dnorm_g, "ln1_g": dg1, "ln1_b": db1, "ln2_g": dg2, "ln2_b": db2,
        "ln3_g": dg3, "ln3_b": db3,
        "conv": jnp.concatenate([_unpad_ff(dcwg), _unpad_ff(dcwu)], axis=1),
        "gla_gate_w2": dw2[:GLA_RANK],
    }
    sent = stages.small_out(small_parts)
    dq_d, dk_d, dv_d = _dil_bwd_all(qr, kr, h_b, dmix, o_d, lse_tot)
    dh_b = _dil_dh(dq_d, dk_d, dv_d, tabs)
    g_wa_t, g_wa16 = _matmul(dh_a, xb, "tn", F32, 640, 1024, t, "mm_g_wa", also_bf16=True, dep=sent)
    g_wb_t, g_wb16 = _matmul(dh_b, xb, "tn", F32, 512, 1024, t, "mm_g_wb", also_bf16=True)

    def w_in_slabs(a, b):
        full = jnp.concatenate([a[:N_GLR], b], axis=0)
        return full.reshape(8, full.shape[0] // 8, full.shape[1])

    sent = stages.grads_out("w_in", {"w_in": (w_in_slabs(g_wa_t, g_wb_t), w_in_slabs(g_wa16, g_wb16))})
    dx = _matmul(dh_a, w_a_t, "nn", F32, 512, 1024, HA_W, "mm_dx_a", resid=dp1, resid_scale=ALPHA, dep=sent)
    dx = _matmul(dh_b, w_b_t, "nn", F32, 512, 1024, HB_W, "mm_dx_b", resid=dx)

    grads = {"w_a_t": g_wa_t, "w_b_t": g_wb_t, "w_out": g_wout, "ca_wq": g_caq, "ca_wkv": g_cakv, "ca_wo": g_cao,
             "ffn_w_in": g_ffn_in, "wo": g_wo}
    return loss_part, dx, grads, small_parts


BIG = ("w_in", "w_out", "ca_wq", "ca_wkv", "ca_wo", "ffn_w_in", "ffn_w_out")
COL_SHARDED = ("w_in", "ca_wkv", "ffn_w_in")
SMALL_ORDER = ("gla_gate_b", "gla_norm_g", "ln1_g", "ln1_b", "ln2_g", "ln2_b", "ln3_g", "ln3_b")


def _gathered_full(name, g):
    if name in COL_SHARDED:
        return g.transpose(1, 0, 2).reshape(g.shape[1], 8 * g.shape[2])
    return g.reshape(8 * g.shape[1], g.shape[2])


def _to_slabs(name, full):
    if name in COL_SHARDED:
        r, cc = full.shape
        s = full.reshape(r, 8, cc // 8).transpose(1, 0, 2)
    else:
        rr, c = full.shape
        s = full.reshape(8, rr // 8, c)
    return s.reshape((4, 2) + s.shape[1:])


def kernel(x, mem, positions, w_in, gla_gate_w2, gla_gate_b, gla_norm_g, w_out, ln1_g, ln1_b, ca_wq, ca_wkv, ca_wo, ln2_g, ln2_b, ffn_w_in, ffn_conv_w, ffn_conv_b, ffn_w_out, ln3_g, ln3_b, loss_target, m_w_in, m_gla_gate_w2, m_gla_gate_b, m_gla_norm_g, m_w_out, m_ln1_g, m_ln1_b, m_ca_wq, m_ca_wkv, m_ca_wo, m_ln2_g, m_ln2_b, m_ffn_w_in, m_ffn_conv_w, m_ffn_conv_b, m_ffn_w_out, m_ln3_g, m_ln3_b, v_w_in, v_gla_gate_w2, v_gla_gate_b, v_gla_norm_g, v_w_out, v_ln1_g, v_ln1_b, v_ca_wq, v_ca_wkv, v_ca_wo, v_ln2_g, v_ln2_b, v_ffn_w_in, v_ffn_conv_w, v_ffn_conv_b, v_ffn_w_out, v_ln3_g, v_ln3_b):
    weights = dict(w_in=w_in, gla_gate_w2=gla_gate_w2, gla_gate_b=gla_gate_b, gla_norm_g=gla_norm_g, w_out=w_out,
                   ln1_g=ln1_g, ln1_b=ln1_b, ca_wq=ca_wq, ca_wkv=ca_wkv, ca_wo=ca_wo, ln2_g=ln2_g, ln2_b=ln2_b,
                   ffn_w_in=ffn_w_in, ffn_conv_w=ffn_conv_w, ffn_conv_b=ffn_conv_b, ffn_w_out=ffn_w_out,
                   ln3_g=ln3_g, ln3_b=ln3_b)
    moms = dict(w_in=(m_w_in, v_w_in), gla_gate_w2=(m_gla_gate_w2, v_gla_gate_w2), gla_gate_b=(m_gla_gate_b, v_gla_gate_b),
                gla_norm_g=(m_gla_norm_g, v_gla_norm_g), w_out=(m_w_out, v_w_out), ln1_g=(m_ln1_g, v_ln1_g),
                ln1_b=(m_ln1_b, v_ln1_b), ca_wq=(m_ca_wq, v_ca_wq), ca_wkv=(m_ca_wkv, v_ca_wkv), ca_wo=(m_ca_wo, v_ca_wo),
                ln2_g=(m_ln2_g, v_ln2_g), ln2_b=(m_ln2_b, v_ln2_b), ffn_w_in=(m_ffn_w_in, v_ffn_w_in),
                ffn_conv_w=(m_ffn_conv_w, v_ffn_conv_w), ffn_conv_b=(m_ffn_conv_b, v_ffn_conv_b),
                ffn_w_out=(m_ffn_w_out, v_ffn_w_out), ln3_g=(m_ln3_g, v_ln3_g), ln3_b=(m_ln3_b, v_ln3_b))
    order = list(weights)
    xi, yi, ci = lax.axis_index("x"), lax.axis_index("y"), lax.axis_index("c")
    me = 4 * xi + 2 * yi + ci

    def travel(n, a):
        return jnp.swapaxes(a, 1, 2) if n in TRANSPOSED else a

    shard = {n: travel(n, weights[n]).astype(BF16)[0] for n in BIG}
    first = _all_gather([shard["w_in"], gla_gate_w2.astype(BF16)[0], ffn_conv_w[0]], "ag_first")
    p = _prepare_sub1({"w_in": first[0], "gla_gate_w2": first[1]})
    p["cw"] = _prepare_conv(first[2], ffn_conv_b)
    later = ("w_out", "ca_wq", "ca_wkv", "ca_wo", "ffn_w_in", "ffn_w_out")
    srcs = [shard[n] for n in later]
    lands = [_landing(shard[n].shape, BF16, shard[n], me) for n in later]
    send, recv, srcs, lands, token = _spread_start(srcs, lands, first[0], True, "ag_rest_start")

    class stages:
        pass

    stages.token = token

    def arrived(lo, hi, after, name):
        return _spread_wait(send[lo:hi], recv[lo:hi], srcs[lo:hi], lands[lo:hi], after, True, name)

    def sub2(after):
        g = dict(zip(later[:4], arrived(0, 4, after, "ag_wait_attn")))
        w = {n: _gathered_full(n, g[n]) for n in ("w_out", "ca_wq", "ca_wo")}
        w["ca_wkv"] = g["ca_wkv"]
        return w

    stages.sub2 = sub2
    stages.ffn_in = lambda after: _prepare_ffn_in(arrived(4, 5, after, "ag_wait_ffn_in")[0])
    stages.ffn_out = lambda after: _prepare_ffn_out(arrived(5, 6, after, "ag_wait_ffn_out")[0])
    sent = {}

    def grads_out(group, slabs):
        names = list(slabs)
        srcs16 = [slabs[n][1] for n in names]
        zones = [_landing(s.shape[1:], BF16, jnp.zeros(s.shape[1:], BF16), me) for s in srcs16]
        snd, rcv, s_thru, l_thru, tok = _spread_start(srcs16, zones, slabs[names[0]][0], False, f"rs_{group}_start")
        sent[group] = (names, [slabs[n][0] for n in names], (snd, rcv, s_thru, l_thru))
        return tok

    stages.grads_out = grads_out
    small_sent = []

    def small_out(parts):
        packed = jnp.concatenate([parts[n] for n in SMALL_ORDER] + [parts["conv"],
                                 parts["gla_gate_w2"].reshape(SUBLANES, -1)], axis=1)
        packed = jnp.pad(packed, ((0, 0), (0, (-packed.shape[1]) % 2048)))
        zone = _landing(packed.shape, F32, packed, me)
        snd, rcv, s_thru, l_thru, tok = _spread_start([packed], [zone], packed, True, "ag_small_start")
        small_sent.append((snd, rcv, s_thru, l_thru))
        return tok

    stages.small_out = small_out
    small = dict(gla_gate_b=gla_gate_b, gla_norm_g=gla_norm_g, ln1_g=ln1_g, ln1_b=ln1_b, ln2_g=ln2_g, ln2_b=ln2_b,
                 ln3_g=ln3_g, ln3_b=ln3_b)

    loss_part, dx, grads, small_parts = _local_step(x[0], mem[0], positions[0], loss_target[0], p, small, stages)
    loss = lax.psum(jnp.sum(loss_part), ("x", "y", "c"))

    out = {}
    (allp,) = _spread_wait(*small_sent[0], dx, True, "ag_small_wait")
    dev_sum, row_sum = _small_reduce(allp)

    me1 = me.reshape(1).astype(jnp.int32)
    for group, (names, own32, handles) in sent.items():
        landed = _spread_wait(*handles, dev_sum if group == "w_in" else dx, False, f"rs_{group}_wait")
        for n, own, land in zip(names, own32, landed):
            m_, v_ = moms[n]
            res4 = _adamw_direct(travel(n, weights[n]), travel(n, m_), travel(n, v_), own, land, me1, f"adamw_{n}")
            out[n] = [travel(n, a) for a in res4]

    off = 0
    for n in SMALL_ORDER:
        width = weights[n].shape[1]
        g = row_sum[0:1, off:off + width]
        off += width
        m_, v_ = moms[n]
        out[n] = _adamw(weights[n], m_, v_, g, f"adamw_{n}")
    conv_g = dev_sum[:, off:off + 2 * D_FF]
    off += 2 * D_FF
    g_cb = conv_g[3:4]
    out["ffn_conv_b"] = _adamw(ffn_conv_b, m_ffn_conv_b, v_ffn_conv_b, g_cb, "adamw_ffn_conv_b")
    wsh = ffn_conv_w.shape[2]
    g_cw = lax.dynamic_slice_in_dim(conv_g[0:3], me * wsh, wsh, axis=1)
    out["ffn_conv_w"] = _adamw(ffn_conv_w[0], m_ffn_conv_w[0], v_ffn_conv_w[0], g_cw, "adamw_ffn_conv_w")
    w2_g = dev_sum[:, off:off + GLA_RANK * GLA_HEADS * GLA_DK // SUBLANES].reshape(GLA_RANK, GLA_HEADS * GLA_DK)
    wsh2 = gla_gate_w2.shape[2]
    g_w2 = lax.dynamic_slice_in_dim(w2_g, me * wsh2, wsh2, axis=1)
    out["gla_gate_w2"] = _adamw(gla_gate_w2[0], m_gla_gate_w2[0], v_gla_gate_w2[0], g_w2, "adamw_gla_gate_w2")

    def shaped(n, a):
        return a.reshape(weights[n].shape)

    res = [loss, dx[None]]
    for k in range(4):
        res += [shaped(n, out[n][k]) for n in order]
    return tuple(res)


def _adamw_direct(w, m, v, own, land, me, name):
    _, r, c = w.shape
    tr, tc = _tile2d(r, c)
    blk = pl.BlockSpec((None, tr, tc), lambda i, j, s: (0, i, j))
    mine = pl.BlockSpec((None, tr, tc), lambda i, j, s: (s[0], i, j))
    slots = [pl.BlockSpec((None, tr, tc), lambda i, j, s, k=k: (k, i, j)) for k in range(8)]

    def body(s_ref, w_ref, m_ref, v_ref, p_ref, *rest):
        slot_refs, (g_ref, d_ref, nm_ref, nv_ref) = rest[:8], rest[8:]
        g = p_ref[...]
        for sr in slot_refs:
            g = g + sr[...].astype(F32)
        d_ref[...], nm_ref[...], nv_ref[...] = _adamw_math(w_ref[...], m_ref[...], v_ref[...], g)
        g_ref[...] = g

    gs = pltpu.PrefetchScalarGridSpec(num_scalar_prefetch=1, grid=(r // tr, c // tc),
                                      in_specs=[blk, blk, blk, mine] + slots, out_specs=[blk] * 4)
    return pl.pallas_call(body, name=name, grid_spec=gs, out_shape=[jax.ShapeDtypeStruct((1, r, c), F32)] * 4,
                          compiler_params=_params(("parallel", "parallel")))(me, w, m, v, own, *([land] * 8))


def _adamw_big(w, m, v, p32, rc, chip, name):
    _, r, c = w.shape
    tr, tc = _tile2d(r, c)
    blk = pl.BlockSpec((None, tr, tc), lambda i, j, s: (0, i, j))
    own = pl.BlockSpec((None, tr, tc), lambda i, j, s: (s[0], i, j))
    others = [pl.BlockSpec((None, tr, tc), lambda i, j, s, k=k: (k, i, j)) for k in range(3)]

    def body(s_ref, w_ref, m_ref, v_ref, p_ref, r0_ref, r1_ref, r2_ref, g_ref, d_ref, nm_ref, nv_ref):
        g = ((p_ref[...] + r0_ref[...].astype(F32)) + r1_ref[...].astype(F32)) + r2_ref[...].astype(F32)
        d_ref[...], nm_ref[...], nv_ref[...] = _adamw_math(w_ref[...], m_ref[...], v_ref[...], g)
        g_ref[...] = g

    gs = pltpu.PrefetchScalarGridSpec(num_scalar_prefetch=1, grid=(r // tr, c // tc),
                                      in_specs=[blk, blk, blk, own] + others, out_specs=[blk] * 4)
    return pl.pallas_call(body, name=name, grid_spec=gs, out_shape=[jax.ShapeDtypeStruct((1, r, c), F32)] * 4,
                          compiler_params=_params(("parallel", "parallel")))(chip, w, m, v, p32, rc, rc, rc)
```

```python
import functools
import math

import jax
import jax.numpy as jnp
from jax import lax
from jax.experimental import pallas as pl
from jax.experimental.pallas import tpu as pltpu

F32 = jnp.float32
BF16 = jnp.bfloat16
MESH = pl.DeviceIdType.MESH

D_MODEL = 2048
LN_EPS = 1e-5
GLA_HEADS = 4
GLA_DV = 256
GLA_DK = 128
GLA_RANK = 16
GLA_TAU = 16.0
GLA_CHUNK = 64
DIL_HD = 128
DIL_HEADS = 8
DIL_BAND = 128
DIL_DILATIONS = (1, 4, 16)
ROPE_THETA = 500000.0
ROPE_DIMS = 32
CA_HEADS = 4
CA_HD = 512
D_FF = 5504
ALPHA = 2.0 ** 0.25
ADAM_LR = 0.001
ADAM_B1 = 0.9
ADAM_B2 = 0.999
ADAM_EPS = 1e-08
ADAM_WD = 0.01
ADAM_STEP = 10

LANES = 128
SUBLANES = 8
VMEM_LIMIT = 56 * 1024 * 1024

GLA_W = 2 * GLA_HEADS * GLA_DK + 2 * GLA_HEADS * GLA_DV
HA_W = GLA_W + LANES
HB_W = 3 * DIL_HEADS * DIL_HD
FFP = 5632
NEG = -1e30


def _params(sem):
    return pltpu.CompilerParams(dimension_semantics=sem, vmem_limit_bytes=VMEM_LIMIT)


def _sigmoid(x):
    return 1.0 / (1.0 + jnp.exp(-x))


def _dot(a, b, dn, precision=None):
    return lax.dot_general(a, b, (dn, ((), ())), preferred_element_type=F32, precision=precision)


NN = ((1,), (0,))
NT = ((1,), (1,))
TN = ((0,), (0,))


def _bf(v):
    return v if v.dtype == BF16 else v.astype(BF16)


def _matmul(a, b, kind, out_dtype, tm, tn, tk, name, resid=None, resid_scale=1.0, b_k_off=0, b_slabs=False,
            out_slabs=False, also_bf16=False, dep=None):
    if b_slabs:
        assert kind != "nt" and b.shape[2] == tn
        k2, n = b.shape[1], b.shape[0] * tn
    elif kind == "nt":
        n, k2 = b.shape
    else:
        k2, n = b.shape
    (k, m) = a.shape if kind == "tn" else a.shape[::-1]
    assert k2 >= k and (k2 == k or not b_slabs) and m % tm == 0 and n % tn == 0 and k % tk == 0, \
        (name, a.shape, b.shape, tm, tn, tk)
    nk = k // tk
    dn = {"nn": NN, "nt": NT, "tn": TN}[kind]
    a_spec = pl.BlockSpec((tk, tm), lambda i, j, kk: (kk, i)) if kind == "tn" else pl.BlockSpec((tm, tk), lambda i, j, kk: (i, kk))
    if b_slabs:
        b_spec = pl.BlockSpec((None, tk, tn), lambda i, j, kk: (j, kk, 0))
    elif kind == "nt":
        b_spec = pl.BlockSpec((tn, tk), lambda i, j, kk: (j, kk + b_k_off))
    else:
        b_spec = pl.BlockSpec((tk, tn), lambda i, j, kk: (kk + b_k_off, j))
    if out_slabs:
        o_spec = pl.BlockSpec((None, tm, tn), lambda i, j, kk: (j, i, 0))
        o_shape = (n // tn, m, tn)
    else:
        o_spec = pl.BlockSpec((tm, tn), lambda i, j, kk: (i, j))
        o_shape = (m, n)
    has_resid = resid is not None

    n_in = 2 + int(has_resid) + int(dep is not None)

    def body(*refs):
        a_ref, b_ref = refs[:2]
        r_ref = refs[2] if has_resid else None
        o_ref = refs[n_in]
        ob_ref = refs[n_in + 1] if also_bf16 else None
        part = _dot(_bf(a_ref[...]), _bf(b_ref[...]), dn)

        def finish(acc):
            if has_resid:
                acc = acc + resid_scale * r_ref[...].astype(F32)
            o_ref[...] = acc.astype(out_dtype)
            if also_bf16:
                ob_ref[...] = acc.astype(BF16)

        if nk == 1:
            finish(part)
        else:
            acc_ref = refs[-1]
            kk = pl.program_id(2)

            @pl.when(kk == 0)
            def _():
                acc_ref[...] = part

            @pl.when(kk > 0)
            def _():
                acc_ref[...] += part

            @pl.when(kk == nk - 1)
            def _():
                finish(acc_ref[...])

    in_specs = [a_spec, b_spec] + ([o_spec] if has_resid else [])
    args = (a, b) + ((resid,) if has_resid else ())
    if dep is not None:
        in_specs.append(pl.BlockSpec((SUBLANES, LANES), lambda i, j, kk: (0, 0)))
        args += (dep,)
    o_struct = jax.ShapeDtypeStruct(o_shape, out_dtype)
    return pl.pallas_call(
        body, name=name, out_shape=[o_struct, jax.ShapeDtypeStruct(o_shape, BF16)] if also_bf16 else o_struct,
        grid=(m // tm, n // tn, nk), in_specs=in_specs, out_specs=[o_spec, o_spec] if also_bf16 else o_spec,
        scratch_shapes=[pltpu.VMEM((tm, tn), F32)] if nk > 1 else [],
        compiler_params=_params(("parallel", "parallel", "arbitrary")),
    )(*args)


def _ln_core(xres, f):
    p = ALPHA * xres + f
    mu = jnp.mean(p, axis=-1, keepdims=True)
    xc = p - mu
    var = jnp.mean(xc * xc, axis=-1, keepdims=True)
    rstd = lax.rsqrt(var + LN_EPS)
    return xc * rstd, rstd


def _rows8(v):
    r, c = v.shape
    return jnp.sum(v.reshape(r // SUBLANES, SUBLANES, c), axis=0)


def _ln_fwd(xres, f, g, b, name, tr=256):
    t, d = xres.shape
    row = pl.BlockSpec((tr, d), lambda i: (i, 0))
    vec = pl.BlockSpec((1, d), lambda i: (0, 0))

    def body(x_ref, f_ref, g_ref, b_ref, y_ref, yb_ref, yt_ref):
        xhat, _ = _ln_core(x_ref[...], f_ref[...])
        y = xhat * g_ref[...] + b_ref[...]
        y_ref[...] = y
        yb = y.astype(BF16)
        yb_ref[...] = yb
        yt_ref[...] = yb.T

    return pl.pallas_call(
        body, name=name, grid=(t // tr,), in_specs=[row, row, vec, vec],
        out_specs=[row, row, pl.BlockSpec((d, tr), lambda i: (0, i))],
        out_shape=[jax.ShapeDtypeStruct((t, d), F32), jax.ShapeDtypeStruct((t, d), BF16),
                   jax.ShapeDtypeStruct((d, t), BF16)],
        compiler_params=_params(("parallel",)),
    )(xres, f, g, b)


def _ln_bwd(xres, f, g, b, dy_or_target, loss_head, name, tr=256):
    t, d = xres.shape
    row = pl.BlockSpec((tr, d), lambda i: (i, 0))
    vec = pl.BlockSpec((1, d), lambda i: (0, 0))
    acc = pl.BlockSpec((SUBLANES, d), lambda i: (0, 0))
    lacc = pl.BlockSpec((SUBLANES, LANES), lambda i: (0, 0))

    def body(x_ref, f_ref, g_ref, b_ref, t_ref, dp_ref, dpb_ref, dg_ref, db_ref, *rest):
        i = pl.program_id(0)
        xhat, rstd = _ln_core(x_ref[...], f_ref[...])
        if loss_head:
            err = xhat * g_ref[...] + b_ref[...] - t_ref[...]
            dy = err * (1.0 / d)
            sq = err * err
            lanes = sq[:, :LANES]
            for kk in range(1, d // LANES):
                lanes = lanes + sq[:, kk * LANES:(kk + 1) * LANES]
            lpart = _rows8(lanes) * (0.5 / d)
        else:
            dy = t_ref[...]
        dxh = dy * g_ref[...]
        m1 = jnp.mean(dxh, axis=-1, keepdims=True)
        m2 = jnp.mean(dxh * xhat, axis=-1, keepdims=True)
        dp = rstd * (dxh - m1 - xhat * m2)
        dp_ref[...] = dp
        dpb_ref[...] = dp.astype(BF16)
        dgp = _rows8(dy * xhat)
        dbp = _rows8(dy)

        @pl.when(i == 0)
        def _():
            dg_ref[...] = dgp
            db_ref[...] = dbp
            if loss_head:
                rest[0][...] = lpart

        @pl.when(i > 0)
        def _():
            dg_ref[...] += dgp
            db_ref[...] += dbp
            if loss_head:
                rest[0][...] += lpart

    out_shape = [jax.ShapeDtypeStruct((t, d), F32), jax.ShapeDtypeStruct((t, d), BF16),
                 jax.ShapeDtypeStruct((SUBLANES, d), F32), jax.ShapeDtypeStruct((SUBLANES, d), F32)]
    out_specs = [row, row, acc, acc]
    if loss_head:
        out_shape.append(jax.ShapeDtypeStruct((SUBLANES, LANES), F32))
        out_specs.append(lacc)
    return pl.pallas_call(
        body, name=name, grid=(t // tr,), in_specs=[row, row, vec, vec, row], out_specs=out_specs,
        out_shape=out_shape, compiler_params=_params(("arbitrary",)),
    )(xres, f, g, b, dy_or_target)


def _gla_gates(glr, w2, gb):
    z = _dot(_bf(glr), w2, NN) + gb
    lg = (jnp.minimum(z, 0.0) - jnp.log(1.0 + jnp.exp(-jnp.abs(z)))) * (1.0 / GLA_TAU)
    c = z.shape[0]
    ri = lax.broadcasted_iota(jnp.int32, (c, c), 0)
    ci = lax.broadcasted_iota(jnp.int32, (c, c), 1)
    tri = (ci <= ri).astype(F32)
    bcum = _dot(tri, lg, NN, precision=lax.Precision.HIGHEST)
    blast = jnp.sum(lg, axis=0, keepdims=True)
    return z, bcum, blast, tri


def _gla_specs(t):
    c = GLA_CHUNK
    return c, t // c


def _gla_fwd(h_a, w2p, gate_b, norm_g):
    t = h_a.shape[0]
    c, n = _gla_specs(t)
    hk, hv = GLA_HEADS * GLA_DK, GLA_HEADS * GLA_DV
    scale = GLA_DK ** -0.5

    def body(q_ref, k_ref, v_ref, r_ref, glr_ref, w2_ref, gb_ref, ng_ref, og_ref, oraw_ref, sb_ref, st_ref):
        i = pl.program_id(0)

        @pl.when(i == 0)
        def _():
            st_ref[...] = jnp.zeros_like(st_ref)

        _, bcum, blast, _ = _gla_gates(glr_ref[...], w2_ref[...], gb_ref[...])
        ri = lax.broadcasted_iota(jnp.int32, (c, c), 0)
        ci = lax.broadcasted_iota(jnp.int32, (c, c), 1)
        causal = ci <= ri
        for h in range(GLA_HEADS):
            ks = slice(h * GLA_DK, (h + 1) * GLA_DK)
            vs = slice(h * GLA_DV, (h + 1) * GLA_DV)
            b_h, bl_h = bcum[:, ks], blast[:, ks]
            q_h, k_h = q_ref[:, ks], k_ref[:, ks]
            v_h = _bf(v_ref[:, vs])
            qi = _bf(q_h * scale * jnp.exp(b_h))
            ki = _bf(k_h * jnp.exp(-b_h))
            ke = _bf(k_h * jnp.exp(bl_h - b_h))
            st = st_ref[h]
            sb_ref[0, h] = st
            a = jnp.where(causal, _dot(qi, ki, NT), 0.0)
            o = _dot(_bf(a), v_h, NN) + _dot(qi, _bf(st), NT)
            st_ref[h] = st * jnp.exp(bl_h) + _dot(v_h, ke, TN)
            oraw_ref[:, vs] = o
            mu = jnp.mean(o, axis=-1, keepdims=True)
            oc = o - mu
            var = jnp.mean(oc * oc, axis=-1, keepdims=True)
            xh = oc * lax.rsqrt(var + LN_EPS)
            r_h = r_ref[:, vs]
            og_ref[:, vs] = (xh * ng_ref[:, vs] * (r_h * _sigmoid(r_h))).astype(BF16)

    return pl.pallas_call(
        body, name="gla_fwd", grid=(n,),
        in_specs=[pl.BlockSpec((c, hk), lambda i: (i, 0)), pl.BlockSpec((c, hk), lambda i: (i, 1)),
                  pl.BlockSpec((c, hv), lambda i: (i, 1)), pl.BlockSpec((c, hv), lambda i: (i, 2)),
                  pl.BlockSpec((c, LANES), lambda i: (i, GLA_W // LANES)),
                  pl.BlockSpec((LANES, hk), lambda i: (0, 0)), pl.BlockSpec((1, hk), lambda i: (0, 0)),
                  pl.BlockSpec((1, hv), lambda i: (0, 0))],
        out_specs=[pl.BlockSpec((c, hv), lambda i: (i, 0)), pl.BlockSpec((c, hv), lambda i: (i, 0)),
                   pl.BlockSpec((1, GLA_HEADS, GLA_DV, GLA_DK), lambda i: (i, 0, 0, 0))],
        out_shape=[jax.ShapeDtypeStruct((t, hv), BF16), jax.ShapeDtypeStruct((t, hv), F32),
                   jax.ShapeDtypeStruct((n, GLA_HEADS, GLA_DV, GLA_DK), F32)],
        scratch_shapes=[pltpu.VMEM((GLA_HEADS, GLA_DV, GLA_DK), F32)],
        compiler_params=_params(("arbitrary",)),
    )(h_a, h_a, h_a, h_a, h_a, w2p, gate_b, norm_g)


def _gla_bwd(h_a, w2p, gate_b, norm_g, o_raw, s_before, dmix):
    t = h_a.shape[0]
    c, n = _gla_specs(t)
    hk, hv = GLA_HEADS * GLA_DK, GLA_HEADS * GLA_DV
    scale = GLA_DK ** -0.5
    rev = lambda i: n - 1 - i

    def body(q_ref, k_ref, v_ref, r_ref, glr_ref, w2_ref, gb_ref, ng_ref, oraw_ref, sb_ref, do_ref,
             dh_ref, dw2_ref, dgb_ref, dng_ref, dst_ref):
        i = pl.program_id(0)

        @pl.when(i == 0)
        def _():
            dst_ref[...] = jnp.zeros_like(dst_ref)

        glr = glr_ref[...]
        z, bcum, blast, tri = _gla_gates(glr, w2_ref[...], gb_ref[...])
        ri = lax.broadcasted_iota(jnp.int32, (c, c), 0)
        ci = lax.broadcasted_iota(jnp.int32, (c, c), 1)
        causal = ci <= ri
        dlg_parts = []
        dng_parts = []
        for h in range(GLA_HEADS):
            ks = slice(h * GLA_DK, (h + 1) * GLA_DK)
            vs = slice(h * GLA_DV, (h + 1) * GLA_DV)
            o = oraw_ref[:, vs]
            mu = jnp.mean(o, axis=-1, keepdims=True)
            oc = o - mu
            var = jnp.mean(oc * oc, axis=-1, keepdims=True)
            rstd = lax.rsqrt(var + LN_EPS)
            xh = oc * rstd
            r_h = r_ref[:, vs]
            sg = _sigmoid(r_h)
            silu = r_h * sg
            dout = do_ref[:, vs]
            ng = ng_ref[:, vs]
            dng_parts.append(_rows8(dout * xh * silu))
            dr = dout * xh * ng * (sg * (1.0 + r_h * (1.0 - sg)))
            dxh = dout * ng * silu
            m1 = jnp.mean(dxh, axis=-1, keepdims=True)
            m2 = jnp.mean(dxh * xh, axis=-1, keepdims=True)
            do_raw = _bf(rstd * (dxh - m1 - xh * m2))
            b_h, bl_h = bcum[:, ks], blast[:, ks]
            q_h, k_h = q_ref[:, ks], k_ref[:, ks]
            v_h = _bf(v_ref[:, vs])
            eb, enb, eend = jnp.exp(b_h), jnp.exp(-b_h), jnp.exp(bl_h - b_h)
            decay = jnp.exp(bl_h)
            qi_f, ki_f, ke_f = q_h * scale * eb, k_h * enb, k_h * eend
            qi, ki, ke = _bf(qi_f), _bf(ki_f), _bf(ke_f)
            st = sb_ref[0, h]
            dst = dst_ref[h]
            dst_b = _bf(dst)
            a = _bf(jnp.where(causal, _dot(qi, ki, NT), 0.0))
            da = _bf(jnp.where(causal, _dot(do_raw, v_h, NT), 0.0))
            dv = _dot(a, do_raw, TN) + _dot(ke, dst_b, NT)
            dqi = _dot(da, ki, NN) + _dot(do_raw, _bf(st), NN)
            dki = _dot(da, qi, TN)
            dke = _dot(v_h, dst_b, NN)
            dst_ref[h] = _dot(do_raw, qi, TN) + dst * decay
            dbl = decay * jnp.sum(st * dst, axis=0, keepdims=True) + jnp.sum(dke * ke_f, axis=0, keepdims=True)
            dbc = dqi * qi_f - dki * ki_f - dke * ke_f
            dlg_parts.append(_dot(tri, dbc, TN, precision=lax.Precision.HIGHEST) + dbl)
            dh_ref[:, ks] = (dqi * eb * scale).astype(BF16)
            dh_ref[:, hk + h * GLA_DK: hk + (h + 1) * GLA_DK] = (dki * enb + dke * eend).astype(BF16)
            dh_ref[:, 2 * hk + h * GLA_DV: 2 * hk + (h + 1) * GLA_DV] = dv.astype(BF16)
            dh_ref[:, 2 * hk + hv + h * GLA_DV: 2 * hk + hv + (h + 1) * GLA_DV] = dr.astype(BF16)
        dlg = jnp.concatenate(dlg_parts, axis=1)
        dz = dlg * (1.0 / GLA_TAU) * _sigmoid(-z)
        dz_b = _bf(dz)
        dh_ref[:, GLA_W:] = _dot(dz_b, w2_ref[...], NT).astype(BF16)
        dw2p = _dot(_bf(glr), dz_b, TN)
        dgbp = _rows8(dz)
        dngp = jnp.concatenate(dng_parts, axis=1)

        @pl.when(i == 0)
        def _():
            dw2_ref[...] = dw2p
            dgb_ref[...] = dgbp
            dng_ref[...] = dngp

        @pl.when(i > 0)
        def _():
            dw2_ref[...] += dw2p
            dgb_ref[...] += dgbp
            dng_ref[...] += dngp

    return pl.pallas_call(
        body, name="gla_bwd", grid=(n,),
        in_specs=[pl.BlockSpec((c, hk), lambda i: (rev(i), 0)), pl.BlockSpec((c, hk), lambda i: (rev(i), 1)),
                  pl.BlockSpec((c, hv), lambda i: (rev(i), 1)), pl.BlockSpec((c, hv), lambda i: (rev(i), 2)),
                  pl.BlockSpec((c, LANES), lambda i: (rev(i), GLA_W // LANES)),
                  pl.BlockSpec((LANES, hk), lambda i: (0, 0)), pl.BlockSpec((1, hk), lambda i: (0, 0)),
                  pl.BlockSpec((1, hv), lambda i: (0, 0)),
                  pl.BlockSpec((c, hv), lambda i: (rev(i), 0)),
                  pl.BlockSpec((1, GLA_HEADS, GLA_DV, GLA_DK), lambda i: (rev(i), 0, 0, 0)),
                  pl.BlockSpec((c, hv), lambda i: (rev(i), 0))],
        out_specs=[pl.BlockSpec((c, HA_W), lambda i: (rev(i), 0)),
                   pl.BlockSpec((LANES, hk), lambda i: (0, 0)),
                   pl.BlockSpec((SUBLANES, hk), lambda i: (0, 0)),
                   pl.BlockSpec((SUBLANES, hv), lambda i: (0, 0))],
        out_shape=[jax.ShapeDtypeStruct((t, HA_W), BF16), jax.ShapeDtypeStruct((LANES, hk), F32),
                   jax.ShapeDtypeStruct((SUBLANES, hk), F32), jax.ShapeDtypeStruct((SUBLANES, hv), F32)],
        scratch_shapes=[pltpu.VMEM((GLA_HEADS, GLA_DV, GLA_DK), F32)],
        compiler_params=_params(("arbitrary",)),
    )(h_a, h_a, h_a, h_a, h_a, w2p, gate_b, norm_g, o_raw, s_before, dmix)


def _rope_tables(positions):
    half = ROPE_DIMS // 2
    inv_freq = ROPE_THETA ** (-jnp.arange(0, ROPE_DIMS, 2, dtype=F32) / ROPE_DIMS)
    ang = positions.astype(F32).reshape(-1, 1) * inv_freq
    cos, sin = jnp.cos(ang), jnp.sin(ang)
    t = cos.shape[0]
    one = jnp.ones((t, DIL_HD - ROPE_DIMS), F32)
    zero = jnp.zeros((t, DIL_HD - ROPE_DIMS), F32)
    zh = jnp.zeros((t, half), F32)
    return (jnp.concatenate([cos, cos, one], axis=1), jnp.concatenate([-sin, zh, zero], axis=1),
            jnp.concatenate([zh, sin, zero], axis=1))


def _rope_apply(x, c, s1, s2):
    half = ROPE_DIMS // 2
    return x * c + pltpu.roll(x, DIL_HD - half, 1) * s1 + pltpu.roll(x, half, 1) * s2


def _rope_apply_t(dy, c, s1, s2):
    half = ROPE_DIMS // 2
    return dy * c + pltpu.roll(dy * s1, half, 1) + pltpu.roll(dy * s2, DIL_HD - half, 1)


def _rope_fwd(h_b, tabs, tr=256):
    t = h_b.shape[0]
    w = DIL_HEADS * DIL_HD
    scale = DIL_HD ** -0.5
    tab = pl.BlockSpec((tr, DIL_HD), lambda i: (i, 0))
    outb = pl.BlockSpec((tr, w), lambda i: (i, 0))

    def body(q_ref, k_ref, c_ref, s1_ref, s2_ref, qo_ref, ko_ref):
        c, s1, s2 = c_ref[...], s1_ref[...], s2_ref[...]
        for h in range(DIL_HEADS):
            hs = slice(h * DIL_HD, (h + 1) * DIL_HD)
            qo_ref[:, hs] = _rope_apply(q_ref[:, hs] * scale, c, s1, s2)
            ko_ref[:, hs] = _rope_apply(k_ref[:, hs], c, s1, s2)

    return pl.pallas_call(
        body, name="rope_fwd", grid=(t // tr,),
        in_specs=[pl.BlockSpec((tr, w), lambda i: (i, 0)), pl.BlockSpec((tr, w), lambda i: (i, 1)), tab, tab, tab],
        out_specs=[outb, outb],
        out_shape=[jax.ShapeDtypeStruct((t, w), F32)] * 2,
        compiler_params=_params(("parallel",)),
    )(h_b, h_b, *tabs)


def _dil_dh(dq, dk, dv, tabs, tr=256):
    t, w = dq.shape
    scale = DIL_HD ** -0.5
    tab = pl.BlockSpec((tr, DIL_HD), lambda i: (i, 0))
    inb = pl.BlockSpec((tr, w), lambda i: (i, 0))

    def body(dq_ref, dk_ref, dv_ref, c_ref, s1_ref, s2_ref, o_ref):
        c, s1, s2 = c_ref[...], s1_ref[...], s2_ref[...]
        for h in range(DIL_HEADS):
            hs = slice(h * DIL_HD, (h + 1) * DIL_HD)
            o_ref[:, h * DIL_HD:(h + 1) * DIL_HD] = (_rope_apply_t(dq_ref[:, hs], c, s1, s2) * scale).astype(BF16)
            o_ref[:, w + h * DIL_HD: w + (h + 1) * DIL_HD] = _rope_apply_t(dk_ref[:, hs], c, s1, s2).astype(BF16)
        o_ref[:, 2 * w:] = dv_ref[...].astype(BF16)

    return pl.pallas_call(
        body, name="dil_dh", grid=(t // tr,), in_specs=[inb] * 3 + [tab] * 3,
        out_specs=pl.BlockSpec((tr, 3 * w), lambda i: (i, 0)),
        out_shape=jax.ShapeDtypeStruct((t, 3 * w), BF16), compiler_params=_params(("parallel",)),
    )(dq, dk, dv, *tabs)


BANDS = 8


def _to_branch(a, d):
    t, w = a.shape
    return a.reshape(t // d, d, w // DIL_HD, DIL_HD).transpose(1, 2, 0, 3).reshape(-1, DIL_HD)


def _from_branch(a, d, t):
    hds = a.shape[0] // t
    return a.reshape(d, hds, t // d, DIL_HD).transpose(2, 0, 1, 3).reshape(t, hds * DIL_HD)


def _band_masks(not_first):
    r = lax.broadcasted_iota(jnp.int32, (DIL_BAND, 2 * DIL_BAND), 0)
    c = lax.broadcasted_iota(jnp.int32, (DIL_BAND, 2 * DIL_BAND), 1)
    nf = jnp.full((DIL_BAND, 2 * DIL_BAND), not_first, jnp.int32)
    look_back = jnp.logical_and(jnp.logical_and(c < DIL_BAND, c >= r), nf > 0)
    own_band = jnp.logical_and(c >= DIL_BAND, (c - DIL_BAND) <= r)
    return jnp.logical_or(look_back, own_band)


def _dil_fwd(q, k, v, nb, name):
    rows = q.shape[0]
    blk = BANDS * DIL_BAND
    steps = rows // blk
    main = pl.BlockSpec((blk, DIL_HD), lambda i: (i, 0))
    prev = pl.BlockSpec((DIL_BAND, DIL_HD), lambda i: (jnp.maximum(i * BANDS - 1, 0), 0))

    def body(q_ref, k_ref, v_ref, kp_ref, vp_ref, o_ref, l_ref):
        i = pl.program_id(0)
        for j in range(BANDS):
            lo, hi = j * DIL_BAND, (j + 1) * DIL_BAND
            if j == 0:
                kcat = jnp.concatenate([kp_ref[...], k_ref[lo:hi, :]], axis=0)
                vcat = jnp.concatenate([vp_ref[...], v_ref[lo:hi, :]], axis=0)
            else:
                kcat = k_ref[lo - DIL_BAND:hi, :]
                vcat = v_ref[lo - DIL_BAND:hi, :]
            not_first = (((i * BANDS + j) % nb) != 0).astype(jnp.int32)
            s = jnp.where(_band_masks(not_first), _dot(q_ref[lo:hi, :], kcat, NT), NEG)
            m = jnp.max(s, axis=-1, keepdims=True)
            p = jnp.exp(s - m)
            den = jnp.sum(p, axis=-1, keepdims=True)
            o_ref[lo:hi, :] = _dot(_bf(p), vcat, NN) / den
            l_ref[lo:hi, :] = jnp.broadcast_to(m + jnp.log(den), (DIL_BAND, DIL_HD))

    return pl.pallas_call(
        body, name=name, grid=(steps,), in_specs=[main, main, main, prev, prev], out_specs=[main, main],
        out_shape=[jax.ShapeDtypeStruct((rows, DIL_HD), F32)] * 2, compiler_params=_params(("parallel",)),
    )(q, k, v, k, v)


def _dil_bwd(q, k, v, do, lse, dd, nb, name):
    rows = q.shape[0]
    blk = BANDS * DIL_BAND
    steps = rows // blk
    last_band = rows // DIL_BAND - 1
    main = pl.BlockSpec((blk, DIL_HD), lambda i: (i, 0))
    prev = pl.BlockSpec((DIL_BAND, DIL_HD), lambda i: (jnp.maximum(i * BANDS - 1, 0), 0))
    nxt = pl.BlockSpec((DIL_BAND, DIL_HD), lambda i: (jnp.minimum(i * BANDS + BANDS, last_band), 0))

    def body(q_ref, k_ref, v_ref, do_ref, l_ref, dd_ref, kp_ref, vp_ref, qn_ref, don_ref, ln_ref, ddn_ref,
             dq_ref, dk_ref, dv_ref, ak_ref, av_ref):
        i = pl.program_id(0)
        ak_ref[...] = jnp.zeros_like(ak_ref)
        av_ref[...] = jnp.zeros_like(av_ref)
        for j in range(BANDS + 1):
            lo, hi = j * DIL_BAND, (j + 1) * DIL_BAND
            if j == 0:
                kcat = jnp.concatenate([kp_ref[...], k_ref[lo:hi, :]], axis=0)
                vcat = jnp.concatenate([vp_ref[...], v_ref[lo:hi, :]], axis=0)
            elif j < BANDS:
                kcat = k_ref[lo - DIL_BAND:hi, :]
                vcat = v_ref[lo - DIL_BAND:hi, :]
            else:
                kcat = jnp.concatenate([k_ref[lo - DIL_BAND:lo, :], k_ref[lo - DIL_BAND:lo, :]], axis=0)
                vcat = jnp.concatenate([v_ref[lo - DIL_BAND:lo, :], v_ref[lo - DIL_BAND:lo, :]], axis=0)
            if j < BANDS:
                qj, doj, lj, ddj = q_ref[lo:hi, :], do_ref[lo:hi, :], l_ref[lo:hi, :], dd_ref[lo:hi, :]
            else:
                qj, doj, lj, ddj = qn_ref[...], don_ref[...], ln_ref[...], ddn_ref[...]
            not_first = (((i * BANDS + j) % nb) != 0).astype(jnp.int32)
            mask = _band_masks(not_first)
            if j == BANDS:
                cidx = lax.broadcasted_iota(jnp.int32, mask.shape, 1)
                mask = jnp.logical_and(mask, cidx < DIL_BAND)
            s = jnp.where(mask, _dot(qj, kcat, NT), NEG)
            p = jnp.exp(s - jnp.concatenate([lj, lj], axis=1))
            dp = _dot(doj, vcat, NT)
            ds = _bf(p * (dp - jnp.concatenate([ddj, ddj], axis=1)))
            if j < BANDS:
                dq_ref[lo:hi, :] = _dot(ds, kcat, NN)
            ak_ref[lo:hi + DIL_BAND, :] += _dot(ds, qj, TN)
            av_ref[lo:hi + DIL_BAND, :] += _dot(_bf(p), doj, TN)
        dk_ref[...] = ak_ref[DIL_BAND:DIL_BAND + blk, :]
        dv_ref[...] = av_ref[DIL_BAND:DIL_BAND + blk, :]

    return pl.pallas_call(
        body, name=name, grid=(steps,),
        in_specs=[main] * 6 + [prev, prev] + [nxt] * 4, out_specs=[main] * 3,
        out_shape=[jax.ShapeDtypeStruct((rows, DIL_HD), F32)] * 3,
        scratch_shapes=[pltpu.VMEM((blk + 2 * DIL_BAND, DIL_HD), F32)] * 2,
        compiler_params=_params(("parallel",)),
    )(q, k, v, do, lse, dd, k, v, q, do, lse, dd)


def _dil_merge(os_, ls_, tr=256):
    t, w = os_[0].shape
    blk = pl.BlockSpec((tr, w), lambda i: (i, 0))

    def body(o1, o2, o3, l1, l2, l3, ob_ref, of_ref, lt_ref):
        a, b, c = l1[...], l2[...], l3[...]
        m = jnp.maximum(jnp.maximum(a, b), c)
        ea, eb, ec = jnp.exp(a - m), jnp.exp(b - m), jnp.exp(c - m)
        den = ea + eb + ec
        o = (ea * o1[...] + eb * o2[...] + ec * o3[...]) / den
        ob_ref[...] = o.astype(BF16)
        of_ref[...] = o
        lt_ref[...] = m + jnp.log(den)

    return pl.pallas_call(
        body, name="dil_merge", grid=(t // tr,), in_specs=[blk] * 6, out_specs=[blk] * 3,
        out_shape=[jax.ShapeDtypeStruct((t, w), BF16), jax.ShapeDtypeStruct((t, w), F32),
                   jax.ShapeDtypeStruct((t, w), F32)],
        compiler_params=_params(("parallel",)),
    )(*os_, *ls_)


def _dil_bwd_prep(dmix, o_d, tr=256):
    t, w = o_d.shape
    blk = pl.BlockSpec((tr, w), lambda i: (i, 0))

    def body(do_ref, o_ref, dob_ref, dd_ref):
        do = do_ref[...]
        prod = do * o_ref[...]
        dob_ref[...] = do.astype(BF16)
        for h in range(DIL_HEADS):
            hs = slice(h * DIL_HD, (h + 1) * DIL_HD)
            dd_ref[:, hs] = jnp.broadcast_to(jnp.sum(prod[:, hs], axis=-1, keepdims=True), (tr, DIL_HD))

    return pl.pallas_call(
        body, name="dil_bwd_prep", grid=(t // tr,),
        in_specs=[pl.BlockSpec((tr, w), lambda i: (i, 1)), blk], out_specs=[blk, blk],
        out_shape=[jax.ShapeDtypeStruct((t, w), BF16), jax.ShapeDtypeStruct((t, w), F32)],
        compiler_params=_params(("parallel",)),
    )(dmix, o_d)


def _gather_rows(dst_ref, src_ref, t, d, cast=None):
    n = t // d
    for r in range(d):
        v = src_ref[pl.ds(r, n, stride=d), :] if d > 1 else src_ref[...]
        dst_ref[r * n:(r + 1) * n, :] = v if cast is None else v.astype(cast)


def _tri_mask():
    r = lax.broadcasted_iota(jnp.int32, (DIL_BAND, DIL_BAND), 0)
    c = lax.broadcasted_iota(jnp.int32, (DIL_BAND, DIL_BAND), 1)
    return c <= r


def _dil_fwd_all(qr, kr, h_b):
    t = qr.shape[0]
    nbands = t // DIL_BAND
    nbr = len(DIL_DILATIONS)
    hoff = DIL_HEADS

    def col(off):
        return pl.BlockSpec((t, DIL_HD), lambda h: (0, off + h), pipeline_mode=pl.Buffered(1))

    outb = pl.BlockSpec((t, DIL_HD), lambda h: (0, h))

    def body(q_ref, k_ref, v_ref, ob_ref, of_ref, lt_ref, qs, ks, vs, os_, ls_, *br):
        obr, lbr = br[:nbr], br[nbr:]
        for bi, d in enumerate(DIL_DILATIONS):
            n = t // d
            nb = n // DIL_BAND
            _gather_rows(qs, q_ref, t, d, BF16)
            _gather_rows(ks, k_ref, t, d, BF16)
            _gather_rows(vs, v_ref, t, d, BF16)
            s = jnp.where(_tri_mask(), _dot(qs[0:DIL_BAND, :], ks[0:DIL_BAND, :], NT), NEG)
            m = jnp.max(s, axis=-1, keepdims=True)
            pr = jnp.exp(s - m)
            den = jnp.sum(pr, axis=-1, keepdims=True)
            os_[0:DIL_BAND, :] = _dot(_bf(pr), vs[0:DIL_BAND, :], NN) / den
            ls_[0:DIL_BAND, :] = jnp.broadcast_to(m + jnp.log(den), (DIL_BAND, DIL_HD))

            def band(b, carry, nb=nb):
                st = pl.multiple_of((b - 1) * DIL_BAND, DIL_BAND)
                cur = pl.ds(st + DIL_BAND, DIL_BAND)
                both = pl.ds(st, 2 * DIL_BAND)
                not_first = ((b % nb) != 0).astype(jnp.int32)
                s = jnp.where(_band_masks(not_first), _dot(qs[cur, :], ks[both, :], NT), NEG)
                m = jnp.max(s, axis=-1, keepdims=True)
                pr = jnp.exp(s - m)
                den = jnp.sum(pr, axis=-1, keepdims=True)
                os_[cur, :] = _dot(_bf(pr), vs[both, :], NN) / den
                ls_[cur, :] = jnp.broadcast_to(m + jnp.log(den), (DIL_BAND, DIL_HD))
                return carry

            lax.fori_loop(1, nbands, band, 0, unroll=4)
            for r in range(d):
                dst = pl.ds(r, n, stride=d) if d > 1 else slice(None)
                obr[bi][dst, :] = os_[r * n:(r + 1) * n, :]
                lbr[bi][dst, :] = ls_[r * n:(r + 1) * n, :]
        rows = 512
        for c0 in range(0, t, rows):
            sl = slice(c0, c0 + rows)
            la, lb, lc = lbr[0][sl, :], lbr[1][sl, :], lbr[2][sl, :]
            m = jnp.maximum(jnp.maximum(la, lb), lc)
            ea, eb, ec = jnp.exp(la - m), jnp.exp(lb - m), jnp.exp(lc - m)
            den = ea + eb + ec
            o = (ea * obr[0][sl, :] + eb * obr[1][sl, :] + ec * obr[2][sl, :]) / den
            ob_ref[sl, :] = o.astype(BF16)
            of_ref[sl, :] = o
            lt_ref[sl, :] = m + jnp.log(den)

    w = DIL_HEADS * DIL_HD
    vm = lambda dt: pltpu.VMEM((t, DIL_HD), dt)
    return pl.pallas_call(
        body, name="dil_fwd", grid=(DIL_HEADS,), in_specs=[col(0), col(0), col(2 * hoff)],
        out_specs=[outb, outb, outb],
        out_shape=[jax.ShapeDtypeStruct((t, w), BF16), jax.ShapeDtypeStruct((t, w), F32),
                   jax.ShapeDtypeStruct((t, w), F32)],
        scratch_shapes=[vm(BF16)] * 3 + [vm(F32)] * 2 + [vm(F32)] * (2 * nbr),
        compiler_params=_params(("parallel",)),
    )(qr, kr, h_b)


def _dil_bwd_all(qr, kr, h_b, dmix, o_d, lse_tot):
    t = qr.shape[0]
    nbands = t // DIL_BAND
    hoff = DIL_HEADS

    def col(off):
        return pl.BlockSpec((t, DIL_HD), lambda h: (0, off + h), pipeline_mode=pl.Buffered(1))

    outb = pl.BlockSpec((t, DIL_HD), lambda h: (0, h))

    def body(q_ref, k_ref, v_ref, do_ref, o_ref, l_ref, dq_ref, dk_ref, dv_ref,
             qs, ks, vs, dos, lss, dds, dqs, acck, accv):
        for bi, d in enumerate(DIL_DILATIONS):
            n = t // d
            nb = n // DIL_BAND
            _gather_rows(qs, q_ref, t, d, BF16)
            _gather_rows(ks, k_ref, t, d, BF16)
            _gather_rows(vs, v_ref, t, d, BF16)
            _gather_rows(dos, do_ref, t, d, BF16)
            _gather_rows(lss, l_ref, t, d)
            for r in range(d):
                src = pl.ds(r, n, stride=d) if d > 1 else slice(None)
                prod = do_ref[src, :] * o_ref[src, :]
                dds[r * n:(r + 1) * n, :] = jnp.broadcast_to(jnp.sum(prod, axis=-1, keepdims=True), (n, DIL_HD))
            acck[...] = jnp.zeros_like(acck)
            accv[...] = jnp.zeros_like(accv)
            b0 = slice(0, DIL_BAND)
            s = jnp.where(_tri_mask(), _dot(qs[b0, :], ks[b0, :], NT), NEG)
            pr = jnp.exp(s - lss[b0, :])
            ds = _bf(pr * (_dot(dos[b0, :], vs[b0, :], NT) - dds[b0, :]))
            dqs[b0, :] = _dot(ds, ks[b0, :], NN)
            acck[DIL_BAND:2 * DIL_BAND, :] += _dot(ds, qs[b0, :], TN)
            accv[DIL_BAND:2 * DIL_BAND, :] += _dot(_bf(pr), dos[b0, :], TN)

            def band(b, carry, nb=nb):
                st = pl.multiple_of((b - 1) * DIL_BAND, DIL_BAND)
                cur = pl.ds(st + DIL_BAND, DIL_BAND)
                both = pl.ds(st, 2 * DIL_BAND)
                acc_rows = pl.ds(st + DIL_BAND, 2 * DIL_BAND)
                not_first = ((b % nb) != 0).astype(jnp.int32)
                qb, dob, lb, ddb = qs[cur, :], dos[cur, :], lss[cur, :], dds[cur, :]
                kcat, vcat = ks[both, :], vs[both, :]
                s = jnp.where(_band_masks(not_first), _dot(qb, kcat, NT), NEG)
                pr = jnp.exp(s - jnp.concatenate([lb, lb], axis=1))
                ds = _bf(pr * (_dot(dob, vcat, NT) - jnp.concatenate([ddb, ddb], axis=1)))
                dqs[cur, :] = _dot(ds, kcat, NN)
                acck[acc_rows, :] += _dot(ds, qb, TN)
                accv[acc_rows, :] += _dot(_bf(pr), dob, TN)
                return carry

            lax.fori_loop(1, nbands, band, 0, unroll=4)
            for r in range(d):
                lo = r * n
                if d == 1:
                    dq_ref[...] = dqs[...]
                    dk_ref[...] = acck[DIL_BAND:DIL_BAND + t, :]
                    dv_ref[...] = accv[DIL_BAND:DIL_BAND + t, :]
                else:
                    dst = pl.ds(r, n, stride=d)
                    dq_ref[dst, :] = dq_ref[dst, :] + dqs[lo:lo + n, :]
                    dk_ref[dst, :] = dk_ref[dst, :] + acck[DIL_BAND + lo:DIL_BAND + lo + n, :]
                    dv_ref[dst, :] = dv_ref[dst, :] + accv[DIL_BAND + lo:DIL_BAND + lo + n, :]

    w = DIL_HEADS * DIL_HD
    vm = lambda dt, extra=0: pltpu.VMEM((t + extra, DIL_HD), dt)
    return pl.pallas_call(
        body, name="dil_bwd", grid=(DIL_HEADS,),
        in_specs=[col(0), col(0), col(2 * hoff), col(hoff), col(0), col(0)], out_specs=[outb] * 3,
        out_shape=[jax.ShapeDtypeStruct((t, w), F32)] * 3,
        scratch_shapes=[vm(BF16)] * 4 + [vm(F32)] * 3 + [vm(F32, DIL_BAND)] * 2,
        compiler_params=_params(("parallel",)),
    )(qr, kr, h_b, dmix, o_d, lse_tot)


def _ca_fwd(q, memkv, tq=512):
    t, d = q.shape
    m = memkv.shape[0]
    scale = CA_HD ** -0.5

    def body(q_ref, k_ref, v_ref, o_ref, ot_ref):
        for h in range(CA_HEADS):
            hs = slice(h * CA_HD, (h + 1) * CA_HD)
            s = _dot(q_ref[:, hs], k_ref[:, hs], NT) * scale
            p = jnp.exp(s - jnp.max(s, axis=-1, keepdims=True))
            p = p / jnp.sum(p, axis=-1, keepdims=True)
            o = _dot(_bf(p), v_ref[:, hs], NN).astype(BF16)
            o_ref[:, hs] = o
            ot_ref[hs, :] = o.T

    return pl.pallas_call(
        body, name="ca_fwd", grid=(t // tq,),
        in_specs=[pl.BlockSpec((tq, d), lambda i: (i, 0)), pl.BlockSpec((m, d), lambda i: (0, 0)),
                  pl.BlockSpec((m, d), lambda i: (0, 1))],
        out_specs=[pl.BlockSpec((tq, d), lambda i: (i, 0)), pl.BlockSpec((d, tq), lambda i: (0, i))],
        out_shape=[jax.ShapeDtypeStruct((t, d), BF16), jax.ShapeDtypeStruct((d, t), BF16)],
        compiler_params=_params(("parallel",)),
    )(q, memkv, memkv)


def _ca_bwd(q, memkv, do, tq=512):
    t, d = q.shape
    m = memkv.shape[0]
    scale = CA_HD ** -0.5

    def body(q_ref, k_ref, v_ref, do_ref, dq_ref, dkv_ref):
        i = pl.program_id(0)

        @pl.when(i == 0)
        def _():
            dkv_ref[...] = jnp.zeros_like(dkv_ref)

        for h in range(CA_HEADS):
            hs = slice(h * CA_HD, (h + 1) * CA_HD)
            q_h, k_h, v_h, do_h = q_ref[:, hs], k_ref[:, hs], v_ref[:, hs], do_ref[:, hs]
            s = _dot(q_h, k_h, NT) * scale
            p = jnp.exp(s - jnp.max(s, axis=-1, keepdims=True))
            p = p / jnp.sum(p, axis=-1, keepdims=True)
            dp = _dot(do_h, v_h, NT)
            ds = _bf(p * (dp - jnp.sum(p * dp, axis=-1, keepdims=True)) * scale)
            dq_ref[:, hs] = _dot(ds, k_h, NN).astype(BF16)
            dkv_ref[:, hs] += _dot(ds, q_h, TN)
            dkv_ref[:, d + h * CA_HD: d + (h + 1) * CA_HD] += _dot(_bf(p), do_h, TN)

    return pl.pallas_call(
        body, name="ca_bwd", grid=(t // tq,),
        in_specs=[pl.BlockSpec((tq, d), lambda i: (i, 0)), pl.BlockSpec((m, d), lambda i: (0, 0)),
                  pl.BlockSpec((m, d), lambda i: (0, 1)), pl.BlockSpec((tq, d), lambda i: (i, 0))],
        out_specs=[pl.BlockSpec((tq, d), lambda i: (i, 0)), pl.BlockSpec((m, 2 * d), lambda i: (0, 0))],
        out_shape=[jax.ShapeDtypeStruct((t, d), BF16), jax.ShapeDtypeStruct((m, 2 * d), F32)],
        compiler_params=_params(("arbitrary",)),
    )(q, memkv, memkv, do)


STRIP = 256


def _shift_down(u, n, row):
    return jnp.where(row >= n, pltpu.roll(u, n, 0), 0.0)


def _shift_up(u, n, row):
    t = u.shape[0]
    return jnp.where(row < t - n, pltpu.roll(u, t - n, 0), 0.0)


def _conv(u, cw_ref, row):
    return ((cw_ref[3:4, :] + cw_ref[0:1, :] * _shift_down(u, 2, row)) + cw_ref[1:2, :] * _shift_down(u, 1, row)) \
        + cw_ref[2:3, :] * u


def _swiglu_fwd(u0, cw):
    t, w = u0.shape[0], u0.shape[1] // 2
    ns = w // STRIP
    col = pl.BlockSpec((t, STRIP), lambda j: (0, j))
    col_up = pl.BlockSpec((t, STRIP), lambda j: (0, ns + j))
    cws = pl.BlockSpec((SUBLANES, STRIP), lambda j: (0, j))
    cws_up = pl.BlockSpec((SUBLANES, STRIP), lambda j: (0, ns + j))

    def body(g_ref, u_ref, cg_ref, cu_ref, a_ref, at_ref):
        row = lax.broadcasted_iota(jnp.int32, (t, STRIP), 0)
        gate = _conv(g_ref[...].astype(F32), cg_ref, row)
        up = _conv(u_ref[...].astype(F32), cu_ref, row)
        act = (gate * _sigmoid(gate) * up).astype(BF16)
        a_ref[...] = act
        at_ref[...] = act.T

    return pl.pallas_call(
        body, name="swiglu_fwd", grid=(ns,), in_specs=[col, col_up, cws, cws_up],
        out_specs=[col, pl.BlockSpec((STRIP, t), lambda j: (j, 0))],
        out_shape=[jax.ShapeDtypeStruct((t, w), BF16), jax.ShapeDtypeStruct((w, t), BF16)],
        compiler_params=_params(("parallel",)),
    )(u0, u0, cw, cw)


def _swiglu_bwd(u0, cw, da):
    t, w = u0.shape[0], u0.shape[1] // 2
    ns = w // STRIP
    col = pl.BlockSpec((t, STRIP), lambda j: (0, j))
    col_up = pl.BlockSpec((t, STRIP), lambda j: (0, ns + j))
    cws = pl.BlockSpec((SUBLANES, STRIP), lambda j: (0, j))
    cws_up = pl.BlockSpec((SUBLANES, STRIP), lambda j: (0, ns + j))

    def conv_bwd(du, u0, cw_ref, row, du0_ref, du0t_ref, dcw_ref):
        du0 = (cw_ref[2:3, :] * du + cw_ref[1:2, :] * _shift_up(du, 1, row)) + cw_ref[0:1, :] * _shift_up(du, 2, row)
        du0 = du0.astype(BF16)
        du0_ref[...] = du0
        du0t_ref[...] = du0.T
        dcw_ref[0:1, :] = jnp.sum(du * _shift_down(u0, 2, row), axis=0, keepdims=True)
        dcw_ref[1:2, :] = jnp.sum(du * _shift_down(u0, 1, row), axis=0, keepdims=True)
        dcw_ref[2:3, :] = jnp.sum(du * u0, axis=0, keepdims=True)
        dcw_ref[3:4, :] = jnp.sum(du, axis=0, keepdims=True)
        dcw_ref[4:8, :] = jnp.zeros((4, STRIP), F32)

    def body(g_ref, u_ref, cg_ref, cu_ref, da_ref, dg0_ref, du0_ref, dut_ref, dcg_ref, dcu_ref):
        row = lax.broadcasted_iota(jnp.int32, (t, STRIP), 0)
        g0, up0 = g_ref[...].astype(F32), u_ref[...].astype(F32)
        gate = _conv(g0, cg_ref, row)
        up = _conv(up0, cu_ref, row)
        sg = _sigmoid(gate)
        da = da_ref[...].astype(F32)
        dgate = da * up * (sg * (1.0 + gate * (1.0 - sg)))
        dup = da * (gate * sg)
        conv_bwd(dgate, g0, cg_ref, row, dg0_ref, dut_ref.at[0], dcg_ref)
        conv_bwd(dup, up0, cu_ref, row, du0_ref, dut_ref.at[1], dcu_ref)

    return pl.pallas_call(
        body, name="swiglu_bwd", grid=(ns,), in_specs=[col, col_up, cws, cws_up, col],
        out_specs=[col, col, pl.BlockSpec((2, STRIP, t), lambda j: (0, j, 0)), cws, cws],
        out_shape=[jax.ShapeDtypeStruct((t, w), BF16), jax.ShapeDtypeStruct((t, w), BF16),
                   jax.ShapeDtypeStruct((2, w, t), BF16),
                   jax.ShapeDtypeStruct((SUBLANES, w), F32), jax.ShapeDtypeStruct((SUBLANES, w), F32)],
        compiler_params=_params(("parallel",)),
    )(u0, u0, cw, cw, da)


def _ffn_win_grad(dut, x2b, tn=512):
    t, d = x2b.shape
    sp, sw = FF_SLAB_P, FF_SLAB

    def body(a_ref, b_ref, o_ref, ob_ref):
        res = _dot(a_ref[...], b_ref[...], NN)
        o_ref[...] = res[:sw, :]
        ob_ref[...] = res[:sw, :].astype(BF16)

    o_spec = pl.BlockSpec((None, sw, tn), lambda j, n: (j, 0, n))
    return pl.pallas_call(
        body, name="mm_g_ffn_in", grid=(8, d // tn),
        in_specs=[pl.BlockSpec((None, sp, t), lambda j, n: (j // 4, j % 4, 0)),
                  pl.BlockSpec((t, tn), lambda j, n: (0, n))],
        out_specs=[o_spec, o_spec],
        out_shape=[jax.ShapeDtypeStruct((8, sw, d), F32), jax.ShapeDtypeStruct((8, sw, d), BF16)],
        compiler_params=_params(("parallel", "parallel")),
    )(dut, x2b)


def _tile2d(r, c, limit=1 << 20):
    tr, tc = r, c
    while tr * tc * 4 > limit:
        if tr % (2 * SUBLANES) == 0:
            tr //= 2
        elif tc % (2 * LANES) == 0:
            tc //= 2
        else:
            break
    return tr, tc


def _adamw_math(w, m, v, g):
    c1 = 1.0 - ADAM_B1 ** ADAM_STEP
    c2 = 1.0 - ADAM_B2 ** ADAM_STEP
    mm = ADAM_B1 * m + (1.0 - ADAM_B1) * g
    vv = ADAM_B2 * v + (1.0 - ADAM_B2) * (g * g)
    delta = -ADAM_LR * ((mm / c1) / (jnp.sqrt(vv / c2) + ADAM_EPS) + ADAM_WD * w)
    return delta, mm, vv


def _adamw(w, m, v, g, name):
    r, c = w.shape
    blk = pl.BlockSpec((r, c), lambda i: (0, 0))

    def body(w_ref, m_ref, v_ref, gi_ref, g_ref, d_ref, nm_ref, nv_ref):
        g = gi_ref[...]
        d_ref[...], nm_ref[...], nv_ref[...] = _adamw_math(w_ref[...], m_ref[...], v_ref[...], g)
        g_ref[...] = g

    return pl.pallas_call(body, name=name, grid=(1,), in_specs=[blk] * 4, out_specs=[blk] * 4,
                          out_shape=[jax.ShapeDtypeStruct((r, c), F32)] * 4,
                          compiler_params=_params(("arbitrary",)))(w, m, v, g)


def _pair_add(gs, ra, core, name):
    _, _, r, c = gs.shape
    tr, tc = _tile2d(r, c)
    blk = pl.BlockSpec((None, tr, tc), lambda k, i, j, s: (k, i, j))

    def body(s_ref, g_ref, r_ref, o_ref, ob_ref):
        p = g_ref[...] + r_ref[...]
        o_ref[...] = p
        ob_ref[...] = p.astype(BF16)

    gspec = pltpu.PrefetchScalarGridSpec(
        num_scalar_prefetch=1, grid=(4, r // tr, c // tc),
        in_specs=[pl.BlockSpec((None, None, tr, tc), lambda k, i, j, s: (k, s[0], i, j)), blk], out_specs=[blk, blk])
    return pl.pallas_call(body, name=name, grid_spec=gspec,
                          out_shape=[jax.ShapeDtypeStruct((4, r, c), F32), jax.ShapeDtypeStruct((4, r, c), BF16)],
                          compiler_params=_params(("parallel", "parallel", "parallel")))(core, gs, ra)


def _small_reduce(gathered):
    nd, r, n = gathered.shape
    tn = 2048 if n % 2048 == 0 else n
    def body(g_ref, s_ref, t_ref):
        s = g_ref[0]
        for dv in range(1, nd):
            s = s + g_ref[dv]
        s_ref[...] = s
        t_ref[...] = jnp.broadcast_to(jnp.sum(s, axis=0, keepdims=True), (r, tn))

    return pl.pallas_call(
        body, name="small_reduce", grid=(n // tn,),
        in_specs=[pl.BlockSpec((nd, r, tn), lambda j: (0, 0, j))],
        out_specs=[pl.BlockSpec((r, tn), lambda j: (0, j))] * 2,
        out_shape=[jax.ShapeDtypeStruct((r, n), F32)] * 2, compiler_params=_params(("parallel",)),
    )(gathered)


HBM = pl.BlockSpec(memory_space=pltpu.HBM)


def _all_gather(arrs, name):
    n = len(arrs)

    def body(*refs):
        ins, outs = refs[:n], refs[n:2 * n]
        send, recv, lsem = refs[2 * n:]
        x, y, c = lax.axis_index("x"), lax.axis_index("y"), lax.axis_index("c")
        me, sib = (x, y, c), (x, y, 1 - c)
        chips = [(1 - x, y), (x, 1 - y), (1 - x, 1 - y)]

        def slot(w, p):
            return outs[w].at[4 * p[0] + 2 * p[1] + p[2]]

        def cp(w, k, block, to, src=None):
            return pltpu.make_async_remote_copy(
                src_ref=slot(w, block) if src is None else src, dst_ref=slot(w, block),
                send_sem=send.at[w * 7 + k], recv_sem=recv.at[w * 7 + k], device_id=to, device_id_type=MESH)

        mine = [pltpu.make_async_copy(ins[w], slot(w, me), lsem.at[w]) for w in range(n)]
        for m in mine:
            m.start()
        first = []
        for w in range(n):
            first.append(cp(w, 0, me, sib, src=ins[w]))
            first += [cp(w, 1 + j, me, (*chip, c), src=ins[w]) for j, chip in enumerate(chips)]
        for f in first:
            f.start()
        passed = []
        for j, chip in enumerate(chips):
            for w in range(n):
                cp(w, 1 + j, (*chip, c), me).wait_recv()
                fwd = cp(w, 4 + j, (*chip, c), sib)
                fwd.start()
                passed.append(fwd)
        for w in range(n):
            cp(w, 0, sib, me).wait_recv()
            for j, chip in enumerate(chips):
                cp(w, 4 + j, (*chip, 1 - c), me).wait_recv()
        for f in first + passed:
            f.wait_send()
        for m in mine:
            m.wait()

    return pl.pallas_call(
        body, name=name, in_specs=[HBM] * n, out_specs=[HBM] * n,
        out_shape=[jax.ShapeDtypeStruct((8,) + a.shape, a.dtype) for a in arrs],
        scratch_shapes=[pltpu.SemaphoreType.DMA((7 * n,)), pltpu.SemaphoreType.DMA((7 * n,)),
                        pltpu.SemaphoreType.DMA((n,))],
    )(*arrs)


def _sibling_exchange(arrs, name):
    n = len(arrs)

    def body(*refs):
        ins, outs = refs[:n], refs[n:2 * n]
        send, recv = refs[2 * n:]
        x, y, c = lax.axis_index("x"), lax.axis_index("y"), lax.axis_index("c")
        copies = [pltpu.make_async_remote_copy(
            src_ref=ins[w].at[:, 1 - c], dst_ref=outs[w], send_sem=send.at[w], recv_sem=recv.at[w],
            device_id=(x, y, 1 - c), device_id_type=MESH) for w in range(n)]
        for cpy in copies:
            cpy.start()
        for cpy in copies:
            cpy.wait()

    return pl.pallas_call(
        body, name=name, in_specs=[HBM] * n, out_specs=[HBM] * n,
        out_shape=[jax.ShapeDtypeStruct((a.shape[0],) + a.shape[2:], a.dtype) for a in arrs],
        scratch_shapes=[pltpu.SemaphoreType.DMA((n,)), pltpu.SemaphoreType.DMA((n,))],
    )(*arrs)


def _chip_exchange(arrs, name):
    n = len(arrs)

    def body(*refs):
        ins, outs = refs[:n], refs[n:2 * n]
        send, recv = refs[2 * n:]
        x, y, c = lax.axis_index("x"), lax.axis_index("y"), lax.axis_index("c")
        chips = [(1 - x, y), (x, 1 - y), (1 - x, 1 - y)]
        copies = []
        for w in range(n):
            for j, (cx, cy) in enumerate(chips):
                copies.append(pltpu.make_async_remote_copy(
                    src_ref=ins[w].at[2 * cx + cy], dst_ref=outs[w].at[j], send_sem=send.at[3 * w + j],
                    recv_sem=recv.at[3 * w + j], device_id=(cx, cy, c), device_id_type=MESH))
        for cpy in copies:
            cpy.start()
        for cpy in copies:
            cpy.wait()

    return pl.pallas_call(
        body, name=name, in_specs=[HBM] * n, out_specs=[HBM] * n,
        out_shape=[jax.ShapeDtypeStruct((3,) + a.shape[1:], a.dtype) for a in arrs],
        scratch_shapes=[pltpu.SemaphoreType.DMA((3 * n,)), pltpu.SemaphoreType.DMA((3 * n,))],
    )(*arrs)


SEM = pl.BlockSpec(memory_space=pltpu.SEMAPHORE)
ANY = pl.BlockSpec(memory_space=pl.ANY)
EFFECT = pltpu.SideEffectType.DATAFLOW_SIDE_EFFECTING
N_PEERS = 7


def _peers(x, y, c):
    return [((1 - x) if k & 4 else x, (1 - y) if k & 2 else y, (1 - c) if k & 1 else c) for k in range(1, 8)]


def _spread_copies(src_refs, land_refs, send, recv, gather):
    x, y, c = lax.axis_index("x"), lax.axis_index("y"), lax.axis_index("c")
    me = 4 * x + 2 * y + c
    copies = []
    for w in range(len(src_refs)):
        for k, (px, py, pc) in enumerate(_peers(x, y, c)):
            p = 4 * px + 2 * py + pc
            copies.append((pltpu.make_async_remote_copy(
                src_ref=src_refs[w] if gather else src_refs[w].at[p], dst_ref=land_refs[w].at[me],
                send_sem=send[w].at[k], recv_sem=recv[w].at[k], device_id=(px, py, pc), device_id_type=MESH),
                pltpu.make_async_remote_copy(
                src_ref=src_refs[w] if gather else src_refs[w].at[p], dst_ref=land_refs[w].at[p],
                send_sem=send[w].at[k], recv_sem=recv[w].at[k], device_id=(px, py, pc), device_id_type=MESH)))
    return copies


def _hbm(a):
    return pltpu.with_memory_space_constraint(a, pltpu.HBM)


def _spread_start(srcs, lands, after, gather, name):
    n = len(srcs)

    def body(*refs):
        src_refs, land_refs = refs[:n], refs[n:2 * n]
        outs = refs[2 * n + 1:]
        send, recv, token = outs[:n], outs[n:2 * n], outs[4 * n]
        for start, _ in _spread_copies(src_refs, land_refs, send, recv, gather):
            start.start()
        token[...] = jnp.zeros_like(token)

    res = pl.pallas_call(
        body, name=name,
        out_shape=tuple([pltpu.SemaphoreType.DMA((N_PEERS,))] * (2 * n)
                        + [pltpu.HBM(a.shape, a.dtype) for a in srcs] + [pltpu.HBM(a.shape, a.dtype) for a in lands]
                        + [jax.ShapeDtypeStruct((SUBLANES, LANES), F32)]),
        in_specs=[HBM] * (2 * n) + [ANY],
        out_specs=tuple([SEM] * (2 * n) + [HBM] * (2 * n) + [pl.BlockSpec(memory_space=pltpu.VMEM)]),
        input_output_aliases={i: 2 * n + i for i in range(2 * n)},
        compiler_params=pltpu.CompilerParams(has_side_effects=EFFECT),
    )(*[_hbm(a) for a in srcs], *[_hbm(a) for a in lands], after)
    return res[:n], res[n:2 * n], res[2 * n:3 * n], res[3 * n:4 * n], res[4 * n]


def _spread_wait(send, recv, srcs, lands, after, gather, name):
    n = len(srcs)
    after = list(after) if isinstance(after, (list, tuple)) else [after]

    def body(*refs):
        src_refs, land_refs = refs[:n], refs[n:2 * n]
        send_refs, recv_refs = refs[2 * n:3 * n], refs[3 * n:4 * n]
        for _, arrive in _spread_copies(src_refs, land_refs, send_refs, recv_refs, gather):
            arrive.wait_send()
            arrive.wait_recv()

    res = pl.pallas_call(
        body, name=name,
        out_shape=tuple([pltpu.HBM(a.shape, a.dtype) for a in srcs] + [pltpu.HBM(a.shape, a.dtype) for a in lands]),
        in_specs=[HBM] * (2 * n) + [SEM] * (2 * n) + [ANY] * len(after),
        out_specs=tuple([HBM] * (2 * n)),
        input_output_aliases={i: i for i in range(2 * n)},
        compiler_params=pltpu.CompilerParams(has_side_effects=EFFECT),
    )(*srcs, *lands, *send, *recv, *after)
    return res[n:]


def _landing(shape, dtype, own, me):
    return lax.dynamic_update_index_in_dim(lax.empty((8,) + shape, dtype), own, me, 0)


def _pad_cols(a, to):
    return jnp.pad(a, ((0, 0), (0, to - a.shape[1])))


N_GLR = GLA_W + GLA_RANK
FF_SLAB = D_FF // 4
FF_SLAB_P = FFP // 4


TRANSPOSED = ("w_in", "ffn_w_in")


def _prepare_sub1(gath):
    w_in_t = gath["w_in"].reshape(-1, gath["w_in"].shape[2])
    w2 = jnp.concatenate([gath["gla_gate_w2"][s] for s in range(8)], axis=1)
    return {"w_a_t": jnp.pad(w_in_t[:N_GLR], ((0, HA_W - N_GLR), (0, 0))), "w_b_t": w_in_t[N_GLR:],
            "w2p": jnp.pad(w2, ((0, LANES - GLA_RANK), (0, 0)))}


def _prepare_ffn_in(g):
    f = jnp.pad(g, ((0, 0), (0, FF_SLAB_P - FF_SLAB), (0, 0)))
    return f.reshape(2 * FFP, f.shape[2])


def _prepare_ffn_out(g):
    return jnp.pad(g.reshape(4, FF_SLAB, -1), ((0, 0), (0, FF_SLAB_P - FF_SLAB), (0, 0))).reshape(FFP, -1)


def _prepare_conv(g, conv_b):
    padc = FF_SLAB_P - FF_SLAB
    cw = jnp.pad(g, ((0, 0), (0, 0), (0, padc)))
    cb = jnp.pad(conv_b.reshape(8, 1, FF_SLAB), ((0, 0), (0, 0), (0, padc)))
    rows = jnp.concatenate([cw, cb, jnp.zeros((8, 4, FF_SLAB_P), F32)], axis=1)
    return jnp.concatenate([rows[s] for s in range(8)], axis=1)


def _prepare_ffn(gath, conv_b):
    return {"w_ffn_t": _prepare_ffn_in(gath["ffn_w_in"]), "wo": _prepare_ffn_out(gath["ffn_w_out"]),
            "cw": _prepare_conv(gath["ffn_conv_w"], conv_b)}


def _unpad_ff(a):
    r = a.shape[0]
    return a.reshape(r, 4, FF_SLAB_P)[:, :, :FF_SLAB].reshape(r, D_FF)


def _grad_slabs(g):
    w_in_t = jnp.concatenate([g["w_a_t"][:N_GLR], g["w_b_t"]], axis=0)
    s = {"w_in": w_in_t.reshape(4, 2, w_in_t.shape[0] // 8, w_in_t.shape[1])}
    for n in ("w_out", "ca_wq", "ca_wo"):
        s[n] = _to_slabs(n, g[n])
    for n in ("ca_wkv", "ffn_w_in"):
        s[n] = g[n].reshape((4, 2) + g[n].shape[1:])
    wo = g["wo"].reshape(4, FF_SLAB_P, -1)[:, :FF_SLAB]
    s["ffn_w_out"] = wo.reshape(4, 2, FF_SLAB // 2, wo.shape[-1])
    return s


class _AtHand:
    def __init__(self, p):
        self.p = p
        self.token = None

    def sub2(self, after):
        return self.p

    def ffn_in(self, after):
        return self.p["w_ffn_t"]

    def ffn_out(self, after):
        return self.p["wo"]

    def grads_out(self, group, slabs):
        pass

    def small_out(self, parts):
        pass


def _local_step(x, mem, positions, target, p, small, stages=None):
    t, d = x.shape
    stages = _AtHand(p) if stages is None else stages
    w_a_t, w_b_t, w2p, cw = p["w_a_t"], p["w_b_t"], p["w2p"], p["cw"]
    tabs = _rope_tables(positions)
    xb = x.astype(BF16) if stages.token is None else (x + stages.token[0, 0]).astype(BF16)
    memb = mem.astype(BF16)

    h_a = _matmul(xb, w_a_t, "nt", F32, 1024, 640, d, "mm_h_a")
    h_b = _matmul(xb, w_b_t, "nt", F32, 1024, 1024, d, "mm_h_b")
    o_g, o_raw, s_before = _gla_fwd(h_a, w2p, small["gla_gate_b"], small["gla_norm_g"])
    qr, kr = _rope_fwd(h_b, tabs)
    o_d_b, o_d, lse_tot = _dil_fwd_all(qr, kr, h_b)
    mixin = jnp.concatenate([o_g, o_d_b], axis=1)
    wts = stages.sub2(mixin)
    mix = _matmul(mixin, wts["w_out"], "nn", F32, 1024, 1024, d, "mm_mix")
    x1, x1b, x1t = _ln_fwd(x, mix, small["ln1_g"], small["ln1_b"], "ln1_fwd")

    q_ca = _matmul(x1b, wts["ca_wq"], "nn", BF16, 1024, 1024, d, "mm_caq")
    kvw = wts["ca_wkv"].shape[2]
    memkv = _matmul(memb, wts["ca_wkv"], "nn", BF16, mem.shape[0], kvw, d, "mm_memkv", b_slabs=True)
    o_c, o_ct = _ca_fwd(q_ca, memkv)
    ca_out = _matmul(o_c, wts["ca_wo"], "nn", F32, 1024, 1024, d, "mm_cao")
    x2, x2b, x2t = _ln_fwd(x1, ca_out, small["ln2_g"], small["ln2_b"], "ln2_fwd")

    w_ffn_t = stages.ffn_in(x2b)
    u0 = _matmul(x2b, w_ffn_t, "nt", BF16, 1024, 1024, d, "mm_u0")
    act, act_t = _swiglu_fwd(u0, cw)
    wo = stages.ffn_out(act)
    ffn = _matmul(act, wo, "nn", F32, 512, 512, FFP, "mm_ffn")

    dp3, dp3b, dg3, db3, loss_part = _ln_bwd(x2, ffn, small["ln3_g"], small["ln3_b"], target, True, "ln3_bwd")
    g_wo, g_wo16 = _matmul(act_t, dp3b, "nn", F32, 512, 1024, t, "mm_g_wo", also_bf16=True)
    dact = _matmul(dp3b, wo, "nt", BF16, 1024, 512, d, "mm_dact")
    dug, duu, du_t, dcwg, dcwu = _swiglu_bwd(u0, cw, dact)
    g_ffn_in, g_ffn_in16 = _ffn_win_grad(du_t, x2b)

    def wo_slabs(a):
        a = a.reshape(4, FF_SLAB_P, -1)[:, :FF_SLAB]
        return a.reshape(8, FF_SLAB // 2, a.shape[-1])

    sent = stages.grads_out("ffn", {"ffn_w_out": (wo_slabs(g_wo), wo_slabs(g_wo16)), "ffn_w_in": (g_ffn_in, g_ffn_in16)})
    dx2 = _matmul(dug, w_ffn_t, "nn", F32, 512, 1024, FFP, "mm_dx2_g", resid=dp3, resid_scale=ALPHA, dep=sent)
    dx2 = _matmul(duu, w_ffn_t, "nn", F32, 512, 1024, FFP, "mm_dx2_u", resid=dx2, b_k_off=1)

    dp2, dp2b, dg2, db2 = _ln_bwd(x1, ca_out, small["ln2_g"], small["ln2_b"], dx2, False, "ln2_bwd")
    g_cao, g_cao16 = _matmul(o_ct, dp2b, "nn", F32, 512, 1024, t, "mm_g_cao", also_bf16=True)
    do_c = _matmul(dp2b, wts["ca_wo"], "nt", BF16, 1024, 1024, d, "mm_do_c")
    dq_ca, dmemkv = _ca_bwd(q_ca, memkv, do_c)
    g_caq, g_caq16 = _matmul(x1t, dq_ca, "nn", F32, 512, 1024, t, "mm_g_caq", also_bf16=True)
    g_cakv, g_cakv16 = _matmul(memb, dmemkv.astype(BF16), "tn", F32, 512, kvw, mem.shape[0], "mm_g_cakv",
                               out_slabs=True, also_bf16=True)
    dx1 = _matmul(dq_ca, wts["ca_wq"], "nt", F32, 1024, 1024, d, "mm_dx1", resid=dp2, resid_scale=ALPHA)

    dp1, dp1b, dg1, db1 = _ln_bwd(x, mix, small["ln1_g"], small["ln1_b"], dx1, False, "ln1_bwd")
    g_wout, g_wout16 = _matmul(mixin, dp1b, "tn", F32, 512, 1024, t, "mm_g_wout", also_bf16=True)

    def row_slabs(a):
        return a.reshape(8, a.shape[0] // 8, a.shape[1])

    sent = stages.grads_out("attn", {"ca_wo": (row_slabs(g_cao), row_slabs(g_cao16)),
                                     "ca_wq": (row_slabs(g_caq), row_slabs(g_caq16)), "ca_wkv": (g_cakv, g_cakv16),
                                     "w_out": (row_slabs(g_wout), row_slabs(g_wout16))})
    dmix = _matmul(dp1b, wts["w_out"], "nt", F32, 1024, 1024, d, "mm_dmix", dep=sent)
    dh_a, dw2, dgate_b, dnorm_g = _gla_bwd(h_a, w2p, small["gla_gate_b"], small["gla_norm_g"], o_raw, s_before, dmix)
    small_parts = {
        "gla_gate_b": dgate_b, "gla_norm_g": dnorm_g, "ln1_g": dg1, "ln1_b": db1, "ln2_g": dg2, "ln2_b": db2,
        "ln3_g": dg3, "ln3_b": db3,
        "conv": jnp.concatenate([_unpad_ff(dcwg), _unpad_ff(dcwu)], axis=1),
        "gla_gate_w2": dw2[:GLA_RANK],
    }
    sent = stages.small_out(small_parts)
    dq_d, dk_d, dv_d = _dil_bwd_all(qr, kr, h_b, dmix, o_d, lse_tot)
    dh_b = _dil_dh(dq_d, dk_d, dv_d, tabs)
    g_wa_t, g_wa16 = _matmul(dh_a, xb, "tn", F32, 640, 1024, t, "mm_g_wa", also_bf16=True, dep=sent)
    g_wb_t, g_wb16 = _matmul(dh_b, xb, "tn", F32, 512, 1024, t, "mm_g_wb", also_bf16=True)

    def w_in_slabs(a, b):
        full = jnp.concatenate([a[:N_GLR], b], axis=0)
        return full.reshape(8, full.shape[0] // 8, full.shape[1])

    sent = stages.grads_out("w_in", {"w_in": (w_in_slabs(g_wa_t, g_wb_t), w_in_slabs(g_wa16, g_wb16))})
    dx = _matmul(dh_a, w_a_t, "nn", F32, 512, 1024, HA_W, "mm_dx_a", resid=dp1, resid_scale=ALPHA, dep=sent)
    dx = _matmul(dh_b, w_b_t, "nn", F32, 512, 1024, HB_W, "mm_dx_b", resid=dx)

    grads = {"w_a_t": g_wa_t, "w_b_t": g_wb_t, "w_out": g_wout, "ca_wq": g_caq, "ca_wkv": g_cakv, "ca_wo": g_cao,
             "ffn_w_in": g_ffn_in, "wo": g_wo}
    return loss_part, dx, grads, small_parts


BIG = ("w_in", "w_out", "ca_wq", "ca_wkv", "ca_wo", "ffn_w_in", "ffn_w_out")
COL_SHARDED = ("w_in", "ca_wkv", "ffn_w_in")
SMALL_ORDER = ("gla_gate_b", "gla_norm_g", "ln1_g", "ln1_b", "ln2_g", "ln2_b", "ln3_g", "ln3_b")


def _gathered_full(name, g):
    if name in COL_SHARDED:
        return g.transpose(1, 0, 2).reshape(g.shape[1], 8 * g.shape[2])
    return g.reshape(8 * g.shape[1], g.shape[2])


def _to_slabs(name, full):
    if name in COL_SHARDED:
        r, cc = full.shape
        s = full.reshape(r, 8, cc // 8).transpose(1, 0, 2)
    else:
        rr, c = full.shape
        s = full.reshape(8, rr // 8, c)
    return s.reshape((4, 2) + s.shape[1:])


def kernel(x, mem, positions, w_in, gla_gate_w2, gla_gate_b, gla_norm_g, w_out, ln1_g, ln1_b, ca_wq, ca_wkv, ca_wo, ln2_g, ln2_b, ffn_w_in, ffn_conv_w, ffn_conv_b, ffn_w_out, ln3_g, ln3_b, loss_target, m_w_in, m_gla_gate_w2, m_gla_gate_b, m_gla_norm_g, m_w_out, m_ln1_g, m_ln1_b, m_ca_wq, m_ca_wkv, m_ca_wo, m_ln2_g, m_ln2_b, m_ffn_w_in, m_ffn_conv_w, m_ffn_conv_b, m_ffn_w_out, m_ln3_g, m_ln3_b, v_w_in, v_gla_gate_w2, v_gla_gate_b, v_gla_norm_g, v_w_out, v_ln1_g, v_ln1_b, v_ca_wq, v_ca_wkv, v_ca_wo, v_ln2_g, v_ln2_b, v_ffn_w_in, v_ffn_conv_w, v_ffn_conv_b, v_ffn_w_out, v_ln3_g, v_ln3_b):
    weights = dict(w_in=w_in, gla_gate_w2=gla_gate_w2, gla_gate_b=gla_gate_b, gla_norm_g=gla_norm_g, w_out=w_out,
                   ln1_g=ln1_g, ln1_b=ln1_b, ca_wq=ca_wq, ca_wkv=ca_wkv, ca_wo=ca_wo, ln2_g=ln2_g, ln2_b=ln2_b,
                   ffn_w_in=ffn_w_in, ffn_conv_w=ffn_conv_w, ffn_conv_b=ffn_conv_b, ffn_w_out=ffn_w_out,
                   ln3_g=ln3_g, ln3_b=ln3_b)
    moms = dict(w_in=(m_w_in, v_w_in), gla_gate_w2=(m_gla_gate_w2, v_gla_gate_w2), gla_gate_b=(m_gla_gate_b, v_gla_gate_b),
                gla_norm_g=(m_gla_norm_g, v_gla_norm_g), w_out=(m_w_out, v_w_out), ln1_g=(m_ln1_g, v_ln1_g),
                ln1_b=(m_ln1_b, v_ln1_b), ca_wq=(m_ca_wq, v_ca_wq), ca_wkv=(m_ca_wkv, v_ca_wkv), ca_wo=(m_ca_wo, v_ca_wo),
                ln2_g=(m_ln2_g, v_ln2_g), ln2_b=(m_ln2_b, v_ln2_b), ffn_w_in=(m_ffn_w_in, v_ffn_w_in),
                ffn_conv_w=(m_ffn_conv_w, v_ffn_conv_w), ffn_conv_b=(m_ffn_conv_b, v_ffn_conv_b),
                ffn_w_out=(m_ffn_w_out, v_ffn_w_out), ln3_g=(m_ln3_g, v_ln3_g), ln3_b=(m_ln3_b, v_ln3_b))
    order = list(weights)
    xi, yi, ci = lax.axis_index("x"), lax.axis_index("y"), lax.axis_index("c")
    me = 4 * xi + 2 * yi + ci

    def travel(n, a):
        return jnp.swapaxes(a, 1, 2) if n in TRANSPOSED else a

    shard = {n: travel(n, weights[n]).astype(BF16)[0] for n in BIG}
    first = _all_gather([shard["w_in"], gla_gate_w2.astype(BF16)[0], ffn_conv_w[0]], "ag_first")
    p = _prepare_sub1({"w_in": first[0], "gla_gate_w2": first[1]})
    p["cw"] = _prepare_conv(first[2], ffn_conv_b)
    later = ("w_out", "ca_wq", "ca_wkv", "ca_wo", "ffn_w_in", "ffn_w_out")
    srcs = [shard[n] for n in later]
    lands = [_landing(shard[n].shape, BF16, shard[n], me) for n in later]
    send, recv, srcs, lands, token = _spread_start(srcs, lands, first[0], True, "ag_rest_start")

    class stages:
        pass

    stages.token = token

    def arrived(lo, hi, after, name):
        return _spread_wait(send[lo:hi], recv[lo:hi], srcs[lo:hi], lands[lo:hi], after, True, name)

    def sub2(after):
        g = dict(zip(later[:4], arrived(0, 4, after, "ag_wait_attn")))
        w = {n: _gathered_full(n, g[n]) for n in ("w_out", "ca_wq", "ca_wo")}
        w["ca_wkv"] = g["ca_wkv"]
        return w

    stages.sub2 = sub2
    stages.ffn_in = lambda after: _prepare_ffn_in(arrived(4, 5, after, "ag_wait_ffn_in")[0])
    stages.ffn_out = lambda after: _prepare_ffn_out(arrived(5, 6, after, "ag_wait_ffn_out")[0])
    sent = {}

    def grads_out(group, slabs):
        names = list(slabs)
        srcs16 = [slabs[n][1] for n in names]
        zones = [_landing(s.shape[1:], BF16, jnp.zeros(s.shape[1:], BF16), me) for s in srcs16]
        snd, rcv, s_thru, l_thru, tok = _spread_start(srcs16, zones, slabs[names[0]][0], False, f"rs_{group}_start")
        sent[group] = (names, [slabs[n][0] for n in names], (snd, rcv, s_thru, l_thru))
        return tok

    stages.grads_out = grads_out
    small_sent = []

    def small_out(parts):
        packed = jnp.concatenate([parts[n] for n in SMALL_ORDER] + [parts["conv"],
                                 parts["gla_gate_w2"].reshape(SUBLANES, -1)], axis=1)
        packed = jnp.pad(packed, ((0, 0), (0, (-packed.shape[1]) % 2048)))
        zone = _landing(packed.shape, F32, packed, me)
        snd, rcv, s_thru, l_thru, tok = _spread_start([packed], [zone], packed, True, "ag_small_start")
        small_sent.append((snd, rcv, s_thru, l_thru))
        return tok

    stages.small_out = small_out
    small = dict(gla_gate_b=gla_gate_b, gla_norm_g=gla_norm_g, ln1_g=ln1_g, ln1_b=ln1_b, ln2_g=ln2_g, ln2_b=ln2_b,
                 ln3_g=ln3_g, ln3_b=ln3_b)

    loss_part, dx, grads, small_parts = _local_step(x[0], mem[0], positions[0], loss_target[0], p, small, stages)
    loss = lax.psum(jnp.sum(loss_part), ("x", "y", "c"))

    out = {}
    (allp,) = _spread_wait(*small_sent[0], dx, True, "ag_small_wait")
    dev_sum, row_sum = _small_reduce(allp)

    me1 = me.reshape(1).astype(jnp.int32)

    def finish_group(group, after):
        names, own32, handles = sent[group]
        landed = _spread_wait(*handles, after, False, f"rs_{group}_wait")
        for n, own, land in zip(names, own32, landed):
            m_, v_ = moms[n]
            res4 = _adamw_direct(travel(n, weights[n]), travel(n, m_), travel(n, v_), own, land, me1, f"adamw_{n}")
            out[n] = [travel(n, a) for a in res4]

    finish_group("ffn", dx)
    finish_group("attn", dx)
    off = 0
    for n in SMALL_ORDER:
        width = weights[n].shape[1]
        g = row_sum[0:1, off:off + width]
        off += width
        m_, v_ = moms[n]
        out[n] = _adamw(weights[n], m_, v_, g, f"adamw_{n}")
    conv_g = dev_sum[:, off:off + 2 * D_FF]
    off += 2 * D_FF
    g_cb = conv_g[3:4]
    out["ffn_conv_b"] = _adamw(ffn_conv_b, m_ffn_conv_b, v_ffn_conv_b, g_cb, "adamw_ffn_conv_b")
    wsh = ffn_conv_w.shape[2]
    g_cw = lax.dynamic_slice_in_dim(conv_g[0:3], me * wsh, wsh, axis=1)
    out["ffn_conv_w"] = _adamw(ffn_conv_w[0], m_ffn_conv_w[0], v_ffn_conv_w[0], g_cw, "adamw_ffn_conv_w")
    w2_g = dev_sum[:, off:off + GLA_RANK * GLA_HEADS * GLA_DK // SUBLANES].reshape(GLA_RANK, GLA_HEADS * GLA_DK)
    wsh2 = gla_gate_w2.shape[2]
    g_w2 = lax.dynamic_slice_in_dim(w2_g, me * wsh2, wsh2, axis=1)
    out["gla_gate_w2"] = _adamw(gla_gate_w2[0], m_gla_gate_w2[0], v_gla_gate_w2[0], g_w2, "adamw_gla_gate_w2")
    finish_group("w_in", [o[1] for o in out.values()])

    def shaped(n, a):
        return a.reshape(weights[n].shape)

    res = [loss, dx[None]]
    for k in range(4):
        res += [shaped(n, out[n][k]) for n in order]
    return tuple(res)


def _adamw_direct(w, m, v, own, land, me, name):
    _, r, c = w.shape
    tr, tc = _tile2d(r, c)
    blk = pl.BlockSpec((None, tr, tc), lambda i, j, s: (0, i, j))
    mine = pl.BlockSpec((None, tr, tc), lambda i, j, s: (s[0], i, j))
    slots = [pl.BlockSpec((None, tr, tc), lambda i, j, s, k=k: (k, i, j)) for k in range(8)]

    def body(s_ref, w_ref, m_ref, v_ref, p_ref, *rest):
        slot_refs, (g_ref, d_ref, nm_ref, nv_ref) = rest[:8], rest[8:]
        g = p_ref[...]
        for sr in slot_refs:
            g = g + sr[...].astype(F32)
        d_ref[...], nm_ref[...], nv_ref[...] = _adamw_math(w_ref[...], m_ref[...], v_ref[...], g)
        g_ref[...] = g

    gs = pltpu.PrefetchScalarGridSpec(num_scalar_prefetch=1, grid=(r // tr, c // tc),
                                      in_specs=[blk, blk, blk, mine] + slots, out_specs=[blk] * 4)
    return pl.pallas_call(body, name=name, grid_spec=gs, out_shape=[jax.ShapeDtypeStruct((1, r, c), F32)] * 4,
                          compiler_params=_params(("parallel", "parallel")))(me, w, m, v, own, *([land] * 8))


def _adamw_big(w, m, v, p32, rc, chip, name):
    _, r, c = w.shape
    tr, tc = _tile2d(r, c)
    blk = pl.BlockSpec((None, tr, tc), lambda i, j, s: (0, i, j))
    own = pl.BlockSpec((None, tr, tc), lambda i, j, s: (s[0], i, j))
    others = [pl.BlockSpec((None, tr, tc), lambda i, j, s, k=k: (k, i, j)) for k in range(3)]

    def body(s_ref, w_ref, m_ref, v_ref, p_ref, r0_ref, r1_ref, r2_ref, g_ref, d_ref, nm_ref, nv_ref):
        g = ((p_ref[...] + r0_ref[...].astype(F32)) + r1_ref[...].astype(F32)) + r2_ref[...].astype(F32)
        d_ref[...], nm_ref[...], nv_ref[...] = _adamw_math(w_ref[...], m_ref[...], v_ref[...], g)
        g_ref[...] = g

    gs = pltpu.PrefetchScalarGridSpec(num_scalar_prefetch=1, grid=(r // tr, c // tc),
                                      in_specs=[blk, blk, blk, own] + others, out_specs=[blk] * 4)
    return pl.pallas_call(body, name=name, grid_spec=gs, out_shape=[jax.ShapeDtypeStruct((1, r, c), F32)] * 4,
                          compiler_params=_params(("parallel", "parallel")))(chip, w, m, v, p32, rc, rc, rc)
```

```python
import functools
import math

import jax
import jax.numpy as jnp
from jax import lax
from jax.experimental import pallas as pl
from jax.experimental.pallas import tpu as pltpu

F32 = jnp.float32
BF16 = jnp.bfloat16
MESH = pl.DeviceIdType.MESH

D_MODEL = 2048
LN_EPS = 1e-5
GLA_HEADS = 4
GLA_DV = 256
GLA_DK = 128
GLA_RANK = 16
GLA_TAU = 16.0
GLA_CHUNK = 64
DIL_HD = 128
DIL_HEADS = 8
DIL_BAND = 128
DIL_DILATIONS = (1, 4, 16)
ROPE_THETA = 500000.0
ROPE_DIMS = 32
CA_HEADS = 4
CA_HD = 512
D_FF = 5504
ALPHA = 2.0 ** 0.25
ADAM_LR = 0.001
ADAM_B1 = 0.9
ADAM_B2 = 0.999
ADAM_EPS = 1e-08
ADAM_WD = 0.01
ADAM_STEP = 10

LANES = 128
SUBLANES = 8
VMEM_LIMIT = 56 * 1024 * 1024

GLA_W = 2 * GLA_HEADS * GLA_DK + 2 * GLA_HEADS * GLA_DV
HA_W = GLA_W + LANES
HB_W = 3 * DIL_HEADS * DIL_HD
FFP = 5632
NEG = -1e30


def _params(sem):
    return pltpu.CompilerParams(dimension_semantics=sem, vmem_limit_bytes=VMEM_LIMIT)


def _sigmoid(x):
    return 1.0 / (1.0 + jnp.exp(-x))


def _dot(a, b, dn, precision=None):
    return lax.dot_general(a, b, (dn, ((), ())), preferred_element_type=F32, precision=precision)


NN = ((1,), (0,))
NT = ((1,), (1,))
TN = ((0,), (0,))


def _bf(v):
    return v if v.dtype == BF16 else v.astype(BF16)


def _matmul(a, b, kind, out_dtype, tm, tn, tk, name, resid=None, resid_scale=1.0, b_k_off=0, b_slabs=False,
            out_slabs=False, also_bf16=False, dep=None):
    if b_slabs:
        assert kind != "nt" and b.shape[2] == tn
        k2, n = b.shape[1], b.shape[0] * tn
    elif kind == "nt":
        n, k2 = b.shape
    else:
        k2, n = b.shape
    (k, m) = a.shape if kind == "tn" else a.shape[::-1]
    assert k2 >= k and (k2 == k or not b_slabs) and m % tm == 0 and n % tn == 0 and k % tk == 0, \
        (name, a.shape, b.shape, tm, tn, tk)
    nk = k // tk
    dn = {"nn": NN, "nt": NT, "tn": TN}[kind]
    a_spec = pl.BlockSpec((tk, tm), lambda i, j, kk: (kk, i)) if kind == "tn" else pl.BlockSpec((tm, tk), lambda i, j, kk: (i, kk))
    if b_slabs:
        b_spec = pl.BlockSpec((None, tk, tn), lambda i, j, kk: (j, kk, 0))
    elif kind == "nt":
        b_spec = pl.BlockSpec((tn, tk), lambda i, j, kk: (j, kk + b_k_off))
    else:
        b_spec = pl.BlockSpec((tk, tn), lambda i, j, kk: (kk + b_k_off, j))
    if out_slabs:
        o_spec = pl.BlockSpec((None, tm, tn), lambda i, j, kk: (j, i, 0))
        o_shape = (n // tn, m, tn)
    else:
        o_spec = pl.BlockSpec((tm, tn), lambda i, j, kk: (i, j))
        o_shape = (m, n)
    has_resid = resid is not None

    n_in = 2 + int(has_resid) + int(dep is not None)

    def body(*refs):
        a_ref, b_ref = refs[:2]
        r_ref = refs[2] if has_resid else None
        o_ref = refs[n_in]
        ob_ref = refs[n_in + 1] if also_bf16 else None
        part = _dot(_bf(a_ref[...]), _bf(b_ref[...]), dn)

        def finish(acc):
            if has_resid:
                acc = acc + resid_scale * r_ref[...].astype(F32)
            o_ref[...] = acc.astype(out_dtype)
            if also_bf16:
                ob_ref[...] = acc.astype(BF16)

        if nk == 1:
            finish(part)
        else:
            acc_ref = refs[-1]
            kk = pl.program_id(2)

            @pl.when(kk == 0)
            def _():
                acc_ref[...] = part

            @pl.when(kk > 0)
            def _():
                acc_ref[...] += part

            @pl.when(kk == nk - 1)
            def _():
                finish(acc_ref[...])

    in_specs = [a_spec, b_spec] + ([o_spec] if has_resid else [])
    args = (a, b) + ((resid,) if has_resid else ())
    if dep is not None:
        in_specs.append(pl.BlockSpec((SUBLANES, LANES), lambda i, j, kk: (0, 0)))
        args += (dep,)
    o_struct = jax.ShapeDtypeStruct(o_shape, out_dtype)
    return pl.pallas_call(
        body, name=name, out_shape=[o_struct, jax.ShapeDtypeStruct(o_shape, BF16)] if also_bf16 else o_struct,
        grid=(m // tm, n // tn, nk), in_specs=in_specs, out_specs=[o_spec, o_spec] if also_bf16 else o_spec,
        scratch_shapes=[pltpu.VMEM((tm, tn), F32)] if nk > 1 else [],
        compiler_params=_params(("parallel", "parallel", "arbitrary")),
    )(*args)


def _ln_core(xres, f):
    p = ALPHA * xres + f
    mu = jnp.mean(p, axis=-1, keepdims=True)
    xc = p - mu
    var = jnp.mean(xc * xc, axis=-1, keepdims=True)
    rstd = lax.rsqrt(var + LN_EPS)
    return xc * rstd, rstd


def _rows8(v):
    r, c = v.shape
    return jnp.sum(v.reshape(r // SUBLANES, SUBLANES, c), axis=0)


def _ln_fwd(xres, f, g, b, name, transposed, tr=256):
    t, d = xres.shape
    row = pl.BlockSpec((tr, d), lambda i: (i, 0))
    vec = pl.BlockSpec((1, d), lambda i: (0, 0))

    def body(x_ref, f_ref, g_ref, b_ref, y_ref, yb_ref, *yt_ref):
        xhat, _ = _ln_core(x_ref[...], f_ref[...])
        y = xhat * g_ref[...] + b_ref[...]
        y_ref[...] = y
        yb = y.astype(BF16)
        yb_ref[...] = yb
        if transposed:
            yt_ref[0][...] = yb.T

    out_specs = [row, row] + ([pl.BlockSpec((d, tr), lambda i: (0, i))] if transposed else [])
    out_shape = [jax.ShapeDtypeStruct((t, d), F32), jax.ShapeDtypeStruct((t, d), BF16)] \
        + ([jax.ShapeDtypeStruct((d, t), BF16)] if transposed else [])
    return pl.pallas_call(
        body, name=name, grid=(t // tr,), in_specs=[row, row, vec, vec], out_specs=out_specs, out_shape=out_shape,
        compiler_params=_params(("parallel",)),
    )(xres, f, g, b)


def _ln_bwd(xres, f, g, b, dy_or_target, loss_head, name, tr=256):
    t, d = xres.shape
    row = pl.BlockSpec((tr, d), lambda i: (i, 0))
    vec = pl.BlockSpec((1, d), lambda i: (0, 0))
    acc = pl.BlockSpec((SUBLANES, d), lambda i: (0, 0))
    lacc = pl.BlockSpec((SUBLANES, LANES), lambda i: (0, 0))

    def body(x_ref, f_ref, g_ref, b_ref, t_ref, dp_ref, dpb_ref, dg_ref, db_ref, *rest):
        i = pl.program_id(0)
        xhat, rstd = _ln_core(x_ref[...], f_ref[...])
        if loss_head:
            err = xhat * g_ref[...] + b_ref[...] - t_ref[...]
            dy = err * (1.0 / d)
            sq = err * err
            lanes = sq[:, :LANES]
            for kk in range(1, d // LANES):
                lanes = lanes + sq[:, kk * LANES:(kk + 1) * LANES]
            lpart = _rows8(lanes) * (0.5 / d)
        else:
            dy = t_ref[...]
        dxh = dy * g_ref[...]
        m1 = jnp.mean(dxh, axis=-1, keepdims=True)
        m2 = jnp.mean(dxh * xhat, axis=-1, keepdims=True)
        dp = rstd * (dxh - m1 - xhat * m2)
        dp_ref[...] = dp
        dpb_ref[...] = dp.astype(BF16)
        dgp = _rows8(dy * xhat)
        dbp = _rows8(dy)

        @pl.when(i == 0)
        def _():
            dg_ref[...] = dgp
            db_ref[...] = dbp
            if loss_head:
                rest[0][...] = lpart

        @pl.when(i > 0)
        def _():
            dg_ref[...] += dgp
            db_ref[...] += dbp
            if loss_head:
                rest[0][...] += lpart

    out_shape = [jax.ShapeDtypeStruct((t, d), F32), jax.ShapeDtypeStruct((t, d), BF16),
                 jax.ShapeDtypeStruct((SUBLANES, d), F32), jax.ShapeDtypeStruct((SUBLANES, d), F32)]
    out_specs = [row, row, acc, acc]
    if loss_head:
        out_shape.append(jax.ShapeDtypeStruct((SUBLANES, LANES), F32))
        out_specs.append(lacc)
    return pl.pallas_call(
        body, name=name, grid=(t // tr,), in_specs=[row, row, vec, vec, row], out_specs=out_specs,
        out_shape=out_shape, compiler_params=_params(("arbitrary",)),
    )(xres, f, g, b, dy_or_target)


def _gla_gates(glr, w2, gb):
    z = _dot(_bf(glr), w2, NN) + gb
    lg = (jnp.minimum(z, 0.0) - jnp.log(1.0 + jnp.exp(-jnp.abs(z)))) * (1.0 / GLA_TAU)
    c = z.shape[0]
    ri = lax.broadcasted_iota(jnp.int32, (c, c), 0)
    ci = lax.broadcasted_iota(jnp.int32, (c, c), 1)
    tri = (ci <= ri).astype(F32)
    bcum = _dot(tri, lg, NN, precision=lax.Precision.HIGHEST)
    blast = jnp.sum(lg, axis=0, keepdims=True)
    return z, bcum, blast, tri


def _gla_specs(t):
    c = GLA_CHUNK
    return c, t // c


def _gla_fwd(h_a, w2p, gate_b, norm_g):
    t = h_a.shape[0]
    c, n = _gla_specs(t)
    hk, hv = GLA_HEADS * GLA_DK, GLA_HEADS * GLA_DV
    scale = GLA_DK ** -0.5

    def body(q_ref, k_ref, v_ref, r_ref, glr_ref, w2_ref, gb_ref, ng_ref, og_ref, oraw_ref, sb_ref, st_ref):
        i = pl.program_id(0)

        @pl.when(i == 0)
        def _():
            st_ref[...] = jnp.zeros_like(st_ref)

        _, bcum, blast, _ = _gla_gates(glr_ref[...], w2_ref[...], gb_ref[...])
        ri = lax.broadcasted_iota(jnp.int32, (c, c), 0)
        ci = lax.broadcasted_iota(jnp.int32, (c, c), 1)
        causal = ci <= ri
        for h in range(GLA_HEADS):
            ks = slice(h * GLA_DK, (h + 1) * GLA_DK)
            vs = slice(h * GLA_DV, (h + 1) * GLA_DV)
            b_h, bl_h = bcum[:, ks], blast[:, ks]
            q_h, k_h = q_ref[:, ks], k_ref[:, ks]
            v_h = _bf(v_ref[:, vs])
            qi = _bf(q_h * scale * jnp.exp(b_h))
            ki = _bf(k_h * jnp.exp(-b_h))
            ke = _bf(k_h * jnp.exp(bl_h - b_h))
            st = st_ref[h]
            sb_ref[0, h] = st
            a = jnp.where(causal, _dot(qi, ki, NT), 0.0)
            o = _dot(_bf(a), v_h, NN) + _dot(qi, _bf(st), NT)
            st_ref[h] = st * jnp.exp(bl_h) + _dot(v_h, ke, TN)
            oraw_ref[:, vs] = o
            mu = jnp.mean(o, axis=-1, keepdims=True)
            oc = o - mu
            var = jnp.mean(oc * oc, axis=-1, keepdims=True)
            xh = oc * lax.rsqrt(var + LN_EPS)
            r_h = r_ref[:, vs]
            og_ref[:, vs] = (xh * ng_ref[:, vs] * (r_h * _sigmoid(r_h))).astype(BF16)

    return pl.pallas_call(
        body, name="gla_fwd", grid=(n,),
        in_specs=[pl.BlockSpec((c, hk), lambda i: (i, 0)), pl.BlockSpec((c, hk), lambda i: (i, 1)),
                  pl.BlockSpec((c, hv), lambda i: (i, 1)), pl.BlockSpec((c, hv), lambda i: (i, 2)),
                  pl.BlockSpec((c, LANES), lambda i: (i, GLA_W // LANES)),
                  pl.BlockSpec((LANES, hk), lambda i: (0, 0)), pl.BlockSpec((1, hk), lambda i: (0, 0)),
                  pl.BlockSpec((1, hv), lambda i: (0, 0))],
        out_specs=[pl.BlockSpec((c, hv), lambda i: (i, 0)), pl.BlockSpec((c, hv), lambda i: (i, 0)),
                   pl.BlockSpec((1, GLA_HEADS, GLA_DV, GLA_DK), lambda i: (i, 0, 0, 0))],
        out_shape=[jax.ShapeDtypeStruct((t, hv), BF16), jax.ShapeDtypeStruct((t, hv), F32),
                   jax.ShapeDtypeStruct((n, GLA_HEADS, GLA_DV, GLA_DK), F32)],
        scratch_shapes=[pltpu.VMEM((GLA_HEADS, GLA_DV, GLA_DK), F32)],
        compiler_params=_params(("arbitrary",)),
    )(h_a, h_a, h_a, h_a, h_a, w2p, gate_b, norm_g)


def _gla_bwd(h_a, w2p, gate_b, norm_g, o_raw, s_before, dmix):
    t = h_a.shape[0]
    c, n = _gla_specs(t)
    hk, hv = GLA_HEADS * GLA_DK, GLA_HEADS * GLA_DV
    scale = GLA_DK ** -0.5
    rev = lambda i: n - 1 - i

    def body(q_ref, k_ref, v_ref, r_ref, glr_ref, w2_ref, gb_ref, ng_ref, oraw_ref, sb_ref, do_ref,
             dh_ref, dw2_ref, dgb_ref, dng_ref, dst_ref):
        i = pl.program_id(0)

        @pl.when(i == 0)
        def _():
            dst_ref[...] = jnp.zeros_like(dst_ref)

        glr = glr_ref[...]
        z, bcum, blast, tri = _gla_gates(glr, w2_ref[...], gb_ref[...])
        ri = lax.broadcasted_iota(jnp.int32, (c, c), 0)
        ci = lax.broadcasted_iota(jnp.int32, (c, c), 1)
        causal = ci <= ri
        dlg_parts = []
        dng_parts = []
        for h in range(GLA_HEADS):
            ks = slice(h * GLA_DK, (h + 1) * GLA_DK)
            vs = slice(h * GLA_DV, (h + 1) * GLA_DV)
            o = oraw_ref[:, vs]
            mu = jnp.mean(o, axis=-1, keepdims=True)
            oc = o - mu
            var = jnp.mean(oc * oc, axis=-1, keepdims=True)
            rstd = lax.rsqrt(var + LN_EPS)
            xh = oc * rstd
            r_h = r_ref[:, vs]
            sg = _sigmoid(r_h)
            silu = r_h * sg
            dout = do_ref[:, vs]
            ng = ng_ref[:, vs]
            dng_parts.append(_rows8(dout * xh * silu))
            dr = dout * xh * ng * (sg * (1.0 + r_h * (1.0 - sg)))
            dxh = dout * ng * silu
            m1 = jnp.mean(dxh, axis=-1, keepdims=True)
            m2 = jnp.mean(dxh * xh, axis=-1, keepdims=True)
            do_raw = _bf(rstd * (dxh - m1 - xh * m2))
            b_h, bl_h = bcum[:, ks], blast[:, ks]
            q_h, k_h = q_ref[:, ks], k_ref[:, ks]
            v_h = _bf(v_ref[:, vs])
            eb, enb, eend = jnp.exp(b_h), jnp.exp(-b_h), jnp.exp(bl_h - b_h)
            decay = jnp.exp(bl_h)
            qi_f, ki_f, ke_f = q_h * scale * eb, k_h * enb, k_h * eend
            qi, ki, ke = _bf(qi_f), _bf(ki_f), _bf(ke_f)
            st = sb_ref[0, h]
            dst = dst_ref[h]
            dst_b = _bf(dst)
            a = _bf(jnp.where(causal, _dot(qi, ki, NT), 0.0))
            da = _bf(jnp.where(causal, _dot(do_raw, v_h, NT), 0.0))
            dv = _dot(a, do_raw, TN) + _dot(ke, dst_b, NT)
            dqi = _dot(da, ki, NN) + _dot(do_raw, _bf(st), NN)
            dki = _dot(da, qi, TN)
            dke = _dot(v_h, dst_b, NN)
            dst_ref[h] = _dot(do_raw, qi, TN) + dst * decay
            dbl = decay * jnp.sum(st * dst, axis=0, keepdims=True) + jnp.sum(dke * ke_f, axis=0, keepdims=True)
            dbc = dqi * qi_f - dki * ki_f - dke * ke_f
            dlg_parts.append(_dot(tri, dbc, TN, precision=lax.Precision.HIGHEST) + dbl)
            dh_ref[:, ks] = (dqi * eb * scale).astype(BF16)
            dh_ref[:, hk + h * GLA_DK: hk + (h + 1) * GLA_DK] = (dki * enb + dke * eend).astype(BF16)
            dh_ref[:, 2 * hk + h * GLA_DV: 2 * hk + (h + 1) * GLA_DV] = dv.astype(BF16)
            dh_ref[:, 2 * hk + hv + h * GLA_DV: 2 * hk + hv + (h + 1) * GLA_DV] = dr.astype(BF16)
        dlg = jnp.concatenate(dlg_parts, axis=1)
        dz = dlg * (1.0 / GLA_TAU) * _sigmoid(-z)
        dz_b = _bf(dz)
        dh_ref[:, GLA_W:] = _dot(dz_b, w2_ref[...], NT).astype(BF16)
        dw2p = _dot(_bf(glr), dz_b, TN)
        dgbp = _rows8(dz)
        dngp = jnp.concatenate(dng_parts, axis=1)

        @pl.when(i == 0)
        def _():
            dw2_ref[...] = dw2p
            dgb_ref[...] = dgbp
            dng_ref[...] = dngp

        @pl.when(i > 0)
        def _():
            dw2_ref[...] += dw2p
            dgb_ref[...] += dgbp
            dng_ref[...] += dngp

    return pl.pallas_call(
        body, name="gla_bwd", grid=(n,),
        in_specs=[pl.BlockSpec((c, hk), lambda i: (rev(i), 0)), pl.BlockSpec((c, hk), lambda i: (rev(i), 1)),
                  pl.BlockSpec((c, hv), lambda i: (rev(i), 1)), pl.BlockSpec((c, hv), lambda i: (rev(i), 2)),
                  pl.BlockSpec((c, LANES), lambda i: (rev(i), GLA_W // LANES)),
                  pl.BlockSpec((LANES, hk), lambda i: (0, 0)), pl.BlockSpec((1, hk), lambda i: (0, 0)),
                  pl.BlockSpec((1, hv), lambda i: (0, 0)),
                  pl.BlockSpec((c, hv), lambda i: (rev(i), 0)),
                  pl.BlockSpec((1, GLA_HEADS, GLA_DV, GLA_DK), lambda i: (rev(i), 0, 0, 0)),
                  pl.BlockSpec((c, hv), lambda i: (rev(i), 0))],
        out_specs=[pl.BlockSpec((c, HA_W), lambda i: (rev(i), 0)),
                   pl.BlockSpec((LANES, hk), lambda i: (0, 0)),
                   pl.BlockSpec((SUBLANES, hk), lambda i: (0, 0)),
                   pl.BlockSpec((SUBLANES, hv), lambda i: (0, 0))],
        out_shape=[jax.ShapeDtypeStruct((t, HA_W), BF16), jax.ShapeDtypeStruct((LANES, hk), F32),
                   jax.ShapeDtypeStruct((SUBLANES, hk), F32), jax.ShapeDtypeStruct((SUBLANES, hv), F32)],
        scratch_shapes=[pltpu.VMEM((GLA_HEADS, GLA_DV, GLA_DK), F32)],
        compiler_params=_params(("arbitrary",)),
    )(h_a, h_a, h_a, h_a, h_a, w2p, gate_b, norm_g, o_raw, s_before, dmix)


def _rope_tables(positions):
    half = ROPE_DIMS // 2
    inv_freq = ROPE_THETA ** (-jnp.arange(0, ROPE_DIMS, 2, dtype=F32) / ROPE_DIMS)
    ang = positions.astype(F32).reshape(-1, 1) * inv_freq
    cos, sin = jnp.cos(ang), jnp.sin(ang)
    t = cos.shape[0]
    one = jnp.ones((t, DIL_HD - ROPE_DIMS), F32)
    zero = jnp.zeros((t, DIL_HD - ROPE_DIMS), F32)
    zh = jnp.zeros((t, half), F32)
    return (jnp.concatenate([cos, cos, one], axis=1), jnp.concatenate([-sin, zh, zero], axis=1),
            jnp.concatenate([zh, sin, zero], axis=1))


def _rope_apply(x, c, s1, s2):
    half = ROPE_DIMS // 2
    return x * c + pltpu.roll(x, DIL_HD - half, 1) * s1 + pltpu.roll(x, half, 1) * s2


def _rope_apply_t(dy, c, s1, s2):
    half = ROPE_DIMS // 2
    return dy * c + pltpu.roll(dy * s1, half, 1) + pltpu.roll(dy * s2, DIL_HD - half, 1)


def _rope_fwd(h_b, tabs, tr=256):
    t = h_b.shape[0]
    w = DIL_HEADS * DIL_HD
    scale = DIL_HD ** -0.5
    tab = pl.BlockSpec((tr, DIL_HD), lambda i: (i, 0))
    outb = pl.BlockSpec((tr, w), lambda i: (i, 0))

    def body(q_ref, k_ref, c_ref, s1_ref, s2_ref, qo_ref, ko_ref):
        c, s1, s2 = c_ref[...], s1_ref[...], s2_ref[...]
        for h in range(DIL_HEADS):
            hs = slice(h * DIL_HD, (h + 1) * DIL_HD)
            qo_ref[:, hs] = _rope_apply(q_ref[:, hs] * scale, c, s1, s2)
            ko_ref[:, hs] = _rope_apply(k_ref[:, hs], c, s1, s2)

    return pl.pallas_call(
        body, name="rope_fwd", grid=(t // tr,),
        in_specs=[pl.BlockSpec((tr, w), lambda i: (i, 0)), pl.BlockSpec((tr, w), lambda i: (i, 1)), tab, tab, tab],
        out_specs=[outb, outb],
        out_shape=[jax.ShapeDtypeStruct((t, w), F32)] * 2,
        compiler_params=_params(("parallel",)),
    )(h_b, h_b, *tabs)


def _dil_dh(dq, dk, dv, tabs, tr=256):
    t, w = dq.shape
    scale = DIL_HD ** -0.5
    tab = pl.BlockSpec((tr, DIL_HD), lambda i: (i, 0))
    inb = pl.BlockSpec((tr, w), lambda i: (i, 0))

    def body(dq_ref, dk_ref, dv_ref, c_ref, s1_ref, s2_ref, o_ref):
        c, s1, s2 = c_ref[...], s1_ref[...], s2_ref[...]
        for h in range(DIL_HEADS):
            hs = slice(h * DIL_HD, (h + 1) * DIL_HD)
            o_ref[:, h * DIL_HD:(h + 1) * DIL_HD] = (_rope_apply_t(dq_ref[:, hs], c, s1, s2) * scale).astype(BF16)
            o_ref[:, w + h * DIL_HD: w + (h + 1) * DIL_HD] = _rope_apply_t(dk_ref[:, hs], c, s1, s2).astype(BF16)
        o_ref[:, 2 * w:] = dv_ref[...].astype(BF16)

    return pl.pallas_call(
        body, name="dil_dh", grid=(t // tr,), in_specs=[inb] * 3 + [tab] * 3,
        out_specs=pl.BlockSpec((tr, 3 * w), lambda i: (i, 0)),
        out_shape=jax.ShapeDtypeStruct((t, 3 * w), BF16), compiler_params=_params(("parallel",)),
    )(dq, dk, dv, *tabs)


def _band_masks(not_first):
    r = lax.broadcasted_iota(jnp.int32, (DIL_BAND, 2 * DIL_BAND), 0)
    c = lax.broadcasted_iota(jnp.int32, (DIL_BAND, 2 * DIL_BAND), 1)
    nf = jnp.full((DIL_BAND, 2 * DIL_BAND), not_first, jnp.int32)
    look_back = jnp.logical_and(jnp.logical_and(c < DIL_BAND, c >= r), nf > 0)
    own_band = jnp.logical_and(c >= DIL_BAND, (c - DIL_BAND) <= r)
    return jnp.logical_or(look_back, own_band)


def _dil_fwd(q, k, v, nb, name):
    rows = q.shape[0]
    blk = BANDS * DIL_BAND
    steps = rows // blk
    main = pl.BlockSpec((blk, DIL_HD), lambda i: (i, 0))
    prev = pl.BlockSpec((DIL_BAND, DIL_HD), lambda i: (jnp.maximum(i * BANDS - 1, 0), 0))

    def body(q_ref, k_ref, v_ref, kp_ref, vp_ref, o_ref, l_ref):
        i = pl.program_id(0)
        for j in range(BANDS):
            lo, hi = j * DIL_BAND, (j + 1) * DIL_BAND
            if j == 0:
                kcat = jnp.concatenate([kp_ref[...], k_ref[lo:hi, :]], axis=0)
                vcat = jnp.concatenate([vp_ref[...], v_ref[lo:hi, :]], axis=0)
            else:
                kcat = k_ref[lo - DIL_BAND:hi, :]
                vcat = v_ref[lo - DIL_BAND:hi, :]
            not_first = (((i * BANDS + j) % nb) != 0).astype(jnp.int32)
            s = jnp.where(_band_masks(not_first), _dot(q_ref[lo:hi, :], kcat, NT), NEG)
            m = jnp.max(s, axis=-1, keepdims=True)
            p = jnp.exp(s - m)
            den = jnp.sum(p, axis=-1, keepdims=True)
            o_ref[lo:hi, :] = _dot(_bf(p), vcat, NN) / den
            l_ref[lo:hi, :] = jnp.broadcast_to(m + jnp.log(den), (DIL_BAND, DIL_HD))

    return pl.pallas_call(
        body, name=name, grid=(steps,), in_specs=[main, main, main, prev, prev], out_specs=[main, main],
        out_shape=[jax.ShapeDtypeStruct((rows, DIL_HD), F32)] * 2, compiler_params=_params(("parallel",)),
    )(q, k, v, k, v)


def _dil_bwd(q, k, v, do, lse, dd, nb, name):
    rows = q.shape[0]
    blk = BANDS * DIL_BAND
    steps = rows // blk
    last_band = rows // DIL_BAND - 1
    main = pl.BlockSpec((blk, DIL_HD), lambda i: (i, 0))
    prev = pl.BlockSpec((DIL_BAND, DIL_HD), lambda i: (jnp.maximum(i * BANDS - 1, 0), 0))
    nxt = pl.BlockSpec((DIL_BAND, DIL_HD), lambda i: (jnp.minimum(i * BANDS + BANDS, last_band), 0))

    def body(q_ref, k_ref, v_ref, do_ref, l_ref, dd_ref, kp_ref, vp_ref, qn_ref, don_ref, ln_ref, ddn_ref,
             dq_ref, dk_ref, dv_ref, ak_ref, av_ref):
        i = pl.program_id(0)
        ak_ref[...] = jnp.zeros_like(ak_ref)
        av_ref[...] = jnp.zeros_like(av_ref)
        for j in range(BANDS + 1):
            lo, hi = j * DIL_BAND, (j + 1) * DIL_BAND
            if j == 0:
                kcat = jnp.concatenate([kp_ref[...], k_ref[lo:hi, :]], axis=0)
                vcat = jnp.concatenate([vp_ref[...], v_ref[lo:hi, :]], axis=0)
            elif j < BANDS:
                kcat = k_ref[lo - DIL_BAND:hi, :]
                vcat = v_ref[lo - DIL_BAND:hi, :]
            else:
                kcat = jnp.concatenate([k_ref[lo - DIL_BAND:lo, :], k_ref[lo - DIL_BAND:lo, :]], axis=0)
                vcat = jnp.concatenate([v_ref[lo - DIL_BAND:lo, :], v_ref[lo - DIL_BAND:lo, :]], axis=0)
            if j < BANDS:
                qj, doj, lj, ddj = q_ref[lo:hi, :], do_ref[lo:hi, :], l_ref[lo:hi, :], dd_ref[lo:hi, :]
            else:
                qj, doj, lj, ddj = qn_ref[...], don_ref[...], ln_ref[...], ddn_ref[...]
            not_first = (((i * BANDS + j) % nb) != 0).astype(jnp.int32)
            mask = _band_masks(not_first)
            if j == BANDS:
                cidx = lax.broadcasted_iota(jnp.int32, mask.shape, 1)
                mask = jnp.logical_and(mask, cidx < DIL_BAND)
            s = jnp.where(mask, _dot(qj, kcat, NT), NEG)
            p = jnp.exp(s - jnp.concatenate([lj, lj], axis=1))
            dp = _dot(doj, vcat, NT)
            ds = _bf(p * (dp - jnp.concatenate([ddj, ddj], axis=1)))
            if j < BANDS:
                dq_ref[lo:hi, :] = _dot(ds, kcat, NN)
            ak_ref[lo:hi + DIL_BAND, :] += _dot(ds, qj, TN)
            av_ref[lo:hi + DIL_BAND, :] += _dot(_bf(p), doj, TN)
        dk_ref[...] = ak_ref[DIL_BAND:DIL_BAND + blk, :]
        dv_ref[...] = av_ref[DIL_BAND:DIL_BAND + blk, :]

    return pl.pallas_call(
        body, name=name, grid=(steps,),
        in_specs=[main] * 6 + [prev, prev] + [nxt] * 4, out_specs=[main] * 3,
        out_shape=[jax.ShapeDtypeStruct((rows, DIL_HD), F32)] * 3,
        scratch_shapes=[pltpu.VMEM((blk + 2 * DIL_BAND, DIL_HD), F32)] * 2,
        compiler_params=_params(("parallel",)),
    )(q, k, v, do, lse, dd, k, v, q, do, lse, dd)


def _dil_merge(os_, ls_, tr=256):
    t, w = os_[0].shape
    blk = pl.BlockSpec((tr, w), lambda i: (i, 0))

    def body(o1, o2, o3, l1, l2, l3, ob_ref, of_ref, lt_ref):
        a, b, c = l1[...], l2[...], l3[...]
        m = jnp.maximum(jnp.maximum(a, b), c)
        ea, eb, ec = jnp.exp(a - m), jnp.exp(b - m), jnp.exp(c - m)
        den = ea + eb + ec
        o = (ea * o1[...] + eb * o2[...] + ec * o3[...]) / den
        ob_ref[...] = o.astype(BF16)
        of_ref[...] = o
        lt_ref[...] = m + jnp.log(den)

    return pl.pallas_call(
        body, name="dil_merge", grid=(t // tr,), in_specs=[blk] * 6, out_specs=[blk] * 3,
        out_shape=[jax.ShapeDtypeStruct((t, w), BF16), jax.ShapeDtypeStruct((t, w), F32),
                   jax.ShapeDtypeStruct((t, w), F32)],
        compiler_params=_params(("parallel",)),
    )(*os_, *ls_)


def _dil_bwd_prep(dmix, o_d, tr=256):
    t, w = o_d.shape
    blk = pl.BlockSpec((tr, w), lambda i: (i, 0))

    def body(do_ref, o_ref, dob_ref, dd_ref):
        do = do_ref[...]
        prod = do * o_ref[...]
        dob_ref[...] = do.astype(BF16)
        for h in range(DIL_HEADS):
            hs = slice(h * DIL_HD, (h + 1) * DIL_HD)
            dd_ref[:, hs] = jnp.broadcast_to(jnp.sum(prod[:, hs], axis=-1, keepdims=True), (tr, DIL_HD))

    return pl.pallas_call(
        body, name="dil_bwd_prep", grid=(t // tr,),
        in_specs=[pl.BlockSpec((tr, w), lambda i: (i, 1)), blk], out_specs=[blk, blk],
        out_shape=[jax.ShapeDtypeStruct((t, w), BF16), jax.ShapeDtypeStruct((t, w), F32)],
        compiler_params=_params(("parallel",)),
    )(dmix, o_d)


def _gather_rows(dst_ref, src_ref, t, d, cast=None):
    n = t // d
    for r in range(d):
        v = src_ref[pl.ds(r, n, stride=d), :] if d > 1 else src_ref[...]
        dst_ref[r * n:(r + 1) * n, :] = v if cast is None else v.astype(cast)


def _tri_mask():
    r = lax.broadcasted_iota(jnp.int32, (DIL_BAND, DIL_BAND), 0)
    c = lax.broadcasted_iota(jnp.int32, (DIL_BAND, DIL_BAND), 1)
    return c <= r


def _dil_fwd_all(qr, kr, h_b):
    t = qr.shape[0]
    nbands = t // DIL_BAND
    nbr = len(DIL_DILATIONS)
    hoff = DIL_HEADS

    def col(off):
        return pl.BlockSpec((t, DIL_HD), lambda h: (0, off + h), pipeline_mode=pl.Buffered(1))

    outb = pl.BlockSpec((t, DIL_HD), lambda h: (0, h))

    def body(q_ref, k_ref, v_ref, ob_ref, of_ref, lt_ref, qs, ks, vs, os_, ls_, *br):
        obr, lbr = br[:nbr], br[nbr:]
        for bi, d in enumerate(DIL_DILATIONS):
            n = t // d
            nb = n // DIL_BAND
            _gather_rows(qs, q_ref, t, d, BF16)
            _gather_rows(ks, k_ref, t, d, BF16)
            _gather_rows(vs, v_ref, t, d, BF16)
            s = jnp.where(_tri_mask(), _dot(qs[0:DIL_BAND, :], ks[0:DIL_BAND, :], NT), NEG)
            m = jnp.max(s, axis=-1, keepdims=True)
            pr = jnp.exp(s - m)
            den = jnp.sum(pr, axis=-1, keepdims=True)
            os_[0:DIL_BAND, :] = _dot(_bf(pr), vs[0:DIL_BAND, :], NN) / den
            ls_[0:DIL_BAND, :] = jnp.broadcast_to(m + jnp.log(den), (DIL_BAND, DIL_HD))

            def band(b, carry, nb=nb):
                st = pl.multiple_of((b - 1) * DIL_BAND, DIL_BAND)
                cur = pl.ds(st + DIL_BAND, DIL_BAND)
                both = pl.ds(st, 2 * DIL_BAND)
                not_first = ((b % nb) != 0).astype(jnp.int32)
                s = jnp.where(_band_masks(not_first), _dot(qs[cur, :], ks[both, :], NT), NEG)
                m = jnp.max(s, axis=-1, keepdims=True)
                pr = jnp.exp(s - m)
                den = jnp.sum(pr, axis=-1, keepdims=True)
                os_[cur, :] = _dot(_bf(pr), vs[both, :], NN) / den
                ls_[cur, :] = jnp.broadcast_to(m + jnp.log(den), (DIL_BAND, DIL_HD))
                return carry

            lax.fori_loop(1, nbands, band, 0, unroll=4)
            for r in range(d):
                dst = pl.ds(r, n, stride=d) if d > 1 else slice(None)
                obr[bi][dst, :] = os_[r * n:(r + 1) * n, :]
                lbr[bi][dst, :] = ls_[r * n:(r + 1) * n, :]
        rows = 512
        for c0 in range(0, t, rows):
            sl = slice(c0, c0 + rows)
            la, lb, lc = lbr[0][sl, :], lbr[1][sl, :], lbr[2][sl, :]
            m = jnp.maximum(jnp.maximum(la, lb), lc)
            ea, eb, ec = jnp.exp(la - m), jnp.exp(lb - m), jnp.exp(lc - m)
            den = ea + eb + ec
            o = (ea * obr[0][sl, :] + eb * obr[1][sl, :] + ec * obr[2][sl, :]) / den
            ob_ref[sl, :] = o.astype(BF16)
            of_ref[sl, :] = o
            lt_ref[sl, :] = m + jnp.log(den)

    w = DIL_HEADS * DIL_HD
    vm = lambda dt: pltpu.VMEM((t, DIL_HD), dt)
    return pl.pallas_call(
        body, name="dil_fwd", grid=(DIL_HEADS,), in_specs=[col(0), col(0), col(2 * hoff)],
        out_specs=[outb, outb, outb],
        out_shape=[jax.ShapeDtypeStruct((t, w), BF16), jax.ShapeDtypeStruct((t, w), F32),
                   jax.ShapeDtypeStruct((t, w), F32)],
        scratch_shapes=[vm(BF16)] * 3 + [vm(F32)] * 2 + [vm(F32)] * (2 * nbr),
        compiler_params=_params(("parallel",)),
    )(qr, kr, h_b)


def _dil_bwd_all(qr, kr, h_b, dmix, o_d, lse_tot):
    t = qr.shape[0]
    nbands = t // DIL_BAND
    hoff = DIL_HEADS

    def col(off):
        return pl.BlockSpec((t, DIL_HD), lambda h: (0, off + h), pipeline_mode=pl.Buffered(1))

    outb = pl.BlockSpec((t, DIL_HD), lambda h: (0, h))

    def body(q_ref, k_ref, v_ref, do_ref, o_ref, l_ref, dq_ref, dk_ref, dv_ref,
             qs, ks, vs, dos, lss, dds, dqs, acck, accv):
        for bi, d in enumerate(DIL_DILATIONS):
            n = t // d
            nb = n // DIL_BAND
            _gather_rows(qs, q_ref, t, d, BF16)
            _gather_rows(ks, k_ref, t, d, BF16)
            _gather_rows(vs, v_ref, t, d, BF16)
            _gather_rows(dos, do_ref, t, d, BF16)
            _gather_rows(lss, l_ref, t, d)
            for r in range(d):
                src = pl.ds(r, n, stride=d) if d > 1 else slice(None)
                prod = do_ref[src, :] * o_ref[src, :]
                dds[r * n:(r + 1) * n, :] = jnp.broadcast_to(jnp.sum(prod, axis=-1, keepdims=True), (n, DIL_HD))
            acck[...] = jnp.zeros_like(acck)
            accv[...] = jnp.zeros_like(accv)
            b0 = slice(0, DIL_BAND)
            s = jnp.where(_tri_mask(), _dot(qs[b0, :], ks[b0, :], NT), NEG)
            pr = jnp.exp(s - lss[b0, :])
            ds = _bf(pr * (_dot(dos[b0, :], vs[b0, :], NT) - dds[b0, :]))
            dqs[b0, :] = _dot(ds, ks[b0, :], NN)
            acck[DIL_BAND:2 * DIL_BAND, :] += _dot(ds, qs[b0, :], TN)
            accv[DIL_BAND:2 * DIL_BAND, :] += _dot(_bf(pr), dos[b0, :], TN)

            def band(b, carry, nb=nb):
                st = pl.multiple_of((b - 1) * DIL_BAND, DIL_BAND)
                cur = pl.ds(st + DIL_BAND, DIL_BAND)
                both = pl.ds(st, 2 * DIL_BAND)
                acc_rows = pl.ds(st + DIL_BAND, 2 * DIL_BAND)
                not_first = ((b % nb) != 0).astype(jnp.int32)
                qb, dob, lb, ddb = qs[cur, :], dos[cur, :], lss[cur, :], dds[cur, :]
                kcat, vcat = ks[both, :], vs[both, :]
                s = jnp.where(_band_masks(not_first), _dot(qb, kcat, NT), NEG)
                pr = jnp.exp(s - jnp.concatenate([lb, lb], axis=1))
                ds = _bf(pr * (_dot(dob, vcat, NT) - jnp.concatenate([ddb, ddb], axis=1)))
                dqs[cur, :] = _dot(ds, kcat, NN)
                acck[acc_rows, :] += _dot(ds, qb, TN)
                accv[acc_rows, :] += _dot(_bf(pr), dob, TN)
                return carry

            lax.fori_loop(1, nbands, band, 0, unroll=4)
            for r in range(d):
                lo = r * n
                if d == 1:
                    dq_ref[...] = dqs[...]
                    dk_ref[...] = acck[DIL_BAND:DIL_BAND + t, :]
                    dv_ref[...] = accv[DIL_BAND:DIL_BAND + t, :]
                else:
                    dst = pl.ds(r, n, stride=d)
                    dq_ref[dst, :] = dq_ref[dst, :] + dqs[lo:lo + n, :]
                    dk_ref[dst, :] = dk_ref[dst, :] + acck[DIL_BAND + lo:DIL_BAND + lo + n, :]
                    dv_ref[dst, :] = dv_ref[dst, :] + accv[DIL_BAND + lo:DIL_BAND + lo + n, :]

    w = DIL_HEADS * DIL_HD
    vm = lambda dt, extra=0: pltpu.VMEM((t + extra, DIL_HD), dt)
    return pl.pallas_call(
        body, name="dil_bwd", grid=(DIL_HEADS,),
        in_specs=[col(0), col(0), col(2 * hoff), col(hoff), col(0), col(0)], out_specs=[outb] * 3,
        out_shape=[jax.ShapeDtypeStruct((t, w), F32)] * 3,
        scratch_shapes=[vm(BF16)] * 4 + [vm(F32)] * 3 + [vm(F32, DIL_BAND)] * 2,
        compiler_params=_params(("parallel",)),
    )(qr, kr, h_b, dmix, o_d, lse_tot)


def _ca_fwd(q, memkv, tq=512):
    t, d = q.shape
    m = memkv.shape[0]
    scale = CA_HD ** -0.5

    def body(q_ref, k_ref, v_ref, o_ref, ot_ref):
        for h in range(CA_HEADS):
            hs = slice(h * CA_HD, (h + 1) * CA_HD)
            s = _dot(q_ref[:, hs], k_ref[:, hs], NT) * scale
            p = jnp.exp(s - jnp.max(s, axis=-1, keepdims=True))
            p = p / jnp.sum(p, axis=-1, keepdims=True)
            o = _dot(_bf(p), v_ref[:, hs], NN).astype(BF16)
            o_ref[:, hs] = o
            ot_ref[hs, :] = o.T

    return pl.pallas_call(
        body, name="ca_fwd", grid=(t // tq,),
        in_specs=[pl.BlockSpec((tq, d), lambda i: (i, 0)), pl.BlockSpec((m, d), lambda i: (0, 0)),
                  pl.BlockSpec((m, d), lambda i: (0, 1))],
        out_specs=[pl.BlockSpec((tq, d), lambda i: (i, 0)), pl.BlockSpec((d, tq), lambda i: (0, i))],
        out_shape=[jax.ShapeDtypeStruct((t, d), BF16), jax.ShapeDtypeStruct((d, t), BF16)],
        compiler_params=_params(("parallel",)),
    )(q, memkv, memkv)


def _ca_bwd(q, memkv, do, tq=512):
    t, d = q.shape
    m = memkv.shape[0]
    scale = CA_HD ** -0.5

    def body(q_ref, k_ref, v_ref, do_ref, dq_ref, dkv_ref):
        i = pl.program_id(0)

        @pl.when(i == 0)
        def _():
            dkv_ref[...] = jnp.zeros_like(dkv_ref)

        for h in range(CA_HEADS):
            hs = slice(h * CA_HD, (h + 1) * CA_HD)
            q_h, k_h, v_h, do_h = q_ref[:, hs], k_ref[:, hs], v_ref[:, hs], do_ref[:, hs]
            s = _dot(q_h, k_h, NT) * scale
            p = jnp.exp(s - jnp.max(s, axis=-1, keepdims=True))
            p = p / jnp.sum(p, axis=-1, keepdims=True)
            dp = _dot(do_h, v_h, NT)
            ds = _bf(p * (dp - jnp.sum(p * dp, axis=-1, keepdims=True)) * scale)
            dq_ref[:, hs] = _dot(ds, k_h, NN).astype(BF16)
            dkv_ref[:, hs] += _dot(ds, q_h, TN)
            dkv_ref[:, d + h * CA_HD: d + (h + 1) * CA_HD] += _dot(_bf(p), do_h, TN)

    return pl.pallas_call(
        body, name="ca_bwd", grid=(t // tq,),
        in_specs=[pl.BlockSpec((tq, d), lambda i: (i, 0)), pl.BlockSpec((m, d), lambda i: (0, 0)),
                  pl.BlockSpec((m, d), lambda i: (0, 1)), pl.BlockSpec((tq, d), lambda i: (i, 0))],
        out_specs=[pl.BlockSpec((tq, d), lambda i: (i, 0)), pl.BlockSpec((m, 2 * d), lambda i: (0, 0))],
        out_shape=[jax.ShapeDtypeStruct((t, d), BF16), jax.ShapeDtypeStruct((m, 2 * d), F32)],
        compiler_params=_params(("arbitrary",)),
    )(q, memkv, memkv, do)


STRIP = 256


def _shift_down(u, n, row):
    return jnp.where(row >= n, pltpu.roll(u, n, 0), 0.0)


def _shift_up(u, n, row):
    t = u.shape[0]
    return jnp.where(row < t - n, pltpu.roll(u, t - n, 0), 0.0)


def _conv(u, cw_ref, row):
    return ((cw_ref[3:4, :] + cw_ref[0:1, :] * _shift_down(u, 2, row)) + cw_ref[1:2, :] * _shift_down(u, 1, row)) \
        + cw_ref[2:3, :] * u


def _swiglu_fwd(u0, cw):
    t, w = u0.shape[0], u0.shape[1] // 2
    ns = w // STRIP
    col = pl.BlockSpec((t, STRIP), lambda j: (0, j))
    col_up = pl.BlockSpec((t, STRIP), lambda j: (0, ns + j))
    cws = pl.BlockSpec((SUBLANES, STRIP), lambda j: (0, j))
    cws_up = pl.BlockSpec((SUBLANES, STRIP), lambda j: (0, ns + j))

    def body(g_ref, u_ref, cg_ref, cu_ref, a_ref, at_ref):
        row = lax.broadcasted_iota(jnp.int32, (t, STRIP), 0)
        gate = _conv(g_ref[...].astype(F32), cg_ref, row)
        up = _conv(u_ref[...].astype(F32), cu_ref, row)
        act = (gate * _sigmoid(gate) * up).astype(BF16)
        a_ref[...] = act
        at_ref[...] = act.T

    return pl.pallas_call(
        body, name="swiglu_fwd", grid=(ns,), in_specs=[col, col_up, cws, cws_up],
        out_specs=[col, pl.BlockSpec((STRIP, t), lambda j: (j, 0))],
        out_shape=[jax.ShapeDtypeStruct((t, w), BF16), jax.ShapeDtypeStruct((w, t), BF16)],
        compiler_params=_params(("parallel",)),
    )(u0, u0, cw, cw)


def _swiglu_bwd(u0, cw, da):
    t, w = u0.shape[0], u0.shape[1] // 2
    ns = w // STRIP
    col = pl.BlockSpec((t, STRIP), lambda j: (0, j))
    col_up = pl.BlockSpec((t, STRIP), lambda j: (0, ns + j))
    cws = pl.BlockSpec((SUBLANES, STRIP), lambda j: (0, j))
    cws_up = pl.BlockSpec((SUBLANES, STRIP), lambda j: (0, ns + j))

    def conv_bwd(du, u0, cw_ref, row, du0_ref, du0t_ref, dcw_ref):
        du1, du2 = _shift_up(du, 1, row), _shift_up(du, 2, row)
        du0 = ((cw_ref[2:3, :] * du + cw_ref[1:2, :] * du1) + cw_ref[0:1, :] * du2).astype(BF16)
        du0_ref[...] = du0
        du0t_ref[...] = du0.T
        dcw_ref[0:1, :] = jnp.sum(du2 * u0, axis=0, keepdims=True)
        dcw_ref[1:2, :] = jnp.sum(du1 * u0, axis=0, keepdims=True)
        dcw_ref[2:3, :] = jnp.sum(du * u0, axis=0, keepdims=True)
        dcw_ref[3:4, :] = jnp.sum(du, axis=0, keepdims=True)
        dcw_ref[4:8, :] = jnp.zeros((4, STRIP), F32)

    def body(g_ref, u_ref, cg_ref, cu_ref, da_ref, dg0_ref, du0_ref, dut_ref, dcg_ref, dcu_ref):
        row = lax.broadcasted_iota(jnp.int32, (t, STRIP), 0)
        g0, up0 = g_ref[...].astype(F32), u_ref[...].astype(F32)
        gate = _conv(g0, cg_ref, row)
        up = _conv(up0, cu_ref, row)
        sg = _sigmoid(gate)
        da = da_ref[...].astype(F32)
        dgate = da * up * (sg * (1.0 + gate * (1.0 - sg)))
        dup = da * (gate * sg)
        conv_bwd(dgate, g0, cg_ref, row, dg0_ref, dut_ref.at[0], dcg_ref)
        conv_bwd(dup, up0, cu_ref, row, du0_ref, dut_ref.at[1], dcu_ref)

    return pl.pallas_call(
        body, name="swiglu_bwd", grid=(ns,), in_specs=[col, col_up, cws, cws_up, col],
        out_specs=[col, col, pl.BlockSpec((2, STRIP, t), lambda j: (0, j, 0)), cws, cws],
        out_shape=[jax.ShapeDtypeStruct((t, w), BF16), jax.ShapeDtypeStruct((t, w), BF16),
                   jax.ShapeDtypeStruct((2, w, t), BF16),
                   jax.ShapeDtypeStruct((SUBLANES, w), F32), jax.ShapeDtypeStruct((SUBLANES, w), F32)],
        compiler_params=_params(("parallel",)),
    )(u0, u0, cw, cw, da)


def _ffn_win_grad(dut, x2b, tn=512):
    t, d = x2b.shape
    sp, sw = FF_SLAB_P, FF_SLAB

    def body(a_ref, b_ref, o_ref, ob_ref):
        res = _dot(a_ref[...], b_ref[...], NN)
        o_ref[...] = res[:sw, :]
        ob_ref[...] = res[:sw, :].astype(BF16)

    o_spec = pl.BlockSpec((None, sw, tn), lambda j, n: (j, 0, n))
    return pl.pallas_call(
        body, name="mm_g_ffn_in", grid=(8, d // tn),
        in_specs=[pl.BlockSpec((None, sp, t), lambda j, n: (j // 4, j % 4, 0)),
                  pl.BlockSpec((t, tn), lambda j, n: (0, n))],
        out_specs=[o_spec, o_spec],
        out_shape=[jax.ShapeDtypeStruct((8, sw, d), F32), jax.ShapeDtypeStruct((8, sw, d), BF16)],
        compiler_params=_params(("parallel", "parallel")),
    )(dut, x2b)


def _tile2d(r, c, limit=1 << 20):
    tr, tc = r, c
    while tr * tc * 4 > limit:
        if tr % (2 * SUBLANES) == 0:
            tr //= 2
        elif tc % (2 * LANES) == 0:
            tc //= 2
        else:
            break
    return tr, tc


def _adamw_math(w, m, v, g):
    c1 = 1.0 - ADAM_B1 ** ADAM_STEP
    c2 = 1.0 - ADAM_B2 ** ADAM_STEP
    mm = ADAM_B1 * m + (1.0 - ADAM_B1) * g
    vv = ADAM_B2 * v + (1.0 - ADAM_B2) * (g * g)
    delta = -ADAM_LR * ((mm / c1) / (jnp.sqrt(vv / c2) + ADAM_EPS) + ADAM_WD * w)
    return delta, mm, vv


def _adamw(w, m, v, g, name):
    r, c = w.shape
    blk = pl.BlockSpec((r, c), lambda i: (0, 0))

    def body(w_ref, m_ref, v_ref, gi_ref, g_ref, d_ref, nm_ref, nv_ref):
        g = gi_ref[...]
        d_ref[...], nm_ref[...], nv_ref[...] = _adamw_math(w_ref[...], m_ref[...], v_ref[...], g)
        g_ref[...] = g

    return pl.pallas_call(body, name=name, grid=(1,), in_specs=[blk] * 4, out_specs=[blk] * 4,
                          out_shape=[jax.ShapeDtypeStruct((r, c), F32)] * 4,
                          compiler_params=_params(("arbitrary",)))(w, m, v, g)


def _pair_add(gs, ra, core, name):
    _, _, r, c = gs.shape
    tr, tc = _tile2d(r, c)
    blk = pl.BlockSpec((None, tr, tc), lambda k, i, j, s: (k, i, j))

    def body(s_ref, g_ref, r_ref, o_ref, ob_ref):
        p = g_ref[...] + r_ref[...]
        o_ref[...] = p
        ob_ref[...] = p.astype(BF16)

    gspec = pltpu.PrefetchScalarGridSpec(
        num_scalar_prefetch=1, grid=(4, r // tr, c // tc),
        in_specs=[pl.BlockSpec((None, None, tr, tc), lambda k, i, j, s: (k, s[0], i, j)), blk], out_specs=[blk, blk])
    return pl.pallas_call(body, name=name, grid_spec=gspec,
                          out_shape=[jax.ShapeDtypeStruct((4, r, c), F32), jax.ShapeDtypeStruct((4, r, c), BF16)],
                          compiler_params=_params(("parallel", "parallel", "parallel")))(core, gs, ra)


def _small_reduce(gathered):
    nd, r, n = gathered.shape
    tn = 2048 if n % 2048 == 0 else n
    def body(g_ref, s_ref, t_ref):
        s = g_ref[0]
        for dv in range(1, nd):
            s = s + g_ref[dv]
        s_ref[...] = s
        t_ref[...] = jnp.broadcast_to(jnp.sum(s, axis=0, keepdims=True), (r, tn))

    return pl.pallas_call(
        body, name="small_reduce", grid=(n // tn,),
        in_specs=[pl.BlockSpec((nd, r, tn), lambda j: (0, 0, j))],
        out_specs=[pl.BlockSpec((r, tn), lambda j: (0, j))] * 2,
        out_shape=[jax.ShapeDtypeStruct((r, n), F32)] * 2, compiler_params=_params(("parallel",)),
    )(gathered)


HBM = pl.BlockSpec(memory_space=pltpu.HBM)


def _all_gather(arrs, name):
    n = len(arrs)

    def body(*refs):
        ins, outs = refs[:n], refs[n:2 * n]
        send, recv, lsem = refs[2 * n:]
        x, y, c = lax.axis_index("x"), lax.axis_index("y"), lax.axis_index("c")
        me, sib = (x, y, c), (x, y, 1 - c)
        chips = [(1 - x, y), (x, 1 - y), (1 - x, 1 - y)]

        def slot(w, p):
            return outs[w].at[4 * p[0] + 2 * p[1] + p[2]]

        def cp(w, k, block, to, src=None):
            return pltpu.make_async_remote_copy(
                src_ref=slot(w, block) if src is None else src, dst_ref=slot(w, block),
                send_sem=send.at[w * 7 + k], recv_sem=recv.at[w * 7 + k], device_id=to, device_id_type=MESH)

        mine = [pltpu.make_async_copy(ins[w], slot(w, me), lsem.at[w]) for w in range(n)]
        for m in mine:
            m.start()
        first = []
        for w in range(n):
            first.append(cp(w, 0, me, sib, src=ins[w]))
            first += [cp(w, 1 + j, me, (*chip, c), src=ins[w]) for j, chip in enumerate(chips)]
        for f in first:
            f.start()
        passed = []
        for j, chip in enumerate(chips):
            for w in range(n):
                cp(w, 1 + j, (*chip, c), me).wait_recv()
                fwd = cp(w, 4 + j, (*chip, c), sib)
                fwd.start()
                passed.append(fwd)
        for w in range(n):
            cp(w, 0, sib, me).wait_recv()
            for j, chip in enumerate(chips):
                cp(w, 4 + j, (*chip, 1 - c), me).wait_recv()
        for f in first + passed:
            f.wait_send()
        for m in mine:
            m.wait()

    return pl.pallas_call(
        body, name=name, in_specs=[HBM] * n, out_specs=[HBM] * n,
        out_shape=[jax.ShapeDtypeStruct((8,) + a.shape, a.dtype) for a in arrs],
        scratch_shapes=[pltpu.SemaphoreType.DMA((7 * n,)), pltpu.SemaphoreType.DMA((7 * n,)),
                        pltpu.SemaphoreType.DMA((n,))],
    )(*arrs)


def _sibling_exchange(arrs, name):
    n = len(arrs)

    def body(*refs):
        ins, outs = refs[:n], refs[n:2 * n]
        send, recv = refs[2 * n:]
        x, y, c = lax.axis_index("x"), lax.axis_index("y"), lax.axis_index("c")
        copies = [pltpu.make_async_remote_copy(
            src_ref=ins[w].at[:, 1 - c], dst_ref=outs[w], send_sem=send.at[w], recv_sem=recv.at[w],
            device_id=(x, y, 1 - c), device_id_type=MESH) for w in range(n)]
        for cpy in copies:
            cpy.start()
        for cpy in copies:
            cpy.wait()

    return pl.pallas_call(
        body, name=name, in_specs=[HBM] * n, out_specs=[HBM] * n,
        out_shape=[jax.ShapeDtypeStruct((a.shape[0],) + a.shape[2:], a.dtype) for a in arrs],
        scratch_shapes=[pltpu.SemaphoreType.DMA((n,)), pltpu.SemaphoreType.DMA((n,))],
    )(*arrs)


def _chip_exchange(arrs, name):
    n = len(arrs)

    def body(*refs):
        ins, outs = refs[:n], refs[n:2 * n]
        send, recv = refs[2 * n:]
        x, y, c = lax.axis_index("x"), lax.axis_index("y"), lax.axis_index("c")
        chips = [(1 - x, y), (x, 1 - y), (1 - x, 1 - y)]
        copies = []
        for w in range(n):
            for j, (cx, cy) in enumerate(chips):
                copies.append(pltpu.make_async_remote_copy(
                    src_ref=ins[w].at[2 * cx + cy], dst_ref=outs[w].at[j], send_sem=send.at[3 * w + j],
                    recv_sem=recv.at[3 * w + j], device_id=(cx, cy, c), device_id_type=MESH))
        for cpy in copies:
            cpy.start()
        for cpy in copies:
            cpy.wait()

    return pl.pallas_call(
        body, name=name, in_specs=[HBM] * n, out_specs=[HBM] * n,
        out_shape=[jax.ShapeDtypeStruct((3,) + a.shape[1:], a.dtype) for a in arrs],
        scratch_shapes=[pltpu.SemaphoreType.DMA((3 * n,)), pltpu.SemaphoreType.DMA((3 * n,))],
    )(*arrs)


SEM = pl.BlockSpec(memory_space=pltpu.SEMAPHORE)
ANY = pl.BlockSpec(memory_space=pl.ANY)
EFFECT = pltpu.SideEffectType.DATAFLOW_SIDE_EFFECTING
N_PEERS = 7


def _peers(x, y, c):
    return [((1 - x) if k & 4 else x, (1 - y) if k & 2 else y, (1 - c) if k & 1 else c) for k in range(1, 8)]


def _spread_copies(src_refs, land_refs, send, recv, gather):
    x, y, c = lax.axis_index("x"), lax.axis_index("y"), lax.axis_index("c")
    me = 4 * x + 2 * y + c
    copies = []
    for w in range(len(src_refs)):
        for k, (px, py, pc) in enumerate(_peers(x, y, c)):
            p = 4 * px + 2 * py + pc
            copies.append((pltpu.make_async_remote_copy(
                src_ref=src_refs[w] if gather else src_refs[w].at[p], dst_ref=land_refs[w].at[me],
                send_sem=send[w].at[k], recv_sem=recv[w].at[k], device_id=(px, py, pc), device_id_type=MESH),
                pltpu.make_async_remote_copy(
                src_ref=src_refs[w] if gather else src_refs[w].at[p], dst_ref=land_refs[w].at[p],
                send_sem=send[w].at[k], recv_sem=recv[w].at[k], device_id=(px, py, pc), device_id_type=MESH)))
    return copies


def _hbm(a):
    return pltpu.with_memory_space_constraint(a, pltpu.HBM)


def _spread_start(srcs, lands, after, gather, name):
    n = len(srcs)

    def body(*refs):
        src_refs, land_refs = refs[:n], refs[n:2 * n]
        outs = refs[2 * n + 1:]
        send, recv, token = outs[:n], outs[n:2 * n], outs[4 * n]
        for start, _ in _spread_copies(src_refs, land_refs, send, recv, gather):
            start.start()
        token[...] = jnp.zeros_like(token)

    res = pl.pallas_call(
        body, name=name,
        out_shape=tuple([pltpu.SemaphoreType.DMA((N_PEERS,))] * (2 * n)
                        + [pltpu.HBM(a.shape, a.dtype) for a in srcs] + [pltpu.HBM(a.shape, a.dtype) for a in lands]
                        + [jax.ShapeDtypeStruct((SUBLANES, LANES), F32)]),
        in_specs=[HBM] * (2 * n) + [ANY],
        out_specs=tuple([SEM] * (2 * n) + [HBM] * (2 * n) + [pl.BlockSpec(memory_space=pltpu.VMEM)]),
        input_output_aliases={i: 2 * n + i for i in range(2 * n)},
        compiler_params=pltpu.CompilerParams(has_side_effects=EFFECT),
    )(*[_hbm(a) for a in srcs], *[_hbm(a) for a in lands], after)
    return res[:n], res[n:2 * n], res[2 * n:3 * n], res[3 * n:4 * n], res[4 * n]


def _spread_wait(send, recv, srcs, lands, after, gather, name):
    n = len(srcs)
    after = list(after) if isinstance(after, (list, tuple)) else [after]

    def body(*refs):
        src_refs, land_refs = refs[:n], refs[n:2 * n]
        send_refs, recv_refs = refs[2 * n:3 * n], refs[3 * n:4 * n]
        for _, arrive in _spread_copies(src_refs, land_refs, send_refs, recv_refs, gather):
            arrive.wait_send()
            arrive.wait_recv()

    res = pl.pallas_call(
        body, name=name,
        out_shape=tuple([pltpu.HBM(a.shape, a.dtype) for a in srcs] + [pltpu.HBM(a.shape, a.dtype) for a in lands]),
        in_specs=[HBM] * (2 * n) + [SEM] * (2 * n) + [ANY] * len(after),
        out_specs=tuple([HBM] * (2 * n)),
        input_output_aliases={i: i for i in range(2 * n)},
        compiler_params=pltpu.CompilerParams(has_side_effects=EFFECT),
    )(*srcs, *lands, *send, *recv, *after)
    return res[n:]


def _landing(shape, dtype, own, me):
    return lax.dynamic_update_index_in_dim(lax.empty((8,) + shape, dtype), own, me, 0)


def _pad_cols(a, to):
    return jnp.pad(a, ((0, 0), (0, to - a.shape[1])))


N_GLR = GLA_W + GLA_RANK
FF_SLAB = D_FF // 4
FF_SLAB_P = FFP // 4


TRANSPOSED = ("w_in", "ffn_w_in")


def _prepare_sub1(gath):
    w_in_t = gath["w_in"].reshape(-1, gath["w_in"].shape[2])
    w2 = jnp.concatenate([gath["gla_gate_w2"][s] for s in range(8)], axis=1)
    return {"w_a_t": jnp.pad(w_in_t[:N_GLR], ((0, HA_W - N_GLR), (0, 0))), "w_b_t": w_in_t[N_GLR:],
            "w2p": jnp.pad(w2, ((0, LANES - GLA_RANK), (0, 0)))}


def _prepare_ffn_in(g):
    f = jnp.pad(g, ((0, 0), (0, FF_SLAB_P - FF_SLAB), (0, 0)))
    return f.reshape(2 * FFP, f.shape[2])


def _prepare_ffn_out(g):
    return jnp.pad(g.reshape(4, FF_SLAB, -1), ((0, 0), (0, FF_SLAB_P - FF_SLAB), (0, 0))).reshape(FFP, -1)


def _prepare_conv(g, conv_b):
    padc = FF_SLAB_P - FF_SLAB
    cw = jnp.pad(g, ((0, 0), (0, 0), (0, padc)))
    cb = jnp.pad(conv_b.reshape(8, 1, FF_SLAB), ((0, 0), (0, 0), (0, padc)))
    rows = jnp.concatenate([cw, cb, jnp.zeros((8, 4, FF_SLAB_P), F32)], axis=1)
    return jnp.concatenate([rows[s] for s in range(8)], axis=1)


def _prepare_ffn(gath, conv_b):
    return {"w_ffn_t": _prepare_ffn_in(gath["ffn_w_in"]), "wo": _prepare_ffn_out(gath["ffn_w_out"]),
            "cw": _prepare_conv(gath["ffn_conv_w"], conv_b)}


def _unpad_ff(a):
    r = a.shape[0]
    return a.reshape(r, 4, FF_SLAB_P)[:, :, :FF_SLAB].reshape(r, D_FF)


def _grad_slabs(g):
    w_in_t = jnp.concatenate([g["w_a_t"][:N_GLR], g["w_b_t"]], axis=0)
    s = {"w_in": w_in_t.reshape(4, 2, w_in_t.shape[0] // 8, w_in_t.shape[1])}
    for n in ("w_out", "ca_wq", "ca_wo"):
        s[n] = _to_slabs(n, g[n])
    for n in ("ca_wkv", "ffn_w_in"):
        s[n] = g[n].reshape((4, 2) + g[n].shape[1:])
    wo = g["wo"].reshape(4, FF_SLAB_P, -1)[:, :FF_SLAB]
    s["ffn_w_out"] = wo.reshape(4, 2, FF_SLAB // 2, wo.shape[-1])
    return s


class _AtHand:
    def __init__(self, p):
        self.p = p
        self.token = None

    def sub2(self, after):
        return self.p

    def ffn_in(self, after):
        return self.p["w_ffn_t"]

    def ffn_out(self, after):
        return self.p["wo"]

    def grads_out(self, group, slabs):
        pass

    def small_out(self, parts):
        pass


def _local_step(x, mem, positions, target, p, small, stages=None):
    t, d = x.shape
    stages = _AtHand(p) if stages is None else stages
    w_a_t, w_b_t, w2p, cw = p["w_a_t"], p["w_b_t"], p["w2p"], p["cw"]
    tabs = _rope_tables(positions)
    xb = x.astype(BF16) if stages.token is None else (x + stages.token[0, 0]).astype(BF16)
    memb = mem.astype(BF16)

    h_a = _matmul(xb, w_a_t, "nt", F32, 1024, 640, d, "mm_h_a")
    h_b = _matmul(xb, w_b_t, "nt", F32, 1024, 1024, d, "mm_h_b")
    o_g, o_raw, s_before = _gla_fwd(h_a, w2p, small["gla_gate_b"], small["gla_norm_g"])
    qr, kr = _rope_fwd(h_b, tabs)
    o_d_b, o_d, lse_tot = _dil_fwd_all(qr, kr, h_b)
    mixin = jnp.concatenate([o_g, o_d_b], axis=1)
    wts = stages.sub2(mixin)
    mix = _matmul(mixin, wts["w_out"], "nn", F32, 1024, 1024, d, "mm_mix")
    x1, x1b, x1t = _ln_fwd(x, mix, small["ln1_g"], small["ln1_b"], "ln1_fwd", True)

    q_ca = _matmul(x1b, wts["ca_wq"], "nn", BF16, 1024, 1024, d, "mm_caq")
    kvw = wts["ca_wkv"].shape[2]
    memkv = _matmul(memb, wts["ca_wkv"], "nn", BF16, mem.shape[0], kvw, d, "mm_memkv", b_slabs=True)
    o_c, o_ct = _ca_fwd(q_ca, memkv)
    ca_out = _matmul(o_c, wts["ca_wo"], "nn", F32, 1024, 1024, d, "mm_cao")
    x2, x2b = _ln_fwd(x1, ca_out, small["ln2_g"], small["ln2_b"], "ln2_fwd", False)

    w_ffn_t = stages.ffn_in(x2b)
    u0 = _matmul(x2b, w_ffn_t, "nt", BF16, 1024, 1024, d, "mm_u0")
    act, act_t = _swiglu_fwd(u0, cw)
    wo = stages.ffn_out(act)
    ffn = _matmul(act, wo, "nn", F32, 512, 512, FFP, "mm_ffn")

    dp3, dp3b, dg3, db3, loss_part = _ln_bwd(x2, ffn, small["ln3_g"], small["ln3_b"], target, True, "ln3_bwd")
    g_wo, g_wo16 = _matmul(act_t, dp3b, "nn", F32, 512, 1024, t, "mm_g_wo", also_bf16=True)
    dact = _matmul(dp3b, wo, "nt", BF16, 1024, 512, d, "mm_dact")
    dug, duu, du_t, dcwg, dcwu = _swiglu_bwd(u0, cw, dact)
    g_ffn_in, g_ffn_in16 = _ffn_win_grad(du_t, x2b)

    def wo_slabs(a):
        a = a.reshape(4, FF_SLAB_P, -1)[:, :FF_SLAB]
        return a.reshape(8, FF_SLAB // 2, a.shape[-1])

    sent = stages.grads_out("ffn", {"ffn_w_out": (wo_slabs(g_wo), wo_slabs(g_wo16)), "ffn_w_in": (g_ffn_in, g_ffn_in16)})
    dx2 = _matmul(dug, w_ffn_t, "nn", F32, 512, 1024, FFP, "mm_dx2_g", resid=dp3, resid_scale=ALPHA, dep=sent)
    dx2 = _matmul(duu, w_ffn_t, "nn", F32, 512, 1024, FFP, "mm_dx2_u", resid=dx2, b_k_off=1)

    dp2, dp2b, dg2, db2 = _ln_bwd(x1, ca_out, small["ln2_g"], small["ln2_b"], dx2, False, "ln2_bwd")
    g_cao, g_cao16 = _matmul(o_ct, dp2b, "nn", F32, 512, 1024, t, "mm_g_cao", also_bf16=True)
    do_c = _matmul(dp2b, wts["ca_wo"], "nt", BF16, 1024, 1024, d, "mm_do_c")
    dq_ca, dmemkv = _ca_bwd(q_ca, memkv, do_c)
    g_caq, g_caq16 = _matmul(x1t, dq_ca, "nn", F32, 512, 1024, t, "mm_g_caq", also_bf16=True)
    g_cakv, g_cakv16 = _matmul(memb, dmemkv.astype(BF16), "tn", F32, 512, kvw, mem.shape[0], "mm_g_cakv",
                               out_slabs=True, also_bf16=True)
    dx1 = _matmul(dq_ca, wts["ca_wq"], "nt", F32, 1024, 1024, d, "mm_dx1", resid=dp2, resid_scale=ALPHA)

    dp1, dp1b, dg1, db1 = _ln_bwd(x, mix, small["ln1_g"], small["ln1_b"], dx1, False, "ln1_bwd")
    g_wout, g_wout16 = _matmul(mixin, dp1b, "tn", F32, 512, 1024, t, "mm_g_wout", also_bf16=True)

    def row_slabs(a):
        return a.reshape(8, a.shape[0] // 8, a.shape[1])

    sent = stages.grads_out("attn", {"ca_wo": (row_slabs(g_cao), row_slabs(g_cao16)),
                                     "ca_wq": (row_slabs(g_caq), row_slabs(g_caq16)), "ca_wkv": (g_cakv, g_cakv16),
                                     "w_out": (row_slabs(g_wout), row_slabs(g_wout16))})
    dmix = _matmul(dp1b, wts["w_out"], "nt", F32, 1024, 1024, d, "mm_dmix", dep=sent)
    dh_a, dw2, dgate_b, dnorm_g = _gla_bwd(h_a, w2p, small["gla_gate_b"], small["gla_norm_g"], o_raw, s_before, dmix)
    small_parts = {
        "gla_gate_b": dgate_b, "gla_norm_g": dnorm_g, "ln1_g": dg1, "ln1_b": db1, "ln2_g": dg2, "ln2_b": db2,
        "ln3_g": dg3, "ln3_b": db3,
        "conv": jnp.concatenate([_unpad_ff(dcwg), _unpad_ff(dcwu)], axis=1),
        "gla_gate_w2": dw2[:GLA_RANK],
    }
    sent = stages.small_out(small_parts)
    dq_d, dk_d, dv_d = _dil_bwd_all(qr, kr, h_b, dmix, o_d, lse_tot)
    dh_b = _dil_dh(dq_d, dk_d, dv_d, tabs)
    g_wa_t, g_wa16 = _matmul(dh_a, xb, "tn", F32, 640, 1024, t, "mm_g_wa", also_bf16=True, dep=sent)
    g_wb_t, g_wb16 = _matmul(dh_b, xb, "tn", F32, 512, 1024, t, "mm_g_wb", also_bf16=True)

    def w_in_slabs(a, b):
        full = jnp.concatenate([a[:N_GLR], b], axis=0)
        return full.reshape(8, full.shape[0] // 8, full.shape[1])

    sent = stages.grads_out("w_in", {"w_in": (w_in_slabs(g_wa_t, g_wb_t), w_in_slabs(g_wa16, g_wb16))})
    dx = _matmul(dh_a, w_a_t, "nn", F32, 512, 1024, HA_W, "mm_dx_a", resid=dp1, resid_scale=ALPHA, dep=sent)
    dx = _matmul(dh_b, w_b_t, "nn", F32, 512, 1024, HB_W, "mm_dx_b", resid=dx)

    grads = {"w_a_t": g_wa_t, "w_b_t": g_wb_t, "w_out": g_wout, "ca_wq": g_caq, "ca_wkv": g_cakv, "ca_wo": g_cao,
             "ffn_w_in": g_ffn_in, "wo": g_wo}
    return loss_part, dx, grads, small_parts


BIG = ("w_in", "w_out", "ca_wq", "ca_wkv", "ca_wo", "ffn_w_in", "ffn_w_out")
COL_SHARDED = ("w_in", "ca_wkv", "ffn_w_in")
SMALL_ORDER = ("gla_gate_b", "gla_norm_g", "ln1_g", "ln1_b", "ln2_g", "ln2_b", "ln3_g", "ln3_b")


def _gathered_full(name, g):
    if name in COL_SHARDED:
        return g.transpose(1, 0, 2).reshape(g.shape[1], 8 * g.shape[2])
    return g.reshape(8 * g.shape[1], g.shape[2])


def _to_slabs(name, full):
    if name in COL_SHARDED:
        r, cc = full.shape
        s = full.reshape(r, 8, cc // 8).transpose(1, 0, 2)
    else:
        rr, c = full.shape
        s = full.reshape(8, rr // 8, c)
    return s.reshape((4, 2) + s.shape[1:])


def kernel(x, mem, positions, w_in, gla_gate_w2, gla_gate_b, gla_norm_g, w_out, ln1_g, ln1_b, ca_wq, ca_wkv, ca_wo, ln2_g, ln2_b, ffn_w_in, ffn_conv_w, ffn_conv_b, ffn_w_out, ln3_g, ln3_b, loss_target, m_w_in, m_gla_gate_w2, m_gla_gate_b, m_gla_norm_g, m_w_out, m_ln1_g, m_ln1_b, m_ca_wq, m_ca_wkv, m_ca_wo, m_ln2_g, m_ln2_b, m_ffn_w_in, m_ffn_conv_w, m_ffn_conv_b, m_ffn_w_out, m_ln3_g, m_ln3_b, v_w_in, v_gla_gate_w2, v_gla_gate_b, v_gla_norm_g, v_w_out, v_ln1_g, v_ln1_b, v_ca_wq, v_ca_wkv, v_ca_wo, v_ln2_g, v_ln2_b, v_ffn_w_in, v_ffn_conv_w, v_ffn_conv_b, v_ffn_w_out, v_ln3_g, v_ln3_b):
    weights = dict(w_in=w_in, gla_gate_w2=gla_gate_w2, gla_gate_b=gla_gate_b, gla_norm_g=gla_norm_g, w_out=w_out,
                   ln1_g=ln1_g, ln1_b=ln1_b, ca_wq=ca_wq, ca_wkv=ca_wkv, ca_wo=ca_wo, ln2_g=ln2_g, ln2_b=ln2_b,
                   ffn_w_in=ffn_w_in, ffn_conv_w=ffn_conv_w, ffn_conv_b=ffn_conv_b, ffn_w_out=ffn_w_out,
                   ln3_g=ln3_g, ln3_b=ln3_b)
    moms = dict(w_in=(m_w_in, v_w_in), gla_gate_w2=(m_gla_gate_w2, v_gla_gate_w2), gla_gate_b=(m_gla_gate_b, v_gla_gate_b),
                gla_norm_g=(m_gla_norm_g, v_gla_norm_g), w_out=(m_w_out, v_w_out), ln1_g=(m_ln1_g, v_ln1_g),
                ln1_b=(m_ln1_b, v_ln1_b), ca_wq=(m_ca_wq, v_ca_wq), ca_wkv=(m_ca_wkv, v_ca_wkv), ca_wo=(m_ca_wo, v_ca_wo),
                ln2_g=(m_ln2_g, v_ln2_g), ln2_b=(m_ln2_b, v_ln2_b), ffn_w_in=(m_ffn_w_in, v_ffn_w_in),
                ffn_conv_w=(m_ffn_conv_w, v_ffn_conv_w), ffn_conv_b=(m_ffn_conv_b, v_ffn_conv_b),
                ffn_w_out=(m_ffn_w_out, v_ffn_w_out), ln3_g=(m_ln3_g, v_ln3_g), ln3_b=(m_ln3_b, v_ln3_b))
    order = list(weights)
    xi, yi, ci = lax.axis_index("x"), lax.axis_index("y"), lax.axis_index("c")
    me = 4 * xi + 2 * yi + ci

    def travel(n, a):
        return jnp.swapaxes(a, 1, 2) if n in TRANSPOSED else a

    shard = {n: travel(n, weights[n]).astype(BF16)[0] for n in BIG}
    first = _all_gather([shard["w_in"], gla_gate_w2.astype(BF16)[0], ffn_conv_w[0]], "ag_first")
    p = _prepare_sub1({"w_in": first[0], "gla_gate_w2": first[1]})
    p["cw"] = _prepare_conv(first[2], ffn_conv_b)
    later = ("w_out", "ca_wq", "ca_wkv", "ca_wo", "ffn_w_in", "ffn_w_out")
    srcs = [shard[n] for n in later]
    lands = [_landing(shard[n].shape, BF16, shard[n], me) for n in later]
    send, recv, srcs, lands, token = _spread_start(srcs, lands, first[0], True, "ag_rest_start")

    class stages:
        pass

    stages.token = token

    def arrived(lo, hi, after, name):
        return _spread_wait(send[lo:hi], recv[lo:hi], srcs[lo:hi], lands[lo:hi], after, True, name)

    def sub2(after):
        g = dict(zip(later[:4], arrived(0, 4, after, "ag_wait_attn")))
        w = {n: _gathered_full(n, g[n]) for n in ("w_out", "ca_wq", "ca_wo")}
        w["ca_wkv"] = g["ca_wkv"]
        return w

    stages.sub2 = sub2
    stages.ffn_in = lambda after: _prepare_ffn_in(arrived(4, 5, after, "ag_wait_ffn_in")[0])
    stages.ffn_out = lambda after: _prepare_ffn_out(arrived(5, 6, after, "ag_wait_ffn_out")[0])
    sent = {}

    def grads_out(group, slabs):
        names = list(slabs)
        srcs16 = [slabs[n][1] for n in names]
        zones = [_landing(s.shape[1:], BF16, jnp.zeros(s.shape[1:], BF16), me) for s in srcs16]
        snd, rcv, s_thru, l_thru, tok = _spread_start(srcs16, zones, slabs[names[0]][0], False, f"rs_{group}_start")
        sent[group] = (names, [slabs[n][0] for n in names], (snd, rcv, s_thru, l_thru))
        return tok

    stages.grads_out = grads_out
    small_sent = []

    def small_out(parts):
        packed = jnp.concatenate([parts[n] for n in SMALL_ORDER] + [parts["conv"],
                                 parts["gla_gate_w2"].reshape(SUBLANES, -1)], axis=1)
        packed = jnp.pad(packed, ((0, 0), (0, (-packed.shape[1]) % 2048)))
        zone = _landing(packed.shape, F32, packed, me)
        snd, rcv, s_thru, l_thru, tok = _spread_start([packed], [zone], packed, True, "ag_small_start")
        small_sent.append((snd, rcv, s_thru, l_thru))
        return tok

    stages.small_out = small_out
    small = dict(gla_gate_b=gla_gate_b, gla_norm_g=gla_norm_g, ln1_g=ln1_g, ln1_b=ln1_b, ln2_g=ln2_g, ln2_b=ln2_b,
                 ln3_g=ln3_g, ln3_b=ln3_b)

    loss_part, dx, grads, small_parts = _local_step(x[0], mem[0], positions[0], loss_target[0], p, small, stages)
    loss = lax.psum(jnp.sum(loss_part), ("x", "y", "c"))

    out = {}
    (allp,) = _spread_wait(*small_sent[0], dx, True, "ag_small_wait")
    dev_sum, row_sum = _small_reduce(allp)

    me1 = me.reshape(1).astype(jnp.int32)

    def finish_group(group, after):
        names, own32, handles = sent[group]
        landed = _spread_wait(*handles, after, False, f"rs_{group}_wait")
        for n, own, land in zip(names, own32, landed):
            m_, v_ = moms[n]
            res4 = _adamw_direct(travel(n, weights[n]), travel(n, m_), travel(n, v_), own, land, me1, f"adamw_{n}")
            out[n] = [travel(n, a) for a in res4]

    finish_group("ffn", dx)
    finish_group("attn", dx)
    off = 0
    for n in SMALL_ORDER:
        width = weights[n].shape[1]
        g = row_sum[0:1, off:off + width]
        off += width
        m_, v_ = moms[n]
        out[n] = _adamw(weights[n], m_, v_, g, f"adamw_{n}")
    conv_g = dev_sum[:, off:off + 2 * D_FF]
    off += 2 * D_FF
    g_cb = conv_g[3:4]
    out["ffn_conv_b"] = _adamw(ffn_conv_b, m_ffn_conv_b, v_ffn_conv_b, g_cb, "adamw_ffn_conv_b")
    wsh = ffn_conv_w.shape[2]
    g_cw = lax.dynamic_slice_in_dim(conv_g[0:3], me * wsh, wsh, axis=1)
    out["ffn_conv_w"] = _adamw(ffn_conv_w[0], m_ffn_conv_w[0], v_ffn_conv_w[0], g_cw, "adamw_ffn_conv_w")
    w2_g = dev_sum[:, off:off + GLA_RANK * GLA_HEADS * GLA_DK // SUBLANES].reshape(GLA_RANK, GLA_HEADS * GLA_DK)
    wsh2 = gla_gate_w2.shape[2]
    g_w2 = lax.dynamic_slice_in_dim(w2_g, me * wsh2, wsh2, axis=1)
    out["gla_gate_w2"] = _adamw(gla_gate_w2[0], m_gla_gate_w2[0], v_gla_gate_w2[0], g_w2, "adamw_gla_gate_w2")
    finish_group("w_in", [o[1] for o in out.values()])

    def shaped(n, a):
        return a.reshape(weights[n].shape)

    res = [loss, dx[None]]
    for k in range(4):
        res += [shaped(n, out[n][k]) for n in order]
    return tuple(res)


def _adamw_direct(w, m, v, own, land, me, name):
    _, r, c = w.shape
    tr, tc = _tile2d(r, c)
    blk = pl.BlockSpec((None, tr, tc), lambda i, j, s: (0, i, j))
    mine = pl.BlockSpec((None, tr, tc), lambda i, j, s: (s[0], i, j))
    slots = [pl.BlockSpec((None, tr, tc), lambda i, j, s, k=k: (k, i, j)) for k in range(8)]

    def body(s_ref, w_ref, m_ref, v_ref, p_ref, *rest):
        slot_refs, (g_ref, d_ref, nm_ref, nv_ref) = rest[:8], rest[8:]
        g = p_ref[...]
        for sr in slot_refs:
            g = g + sr[...].astype(F32)
        d_ref[...], nm_ref[...], nv_ref[...] = _adamw_math(w_ref[...], m_ref[...], v_ref[...], g)
        g_ref[...] = g

    gs = pltpu.PrefetchScalarGridSpec(num_scalar_prefetch=1, grid=(r // tr, c // tc),
                                      in_specs=[blk, blk, blk, mine] + slots, out_specs=[blk] * 4)
    return pl.pallas_call(body, name=name, grid_spec=gs, out_shape=[jax.ShapeDtypeStruct((1, r, c), F32)] * 4,
                          compiler_params=_params(("parallel", "parallel")))(me, w, m, v, own, *([land] * 8))


def _adamw_big(w, m, v, p32, rc, chip, name):
    _, r, c = w.shape
    tr, tc = _tile2d(r, c)
    blk = pl.BlockSpec((None, tr, tc), lambda i, j, s: (0, i, j))
    own = pl.BlockSpec((None, tr, tc), lambda i, j, s: (s[0], i, j))
    others = [pl.BlockSpec((None, tr, tc), lambda i, j, s, k=k: (k, i, j)) for k in range(3)]

    def body(s_ref, w_ref, m_ref, v_ref, p_ref, r0_ref, r1_ref, r2_ref, g_ref, d_ref, nm_ref, nv_ref):
        g = ((p_ref[...] + r0_ref[...].astype(F32)) + r1_ref[...].astype(F32)) + r2_ref[...].astype(F32)
        d_ref[...], nm_ref[...], nv_ref[...] = _adamw_math(w_ref[...], m_ref[...], v_ref[...], g)
        g_ref[...] = g

    gs = pltpu.PrefetchScalarGridSpec(num_scalar_prefetch=1, grid=(r // tr, c // tc),
                                      in_specs=[blk, blk, blk, own] + others, out_specs=[blk] * 4)
    return pl.pallas_call(body, name=name, grid_spec=gs, out_shape=[jax.ShapeDtypeStruct((1, r, c), F32)] * 4,
                          compiler_params=_params(("parallel", "parallel")))(chip, w, m, v, p32, rc, rc, rc)
```

```python
import functools
import math

import jax
import jax.numpy as jnp
from jax import lax
from jax.experimental import pallas as pl
from jax.experimental.pallas import tpu as pltpu

F32 = jnp.float32
BF16 = jnp.bfloat16
MESH = pl.DeviceIdType.MESH

D_MODEL = 2048
LN_EPS = 1e-5
GLA_HEADS = 4
GLA_DV = 256
GLA_DK = 128
GLA_RANK = 16
GLA_TAU = 16.0
GLA_CHUNK = 64
DIL_HD = 128
DIL_HEADS = 8
DIL_BAND = 128
DIL_DILATIONS = (1, 4, 16)
ROPE_THETA = 500000.0
ROPE_DIMS = 32
CA_HEADS = 4
CA_HD = 512
D_FF = 5504
ALPHA = 2.0 ** 0.25
ADAM_LR = 0.001
ADAM_B1 = 0.9
ADAM_B2 = 0.999
ADAM_EPS = 1e-08
ADAM_WD = 0.01
ADAM_STEP = 10

LANES = 128
SUBLANES = 8
VMEM_LIMIT = 56 * 1024 * 1024

GLA_W = 2 * GLA_HEADS * GLA_DK + 2 * GLA_HEADS * GLA_DV
HA_W = GLA_W + LANES
HB_W = 3 * DIL_HEADS * DIL_HD
FFP = 5632
NEG = -1e30


def _params(sem):
    return pltpu.CompilerParams(dimension_semantics=sem, vmem_limit_bytes=VMEM_LIMIT)


def _sigmoid(x):
    return 1.0 / (1.0 + jnp.exp(-x))


def _dot(a, b, dn, precision=None):
    return lax.dot_general(a, b, (dn, ((), ())), preferred_element_type=F32, precision=precision)


NN = ((1,), (0,))
NT = ((1,), (1,))
TN = ((0,), (0,))


def _bf(v):
    return v if v.dtype == BF16 else v.astype(BF16)


def _matmul(a, b, kind, out_dtype, tm, tn, tk, name, resid=None, resid_scale=1.0, b_k_off=0, b_slabs=False,
            out_slabs=False, also_bf16=False, dep=None):
    if b_slabs:
        assert kind != "nt" and b.shape[2] == tn
        k2, n = b.shape[1], b.shape[0] * tn
    elif kind == "nt":
        n, k2 = b.shape
    else:
        k2, n = b.shape
    (k, m) = a.shape if kind == "tn" else a.shape[::-1]
    assert k2 >= k and (k2 == k or not b_slabs) and m % tm == 0 and n % tn == 0 and k % tk == 0, \
        (name, a.shape, b.shape, tm, tn, tk)
    nk = k // tk
    dn = {"nn": NN, "nt": NT, "tn": TN}[kind]
    a_spec = pl.BlockSpec((tk, tm), lambda i, j, kk: (kk, i)) if kind == "tn" else pl.BlockSpec((tm, tk), lambda i, j, kk: (i, kk))
    if b_slabs:
        b_spec = pl.BlockSpec((None, tk, tn), lambda i, j, kk: (j, kk, 0))
    elif kind == "nt":
        b_spec = pl.BlockSpec((tn, tk), lambda i, j, kk: (j, kk + b_k_off))
    else:
        b_spec = pl.BlockSpec((tk, tn), lambda i, j, kk: (kk + b_k_off, j))
    if out_slabs:
        o_spec = pl.BlockSpec((None, tm, tn), lambda i, j, kk: (j, i, 0))
        o_shape = (n // tn, m, tn)
    else:
        o_spec = pl.BlockSpec((tm, tn), lambda i, j, kk: (i, j))
        o_shape = (m, n)
    has_resid = resid is not None

    n_in = 2 + int(has_resid) + int(dep is not None)

    def body(*refs):
        a_ref, b_ref = refs[:2]
        r_ref = refs[2] if has_resid else None
        o_ref = refs[n_in]
        ob_ref = refs[n_in + 1] if also_bf16 else None
        part = _dot(_bf(a_ref[...]), _bf(b_ref[...]), dn)

        def finish(acc):
            if has_resid:
                acc = acc + resid_scale * r_ref[...].astype(F32)
            o_ref[...] = acc.astype(out_dtype)
            if also_bf16:
                ob_ref[...] = acc.astype(BF16)

        if nk == 1:
            finish(part)
        else:
            acc_ref = refs[-1]
            kk = pl.program_id(2)

            @pl.when(kk == 0)
            def _():
                acc_ref[...] = part

            @pl.when(kk > 0)
            def _():
                acc_ref[...] += part

            @pl.when(kk == nk - 1)
            def _():
                finish(acc_ref[...])

    in_specs = [a_spec, b_spec] + ([o_spec] if has_resid else [])
    args = (a, b) + ((resid,) if has_resid else ())
    if dep is not None:
        in_specs.append(pl.BlockSpec((SUBLANES, LANES), lambda i, j, kk: (0, 0)))
        args += (dep,)
    o_struct = jax.ShapeDtypeStruct(o_shape, out_dtype)
    return pl.pallas_call(
        body, name=name, out_shape=[o_struct, jax.ShapeDtypeStruct(o_shape, BF16)] if also_bf16 else o_struct,
        grid=(m // tm, n // tn, nk), in_specs=in_specs, out_specs=[o_spec, o_spec] if also_bf16 else o_spec,
        scratch_shapes=[pltpu.VMEM((tm, tn), F32)] if nk > 1 else [],
        compiler_params=_params(("parallel", "parallel", "arbitrary")),
    )(*args)


def _ln_core(xres, f):
    p = ALPHA * xres + f
    mu = jnp.mean(p, axis=-1, keepdims=True)
    xc = p - mu
    var = jnp.mean(xc * xc, axis=-1, keepdims=True)
    rstd = lax.rsqrt(var + LN_EPS)
    return xc * rstd, rstd


def _rows8(v):
    r, c = v.shape
    return jnp.sum(v.reshape(r // SUBLANES, SUBLANES, c), axis=0)


def _ln_fwd(xres, f, g, b, name, transposed, tr=256):
    t, d = xres.shape
    row = pl.BlockSpec((tr, d), lambda i: (i, 0))
    vec = pl.BlockSpec((1, d), lambda i: (0, 0))

    def body(x_ref, f_ref, g_ref, b_ref, y_ref, yb_ref, *yt_ref):
        xhat, _ = _ln_core(x_ref[...], f_ref[...])
        y = xhat * g_ref[...] + b_ref[...]
        y_ref[...] = y
        yb = y.astype(BF16)
        yb_ref[...] = yb
        if transposed:
            yt_ref[0][...] = yb.T

    out_specs = [row, row] + ([pl.BlockSpec((d, tr), lambda i: (0, i))] if transposed else [])
    out_shape = [jax.ShapeDtypeStruct((t, d), F32), jax.ShapeDtypeStruct((t, d), BF16)] \
        + ([jax.ShapeDtypeStruct((d, t), BF16)] if transposed else [])
    return pl.pallas_call(
        body, name=name, grid=(t // tr,), in_specs=[row, row, vec, vec], out_specs=out_specs, out_shape=out_shape,
        compiler_params=_params(("parallel",)),
    )(xres, f, g, b)


def _ln_bwd(xres, f, g, b, dy_or_target, loss_head, name, tr=256):
    t, d = xres.shape
    row = pl.BlockSpec((tr, d), lambda i: (i, 0))
    vec = pl.BlockSpec((1, d), lambda i: (0, 0))
    acc = pl.BlockSpec((SUBLANES, d), lambda i: (0, 0))
    lacc = pl.BlockSpec((SUBLANES, LANES), lambda i: (0, 0))

    def body(x_ref, f_ref, g_ref, b_ref, t_ref, dp_ref, dpb_ref, dg_ref, db_ref, *rest):
        i = pl.program_id(0)
        xhat, rstd = _ln_core(x_ref[...], f_ref[...])
        if loss_head:
            err = xhat * g_ref[...] + b_ref[...] - t_ref[...]
            dy = err * (1.0 / d)
            sq = err * err
            lanes = sq[:, :LANES]
            for kk in range(1, d // LANES):
                lanes = lanes + sq[:, kk * LANES:(kk + 1) * LANES]
            lpart = _rows8(lanes) * (0.5 / d)
        else:
            dy = t_ref[...]
        dxh = dy * g_ref[...]
        m1 = jnp.mean(dxh, axis=-1, keepdims=True)
        m2 = jnp.mean(dxh * xhat, axis=-1, keepdims=True)
        dp = rstd * (dxh - m1 - xhat * m2)
        dp_ref[...] = dp
        dpb_ref[...] = dp.astype(BF16)
        dgp = _rows8(dy * xhat)
        dbp = _rows8(dy)

        @pl.when(i == 0)
        def _():
            dg_ref[...] = dgp
            db_ref[...] = dbp
            if loss_head:
                rest[0][...] = lpart

        @pl.when(i > 0)
        def _():
            dg_ref[...] += dgp
            db_ref[...] += dbp
            if loss_head:
                rest[0][...] += lpart

    out_shape = [jax.ShapeDtypeStruct((t, d), F32), jax.ShapeDtypeStruct((t, d), BF16),
                 jax.ShapeDtypeStruct((SUBLANES, d), F32), jax.ShapeDtypeStruct((SUBLANES, d), F32)]
    out_specs = [row, row, acc, acc]
    if loss_head:
        out_shape.append(jax.ShapeDtypeStruct((SUBLANES, LANES), F32))
        out_specs.append(lacc)
    return pl.pallas_call(
        body, name=name, grid=(t // tr,), in_specs=[row, row, vec, vec, row], out_specs=out_specs,
        out_shape=out_shape, compiler_params=_params(("arbitrary",)),
    )(xres, f, g, b, dy_or_target)


def _gla_gates(glr, w2, gb):
    z = _dot(_bf(glr), w2, NN) + gb
    lg = (jnp.minimum(z, 0.0) - jnp.log(1.0 + jnp.exp(-jnp.abs(z)))) * (1.0 / GLA_TAU)
    c = z.shape[0]
    ri = lax.broadcasted_iota(jnp.int32, (c, c), 0)
    ci = lax.broadcasted_iota(jnp.int32, (c, c), 1)
    tri = (ci <= ri).astype(F32)
    bcum = _dot(tri, lg, NN, precision=lax.Precision.HIGHEST)
    blast = jnp.sum(lg, axis=0, keepdims=True)
    return z, bcum, blast, tri


def _gla_specs(t):
    c = GLA_CHUNK
    return c, t // c


def _gla_fwd(h_a, w2p, gate_b, norm_g):
    t = h_a.shape[0]
    c, n = _gla_specs(t)
    hk, hv = GLA_HEADS * GLA_DK, GLA_HEADS * GLA_DV
    scale = GLA_DK ** -0.5

    def body(q_ref, k_ref, v_ref, r_ref, glr_ref, w2_ref, gb_ref, ng_ref, og_ref, oraw_ref, sb_ref, st_ref):
        i = pl.program_id(0)

        @pl.when(i == 0)
        def _():
            st_ref[...] = jnp.zeros_like(st_ref)

        _, bcum, blast, _ = _gla_gates(glr_ref[...], w2_ref[...], gb_ref[...])
        ri = lax.broadcasted_iota(jnp.int32, (c, c), 0)
        ci = lax.broadcasted_iota(jnp.int32, (c, c), 1)
        causal = ci <= ri
        for h in range(GLA_HEADS):
            ks = slice(h * GLA_DK, (h + 1) * GLA_DK)
            vs = slice(h * GLA_DV, (h + 1) * GLA_DV)
            b_h, bl_h = bcum[:, ks], blast[:, ks]
            q_h, k_h = q_ref[:, ks], k_ref[:, ks]
            v_h = _bf(v_ref[:, vs])
            qi = _bf(q_h * scale * jnp.exp(b_h))
            ki = _bf(k_h * jnp.exp(-b_h))
            ke = _bf(k_h * jnp.exp(bl_h - b_h))
            st = st_ref[h]
            sb_ref[0, h] = st
            a = jnp.where(causal, _dot(qi, ki, NT), 0.0)
            o = _dot(_bf(a), v_h, NN) + _dot(qi, _bf(st), NT)
            st_ref[h] = st * jnp.exp(bl_h) + _dot(v_h, ke, TN)
            oraw_ref[:, vs] = o
            mu = jnp.mean(o, axis=-1, keepdims=True)
            oc = o - mu
            var = jnp.mean(oc * oc, axis=-1, keepdims=True)
            xh = oc * lax.rsqrt(var + LN_EPS)
            r_h = r_ref[:, vs]
            og_ref[:, vs] = (xh * ng_ref[:, vs] * (r_h * _sigmoid(r_h))).astype(BF16)

    return pl.pallas_call(
        body, name="gla_fwd", grid=(n,),
        in_specs=[pl.BlockSpec((c, hk), lambda i: (i, 0)), pl.BlockSpec((c, hk), lambda i: (i, 1)),
                  pl.BlockSpec((c, hv), lambda i: (i, 1)), pl.BlockSpec((c, hv), lambda i: (i, 2)),
                  pl.BlockSpec((c, LANES), lambda i: (i, GLA_W // LANES)),
                  pl.BlockSpec((LANES, hk), lambda i: (0, 0)), pl.BlockSpec((1, hk), lambda i: (0, 0)),
                  pl.BlockSpec((1, hv), lambda i: (0, 0))],
        out_specs=[pl.BlockSpec((c, hv), lambda i: (i, 0)), pl.BlockSpec((c, hv), lambda i: (i, 0)),
                   pl.BlockSpec((1, GLA_HEADS, GLA_DV, GLA_DK), lambda i: (i, 0, 0, 0))],
        out_shape=[jax.ShapeDtypeStruct((t, hv), BF16), jax.ShapeDtypeStruct((t, hv), F32),
                   jax.ShapeDtypeStruct((n, GLA_HEADS, GLA_DV, GLA_DK), F32)],
        scratch_shapes=[pltpu.VMEM((GLA_HEADS, GLA_DV, GLA_DK), F32)],
        compiler_params=_params(("arbitrary",)),
    )(h_a, h_a, h_a, h_a, h_a, w2p, gate_b, norm_g)


def _gla_bwd(h_a, w2p, gate_b, norm_g, o_raw, s_before, dmix):
    t = h_a.shape[0]
    c, n = _gla_specs(t)
    hk, hv = GLA_HEADS * GLA_DK, GLA_HEADS * GLA_DV
    scale = GLA_DK ** -0.5
    rev = lambda i: n - 1 - i

    def body(q_ref, k_ref, v_ref, r_ref, glr_ref, w2_ref, gb_ref, ng_ref, oraw_ref, sb_ref, do_ref,
             dh_ref, dw2_ref, dgb_ref, dng_ref, dst_ref):
        i = pl.program_id(0)

        @pl.when(i == 0)
        def _():
            dst_ref[...] = jnp.zeros_like(dst_ref)

        glr = glr_ref[...]
        z, bcum, blast, tri = _gla_gates(glr, w2_ref[...], gb_ref[...])
        ri = lax.broadcasted_iota(jnp.int32, (c, c), 0)
        ci = lax.broadcasted_iota(jnp.int32, (c, c), 1)
        causal = ci <= ri
        dlg_parts = []
        dng_parts = []
        for h in range(GLA_HEADS):
            ks = slice(h * GLA_DK, (h + 1) * GLA_DK)
            vs = slice(h * GLA_DV, (h + 1) * GLA_DV)
            o = oraw_ref[:, vs]
            mu = jnp.mean(o, axis=-1, keepdims=True)
            oc = o - mu
            var = jnp.mean(oc * oc, axis=-1, keepdims=True)
            rstd = lax.rsqrt(var + LN_EPS)
            xh = oc * rstd
            r_h = r_ref[:, vs]
            sg = _sigmoid(r_h)
            silu = r_h * sg
            dout = do_ref[:, vs]
            ng = ng_ref[:, vs]
            dng_parts.append(_rows8(dout * xh * silu))
            dr = dout * xh * ng * (sg * (1.0 + r_h * (1.0 - sg)))
            dxh = dout * ng * silu
            m1 = jnp.mean(dxh, axis=-1, keepdims=True)
            m2 = jnp.mean(dxh * xh, axis=-1, keepdims=True)
            do_raw = _bf(rstd * (dxh - m1 - xh * m2))
            b_h, bl_h = bcum[:, ks], blast[:, ks]
            q_h, k_h = q_ref[:, ks], k_ref[:, ks]
            v_h = _bf(v_ref[:, vs])
            eb, enb, eend = jnp.exp(b_h), jnp.exp(-b_h), jnp.exp(bl_h - b_h)
            decay = jnp.exp(bl_h)
            qi_f, ki_f, ke_f = q_h * scale * eb, k_h * enb, k_h * eend
            qi, ki, ke = _bf(qi_f), _bf(ki_f), _bf(ke_f)
            st = sb_ref[0, h]
            dst = dst_ref[h]
            dst_b = _bf(dst)
            a = _bf(jnp.where(causal, _dot(qi, ki, NT), 0.0))
            da = _bf(jnp.where(causal, _dot(do_raw, v_h, NT), 0.0))
            dv = _dot(a, do_raw, TN) + _dot(ke, dst_b, NT)
            dqi = _dot(da, ki, NN) + _dot(do_raw, _bf(st), NN)
            dki = _dot(da, qi, TN)
            dke = _dot(v_h, dst_b, NN)
            dst_ref[h] = _dot(do_raw, qi, TN) + dst * decay
            dbl = decay * jnp.sum(st * dst, axis=0, keepdims=True) + jnp.sum(dke * ke_f, axis=0, keepdims=True)
            dbc = dqi * qi_f - dki * ki_f - dke * ke_f
            dlg_parts.append(_dot(tri, dbc, TN, precision=lax.Precision.HIGHEST) + dbl)
            dh_ref[:, ks] = (dqi * eb * scale).astype(BF16)
            dh_ref[:, hk + h * GLA_DK: hk + (h + 1) * GLA_DK] = (dki * enb + dke * eend).astype(BF16)
            dh_ref[:, 2 * hk + h * GLA_DV: 2 * hk + (h + 1) * GLA_DV] = dv.astype(BF16)
            dh_ref[:, 2 * hk + hv + h * GLA_DV: 2 * hk + hv + (h + 1) * GLA_DV] = dr.astype(BF16)
        dlg = jnp.concatenate(dlg_parts, axis=1)
        dz = dlg * (1.0 / GLA_TAU) * _sigmoid(-z)
        dz_b = _bf(dz)
        dh_ref[:, GLA_W:] = _dot(dz_b, w2_ref[...], NT).astype(BF16)
        dw2p = _dot(_bf(glr), dz_b, TN)
        dgbp = _rows8(dz)
        dngp = jnp.concatenate(dng_parts, axis=1)

        @pl.when(i == 0)
        def _():
            dw2_ref[...] = dw2p
            dgb_ref[...] = dgbp
            dng_ref[...] = dngp

        @pl.when(i > 0)
        def _():
            dw2_ref[...] += dw2p
            dgb_ref[...] += dgbp
            dng_ref[...] += dngp

    return pl.pallas_call(
        body, name="gla_bwd", grid=(n,),
        in_specs=[pl.BlockSpec((c, hk), lambda i: (rev(i), 0)), pl.BlockSpec((c, hk), lambda i: (rev(i), 1)),
                  pl.BlockSpec((c, hv), lambda i: (rev(i), 1)), pl.BlockSpec((c, hv), lambda i: (rev(i), 2)),
                  pl.BlockSpec((c, LANES), lambda i: (rev(i), GLA_W // LANES)),
                  pl.BlockSpec((LANES, hk), lambda i: (0, 0)), pl.BlockSpec((1, hk), lambda i: (0, 0)),
                  pl.BlockSpec((1, hv), lambda i: (0, 0)),
                  pl.BlockSpec((c, hv), lambda i: (rev(i), 0)),
                  pl.BlockSpec((1, GLA_HEADS, GLA_DV, GLA_DK), lambda i: (rev(i), 0, 0, 0)),
                  pl.BlockSpec((c, hv), lambda i: (rev(i), 0))],
        out_specs=[pl.BlockSpec((c, HA_W), lambda i: (rev(i), 0)),
                   pl.BlockSpec((LANES, hk), lambda i: (0, 0)),
                   pl.BlockSpec((SUBLANES, hk), lambda i: (0, 0)),
                   pl.BlockSpec((SUBLANES, hv), lambda i: (0, 0))],
        out_shape=[jax.ShapeDtypeStruct((t, HA_W), BF16), jax.ShapeDtypeStruct((LANES, hk), F32),
                   jax.ShapeDtypeStruct((SUBLANES, hk), F32), jax.ShapeDtypeStruct((SUBLANES, hv), F32)],
        scratch_shapes=[pltpu.VMEM((GLA_HEADS, GLA_DV, GLA_DK), F32)],
        compiler_params=_params(("arbitrary",)),
    )(h_a, h_a, h_a, h_a, h_a, w2p, gate_b, norm_g, o_raw, s_before, dmix)


def _rope_tables(positions):
    half = ROPE_DIMS // 2
    inv_freq = ROPE_THETA ** (-jnp.arange(0, ROPE_DIMS, 2, dtype=F32) / ROPE_DIMS)
    ang = positions.astype(F32).reshape(-1, 1) * inv_freq
    cos, sin = jnp.cos(ang), jnp.sin(ang)
    t = cos.shape[0]
    one = jnp.ones((t, DIL_HD - ROPE_DIMS), F32)
    zero = jnp.zeros((t, DIL_HD - ROPE_DIMS), F32)
    zh = jnp.zeros((t, half), F32)
    return (jnp.concatenate([cos, cos, one], axis=1), jnp.concatenate([-sin, zh, zero], axis=1),
            jnp.concatenate([zh, sin, zero], axis=1))


def _rope_apply(x, c, s1, s2):
    half = ROPE_DIMS // 2
    return x * c + pltpu.roll(x, DIL_HD - half, 1) * s1 + pltpu.roll(x, half, 1) * s2


def _rope_apply_t(dy, c, s1, s2):
    half = ROPE_DIMS // 2
    return dy * c + pltpu.roll(dy * s1, half, 1) + pltpu.roll(dy * s2, DIL_HD - half, 1)


def _rope_fwd(h_b, tabs, tr=256):
    t = h_b.shape[0]
    w = DIL_HEADS * DIL_HD
    scale = DIL_HD ** -0.5
    tab = pl.BlockSpec((tr, DIL_HD), lambda i: (i, 0))
    outb = pl.BlockSpec((tr, w), lambda i: (i, 0))

    def body(q_ref, k_ref, c_ref, s1_ref, s2_ref, qo_ref, ko_ref):
        c, s1, s2 = c_ref[...], s1_ref[...], s2_ref[...]
        for h in range(DIL_HEADS):
            hs = slice(h * DIL_HD, (h + 1) * DIL_HD)
            qo_ref[:, hs] = _rope_apply(q_ref[:, hs] * scale, c, s1, s2)
            ko_ref[:, hs] = _rope_apply(k_ref[:, hs], c, s1, s2)

    return pl.pallas_call(
        body, name="rope_fwd", grid=(t // tr,),
        in_specs=[pl.BlockSpec((tr, w), lambda i: (i, 0)), pl.BlockSpec((tr, w), lambda i: (i, 1)), tab, tab, tab],
        out_specs=[outb, outb],
        out_shape=[jax.ShapeDtypeStruct((t, w), F32)] * 2,
        compiler_params=_params(("parallel",)),
    )(h_b, h_b, *tabs)


def _dil_dh(dq, dk, dv, tabs, tr=256):
    t, w = dq.shape
    scale = DIL_HD ** -0.5
    tab = pl.BlockSpec((tr, DIL_HD), lambda i: (i, 0))
    inb = pl.BlockSpec((tr, w), lambda i: (i, 0))

    def body(dq_ref, dk_ref, dv_ref, c_ref, s1_ref, s2_ref, o_ref):
        c, s1, s2 = c_ref[...], s1_ref[...], s2_ref[...]
        for h in range(DIL_HEADS):
            hs = slice(h * DIL_HD, (h + 1) * DIL_HD)
            o_ref[:, h * DIL_HD:(h + 1) * DIL_HD] = (_rope_apply_t(dq_ref[:, hs], c, s1, s2) * scale).astype(BF16)
            o_ref[:, w + h * DIL_HD: w + (h + 1) * DIL_HD] = _rope_apply_t(dk_ref[:, hs], c, s1, s2).astype(BF16)
        o_ref[:, 2 * w:] = dv_ref[...].astype(BF16)

    return pl.pallas_call(
        body, name="dil_dh", grid=(t // tr,), in_specs=[inb] * 3 + [tab] * 3,
        out_specs=pl.BlockSpec((tr, 3 * w), lambda i: (i, 0)),
        out_shape=jax.ShapeDtypeStruct((t, 3 * w), BF16), compiler_params=_params(("parallel",)),
    )(dq, dk, dv, *tabs)


def _band_masks(not_first):
    r = lax.broadcasted_iota(jnp.int32, (DIL_BAND, 2 * DIL_BAND), 0)
    c = lax.broadcasted_iota(jnp.int32, (DIL_BAND, 2 * DIL_BAND), 1)
    nf = jnp.full((DIL_BAND, 2 * DIL_BAND), not_first, jnp.int32)
    look_back = jnp.logical_and(jnp.logical_and(c < DIL_BAND, c >= r), nf > 0)
    own_band = jnp.logical_and(c >= DIL_BAND, (c - DIL_BAND) <= r)
    return jnp.logical_or(look_back, own_band)


def _gather_rows(dst_ref, src_ref, t, d, cast=None):
    n = t // d
    for r in range(d):
        v = src_ref[pl.ds(r, n, stride=d), :] if d > 1 else src_ref[...]
        dst_ref[r * n:(r + 1) * n, :] = v if cast is None else v.astype(cast)


def _tri_mask():
    r = lax.broadcasted_iota(jnp.int32, (DIL_BAND, DIL_BAND), 0)
    c = lax.broadcasted_iota(jnp.int32, (DIL_BAND, DIL_BAND), 1)
    return c <= r


def _dil_fwd_all(qr, kr, h_b):
    t = qr.shape[0]
    nbands = t // DIL_BAND
    nbr = len(DIL_DILATIONS)
    hoff = DIL_HEADS

    def col(off):
        return pl.BlockSpec((t, DIL_HD), lambda h: (0, off + h), pipeline_mode=pl.Buffered(1))

    outb = pl.BlockSpec((t, DIL_HD), lambda h: (0, h))

    def body(q_ref, k_ref, v_ref, ob_ref, of_ref, lt_ref, qs, ks, vs, os_, ls_, *br):
        obr, lbr = br[:nbr], br[nbr:]
        for bi, d in enumerate(DIL_DILATIONS):
            n = t // d
            nb = n // DIL_BAND
            _gather_rows(qs, q_ref, t, d, BF16)
            _gather_rows(ks, k_ref, t, d, BF16)
            _gather_rows(vs, v_ref, t, d, BF16)
            s = jnp.where(_tri_mask(), _dot(qs[0:DIL_BAND, :], ks[0:DIL_BAND, :], NT), NEG)
            m = jnp.max(s, axis=-1, keepdims=True)
            pr = jnp.exp(s - m)
            den = jnp.sum(pr, axis=-1, keepdims=True)
            os_[0:DIL_BAND, :] = _dot(_bf(pr), vs[0:DIL_BAND, :], NN) / den
            ls_[0:DIL_BAND, :] = jnp.broadcast_to(m + jnp.log(den), (DIL_BAND, DIL_HD))

            def band(b, carry, nb=nb):
                st = pl.multiple_of((b - 1) * DIL_BAND, DIL_BAND)
                cur = pl.ds(st + DIL_BAND, DIL_BAND)
                both = pl.ds(st, 2 * DIL_BAND)
                not_first = ((b % nb) != 0).astype(jnp.int32)
                s = jnp.where(_band_masks(not_first), _dot(qs[cur, :], ks[both, :], NT), NEG)
                m = jnp.max(s, axis=-1, keepdims=True)
                pr = jnp.exp(s - m)
                den = jnp.sum(pr, axis=-1, keepdims=True)
                os_[cur, :] = _dot(_bf(pr), vs[both, :], NN) / den
                ls_[cur, :] = jnp.broadcast_to(m + jnp.log(den), (DIL_BAND, DIL_HD))
                return carry

            lax.fori_loop(1, nbands, band, 0, unroll=4)
            for r in range(d):
                dst = pl.ds(r, n, stride=d) if d > 1 else slice(None)
                obr[bi][dst, :] = os_[r * n:(r + 1) * n, :]
                lbr[bi][dst, :] = ls_[r * n:(r + 1) * n, :]
        rows = 512
        for c0 in range(0, t, rows):
            sl = slice(c0, c0 + rows)
            la, lb, lc = lbr[0][sl, :], lbr[1][sl, :], lbr[2][sl, :]
            m = jnp.maximum(jnp.maximum(la, lb), lc)
            ea, eb, ec = jnp.exp(la - m), jnp.exp(lb - m), jnp.exp(lc - m)
            den = ea + eb + ec
            o = (ea * obr[0][sl, :] + eb * obr[1][sl, :] + ec * obr[2][sl, :]) / den
            ob_ref[sl, :] = o.astype(BF16)
            of_ref[sl, :] = o
            lt_ref[sl, :] = m + jnp.log(den)

    w = DIL_HEADS * DIL_HD
    vm = lambda dt: pltpu.VMEM((t, DIL_HD), dt)
    return pl.pallas_call(
        body, name="dil_fwd", grid=(DIL_HEADS,), in_specs=[col(0), col(0), col(2 * hoff)],
        out_specs=[outb, outb, outb],
        out_shape=[jax.ShapeDtypeStruct((t, w), BF16), jax.ShapeDtypeStruct((t, w), F32),
                   jax.ShapeDtypeStruct((t, w), F32)],
        scratch_shapes=[vm(BF16)] * 3 + [vm(F32)] * 2 + [vm(F32)] * (2 * nbr),
        compiler_params=_params(("parallel",)),
    )(qr, kr, h_b)


def _dil_bwd_all(qr, kr, h_b, dmix, o_d, lse_tot):
    t = qr.shape[0]
    nbands = t // DIL_BAND
    hoff = DIL_HEADS

    def col(off):
        return pl.BlockSpec((t, DIL_HD), lambda h: (0, off + h), pipeline_mode=pl.Buffered(1))

    outb = pl.BlockSpec((t, DIL_HD), lambda h: (0, h))

    def body(q_ref, k_ref, v_ref, do_ref, o_ref, l_ref, dq_ref, dk_ref, dv_ref,
             qs, ks, vs, dos, lss, dds, dqs, acck, accv):
        for bi, d in enumerate(DIL_DILATIONS):
            n = t // d
            nb = n // DIL_BAND
            _gather_rows(qs, q_ref, t, d, BF16)
            _gather_rows(ks, k_ref, t, d, BF16)
            _gather_rows(vs, v_ref, t, d, BF16)
            _gather_rows(dos, do_ref, t, d, BF16)
            _gather_rows(lss, l_ref, t, d)
            for r in range(d):
                src = pl.ds(r, n, stride=d) if d > 1 else slice(None)
                prod = do_ref[src, :] * o_ref[src, :]
                dds[r * n:(r + 1) * n, :] = jnp.broadcast_to(jnp.sum(prod, axis=-1, keepdims=True), (n, DIL_HD))
            acck[...] = jnp.zeros_like(acck)
            accv[...] = jnp.zeros_like(accv)
            b0 = slice(0, DIL_BAND)
            s = jnp.where(_tri_mask(), _dot(qs[b0, :], ks[b0, :], NT), NEG)
            pr = jnp.exp(s - lss[b0, :])
            ds = _bf(pr * (_dot(dos[b0, :], vs[b0, :], NT) - dds[b0, :]))
            dqs[b0, :] = _dot(ds, ks[b0, :], NN)
            acck[DIL_BAND:2 * DIL_BAND, :] += _dot(ds, qs[b0, :], TN)
            accv[DIL_BAND:2 * DIL_BAND, :] += _dot(_bf(pr), dos[b0, :], TN)

            def band(b, carry, nb=nb):
                st = pl.multiple_of((b - 1) * DIL_BAND, DIL_BAND)
                cur = pl.ds(st + DIL_BAND, DIL_BAND)
                both = pl.ds(st, 2 * DIL_BAND)
                acc_rows = pl.ds(st + DIL_BAND, 2 * DIL_BAND)
                not_first = ((b % nb) != 0).astype(jnp.int32)
                qb, dob, lb, ddb = qs[cur, :], dos[cur, :], lss[cur, :], dds[cur, :]
                kcat, vcat = ks[both, :], vs[both, :]
                s = jnp.where(_band_masks(not_first), _dot(qb, kcat, NT), NEG)
                pr = jnp.exp(s - jnp.concatenate([lb, lb], axis=1))
                ds = _bf(pr * (_dot(dob, vcat, NT) - jnp.concatenate([ddb, ddb], axis=1)))
                dqs[cur, :] = _dot(ds, kcat, NN)
                acck[acc_rows, :] += _dot(ds, qb, TN)
                accv[acc_rows, :] += _dot(_bf(pr), dob, TN)
                return carry

            lax.fori_loop(1, nbands, band, 0, unroll=4)
            for r in range(d):
                lo = r * n
                if d == 1:
                    dq_ref[...] = dqs[...]
                    dk_ref[...] = acck[DIL_BAND:DIL_BAND + t, :]
                    dv_ref[...] = accv[DIL_BAND:DIL_BAND + t, :]
                else:
                    dst = pl.ds(r, n, stride=d)
                    dq_ref[dst, :] = dq_ref[dst, :] + dqs[lo:lo + n, :]
                    dk_ref[dst, :] = dk_ref[dst, :] + acck[DIL_BAND + lo:DIL_BAND + lo + n, :]
                    dv_ref[dst, :] = dv_ref[dst, :] + accv[DIL_BAND + lo:DIL_BAND + lo + n, :]

    w = DIL_HEADS * DIL_HD
    vm = lambda dt, extra=0: pltpu.VMEM((t + extra, DIL_HD), dt)
    return pl.pallas_call(
        body, name="dil_bwd", grid=(DIL_HEADS,),
        in_specs=[col(0), col(0), col(2 * hoff), col(hoff), col(0), col(0)], out_specs=[outb] * 3,
        out_shape=[jax.ShapeDtypeStruct((t, w), F32)] * 3,
        scratch_shapes=[vm(BF16)] * 4 + [vm(F32)] * 3 + [vm(F32, DIL_BAND)] * 2,
        compiler_params=_params(("parallel",)),
    )(qr, kr, h_b, dmix, o_d, lse_tot)


def _ca_fwd(q, memkv, tq=512):
    t, d = q.shape
    m = memkv.shape[0]
    scale = CA_HD ** -0.5

    def body(q_ref, k_ref, v_ref, o_ref, ot_ref):
        for h in range(CA_HEADS):
            hs = slice(h * CA_HD, (h + 1) * CA_HD)
            s = _dot(q_ref[:, hs], k_ref[:, hs], NT) * scale
            p = jnp.exp(s - jnp.max(s, axis=-1, keepdims=True))
            p = p / jnp.sum(p, axis=-1, keepdims=True)
            o = _dot(_bf(p), v_ref[:, hs], NN).astype(BF16)
            o_ref[:, hs] = o
            ot_ref[hs, :] = o.T

    return pl.pallas_call(
        body, name="ca_fwd", grid=(t // tq,),
        in_specs=[pl.BlockSpec((tq, d), lambda i: (i, 0)), pl.BlockSpec((m, d), lambda i: (0, 0)),
                  pl.BlockSpec((m, d), lambda i: (0, 1))],
        out_specs=[pl.BlockSpec((tq, d), lambda i: (i, 0)), pl.BlockSpec((d, tq), lambda i: (0, i))],
        out_shape=[jax.ShapeDtypeStruct((t, d), BF16), jax.ShapeDtypeStruct((d, t), BF16)],
        compiler_params=_params(("parallel",)),
    )(q, memkv, memkv)


def _ca_bwd(q, memkv, do, tq=512):
    t, d = q.shape
    m = memkv.shape[0]
    scale = CA_HD ** -0.5

    def body(q_ref, k_ref, v_ref, do_ref, dq_ref, dkv_ref):
        i = pl.program_id(0)

        @pl.when(i == 0)
        def _():
            dkv_ref[...] = jnp.zeros_like(dkv_ref)

        for h in range(CA_HEADS):
            hs = slice(h * CA_HD, (h + 1) * CA_HD)
            q_h, k_h, v_h, do_h = q_ref[:, hs], k_ref[:, hs], v_ref[:, hs], do_ref[:, hs]
            s = _dot(q_h, k_h, NT) * scale
            p = jnp.exp(s - jnp.max(s, axis=-1, keepdims=True))
            p = p / jnp.sum(p, axis=-1, keepdims=True)
            dp = _dot(do_h, v_h, NT)
            ds = _bf(p * (dp - jnp.sum(p * dp, axis=-1, keepdims=True)) * scale)
            dq_ref[:, hs] = _dot(ds, k_h, NN).astype(BF16)
            dkv_ref[:, hs] += _dot(ds, q_h, TN)
            dkv_ref[:, d + h * CA_HD: d + (h + 1) * CA_HD] += _dot(_bf(p), do_h, TN)

    return pl.pallas_call(
        body, name="ca_bwd", grid=(t // tq,),
        in_specs=[pl.BlockSpec((tq, d), lambda i: (i, 0)), pl.BlockSpec((m, d), lambda i: (0, 0)),
                  pl.BlockSpec((m, d), lambda i: (0, 1)), pl.BlockSpec((tq, d), lambda i: (i, 0))],
        out_specs=[pl.BlockSpec((tq, d), lambda i: (i, 0)), pl.BlockSpec((m, 2 * d), lambda i: (0, 0))],
        out_shape=[jax.ShapeDtypeStruct((t, d), BF16), jax.ShapeDtypeStruct((m, 2 * d), F32)],
        compiler_params=_params(("arbitrary",)),
    )(q, memkv, memkv, do)


STRIP = 256


def _shift_down(u, n, row):
    return jnp.where(row >= n, pltpu.roll(u, n, 0), 0.0)


def _shift_up(u, n, row):
    t = u.shape[0]
    return jnp.where(row < t - n, pltpu.roll(u, t - n, 0), 0.0)


def _conv(u, cw_ref, row):
    return ((cw_ref[3:4, :] + cw_ref[0:1, :] * _shift_down(u, 2, row)) + cw_ref[1:2, :] * _shift_down(u, 1, row)) \
        + cw_ref[2:3, :] * u


def _swiglu_fwd(u0, cw):
    t, w = u0.shape[0], u0.shape[1] // 2
    ns = w // STRIP
    col = pl.BlockSpec((t, STRIP), lambda j: (0, j))
    col_up = pl.BlockSpec((t, STRIP), lambda j: (0, ns + j))
    cws = pl.BlockSpec((SUBLANES, STRIP), lambda j: (0, j))
    cws_up = pl.BlockSpec((SUBLANES, STRIP), lambda j: (0, ns + j))

    def body(g_ref, u_ref, cg_ref, cu_ref, a_ref, at_ref):
        row = lax.broadcasted_iota(jnp.int32, (t, STRIP), 0)
        gate = _conv(g_ref[...].astype(F32), cg_ref, row)
        up = _conv(u_ref[...].astype(F32), cu_ref, row)
        act = (gate * _sigmoid(gate) * up).astype(BF16)
        a_ref[...] = act
        at_ref[...] = act.T

    return pl.pallas_call(
        body, name="swiglu_fwd", grid=(ns,), in_specs=[col, col_up, cws, cws_up],
        out_specs=[col, pl.BlockSpec((STRIP, t), lambda j: (j, 0))],
        out_shape=[jax.ShapeDtypeStruct((t, w), BF16), jax.ShapeDtypeStruct((w, t), BF16)],
        compiler_params=_params(("parallel",)),
    )(u0, u0, cw, cw)


def _swiglu_bwd(u0, cw, da):
    t, w = u0.shape[0], u0.shape[1] // 2
    ns = w // STRIP
    col = pl.BlockSpec((t, STRIP), lambda j: (0, j))
    col_up = pl.BlockSpec((t, STRIP), lambda j: (0, ns + j))
    cws = pl.BlockSpec((SUBLANES, STRIP), lambda j: (0, j))
    cws_up = pl.BlockSpec((SUBLANES, STRIP), lambda j: (0, ns + j))

    def conv_bwd(du, u0, cw_ref, row, du0_ref, du0t_ref, dcw_ref):
        du1, du2 = _shift_up(du, 1, row), _shift_up(du, 2, row)
        du0 = ((cw_ref[2:3, :] * du + cw_ref[1:2, :] * du1) + cw_ref[0:1, :] * du2).astype(BF16)
        du0_ref[...] = du0
        du0t_ref[...] = du0.T
        dcw_ref[0:1, :] = jnp.sum(du2 * u0, axis=0, keepdims=True)
        dcw_ref[1:2, :] = jnp.sum(du1 * u0, axis=0, keepdims=True)
        dcw_ref[2:3, :] = jnp.sum(du * u0, axis=0, keepdims=True)
        dcw_ref[3:4, :] = jnp.sum(du, axis=0, keepdims=True)
        dcw_ref[4:8, :] = jnp.zeros((4, STRIP), F32)

    def body(g_ref, u_ref, cg_ref, cu_ref, da_ref, dg0_ref, du0_ref, dut_ref, dcg_ref, dcu_ref):
        row = lax.broadcasted_iota(jnp.int32, (t, STRIP), 0)
        g0, up0 = g_ref[...].astype(F32), u_ref[...].astype(F32)
        gate = _conv(g0, cg_ref, row)
        up = _conv(up0, cu_ref, row)
        sg = _sigmoid(gate)
        da = da_ref[...].astype(F32)
        dgate = da * up * (sg * (1.0 + gate * (1.0 - sg)))
        dup = da * (gate * sg)
        conv_bwd(dgate, g0, cg_ref, row, dg0_ref, dut_ref.at[0], dcg_ref)
        conv_bwd(dup, up0, cu_ref, row, du0_ref, dut_ref.at[1], dcu_ref)

    return pl.pallas_call(
        body, name="swiglu_bwd", grid=(ns,), in_specs=[col, col_up, cws, cws_up, col],
        out_specs=[col, col, pl.BlockSpec((2, STRIP, t), lambda j: (0, j, 0)), cws, cws],
        out_shape=[jax.ShapeDtypeStruct((t, w), BF16), jax.ShapeDtypeStruct((t, w), BF16),
                   jax.ShapeDtypeStruct((2, w, t), BF16),
                   jax.ShapeDtypeStruct((SUBLANES, w), F32), jax.ShapeDtypeStruct((SUBLANES, w), F32)],
        compiler_params=_params(("parallel",)),
    )(u0, u0, cw, cw, da)


def _ffn_win_grad(dut, x2b, tn=512):
    t, d = x2b.shape
    sp, sw = FF_SLAB_P, FF_SLAB

    def body(a_ref, b_ref, o_ref, ob_ref):
        res = _dot(a_ref[...], b_ref[...], NN)
        o_ref[...] = res[:sw, :]
        ob_ref[...] = res[:sw, :].astype(BF16)

    o_spec = pl.BlockSpec((None, sw, tn), lambda j, n: (j, 0, n))
    return pl.pallas_call(
        body, name="mm_g_ffn_in", grid=(8, d // tn),
        in_specs=[pl.BlockSpec((None, sp, t), lambda j, n: (j // 4, j % 4, 0)),
                  pl.BlockSpec((t, tn), lambda j, n: (0, n))],
        out_specs=[o_spec, o_spec],
        out_shape=[jax.ShapeDtypeStruct((8, sw, d), F32), jax.ShapeDtypeStruct((8, sw, d), BF16)],
        compiler_params=_params(("parallel", "parallel")),
    )(dut, x2b)


def _tile2d(r, c, limit=1 << 20):
    tr, tc = r, c
    while tr * tc * 4 > limit:
        if tr % (2 * SUBLANES) == 0:
            tr //= 2
        elif tc % (2 * LANES) == 0:
            tc //= 2
        else:
            break
    return tr, tc


def _adamw_math(w, m, v, g):
    c1 = 1.0 - ADAM_B1 ** ADAM_STEP
    c2 = 1.0 - ADAM_B2 ** ADAM_STEP
    mm = ADAM_B1 * m + (1.0 - ADAM_B1) * g
    vv = ADAM_B2 * v + (1.0 - ADAM_B2) * (g * g)
    delta = -ADAM_LR * ((mm / c1) / (jnp.sqrt(vv / c2) + ADAM_EPS) + ADAM_WD * w)
    return delta, mm, vv


def _adamw(w, m, v, g, name):
    r, c = w.shape
    blk = pl.BlockSpec((r, c), lambda i: (0, 0))

    def body(w_ref, m_ref, v_ref, gi_ref, g_ref, d_ref, nm_ref, nv_ref):
        g = gi_ref[...]
        d_ref[...], nm_ref[...], nv_ref[...] = _adamw_math(w_ref[...], m_ref[...], v_ref[...], g)
        g_ref[...] = g

    return pl.pallas_call(body, name=name, grid=(1,), in_specs=[blk] * 4, out_specs=[blk] * 4,
                          out_shape=[jax.ShapeDtypeStruct((r, c), F32)] * 4,
                          compiler_params=_params(("arbitrary",)))(w, m, v, g)


def _small_reduce(gathered):
    nd, r, n = gathered.shape
    tn = 2048 if n % 2048 == 0 else n
    def body(g_ref, s_ref, t_ref):
        s = g_ref[0]
        for dv in range(1, nd):
            s = s + g_ref[dv]
        s_ref[...] = s
        t_ref[...] = jnp.broadcast_to(jnp.sum(s, axis=0, keepdims=True), (r, tn))

    return pl.pallas_call(
        body, name="small_reduce", grid=(n // tn,),
        in_specs=[pl.BlockSpec((nd, r, tn), lambda j: (0, 0, j))],
        out_specs=[pl.BlockSpec((r, tn), lambda j: (0, j))] * 2,
        out_shape=[jax.ShapeDtypeStruct((r, n), F32)] * 2, compiler_params=_params(("parallel",)),
    )(gathered)


HBM = pl.BlockSpec(memory_space=pltpu.HBM)


def _all_gather(arrs, name):
    n = len(arrs)

    def body(*refs):
        ins, outs = refs[:n], refs[n:2 * n]
        send, recv, lsem = refs[2 * n:]
        x, y, c = lax.axis_index("x"), lax.axis_index("y"), lax.axis_index("c")
        me, sib = (x, y, c), (x, y, 1 - c)
        chips = [(1 - x, y), (x, 1 - y), (1 - x, 1 - y)]

        def slot(w, p):
            return outs[w].at[4 * p[0] + 2 * p[1] + p[2]]

        def cp(w, k, block, to, src=None):
            return pltpu.make_async_remote_copy(
                src_ref=slot(w, block) if src is None else src, dst_ref=slot(w, block),
                send_sem=send.at[w * 7 + k], recv_sem=recv.at[w * 7 + k], device_id=to, device_id_type=MESH)

        mine = [pltpu.make_async_copy(ins[w], slot(w, me), lsem.at[w]) for w in range(n)]
        for m in mine:
            m.start()
        first = []
        for w in range(n):
            first.append(cp(w, 0, me, sib, src=ins[w]))
            first += [cp(w, 1 + j, me, (*chip, c), src=ins[w]) for j, chip in enumerate(chips)]
        for f in first:
            f.start()
        passed = []
        for j, chip in enumerate(chips):
            for w in range(n):
                cp(w, 1 + j, (*chip, c), me).wait_recv()
                fwd = cp(w, 4 + j, (*chip, c), sib)
                fwd.start()
                passed.append(fwd)
        for w in range(n):
            cp(w, 0, sib, me).wait_recv()
            for j, chip in enumerate(chips):
                cp(w, 4 + j, (*chip, 1 - c), me).wait_recv()
        for f in first + passed:
            f.wait_send()
        for m in mine:
            m.wait()

    return pl.pallas_call(
        body, name=name, in_specs=[HBM] * n, out_specs=[HBM] * n,
        out_shape=[jax.ShapeDtypeStruct((8,) + a.shape, a.dtype) for a in arrs],
        scratch_shapes=[pltpu.SemaphoreType.DMA((7 * n,)), pltpu.SemaphoreType.DMA((7 * n,)),
                        pltpu.SemaphoreType.DMA((n,))],
    )(*arrs)


SEM = pl.BlockSpec(memory_space=pltpu.SEMAPHORE)
ANY = pl.BlockSpec(memory_space=pl.ANY)
EFFECT = pltpu.SideEffectType.DATAFLOW_SIDE_EFFECTING
N_PEERS = 7


def _peers(x, y, c):
    return [((1 - x) if k & 4 else x, (1 - y) if k & 2 else y, (1 - c) if k & 1 else c) for k in range(1, 8)]


def _spread_copies(src_refs, land_refs, send, recv, gather):
    x, y, c = lax.axis_index("x"), lax.axis_index("y"), lax.axis_index("c")
    me = 4 * x + 2 * y + c
    copies = []
    for w in range(len(src_refs)):
        for k, (px, py, pc) in enumerate(_peers(x, y, c)):
            p = 4 * px + 2 * py + pc
            copies.append((pltpu.make_async_remote_copy(
                src_ref=src_refs[w] if gather else src_refs[w].at[p], dst_ref=land_refs[w].at[me],
                send_sem=send[w].at[k], recv_sem=recv[w].at[k], device_id=(px, py, pc), device_id_type=MESH),
                pltpu.make_async_remote_copy(
                src_ref=src_refs[w] if gather else src_refs[w].at[p], dst_ref=land_refs[w].at[p],
                send_sem=send[w].at[k], recv_sem=recv[w].at[k], device_id=(px, py, pc), device_id_type=MESH)))
    return copies


def _hbm(a):
    return pltpu.with_memory_space_constraint(a, pltpu.HBM)


def _spread_start(srcs, lands, after, gather, name):
    n = len(srcs)

    def body(*refs):
        src_refs, land_refs = refs[:n], refs[n:2 * n]
        outs = refs[2 * n + 1:]
        send, recv, token = outs[:n], outs[n:2 * n], outs[4 * n]
        for start, _ in _spread_copies(src_refs, land_refs, send, recv, gather):
            start.start()
        token[...] = jnp.zeros_like(token)

    res = pl.pallas_call(
        body, name=name,
        out_shape=tuple([pltpu.SemaphoreType.DMA((N_PEERS,))] * (2 * n)
                        + [pltpu.HBM(a.shape, a.dtype) for a in srcs] + [pltpu.HBM(a.shape, a.dtype) for a in lands]
                        + [jax.ShapeDtypeStruct((SUBLANES, LANES), F32)]),
        in_specs=[HBM] * (2 * n) + [ANY],
        out_specs=tuple([SEM] * (2 * n) + [HBM] * (2 * n) + [pl.BlockSpec(memory_space=pltpu.VMEM)]),
        input_output_aliases={i: 2 * n + i for i in range(2 * n)},
        compiler_params=pltpu.CompilerParams(has_side_effects=EFFECT),
    )(*[_hbm(a) for a in srcs], *[_hbm(a) for a in lands], after)
    return res[:n], res[n:2 * n], res[2 * n:3 * n], res[3 * n:4 * n], res[4 * n]


def _spread_wait(send, recv, srcs, lands, after, gather, name):
    n = len(srcs)
    after = list(after) if isinstance(after, (list, tuple)) else [after]

    def body(*refs):
        src_refs, land_refs = refs[:n], refs[n:2 * n]
        send_refs, recv_refs = refs[2 * n:3 * n], refs[3 * n:4 * n]
        for _, arrive in _spread_copies(src_refs, land_refs, send_refs, recv_refs, gather):
            arrive.wait_send()
            arrive.wait_recv()

    res = pl.pallas_call(
        body, name=name,
        out_shape=tuple([pltpu.HBM(a.shape, a.dtype) for a in srcs] + [pltpu.HBM(a.shape, a.dtype) for a in lands]),
        in_specs=[HBM] * (2 * n) + [SEM] * (2 * n) + [ANY] * len(after),
        out_specs=tuple([HBM] * (2 * n)),
        input_output_aliases={i: i for i in range(2 * n)},
        compiler_params=pltpu.CompilerParams(has_side_effects=EFFECT),
    )(*srcs, *lands, *send, *recv, *after)
    return res[n:]


def _landing(shape, dtype, own, me):
    return lax.dynamic_update_index_in_dim(lax.empty((8,) + shape, dtype), own, me, 0)


def _pad_cols(a, to):
    return jnp.pad(a, ((0, 0), (0, to - a.shape[1])))


N_GLR = GLA_W + GLA_RANK
FF_SLAB = D_FF // 4
FF_SLAB_P = FFP // 4


TRANSPOSED = ("w_in", "ffn_w_in")


def _prepare_sub1(gath):
    w_in_t = gath["w_in"].reshape(-1, gath["w_in"].shape[2])
    w2 = jnp.concatenate([gath["gla_gate_w2"][s] for s in range(8)], axis=1)
    return {"w_a_t": jnp.pad(w_in_t[:N_GLR], ((0, HA_W - N_GLR), (0, 0))), "w_b_t": w_in_t[N_GLR:],
            "w2p": jnp.pad(w2, ((0, LANES - GLA_RANK), (0, 0)))}


def _prepare_ffn_in(g):
    f = jnp.pad(g, ((0, 0), (0, FF_SLAB_P - FF_SLAB), (0, 0)))
    return f.reshape(2 * FFP, f.shape[2])


def _prepare_ffn_out(g):
    return jnp.pad(g.reshape(4, FF_SLAB, -1), ((0, 0), (0, FF_SLAB_P - FF_SLAB), (0, 0))).reshape(FFP, -1)


def _prepare_conv(g, conv_b):
    padc = FF_SLAB_P - FF_SLAB
    cw = jnp.pad(g, ((0, 0), (0, 0), (0, padc)))
    cb = jnp.pad(conv_b.reshape(8, 1, FF_SLAB), ((0, 0), (0, 0), (0, padc)))
    rows = jnp.concatenate([cw, cb, jnp.zeros((8, 4, FF_SLAB_P), F32)], axis=1)
    return jnp.concatenate([rows[s] for s in range(8)], axis=1)


def _prepare_ffn(gath, conv_b):
    return {"w_ffn_t": _prepare_ffn_in(gath["ffn_w_in"]), "wo": _prepare_ffn_out(gath["ffn_w_out"]),
            "cw": _prepare_conv(gath["ffn_conv_w"], conv_b)}


def _unpad_ff(a):
    r = a.shape[0]
    return a.reshape(r, 4, FF_SLAB_P)[:, :, :FF_SLAB].reshape(r, D_FF)


def _grad_slabs(g):
    w_in_t = jnp.concatenate([g["w_a_t"][:N_GLR], g["w_b_t"]], axis=0)
    s = {"w_in": w_in_t.reshape(4, 2, w_in_t.shape[0] // 8, w_in_t.shape[1])}
    for n in ("w_out", "ca_wq", "ca_wo"):
        s[n] = _to_slabs(n, g[n])
    for n in ("ca_wkv", "ffn_w_in"):
        s[n] = g[n].reshape((4, 2) + g[n].shape[1:])
    wo = g["wo"].reshape(4, FF_SLAB_P, -1)[:, :FF_SLAB]
    s["ffn_w_out"] = wo.reshape(4, 2, FF_SLAB // 2, wo.shape[-1])
    return s


class _AtHand:
    def __init__(self, p):
        self.p = p
        self.token = None

    def sub2(self, after):
        return self.p

    def ffn_in(self, after):
        return self.p["w_ffn_t"]

    def ffn_out(self, after):
        return self.p["wo"]

    def grads_out(self, group, slabs):
        pass

    def small_out(self, parts):
        pass


def _local_step(x, mem, positions, target, p, small, stages=None):
    t, d = x.shape
    stages = _AtHand(p) if stages is None else stages
    w_a_t, w_b_t, w2p, cw = p["w_a_t"], p["w_b_t"], p["w2p"], p["cw"]
    tabs = _rope_tables(positions)
    xb = x.astype(BF16) if stages.token is None else (x + stages.token[0, 0]).astype(BF16)
    memb = mem.astype(BF16)

    h_a = _matmul(xb, w_a_t, "nt", F32, 1024, 640, d, "mm_h_a")
    h_b = _matmul(xb, w_b_t, "nt", F32, 1024, 1024, d, "mm_h_b")
    o_g, o_raw, s_before = _gla_fwd(h_a, w2p, small["gla_gate_b"], small["gla_norm_g"])
    qr, kr = _rope_fwd(h_b, tabs)
    o_d_b, o_d, lse_tot = _dil_fwd_all(qr, kr, h_b)
    mixin = jnp.concatenate([o_g, o_d_b], axis=1)
    wts = stages.sub2(mixin)
    mix = _matmul(mixin, wts["w_out"], "nn", F32, 1024, 1024, d, "mm_mix")
    x1, x1b, x1t = _ln_fwd(x, mix, small["ln1_g"], small["ln1_b"], "ln1_fwd", True)

    q_ca = _matmul(x1b, wts["ca_wq"], "nn", BF16, 1024, 1024, d, "mm_caq")
    kvw = wts["ca_wkv"].shape[2]
    memkv = _matmul(memb, wts["ca_wkv"], "nn", BF16, mem.shape[0], kvw, d, "mm_memkv", b_slabs=True)
    o_c, o_ct = _ca_fwd(q_ca, memkv)
    ca_out = _matmul(o_c, wts["ca_wo"], "nn", F32, 1024, 1024, d, "mm_cao")
    x2, x2b = _ln_fwd(x1, ca_out, small["ln2_g"], small["ln2_b"], "ln2_fwd", False)

    w_ffn_t = stages.ffn_in(x2b)
    u0 = _matmul(x2b, w_ffn_t, "nt", BF16, 1024, 1024, d, "mm_u0")
    act, act_t = _swiglu_fwd(u0, cw)
    wo = stages.ffn_out(act)
    ffn = _matmul(act, wo, "nn", F32, 512, 512, FFP, "mm_ffn")

    dp3, dp3b, dg3, db3, loss_part = _ln_bwd(x2, ffn, small["ln3_g"], small["ln3_b"], target, True, "ln3_bwd")
    g_wo, g_wo16 = _matmul(act_t, dp3b, "nn", F32, 512, 1024, t, "mm_g_wo", also_bf16=True)
    dact = _matmul(dp3b, wo, "nt", BF16, 1024, 512, d, "mm_dact")
    dug, duu, du_t, dcwg, dcwu = _swiglu_bwd(u0, cw, dact)
    g_ffn_in, g_ffn_in16 = _ffn_win_grad(du_t, x2b)

    def wo_slabs(a):
        a = a.reshape(4, FF_SLAB_P, -1)[:, :FF_SLAB]
        return a.reshape(8, FF_SLAB // 2, a.shape[-1])

    def wo_own(me):
        half = FF_SLAB // 2
        return lax.dynamic_slice_in_dim(g_wo, FF_SLAB_P * (me // 2) + half * (me % 2), half, axis=0)

    sent = stages.grads_out("ffn", {"ffn_w_out": (wo_own, wo_slabs(g_wo16)), "ffn_w_in": (g_ffn_in, g_ffn_in16)})
    dx2 = _matmul(dug, w_ffn_t, "nn", F32, 512, 1024, FFP, "mm_dx2_g", resid=dp3, resid_scale=ALPHA, dep=sent)
    dx2 = _matmul(duu, w_ffn_t, "nn", F32, 512, 1024, FFP, "mm_dx2_u", resid=dx2, b_k_off=1)

    dp2, dp2b, dg2, db2 = _ln_bwd(x1, ca_out, small["ln2_g"], small["ln2_b"], dx2, False, "ln2_bwd")
    g_cao, g_cao16 = _matmul(o_ct, dp2b, "nn", F32, 512, 1024, t, "mm_g_cao", also_bf16=True)
    do_c = _matmul(dp2b, wts["ca_wo"], "nt", BF16, 1024, 1024, d, "mm_do_c")
    dq_ca, dmemkv = _ca_bwd(q_ca, memkv, do_c)
    g_caq, g_caq16 = _matmul(x1t, dq_ca, "nn", F32, 512, 1024, t, "mm_g_caq", also_bf16=True)
    g_cakv, g_cakv16 = _matmul(memb, dmemkv.astype(BF16), "tn", F32, 512, kvw, mem.shape[0], "mm_g_cakv",
                               out_slabs=True, also_bf16=True)
    dx1 = _matmul(dq_ca, wts["ca_wq"], "nt", F32, 1024, 1024, d, "mm_dx1", resid=dp2, resid_scale=ALPHA)

    dp1, dp1b, dg1, db1 = _ln_bwd(x, mix, small["ln1_g"], small["ln1_b"], dx1, False, "ln1_bwd")
    g_wout, g_wout16 = _matmul(mixin, dp1b, "tn", F32, 512, 1024, t, "mm_g_wout", also_bf16=True)

    def row_slabs(a):
        return a.reshape(8, a.shape[0] // 8, a.shape[1])

    sent = stages.grads_out("attn", {"ca_wo": (row_slabs(g_cao), row_slabs(g_cao16)),
                                     "ca_wq": (row_slabs(g_caq), row_slabs(g_caq16)), "ca_wkv": (g_cakv, g_cakv16),
                                     "w_out": (row_slabs(g_wout), row_slabs(g_wout16))})
    dmix = _matmul(dp1b, wts["w_out"], "nt", F32, 1024, 1024, d, "mm_dmix", dep=sent)
    dh_a, dw2, dgate_b, dnorm_g = _gla_bwd(h_a, w2p, small["gla_gate_b"], small["gla_norm_g"], o_raw, s_before, dmix)
    small_parts = {
        "gla_gate_b": dgate_b, "gla_norm_g": dnorm_g, "ln1_g": dg1, "ln1_b": db1, "ln2_g": dg2, "ln2_b": db2,
        "ln3_g": dg3, "ln3_b": db3,
        "conv": jnp.concatenate([_unpad_ff(dcwg), _unpad_ff(dcwu)], axis=1),
        "gla_gate_w2": dw2[:GLA_RANK],
    }
    sent = stages.small_out(small_parts)
    dq_d, dk_d, dv_d = _dil_bwd_all(qr, kr, h_b, dmix, o_d, lse_tot)
    dh_b = _dil_dh(dq_d, dk_d, dv_d, tabs)
    g_wa_t, g_wa16 = _matmul(dh_a, xb, "tn", F32, 640, 1024, t, "mm_g_wa", also_bf16=True, dep=sent)
    g_wb_t, g_wb16 = _matmul(dh_b, xb, "tn", F32, 512, 1024, t, "mm_g_wb", also_bf16=True)

    def w_in_slabs(a, b):
        full = jnp.concatenate([a[:N_GLR], b], axis=0)
        return full.reshape(8, full.shape[0] // 8, full.shape[1])

    def w_in_own(me):
        rows = (N_GLR + g_wb_t.shape[0]) // 8
        full = jnp.concatenate([g_wa_t[:N_GLR], g_wb_t], axis=0)
        return lax.dynamic_slice_in_dim(full, me * rows, rows, axis=0)

    sent = stages.grads_out("w_in", {"w_in": (w_in_own, w_in_slabs(g_wa16, g_wb16))})
    dx = _matmul(dh_a, w_a_t, "nn", F32, 512, 1024, HA_W, "mm_dx_a", resid=dp1, resid_scale=ALPHA, dep=sent)
    dx = _matmul(dh_b, w_b_t, "nn", F32, 512, 1024, HB_W, "mm_dx_b", resid=dx)

    grads = {"w_a_t": g_wa_t, "w_b_t": g_wb_t, "w_out": g_wout, "ca_wq": g_caq, "ca_wkv": g_cakv, "ca_wo": g_cao,
             "ffn_w_in": g_ffn_in, "wo": g_wo}
    return loss_part, dx, grads, small_parts


BIG = ("w_in", "w_out", "ca_wq", "ca_wkv", "ca_wo", "ffn_w_in", "ffn_w_out")
COL_SHARDED = ("w_in", "ca_wkv", "ffn_w_in")
SMALL_ORDER = ("gla_gate_b", "gla_norm_g", "ln1_g", "ln1_b", "ln2_g", "ln2_b", "ln3_g", "ln3_b")


def _gathered_full(name, g):
    if name in COL_SHARDED:
        return g.transpose(1, 0, 2).reshape(g.shape[1], 8 * g.shape[2])
    return g.reshape(8 * g.shape[1], g.shape[2])


def _to_slabs(name, full):
    if name in COL_SHARDED:
        r, cc = full.shape
        s = full.reshape(r, 8, cc // 8).transpose(1, 0, 2)
    else:
        rr, c = full.shape
        s = full.reshape(8, rr // 8, c)
    return s.reshape((4, 2) + s.shape[1:])


def kernel(x, mem, positions, w_in, gla_gate_w2, gla_gate_b, gla_norm_g, w_out, ln1_g, ln1_b, ca_wq, ca_wkv, ca_wo, ln2_g, ln2_b, ffn_w_in, ffn_conv_w, ffn_conv_b, ffn_w_out, ln3_g, ln3_b, loss_target, m_w_in, m_gla_gate_w2, m_gla_gate_b, m_gla_norm_g, m_w_out, m_ln1_g, m_ln1_b, m_ca_wq, m_ca_wkv, m_ca_wo, m_ln2_g, m_ln2_b, m_ffn_w_in, m_ffn_conv_w, m_ffn_conv_b, m_ffn_w_out, m_ln3_g, m_ln3_b, v_w_in, v_gla_gate_w2, v_gla_gate_b, v_gla_norm_g, v_w_out, v_ln1_g, v_ln1_b, v_ca_wq, v_ca_wkv, v_ca_wo, v_ln2_g, v_ln2_b, v_ffn_w_in, v_ffn_conv_w, v_ffn_conv_b, v_ffn_w_out, v_ln3_g, v_ln3_b):
    weights = dict(w_in=w_in, gla_gate_w2=gla_gate_w2, gla_gate_b=gla_gate_b, gla_norm_g=gla_norm_g, w_out=w_out,
                   ln1_g=ln1_g, ln1_b=ln1_b, ca_wq=ca_wq, ca_wkv=ca_wkv, ca_wo=ca_wo, ln2_g=ln2_g, ln2_b=ln2_b,
                   ffn_w_in=ffn_w_in, ffn_conv_w=ffn_conv_w, ffn_conv_b=ffn_conv_b, ffn_w_out=ffn_w_out,
                   ln3_g=ln3_g, ln3_b=ln3_b)
    moms = dict(w_in=(m_w_in, v_w_in), gla_gate_w2=(m_gla_gate_w2, v_gla_gate_w2), gla_gate_b=(m_gla_gate_b, v_gla_gate_b),
                gla_norm_g=(m_gla_norm_g, v_gla_norm_g), w_out=(m_w_out, v_w_out), ln1_g=(m_ln1_g, v_ln1_g),
                ln1_b=(m_ln1_b, v_ln1_b), ca_wq=(m_ca_wq, v_ca_wq), ca_wkv=(m_ca_wkv, v_ca_wkv), ca_wo=(m_ca_wo, v_ca_wo),
                ln2_g=(m_ln2_g, v_ln2_g), ln2_b=(m_ln2_b, v_ln2_b), ffn_w_in=(m_ffn_w_in, v_ffn_w_in),
                ffn_conv_w=(m_ffn_conv_w, v_ffn_conv_w), ffn_conv_b=(m_ffn_conv_b, v_ffn_conv_b),
                ffn_w_out=(m_ffn_w_out, v_ffn_w_out), ln3_g=(m_ln3_g, v_ln3_g), ln3_b=(m_ln3_b, v_ln3_b))
    order = list(weights)
    xi, yi, ci = lax.axis_index("x"), lax.axis_index("y"), lax.axis_index("c")
    me = 4 * xi + 2 * yi + ci

    def travel(n, a):
        return jnp.swapaxes(a, 1, 2) if n in TRANSPOSED else a

    shard = {n: travel(n, weights[n]).astype(BF16)[0] for n in BIG}
    first = _all_gather([shard["w_in"], gla_gate_w2.astype(BF16)[0], ffn_conv_w[0]], "ag_first")
    p = _prepare_sub1({"w_in": first[0], "gla_gate_w2": first[1]})
    p["cw"] = _prepare_conv(first[2], ffn_conv_b)
    later = ("w_out", "ca_wq", "ca_wkv", "ca_wo", "ffn_w_in", "ffn_w_out")
    srcs = [shard[n] for n in later]
    lands = [_landing(shard[n].shape, BF16, shard[n], me) for n in later]
    send, recv, srcs, lands, token = _spread_start(srcs, lands, first[0], True, "ag_rest_start")

    class stages:
        pass

    stages.token = token

    def arrived(lo, hi, after, name):
        return _spread_wait(send[lo:hi], recv[lo:hi], srcs[lo:hi], lands[lo:hi], after, True, name)

    def sub2(after):
        g = dict(zip(later[:4], arrived(0, 4, after, "ag_wait_attn")))
        w = {n: _gathered_full(n, g[n]) for n in ("w_out", "ca_wq", "ca_wo")}
        w["ca_wkv"] = g["ca_wkv"]
        return w

    stages.sub2 = sub2
    stages.ffn_in = lambda after: _prepare_ffn_in(arrived(4, 5, after, "ag_wait_ffn_in")[0])
    stages.ffn_out = lambda after: _prepare_ffn_out(arrived(5, 6, after, "ag_wait_ffn_out")[0])
    sent = {}

    def grads_out(group, slabs):
        names = list(slabs)
        srcs16 = [slabs[n][1] for n in names]
        zones = [_landing(s.shape[1:], BF16, jnp.zeros(s.shape[1:], BF16), me) for s in srcs16]
        snd, rcv, s_thru, l_thru, tok = _spread_start(srcs16, zones, srcs16[0], False, f"rs_{group}_start")
        own32 = [slabs[n][0](me) if callable(slabs[n][0]) else slabs[n][0] for n in names]
        sent[group] = (names, own32, (snd, rcv, s_thru, l_thru))
        return tok

    stages.grads_out = grads_out
    small_sent = []

    def small_out(parts):
        packed = jnp.concatenate([parts[n] for n in SMALL_ORDER] + [parts["conv"],
                                 parts["gla_gate_w2"].reshape(SUBLANES, -1)], axis=1)
        packed = jnp.pad(packed, ((0, 0), (0, (-packed.shape[1]) % 2048)))
        zone = _landing(packed.shape, F32, packed, me)
        snd, rcv, s_thru, l_thru, tok = _spread_start([packed], [zone], packed, True, "ag_small_start")
        small_sent.append((snd, rcv, s_thru, l_thru))
        return tok

    stages.small_out = small_out
    small = dict(gla_gate_b=gla_gate_b, gla_norm_g=gla_norm_g, ln1_g=ln1_g, ln1_b=ln1_b, ln2_g=ln2_g, ln2_b=ln2_b,
                 ln3_g=ln3_g, ln3_b=ln3_b)

    loss_part, dx, grads, small_parts = _local_step(x[0], mem[0], positions[0], loss_target[0], p, small, stages)
    loss = lax.psum(jnp.sum(loss_part), ("x", "y", "c"))

    out = {}
    (allp,) = _spread_wait(*small_sent[0], dx, True, "ag_small_wait")
    dev_sum, row_sum = _small_reduce(allp)

    me1 = me.reshape(1).astype(jnp.int32)

    def finish_group(group, after):
        names, own32, handles = sent[group]
        landed = _spread_wait(*handles, after, False, f"rs_{group}_wait")
        for n, own, land in zip(names, own32, landed):
            m_, v_ = moms[n]
            res4 = _adamw_direct(travel(n, weights[n]), travel(n, m_), travel(n, v_), own, land, me1, f"adamw_{n}")
            out[n] = [travel(n, a) for a in res4]

    finish_group("ffn", dx)
    finish_group("attn", dx)
    off = 0
    for n in SMALL_ORDER:
        width = weights[n].shape[1]
        g = row_sum[0:1, off:off + width]
        off += width
        m_, v_ = moms[n]
        out[n] = _adamw(weights[n], m_, v_, g, f"adamw_{n}")
    conv_g = dev_sum[:, off:off + 2 * D_FF]
    off += 2 * D_FF
    g_cb = conv_g[3:4]
    out["ffn_conv_b"] = _adamw(ffn_conv_b, m_ffn_conv_b, v_ffn_conv_b, g_cb, "adamw_ffn_conv_b")
    wsh = ffn_conv_w.shape[2]
    g_cw = lax.dynamic_slice_in_dim(conv_g[0:3], me * wsh, wsh, axis=1)
    out["ffn_conv_w"] = _adamw(ffn_conv_w[0], m_ffn_conv_w[0], v_ffn_conv_w[0], g_cw, "adamw_ffn_conv_w")
    w2_g = dev_sum[:, off:off + GLA_RANK * GLA_HEADS * GLA_DK // SUBLANES].reshape(GLA_RANK, GLA_HEADS * GLA_DK)
    wsh2 = gla_gate_w2.shape[2]
    g_w2 = lax.dynamic_slice_in_dim(w2_g, me * wsh2, wsh2, axis=1)
    out["gla_gate_w2"] = _adamw(gla_gate_w2[0], m_gla_gate_w2[0], v_gla_gate_w2[0], g_w2, "adamw_gla_gate_w2")
    finish_group("w_in", [o[1] for o in out.values()])

    def shaped(n, a):
        return a.reshape(weights[n].shape)

    res = [loss, dx[None]]
    for k in range(4):
        res += [shaped(n, out[n][k]) for n in order]
    return tuple(res)


def _adamw_direct(w, m, v, own, land, me, name):
    _, r, c = w.shape
    tr, tc = _tile2d(r, c)
    blk = pl.BlockSpec((None, tr, tc), lambda i, j, s: (0, i, j))
    if own.ndim == 2:
        mine = pl.BlockSpec((tr, tc), lambda i, j, s: (i, j))
    else:
        mine = pl.BlockSpec((None, tr, tc), lambda i, j, s: (s[0], i, j))
    slots = [pl.BlockSpec((None, tr, tc), lambda i, j, s, k=k: (k, i, j)) for k in range(8)]

    def body(s_ref, w_ref, m_ref, v_ref, p_ref, *rest):
        slot_refs, (g_ref, d_ref, nm_ref, nv_ref) = rest[:8], rest[8:]
        g = p_ref[...]
        for sr in slot_refs:
            g = g + sr[...].astype(F32)
        d_ref[...], nm_ref[...], nv_ref[...] = _adamw_math(w_ref[...], m_ref[...], v_ref[...], g)
        g_ref[...] = g

    gs = pltpu.PrefetchScalarGridSpec(num_scalar_prefetch=1, grid=(r // tr, c // tc),
                                      in_specs=[blk, blk, blk, mine] + slots, out_specs=[blk] * 4)
    return pl.pallas_call(body, name=name, grid_spec=gs, out_shape=[jax.ShapeDtypeStruct((1, r, c), F32)] * 4,
                          compiler_params=_params(("parallel", "parallel")))(me, w, m, v, own, *([land] * 8))
```

```python
import jax
import jax.numpy as jnp
from jax import lax
from jax.experimental import pallas as pl
from jax.experimental.pallas import tpu as pltpu

F32 = jnp.float32
BF16 = jnp.bfloat16
MESH = pl.DeviceIdType.MESH

D_MODEL = 2048
LN_EPS = 1e-5
GLA_HEADS = 4
GLA_DV = 256
GLA_DK = 128
GLA_RANK = 16
GLA_TAU = 16.0
GLA_CHUNK = 64
DIL_HD = 128
DIL_HEADS = 8
DIL_BAND = 128
DIL_DILATIONS = (1, 4, 16)
ROPE_THETA = 500000.0
ROPE_DIMS = 32
CA_HEADS = 4
CA_HD = 512
D_FF = 5504
ALPHA = 2.0 ** 0.25
ADAM_LR = 0.001
ADAM_B1 = 0.9
ADAM_B2 = 0.999
ADAM_EPS = 1e-08
ADAM_WD = 0.01
ADAM_STEP = 10

LANES = 128
SUBLANES = 8
VMEM_LIMIT = 56 * 1024 * 1024

GLA_W = 2 * GLA_HEADS * GLA_DK + 2 * GLA_HEADS * GLA_DV
HA_W = GLA_W + LANES
HB_W = 3 * DIL_HEADS * DIL_HD
FFP = 5632
NEG = -1e30


def _params(sem):
    return pltpu.CompilerParams(dimension_semantics=sem, vmem_limit_bytes=VMEM_LIMIT)


def _sigmoid(x):
    return 1.0 / (1.0 + jnp.exp(-x))


def _dot(a, b, dn, precision=None):
    return lax.dot_general(a, b, (dn, ((), ())), preferred_element_type=F32, precision=precision)


NN = ((1,), (0,))
NT = ((1,), (1,))
TN = ((0,), (0,))


def _bf(v):
    return v if v.dtype == BF16 else v.astype(BF16)


def _matmul(a, b, kind, out_dtype, tm, tn, tk, name, resid=None, resid_scale=1.0, b_k_off=0, b_slabs=False,
            out_slabs=False, also_bf16=False, dep=None):
    if b_slabs:
        assert kind != "nt" and b.shape[2] == tn
        k2, n = b.shape[1], b.shape[0] * tn
    elif kind == "nt":
        n, k2 = b.shape
    else:
        k2, n = b.shape
    (k, m) = a.shape if kind == "tn" else a.shape[::-1]
    assert k2 >= k and (k2 == k or not b_slabs) and m % tm == 0 and n % tn == 0 and k % tk == 0, \
        (name, a.shape, b.shape, tm, tn, tk)
    nk = k // tk
    dn = {"nn": NN, "nt": NT, "tn": TN}[kind]
    a_spec = pl.BlockSpec((tk, tm), lambda i, j, kk: (kk, i)) if kind == "tn" else pl.BlockSpec((tm, tk), lambda i, j, kk: (i, kk))
    if b_slabs:
        b_spec = pl.BlockSpec((None, tk, tn), lambda i, j, kk: (j, kk, 0))
    elif kind == "nt":
        b_spec = pl.BlockSpec((tn, tk), lambda i, j, kk: (j, kk + b_k_off))
    else:
        b_spec = pl.BlockSpec((tk, tn), lambda i, j, kk: (kk + b_k_off, j))
    if out_slabs:
        o_spec = pl.BlockSpec((None, tm, tn), lambda i, j, kk: (j, i, 0))
        o_shape = (n // tn, m, tn)
    else:
        o_spec = pl.BlockSpec((tm, tn), lambda i, j, kk: (i, j))
        o_shape = (m, n)
    has_resid = resid is not None

    n_in = 2 + int(has_resid) + int(dep is not None)

    def body(*refs):
        a_ref, b_ref = refs[:2]
        r_ref = refs[2] if has_resid else None
        o_ref = refs[n_in]
        ob_ref = refs[n_in + 1] if also_bf16 else None
        part = _dot(_bf(a_ref[...]), _bf(b_ref[...]), dn)

        def finish(acc):
            if has_resid:
                acc = acc + resid_scale * r_ref[...].astype(F32)
            o_ref[...] = acc.astype(out_dtype)
            if also_bf16:
                ob_ref[...] = acc.astype(BF16)

        if nk == 1:
            finish(part)
        else:
            acc_ref = refs[-1]
            kk = pl.program_id(2)

            @pl.when(kk == 0)
            def _():
                acc_ref[...] = part

            @pl.when(kk > 0)
            def _():
                acc_ref[...] += part

            @pl.when(kk == nk - 1)
            def _():
                finish(acc_ref[...])

    in_specs = [a_spec, b_spec] + ([o_spec] if has_resid else [])
    args = (a, b) + ((resid,) if has_resid else ())
    if dep is not None:
        in_specs.append(pl.BlockSpec((SUBLANES, LANES), lambda i, j, kk: (0, 0)))
        args += (dep,)
    o_struct = jax.ShapeDtypeStruct(o_shape, out_dtype)
    return pl.pallas_call(
        body, name=name, out_shape=[o_struct, jax.ShapeDtypeStruct(o_shape, BF16)] if also_bf16 else o_struct,
        grid=(m // tm, n // tn, nk), in_specs=in_specs, out_specs=[o_spec, o_spec] if also_bf16 else o_spec,
        scratch_shapes=[pltpu.VMEM((tm, tn), F32)] if nk > 1 else [],
        compiler_params=_params(("parallel", "parallel", "arbitrary")),
    )(*args)


def _ln_core(xres, f):
    p = ALPHA * xres + f
    mu = jnp.mean(p, axis=-1, keepdims=True)
    xc = p - mu
    var = jnp.mean(xc * xc, axis=-1, keepdims=True)
    rstd = lax.rsqrt(var + LN_EPS)
    return xc * rstd, rstd


def _rows8(v):
    r, c = v.shape
    return jnp.sum(v.reshape(r // SUBLANES, SUBLANES, c), axis=0)


def _ln_fwd(xres, f, g, b, name, transposed, tr=256):
    t, d = xres.shape
    row = pl.BlockSpec((tr, d), lambda i: (i, 0))
    vec = pl.BlockSpec((1, d), lambda i: (0, 0))

    def body(x_ref, f_ref, g_ref, b_ref, y_ref, yb_ref, *yt_ref):
        xhat, _ = _ln_core(x_ref[...], f_ref[...])
        y = xhat * g_ref[...] + b_ref[...]
        y_ref[...] = y
        yb = y.astype(BF16)
        yb_ref[...] = yb
        if transposed:
            yt_ref[0][...] = yb.T

    out_specs = [row, row] + ([pl.BlockSpec((d, tr), lambda i: (0, i))] if transposed else [])
    out_shape = [jax.ShapeDtypeStruct((t, d), F32), jax.ShapeDtypeStruct((t, d), BF16)] \
        + ([jax.ShapeDtypeStruct((d, t), BF16)] if transposed else [])
    return pl.pallas_call(
        body, name=name, grid=(t // tr,), in_specs=[row, row, vec, vec], out_specs=out_specs, out_shape=out_shape,
        compiler_params=_params(("parallel",)),
    )(xres, f, g, b)


def _ln_bwd(xres, f, g, b, dy_or_target, loss_head, name, tr=256):
    t, d = xres.shape
    row = pl.BlockSpec((tr, d), lambda i: (i, 0))
    vec = pl.BlockSpec((1, d), lambda i: (0, 0))
    acc = pl.BlockSpec((SUBLANES, d), lambda i: (0, 0))
    lacc = pl.BlockSpec((SUBLANES, LANES), lambda i: (0, 0))

    def body(x_ref, f_ref, g_ref, b_ref, t_ref, dp_ref, dpb_ref, dg_ref, db_ref, *rest):
        i = pl.program_id(0)
        xhat, rstd = _ln_core(x_ref[...], f_ref[...])
        if loss_head:
            err = xhat * g_ref[...] + b_ref[...] - t_ref[...]
            dy = err * (1.0 / d)
            sq = err * err
            lanes = sq[:, :LANES]
            for kk in range(1, d // LANES):
                lanes = lanes + sq[:, kk * LANES:(kk + 1) * LANES]
            lpart = _rows8(lanes) * (0.5 / d)
        else:
            dy = t_ref[...]
        dxh = dy * g_ref[...]
        m1 = jnp.mean(dxh, axis=-1, keepdims=True)
        m2 = jnp.mean(dxh * xhat, axis=-1, keepdims=True)
        dp = rstd * (dxh - m1 - xhat * m2)
        dp_ref[...] = dp
        dpb_ref[...] = dp.astype(BF16)
        dgp = _rows8(dy * xhat)
        dbp = _rows8(dy)

        @pl.when(i == 0)
        def _():
            dg_ref[...] = dgp
            db_ref[...] = dbp
            if loss_head:
                rest[0][...] = lpart

        @pl.when(i > 0)
        def _():
            dg_ref[...] += dgp
            db_ref[...] += dbp
            if loss_head:
                rest[0][...] += lpart

    out_shape = [jax.ShapeDtypeStruct((t, d), F32), jax.ShapeDtypeStruct((t, d), BF16),
                 jax.ShapeDtypeStruct((SUBLANES, d), F32), jax.ShapeDtypeStruct((SUBLANES, d), F32)]
    out_specs = [row, row, acc, acc]
    if loss_head:
        out_shape.append(jax.ShapeDtypeStruct((SUBLANES, LANES), F32))
        out_specs.append(lacc)
    return pl.pallas_call(
        body, name=name, grid=(t // tr,), in_specs=[row, row, vec, vec, row], out_specs=out_specs,
        out_shape=out_shape, compiler_params=_params(("arbitrary",)),
    )(xres, f, g, b, dy_or_target)


def _gla_gates(glr, w2, gb):
    z = _dot(_bf(glr), w2, NN) + gb
    lg = (jnp.minimum(z, 0.0) - jnp.log(1.0 + jnp.exp(-jnp.abs(z)))) * (1.0 / GLA_TAU)
    c = z.shape[0]
    ri = lax.broadcasted_iota(jnp.int32, (c, c), 0)
    ci = lax.broadcasted_iota(jnp.int32, (c, c), 1)
    tri = (ci <= ri).astype(F32)
    bcum = _dot(tri, lg, NN, precision=lax.Precision.HIGHEST)
    blast = jnp.sum(lg, axis=0, keepdims=True)
    return z, bcum, blast, tri


def _gla_specs(t):
    c = GLA_CHUNK
    return c, t // c


def _gla_fwd(h_a, w2p, gate_b, norm_g):
    t = h_a.shape[0]
    c, n = _gla_specs(t)
    hk, hv = GLA_HEADS * GLA_DK, GLA_HEADS * GLA_DV
    scale = GLA_DK ** -0.5

    def body(q_ref, k_ref, v_ref, r_ref, glr_ref, w2_ref, gb_ref, ng_ref, og_ref, oraw_ref, sb_ref, st_ref):
        i = pl.program_id(0)

        @pl.when(i == 0)
        def _():
            st_ref[...] = jnp.zeros_like(st_ref)

        _, bcum, blast, _ = _gla_gates(glr_ref[...], w2_ref[...], gb_ref[...])
        ri = lax.broadcasted_iota(jnp.int32, (c, c), 0)
        ci = lax.broadcasted_iota(jnp.int32, (c, c), 1)
        causal = ci <= ri
        for h in range(GLA_HEADS):
            ks = slice(h * GLA_DK, (h + 1) * GLA_DK)
            vs = slice(h * GLA_DV, (h + 1) * GLA_DV)
            b_h, bl_h = bcum[:, ks], blast[:, ks]
            q_h, k_h = q_ref[:, ks], k_ref[:, ks]
            v_h = _bf(v_ref[:, vs])
            qi = _bf(q_h * scale * jnp.exp(b_h))
            ki = _bf(k_h * jnp.exp(-b_h))
            ke = _bf(k_h * jnp.exp(bl_h - b_h))
            st = st_ref[h]
            sb_ref[0, h] = st
            a = jnp.where(causal, _dot(qi, ki, NT), 0.0)
            o = _dot(_bf(a), v_h, NN) + _dot(qi, _bf(st), NT)
            st_ref[h] = st * jnp.exp(bl_h) + _dot(v_h, ke, TN)
            oraw_ref[:, vs] = o
            mu = jnp.mean(o, axis=-1, keepdims=True)
            oc = o - mu
            var = jnp.mean(oc * oc, axis=-1, keepdims=True)
            xh = oc * lax.rsqrt(var + LN_EPS)
            r_h = r_ref[:, vs]
            og_ref[:, vs] = (xh * ng_ref[:, vs] * (r_h * _sigmoid(r_h))).astype(BF16)

    return pl.pallas_call(
        body, name="gla_fwd", grid=(n,),
        in_specs=[pl.BlockSpec((c, hk), lambda i: (i, 0)), pl.BlockSpec((c, hk), lambda i: (i, 1)),
                  pl.BlockSpec((c, hv), lambda i: (i, 1)), pl.BlockSpec((c, hv), lambda i: (i, 2)),
                  pl.BlockSpec((c, LANES), lambda i: (i, GLA_W // LANES)),
                  pl.BlockSpec((LANES, hk), lambda i: (0, 0)), pl.BlockSpec((1, hk), lambda i: (0, 0)),
                  pl.BlockSpec((1, hv), lambda i: (0, 0))],
        out_specs=[pl.BlockSpec((c, hv), lambda i: (i, 0)), pl.BlockSpec((c, hv), lambda i: (i, 0)),
                   pl.BlockSpec((1, GLA_HEADS, GLA_DV, GLA_DK), lambda i: (i, 0, 0, 0))],
        out_shape=[jax.ShapeDtypeStruct((t, hv), BF16), jax.ShapeDtypeStruct((t, hv), F32),
                   jax.ShapeDtypeStruct((n, GLA_HEADS, GLA_DV, GLA_DK), F32)],
        scratch_shapes=[pltpu.VMEM((GLA_HEADS, GLA_DV, GLA_DK), F32)],
        compiler_params=_params(("arbitrary",)),
    )(h_a, h_a, h_a, h_a, h_a, w2p, gate_b, norm_g)


def _gla_bwd(h_a, w2p, gate_b, norm_g, o_raw, s_before, dmix):
    t = h_a.shape[0]
    c, n = _gla_specs(t)
    hk, hv = GLA_HEADS * GLA_DK, GLA_HEADS * GLA_DV
    scale = GLA_DK ** -0.5
    rev = lambda i: n - 1 - i

    def body(q_ref, k_ref, v_ref, r_ref, glr_ref, w2_ref, gb_ref, ng_ref, oraw_ref, sb_ref, do_ref,
             dh_ref, dw2_ref, dgb_ref, dng_ref, dst_ref):
        i = pl.program_id(0)

        @pl.when(i == 0)
        def _():
            dst_ref[...] = jnp.zeros_like(dst_ref)

        glr = glr_ref[...]
        z, bcum, blast, tri = _gla_gates(glr, w2_ref[...], gb_ref[...])
        ri = lax.broadcasted_iota(jnp.int32, (c, c), 0)
        ci = lax.broadcasted_iota(jnp.int32, (c, c), 1)
        causal = ci <= ri
        dlg_parts = []
        dng_parts = []
        for h in range(GLA_HEADS):
            ks = slice(h * GLA_DK, (h + 1) * GLA_DK)
            vs = slice(h * GLA_DV, (h + 1) * GLA_DV)
            o = oraw_ref[:, vs]
            mu = jnp.mean(o, axis=-1, keepdims=True)
            oc = o - mu
            var = jnp.mean(oc * oc, axis=-1, keepdims=True)
            rstd = lax.rsqrt(var + LN_EPS)
            xh = oc * rstd
            r_h = r_ref[:, vs]
            sg = _sigmoid(r_h)
            silu = r_h * sg
            dout = do_ref[:, vs]
            ng = ng_ref[:, vs]
            dng_parts.append(_rows8(dout * xh * silu))
            dr = dout * xh * ng * (sg * (1.0 + r_h * (1.0 - sg)))
            dxh = dout * ng * silu
            m1 = jnp.mean(dxh, axis=-1, keepdims=True)
            m2 = jnp.mean(dxh * xh, axis=-1, keepdims=True)
            do_raw = _bf(rstd * (dxh - m1 - xh * m2))
            b_h, bl_h = bcum[:, ks], blast[:, ks]
            q_h, k_h = q_ref[:, ks], k_ref[:, ks]
            v_h = _bf(v_ref[:, vs])
            eb, enb, eend = jnp.exp(b_h), jnp.exp(-b_h), jnp.exp(bl_h - b_h)
            decay = jnp.exp(bl_h)
            qi_f, ki_f, ke_f = q_h * scale * eb, k_h * enb, k_h * eend
            qi, ki, ke = _bf(qi_f), _bf(ki_f), _bf(ke_f)
            st = sb_ref[0, h]
            dst = dst_ref[h]
            dst_b = _bf(dst)
            a = _bf(jnp.where(causal, _dot(qi, ki, NT), 0.0))
            da = _bf(jnp.where(causal, _dot(do_raw, v_h, NT), 0.0))
            dv = _dot(a, do_raw, TN) + _dot(ke, dst_b, NT)
            dqi = _dot(da, ki, NN) + _dot(do_raw, _bf(st), NN)
            dki = _dot(da, qi, TN)
            dke = _dot(v_h, dst_b, NN)
            dst_ref[h] = _dot(do_raw, qi, TN) + dst * decay
            dbl = decay * jnp.sum(st * dst, axis=0, keepdims=True) + jnp.sum(dke * ke_f, axis=0, keepdims=True)
            dbc = dqi * qi_f - dki * ki_f - dke * ke_f
            dlg_parts.append(_dot(tri, dbc, TN, precision=lax.Precision.HIGHEST) + dbl)
            dh_ref[:, ks] = (dqi * eb * scale).astype(BF16)
            dh_ref[:, hk + h * GLA_DK: hk + (h + 1) * GLA_DK] = (dki * enb + dke * eend).astype(BF16)
            dh_ref[:, 2 * hk + h * GLA_DV: 2 * hk + (h + 1) * GLA_DV] = dv.astype(BF16)
            dh_ref[:, 2 * hk + hv + h * GLA_DV: 2 * hk + hv + (h + 1) * GLA_DV] = dr.astype(BF16)
        dlg = jnp.concatenate(dlg_parts, axis=1)
        dz = dlg * (1.0 / GLA_TAU) * _sigmoid(-z)
        dz_b = _bf(dz)
        dh_ref[:, GLA_W:] = _dot(dz_b, w2_ref[...], NT).astype(BF16)
        dw2p = _dot(_bf(glr), dz_b, TN)
        dgbp = _rows8(dz)
        dngp = jnp.concatenate(dng_parts, axis=1)

        @pl.when(i == 0)
        def _():
            dw2_ref[...] = dw2p
            dgb_ref[...] = dgbp
            dng_ref[...] = dngp

        @pl.when(i > 0)
        def _():
            dw2_ref[...] += dw2p
            dgb_ref[...] += dgbp
            dng_ref[...] += dngp

    return pl.pallas_call(
        body, name="gla_bwd", grid=(n,),
        in_specs=[pl.BlockSpec((c, hk), lambda i: (rev(i), 0)), pl.BlockSpec((c, hk), lambda i: (rev(i), 1)),
                  pl.BlockSpec((c, hv), lambda i: (rev(i), 1)), pl.BlockSpec((c, hv), lambda i: (rev(i), 2)),
                  pl.BlockSpec((c, LANES), lambda i: (rev(i), GLA_W // LANES)),
                  pl.BlockSpec((LANES, hk), lambda i: (0, 0)), pl.BlockSpec((1, hk), lambda i: (0, 0)),
                  pl.BlockSpec((1, hv), lambda i: (0, 0)),
                  pl.BlockSpec((c, hv), lambda i: (rev(i), 0)),
                  pl.BlockSpec((1, GLA_HEADS, GLA_DV, GLA_DK), lambda i: (rev(i), 0, 0, 0)),
                  pl.BlockSpec((c, hv), lambda i: (rev(i), 0))],
        out_specs=[pl.BlockSpec((c, HA_W), lambda i: (rev(i), 0)),
                   pl.BlockSpec((LANES, hk), lambda i: (0, 0)),
                   pl.BlockSpec((SUBLANES, hk), lambda i: (0, 0)),
                   pl.BlockSpec((SUBLANES, hv), lambda i: (0, 0))],
        out_shape=[jax.ShapeDtypeStruct((t, HA_W), BF16), jax.ShapeDtypeStruct((LANES, hk), F32),
                   jax.ShapeDtypeStruct((SUBLANES, hk), F32), jax.ShapeDtypeStruct((SUBLANES, hv), F32)],
        scratch_shapes=[pltpu.VMEM((GLA_HEADS, GLA_DV, GLA_DK), F32)],
        compiler_params=_params(("arbitrary",)),
    )(h_a, h_a, h_a, h_a, h_a, w2p, gate_b, norm_g, o_raw, s_before, dmix)


def _rope_tables(positions):
    half = ROPE_DIMS // 2
    inv_freq = ROPE_THETA ** (-jnp.arange(0, ROPE_DIMS, 2, dtype=F32) / ROPE_DIMS)
    ang = positions.astype(F32).reshape(-1, 1) * inv_freq
    cos, sin = jnp.cos(ang), jnp.sin(ang)
    t = cos.shape[0]
    one = jnp.ones((t, DIL_HD - ROPE_DIMS), F32)
    zero = jnp.zeros((t, DIL_HD - ROPE_DIMS), F32)
    zh = jnp.zeros((t, half), F32)
    return (jnp.concatenate([cos, cos, one], axis=1), jnp.concatenate([-sin, zh, zero], axis=1),
            jnp.concatenate([zh, sin, zero], axis=1))


def _rope_apply(x, c, s1, s2):
    half = ROPE_DIMS // 2
    return x * c + pltpu.roll(x, DIL_HD - half, 1) * s1 + pltpu.roll(x, half, 1) * s2


def _rope_apply_t(dy, c, s1, s2):
    half = ROPE_DIMS // 2
    return dy * c + pltpu.roll(dy * s1, half, 1) + pltpu.roll(dy * s2, DIL_HD - half, 1)


def _rope_fwd(h_b, tabs, tr=256):
    t = h_b.shape[0]
    w = DIL_HEADS * DIL_HD
    scale = DIL_HD ** -0.5
    tab = pl.BlockSpec((tr, DIL_HD), lambda i: (i, 0))
    outb = pl.BlockSpec((tr, w), lambda i: (i, 0))

    def body(q_ref, k_ref, c_ref, s1_ref, s2_ref, qo_ref, ko_ref):
        c, s1, s2 = c_ref[...], s1_ref[...], s2_ref[...]
        for h in range(DIL_HEADS):
            hs = slice(h * DIL_HD, (h + 1) * DIL_HD)
            qo_ref[:, hs] = _rope_apply(q_ref[:, hs] * scale, c, s1, s2)
            ko_ref[:, hs] = _rope_apply(k_ref[:, hs], c, s1, s2)

    return pl.pallas_call(
        body, name="rope_fwd", grid=(t // tr,),
        in_specs=[pl.BlockSpec((tr, w), lambda i: (i, 0)), pl.BlockSpec((tr, w), lambda i: (i, 1)), tab, tab, tab],
        out_specs=[outb, outb],
        out_shape=[jax.ShapeDtypeStruct((t, w), F32)] * 2,
        compiler_params=_params(("parallel",)),
    )(h_b, h_b, *tabs)


def _dil_dh(dq, dk, dv, tabs, tr=256):
    t, w = dq.shape
    scale = DIL_HD ** -0.5
    tab = pl.BlockSpec((tr, DIL_HD), lambda i: (i, 0))
    inb = pl.BlockSpec((tr, w), lambda i: (i, 0))

    def body(dq_ref, dk_ref, dv_ref, c_ref, s1_ref, s2_ref, o_ref):
        c, s1, s2 = c_ref[...], s1_ref[...], s2_ref[...]
        for h in range(DIL_HEADS):
            hs = slice(h * DIL_HD, (h + 1) * DIL_HD)
            o_ref[:, h * DIL_HD:(h + 1) * DIL_HD] = (_rope_apply_t(dq_ref[:, hs], c, s1, s2) * scale).astype(BF16)
            o_ref[:, w + h * DIL_HD: w + (h + 1) * DIL_HD] = _rope_apply_t(dk_ref[:, hs], c, s1, s2).astype(BF16)
        o_ref[:, 2 * w:] = dv_ref[...].astype(BF16)

    return pl.pallas_call(
        body, name="dil_dh", grid=(t // tr,), in_specs=[inb] * 3 + [tab] * 3,
        out_specs=pl.BlockSpec((tr, 3 * w), lambda i: (i, 0)),
        out_shape=jax.ShapeDtypeStruct((t, 3 * w), BF16), compiler_params=_params(("parallel",)),
    )(dq, dk, dv, *tabs)


def _band_masks(not_first):
    r = lax.broadcasted_iota(jnp.int32, (DIL_BAND, 2 * DIL_BAND), 0)
    c = lax.broadcasted_iota(jnp.int32, (DIL_BAND, 2 * DIL_BAND), 1)
    nf = jnp.full((DIL_BAND, 2 * DIL_BAND), not_first, jnp.int32)
    look_back = jnp.logical_and(jnp.logical_and(c < DIL_BAND, c >= r), nf > 0)
    own_band = jnp.logical_and(c >= DIL_BAND, (c - DIL_BAND) <= r)
    return jnp.logical_or(look_back, own_band)


def _gather_rows(dst_ref, src_ref, t, d, cast=None):
    n = t // d
    for r in range(d):
        v = src_ref[pl.ds(r, n, stride=d), :] if d > 1 else src_ref[...]
        dst_ref[r * n:(r + 1) * n, :] = v if cast is None else v.astype(cast)


def _tri_mask():
    r = lax.broadcasted_iota(jnp.int32, (DIL_BAND, DIL_BAND), 0)
    c = lax.broadcasted_iota(jnp.int32, (DIL_BAND, DIL_BAND), 1)
    return c <= r


def _dil_fwd_all(qr, kr, h_b):
    t = qr.shape[0]
    nbands = t // DIL_BAND
    nbr = len(DIL_DILATIONS)
    hoff = DIL_HEADS

    def col(off):
        return pl.BlockSpec((t, DIL_HD), lambda h: (0, off + h), pipeline_mode=pl.Buffered(1))

    outb = pl.BlockSpec((t, DIL_HD), lambda h: (0, h))

    def body(q_ref, k_ref, v_ref, ob_ref, of_ref, lt_ref, qs, ks, vs, os_, ls_, *br):
        obr, lbr = br[:nbr], br[nbr:]
        for bi, d in enumerate(DIL_DILATIONS):
            n = t // d
            nb = n // DIL_BAND
            _gather_rows(qs, q_ref, t, d, BF16)
            _gather_rows(ks, k_ref, t, d, BF16)
            _gather_rows(vs, v_ref, t, d, BF16)
            s = jnp.where(_tri_mask(), _dot(qs[0:DIL_BAND, :], ks[0:DIL_BAND, :], NT), NEG)
            m = jnp.max(s, axis=-1, keepdims=True)
            pr = jnp.exp(s - m)
            den = jnp.sum(pr, axis=-1, keepdims=True)
            os_[0:DIL_BAND, :] = _dot(_bf(pr), vs[0:DIL_BAND, :], NN) / den
            ls_[0:DIL_BAND, :] = jnp.broadcast_to(m + jnp.log(den), (DIL_BAND, DIL_HD))

            def band(b, carry, nb=nb):
                st = pl.multiple_of((b - 1) * DIL_BAND, DIL_BAND)
                cur = pl.ds(st + DIL_BAND, DIL_BAND)
                both = pl.ds(st, 2 * DIL_BAND)
                not_first = ((b % nb) != 0).astype(jnp.int32)
                s = jnp.where(_band_masks(not_first), _dot(qs[cur, :], ks[both, :], NT), NEG)
                m = jnp.max(s, axis=-1, keepdims=True)
                pr = jnp.exp(s - m)
                den = jnp.sum(pr, axis=-1, keepdims=True)
                os_[cur, :] = _dot(_bf(pr), vs[both, :], NN) / den
                ls_[cur, :] = jnp.broadcast_to(m + jnp.log(den), (DIL_BAND, DIL_HD))
                return carry

            lax.fori_loop(1, nbands, band, 0, unroll=8)
            for r in range(d):
                dst = pl.ds(r, n, stride=d) if d > 1 else slice(None)
                obr[bi][dst, :] = os_[r * n:(r + 1) * n, :]
                lbr[bi][dst, :] = ls_[r * n:(r + 1) * n, :]
        rows = 512
        for c0 in range(0, t, rows):
            sl = slice(c0, c0 + rows)
            la, lb, lc = lbr[0][sl, :], lbr[1][sl, :], lbr[2][sl, :]
            m = jnp.maximum(jnp.maximum(la, lb), lc)
            ea, eb, ec = jnp.exp(la - m), jnp.exp(lb - m), jnp.exp(lc - m)
            den = ea + eb + ec
            o = (ea * obr[0][sl, :] + eb * obr[1][sl, :] + ec * obr[2][sl, :]) / den
            ob_ref[sl, :] = o.astype(BF16)
            of_ref[sl, :] = o
            lt_ref[sl, :] = m + jnp.log(den)

    w = DIL_HEADS * DIL_HD
    vm = lambda dt: pltpu.VMEM((t, DIL_HD), dt)
    return pl.pallas_call(
        body, name="dil_fwd", grid=(DIL_HEADS,), in_specs=[col(0), col(0), col(2 * hoff)],
        out_specs=[outb, outb, outb],
        out_shape=[jax.ShapeDtypeStruct((t, w), BF16), jax.ShapeDtypeStruct((t, w), F32),
                   jax.ShapeDtypeStruct((t, w), F32)],
        scratch_shapes=[vm(BF16)] * 3 + [vm(F32)] * 2 + [vm(F32)] * (2 * nbr),
        compiler_params=_params(("parallel",)),
    )(qr, kr, h_b)


def _dil_bwd_all(qr, kr, h_b, dmix, o_d, lse_tot):
    t = qr.shape[0]
    nbands = t // DIL_BAND
    hoff = DIL_HEADS

    def col(off):
        return pl.BlockSpec((t, DIL_HD), lambda h: (0, off + h), pipeline_mode=pl.Buffered(1))

    outb = pl.BlockSpec((t, DIL_HD), lambda h: (0, h))

    def body(q_ref, k_ref, v_ref, do_ref, o_ref, l_ref, dq_ref, dk_ref, dv_ref,
             qs, ks, vs, dos, lss, dds, dqs, acck, accv, ddt):
        rows = 512
        for c0 in range(0, t, rows):
            prod = do_ref[c0:c0 + rows, :] * o_ref[c0:c0 + rows, :]
            ddt[c0:c0 + rows, :] = jnp.broadcast_to(jnp.sum(prod, axis=-1, keepdims=True), (rows, DIL_HD))
        for bi, d in enumerate(DIL_DILATIONS):
            n = t // d
            nb = n // DIL_BAND
            _gather_rows(qs, q_ref, t, d, BF16)
            _gather_rows(ks, k_ref, t, d, BF16)
            _gather_rows(vs, v_ref, t, d, BF16)
            _gather_rows(dos, do_ref, t, d, BF16)
            _gather_rows(lss, l_ref, t, d)
            _gather_rows(dds, ddt, t, d)
            acck[...] = jnp.zeros_like(acck)
            accv[...] = jnp.zeros_like(accv)
            b0 = slice(0, DIL_BAND)
            s = jnp.where(_tri_mask(), _dot(qs[b0, :], ks[b0, :], NT), NEG)
            pr = jnp.exp(s - lss[b0, :])
            ds = _bf(pr * (_dot(dos[b0, :], vs[b0, :], NT) - dds[b0, :]))
            dqs[b0, :] = _dot(ds, ks[b0, :], NN)
            acck[DIL_BAND:2 * DIL_BAND, :] += _dot(ds, qs[b0, :], TN)
            accv[DIL_BAND:2 * DIL_BAND, :] += _dot(_bf(pr), dos[b0, :], TN)

            def band(b, carry, nb=nb):
                st = pl.multiple_of((b - 1) * DIL_BAND, DIL_BAND)
                cur = pl.ds(st + DIL_BAND, DIL_BAND)
                both = pl.ds(st, 2 * DIL_BAND)
                acc_rows = pl.ds(st + DIL_BAND, 2 * DIL_BAND)
                not_first = ((b % nb) != 0).astype(jnp.int32)
                qb, dob, lb, ddb = qs[cur, :], dos[cur, :], lss[cur, :], dds[cur, :]
                kcat, vcat = ks[both, :], vs[both, :]
                s = jnp.where(_band_masks(not_first), _dot(qb, kcat, NT), NEG)
                pr = jnp.exp(s - jnp.concatenate([lb, lb], axis=1))
                ds = _bf(pr * (_dot(dob, vcat, NT) - jnp.concatenate([ddb, ddb], axis=1)))
                dqs[cur, :] = _dot(ds, kcat, NN)
                acck[acc_rows, :] += _dot(ds, qb, TN)
                accv[acc_rows, :] += _dot(_bf(pr), dob, TN)
                return carry

            lax.fori_loop(1, nbands, band, 0, unroll=4)
            for r in range(d):
                lo = r * n
                if d == 1:
                    dq_ref[...] = dqs[...]
                    dk_ref[...] = acck[DIL_BAND:DIL_BAND + t, :]
                    dv_ref[...] = accv[DIL_BAND:DIL_BAND + t, :]
                else:
                    dst = pl.ds(r, n, stride=d)
                    dq_ref[dst, :] = dq_ref[dst, :] + dqs[lo:lo + n, :]
                    dk_ref[dst, :] = dk_ref[dst, :] + acck[DIL_BAND + lo:DIL_BAND + lo + n, :]
                    dv_ref[dst, :] = dv_ref[dst, :] + accv[DIL_BAND + lo:DIL_BAND + lo + n, :]

    w = DIL_HEADS * DIL_HD
    vm = lambda dt, extra=0: pltpu.VMEM((t + extra, DIL_HD), dt)
    return pl.pallas_call(
        body, name="dil_bwd", grid=(DIL_HEADS,),
        in_specs=[col(0), col(0), col(2 * hoff), col(hoff), col(0), col(0)], out_specs=[outb] * 3,
        out_shape=[jax.ShapeDtypeStruct((t, w), F32)] * 3,
        scratch_shapes=[vm(BF16)] * 4 + [vm(F32)] * 3 + [vm(F32, DIL_BAND)] * 2 + [vm(F32)],
        compiler_params=_params(("parallel",)),
    )(qr, kr, h_b, dmix, o_d, lse_tot)


def _ca_fwd(q, memkv, tq=512):
    t, d = q.shape
    m = memkv.shape[0]
    scale = CA_HD ** -0.5

    def body(q_ref, k_ref, v_ref, o_ref, ot_ref):
        for h in range(CA_HEADS):
            hs = slice(h * CA_HD, (h + 1) * CA_HD)
            s = _dot(q_ref[:, hs], k_ref[:, hs], NT) * scale
            p = jnp.exp(s - jnp.max(s, axis=-1, keepdims=True))
            p = p / jnp.sum(p, axis=-1, keepdims=True)
            o = _dot(_bf(p), v_ref[:, hs], NN).astype(BF16)
            o_ref[:, hs] = o
            ot_ref[hs, :] = o.T

    return pl.pallas_call(
        body, name="ca_fwd", grid=(t // tq,),
        in_specs=[pl.BlockSpec((tq, d), lambda i: (i, 0)), pl.BlockSpec((m, d), lambda i: (0, 0)),
                  pl.BlockSpec((m, d), lambda i: (0, 1))],
        out_specs=[pl.BlockSpec((tq, d), lambda i: (i, 0)), pl.BlockSpec((d, tq), lambda i: (0, i))],
        out_shape=[jax.ShapeDtypeStruct((t, d), BF16), jax.ShapeDtypeStruct((d, t), BF16)],
        compiler_params=_params(("parallel",)),
    )(q, memkv, memkv)


def _ca_bwd(q, memkv, do, tq=512):
    t, d = q.shape
    m = memkv.shape[0]
    scale = CA_HD ** -0.5

    def body(q_ref, k_ref, v_ref, do_ref, dq_ref, dkv_ref):
        i = pl.program_id(0)

        @pl.when(i == 0)
        def _():
            dkv_ref[...] = jnp.zeros_like(dkv_ref)

        for h in range(CA_HEADS):
            hs = slice(h * CA_HD, (h + 1) * CA_HD)
            q_h, k_h, v_h, do_h = q_ref[:, hs], k_ref[:, hs], v_ref[:, hs], do_ref[:, hs]
            s = _dot(q_h, k_h, NT) * scale
            p = jnp.exp(s - jnp.max(s, axis=-1, keepdims=True))
            p = p / jnp.sum(p, axis=-1, keepdims=True)
            dp = _dot(do_h, v_h, NT)
            ds = _bf(p * (dp - jnp.sum(p * dp, axis=-1, keepdims=True)) * scale)
            dq_ref[:, hs] = _dot(ds, k_h, NN).astype(BF16)
            dkv_ref[:, hs] += _dot(ds, q_h, TN)
            dkv_ref[:, d + h * CA_HD: d + (h + 1) * CA_HD] += _dot(_bf(p), do_h, TN)

    return pl.pallas_call(
        body, name="ca_bwd", grid=(t // tq,),
        in_specs=[pl.BlockSpec((tq, d), lambda i: (i, 0)), pl.BlockSpec((m, d), lambda i: (0, 0)),
                  pl.BlockSpec((m, d), lambda i: (0, 1)), pl.BlockSpec((tq, d), lambda i: (i, 0))],
        out_specs=[pl.BlockSpec((tq, d), lambda i: (i, 0)), pl.BlockSpec((m, 2 * d), lambda i: (0, 0))],
        out_shape=[jax.ShapeDtypeStruct((t, d), BF16), jax.ShapeDtypeStruct((m, 2 * d), F32)],
        compiler_params=_params(("arbitrary",)),
    )(q, memkv, memkv, do)


STRIP = 256


def _shift_down(u, n, row):
    return jnp.where(row >= n, pltpu.roll(u, n, 0), 0.0)


def _shift_up(u, n, row):
    t = u.shape[0]
    return jnp.where(row < t - n, pltpu.roll(u, t - n, 0), 0.0)


def _conv(u, cw_ref, row):
    return ((cw_ref[3:4, :] + cw_ref[0:1, :] * _shift_down(u, 2, row)) + cw_ref[1:2, :] * _shift_down(u, 1, row)) \
        + cw_ref[2:3, :] * u


def _swiglu_fwd(u0, cw):
    t, w = u0.shape[0], u0.shape[1] // 2
    ns = w // STRIP
    col = pl.BlockSpec((t, STRIP), lambda j: (0, j))
    col_up = pl.BlockSpec((t, STRIP), lambda j: (0, ns + j))
    cws = pl.BlockSpec((SUBLANES, STRIP), lambda j: (0, j))
    cws_up = pl.BlockSpec((SUBLANES, STRIP), lambda j: (0, ns + j))

    def body(g_ref, u_ref, cg_ref, cu_ref, a_ref, at_ref):
        row = lax.broadcasted_iota(jnp.int32, (t, STRIP), 0)
        gate = _conv(g_ref[...].astype(F32), cg_ref, row)
        up = _conv(u_ref[...].astype(F32), cu_ref, row)
        act = (gate * _sigmoid(gate) * up).astype(BF16)
        a_ref[...] = act
        at_ref[...] = act.T

    return pl.pallas_call(
        body, name="swiglu_fwd", grid=(ns,), in_specs=[col, col_up, cws, cws_up],
        out_specs=[col, pl.BlockSpec((STRIP, t), lambda j: (j, 0))],
        out_shape=[jax.ShapeDtypeStruct((t, w), BF16), jax.ShapeDtypeStruct((w, t), BF16)],
        compiler_params=_params(("parallel",)),
    )(u0, u0, cw, cw)


def _swiglu_bwd(u0, cw, da):
    t, w = u0.shape[0], u0.shape[1] // 2
    ns = w // STRIP
    col = pl.BlockSpec((t, STRIP), lambda j: (0, j))
    col_up = pl.BlockSpec((t, STRIP), lambda j: (0, ns + j))
    cws = pl.BlockSpec((SUBLANES, STRIP), lambda j: (0, j))
    cws_up = pl.BlockSpec((SUBLANES, STRIP), lambda j: (0, ns + j))

    def conv_bwd(du, u0, cw_ref, row, du0_ref, du0t_ref, dcw_ref):
        du1, du2 = _shift_up(du, 1, row), _shift_up(du, 2, row)
        du0 = ((cw_ref[2:3, :] * du + cw_ref[1:2, :] * du1) + cw_ref[0:1, :] * du2).astype(BF16)
        du0_ref[...] = du0
        du0t_ref[...] = du0.T
        dcw_ref[0:1, :] = jnp.sum(du2 * u0, axis=0, keepdims=True)
        dcw_ref[1:2, :] = jnp.sum(du1 * u0, axis=0, keepdims=True)
        dcw_ref[2:3, :] = jnp.sum(du * u0, axis=0, keepdims=True)
        dcw_ref[3:4, :] = jnp.sum(du, axis=0, keepdims=True)
        dcw_ref[4:8, :] = jnp.zeros((4, STRIP), F32)

    def body(g_ref, u_ref, cg_ref, cu_ref, da_ref, dg0_ref, du0_ref, dut_ref, dcg_ref, dcu_ref):
        row = lax.broadcasted_iota(jnp.int32, (t, STRIP), 0)
        g0, up0 = g_ref[...].astype(F32), u_ref[...].astype(F32)
        gate = _conv(g0, cg_ref, row)
        up = _conv(up0, cu_ref, row)
        sg = _sigmoid(gate)
        da = da_ref[...].astype(F32)
        dgate = da * up * (sg * (1.0 + gate * (1.0 - sg)))
        dup = da * (gate * sg)
        conv_bwd(dgate, g0, cg_ref, row, dg0_ref, dut_ref.at[0], dcg_ref)
        conv_bwd(dup, up0, cu_ref, row, du0_ref, dut_ref.at[1], dcu_ref)

    return pl.pallas_call(
        body, name="swiglu_bwd", grid=(ns,), in_specs=[col, col_up, cws, cws_up, col],
        out_specs=[col, col, pl.BlockSpec((2, STRIP, t), lambda j: (0, j, 0)), cws, cws],
        out_shape=[jax.ShapeDtypeStruct((t, w), BF16), jax.ShapeDtypeStruct((t, w), BF16),
                   jax.ShapeDtypeStruct((2, w, t), BF16),
                   jax.ShapeDtypeStruct((SUBLANES, w), F32), jax.ShapeDtypeStruct((SUBLANES, w), F32)],
        compiler_params=_params(("parallel",)),
    )(u0, u0, cw, cw, da)


def _ffn_win_grad(dut, x2b, tn=512):
    t, d = x2b.shape
    sp, sw = FF_SLAB_P, FF_SLAB

    def body(a_ref, b_ref, o_ref, ob_ref):
        res = _dot(a_ref[...], b_ref[...], NN)
        o_ref[...] = res[:sw, :]
        ob_ref[...] = res[:sw, :].astype(BF16)

    o_spec = pl.BlockSpec((None, sw, tn), lambda j, n: (j, 0, n))
    return pl.pallas_call(
        body, name="mm_g_ffn_in", grid=(8, d // tn),
        in_specs=[pl.BlockSpec((None, sp, t), lambda j, n: (j // 4, j % 4, 0)),
                  pl.BlockSpec((t, tn), lambda j, n: (0, n))],
        out_specs=[o_spec, o_spec],
        out_shape=[jax.ShapeDtypeStruct((8, sw, d), F32), jax.ShapeDtypeStruct((8, sw, d), BF16)],
        compiler_params=_params(("parallel", "parallel")),
    )(dut, x2b)


def _tile2d(r, c, limit=1 << 20):
    tr, tc = r, c
    while tr * tc * 4 > limit:
        if tr % (2 * SUBLANES) == 0:
            tr //= 2
        elif tc % (2 * LANES) == 0:
            tc //= 2
        else:
            break
    return tr, tc


def _adamw_math(w, m, v, g):
    c1 = 1.0 - ADAM_B1 ** ADAM_STEP
    c2 = 1.0 - ADAM_B2 ** ADAM_STEP
    mm = ADAM_B1 * m + (1.0 - ADAM_B1) * g
    vv = ADAM_B2 * v + (1.0 - ADAM_B2) * (g * g)
    delta = -ADAM_LR * ((mm / c1) / (jnp.sqrt(vv / c2) + ADAM_EPS) + ADAM_WD * w)
    return delta, mm, vv


def _adamw(w, m, v, g, name):
    r, c = w.shape
    blk = pl.BlockSpec((r, c), lambda i: (0, 0))

    def body(w_ref, m_ref, v_ref, gi_ref, g_ref, d_ref, nm_ref, nv_ref):
        g = gi_ref[...]
        d_ref[...], nm_ref[...], nv_ref[...] = _adamw_math(w_ref[...], m_ref[...], v_ref[...], g)
        g_ref[...] = g

    return pl.pallas_call(body, name=name, grid=(1,), in_specs=[blk] * 4, out_specs=[blk] * 4,
                          out_shape=[jax.ShapeDtypeStruct((r, c), F32)] * 4,
                          compiler_params=_params(("arbitrary",)))(w, m, v, g)


def _small_reduce(gathered):
    nd, r, n = gathered.shape
    tn = 2048 if n % 2048 == 0 else n
    def body(g_ref, s_ref, t_ref):
        s = g_ref[0]
        for dv in range(1, nd):
            s = s + g_ref[dv]
        s_ref[...] = s
        t_ref[...] = jnp.broadcast_to(jnp.sum(s, axis=0, keepdims=True), (r, tn))

    return pl.pallas_call(
        body, name="small_reduce", grid=(n // tn,),
        in_specs=[pl.BlockSpec((nd, r, tn), lambda j: (0, 0, j))],
        out_specs=[pl.BlockSpec((r, tn), lambda j: (0, j))] * 2,
        out_shape=[jax.ShapeDtypeStruct((r, n), F32)] * 2, compiler_params=_params(("parallel",)),
    )(gathered)


HBM = pl.BlockSpec(memory_space=pltpu.HBM)


def _all_gather(arrs, name):
    n = len(arrs)

    def body(*refs):
        ins, outs = refs[:n], refs[n:2 * n]
        send, recv, lsem = refs[2 * n:]
        x, y, c = lax.axis_index("x"), lax.axis_index("y"), lax.axis_index("c")
        me, sib = (x, y, c), (x, y, 1 - c)
        chips = [(1 - x, y), (x, 1 - y), (1 - x, 1 - y)]

        def slot(w, p):
            return outs[w].at[4 * p[0] + 2 * p[1] + p[2]]

        def cp(w, k, block, to, src=None):
            return pltpu.make_async_remote_copy(
                src_ref=slot(w, block) if src is None else src, dst_ref=slot(w, block),
                send_sem=send.at[w * 7 + k], recv_sem=recv.at[w * 7 + k], device_id=to, device_id_type=MESH)

        mine = [pltpu.make_async_copy(ins[w], slot(w, me), lsem.at[w]) for w in range(n)]
        for m in mine:
            m.start()
        first = []
        for w in range(n):
            first.append(cp(w, 0, me, sib, src=ins[w]))
            first += [cp(w, 1 + j, me, (*chip, c), src=ins[w]) for j, chip in enumerate(chips)]
        for f in first:
            f.start()
        passed = []
        for j, chip in enumerate(chips):
            for w in range(n):
                cp(w, 1 + j, (*chip, c), me).wait_recv()
                fwd = cp(w, 4 + j, (*chip, c), sib)
                fwd.start()
                passed.append(fwd)
        for w in range(n):
            cp(w, 0, sib, me).wait_recv()
            for j, chip in enumerate(chips):
                cp(w, 4 + j, (*chip, 1 - c), me).wait_recv()
        for f in first + passed:
            f.wait_send()
        for m in mine:
            m.wait()

    return pl.pallas_call(
        body, name=name, in_specs=[HBM] * n, out_specs=[HBM] * n,
        out_shape=[jax.ShapeDtypeStruct((8,) + a.shape, a.dtype) for a in arrs],
        scratch_shapes=[pltpu.SemaphoreType.DMA((7 * n,)), pltpu.SemaphoreType.DMA((7 * n,)),
                        pltpu.SemaphoreType.DMA((n,))],
    )(*arrs)


SEM = pl.BlockSpec(memory_space=pltpu.SEMAPHORE)
ANY = pl.BlockSpec(memory_space=pl.ANY)
EFFECT = pltpu.SideEffectType.DATAFLOW_SIDE_EFFECTING
N_PEERS = 7


def _peers(x, y, c):
    return [((1 - x) if k & 4 else x, (1 - y) if k & 2 else y, (1 - c) if k & 1 else c) for k in range(1, 8)]


def _spread_copies(src_refs, land_refs, send, recv, gather):
    x, y, c = lax.axis_index("x"), lax.axis_index("y"), lax.axis_index("c")
    me = 4 * x + 2 * y + c
    copies = []
    for w in range(len(src_refs)):
        for k, (px, py, pc) in enumerate(_peers(x, y, c)):
            p = 4 * px + 2 * py + pc
            copies.append((pltpu.make_async_remote_copy(
                src_ref=src_refs[w] if gather else src_refs[w].at[p], dst_ref=land_refs[w].at[me],
                send_sem=send[w].at[k], recv_sem=recv[w].at[k], device_id=(px, py, pc), device_id_type=MESH),
                pltpu.make_async_remote_copy(
                src_ref=src_refs[w] if gather else src_refs[w].at[p], dst_ref=land_refs[w].at[p],
                send_sem=send[w].at[k], recv_sem=recv[w].at[k], device_id=(px, py, pc), device_id_type=MESH)))
    return copies


def _hbm(a):
    return pltpu.with_memory_space_constraint(a, pltpu.HBM)


def _spread_start(srcs, lands, after, gather, name):
    n = len(srcs)

    def body(*refs):
        src_refs, land_refs = refs[:n], refs[n:2 * n]
        outs = refs[2 * n + 1:]
        send, recv, token = outs[:n], outs[n:2 * n], outs[4 * n]
        for start, _ in _spread_copies(src_refs, land_refs, send, recv, gather):
            start.start()
        token[...] = jnp.zeros_like(token)

    res = pl.pallas_call(
        body, name=name,
        out_shape=tuple([pltpu.SemaphoreType.DMA((N_PEERS,))] * (2 * n)
                        + [pltpu.HBM(a.shape, a.dtype) for a in srcs] + [pltpu.HBM(a.shape, a.dtype) for a in lands]
                        + [jax.ShapeDtypeStruct((SUBLANES, LANES), F32)]),
        in_specs=[HBM] * (2 * n) + [ANY],
        out_specs=tuple([SEM] * (2 * n) + [HBM] * (2 * n) + [pl.BlockSpec(memory_space=pltpu.VMEM)]),
        input_output_aliases={i: 2 * n + i for i in range(2 * n)},
        compiler_params=pltpu.CompilerParams(has_side_effects=EFFECT),
    )(*[_hbm(a) for a in srcs], *[_hbm(a) for a in lands], after)
    return res[:n], res[n:2 * n], res[2 * n:3 * n], res[3 * n:4 * n], res[4 * n]


def _spread_wait(send, recv, srcs, lands, after, gather, name):
    n = len(srcs)
    after = list(after) if isinstance(after, (list, tuple)) else [after]

    def body(*refs):
        src_refs, land_refs = refs[:n], refs[n:2 * n]
        send_refs, recv_refs = refs[2 * n:3 * n], refs[3 * n:4 * n]
        for _, arrive in _spread_copies(src_refs, land_refs, send_refs, recv_refs, gather):
            arrive.wait_send()
            arrive.wait_recv()

    res = pl.pallas_call(
        body, name=name,
        out_shape=tuple([pltpu.HBM(a.shape, a.dtype) for a in srcs] + [pltpu.HBM(a.shape, a.dtype) for a in lands]),
        in_specs=[HBM] * (2 * n) + [SEM] * (2 * n) + [ANY] * len(after),
        out_specs=tuple([HBM] * (2 * n)),
        input_output_aliases={i: i for i in range(2 * n)},
        compiler_params=pltpu.CompilerParams(has_side_effects=EFFECT),
    )(*srcs, *lands, *send, *recv, *after)
    return res[n:]


def _landing(shape, dtype, own, me):
    return lax.dynamic_update_index_in_dim(lax.empty((8,) + shape, dtype), own, me, 0)


N_GLR = GLA_W + GLA_RANK
FF_SLAB = D_FF // 4
FF_SLAB_P = FFP // 4


TRANSPOSED = ("w_in", "ffn_w_in")


def _prepare_sub1(gath):
    w_in_t = gath["w_in"].reshape(-1, gath["w_in"].shape[2])
    w2 = jnp.concatenate([gath["gla_gate_w2"][s] for s in range(8)], axis=1)
    return {"w_a_t": jnp.pad(w_in_t[:N_GLR], ((0, HA_W - N_GLR), (0, 0))), "w_b_t": w_in_t[N_GLR:],
            "w2p": jnp.pad(w2, ((0, LANES - GLA_RANK), (0, 0)))}


def _prepare_ffn_in(g):
    f = jnp.pad(g, ((0, 0), (0, FF_SLAB_P - FF_SLAB), (0, 0)))
    return f.reshape(2 * FFP, f.shape[2])


def _prepare_ffn_out(g):
    return jnp.pad(g.reshape(4, FF_SLAB, -1), ((0, 0), (0, FF_SLAB_P - FF_SLAB), (0, 0))).reshape(FFP, -1)


def _prepare_conv(g, conv_b):
    padc = FF_SLAB_P - FF_SLAB
    cw = jnp.pad(g, ((0, 0), (0, 0), (0, padc)))
    cb = jnp.pad(conv_b.reshape(8, 1, FF_SLAB), ((0, 0), (0, 0), (0, padc)))
    rows = jnp.concatenate([cw, cb, jnp.zeros((8, 4, FF_SLAB_P), F32)], axis=1)
    return jnp.concatenate([rows[s] for s in range(8)], axis=1)


def _prepare_ffn(gath, conv_b):
    return {"w_ffn_t": _prepare_ffn_in(gath["ffn_w_in"]), "wo": _prepare_ffn_out(gath["ffn_w_out"]),
            "cw": _prepare_conv(gath["ffn_conv_w"], conv_b)}


def _unpad_ff(a):
    r = a.shape[0]
    return a.reshape(r, 4, FF_SLAB_P)[:, :, :FF_SLAB].reshape(r, D_FF)


def _grad_slabs(g):
    w_in_t = jnp.concatenate([g["w_a_t"][:N_GLR], g["w_b_t"]], axis=0)
    s = {"w_in": w_in_t.reshape(4, 2, w_in_t.shape[0] // 8, w_in_t.shape[1])}
    for n in ("w_out", "ca_wq", "ca_wo"):
        s[n] = _to_slabs(n, g[n])
    for n in ("ca_wkv", "ffn_w_in"):
        s[n] = g[n].reshape((4, 2) + g[n].shape[1:])
    wo = g["wo"].reshape(4, FF_SLAB_P, -1)[:, :FF_SLAB]
    s["ffn_w_out"] = wo.reshape(4, 2, FF_SLAB // 2, wo.shape[-1])
    return s


class _AtHand:
    def __init__(self, p):
        self.p = p
        self.token = None

    def sub2(self, after):
        return self.p

    def ffn_in(self, after):
        return self.p["w_ffn_t"]

    def ffn_out(self, after):
        return self.p["wo"]

    def grads_out(self, group, slabs):
        pass

    def small_out(self, parts):
        pass


def _local_step(x, mem, positions, target, p, small, stages=None):
    t, d = x.shape
    stages = _AtHand(p) if stages is None else stages
    w_a_t, w_b_t, w2p, cw = p["w_a_t"], p["w_b_t"], p["w2p"], p["cw"]
    tabs = _rope_tables(positions)
    xb = x.astype(BF16) if stages.token is None else (x + stages.token[0, 0]).astype(BF16)
    memb = mem.astype(BF16)

    h_a = _matmul(xb, w_a_t, "nt", F32, 1024, 640, d, "mm_h_a")
    h_b = _matmul(xb, w_b_t, "nt", F32, 1024, 1024, d, "mm_h_b")
    o_g, o_raw, s_before = _gla_fwd(h_a, w2p, small["gla_gate_b"], small["gla_norm_g"])
    qr, kr = _rope_fwd(h_b, tabs)
    o_d_b, o_d, lse_tot = _dil_fwd_all(qr, kr, h_b)
    mixin = jnp.concatenate([o_g, o_d_b], axis=1)
    wts = stages.sub2(mixin)
    mix = _matmul(mixin, wts["w_out"], "nn", F32, 1024, 1024, d, "mm_mix")
    x1, x1b, x1t = _ln_fwd(x, mix, small["ln1_g"], small["ln1_b"], "ln1_fwd", True)

    q_ca = _matmul(x1b, wts["ca_wq"], "nn", BF16, 1024, 1024, d, "mm_caq")
    kvw = wts["ca_wkv"].shape[2]
    memkv = _matmul(memb, wts["ca_wkv"], "nn", BF16, mem.shape[0], kvw, d, "mm_memkv", b_slabs=True)
    o_c, o_ct = _ca_fwd(q_ca, memkv)
    ca_out = _matmul(o_c, wts["ca_wo"], "nn", F32, 1024, 1024, d, "mm_cao")
    x2, x2b = _ln_fwd(x1, ca_out, small["ln2_g"], small["ln2_b"], "ln2_fwd", False)

    w_ffn_t = stages.ffn_in(x2b)
    u0 = _matmul(x2b, w_ffn_t, "nt", BF16, 1024, 1024, d, "mm_u0")
    act, act_t = _swiglu_fwd(u0, cw)
    wo = stages.ffn_out(act)
    ffn = _matmul(act, wo, "nn", F32, 512, 1024, FFP, "mm_ffn")

    dp3, dp3b, dg3, db3, loss_part = _ln_bwd(x2, ffn, small["ln3_g"], small["ln3_b"], target, True, "ln3_bwd")
    g_wo, g_wo16 = _matmul(act_t, dp3b, "nn", F32, 512, 1024, t, "mm_g_wo", also_bf16=True)
    dact = _matmul(dp3b, wo, "nt", BF16, 1024, 512, d, "mm_dact")
    dug, duu, du_t, dcwg, dcwu = _swiglu_bwd(u0, cw, dact)
    g_ffn_in, g_ffn_in16 = _ffn_win_grad(du_t, x2b)

    def wo_slabs(a):
        a = a.reshape(4, FF_SLAB_P, -1)[:, :FF_SLAB]
        return a.reshape(8, FF_SLAB // 2, a.shape[-1])

    def wo_own(me):
        half = FF_SLAB // 2
        return lax.dynamic_slice_in_dim(g_wo, FF_SLAB_P * (me // 2) + half * (me % 2), half, axis=0)

    sent = stages.grads_out("ffn", {"ffn_w_out": (wo_own, wo_slabs(g_wo16)), "ffn_w_in": (g_ffn_in, g_ffn_in16)})
    dx2 = _matmul(dug, w_ffn_t, "nn", F32, 512, 1024, FFP, "mm_dx2_g", resid=dp3, resid_scale=ALPHA, dep=sent)
    dx2 = _matmul(duu, w_ffn_t, "nn", F32, 512, 1024, FFP, "mm_dx2_u", resid=dx2, b_k_off=1)

    dp2, dp2b, dg2, db2 = _ln_bwd(x1, ca_out, small["ln2_g"], small["ln2_b"], dx2, False, "ln2_bwd")
    g_cao, g_cao16 = _matmul(o_ct, dp2b, "nn", F32, 512, 1024, t, "mm_g_cao", also_bf16=True)
    do_c = _matmul(dp2b, wts["ca_wo"], "nt", BF16, 1024, 1024, d, "mm_do_c")
    dq_ca, dmemkv = _ca_bwd(q_ca, memkv, do_c)
    g_caq, g_caq16 = _matmul(x1t, dq_ca, "nn", F32, 512, 1024, t, "mm_g_caq", also_bf16=True)
    g_cakv, g_cakv16 = _matmul(memb, dmemkv.astype(BF16), "tn", F32, 512, kvw, mem.shape[0], "mm_g_cakv",
                               out_slabs=True, also_bf16=True)
    dx1 = _matmul(dq_ca, wts["ca_wq"], "nt", F32, 1024, 1024, d, "mm_dx1", resid=dp2, resid_scale=ALPHA)

    dp1, dp1b, dg1, db1 = _ln_bwd(x, mix, small["ln1_g"], small["ln1_b"], dx1, False, "ln1_bwd")
    g_wout, g_wout16 = _matmul(mixin, dp1b, "tn", F32, 512, 1024, t, "mm_g_wout", also_bf16=True)

    def row_slabs(a):
        return a.reshape(8, a.shape[0] // 8, a.shape[1])

    sent = stages.grads_out("attn", {"ca_wo": (row_slabs(g_cao), row_slabs(g_cao16)),
                                     "ca_wq": (row_slabs(g_caq), row_slabs(g_caq16)), "ca_wkv": (g_cakv, g_cakv16),
                                     "w_out": (row_slabs(g_wout), row_slabs(g_wout16))})
    dmix = _matmul(dp1b, wts["w_out"], "nt", F32, 1024, 1024, d, "mm_dmix", dep=sent)
    dh_a, dw2, dgate_b, dnorm_g = _gla_bwd(h_a, w2p, small["gla_gate_b"], small["gla_norm_g"], o_raw, s_before, dmix)
    small_parts = {
        "gla_gate_b": dgate_b, "gla_norm_g": dnorm_g, "ln1_g": dg1, "ln1_b": db1, "ln2_g": dg2, "ln2_b": db2,
        "ln3_g": dg3, "ln3_b": db3,
        "conv": jnp.concatenate([_unpad_ff(dcwg), _unpad_ff(dcwu)], axis=1),
        "gla_gate_w2": dw2[:GLA_RANK],
    }
    sent = stages.small_out(small_parts)
    dq_d, dk_d, dv_d = _dil_bwd_all(qr, kr, h_b, dmix, o_d, lse_tot)
    dh_b = _dil_dh(dq_d, dk_d, dv_d, tabs)
    g_wa_t, g_wa16 = _matmul(dh_a, xb, "tn", F32, 640, 1024, t, "mm_g_wa", also_bf16=True, dep=sent)
    g_wb_t, g_wb16 = _matmul(dh_b, xb, "tn", F32, 512, 1024, t, "mm_g_wb", also_bf16=True)

    def w_in_slabs(a, b):
        full = jnp.concatenate([a[:N_GLR], b], axis=0)
        return full.reshape(8, full.shape[0] // 8, full.shape[1])

    def w_in_own(me):
        rows = (N_GLR + g_wb_t.shape[0]) // 8
        full = jnp.concatenate([g_wa_t[:N_GLR], g_wb_t], axis=0)
        return lax.dynamic_slice_in_dim(full, me * rows, rows, axis=0)

    sent = stages.grads_out("w_in", {"w_in": (w_in_own, w_in_slabs(g_wa16, g_wb16))})
    dx = _matmul(dh_a, w_a_t, "nn", F32, 512, 1024, HA_W, "mm_dx_a", resid=dp1, resid_scale=ALPHA, dep=sent)
    dx = _matmul(dh_b, w_b_t, "nn", F32, 512, 1024, HB_W, "mm_dx_b", resid=dx)

    grads = {"w_a_t": g_wa_t, "w_b_t": g_wb_t, "w_out": g_wout, "ca_wq": g_caq, "ca_wkv": g_cakv, "ca_wo": g_cao,
             "ffn_w_in": g_ffn_in, "wo": g_wo}
    return loss_part, dx, grads, small_parts


BIG = ("w_in", "w_out", "ca_wq", "ca_wkv", "ca_wo", "ffn_w_in", "ffn_w_out")
COL_SHARDED = ("w_in", "ca_wkv", "ffn_w_in")
SMALL_ORDER = ("gla_gate_b", "gla_norm_g", "ln1_g", "ln1_b", "ln2_g", "ln2_b", "ln3_g", "ln3_b")


def _gathered_full(name, g):
    if name in COL_SHARDED:
        return g.transpose(1, 0, 2).reshape(g.shape[1], 8 * g.shape[2])
    return g.reshape(8 * g.shape[1], g.shape[2])


def _to_slabs(name, full):
    if name in COL_SHARDED:
        r, cc = full.shape
        s = full.reshape(r, 8, cc // 8).transpose(1, 0, 2)
    else:
        rr, c = full.shape
        s = full.reshape(8, rr // 8, c)
    return s.reshape((4, 2) + s.shape[1:])


def kernel(x, mem, positions, w_in, gla_gate_w2, gla_gate_b, gla_norm_g, w_out, ln1_g, ln1_b, ca_wq, ca_wkv, ca_wo, ln2_g, ln2_b, ffn_w_in, ffn_conv_w, ffn_conv_b, ffn_w_out, ln3_g, ln3_b, loss_target, m_w_in, m_gla_gate_w2, m_gla_gate_b, m_gla_norm_g, m_w_out, m_ln1_g, m_ln1_b, m_ca_wq, m_ca_wkv, m_ca_wo, m_ln2_g, m_ln2_b, m_ffn_w_in, m_ffn_conv_w, m_ffn_conv_b, m_ffn_w_out, m_ln3_g, m_ln3_b, v_w_in, v_gla_gate_w2, v_gla_gate_b, v_gla_norm_g, v_w_out, v_ln1_g, v_ln1_b, v_ca_wq, v_ca_wkv, v_ca_wo, v_ln2_g, v_ln2_b, v_ffn_w_in, v_ffn_conv_w, v_ffn_conv_b, v_ffn_w_out, v_ln3_g, v_ln3_b):
    weights = dict(w_in=w_in, gla_gate_w2=gla_gate_w2, gla_gate_b=gla_gate_b, gla_norm_g=gla_norm_g, w_out=w_out,
                   ln1_g=ln1_g, ln1_b=ln1_b, ca_wq=ca_wq, ca_wkv=ca_wkv, ca_wo=ca_wo, ln2_g=ln2_g, ln2_b=ln2_b,
                   ffn_w_in=ffn_w_in, ffn_conv_w=ffn_conv_w, ffn_conv_b=ffn_conv_b, ffn_w_out=ffn_w_out,
                   ln3_g=ln3_g, ln3_b=ln3_b)
    moms = dict(w_in=(m_w_in, v_w_in), gla_gate_w2=(m_gla_gate_w2, v_gla_gate_w2), gla_gate_b=(m_gla_gate_b, v_gla_gate_b),
                gla_norm_g=(m_gla_norm_g, v_gla_norm_g), w_out=(m_w_out, v_w_out), ln1_g=(m_ln1_g, v_ln1_g),
                ln1_b=(m_ln1_b, v_ln1_b), ca_wq=(m_ca_wq, v_ca_wq), ca_wkv=(m_ca_wkv, v_ca_wkv), ca_wo=(m_ca_wo, v_ca_wo),
                ln2_g=(m_ln2_g, v_ln2_g), ln2_b=(m_ln2_b, v_ln2_b), ffn_w_in=(m_ffn_w_in, v_ffn_w_in),
                ffn_conv_w=(m_ffn_conv_w, v_ffn_conv_w), ffn_conv_b=(m_ffn_conv_b, v_ffn_conv_b),
                ffn_w_out=(m_ffn_w_out, v_ffn_w_out), ln3_g=(m_ln3_g, v_ln3_g), ln3_b=(m_ln3_b, v_ln3_b))
    order = list(weights)
    xi, yi, ci = lax.axis_index("x"), lax.axis_index("y"), lax.axis_index("c")
    me = 4 * xi + 2 * yi + ci

    def travel(n, a):
        return jnp.swapaxes(a, 1, 2) if n in TRANSPOSED else a

    shard = {n: travel(n, weights[n]).astype(BF16)[0] for n in BIG}
    first = _all_gather([shard["w_in"], gla_gate_w2.astype(BF16)[0], ffn_conv_w[0]], "ag_first")
    p = _prepare_sub1({"w_in": first[0], "gla_gate_w2": first[1]})
    p["cw"] = _prepare_conv(first[2], ffn_conv_b)
    later = ("w_out", "ca_wq", "ca_wkv", "ca_wo", "ffn_w_in", "ffn_w_out")
    srcs = [shard[n] for n in later]
    lands = [_landing(shard[n].shape, BF16, shard[n], me) for n in later]
    send, recv, srcs, lands, token = _spread_start(srcs, lands, first[0], True, "ag_rest_start")

    class stages:
        pass

    stages.token = token

    def arrived(lo, hi, after, name):
        return _spread_wait(send[lo:hi], recv[lo:hi], srcs[lo:hi], lands[lo:hi], after, True, name)

    def sub2(after):
        g = dict(zip(later[:4], arrived(0, 4, after, "ag_wait_attn")))
        w = {n: _gathered_full(n, g[n]) for n in ("w_out", "ca_wq", "ca_wo")}
        w["ca_wkv"] = g["ca_wkv"]
        return w

    stages.sub2 = sub2
    stages.ffn_in = lambda after: _prepare_ffn_in(arrived(4, 5, after, "ag_wait_ffn_in")[0])
    stages.ffn_out = lambda after: _prepare_ffn_out(arrived(5, 6, after, "ag_wait_ffn_out")[0])
    sent = {}

    def grads_out(group, slabs):
        names = list(slabs)
        srcs16 = [slabs[n][1] for n in names]
        zones = [_landing(s.shape[1:], BF16, jnp.zeros(s.shape[1:], BF16), me) for s in srcs16]
        snd, rcv, s_thru, l_thru, tok = _spread_start(srcs16, zones, srcs16[0], False, f"rs_{group}_start")
        own32 = [slabs[n][0](me) if callable(slabs[n][0]) else slabs[n][0] for n in names]
        sent[group] = (names, own32, (snd, rcv, s_thru, l_thru))
        return tok

    stages.grads_out = grads_out
    small_sent = []

    def small_out(parts):
        packed = jnp.concatenate([parts[n] for n in SMALL_ORDER] + [parts["conv"],
                                 parts["gla_gate_w2"].reshape(SUBLANES, -1)], axis=1)
        packed = jnp.pad(packed, ((0, 0), (0, (-packed.shape[1]) % 2048)))
        zone = _landing(packed.shape, F32, packed, me)
        snd, rcv, s_thru, l_thru, tok = _spread_start([packed], [zone], packed, True, "ag_small_start")
        small_sent.append((snd, rcv, s_thru, l_thru))
        return tok

    stages.small_out = small_out
    small = dict(gla_gate_b=gla_gate_b, gla_norm_g=gla_norm_g, ln1_g=ln1_g, ln1_b=ln1_b, ln2_g=ln2_g, ln2_b=ln2_b,
                 ln3_g=ln3_g, ln3_b=ln3_b)

    loss_part, dx, grads, small_parts = _local_step(x[0], mem[0], positions[0], loss_target[0], p, small, stages)
    loss = lax.psum(jnp.sum(loss_part), ("x", "y", "c"))

    out = {}
    (allp,) = _spread_wait(*small_sent[0], dx, True, "ag_small_wait")
    dev_sum, row_sum = _small_reduce(allp)

    me1 = me.reshape(1).astype(jnp.int32)

    def finish_group(group, after):
        names, own32, handles = sent[group]
        landed = _spread_wait(*handles, after, False, f"rs_{group}_wait")
        for n, own, land in zip(names, own32, landed):
            m_, v_ = moms[n]
            res4 = _adamw_direct(travel(n, weights[n]), travel(n, m_), travel(n, v_), own, land, me1, f"adamw_{n}")
            out[n] = [travel(n, a) for a in res4]

    finish_group("ffn", dx)
    finish_group("attn", dx)
    off = 0
    for n in SMALL_ORDER:
        width = weights[n].shape[1]
        g = row_sum[0:1, off:off + width]
        off += width
        m_, v_ = moms[n]
        out[n] = _adamw(weights[n], m_, v_, g, f"adamw_{n}")
    conv_g = dev_sum[:, off:off + 2 * D_FF]
    off += 2 * D_FF
    g_cb = conv_g[3:4]
    out["ffn_conv_b"] = _adamw(ffn_conv_b, m_ffn_conv_b, v_ffn_conv_b, g_cb, "adamw_ffn_conv_b")
    wsh = ffn_conv_w.shape[2]
    g_cw = lax.dynamic_slice_in_dim(conv_g[0:3], me * wsh, wsh, axis=1)
    out["ffn_conv_w"] = _adamw(ffn_conv_w[0], m_ffn_conv_w[0], v_ffn_conv_w[0], g_cw, "adamw_ffn_conv_w")
    w2_g = dev_sum[:, off:off + GLA_RANK * GLA_HEADS * GLA_DK // SUBLANES].reshape(GLA_RANK, GLA_HEADS * GLA_DK)
    wsh2 = gla_gate_w2.shape[2]
    g_w2 = lax.dynamic_slice_in_dim(w2_g, me * wsh2, wsh2, axis=1)
    out["gla_gate_w2"] = _adamw(gla_gate_w2[0], m_gla_gate_w2[0], v_gla_gate_w2[0], g_w2, "adamw_gla_gate_w2")
    finish_group("w_in", [o[1] for o in out.values()])

    def shaped(n, a):
        return a.reshape(weights[n].shape)

    res = [loss, dx[None]]
    for k in range(4):
        res += [shaped(n, out[n][k]) for n in order]
    return tuple(res)


def _adamw_direct(w, m, v, own, land, me, name):
    _, r, c = w.shape
    tr, tc = _tile2d(r, c)
    blk = pl.BlockSpec((None, tr, tc), lambda i, j, s: (0, i, j))
    if own.ndim == 2:
        mine = pl.BlockSpec((tr, tc), lambda i, j, s: (i, j))
    else:
        mine = pl.BlockSpec((None, tr, tc), lambda i, j, s: (s[0], i, j))
    slots = [pl.BlockSpec((None, tr, tc), lambda i, j, s, k=k: (k, i, j)) for k in range(8)]

    def body(s_ref, w_ref, m_ref, v_ref, p_ref, *rest):
        slot_refs, (g_ref, d_ref, nm_ref, nv_ref) = rest[:8], rest[8:]
        g = p_ref[...]
        for sr in slot_refs:
            g = g + sr[...].astype(F32)
        d_ref[...], nm_ref[...], nv_ref[...] = _adamw_math(w_ref[...], m_ref[...], v_ref[...], g)
        g_ref[...] = g

    gs = pltpu.PrefetchScalarGridSpec(num_scalar_prefetch=1, grid=(r // tr, c // tc),
                                      in_specs=[blk, blk, blk, mine] + slots, out_specs=[blk] * 4)
    return pl.pallas_call(body, name=name, grid_spec=gs, out_shape=[jax.ShapeDtypeStruct((1, r, c), F32)] * 4,
                          compiler_params=_params(("parallel", "parallel")))(me, w, m, v, own, *([land] * 8))
```

```python
import jax
import jax.numpy as jnp
from jax import lax
from jax.experimental import pallas as pl
from jax.experimental.pallas import tpu as pltpu

F32 = jnp.float32
BF16 = jnp.bfloat16
MESH = pl.DeviceIdType.MESH

D_MODEL = 2048
LN_EPS = 1e-5
GLA_HEADS = 4
GLA_DV = 256
GLA_DK = 128
GLA_RANK = 16
GLA_TAU = 16.0
GLA_CHUNK = 64
DIL_HD = 128
DIL_HEADS = 8
DIL_BAND = 128
DIL_DILATIONS = (1, 4, 16)
ROPE_THETA = 500000.0
ROPE_DIMS = 32
CA_HEADS = 4
CA_HD = 512
D_FF = 5504
ALPHA = 2.0 ** 0.25
ADAM_LR = 0.001
ADAM_B1 = 0.9
ADAM_B2 = 0.999
ADAM_EPS = 1e-08
ADAM_WD = 0.01
ADAM_STEP = 10

LANES = 128
SUBLANES = 8
VMEM_LIMIT = 56 * 1024 * 1024

GLA_W = 2 * GLA_HEADS * GLA_DK + 2 * GLA_HEADS * GLA_DV
HA_W = GLA_W + LANES
HB_W = 3 * DIL_HEADS * DIL_HD
FFP = 5632
NEG = -1e30


def _params(sem):
    return pltpu.CompilerParams(dimension_semantics=sem, vmem_limit_bytes=VMEM_LIMIT)


def _sigmoid(x):
    return 1.0 / (1.0 + jnp.exp(-x))


def _dot(a, b, dn, precision=None):
    return lax.dot_general(a, b, (dn, ((), ())), preferred_element_type=F32, precision=precision)


NN = ((1,), (0,))
NT = ((1,), (1,))
TN = ((0,), (0,))


def _bf(v):
    return v if v.dtype == BF16 else v.astype(BF16)


def _matmul(a, b, kind, out_dtype, tm, tn, tk, name, resid=None, resid_scale=1.0, b_k_off=0, b_slabs=False,
            out_slabs=False, also_bf16=False, dep=None):
    if b_slabs:
        assert kind != "nt" and b.shape[2] == tn
        k2, n = b.shape[1], b.shape[0] * tn
    elif kind == "nt":
        n, k2 = b.shape
    else:
        k2, n = b.shape
    (k, m) = a.shape if kind == "tn" else a.shape[::-1]
    assert k2 >= k and (k2 == k or not b_slabs) and m % tm == 0 and n % tn == 0 and k % tk == 0, \
        (name, a.shape, b.shape, tm, tn, tk)
    nk = k // tk
    dn = {"nn": NN, "nt": NT, "tn": TN}[kind]
    a_spec = pl.BlockSpec((tk, tm), lambda i, j, kk: (kk, i)) if kind == "tn" else pl.BlockSpec((tm, tk), lambda i, j, kk: (i, kk))
    if b_slabs:
        b_spec = pl.BlockSpec((None, tk, tn), lambda i, j, kk: (j, kk, 0))
    elif kind == "nt":
        b_spec = pl.BlockSpec((tn, tk), lambda i, j, kk: (j, kk + b_k_off))
    else:
        b_spec = pl.BlockSpec((tk, tn), lambda i, j, kk: (kk + b_k_off, j))
    if out_slabs:
        o_spec = pl.BlockSpec((None, tm, tn), lambda i, j, kk: (j, i, 0))
        o_shape = (n // tn, m, tn)
    else:
        o_spec = pl.BlockSpec((tm, tn), lambda i, j, kk: (i, j))
        o_shape = (m, n)
    has_resid = resid is not None

    n_in = 2 + int(has_resid) + int(dep is not None)

    def body(*refs):
        a_ref, b_ref = refs[:2]
        r_ref = refs[2] if has_resid else None
        o_ref = refs[n_in]
        ob_ref = refs[n_in + 1] if also_bf16 else None
        part = _dot(_bf(a_ref[...]), _bf(b_ref[...]), dn)

        def finish(acc):
            if has_resid:
                acc = acc + resid_scale * r_ref[...].astype(F32)
            o_ref[...] = acc.astype(out_dtype)
            if also_bf16:
                ob_ref[...] = acc.astype(BF16)

        if nk == 1:
            finish(part)
        else:
            acc_ref = refs[-1]
            kk = pl.program_id(2)

            @pl.when(kk == 0)
            def _():
                acc_ref[...] = part

            @pl.when(kk > 0)
            def _():
                acc_ref[...] += part

            @pl.when(kk == nk - 1)
            def _():
                finish(acc_ref[...])

    in_specs = [a_spec, b_spec] + ([o_spec] if has_resid else [])
    args = (a, b) + ((resid,) if has_resid else ())
    if dep is not None:
        in_specs.append(pl.BlockSpec((SUBLANES, LANES), lambda i, j, kk: (0, 0)))
        args += (dep,)
    o_struct = jax.ShapeDtypeStruct(o_shape, out_dtype)
    return pl.pallas_call(
        body, name=name, out_shape=[o_struct, jax.ShapeDtypeStruct(o_shape, BF16)] if also_bf16 else o_struct,
        grid=(m // tm, n // tn, nk), in_specs=in_specs, out_specs=[o_spec, o_spec] if also_bf16 else o_spec,
        scratch_shapes=[pltpu.VMEM((tm, tn), F32)] if nk > 1 else [],
        compiler_params=_params(("parallel", "parallel", "arbitrary")),
    )(*args)


def _ln_core(xres, f):
    p = ALPHA * xres + f
    mu = jnp.mean(p, axis=-1, keepdims=True)
    xc = p - mu
    var = jnp.mean(xc * xc, axis=-1, keepdims=True)
    rstd = lax.rsqrt(var + LN_EPS)
    return xc * rstd, rstd


def _rows8(v):
    r, c = v.shape
    return jnp.sum(v.reshape(r // SUBLANES, SUBLANES, c), axis=0)


def _ln_fwd(xres, f, g, b, name, transposed, tr=256):
    t, d = xres.shape
    row = pl.BlockSpec((tr, d), lambda i: (i, 0))
    vec = pl.BlockSpec((1, d), lambda i: (0, 0))

    def body(x_ref, f_ref, g_ref, b_ref, y_ref, yb_ref, *yt_ref):
        xhat, _ = _ln_core(x_ref[...], f_ref[...])
        y = xhat * g_ref[...] + b_ref[...]
        y_ref[...] = y
        yb = y.astype(BF16)
        yb_ref[...] = yb
        if transposed:
            yt_ref[0][...] = yb.T

    out_specs = [row, row] + ([pl.BlockSpec((d, tr), lambda i: (0, i))] if transposed else [])
    out_shape = [jax.ShapeDtypeStruct((t, d), F32), jax.ShapeDtypeStruct((t, d), BF16)] \
        + ([jax.ShapeDtypeStruct((d, t), BF16)] if transposed else [])
    return pl.pallas_call(
        body, name=name, grid=(t // tr,), in_specs=[row, row, vec, vec], out_specs=out_specs, out_shape=out_shape,
        compiler_params=_params(("parallel",)),
    )(xres, f, g, b)


def _ln_bwd(xres, f, g, b, dy_or_target, loss_head, name, tr=256):
    t, d = xres.shape
    row = pl.BlockSpec((tr, d), lambda i: (i, 0))
    vec = pl.BlockSpec((1, d), lambda i: (0, 0))
    acc = pl.BlockSpec((SUBLANES, d), lambda i: (0, 0))
    lacc = pl.BlockSpec((SUBLANES, LANES), lambda i: (0, 0))

    def body(x_ref, f_ref, g_ref, b_ref, t_ref, dp_ref, dpb_ref, dg_ref, db_ref, *rest):
        i = pl.program_id(0)
        xhat, rstd = _ln_core(x_ref[...], f_ref[...])
        if loss_head:
            err = xhat * g_ref[...] + b_ref[...] - t_ref[...]
            dy = err * (1.0 / d)
            sq = err * err
            lanes = sq[:, :LANES]
            for kk in range(1, d // LANES):
                lanes = lanes + sq[:, kk * LANES:(kk + 1) * LANES]
            lpart = _rows8(lanes) * (0.5 / d)
        else:
            dy = t_ref[...]
        dxh = dy * g_ref[...]
        m1 = jnp.mean(dxh, axis=-1, keepdims=True)
        m2 = jnp.mean(dxh * xhat, axis=-1, keepdims=True)
        dp = rstd * (dxh - m1 - xhat * m2)
        dp_ref[...] = dp
        dpb_ref[...] = dp.astype(BF16)
        dgp = _rows8(dy * xhat)
        dbp = _rows8(dy)

        @pl.when(i == 0)
        def _():
            dg_ref[...] = dgp
            db_ref[...] = dbp
            if loss_head:
                rest[0][...] = lpart

        @pl.when(i > 0)
        def _():
            dg_ref[...] += dgp
            db_ref[...] += dbp
            if loss_head:
                rest[0][...] += lpart

    out_shape = [jax.ShapeDtypeStruct((t, d), F32), jax.ShapeDtypeStruct((t, d), BF16),
                 jax.ShapeDtypeStruct((SUBLANES, d), F32), jax.ShapeDtypeStruct((SUBLANES, d), F32)]
    out_specs = [row, row, acc, acc]
    if loss_head:
        out_shape.append(jax.ShapeDtypeStruct((SUBLANES, LANES), F32))
        out_specs.append(lacc)
    return pl.pallas_call(
        body, name=name, grid=(t // tr,), in_specs=[row, row, vec, vec, row], out_specs=out_specs,
        out_shape=out_shape, compiler_params=_params(("arbitrary",)),
    )(xres, f, g, b, dy_or_target)


def _gla_gates(glr, w2, gb):
    z = _dot(_bf(glr), w2, NN) + gb
    lg = (jnp.minimum(z, 0.0) - jnp.log(1.0 + jnp.exp(-jnp.abs(z)))) * (1.0 / GLA_TAU)
    c = z.shape[0]
    ri = lax.broadcasted_iota(jnp.int32, (c, c), 0)
    ci = lax.broadcasted_iota(jnp.int32, (c, c), 1)
    tri = (ci <= ri).astype(F32)
    bcum = _dot(tri, lg, NN, precision=lax.Precision.HIGHEST)
    blast = jnp.sum(lg, axis=0, keepdims=True)
    return z, bcum, blast, tri


def _gla_specs(t):
    c = GLA_CHUNK
    return c, t // c


def _gla_fwd(h_a, w2p, gate_b, norm_g):
    t = h_a.shape[0]
    c, n = _gla_specs(t)
    hk, hv = GLA_HEADS * GLA_DK, GLA_HEADS * GLA_DV
    scale = GLA_DK ** -0.5

    def body(q_ref, k_ref, v_ref, r_ref, glr_ref, w2_ref, gb_ref, ng_ref, og_ref, oraw_ref, sb_ref, st_ref):
        i = pl.program_id(0)

        @pl.when(i == 0)
        def _():
            st_ref[...] = jnp.zeros_like(st_ref)

        _, bcum, blast, _ = _gla_gates(glr_ref[...], w2_ref[...], gb_ref[...])
        ri = lax.broadcasted_iota(jnp.int32, (c, c), 0)
        ci = lax.broadcasted_iota(jnp.int32, (c, c), 1)
        causal = ci <= ri
        for h in range(GLA_HEADS):
            ks = slice(h * GLA_DK, (h + 1) * GLA_DK)
            vs = slice(h * GLA_DV, (h + 1) * GLA_DV)
            b_h, bl_h = bcum[:, ks], blast[:, ks]
            q_h, k_h = q_ref[:, ks], k_ref[:, ks]
            v_h = _bf(v_ref[:, vs])
            qi = _bf(q_h * scale * jnp.exp(b_h))
            ki = _bf(k_h * jnp.exp(-b_h))
            ke = _bf(k_h * jnp.exp(bl_h - b_h))
            st = st_ref[h]
            sb_ref[0, h] = st
            a = jnp.where(causal, _dot(qi, ki, NT), 0.0)
            o = _dot(_bf(a), v_h, NN) + _dot(qi, _bf(st), NT)
            st_ref[h] = st * jnp.exp(bl_h) + _dot(v_h, ke, TN)
            oraw_ref[:, vs] = o
            mu = jnp.mean(o, axis=-1, keepdims=True)
            oc = o - mu
            var = jnp.mean(oc * oc, axis=-1, keepdims=True)
            xh = oc * lax.rsqrt(var + LN_EPS)
            r_h = r_ref[:, vs]
            og_ref[:, vs] = (xh * ng_ref[:, vs] * (r_h * _sigmoid(r_h))).astype(BF16)

    return pl.pallas_call(
        body, name="gla_fwd", grid=(n,),
        in_specs=[pl.BlockSpec((c, hk), lambda i: (i, 0)), pl.BlockSpec((c, hk), lambda i: (i, 1)),
                  pl.BlockSpec((c, hv), lambda i: (i, 1)), pl.BlockSpec((c, hv), lambda i: (i, 2)),
                  pl.BlockSpec((c, LANES), lambda i: (i, GLA_W // LANES)),
                  pl.BlockSpec((LANES, hk), lambda i: (0, 0)), pl.BlockSpec((1, hk), lambda i: (0, 0)),
                  pl.BlockSpec((1, hv), lambda i: (0, 0))],
        out_specs=[pl.BlockSpec((c, hv), lambda i: (i, 0)), pl.BlockSpec((c, hv), lambda i: (i, 0)),
                   pl.BlockSpec((1, GLA_HEADS, GLA_DV, GLA_DK), lambda i: (i, 0, 0, 0))],
        out_shape=[jax.ShapeDtypeStruct((t, hv), BF16), jax.ShapeDtypeStruct((t, hv), F32),
                   jax.ShapeDtypeStruct((n, GLA_HEADS, GLA_DV, GLA_DK), F32)],
        scratch_shapes=[pltpu.VMEM((GLA_HEADS, GLA_DV, GLA_DK), F32)],
        compiler_params=_params(("arbitrary",)),
    )(h_a, h_a, h_a, h_a, h_a, w2p, gate_b, norm_g)


def _gla_bwd(h_a, w2p, gate_b, norm_g, o_raw, s_before, dmix):
    t = h_a.shape[0]
    c, n = _gla_specs(t)
    hk, hv = GLA_HEADS * GLA_DK, GLA_HEADS * GLA_DV
    scale = GLA_DK ** -0.5
    rev = lambda i: n - 1 - i

    def body(q_ref, k_ref, v_ref, r_ref, glr_ref, w2_ref, gb_ref, ng_ref, oraw_ref, sb_ref, do_ref,
             dh_ref, dw2_ref, dgb_ref, dng_ref, dst_ref):
        i = pl.program_id(0)

        @pl.when(i == 0)
        def _():
            dst_ref[...] = jnp.zeros_like(dst_ref)

        glr = glr_ref[...]
        z, bcum, blast, tri = _gla_gates(glr, w2_ref[...], gb_ref[...])
        ri = lax.broadcasted_iota(jnp.int32, (c, c), 0)
        ci = lax.broadcasted_iota(jnp.int32, (c, c), 1)
        causal = ci <= ri
        dlg_parts = []
        dng_parts = []
        for h in range(GLA_HEADS):
            ks = slice(h * GLA_DK, (h + 1) * GLA_DK)
            vs = slice(h * GLA_DV, (h + 1) * GLA_DV)
            o = oraw_ref[:, vs]
            mu = jnp.mean(o, axis=-1, keepdims=True)
            oc = o - mu
            var = jnp.mean(oc * oc, axis=-1, keepdims=True)
            rstd = lax.rsqrt(var + LN_EPS)
            xh = oc * rstd
            r_h = r_ref[:, vs]
            sg = _sigmoid(r_h)
            silu = r_h * sg
            dout = do_ref[:, vs]
            ng = ng_ref[:, vs]
            dng_parts.append(_rows8(dout * xh * silu))
            dr = dout * xh * ng * (sg * (1.0 + r_h * (1.0 - sg)))
            dxh = dout * ng * silu
            m1 = jnp.mean(dxh, axis=-1, keepdims=True)
            m2 = jnp.mean(dxh * xh, axis=-1, keepdims=True)
            do_raw = _bf(rstd * (dxh - m1 - xh * m2))
            b_h, bl_h = bcum[:, ks], blast[:, ks]
            q_h, k_h = q_ref[:, ks], k_ref[:, ks]
            v_h = _bf(v_ref[:, vs])
            eb, enb, eend = jnp.exp(b_h), jnp.exp(-b_h), jnp.exp(bl_h - b_h)
            decay = jnp.exp(bl_h)
            qi_f, ki_f, ke_f = q_h * scale * eb, k_h * enb, k_h * eend
            qi, ki, ke = _bf(qi_f), _bf(ki_f), _bf(ke_f)
            st = sb_ref[0, h]
            dst = dst_ref[h]
            dst_b = _bf(dst)
            a = _bf(jnp.where(causal, _dot(qi, ki, NT), 0.0))
            da = _bf(jnp.where(causal, _dot(do_raw, v_h, NT), 0.0))
            dv = _dot(a, do_raw, TN) + _dot(ke, dst_b, NT)
            dqi = _dot(da, ki, NN) + _dot(do_raw, _bf(st), NN)
            dki = _dot(da, qi, TN)
            dke = _dot(v_h, dst_b, NN)
            dst_ref[h] = _dot(do_raw, qi, TN) + dst * decay
            dbl = decay * jnp.sum(st * dst, axis=0, keepdims=True) + jnp.sum(dke * ke_f, axis=0, keepdims=True)
            dbc = dqi * qi_f - dki * ki_f - dke * ke_f
            dlg_parts.append(_dot(tri, dbc, TN, precision=lax.Precision.HIGHEST) + dbl)
            dh_ref[:, ks] = (dqi * eb * scale).astype(BF16)
            dh_ref[:, hk + h * GLA_DK: hk + (h + 1) * GLA_DK] = (dki * enb + dke * eend).astype(BF16)
            dh_ref[:, 2 * hk + h * GLA_DV: 2 * hk + (h + 1) * GLA_DV] = dv.astype(BF16)
            dh_ref[:, 2 * hk + hv + h * GLA_DV: 2 * hk + hv + (h + 1) * GLA_DV] = dr.astype(BF16)
        dlg = jnp.concatenate(dlg_parts, axis=1)
        dz = dlg * (1.0 / GLA_TAU) * _sigmoid(-z)
        dz_b = _bf(dz)
        dh_ref[:, GLA_W:] = _dot(dz_b, w2_ref[...], NT).astype(BF16)
        dw2p = _dot(_bf(glr), dz_b, TN)
        dgbp = _rows8(dz)
        dngp = jnp.concatenate(dng_parts, axis=1)

        @pl.when(i == 0)
        def _():
            dw2_ref[...] = dw2p
            dgb_ref[...] = dgbp
            dng_ref[...] = dngp

        @pl.when(i > 0)
        def _():
            dw2_ref[...] += dw2p
            dgb_ref[...] += dgbp
            dng_ref[...] += dngp

    return pl.pallas_call(
        body, name="gla_bwd", grid=(n,),
        in_specs=[pl.BlockSpec((c, hk), lambda i: (rev(i), 0)), pl.BlockSpec((c, hk), lambda i: (rev(i), 1)),
                  pl.BlockSpec((c, hv), lambda i: (rev(i), 1)), pl.BlockSpec((c, hv), lambda i: (rev(i), 2)),
                  pl.BlockSpec((c, LANES), lambda i: (rev(i), GLA_W // LANES)),
                  pl.BlockSpec((LANES, hk), lambda i: (0, 0)), pl.BlockSpec((1, hk), lambda i: (0, 0)),
                  pl.BlockSpec((1, hv), lambda i: (0, 0)),
                  pl.BlockSpec((c, hv), lambda i: (rev(i), 0)),
                  pl.BlockSpec((1, GLA_HEADS, GLA_DV, GLA_DK), lambda i: (rev(i), 0, 0, 0)),
                  pl.BlockSpec((c, hv), lambda i: (rev(i), 0))],
        out_specs=[pl.BlockSpec((c, HA_W), lambda i: (rev(i), 0)),
                   pl.BlockSpec((LANES, hk), lambda i: (0, 0)),
                   pl.BlockSpec((SUBLANES, hk), lambda i: (0, 0)),
                   pl.BlockSpec((SUBLANES, hv), lambda i: (0, 0))],
        out_shape=[jax.ShapeDtypeStruct((t, HA_W), BF16), jax.ShapeDtypeStruct((LANES, hk), F32),
                   jax.ShapeDtypeStruct((SUBLANES, hk), F32), jax.ShapeDtypeStruct((SUBLANES, hv), F32)],
        scratch_shapes=[pltpu.VMEM((GLA_HEADS, GLA_DV, GLA_DK), F32)],
        compiler_params=_params(("arbitrary",)),
    )(h_a, h_a, h_a, h_a, h_a, w2p, gate_b, norm_g, o_raw, s_before, dmix)


def _rope_tables(positions):
    half = ROPE_DIMS // 2
    inv_freq = ROPE_THETA ** (-jnp.arange(0, ROPE_DIMS, 2, dtype=F32) / ROPE_DIMS)
    ang = positions.astype(F32).reshape(-1, 1) * inv_freq
    cos, sin = jnp.cos(ang), jnp.sin(ang)
    t = cos.shape[0]
    one = jnp.ones((t, DIL_HD - ROPE_DIMS), F32)
    zero = jnp.zeros((t, DIL_HD - ROPE_DIMS), F32)
    zh = jnp.zeros((t, half), F32)
    return (jnp.concatenate([cos, cos, one], axis=1), jnp.concatenate([-sin, zh, zero], axis=1),
            jnp.concatenate([zh, sin, zero], axis=1))


def _rope_apply(x, c, s1, s2):
    half = ROPE_DIMS // 2
    return x * c + pltpu.roll(x, DIL_HD - half, 1) * s1 + pltpu.roll(x, half, 1) * s2


def _rope_apply_t(dy, c, s1, s2):
    half = ROPE_DIMS // 2
    return dy * c + pltpu.roll(dy * s1, half, 1) + pltpu.roll(dy * s2, DIL_HD - half, 1)


def _rope_fwd(h_b, tabs, tr=256):
    t = h_b.shape[0]
    w = DIL_HEADS * DIL_HD
    scale = DIL_HD ** -0.5
    tab = pl.BlockSpec((tr, DIL_HD), lambda i: (i, 0))
    outb = pl.BlockSpec((tr, w), lambda i: (i, 0))

    def body(q_ref, k_ref, c_ref, s1_ref, s2_ref, qo_ref, ko_ref):
        c, s1, s2 = c_ref[...], s1_ref[...], s2_ref[...]
        for h in range(DIL_HEADS):
            hs = slice(h * DIL_HD, (h + 1) * DIL_HD)
            qo_ref[:, hs] = _rope_apply(q_ref[:, hs] * scale, c, s1, s2)
            ko_ref[:, hs] = _rope_apply(k_ref[:, hs], c, s1, s2)

    return pl.pallas_call(
        body, name="rope_fwd", grid=(t // tr,),
        in_specs=[pl.BlockSpec((tr, w), lambda i: (i, 0)), pl.BlockSpec((tr, w), lambda i: (i, 1)), tab, tab, tab],
        out_specs=[outb, outb],
        out_shape=[jax.ShapeDtypeStruct((t, w), F32)] * 2,
        compiler_params=_params(("parallel",)),
    )(h_b, h_b, *tabs)


def _dil_dh(dq, dk, dv, tabs, tr=256):
    t, w = dq.shape
    scale = DIL_HD ** -0.5
    tab = pl.BlockSpec((tr, DIL_HD), lambda i: (i, 0))
    inb = pl.BlockSpec((tr, w), lambda i: (i, 0))

    def body(dq_ref, dk_ref, dv_ref, c_ref, s1_ref, s2_ref, o_ref):
        c, s1, s2 = c_ref[...], s1_ref[...], s2_ref[...]
        for h in range(DIL_HEADS):
            hs = slice(h * DIL_HD, (h + 1) * DIL_HD)
            o_ref[:, h * DIL_HD:(h + 1) * DIL_HD] = (_rope_apply_t(dq_ref[:, hs], c, s1, s2) * scale).astype(BF16)
            o_ref[:, w + h * DIL_HD: w + (h + 1) * DIL_HD] = _rope_apply_t(dk_ref[:, hs], c, s1, s2).astype(BF16)
        o_ref[:, 2 * w:] = dv_ref[...].astype(BF16)

    return pl.pallas_call(
        body, name="dil_dh", grid=(t // tr,), in_specs=[inb] * 3 + [tab] * 3,
        out_specs=pl.BlockSpec((tr, 3 * w), lambda i: (i, 0)),
        out_shape=jax.ShapeDtypeStruct((t, 3 * w), BF16), compiler_params=_params(("parallel",)),
    )(dq, dk, dv, *tabs)


def _band_masks(not_first):
    r = lax.broadcasted_iota(jnp.int32, (DIL_BAND, 2 * DIL_BAND), 0)
    c = lax.broadcasted_iota(jnp.int32, (DIL_BAND, 2 * DIL_BAND), 1)
    nf = jnp.full((DIL_BAND, 2 * DIL_BAND), not_first, jnp.int32)
    look_back = jnp.logical_and(jnp.logical_and(c < DIL_BAND, c >= r), nf > 0)
    own_band = jnp.logical_and(c >= DIL_BAND, (c - DIL_BAND) <= r)
    return jnp.logical_or(look_back, own_band)


def _gather_rows(dst_ref, src_ref, t, d, cast=None):
    n = t // d
    for r in range(d):
        v = src_ref[pl.ds(r, n, stride=d), :] if d > 1 else src_ref[...]
        dst_ref[r * n:(r + 1) * n, :] = v if cast is None else v.astype(cast)


def _tri_mask():
    r = lax.broadcasted_iota(jnp.int32, (DIL_BAND, DIL_BAND), 0)
    c = lax.broadcasted_iota(jnp.int32, (DIL_BAND, DIL_BAND), 1)
    return c <= r


def _dil_fwd_all(qr, kr, h_b):
    t = qr.shape[0]
    nbands = t // DIL_BAND
    nbr = len(DIL_DILATIONS)
    hoff = DIL_HEADS

    def col(off):
        return pl.BlockSpec((t, DIL_HD), lambda h: (0, off + h), pipeline_mode=pl.Buffered(1))

    outb = pl.BlockSpec((t, DIL_HD), lambda h: (0, h))

    def body(q_ref, k_ref, v_ref, ob_ref, of_ref, lt_ref, qs, ks, vs, os_, ls_, *br):
        obr, lbr = br[:nbr], br[nbr:]
        for bi, d in enumerate(DIL_DILATIONS):
            n = t // d
            nb = n // DIL_BAND
            _gather_rows(qs, q_ref, t, d, BF16)
            _gather_rows(ks, k_ref, t, d, BF16)
            _gather_rows(vs, v_ref, t, d, BF16)
            s = jnp.where(_tri_mask(), _dot(qs[0:DIL_BAND, :], ks[0:DIL_BAND, :], NT), NEG)
            m = jnp.max(s, axis=-1, keepdims=True)
            pr = jnp.exp(s - m)
            den = jnp.sum(pr, axis=-1, keepdims=True)
            os_[0:DIL_BAND, :] = _dot(_bf(pr), vs[0:DIL_BAND, :], NN) / den
            ls_[0:DIL_BAND, :] = jnp.broadcast_to(m + jnp.log(den), (DIL_BAND, DIL_HD))

            def band(b, carry, nb=nb):
                st = pl.multiple_of((b - 1) * DIL_BAND, DIL_BAND)
                cur = pl.ds(st + DIL_BAND, DIL_BAND)
                both = pl.ds(st, 2 * DIL_BAND)
                not_first = ((b % nb) != 0).astype(jnp.int32)
                s = jnp.where(_band_masks(not_first), _dot(qs[cur, :], ks[both, :], NT), NEG)
                m = jnp.max(s, axis=-1, keepdims=True)
                pr = jnp.exp(s - m)
                den = jnp.sum(pr, axis=-1, keepdims=True)
                os_[cur, :] = _dot(_bf(pr), vs[both, :], NN) / den
                ls_[cur, :] = jnp.broadcast_to(m + jnp.log(den), (DIL_BAND, DIL_HD))
                return carry

            lax.fori_loop(1, nbands, band, 0, unroll=8)
            for r in range(d):
                dst = pl.ds(r, n, stride=d) if d > 1 else slice(None)
                obr[bi][dst, :] = os_[r * n:(r + 1) * n, :]
                lbr[bi][dst, :] = ls_[r * n:(r + 1) * n, :]
        rows = 512
        for c0 in range(0, t, rows):
            sl = slice(c0, c0 + rows)
            la, lb, lc = lbr[0][sl, :], lbr[1][sl, :], lbr[2][sl, :]
            m = jnp.maximum(jnp.maximum(la, lb), lc)
            ea, eb, ec = jnp.exp(la - m), jnp.exp(lb - m), jnp.exp(lc - m)
            den = ea + eb + ec
            o = (ea * obr[0][sl, :] + eb * obr[1][sl, :] + ec * obr[2][sl, :]) / den
            ob_ref[sl, :] = o.astype(BF16)
            of_ref[sl, :] = o
            lt_ref[sl, :] = m + jnp.log(den)

    w = DIL_HEADS * DIL_HD
    vm = lambda dt: pltpu.VMEM((t, DIL_HD), dt)
    return pl.pallas_call(
        body, name="dil_fwd", grid=(DIL_HEADS,), in_specs=[col(0), col(0), col(2 * hoff)],
        out_specs=[outb, outb, outb],
        out_shape=[jax.ShapeDtypeStruct((t, w), BF16), jax.ShapeDtypeStruct((t, w), F32),
                   jax.ShapeDtypeStruct((t, w), F32)],
        scratch_shapes=[vm(BF16)] * 3 + [vm(F32)] * 2 + [vm(F32)] * (2 * nbr),
        compiler_params=_params(("parallel",)),
    )(qr, kr, h_b)


def _dil_bwd_all(qr, kr, h_b, dmix, o_d, lse_tot):
    t = qr.shape[0]
    nbands = t // DIL_BAND
    hoff = DIL_HEADS

    def col(off):
        return pl.BlockSpec((t, DIL_HD), lambda h: (0, off + h), pipeline_mode=pl.Buffered(1))

    outb = pl.BlockSpec((t, DIL_HD), lambda h: (0, h))

    def body(q_ref, k_ref, v_ref, do_ref, o_ref, l_ref, dq_ref, dk_ref, dv_ref,
             qs, ks, vs, dos, lss, dds, dqs, acck, accv, ddt):
        rows = 512
        for c0 in range(0, t, rows):
            prod = do_ref[c0:c0 + rows, :] * o_ref[c0:c0 + rows, :]
            ddt[c0:c0 + rows, :] = jnp.broadcast_to(jnp.sum(prod, axis=-1, keepdims=True), (rows, DIL_HD))
        for bi, d in enumerate(DIL_DILATIONS):
            n = t // d
            nb = n // DIL_BAND
            _gather_rows(qs, q_ref, t, d, BF16)
            _gather_rows(ks, k_ref, t, d, BF16)
            _gather_rows(vs, v_ref, t, d, BF16)
            _gather_rows(dos, do_ref, t, d, BF16)
            _gather_rows(lss, l_ref, t, d)
            _gather_rows(dds, ddt, t, d)
            b0 = slice(0, DIL_BAND)
            s = jnp.where(_tri_mask(), _dot(qs[b0, :], ks[b0, :], NT), NEG)
            pr = jnp.exp(s - lss[b0, :])
            ds = _bf(pr * (_dot(dos[b0, :], vs[b0, :], NT) - dds[b0, :]))
            dqs[b0, :] = _dot(ds, ks[b0, :], NN)
            acck[DIL_BAND:2 * DIL_BAND, :] = _dot(ds, qs[b0, :], TN)
            accv[DIL_BAND:2 * DIL_BAND, :] = _dot(_bf(pr), dos[b0, :], TN)

            def band(b, carry, nb=nb):
                st = pl.multiple_of((b - 1) * DIL_BAND, DIL_BAND)
                cur = pl.ds(st + DIL_BAND, DIL_BAND)
                both = pl.ds(st, 2 * DIL_BAND)
                back_rows = pl.ds(st + DIL_BAND, DIL_BAND)
                own_rows = pl.ds(st + 2 * DIL_BAND, DIL_BAND)
                not_first = ((b % nb) != 0).astype(jnp.int32)
                qb, dob, lb, ddb = qs[cur, :], dos[cur, :], lss[cur, :], dds[cur, :]
                kcat, vcat = ks[both, :], vs[both, :]
                s = jnp.where(_band_masks(not_first), _dot(qb, kcat, NT), NEG)
                pr = jnp.exp(s - jnp.concatenate([lb, lb], axis=1))
                ds = _bf(pr * (_dot(dob, vcat, NT) - jnp.concatenate([ddb, ddb], axis=1)))
                dqs[cur, :] = _dot(ds, kcat, NN)
                dkk = _dot(ds, qb, TN)
                dvv = _dot(_bf(pr), dob, TN)
                acck[back_rows, :] += dkk[:DIL_BAND]
                accv[back_rows, :] += dvv[:DIL_BAND]
                acck[own_rows, :] = dkk[DIL_BAND:]
                accv[own_rows, :] = dvv[DIL_BAND:]
                return carry

            lax.fori_loop(1, nbands, band, 0, unroll=8)
            for r in range(d):
                lo = r * n
                if d == 1:
                    dq_ref[...] = dqs[...]
                    dk_ref[...] = acck[DIL_BAND:DIL_BAND + t, :]
                    dv_ref[...] = accv[DIL_BAND:DIL_BAND + t, :]
                else:
                    dst = pl.ds(r, n, stride=d)
                    dq_ref[dst, :] = dq_ref[dst, :] + dqs[lo:lo + n, :]
                    dk_ref[dst, :] = dk_ref[dst, :] + acck[DIL_BAND + lo:DIL_BAND + lo + n, :]
                    dv_ref[dst, :] = dv_ref[dst, :] + accv[DIL_BAND + lo:DIL_BAND + lo + n, :]

    w = DIL_HEADS * DIL_HD
    vm = lambda dt, extra=0: pltpu.VMEM((t + extra, DIL_HD), dt)
    return pl.pallas_call(
        body, name="dil_bwd", grid=(DIL_HEADS,),
        in_specs=[col(0), col(0), col(2 * hoff), col(hoff), col(0), col(0)], out_specs=[outb] * 3,
        out_shape=[jax.ShapeDtypeStruct((t, w), F32)] * 3,
        scratch_shapes=[vm(BF16)] * 4 + [vm(F32)] * 3 + [vm(F32, DIL_BAND)] * 2 + [vm(F32)],
        compiler_params=_params(("parallel",)),
    )(qr, kr, h_b, dmix, o_d, lse_tot)


def _ca_fwd(q, memkv, tq=512):
    t, d = q.shape
    m = memkv.shape[0]
    scale = CA_HD ** -0.5

    def body(q_ref, k_ref, v_ref, o_ref, ot_ref):
        for h in range(CA_HEADS):
            hs = slice(h * CA_HD, (h + 1) * CA_HD)
            s = _dot(q_ref[:, hs], k_ref[:, hs], NT) * scale
            p = jnp.exp(s - jnp.max(s, axis=-1, keepdims=True))
            p = p / jnp.sum(p, axis=-1, keepdims=True)
            o = _dot(_bf(p), v_ref[:, hs], NN).astype(BF16)
            o_ref[:, hs] = o
            ot_ref[hs, :] = o.T

    return pl.pallas_call(
        body, name="ca_fwd", grid=(t // tq,),
        in_specs=[pl.BlockSpec((tq, d), lambda i: (i, 0)), pl.BlockSpec((m, d), lambda i: (0, 0)),
                  pl.BlockSpec((m, d), lambda i: (0, 1))],
        out_specs=[pl.BlockSpec((tq, d), lambda i: (i, 0)), pl.BlockSpec((d, tq), lambda i: (0, i))],
        out_shape=[jax.ShapeDtypeStruct((t, d), BF16), jax.ShapeDtypeStruct((d, t), BF16)],
        compiler_params=_params(("parallel",)),
    )(q, memkv, memkv)


def _ca_bwd(q, memkv, do, tq=512):
    t, d = q.shape
    m = memkv.shape[0]
    scale = CA_HD ** -0.5

    def body(q_ref, k_ref, v_ref, do_ref, dq_ref, dkv_ref):
        i = pl.program_id(0)

        @pl.when(i == 0)
        def _():
            dkv_ref[...] = jnp.zeros_like(dkv_ref)

        for h in range(CA_HEADS):
            hs = slice(h * CA_HD, (h + 1) * CA_HD)
            q_h, k_h, v_h, do_h = q_ref[:, hs], k_ref[:, hs], v_ref[:, hs], do_ref[:, hs]
            s = _dot(q_h, k_h, NT) * scale
            p = jnp.exp(s - jnp.max(s, axis=-1, keepdims=True))
            p = p / jnp.sum(p, axis=-1, keepdims=True)
            dp = _dot(do_h, v_h, NT)
            ds = _bf(p * (dp - jnp.sum(p * dp, axis=-1, keepdims=True)) * scale)
            dq_ref[:, hs] = _dot(ds, k_h, NN).astype(BF16)
            dkv_ref[:, hs] += _dot(ds, q_h, TN)
            dkv_ref[:, d + h * CA_HD: d + (h + 1) * CA_HD] += _dot(_bf(p), do_h, TN)

    return pl.pallas_call(
        body, name="ca_bwd", grid=(t // tq,),
        in_specs=[pl.BlockSpec((tq, d), lambda i: (i, 0)), pl.BlockSpec((m, d), lambda i: (0, 0)),
                  pl.BlockSpec((m, d), lambda i: (0, 1)), pl.BlockSpec((tq, d), lambda i: (i, 0))],
        out_specs=[pl.BlockSpec((tq, d), lambda i: (i, 0)), pl.BlockSpec((m, 2 * d), lambda i: (0, 0))],
        out_shape=[jax.ShapeDtypeStruct((t, d), BF16), jax.ShapeDtypeStruct((m, 2 * d), F32)],
        compiler_params=_params(("arbitrary",)),
    )(q, memkv, memkv, do)


STRIP = 256


def _shift_down(u, n, row):
    return jnp.where(row >= n, pltpu.roll(u, n, 0), 0.0)


def _shift_up(u, n, row):
    t = u.shape[0]
    return jnp.where(row < t - n, pltpu.roll(u, t - n, 0), 0.0)


def _conv(u, cw_ref, row):
    return ((cw_ref[3:4, :] + cw_ref[0:1, :] * _shift_down(u, 2, row)) + cw_ref[1:2, :] * _shift_down(u, 1, row)) \
        + cw_ref[2:3, :] * u


def _swiglu_fwd(u0, cw):
    t, w = u0.shape[0], u0.shape[1] // 2
    ns = w // STRIP
    col = pl.BlockSpec((t, STRIP), lambda j: (0, j))
    col_up = pl.BlockSpec((t, STRIP), lambda j: (0, ns + j))
    cws = pl.BlockSpec((SUBLANES, STRIP), lambda j: (0, j))
    cws_up = pl.BlockSpec((SUBLANES, STRIP), lambda j: (0, ns + j))

    def body(g_ref, u_ref, cg_ref, cu_ref, a_ref, at_ref):
        row = lax.broadcasted_iota(jnp.int32, (t, STRIP), 0)
        gate = _conv(g_ref[...].astype(F32), cg_ref, row)
        up = _conv(u_ref[...].astype(F32), cu_ref, row)
        act = (gate * _sigmoid(gate) * up).astype(BF16)
        a_ref[...] = act
        at_ref[...] = act.T

    return pl.pallas_call(
        body, name="swiglu_fwd", grid=(ns,), in_specs=[col, col_up, cws, cws_up],
        out_specs=[col, pl.BlockSpec((STRIP, t), lambda j: (j, 0))],
        out_shape=[jax.ShapeDtypeStruct((t, w), BF16), jax.ShapeDtypeStruct((w, t), BF16)],
        compiler_params=_params(("parallel",)),
    )(u0, u0, cw, cw)


def _swiglu_bwd(u0, cw, da):
    t, w = u0.shape[0], u0.shape[1] // 2
    ns = w // STRIP
    col = pl.BlockSpec((t, STRIP), lambda j: (0, j))
    col_up = pl.BlockSpec((t, STRIP), lambda j: (0, ns + j))
    cws = pl.BlockSpec((SUBLANES, STRIP), lambda j: (0, j))
    cws_up = pl.BlockSpec((SUBLANES, STRIP), lambda j: (0, ns + j))

    def conv_bwd(du, u0, cw_ref, row, du0_ref, du0t_ref, dcw_ref):
        du1, du2 = _shift_up(du, 1, row), _shift_up(du, 2, row)
        du0 = ((cw_ref[2:3, :] * du + cw_ref[1:2, :] * du1) + cw_ref[0:1, :] * du2).astype(BF16)
        du0_ref[...] = du0
        du0t_ref[...] = du0.T
        dcw_ref[0:1, :] = jnp.sum(du2 * u0, axis=0, keepdims=True)
        dcw_ref[1:2, :] = jnp.sum(du1 * u0, axis=0, keepdims=True)
        dcw_ref[2:3, :] = jnp.sum(du * u0, axis=0, keepdims=True)
        dcw_ref[3:4, :] = jnp.sum(du, axis=0, keepdims=True)
        dcw_ref[4:8, :] = jnp.zeros((4, STRIP), F32)

    def body(g_ref, u_ref, cg_ref, cu_ref, da_ref, dg0_ref, du0_ref, dut_ref, dcg_ref, dcu_ref):
        row = lax.broadcasted_iota(jnp.int32, (t, STRIP), 0)
        g0, up0 = g_ref[...].astype(F32), u_ref[...].astype(F32)
        gate = _conv(g0, cg_ref, row)
        up = _conv(up0, cu_ref, row)
        sg = _sigmoid(gate)
        da = da_ref[...].astype(F32)
        dgate = da * up * (sg * (1.0 + gate * (1.0 - sg)))
        dup = da * (gate * sg)
        conv_bwd(dgate, g0, cg_ref, row, dg0_ref, dut_ref.at[0], dcg_ref)
        conv_bwd(dup, up0, cu_ref, row, du0_ref, dut_ref.at[1], dcu_ref)

    return pl.pallas_call(
        body, name="swiglu_bwd", grid=(ns,), in_specs=[col, col_up, cws, cws_up, col],
        out_specs=[col, col, pl.BlockSpec((2, STRIP, t), lambda j: (0, j, 0)), cws, cws],
        out_shape=[jax.ShapeDtypeStruct((t, w), BF16), jax.ShapeDtypeStruct((t, w), BF16),
                   jax.ShapeDtypeStruct((2, w, t), BF16),
                   jax.ShapeDtypeStruct((SUBLANES, w), F32), jax.ShapeDtypeStruct((SUBLANES, w), F32)],
        compiler_params=_params(("parallel",)),
    )(u0, u0, cw, cw, da)


def _ffn_win_grad(dut, x2b, tn=512):
    t, d = x2b.shape
    sp, sw = FF_SLAB_P, FF_SLAB

    def body(a_ref, b_ref, o_ref, ob_ref):
        res = _dot(a_ref[...], b_ref[...], NN)
        o_ref[...] = res[:sw, :]
        ob_ref[...] = res[:sw, :].astype(BF16)

    o_spec = pl.BlockSpec((None, sw, tn), lambda j, n: (j, 0, n))
    return pl.pallas_call(
        body, name="mm_g_ffn_in", grid=(8, d // tn),
        in_specs=[pl.BlockSpec((None, sp, t), lambda j, n: (j // 4, j % 4, 0)),
                  pl.BlockSpec((t, tn), lambda j, n: (0, n))],
        out_specs=[o_spec, o_spec],
        out_shape=[jax.ShapeDtypeStruct((8, sw, d), F32), jax.ShapeDtypeStruct((8, sw, d), BF16)],
        compiler_params=_params(("parallel", "parallel")),
    )(dut, x2b)


def _tile2d(r, c, limit=1 << 20):
    tr, tc = r, c
    while tr * tc * 4 > limit:
        if tr % (2 * SUBLANES) == 0:
            tr //= 2
        elif tc % (2 * LANES) == 0:
            tc //= 2
        else:
            break
    return tr, tc


def _adamw_math(w, m, v, g):
    c1 = 1.0 - ADAM_B1 ** ADAM_STEP
    c2 = 1.0 - ADAM_B2 ** ADAM_STEP
    mm = ADAM_B1 * m + (1.0 - ADAM_B1) * g
    vv = ADAM_B2 * v + (1.0 - ADAM_B2) * (g * g)
    delta = -ADAM_LR * ((mm / c1) / (jnp.sqrt(vv / c2) + ADAM_EPS) + ADAM_WD * w)
    return delta, mm, vv


def _adamw(w, m, v, g, name):
    r, c = w.shape
    blk = pl.BlockSpec((r, c), lambda i: (0, 0))

    def body(w_ref, m_ref, v_ref, gi_ref, g_ref, d_ref, nm_ref, nv_ref):
        g = gi_ref[...]
        d_ref[...], nm_ref[...], nv_ref[...] = _adamw_math(w_ref[...], m_ref[...], v_ref[...], g)
        g_ref[...] = g

    return pl.pallas_call(body, name=name, grid=(1,), in_specs=[blk] * 4, out_specs=[blk] * 4,
                          out_shape=[jax.ShapeDtypeStruct((r, c), F32)] * 4,
                          compiler_params=_params(("arbitrary",)))(w, m, v, g)


def _small_reduce(gathered):
    nd, r, n = gathered.shape
    tn = 2048 if n % 2048 == 0 else n
    def body(g_ref, s_ref, t_ref):
        s = g_ref[0]
        for dv in range(1, nd):
            s = s + g_ref[dv]
        s_ref[...] = s
        t_ref[...] = jnp.broadcast_to(jnp.sum(s, axis=0, keepdims=True), (r, tn))

    return pl.pallas_call(
        body, name="small_reduce", grid=(n // tn,),
        in_specs=[pl.BlockSpec((nd, r, tn), lambda j: (0, 0, j))],
        out_specs=[pl.BlockSpec((r, tn), lambda j: (0, j))] * 2,
        out_shape=[jax.ShapeDtypeStruct((r, n), F32)] * 2, compiler_params=_params(("parallel",)),
    )(gathered)


HBM = pl.BlockSpec(memory_space=pltpu.HBM)


def _all_gather(arrs, name):
    n = len(arrs)

    def body(*refs):
        ins, outs = refs[:n], refs[n:2 * n]
        send, recv, lsem = refs[2 * n:]
        x, y, c = lax.axis_index("x"), lax.axis_index("y"), lax.axis_index("c")
        me, sib = (x, y, c), (x, y, 1 - c)
        chips = [(1 - x, y), (x, 1 - y), (1 - x, 1 - y)]

        def slot(w, p):
            return outs[w].at[4 * p[0] + 2 * p[1] + p[2]]

        def cp(w, k, block, to, src=None):
            return pltpu.make_async_remote_copy(
                src_ref=slot(w, block) if src is None else src, dst_ref=slot(w, block),
                send_sem=send.at[w * 7 + k], recv_sem=recv.at[w * 7 + k], device_id=to, device_id_type=MESH)

        mine = [pltpu.make_async_copy(ins[w], slot(w, me), lsem.at[w]) for w in range(n)]
        for m in mine:
            m.start()
        first = []
        for w in range(n):
            first.append(cp(w, 0, me, sib, src=ins[w]))
            first += [cp(w, 1 + j, me, (*chip, c), src=ins[w]) for j, chip in enumerate(chips)]
        for f in first:
            f.start()
        passed = []
        for j, chip in enumerate(chips):
            for w in range(n):
                cp(w, 1 + j, (*chip, c), me).wait_recv()
                fwd = cp(w, 4 + j, (*chip, c), sib)
                fwd.start()
                passed.append(fwd)
        for w in range(n):
            cp(w, 0, sib, me).wait_recv()
            for j, chip in enumerate(chips):
                cp(w, 4 + j, (*chip, 1 - c), me).wait_recv()
        for f in first + passed:
            f.wait_send()
        for m in mine:
            m.wait()

    return pl.pallas_call(
        body, name=name, in_specs=[HBM] * n, out_specs=[HBM] * n,
        out_shape=[jax.ShapeDtypeStruct((8,) + a.shape, a.dtype) for a in arrs],
        scratch_shapes=[pltpu.SemaphoreType.DMA((7 * n,)), pltpu.SemaphoreType.DMA((7 * n,)),
                        pltpu.SemaphoreType.DMA((n,))],
    )(*arrs)


SEM = pl.BlockSpec(memory_space=pltpu.SEMAPHORE)
ANY = pl.BlockSpec(memory_space=pl.ANY)
EFFECT = pltpu.SideEffectType.DATAFLOW_SIDE_EFFECTING
N_PEERS = 7


def _peers(x, y, c):
    return [((1 - x) if k & 4 else x, (1 - y) if k & 2 else y, (1 - c) if k & 1 else c) for k in range(1, 8)]


def _spread_copies(src_refs, land_refs, send, recv, gather):
    x, y, c = lax.axis_index("x"), lax.axis_index("y"), lax.axis_index("c")
    me = 4 * x + 2 * y + c
    copies = []
    for w in range(len(src_refs)):
        for k, (px, py, pc) in enumerate(_peers(x, y, c)):
            p = 4 * px + 2 * py + pc
            copies.append((pltpu.make_async_remote_copy(
                src_ref=src_refs[w] if gather else src_refs[w].at[p], dst_ref=land_refs[w].at[me],
                send_sem=send[w].at[k], recv_sem=recv[w].at[k], device_id=(px, py, pc), device_id_type=MESH),
                pltpu.make_async_remote_copy(
                src_ref=src_refs[w] if gather else src_refs[w].at[p], dst_ref=land_refs[w].at[p],
                send_sem=send[w].at[k], recv_sem=recv[w].at[k], device_id=(px, py, pc), device_id_type=MESH)))
    return copies


def _hbm(a):
    return pltpu.with_memory_space_constraint(a, pltpu.HBM)


def _spread_start(srcs, lands, after, gather, name):
    n = len(srcs)

    def body(*refs):
        src_refs, land_refs = refs[:n], refs[n:2 * n]
        outs = refs[2 * n + 1:]
        send, recv, token = outs[:n], outs[n:2 * n], outs[4 * n]
        for start, _ in _spread_copies(src_refs, land_refs, send, recv, gather):
            start.start()
        token[...] = jnp.zeros_like(token)

    res = pl.pallas_call(
        body, name=name,
        out_shape=tuple([pltpu.SemaphoreType.DMA((N_PEERS,))] * (2 * n)
                        + [pltpu.HBM(a.shape, a.dtype) for a in srcs] + [pltpu.HBM(a.shape, a.dtype) for a in lands]
                        + [jax.ShapeDtypeStruct((SUBLANES, LANES), F32)]),
        in_specs=[HBM] * (2 * n) + [ANY],
        out_specs=tuple([SEM] * (2 * n) + [HBM] * (2 * n) + [pl.BlockSpec(memory_space=pltpu.VMEM)]),
        input_output_aliases={i: 2 * n + i for i in range(2 * n)},
        compiler_params=pltpu.CompilerParams(has_side_effects=EFFECT),
    )(*[_hbm(a) for a in srcs], *[_hbm(a) for a in lands], after)
    return res[:n], res[n:2 * n], res[2 * n:3 * n], res[3 * n:4 * n], res[4 * n]


def _spread_wait(send, recv, srcs, lands, after, gather, name):
    n = len(srcs)
    after = list(after) if isinstance(after, (list, tuple)) else [after]

    def body(*refs):
        src_refs, land_refs = refs[:n], refs[n:2 * n]
        send_refs, recv_refs = refs[2 * n:3 * n], refs[3 * n:4 * n]
        for _, arrive in _spread_copies(src_refs, land_refs, send_refs, recv_refs, gather):
            arrive.wait_send()
            arrive.wait_recv()

    res = pl.pallas_call(
        body, name=name,
        out_shape=tuple([pltpu.HBM(a.shape, a.dtype) for a in srcs] + [pltpu.HBM(a.shape, a.dtype) for a in lands]),
        in_specs=[HBM] * (2 * n) + [SEM] * (2 * n) + [ANY] * len(after),
        out_specs=tuple([HBM] * (2 * n)),
        input_output_aliases={i: i for i in range(2 * n)},
        compiler_params=pltpu.CompilerParams(has_side_effects=EFFECT),
    )(*srcs, *lands, *send, *recv, *after)
    return res[n:]


def _landing(shape, dtype, own, me):
    return lax.dynamic_update_index_in_dim(lax.empty((8,) + shape, dtype), own, me, 0)


N_GLR = GLA_W + GLA_RANK
FF_SLAB = D_FF // 4
FF_SLAB_P = FFP // 4


TRANSPOSED = ("w_in", "ffn_w_in")


def _prepare_sub1(gath):
    w_in_t = gath["w_in"].reshape(-1, gath["w_in"].shape[2])
    w2 = jnp.concatenate([gath["gla_gate_w2"][s] for s in range(8)], axis=1)
    return {"w_a_t": jnp.pad(w_in_t[:N_GLR], ((0, HA_W - N_GLR), (0, 0))), "w_b_t": w_in_t[N_GLR:],
            "w2p": jnp.pad(w2, ((0, LANES - GLA_RANK), (0, 0)))}


def _prepare_ffn_in(g):
    f = jnp.pad(g, ((0, 0), (0, FF_SLAB_P - FF_SLAB), (0, 0)))
    return f.reshape(2 * FFP, f.shape[2])


def _prepare_ffn_out(g):
    return jnp.pad(g.reshape(4, FF_SLAB, -1), ((0, 0), (0, FF_SLAB_P - FF_SLAB), (0, 0))).reshape(FFP, -1)


def _prepare_conv(g, conv_b):
    padc = FF_SLAB_P - FF_SLAB
    cw = jnp.pad(g, ((0, 0), (0, 0), (0, padc)))
    cb = jnp.pad(conv_b.reshape(8, 1, FF_SLAB), ((0, 0), (0, 0), (0, padc)))
    rows = jnp.concatenate([cw, cb, jnp.zeros((8, 4, FF_SLAB_P), F32)], axis=1)
    return jnp.concatenate([rows[s] for s in range(8)], axis=1)


def _prepare_ffn(gath, conv_b):
    return {"w_ffn_t": _prepare_ffn_in(gath["ffn_w_in"]), "wo": _prepare_ffn_out(gath["ffn_w_out"]),
            "cw": _prepare_conv(gath["ffn_conv_w"], conv_b)}


def _unpad_ff(a):
    r = a.shape[0]
    return a.reshape(r, 4, FF_SLAB_P)[:, :, :FF_SLAB].reshape(r, D_FF)


def _grad_slabs(g):
    w_in_t = jnp.concatenate([g["w_a_t"][:N_GLR], g["w_b_t"]], axis=0)
    s = {"w_in": w_in_t.reshape(4, 2, w_in_t.shape[0] // 8, w_in_t.shape[1])}
    for n in ("w_out", "ca_wq", "ca_wo"):
        s[n] = _to_slabs(n, g[n])
    for n in ("ca_wkv", "ffn_w_in"):
        s[n] = g[n].reshape((4, 2) + g[n].shape[1:])
    wo = g["wo"].reshape(4, FF_SLAB_P, -1)[:, :FF_SLAB]
    s["ffn_w_out"] = wo.reshape(4, 2, FF_SLAB // 2, wo.shape[-1])
    return s


class _AtHand:
    def __init__(self, p):
        self.p = p
        self.token = None

    def sub2(self, after):
        return self.p

    def ffn_in(self, after):
        return self.p["w_ffn_t"]

    def ffn_out(self, after):
        return self.p["wo"]

    def grads_out(self, group, slabs):
        pass

    def small_out(self, parts):
        pass


def _local_step(x, mem, positions, target, p, small, stages=None):
    t, d = x.shape
    stages = _AtHand(p) if stages is None else stages
    w_a_t, w_b_t, w2p, cw = p["w_a_t"], p["w_b_t"], p["w2p"], p["cw"]
    tabs = _rope_tables(positions)
    xb = x.astype(BF16) if stages.token is None else (x + stages.token[0, 0]).astype(BF16)
    memb = mem.astype(BF16)

    h_a = _matmul(xb, w_a_t, "nt", F32, 1024, 640, d, "mm_h_a")
    h_b = _matmul(xb, w_b_t, "nt", F32, 1024, 1024, d, "mm_h_b")
    o_g, o_raw, s_before = _gla_fwd(h_a, w2p, small["gla_gate_b"], small["gla_norm_g"])
    qr, kr = _rope_fwd(h_b, tabs)
    o_d_b, o_d, lse_tot = _dil_fwd_all(qr, kr, h_b)
    mixin = jnp.concatenate([o_g, o_d_b], axis=1)
    wts = stages.sub2(mixin)
    mix = _matmul(mixin, wts["w_out"], "nn", F32, 1024, 1024, d, "mm_mix")
    x1, x1b, x1t = _ln_fwd(x, mix, small["ln1_g"], small["ln1_b"], "ln1_fwd", True)

    q_ca = _matmul(x1b, wts["ca_wq"], "nn", BF16, 1024, 1024, d, "mm_caq")
    kvw = wts["ca_wkv"].shape[2]
    memkv = _matmul(memb, wts["ca_wkv"], "nn", BF16, mem.shape[0], kvw, d, "mm_memkv", b_slabs=True)
    o_c, o_ct = _ca_fwd(q_ca, memkv)
    ca_out = _matmul(o_c, wts["ca_wo"], "nn", F32, 1024, 1024, d, "mm_cao")
    x2, x2b = _ln_fwd(x1, ca_out, small["ln2_g"], small["ln2_b"], "ln2_fwd", False)

    w_ffn_t = stages.ffn_in(x2b)
    u0 = _matmul(x2b, w_ffn_t, "nt", BF16, 1024, 1024, d, "mm_u0")
    act, act_t = _swiglu_fwd(u0, cw)
    wo = stages.ffn_out(act)
    ffn = _matmul(act, wo, "nn", F32, 512, 1024, FFP, "mm_ffn")

    dp3, dp3b, dg3, db3, loss_part = _ln_bwd(x2, ffn, small["ln3_g"], small["ln3_b"], target, True, "ln3_bwd")
    g_wo, g_wo16 = _matmul(act_t, dp3b, "nn", F32, 512, 1024, t, "mm_g_wo", also_bf16=True)
    dact = _matmul(dp3b, wo, "nt", BF16, 1024, 512, d, "mm_dact")
    dug, duu, du_t, dcwg, dcwu = _swiglu_bwd(u0, cw, dact)
    g_ffn_in, g_ffn_in16 = _ffn_win_grad(du_t, x2b)

    def wo_slabs(a):
        a = a.reshape(4, FF_SLAB_P, -1)[:, :FF_SLAB]
        return a.reshape(8, FF_SLAB // 2, a.shape[-1])

    def wo_own(me):
        half = FF_SLAB // 2
        return lax.dynamic_slice_in_dim(g_wo, FF_SLAB_P * (me // 2) + half * (me % 2), half, axis=0)

    sent = stages.grads_out("ffn", {"ffn_w_out": (wo_own, wo_slabs(g_wo16)), "ffn_w_in": (g_ffn_in, g_ffn_in16)})
    dx2 = _matmul(dug, w_ffn_t, "nn", F32, 512, 1024, FFP, "mm_dx2_g", resid=dp3, resid_scale=ALPHA, dep=sent)
    dx2 = _matmul(duu, w_ffn_t, "nn", F32, 512, 1024, FFP, "mm_dx2_u", resid=dx2, b_k_off=1)

    dp2, dp2b, dg2, db2 = _ln_bwd(x1, ca_out, small["ln2_g"], small["ln2_b"], dx2, False, "ln2_bwd")
    g_cao, g_cao16 = _matmul(o_ct, dp2b, "nn", F32, 512, 1024, t, "mm_g_cao", also_bf16=True)
    do_c = _matmul(dp2b, wts["ca_wo"], "nt", BF16, 1024, 1024, d, "mm_do_c")
    dq_ca, dmemkv = _ca_bwd(q_ca, memkv, do_c)
    g_caq, g_caq16 = _matmul(x1t, dq_ca, "nn", F32, 512, 1024, t, "mm_g_caq", also_bf16=True)
    g_cakv, g_cakv16 = _matmul(memb, dmemkv.astype(BF16), "tn", F32, 512, kvw, mem.shape[0], "mm_g_cakv",
                               out_slabs=True, also_bf16=True)
    dx1 = _matmul(dq_ca, wts["ca_wq"], "nt", F32, 1024, 1024, d, "mm_dx1", resid=dp2, resid_scale=ALPHA)

    dp1, dp1b, dg1, db1 = _ln_bwd(x, mix, small["ln1_g"], small["ln1_b"], dx1, False, "ln1_bwd")
    g_wout, g_wout16 = _matmul(mixin, dp1b, "tn", F32, 512, 1024, t, "mm_g_wout", also_bf16=True)

    def row_slabs(a):
        return a.reshape(8, a.shape[0] // 8, a.shape[1])

    sent = stages.grads_out("attn", {"ca_wo": (row_slabs(g_cao), row_slabs(g_cao16)),
                                     "ca_wq": (row_slabs(g_caq), row_slabs(g_caq16)), "ca_wkv": (g_cakv, g_cakv16),
                                     "w_out": (row_slabs(g_wout), row_slabs(g_wout16))})
    dmix = _matmul(dp1b, wts["w_out"], "nt", F32, 1024, 1024, d, "mm_dmix", dep=sent)
    dh_a, dw2, dgate_b, dnorm_g = _gla_bwd(h_a, w2p, small["gla_gate_b"], small["gla_norm_g"], o_raw, s_before, dmix)
    small_parts = {
        "gla_gate_b": dgate_b, "gla_norm_g": dnorm_g, "ln1_g": dg1, "ln1_b": db1, "ln2_g": dg2, "ln2_b": db2,
        "ln3_g": dg3, "ln3_b": db3,
        "conv": jnp.concatenate([_unpad_ff(dcwg), _unpad_ff(dcwu)], axis=1),
        "gla_gate_w2": dw2[:GLA_RANK],
    }
    sent = stages.small_out(small_parts)
    dq_d, dk_d, dv_d = _dil_bwd_all(qr, kr, h_b, dmix, o_d, lse_tot)
    dh_b = _dil_dh(dq_d, dk_d, dv_d, tabs)
    g_wa_t, g_wa16 = _matmul(dh_a, xb, "tn", F32, 640, 1024, t, "mm_g_wa", also_bf16=True, dep=sent)
    g_wb_t, g_wb16 = _matmul(dh_b, xb, "tn", F32, 512, 1024, t, "mm_g_wb", also_bf16=True)

    def w_in_slabs(a, b):
        full = jnp.concatenate([a[:N_GLR], b], axis=0)
        return full.reshape(8, full.shape[0] // 8, full.shape[1])

    def w_in_own(me):
        rows = (N_GLR + g_wb_t.shape[0]) // 8
        full = jnp.concatenate([g_wa_t[:N_GLR], g_wb_t], axis=0)
        return lax.dynamic_slice_in_dim(full, me * rows, rows, axis=0)

    sent = stages.grads_out("w_in", {"w_in": (w_in_own, w_in_slabs(g_wa16, g_wb16))})
    dx = _matmul(dh_a, w_a_t, "nn", F32, 512, 1024, HA_W, "mm_dx_a", resid=dp1, resid_scale=ALPHA, dep=sent)
    dx = _matmul(dh_b, w_b_t, "nn", F32, 512, 1024, HB_W, "mm_dx_b", resid=dx)

    grads = {"w_a_t": g_wa_t, "w_b_t": g_wb_t, "w_out": g_wout, "ca_wq": g_caq, "ca_wkv": g_cakv, "ca_wo": g_cao,
             "ffn_w_in": g_ffn_in, "wo": g_wo}
    return loss_part, dx, grads, small_parts


BIG = ("w_in", "w_out", "ca_wq", "ca_wkv", "ca_wo", "ffn_w_in", "ffn_w_out")
COL_SHARDED = ("w_in", "ca_wkv", "ffn_w_in")
SMALL_ORDER = ("gla_gate_b", "gla_norm_g", "ln1_g", "ln1_b", "ln2_g", "ln2_b", "ln3_g", "ln3_b")


def _gathered_full(name, g):
    if name in COL_SHARDED:
        return g.transpose(1, 0, 2).reshape(g.shape[1], 8 * g.shape[2])
    return g.reshape(8 * g.shape[1], g.shape[2])


def _to_slabs(name, full):
    if name in COL_SHARDED:
        r, cc = full.shape
        s = full.reshape(r, 8, cc // 8).transpose(1, 0, 2)
    else:
        rr, c = full.shape
        s = full.reshape(8, rr // 8, c)
    return s.reshape((4, 2) + s.shape[1:])


def kernel(x, mem, positions, w_in, gla_gate_w2, gla_gate_b, gla_norm_g, w_out, ln1_g, ln1_b, ca_wq, ca_wkv, ca_wo, ln2_g, ln2_b, ffn_w_in, ffn_conv_w, ffn_conv_b, ffn_w_out, ln3_g, ln3_b, loss_target, m_w_in, m_gla_gate_w2, m_gla_gate_b, m_gla_norm_g, m_w_out, m_ln1_g, m_ln1_b, m_ca_wq, m_ca_wkv, m_ca_wo, m_ln2_g, m_ln2_b, m_ffn_w_in, m_ffn_conv_w, m_ffn_conv_b, m_ffn_w_out, m_ln3_g, m_ln3_b, v_w_in, v_gla_gate_w2, v_gla_gate_b, v_gla_norm_g, v_w_out, v_ln1_g, v_ln1_b, v_ca_wq, v_ca_wkv, v_ca_wo, v_ln2_g, v_ln2_b, v_ffn_w_in, v_ffn_conv_w, v_ffn_conv_b, v_ffn_w_out, v_ln3_g, v_ln3_b):
    weights = dict(w_in=w_in, gla_gate_w2=gla_gate_w2, gla_gate_b=gla_gate_b, gla_norm_g=gla_norm_g, w_out=w_out,
                   ln1_g=ln1_g, ln1_b=ln1_b, ca_wq=ca_wq, ca_wkv=ca_wkv, ca_wo=ca_wo, ln2_g=ln2_g, ln2_b=ln2_b,
                   ffn_w_in=ffn_w_in, ffn_conv_w=ffn_conv_w, ffn_conv_b=ffn_conv_b, ffn_w_out=ffn_w_out,
                   ln3_g=ln3_g, ln3_b=ln3_b)
    moms = dict(w_in=(m_w_in, v_w_in), gla_gate_w2=(m_gla_gate_w2, v_gla_gate_w2), gla_gate_b=(m_gla_gate_b, v_gla_gate_b),
                gla_norm_g=(m_gla_norm_g, v_gla_norm_g), w_out=(m_w_out, v_w_out), ln1_g=(m_ln1_g, v_ln1_g),
                ln1_b=(m_ln1_b, v_ln1_b), ca_wq=(m_ca_wq, v_ca_wq), ca_wkv=(m_ca_wkv, v_ca_wkv), ca_wo=(m_ca_wo, v_ca_wo),
                ln2_g=(m_ln2_g, v_ln2_g), ln2_b=(m_ln2_b, v_ln2_b), ffn_w_in=(m_ffn_w_in, v_ffn_w_in),
                ffn_conv_w=(m_ffn_conv_w, v_ffn_conv_w), ffn_conv_b=(m_ffn_conv_b, v_ffn_conv_b),
                ffn_w_out=(m_ffn_w_out, v_ffn_w_out), ln3_g=(m_ln3_g, v_ln3_g), ln3_b=(m_ln3_b, v_ln3_b))
    order = list(weights)
    xi, yi, ci = lax.axis_index("x"), lax.axis_index("y"), lax.axis_index("c")
    me = 4 * xi + 2 * yi + ci

    def travel(n, a):
        return jnp.swapaxes(a, 1, 2) if n in TRANSPOSED else a

    shard = {n: travel(n, weights[n]).astype(BF16)[0] for n in BIG}
    first = _all_gather([shard["w_in"], gla_gate_w2.astype(BF16)[0], ffn_conv_w[0]], "ag_first")
    p = _prepare_sub1({"w_in": first[0], "gla_gate_w2": first[1]})
    p["cw"] = _prepare_conv(first[2], ffn_conv_b)
    later = ("w_out", "ca_wq", "ca_wkv", "ca_wo", "ffn_w_in", "ffn_w_out")
    srcs = [shard[n] for n in later]
    lands = [_landing(shard[n].shape, BF16, shard[n], me) for n in later]
    send, recv, srcs, lands, token = _spread_start(srcs, lands, first[0], True, "ag_rest_start")

    class stages:
        pass

    stages.token = token

    def arrived(lo, hi, after, name):
        return _spread_wait(send[lo:hi], recv[lo:hi], srcs[lo:hi], lands[lo:hi], after, True, name)

    def sub2(after):
        g = dict(zip(later[:4], arrived(0, 4, after, "ag_wait_attn")))
        w = {n: _gathered_full(n, g[n]) for n in ("w_out", "ca_wq", "ca_wo")}
        w["ca_wkv"] = g["ca_wkv"]
        return w

    stages.sub2 = sub2
    stages.ffn_in = lambda after: _prepare_ffn_in(arrived(4, 5, after, "ag_wait_ffn_in")[0])
    stages.ffn_out = lambda after: _prepare_ffn_out(arrived(5, 6, after, "ag_wait_ffn_out")[0])
    sent = {}

    def grads_out(group, slabs):
        names = list(slabs)
        srcs16 = [slabs[n][1] for n in names]
        zones = [_landing(s.shape[1:], BF16, jnp.zeros(s.shape[1:], BF16), me) for s in srcs16]
        snd, rcv, s_thru, l_thru, tok = _spread_start(srcs16, zones, srcs16[0], False, f"rs_{group}_start")
        own32 = [slabs[n][0](me) if callable(slabs[n][0]) else slabs[n][0] for n in names]
        sent[group] = (names, own32, (snd, rcv, s_thru, l_thru))
        return tok

    stages.grads_out = grads_out
    small_sent = []

    def small_out(parts):
        packed = jnp.concatenate([parts[n] for n in SMALL_ORDER] + [parts["conv"],
                                 parts["gla_gate_w2"].reshape(SUBLANES, -1)], axis=1)
        packed = jnp.pad(packed, ((0, 0), (0, (-packed.shape[1]) % 2048)))
        zone = _landing(packed.shape, F32, packed, me)
        snd, rcv, s_thru, l_thru, tok = _spread_start([packed], [zone], packed, True, "ag_small_start")
        small_sent.append((snd, rcv, s_thru, l_thru))
        return tok

    stages.small_out = small_out
    small = dict(gla_gate_b=gla_gate_b, gla_norm_g=gla_norm_g, ln1_g=ln1_g, ln1_b=ln1_b, ln2_g=ln2_g, ln2_b=ln2_b,
                 ln3_g=ln3_g, ln3_b=ln3_b)

    loss_part, dx, grads, small_parts = _local_step(x[0], mem[0], positions[0], loss_target[0], p, small, stages)
    loss = lax.psum(jnp.sum(loss_part), ("x", "y", "c"))

    out = {}
    (allp,) = _spread_wait(*small_sent[0], dx, True, "ag_small_wait")
    dev_sum, row_sum = _small_reduce(allp)

    me1 = me.reshape(1).astype(jnp.int32)

    def finish_group(group, after):
        names, own32, handles = sent[group]
        landed = _spread_wait(*handles, after, False, f"rs_{group}_wait")
        for n, own, land in zip(names, own32, landed):
            m_, v_ = moms[n]
            res4 = _adamw_direct(travel(n, weights[n]), travel(n, m_), travel(n, v_), own, land, me1, f"adamw_{n}")
            out[n] = [travel(n, a) for a in res4]

    finish_group("ffn", dx)
    finish_group("attn", dx)
    off = 0
    for n in SMALL_ORDER:
        width = weights[n].shape[1]
        g = row_sum[0:1, off:off + width]
        off += width
        m_, v_ = moms[n]
        out[n] = _adamw(weights[n], m_, v_, g, f"adamw_{n}")
    conv_g = dev_sum[:, off:off + 2 * D_FF]
    off += 2 * D_FF
    g_cb = conv_g[3:4]
    out["ffn_conv_b"] = _adamw(ffn_conv_b, m_ffn_conv_b, v_ffn_conv_b, g_cb, "adamw_ffn_conv_b")
    wsh = ffn_conv_w.shape[2]
    g_cw = lax.dynamic_slice_in_dim(conv_g[0:3], me * wsh, wsh, axis=1)
    out["ffn_conv_w"] = _adamw(ffn_conv_w[0], m_ffn_conv_w[0], v_ffn_conv_w[0], g_cw, "adamw_ffn_conv_w")
    w2_g = dev_sum[:, off:off + GLA_RANK * GLA_HEADS * GLA_DK // SUBLANES].reshape(GLA_RANK, GLA_HEADS * GLA_DK)
    wsh2 = gla_gate_w2.shape[2]
    g_w2 = lax.dynamic_slice_in_dim(w2_g, me * wsh2, wsh2, axis=1)
    out["gla_gate_w2"] = _adamw(gla_gate_w2[0], m_gla_gate_w2[0], v_gla_gate_w2[0], g_w2, "adamw_gla_gate_w2")
    finish_group("w_in", [o[1] for o in out.values()])

    def shaped(n, a):
        return a.reshape(weights[n].shape)

    res = [loss, dx[None]]
    for k in range(4):
        res += [shaped(n, out[n][k]) for n in order]
    return tuple(res)


def _adamw_direct(w, m, v, own, land, me, name):
    _, r, c = w.shape
    tr, tc = _tile2d(r, c)
    blk = pl.BlockSpec((None, tr, tc), lambda i, j, s: (0, i, j))
    if own.ndim == 2:
        mine = pl.BlockSpec((tr, tc), lambda i, j, s: (i, j))
    else:
        mine = pl.BlockSpec((None, tr, tc), lambda i, j, s: (s[0], i, j))
    slots = [pl.BlockSpec((None, tr, tc), lambda i, j, s, k=k: (k, i, j)) for k in range(8)]

    def body(s_ref, w_ref, m_ref, v_ref, p_ref, *rest):
        slot_refs, (g_ref, d_ref, nm_ref, nv_ref) = rest[:8], rest[8:]
        g = p_ref[...]
        for sr in slot_refs:
            g = g + sr[...].astype(F32)
        d_ref[...], nm_ref[...], nv_ref[...] = _adamw_math(w_ref[...], m_ref[...], v_ref[...], g)
        g_ref[...] = g

    gs = pltpu.PrefetchScalarGridSpec(num_scalar_prefetch=1, grid=(r // tr, c // tc),
                                      in_specs=[blk, blk, blk, mine] + slots, out_specs=[blk] * 4)
    return pl.pallas_call(body, name=name, grid_spec=gs, out_shape=[jax.ShapeDtypeStruct((1, r, c), F32)] * 4,
                          compiler_params=_params(("parallel", "parallel")))(me, w, m, v, own, *([land] * 8))
```

```python
import jax
import jax.numpy as jnp
from jax import lax
from jax.experimental import pallas as pl
from jax.experimental.pallas import tpu as pltpu

F32 = jnp.float32
BF16 = jnp.bfloat16
MESH = pl.DeviceIdType.MESH

D_MODEL = 2048
LN_EPS = 1e-5
GLA_HEADS = 4
GLA_DV = 256
GLA_DK = 128
GLA_RANK = 16
GLA_TAU = 16.0
GLA_CHUNK = 64
DIL_HD = 128
DIL_HEADS = 8
DIL_BAND = 128
DIL_DILATIONS = (1, 4, 16)
ROPE_THETA = 500000.0
ROPE_DIMS = 32
CA_HEADS = 4
CA_HD = 512
D_FF = 5504
ALPHA = 2.0 ** 0.25
ADAM_LR = 0.001
ADAM_B1 = 0.9
ADAM_B2 = 0.999
ADAM_EPS = 1e-08
ADAM_WD = 0.01
ADAM_STEP = 10

LANES = 128
SUBLANES = 8
VMEM_LIMIT = 56 * 1024 * 1024

GLA_W = 2 * GLA_HEADS * GLA_DK + 2 * GLA_HEADS * GLA_DV
HA_W = GLA_W + LANES
HB_W = 3 * DIL_HEADS * DIL_HD
FFP = 5632
NEG = -1e30


def _params(sem):
    return pltpu.CompilerParams(dimension_semantics=sem, vmem_limit_bytes=VMEM_LIMIT)


def _sigmoid(x):
    return 1.0 / (1.0 + jnp.exp(-x))


def _dot(a, b, dn, precision=None):
    return lax.dot_general(a, b, (dn, ((), ())), preferred_element_type=F32, precision=precision)


NN = ((1,), (0,))
NT = ((1,), (1,))
TN = ((0,), (0,))


def _bf(v):
    return v if v.dtype == BF16 else v.astype(BF16)


def _matmul(a, b, kind, out_dtype, tm, tn, tk, name, resid=None, resid_scale=1.0, b_k_off=0, b_slabs=False,
            out_slabs=False, also_bf16=False, dep=None):
    if b_slabs:
        assert kind != "nt" and b.shape[2] == tn
        k2, n = b.shape[1], b.shape[0] * tn
    elif kind == "nt":
        n, k2 = b.shape
    else:
        k2, n = b.shape
    (k, m) = a.shape if kind == "tn" else a.shape[::-1]
    assert k2 >= k and (k2 == k or not b_slabs) and m % tm == 0 and n % tn == 0 and k % tk == 0, \
        (name, a.shape, b.shape, tm, tn, tk)
    nk = k // tk
    dn = {"nn": NN, "nt": NT, "tn": TN}[kind]
    a_spec = pl.BlockSpec((tk, tm), lambda i, j, kk: (kk, i)) if kind == "tn" else pl.BlockSpec((tm, tk), lambda i, j, kk: (i, kk))
    if b_slabs:
        b_spec = pl.BlockSpec((None, tk, tn), lambda i, j, kk: (j, kk, 0))
    elif kind == "nt":
        b_spec = pl.BlockSpec((tn, tk), lambda i, j, kk: (j, kk + b_k_off))
    else:
        b_spec = pl.BlockSpec((tk, tn), lambda i, j, kk: (kk + b_k_off, j))
    if out_slabs:
        o_spec = pl.BlockSpec((None, tm, tn), lambda i, j, kk: (j, i, 0))
        o_shape = (n // tn, m, tn)
    else:
        o_spec = pl.BlockSpec((tm, tn), lambda i, j, kk: (i, j))
        o_shape = (m, n)
    has_resid = resid is not None

    n_in = 2 + int(has_resid) + int(dep is not None)

    def body(*refs):
        a_ref, b_ref = refs[:2]
        r_ref = refs[2] if has_resid else None
        o_ref = refs[n_in]
        ob_ref = refs[n_in + 1] if also_bf16 else None
        part = _dot(_bf(a_ref[...]), _bf(b_ref[...]), dn)

        def finish(acc):
            if has_resid:
                acc = acc + resid_scale * r_ref[...].astype(F32)
            o_ref[...] = acc.astype(out_dtype)
            if also_bf16:
                ob_ref[...] = acc.astype(BF16)

        if nk == 1:
            finish(part)
        else:
            acc_ref = refs[-1]
            kk = pl.program_id(2)

            @pl.when(kk == 0)
            def _():
                acc_ref[...] = part

            @pl.when(kk > 0)
            def _():
                acc_ref[...] += part

            @pl.when(kk == nk - 1)
            def _():
                finish(acc_ref[...])

    in_specs = [a_spec, b_spec] + ([o_spec] if has_resid else [])
    args = (a, b) + ((resid,) if has_resid else ())
    if dep is not None:
        in_specs.append(pl.BlockSpec((SUBLANES, LANES), lambda i, j, kk: (0, 0)))
        args += (dep,)
    o_struct = jax.ShapeDtypeStruct(o_shape, out_dtype)
    return pl.pallas_call(
        body, name=name, out_shape=[o_struct, jax.ShapeDtypeStruct(o_shape, BF16)] if also_bf16 else o_struct,
        grid=(m // tm, n // tn, nk), in_specs=in_specs, out_specs=[o_spec, o_spec] if also_bf16 else o_spec,
        scratch_shapes=[pltpu.VMEM((tm, tn), F32)] if nk > 1 else [],
        compiler_params=_params(("parallel", "parallel", "arbitrary")),
    )(*args)


def _ln_core(xres, f):
    p = ALPHA * xres + f
    mu = jnp.mean(p, axis=-1, keepdims=True)
    xc = p - mu
    var = jnp.mean(xc * xc, axis=-1, keepdims=True)
    rstd = lax.rsqrt(var + LN_EPS)
    return xc * rstd, rstd


def _rows8(v):
    r, c = v.shape
    return jnp.sum(v.reshape(r // SUBLANES, SUBLANES, c), axis=0)


def _ln_fwd(xres, f, g, b, name, transposed, tr=256):
    t, d = xres.shape
    row = pl.BlockSpec((tr, d), lambda i: (i, 0))
    vec = pl.BlockSpec((1, d), lambda i: (0, 0))

    def body(x_ref, f_ref, g_ref, b_ref, y_ref, yb_ref, *yt_ref):
        xhat, _ = _ln_core(x_ref[...], f_ref[...])
        y = xhat * g_ref[...] + b_ref[...]
        y_ref[...] = y
        yb = y.astype(BF16)
        yb_ref[...] = yb
        if transposed:
            yt_ref[0][...] = yb.T

    out_specs = [row, row] + ([pl.BlockSpec((d, tr), lambda i: (0, i))] if transposed else [])
    out_shape = [jax.ShapeDtypeStruct((t, d), F32), jax.ShapeDtypeStruct((t, d), BF16)] \
        + ([jax.ShapeDtypeStruct((d, t), BF16)] if transposed else [])
    return pl.pallas_call(
        body, name=name, grid=(t // tr,), in_specs=[row, row, vec, vec], out_specs=out_specs, out_shape=out_shape,
        compiler_params=_params(("parallel",)),
    )(xres, f, g, b)


def _ln_bwd(xres, f, g, b, dy_or_target, loss_head, name, tr=256):
    t, d = xres.shape
    row = pl.BlockSpec((tr, d), lambda i: (i, 0))
    vec = pl.BlockSpec((1, d), lambda i: (0, 0))
    acc = pl.BlockSpec((SUBLANES, d), lambda i: (0, 0))
    lacc = pl.BlockSpec((SUBLANES, LANES), lambda i: (0, 0))

    def body(x_ref, f_ref, g_ref, b_ref, t_ref, dp_ref, dpb_ref, dg_ref, db_ref, *rest):
        i = pl.program_id(0)
        xhat, rstd = _ln_core(x_ref[...], f_ref[...])
        if loss_head:
            err = xhat * g_ref[...] + b_ref[...] - t_ref[...]
            dy = err * (1.0 / d)
            sq = err * err
            lanes = sq[:, :LANES]
            for kk in range(1, d // LANES):
                lanes = lanes + sq[:, kk * LANES:(kk + 1) * LANES]
            lpart = _rows8(lanes) * (0.5 / d)
        else:
            dy = t_ref[...]
        dxh = dy * g_ref[...]
        m1 = jnp.mean(dxh, axis=-1, keepdims=True)
        m2 = jnp.mean(dxh * xhat, axis=-1, keepdims=True)
        dp = rstd * (dxh - m1 - xhat * m2)
        dp_ref[...] = dp
        dpb_ref[...] = dp.astype(BF16)
        dgp = _rows8(dy * xhat)
        dbp = _rows8(dy)

        @pl.when(i == 0)
        def _():
            dg_ref[...] = dgp
            db_ref[...] = dbp
            if loss_head:
                rest[0][...] = lpart

        @pl.when(i > 0)
        def _():
            dg_ref[...] += dgp
            db_ref[...] += dbp
            if loss_head:
                rest[0][...] += lpart

    out_shape = [jax.ShapeDtypeStruct((t, d), F32), jax.ShapeDtypeStruct((t, d), BF16),
                 jax.ShapeDtypeStruct((SUBLANES, d), F32), jax.ShapeDtypeStruct((SUBLANES, d), F32)]
    out_specs = [row, row, acc, acc]
    if loss_head:
        out_shape.append(jax.ShapeDtypeStruct((SUBLANES, LANES), F32))
        out_specs.append(lacc)
    return pl.pallas_call(
        body, name=name, grid=(t // tr,), in_specs=[row, row, vec, vec, row], out_specs=out_specs,
        out_shape=out_shape, compiler_params=_params(("arbitrary",)),
    )(xres, f, g, b, dy_or_target)


def _gla_gates(glr, w2, gb):
    z = _dot(_bf(glr), w2, NN) + gb
    lg = (jnp.minimum(z, 0.0) - jnp.log(1.0 + jnp.exp(-jnp.abs(z)))) * (1.0 / GLA_TAU)
    c = z.shape[0]
    ri = lax.broadcasted_iota(jnp.int32, (c, c), 0)
    ci = lax.broadcasted_iota(jnp.int32, (c, c), 1)
    tri = (ci <= ri).astype(F32)
    bcum = _dot(tri, lg, NN, precision=lax.Precision.HIGHEST)
    blast = jnp.sum(lg, axis=0, keepdims=True)
    return z, bcum, blast, tri


def _gla_specs(t):
    c = GLA_CHUNK
    return c, t // c


def _gla_fwd(h_a, w2p, gate_b, norm_g):
    t = h_a.shape[0]
    c, n = _gla_specs(t)
    hk, hv = GLA_HEADS * GLA_DK, GLA_HEADS * GLA_DV
    scale = GLA_DK ** -0.5

    def body(q_ref, k_ref, v_ref, r_ref, glr_ref, w2_ref, gb_ref, ng_ref, og_ref, oraw_ref, sb_ref, st_ref):
        i = pl.program_id(0)

        @pl.when(i == 0)
        def _():
            st_ref[...] = jnp.zeros_like(st_ref)

        _, bcum, blast, _ = _gla_gates(glr_ref[...], w2_ref[...], gb_ref[...])
        ri = lax.broadcasted_iota(jnp.int32, (c, c), 0)
        ci = lax.broadcasted_iota(jnp.int32, (c, c), 1)
        causal = ci <= ri
        for h in range(GLA_HEADS):
            ks = slice(h * GLA_DK, (h + 1) * GLA_DK)
            vs = slice(h * GLA_DV, (h + 1) * GLA_DV)
            b_h, bl_h = bcum[:, ks], blast[:, ks]
            q_h, k_h = q_ref[:, ks], k_ref[:, ks]
            v_h = _bf(v_ref[:, vs])
            qi = _bf(q_h * scale * jnp.exp(b_h))
            ki = _bf(k_h * jnp.exp(-b_h))
            ke = _bf(k_h * jnp.exp(bl_h - b_h))
            st = st_ref[h]
            sb_ref[0, h] = st
            a = jnp.where(causal, _dot(qi, ki, NT), 0.0)
            o = _dot(_bf(a), v_h, NN) + _dot(qi, _bf(st), NT)
            st_ref[h] = st * jnp.exp(bl_h) + _dot(v_h, ke, TN)
            oraw_ref[:, vs] = o
            mu = jnp.mean(o, axis=-1, keepdims=True)
            oc = o - mu
            var = jnp.mean(oc * oc, axis=-1, keepdims=True)
            xh = oc * lax.rsqrt(var + LN_EPS)
            r_h = r_ref[:, vs]
            og_ref[:, vs] = (xh * ng_ref[:, vs] * (r_h * _sigmoid(r_h))).astype(BF16)

    return pl.pallas_call(
        body, name="gla_fwd", grid=(n,),
        in_specs=[pl.BlockSpec((c, hk), lambda i: (i, 0)), pl.BlockSpec((c, hk), lambda i: (i, 1)),
                  pl.BlockSpec((c, hv), lambda i: (i, 1)), pl.BlockSpec((c, hv), lambda i: (i, 2)),
                  pl.BlockSpec((c, LANES), lambda i: (i, GLA_W // LANES)),
                  pl.BlockSpec((LANES, hk), lambda i: (0, 0)), pl.BlockSpec((1, hk), lambda i: (0, 0)),
                  pl.BlockSpec((1, hv), lambda i: (0, 0))],
        out_specs=[pl.BlockSpec((c, hv), lambda i: (i, 0)), pl.BlockSpec((c, hv), lambda i: (i, 0)),
                   pl.BlockSpec((1, GLA_HEADS, GLA_DV, GLA_DK), lambda i: (i, 0, 0, 0))],
        out_shape=[jax.ShapeDtypeStruct((t, hv), BF16), jax.ShapeDtypeStruct((t, hv), F32),
                   jax.ShapeDtypeStruct((n, GLA_HEADS, GLA_DV, GLA_DK), F32)],
        scratch_shapes=[pltpu.VMEM((GLA_HEADS, GLA_DV, GLA_DK), F32)],
        compiler_params=_params(("arbitrary",)),
    )(h_a, h_a, h_a, h_a, h_a, w2p, gate_b, norm_g)


def _gla_bwd(h_a, w2p, gate_b, norm_g, o_raw, s_before, dmix):
    t = h_a.shape[0]
    c, n = _gla_specs(t)
    hk, hv = GLA_HEADS * GLA_DK, GLA_HEADS * GLA_DV
    scale = GLA_DK ** -0.5
    rev = lambda i: n - 1 - i

    def body(q_ref, k_ref, v_ref, r_ref, glr_ref, w2_ref, gb_ref, ng_ref, oraw_ref, sb_ref, do_ref,
             dh_ref, dw2_ref, dgb_ref, dng_ref, dst_ref):
        i = pl.program_id(0)

        @pl.when(i == 0)
        def _():
            dst_ref[...] = jnp.zeros_like(dst_ref)

        glr = glr_ref[...]
        z, bcum, blast, tri = _gla_gates(glr, w2_ref[...], gb_ref[...])
        ri = lax.broadcasted_iota(jnp.int32, (c, c), 0)
        ci = lax.broadcasted_iota(jnp.int32, (c, c), 1)
        causal = ci <= ri
        dlg_parts = []
        dng_parts = []
        for h in range(GLA_HEADS):
            ks = slice(h * GLA_DK, (h + 1) * GLA_DK)
            vs = slice(h * GLA_DV, (h + 1) * GLA_DV)
            o = oraw_ref[:, vs]
            mu = jnp.mean(o, axis=-1, keepdims=True)
            oc = o - mu
            var = jnp.mean(oc * oc, axis=-1, keepdims=True)
            rstd = lax.rsqrt(var + LN_EPS)
            xh = oc * rstd
            r_h = r_ref[:, vs]
            sg = _sigmoid(r_h)
            silu = r_h * sg
            dout = do_ref[:, vs]
            ng = ng_ref[:, vs]
            dng_parts.append(_rows8(dout * xh * silu))
            dr = dout * xh * ng * (sg * (1.0 + r_h * (1.0 - sg)))
            dxh = dout * ng * silu
            m1 = jnp.mean(dxh, axis=-1, keepdims=True)
            m2 = jnp.mean(dxh * xh, axis=-1, keepdims=True)
            do_raw = _bf(rstd * (dxh - m1 - xh * m2))
            b_h, bl_h = bcum[:, ks], blast[:, ks]
            q_h, k_h = q_ref[:, ks], k_ref[:, ks]
            v_h = _bf(v_ref[:, vs])
            eb, enb, eend = jnp.exp(b_h), jnp.exp(-b_h), jnp.exp(bl_h - b_h)
            decay = jnp.exp(bl_h)
            qi_f, ki_f, ke_f = q_h * scale * eb, k_h * enb, k_h * eend
            qi, ki, ke = _bf(qi_f), _bf(ki_f), _bf(ke_f)
            st = sb_ref[0, h]
            dst = dst_ref[h]
            dst_b = _bf(dst)
            a = _bf(jnp.where(causal, _dot(qi, ki, NT), 0.0))
            da = _bf(jnp.where(causal, _dot(do_raw, v_h, NT), 0.0))
            dv = _dot(a, do_raw, TN) + _dot(ke, dst_b, NT)
            dqi = _dot(da, ki, NN) + _dot(do_raw, _bf(st), NN)
            dki = _dot(da, qi, TN)
            dke = _dot(v_h, dst_b, NN)
            dst_ref[h] = _dot(do_raw, qi, TN) + dst * decay
            dbl = decay * jnp.sum(st * dst, axis=0, keepdims=True) + jnp.sum(dke * ke_f, axis=0, keepdims=True)
            dbc = dqi * qi_f - dki * ki_f - dke * ke_f
            dlg_parts.append(_dot(tri, dbc, TN, precision=lax.Precision.HIGHEST) + dbl)
            dh_ref[:, ks] = (dqi * eb * scale).astype(BF16)
            dh_ref[:, hk + h * GLA_DK: hk + (h + 1) * GLA_DK] = (dki * enb + dke * eend).astype(BF16)
            dh_ref[:, 2 * hk + h * GLA_DV: 2 * hk + (h + 1) * GLA_DV] = dv.astype(BF16)
            dh_ref[:, 2 * hk + hv + h * GLA_DV: 2 * hk + hv + (h + 1) * GLA_DV] = dr.astype(BF16)
        dlg = jnp.concatenate(dlg_parts, axis=1)
        dz = dlg * (1.0 / GLA_TAU) * _sigmoid(-z)
        dz_b = _bf(dz)
        dh_ref[:, GLA_W:] = _dot(dz_b, w2_ref[...], NT).astype(BF16)
        dw2p = _dot(_bf(glr), dz_b, TN)
        dgbp = _rows8(dz)
        dngp = jnp.concatenate(dng_parts, axis=1)

        @pl.when(i == 0)
        def _():
            dw2_ref[...] = dw2p
            dgb_ref[...] = dgbp
            dng_ref[...] = dngp

        @pl.when(i > 0)
        def _():
            dw2_ref[...] += dw2p
            dgb_ref[...] += dgbp
            dng_ref[...] += dngp

    return pl.pallas_call(
        body, name="gla_bwd", grid=(n,),
        in_specs=[pl.BlockSpec((c, hk), lambda i: (rev(i), 0)), pl.BlockSpec((c, hk), lambda i: (rev(i), 1)),
                  pl.BlockSpec((c, hv), lambda i: (rev(i), 1)), pl.BlockSpec((c, hv), lambda i: (rev(i), 2)),
                  pl.BlockSpec((c, LANES), lambda i: (rev(i), GLA_W // LANES)),
                  pl.BlockSpec((LANES, hk), lambda i: (0, 0)), pl.BlockSpec((1, hk), lambda i: (0, 0)),
                  pl.BlockSpec((1, hv), lambda i: (0, 0)),
                  pl.BlockSpec((c, hv), lambda i: (rev(i), 0)),
                  pl.BlockSpec((1, GLA_HEADS, GLA_DV, GLA_DK), lambda i: (rev(i), 0, 0, 0)),
                  pl.BlockSpec((c, hv), lambda i: (rev(i), 0))],
        out_specs=[pl.BlockSpec((c, HA_W), lambda i: (rev(i), 0)),
                   pl.BlockSpec((LANES, hk), lambda i: (0, 0)),
                   pl.BlockSpec((SUBLANES, hk), lambda i: (0, 0)),
                   pl.BlockSpec((SUBLANES, hv), lambda i: (0, 0))],
        out_shape=[jax.ShapeDtypeStruct((t, HA_W), BF16), jax.ShapeDtypeStruct((LANES, hk), F32),
                   jax.ShapeDtypeStruct((SUBLANES, hk), F32), jax.ShapeDtypeStruct((SUBLANES, hv), F32)],
        scratch_shapes=[pltpu.VMEM((GLA_HEADS, GLA_DV, GLA_DK), F32)],
        compiler_params=_params(("arbitrary",)),
    )(h_a, h_a, h_a, h_a, h_a, w2p, gate_b, norm_g, o_raw, s_before, dmix)


def _rope_tables(positions):
    half = ROPE_DIMS // 2
    inv_freq = ROPE_THETA ** (-jnp.arange(0, ROPE_DIMS, 2, dtype=F32) / ROPE_DIMS)
    ang = positions.astype(F32).reshape(-1, 1) * inv_freq
    cos, sin = jnp.cos(ang), jnp.sin(ang)
    t = cos.shape[0]
    one = jnp.ones((t, DIL_HD - ROPE_DIMS), F32)
    zero = jnp.zeros((t, DIL_HD - ROPE_DIMS), F32)
    zh = jnp.zeros((t, half), F32)
    return (jnp.concatenate([cos, cos, one], axis=1), jnp.concatenate([-sin, zh, zero], axis=1),
            jnp.concatenate([zh, sin, zero], axis=1))


def _rope_apply(x, c, s1, s2):
    half = ROPE_DIMS // 2
    return x * c + pltpu.roll(x, DIL_HD - half, 1) * s1 + pltpu.roll(x, half, 1) * s2


def _rope_apply_t(dy, c, s1, s2):
    half = ROPE_DIMS // 2
    return dy * c + pltpu.roll(dy * s1, half, 1) + pltpu.roll(dy * s2, DIL_HD - half, 1)


def _rope_fwd(h_b, tabs, tr=256):
    t = h_b.shape[0]
    w = DIL_HEADS * DIL_HD
    scale = DIL_HD ** -0.5
    tab = pl.BlockSpec((tr, DIL_HD), lambda i: (i, 0))
    outb = pl.BlockSpec((tr, w), lambda i: (i, 0))

    def body(q_ref, k_ref, c_ref, s1_ref, s2_ref, qo_ref, ko_ref):
        c, s1, s2 = c_ref[...], s1_ref[...], s2_ref[...]
        for h in range(DIL_HEADS):
            hs = slice(h * DIL_HD, (h + 1) * DIL_HD)
            qo_ref[:, hs] = _rope_apply(q_ref[:, hs] * scale, c, s1, s2)
            ko_ref[:, hs] = _rope_apply(k_ref[:, hs], c, s1, s2)

    return pl.pallas_call(
        body, name="rope_fwd", grid=(t // tr,),
        in_specs=[pl.BlockSpec((tr, w), lambda i: (i, 0)), pl.BlockSpec((tr, w), lambda i: (i, 1)), tab, tab, tab],
        out_specs=[outb, outb],
        out_shape=[jax.ShapeDtypeStruct((t, w), F32)] * 2,
        compiler_params=_params(("parallel",)),
    )(h_b, h_b, *tabs)


def _dil_dh(dq, dk, dv, tabs, tr=256):
    t, w = dq.shape
    scale = DIL_HD ** -0.5
    tab = pl.BlockSpec((tr, DIL_HD), lambda i: (i, 0))
    inb = pl.BlockSpec((tr, w), lambda i: (i, 0))

    def body(dq_ref, dk_ref, dv_ref, c_ref, s1_ref, s2_ref, o_ref):
        c, s1, s2 = c_ref[...], s1_ref[...], s2_ref[...]
        for h in range(DIL_HEADS):
            hs = slice(h * DIL_HD, (h + 1) * DIL_HD)
            o_ref[:, h * DIL_HD:(h + 1) * DIL_HD] = (_rope_apply_t(dq_ref[:, hs], c, s1, s2) * scale).astype(BF16)
            o_ref[:, w + h * DIL_HD: w + (h + 1) * DIL_HD] = _rope_apply_t(dk_ref[:, hs], c, s1, s2).astype(BF16)
        o_ref[:, 2 * w:] = dv_ref[...].astype(BF16)

    return pl.pallas_call(
        body, name="dil_dh", grid=(t // tr,), in_specs=[inb] * 3 + [tab] * 3,
        out_specs=pl.BlockSpec((tr, 3 * w), lambda i: (i, 0)),
        out_shape=jax.ShapeDtypeStruct((t, 3 * w), BF16), compiler_params=_params(("parallel",)),
    )(dq, dk, dv, *tabs)


def _band_masks(not_first):
    r = lax.broadcasted_iota(jnp.int32, (DIL_BAND, 2 * DIL_BAND), 0)
    c = lax.broadcasted_iota(jnp.int32, (DIL_BAND, 2 * DIL_BAND), 1)
    nf = jnp.full((DIL_BAND, 2 * DIL_BAND), not_first, jnp.int32)
    look_back = jnp.logical_and(jnp.logical_and(c < DIL_BAND, c >= r), nf > 0)
    own_band = jnp.logical_and(c >= DIL_BAND, (c - DIL_BAND) <= r)
    return jnp.logical_or(look_back, own_band)


def _gather_rows(dst_ref, src_ref, t, d, cast=None):
    n = t // d
    for r in range(d):
        v = src_ref[pl.ds(r, n, stride=d), :] if d > 1 else src_ref[...]
        dst_ref[r * n:(r + 1) * n, :] = v if cast is None else v.astype(cast)


def _tri_mask():
    r = lax.broadcasted_iota(jnp.int32, (DIL_BAND, DIL_BAND), 0)
    c = lax.broadcasted_iota(jnp.int32, (DIL_BAND, DIL_BAND), 1)
    return c <= r


def _dil_fwd_all(qr, kr, h_b):
    t = qr.shape[0]
    nbands = t // DIL_BAND
    nbr = len(DIL_DILATIONS)
    hoff = DIL_HEADS

    def col(off):
        return pl.BlockSpec((t, DIL_HD), lambda h: (0, off + h), pipeline_mode=pl.Buffered(1))

    outb = pl.BlockSpec((t, DIL_HD), lambda h: (0, h))

    def body(q_ref, k_ref, v_ref, ob_ref, of_ref, lt_ref, qs, ks, vs, os_, ls_, *br):
        obr, lbr = br[:nbr], br[nbr:]
        for bi, d in enumerate(DIL_DILATIONS):
            n = t // d
            nb = n // DIL_BAND
            _gather_rows(qs, q_ref, t, d, BF16)
            _gather_rows(ks, k_ref, t, d, BF16)
            _gather_rows(vs, v_ref, t, d, BF16)
            s = jnp.where(_tri_mask(), _dot(qs[0:DIL_BAND, :], ks[0:DIL_BAND, :], NT), NEG)
            m = jnp.max(s, axis=-1, keepdims=True)
            pr = jnp.exp(s - m)
            den = jnp.sum(pr, axis=-1, keepdims=True)
            os_[0:DIL_BAND, :] = _dot(_bf(pr), vs[0:DIL_BAND, :], NN) / den
            ls_[0:DIL_BAND, :] = jnp.broadcast_to(m + jnp.log(den), (DIL_BAND, DIL_HD))

            def band(b, carry, nb=nb):
                st = pl.multiple_of((b - 1) * DIL_BAND, DIL_BAND)
                cur = pl.ds(st + DIL_BAND, DIL_BAND)
                both = pl.ds(st, 2 * DIL_BAND)
                not_first = ((b % nb) != 0).astype(jnp.int32)
                s = jnp.where(_band_masks(not_first), _dot(qs[cur, :], ks[both, :], NT), NEG)
                m = jnp.max(s, axis=-1, keepdims=True)
                pr = jnp.exp(s - m)
                den = jnp.sum(pr, axis=-1, keepdims=True)
                os_[cur, :] = _dot(_bf(pr), vs[both, :], NN) / den
                ls_[cur, :] = jnp.broadcast_to(m + jnp.log(den), (DIL_BAND, DIL_HD))
                return carry

            lax.fori_loop(1, nbands, band, 0, unroll=16)
            for r in range(d):
                dst = pl.ds(r, n, stride=d) if d > 1 else slice(None)
                obr[bi][dst, :] = os_[r * n:(r + 1) * n, :]
                lbr[bi][dst, :] = ls_[r * n:(r + 1) * n, :]
        rows = 512
        for c0 in range(0, t, rows):
            sl = slice(c0, c0 + rows)
            la, lb, lc = lbr[0][sl, :], lbr[1][sl, :], lbr[2][sl, :]
            m = jnp.maximum(jnp.maximum(la, lb), lc)
            ea, eb, ec = jnp.exp(la - m), jnp.exp(lb - m), jnp.exp(lc - m)
            den = ea + eb + ec
            o = (ea * obr[0][sl, :] + eb * obr[1][sl, :] + ec * obr[2][sl, :]) / den
            ob_ref[sl, :] = o.astype(BF16)
            of_ref[sl, :] = o
            lt_ref[sl, :] = m + jnp.log(den)

    w = DIL_HEADS * DIL_HD
    vm = lambda dt: pltpu.VMEM((t, DIL_HD), dt)
    return pl.pallas_call(
        body, name="dil_fwd", grid=(DIL_HEADS,), in_specs=[col(0), col(0), col(2 * hoff)],
        out_specs=[outb, outb, outb],
        out_shape=[jax.ShapeDtypeStruct((t, w), BF16), jax.ShapeDtypeStruct((t, w), F32),
                   jax.ShapeDtypeStruct((t, w), F32)],
        scratch_shapes=[vm(BF16)] * 3 + [vm(F32)] * 2 + [vm(F32)] * (2 * nbr),
        compiler_params=_params(("parallel",)),
    )(qr, kr, h_b)


def _dil_bwd_all(qr, kr, h_b, dmix, o_d, lse_tot):
    t = qr.shape[0]
    nbands = t // DIL_BAND
    hoff = DIL_HEADS

    def col(off):
        return pl.BlockSpec((t, DIL_HD), lambda h: (0, off + h), pipeline_mode=pl.Buffered(1))

    outb = pl.BlockSpec((t, DIL_HD), lambda h: (0, h))

    def body(q_ref, k_ref, v_ref, do_ref, o_ref, l_ref, dq_ref, dk_ref, dv_ref,
             qs, ks, vs, dos, lss, dds, dqs, acck, accv, ddt):
        rows = 512
        for c0 in range(0, t, rows):
            prod = do_ref[c0:c0 + rows, :] * o_ref[c0:c0 + rows, :]
            ddt[c0:c0 + rows, :] = jnp.broadcast_to(jnp.sum(prod, axis=-1, keepdims=True), (rows, DIL_HD))
        for bi, d in enumerate(DIL_DILATIONS):
            n = t // d
            nb = n // DIL_BAND
            _gather_rows(qs, q_ref, t, d, BF16)
            _gather_rows(ks, k_ref, t, d, BF16)
            _gather_rows(vs, v_ref, t, d, BF16)
            _gather_rows(dos, do_ref, t, d, BF16)
            _gather_rows(lss, l_ref, t, d)
            _gather_rows(dds, ddt, t, d)
            b0 = slice(0, DIL_BAND)
            s = jnp.where(_tri_mask(), _dot(qs[b0, :], ks[b0, :], NT), NEG)
            pr = jnp.exp(s - lss[b0, :])
            ds = _bf(pr * (_dot(dos[b0, :], vs[b0, :], NT) - dds[b0, :]))
            dqs[b0, :] = _dot(ds, ks[b0, :], NN)
            acck[DIL_BAND:2 * DIL_BAND, :] = _dot(ds, qs[b0, :], TN)
            accv[DIL_BAND:2 * DIL_BAND, :] = _dot(_bf(pr), dos[b0, :], TN)

            def band(b, carry, nb=nb):
                st = pl.multiple_of((b - 1) * DIL_BAND, DIL_BAND)
                cur = pl.ds(st + DIL_BAND, DIL_BAND)
                both = pl.ds(st, 2 * DIL_BAND)
                back_rows = pl.ds(st + DIL_BAND, DIL_BAND)
                own_rows = pl.ds(st + 2 * DIL_BAND, DIL_BAND)
                not_first = ((b % nb) != 0).astype(jnp.int32)
                qb, dob, lb, ddb = qs[cur, :], dos[cur, :], lss[cur, :], dds[cur, :]
                kcat, vcat = ks[both, :], vs[both, :]
                s = jnp.where(_band_masks(not_first), _dot(qb, kcat, NT), NEG)
                pr = jnp.exp(s - jnp.concatenate([lb, lb], axis=1))
                ds = _bf(pr * (_dot(dob, vcat, NT) - jnp.concatenate([ddb, ddb], axis=1)))
                dqs[cur, :] = _dot(ds, kcat, NN)
                dkk = _dot(ds, qb, TN)
                dvv = _dot(_bf(pr), dob, TN)
                acck[back_rows, :] += dkk[:DIL_BAND]
                accv[back_rows, :] += dvv[:DIL_BAND]
                acck[own_rows, :] = dkk[DIL_BAND:]
                accv[own_rows, :] = dvv[DIL_BAND:]
                return carry

            lax.fori_loop(1, nbands, band, 0, unroll=8)
            for r in range(d):
                lo = r * n
                if d == 1:
                    dq_ref[...] = dqs[...]
                    dk_ref[...] = acck[DIL_BAND:DIL_BAND + t, :]
                    dv_ref[...] = accv[DIL_BAND:DIL_BAND + t, :]
                else:
                    dst = pl.ds(r, n, stride=d)
                    dq_ref[dst, :] = dq_ref[dst, :] + dqs[lo:lo + n, :]
                    dk_ref[dst, :] = dk_ref[dst, :] + acck[DIL_BAND + lo:DIL_BAND + lo + n, :]
                    dv_ref[dst, :] = dv_ref[dst, :] + accv[DIL_BAND + lo:DIL_BAND + lo + n, :]

    w = DIL_HEADS * DIL_HD
    vm = lambda dt, extra=0: pltpu.VMEM((t + extra, DIL_HD), dt)
    return pl.pallas_call(
        body, name="dil_bwd", grid=(DIL_HEADS,),
        in_specs=[col(0), col(0), col(2 * hoff), col(hoff), col(0), col(0)], out_specs=[outb] * 3,
        out_shape=[jax.ShapeDtypeStruct((t, w), F32)] * 3,
        scratch_shapes=[vm(BF16)] * 4 + [vm(F32)] * 3 + [vm(F32, DIL_BAND)] * 2 + [vm(F32)],
        compiler_params=_params(("parallel",)),
    )(qr, kr, h_b, dmix, o_d, lse_tot)


def _ca_fwd(q, memkv, tq=512):
    t, d = q.shape
    m = memkv.shape[0]
    scale = CA_HD ** -0.5

    def body(q_ref, k_ref, v_ref, o_ref, ot_ref):
        for h in range(CA_HEADS):
            hs = slice(h * CA_HD, (h + 1) * CA_HD)
            s = _dot(q_ref[:, hs], k_ref[:, hs], NT) * scale
            p = jnp.exp(s - jnp.max(s, axis=-1, keepdims=True))
            p = p / jnp.sum(p, axis=-1, keepdims=True)
            o = _dot(_bf(p), v_ref[:, hs], NN).astype(BF16)
            o_ref[:, hs] = o
            ot_ref[hs, :] = o.T

    return pl.pallas_call(
        body, name="ca_fwd", grid=(t // tq,),
        in_specs=[pl.BlockSpec((tq, d), lambda i: (i, 0)), pl.BlockSpec((m, d), lambda i: (0, 0)),
                  pl.BlockSpec((m, d), lambda i: (0, 1))],
        out_specs=[pl.BlockSpec((tq, d), lambda i: (i, 0)), pl.BlockSpec((d, tq), lambda i: (0, i))],
        out_shape=[jax.ShapeDtypeStruct((t, d), BF16), jax.ShapeDtypeStruct((d, t), BF16)],
        compiler_params=_params(("parallel",)),
    )(q, memkv, memkv)


def _ca_bwd(q, memkv, do, tq=512):
    t, d = q.shape
    m = memkv.shape[0]
    scale = CA_HD ** -0.5

    def body(q_ref, k_ref, v_ref, do_ref, dq_ref, dkv_ref):
        i = pl.program_id(0)

        @pl.when(i == 0)
        def _():
            dkv_ref[...] = jnp.zeros_like(dkv_ref)

        for h in range(CA_HEADS):
            hs = slice(h * CA_HD, (h + 1) * CA_HD)
            q_h, k_h, v_h, do_h = q_ref[:, hs], k_ref[:, hs], v_ref[:, hs], do_ref[:, hs]
            s = _dot(q_h, k_h, NT) * scale
            p = jnp.exp(s - jnp.max(s, axis=-1, keepdims=True))
            p = p / jnp.sum(p, axis=-1, keepdims=True)
            dp = _dot(do_h, v_h, NT)
            ds = _bf(p * (dp - jnp.sum(p * dp, axis=-1, keepdims=True)) * scale)
            dq_ref[:, hs] = _dot(ds, k_h, NN).astype(BF16)
            dkv_ref[:, hs] += _dot(ds, q_h, TN)
            dkv_ref[:, d + h * CA_HD: d + (h + 1) * CA_HD] += _dot(_bf(p), do_h, TN)

    return pl.pallas_call(
        body, name="ca_bwd", grid=(t // tq,),
        in_specs=[pl.BlockSpec((tq, d), lambda i: (i, 0)), pl.BlockSpec((m, d), lambda i: (0, 0)),
                  pl.BlockSpec((m, d), lambda i: (0, 1)), pl.BlockSpec((tq, d), lambda i: (i, 0))],
        out_specs=[pl.BlockSpec((tq, d), lambda i: (i, 0)), pl.BlockSpec((m, 2 * d), lambda i: (0, 0))],
        out_shape=[jax.ShapeDtypeStruct((t, d), BF16), jax.ShapeDtypeStruct((m, 2 * d), F32)],
        compiler_params=_params(("arbitrary",)),
    )(q, memkv, memkv, do)


STRIP = 256


def _shift_down(u, n, row):
    return jnp.where(row >= n, pltpu.roll(u, n, 0), 0.0)


def _shift_up(u, n, row):
    t = u.shape[0]
    return jnp.where(row < t - n, pltpu.roll(u, t - n, 0), 0.0)


def _conv(u, cw_ref, row):
    return ((cw_ref[3:4, :] + cw_ref[0:1, :] * _shift_down(u, 2, row)) + cw_ref[1:2, :] * _shift_down(u, 1, row)) \
        + cw_ref[2:3, :] * u


def _swiglu_fwd(u0, cw):
    t, w = u0.shape[0], u0.shape[1] // 2
    ns = w // STRIP
    col = pl.BlockSpec((t, STRIP), lambda j: (0, j))
    col_up = pl.BlockSpec((t, STRIP), lambda j: (0, ns + j))
    cws = pl.BlockSpec((SUBLANES, STRIP), lambda j: (0, j))
    cws_up = pl.BlockSpec((SUBLANES, STRIP), lambda j: (0, ns + j))

    def body(g_ref, u_ref, cg_ref, cu_ref, a_ref, at_ref):
        row = lax.broadcasted_iota(jnp.int32, (t, STRIP), 0)
        gate = _conv(g_ref[...].astype(F32), cg_ref, row)
        up = _conv(u_ref[...].astype(F32), cu_ref, row)
        act = (gate * _sigmoid(gate) * up).astype(BF16)
        a_ref[...] = act
        at_ref[...] = act.T

    return pl.pallas_call(
        body, name="swiglu_fwd", grid=(ns,), in_specs=[col, col_up, cws, cws_up],
        out_specs=[col, pl.BlockSpec((STRIP, t), lambda j: (j, 0))],
        out_shape=[jax.ShapeDtypeStruct((t, w), BF16), jax.ShapeDtypeStruct((w, t), BF16)],
        compiler_params=_params(("parallel",)),
    )(u0, u0, cw, cw)


def _swiglu_bwd(u0, cw, da):
    t, w = u0.shape[0], u0.shape[1] // 2
    ns = w // STRIP
    col = pl.BlockSpec((t, STRIP), lambda j: (0, j))
    col_up = pl.BlockSpec((t, STRIP), lambda j: (0, ns + j))
    cws = pl.BlockSpec((SUBLANES, STRIP), lambda j: (0, j))
    cws_up = pl.BlockSpec((SUBLANES, STRIP), lambda j: (0, ns + j))

    def conv_bwd(du, u0, cw_ref, row, du0_ref, du0t_ref, dcw_ref):
        du1, du2 = _shift_up(du, 1, row), _shift_up(du, 2, row)
        du0 = ((cw_ref[2:3, :] * du + cw_ref[1:2, :] * du1) + cw_ref[0:1, :] * du2).astype(BF16)
        du0_ref[...] = du0
        du0t_ref[...] = du0.T
        dcw_ref[0:1, :] = jnp.sum(du2 * u0, axis=0, keepdims=True)
        dcw_ref[1:2, :] = jnp.sum(du1 * u0, axis=0, keepdims=True)
        dcw_ref[2:3, :] = jnp.sum(du * u0, axis=0, keepdims=True)
        dcw_ref[3:4, :] = jnp.sum(du, axis=0, keepdims=True)
        dcw_ref[4:8, :] = jnp.zeros((4, STRIP), F32)

    def body(g_ref, u_ref, cg_ref, cu_ref, da_ref, dg0_ref, du0_ref, dut_ref, dcg_ref, dcu_ref):
        row = lax.broadcasted_iota(jnp.int32, (t, STRIP), 0)
        g0, up0 = g_ref[...].astype(F32), u_ref[...].astype(F32)
        gate = _conv(g0, cg_ref, row)
        up = _conv(up0, cu_ref, row)
        sg = _sigmoid(gate)
        da = da_ref[...].astype(F32)
        dgate = da * up * (sg * (1.0 + gate * (1.0 - sg)))
        dup = da * (gate * sg)
        conv_bwd(dgate, g0, cg_ref, row, dg0_ref, dut_ref.at[0], dcg_ref)
        conv_bwd(dup, up0, cu_ref, row, du0_ref, dut_ref.at[1], dcu_ref)

    return pl.pallas_call(
        body, name="swiglu_bwd", grid=(ns,), in_specs=[col, col_up, cws, cws_up, col],
        out_specs=[col, col, pl.BlockSpec((2, STRIP, t), lambda j: (0, j, 0)), cws, cws],
        out_shape=[jax.ShapeDtypeStruct((t, w), BF16), jax.ShapeDtypeStruct((t, w), BF16),
                   jax.ShapeDtypeStruct((2, w, t), BF16),
                   jax.ShapeDtypeStruct((SUBLANES, w), F32), jax.ShapeDtypeStruct((SUBLANES, w), F32)],
        compiler_params=_params(("parallel",)),
    )(u0, u0, cw, cw, da)


def _ffn_win_grad(dut, x2b, tn=512):
    t, d = x2b.shape
    sp, sw = FF_SLAB_P, FF_SLAB

    def body(a_ref, b_ref, o_ref, ob_ref):
        res = _dot(a_ref[...], b_ref[...], NN)
        o_ref[...] = res[:sw, :]
        ob_ref[...] = res[:sw, :].astype(BF16)

    o_spec = pl.BlockSpec((None, sw, tn), lambda j, n: (j, 0, n))
    return pl.pallas_call(
        body, name="mm_g_ffn_in", grid=(8, d // tn),
        in_specs=[pl.BlockSpec((None, sp, t), lambda j, n: (j // 4, j % 4, 0)),
                  pl.BlockSpec((t, tn), lambda j, n: (0, n))],
        out_specs=[o_spec, o_spec],
        out_shape=[jax.ShapeDtypeStruct((8, sw, d), F32), jax.ShapeDtypeStruct((8, sw, d), BF16)],
        compiler_params=_params(("parallel", "parallel")),
    )(dut, x2b)


def _tile2d(r, c, limit=1 << 20):
    tr, tc = r, c
    while tr * tc * 4 > limit:
        if tr % (2 * SUBLANES) == 0:
            tr //= 2
        elif tc % (2 * LANES) == 0:
            tc //= 2
        else:
            break
    return tr, tc


def _adamw_math(w, m, v, g):
    c1 = 1.0 - ADAM_B1 ** ADAM_STEP
    c2 = 1.0 - ADAM_B2 ** ADAM_STEP
    mm = ADAM_B1 * m + (1.0 - ADAM_B1) * g
    vv = ADAM_B2 * v + (1.0 - ADAM_B2) * (g * g)
    delta = -ADAM_LR * ((mm / c1) / (jnp.sqrt(vv / c2) + ADAM_EPS) + ADAM_WD * w)
    return delta, mm, vv


def _adamw(w, m, v, g, name):
    r, c = w.shape
    blk = pl.BlockSpec((r, c), lambda i: (0, 0))

    def body(w_ref, m_ref, v_ref, gi_ref, g_ref, d_ref, nm_ref, nv_ref):
        g = gi_ref[...]
        d_ref[...], nm_ref[...], nv_ref[...] = _adamw_math(w_ref[...], m_ref[...], v_ref[...], g)
        g_ref[...] = g

    return pl.pallas_call(body, name=name, grid=(1,), in_specs=[blk] * 4, out_specs=[blk] * 4,
                          out_shape=[jax.ShapeDtypeStruct((r, c), F32)] * 4,
                          compiler_params=_params(("arbitrary",)))(w, m, v, g)


def _small_reduce(gathered):
    nd, r, n = gathered.shape
    tn = 2048 if n % 2048 == 0 else n
    def body(g_ref, s_ref, t_ref):
        s = g_ref[0]
        for dv in range(1, nd):
            s = s + g_ref[dv]
        s_ref[...] = s
        t_ref[...] = jnp.broadcast_to(jnp.sum(s, axis=0, keepdims=True), (r, tn))

    return pl.pallas_call(
        body, name="small_reduce", grid=(n // tn,),
        in_specs=[pl.BlockSpec((nd, r, tn), lambda j: (0, 0, j))],
        out_specs=[pl.BlockSpec((r, tn), lambda j: (0, j))] * 2,
        out_shape=[jax.ShapeDtypeStruct((r, n), F32)] * 2, compiler_params=_params(("parallel",)),
    )(gathered)


HBM = pl.BlockSpec(memory_space=pltpu.HBM)


def _all_gather(arrs, name):
    n = len(arrs)

    def body(*refs):
        ins, outs = refs[:n], refs[n:2 * n]
        send, recv, lsem = refs[2 * n:]
        x, y, c = lax.axis_index("x"), lax.axis_index("y"), lax.axis_index("c")
        me, sib = (x, y, c), (x, y, 1 - c)
        chips = [(1 - x, y), (x, 1 - y), (1 - x, 1 - y)]

        def slot(w, p):
            return outs[w].at[4 * p[0] + 2 * p[1] + p[2]]

        def cp(w, k, block, to, src=None):
            return pltpu.make_async_remote_copy(
                src_ref=slot(w, block) if src is None else src, dst_ref=slot(w, block),
                send_sem=send.at[w * 7 + k], recv_sem=recv.at[w * 7 + k], device_id=to, device_id_type=MESH)

        mine = [pltpu.make_async_copy(ins[w], slot(w, me), lsem.at[w]) for w in range(n)]
        for m in mine:
            m.start()
        first = []
        for w in range(n):
            first.append(cp(w, 0, me, sib, src=ins[w]))
            first += [cp(w, 1 + j, me, (*chip, c), src=ins[w]) for j, chip in enumerate(chips)]
        for f in first:
            f.start()
        passed = []
        for j, chip in enumerate(chips):
            for w in range(n):
                cp(w, 1 + j, (*chip, c), me).wait_recv()
                fwd = cp(w, 4 + j, (*chip, c), sib)
                fwd.start()
                passed.append(fwd)
        for w in range(n):
            cp(w, 0, sib, me).wait_recv()
            for j, chip in enumerate(chips):
                cp(w, 4 + j, (*chip, 1 - c), me).wait_recv()
        for f in first + passed:
            f.wait_send()
        for m in mine:
            m.wait()

    return pl.pallas_call(
        body, name=name, in_specs=[HBM] * n, out_specs=[HBM] * n,
        out_shape=[jax.ShapeDtypeStruct((8,) + a.shape, a.dtype) for a in arrs],
        scratch_shapes=[pltpu.SemaphoreType.DMA((7 * n,)), pltpu.SemaphoreType.DMA((7 * n,)),
                        pltpu.SemaphoreType.DMA((n,))],
    )(*arrs)


SEM = pl.BlockSpec(memory_space=pltpu.SEMAPHORE)
ANY = pl.BlockSpec(memory_space=pl.ANY)
EFFECT = pltpu.SideEffectType.DATAFLOW_SIDE_EFFECTING
N_PEERS = 7


def _peers(x, y, c):
    return [((1 - x) if k & 4 else x, (1 - y) if k & 2 else y, (1 - c) if k & 1 else c) for k in range(1, 8)]


def _spread_copies(src_refs, land_refs, send, recv, gather):
    x, y, c = lax.axis_index("x"), lax.axis_index("y"), lax.axis_index("c")
    me = 4 * x + 2 * y + c
    copies = []
    for w in range(len(src_refs)):
        for k, (px, py, pc) in enumerate(_peers(x, y, c)):
            p = 4 * px + 2 * py + pc
            copies.append((pltpu.make_async_remote_copy(
                src_ref=src_refs[w] if gather else src_refs[w].at[p], dst_ref=land_refs[w].at[me],
                send_sem=send[w].at[k], recv_sem=recv[w].at[k], device_id=(px, py, pc), device_id_type=MESH),
                pltpu.make_async_remote_copy(
                src_ref=src_refs[w] if gather else src_refs[w].at[p], dst_ref=land_refs[w].at[p],
                send_sem=send[w].at[k], recv_sem=recv[w].at[k], device_id=(px, py, pc), device_id_type=MESH)))
    return copies


def _hbm(a):
    return pltpu.with_memory_space_constraint(a, pltpu.HBM)


def _spread_start(srcs, lands, after, gather, name):
    n = len(srcs)

    def body(*refs):
        src_refs, land_refs = refs[:n], refs[n:2 * n]
        outs = refs[2 * n + 1:]
        send, recv, token = outs[:n], outs[n:2 * n], outs[4 * n]
        for start, _ in _spread_copies(src_refs, land_refs, send, recv, gather):
            start.start()
        token[...] = jnp.zeros_like(token)

    res = pl.pallas_call(
        body, name=name,
        out_shape=tuple([pltpu.SemaphoreType.DMA((N_PEERS,))] * (2 * n)
                        + [pltpu.HBM(a.shape, a.dtype) for a in srcs] + [pltpu.HBM(a.shape, a.dtype) for a in lands]
                        + [jax.ShapeDtypeStruct((SUBLANES, LANES), F32)]),
        in_specs=[HBM] * (2 * n) + [ANY],
        out_specs=tuple([SEM] * (2 * n) + [HBM] * (2 * n) + [pl.BlockSpec(memory_space=pltpu.VMEM)]),
        input_output_aliases={i: 2 * n + i for i in range(2 * n)},
        compiler_params=pltpu.CompilerParams(has_side_effects=EFFECT),
    )(*[_hbm(a) for a in srcs], *[_hbm(a) for a in lands], after)
    return res[:n], res[n:2 * n], res[2 * n:3 * n], res[3 * n:4 * n], res[4 * n]


def _spread_wait(send, recv, srcs, lands, after, gather, name):
    n = len(srcs)
    after = list(after) if isinstance(after, (list, tuple)) else [after]

    def body(*refs):
        src_refs, land_refs = refs[:n], refs[n:2 * n]
        send_refs, recv_refs = refs[2 * n:3 * n], refs[3 * n:4 * n]
        for _, arrive in _spread_copies(src_refs, land_refs, send_refs, recv_refs, gather):
            arrive.wait_send()
            arrive.wait_recv()

    res = pl.pallas_call(
        body, name=name,
        out_shape=tuple([pltpu.HBM(a.shape, a.dtype) for a in srcs] + [pltpu.HBM(a.shape, a.dtype) for a in lands]),
        in_specs=[HBM] * (2 * n) + [SEM] * (2 * n) + [ANY] * len(after),
        out_specs=tuple([HBM] * (2 * n)),
        input_output_aliases={i: i for i in range(2 * n)},
        compiler_params=pltpu.CompilerParams(has_side_effects=EFFECT),
    )(*srcs, *lands, *send, *recv, *after)
    return res[n:]


def _landing(shape, dtype, own, me):
    return lax.dynamic_update_index_in_dim(lax.empty((8,) + shape, dtype), own, me, 0)


N_GLR = GLA_W + GLA_RANK
FF_SLAB = D_FF // 4
FF_SLAB_P = FFP // 4


TRANSPOSED = ("w_in", "ffn_w_in")


def _prepare_sub1(gath):
    w_in_t = gath["w_in"].reshape(-1, gath["w_in"].shape[2])
    w2 = jnp.concatenate([gath["gla_gate_w2"][s] for s in range(8)], axis=1)
    return {"w_a_t": jnp.pad(w_in_t[:N_GLR], ((0, HA_W - N_GLR), (0, 0))), "w_b_t": w_in_t[N_GLR:],
            "w2p": jnp.pad(w2, ((0, LANES - GLA_RANK), (0, 0)))}


def _prepare_ffn_in(g):
    f = jnp.pad(g, ((0, 0), (0, FF_SLAB_P - FF_SLAB), (0, 0)))
    return f.reshape(2 * FFP, f.shape[2])


def _prepare_ffn_out(g):
    return jnp.pad(g.reshape(4, FF_SLAB, -1), ((0, 0), (0, FF_SLAB_P - FF_SLAB), (0, 0))).reshape(FFP, -1)


def _prepare_conv(g, conv_b):
    padc = FF_SLAB_P - FF_SLAB
    cw = jnp.pad(g, ((0, 0), (0, 0), (0, padc)))
    cb = jnp.pad(conv_b.reshape(8, 1, FF_SLAB), ((0, 0), (0, 0), (0, padc)))
    rows = jnp.concatenate([cw, cb, jnp.zeros((8, 4, FF_SLAB_P), F32)], axis=1)
    return jnp.concatenate([rows[s] for s in range(8)], axis=1)


def _prepare_ffn(gath, conv_b):
    return {"w_ffn_t": _prepare_ffn_in(gath["ffn_w_in"]), "wo": _prepare_ffn_out(gath["ffn_w_out"]),
            "cw": _prepare_conv(gath["ffn_conv_w"], conv_b)}


def _unpad_ff(a):
    r = a.shape[0]
    return a.reshape(r, 4, FF_SLAB_P)[:, :, :FF_SLAB].reshape(r, D_FF)


def _grad_slabs(g):
    w_in_t = jnp.concatenate([g["w_a_t"][:N_GLR], g["w_b_t"]], axis=0)
    s = {"w_in": w_in_t.reshape(4, 2, w_in_t.shape[0] // 8, w_in_t.shape[1])}
    for n in ("w_out", "ca_wq", "ca_wo"):
        s[n] = _to_slabs(n, g[n])
    for n in ("ca_wkv", "ffn_w_in"):
        s[n] = g[n].reshape((4, 2) + g[n].shape[1:])
    wo = g["wo"].reshape(4, FF_SLAB_P, -1)[:, :FF_SLAB]
    s["ffn_w_out"] = wo.reshape(4, 2, FF_SLAB // 2, wo.shape[-1])
    return s


class _AtHand:
    def __init__(self, p):
        self.p = p
        self.token = None

    def sub2(self, after):
        return self.p

    def ffn_in(self, after):
        return self.p["w_ffn_t"]

    def ffn_out(self, after):
        return self.p["wo"]

    def grads_out(self, group, slabs):
        pass

    def small_out(self, parts):
        pass


def _local_step(x, mem, positions, target, p, small, stages=None):
    t, d = x.shape
    stages = _AtHand(p) if stages is None else stages
    w_a_t, w_b_t, w2p, cw = p["w_a_t"], p["w_b_t"], p["w2p"], p["cw"]
    tabs = _rope_tables(positions)
    xb = x.astype(BF16) if stages.token is None else (x + stages.token[0, 0]).astype(BF16)
    memb = mem.astype(BF16)

    h_a = _matmul(xb, w_a_t, "nt", F32, 1024, 640, d, "mm_h_a")
    h_b = _matmul(xb, w_b_t, "nt", F32, 1024, 1024, d, "mm_h_b")
    o_g, o_raw, s_before = _gla_fwd(h_a, w2p, small["gla_gate_b"], small["gla_norm_g"])
    qr, kr = _rope_fwd(h_b, tabs)
    o_d_b, o_d, lse_tot = _dil_fwd_all(qr, kr, h_b)
    mixin = jnp.concatenate([o_g, o_d_b], axis=1)
    wts = stages.sub2(mixin)
    mix = _matmul(mixin, wts["w_out"], "nn", F32, 1024, 1024, d, "mm_mix")
    x1, x1b, x1t = _ln_fwd(x, mix, small["ln1_g"], small["ln1_b"], "ln1_fwd", True)

    q_ca = _matmul(x1b, wts["ca_wq"], "nn", BF16, 1024, 1024, d, "mm_caq")
    kvw = wts["ca_wkv"].shape[2]
    memkv = _matmul(memb, wts["ca_wkv"], "nn", BF16, mem.shape[0], kvw, d, "mm_memkv", b_slabs=True)
    o_c, o_ct = _ca_fwd(q_ca, memkv)
    ca_out = _matmul(o_c, wts["ca_wo"], "nn", F32, 1024, 1024, d, "mm_cao")
    x2, x2b = _ln_fwd(x1, ca_out, small["ln2_g"], small["ln2_b"], "ln2_fwd", False)

    w_ffn_t = stages.ffn_in(x2b)
    u0 = _matmul(x2b, w_ffn_t, "nt", BF16, 1024, 1024, d, "mm_u0")
    act, act_t = _swiglu_fwd(u0, cw)
    wo = stages.ffn_out(act)
    ffn = _matmul(act, wo, "nn", F32, 512, 1024, FFP, "mm_ffn")

    dp3, dp3b, dg3, db3, loss_part = _ln_bwd(x2, ffn, small["ln3_g"], small["ln3_b"], target, True, "ln3_bwd")
    g_wo, g_wo16 = _matmul(act_t, dp3b, "nn", F32, 512, 1024, t, "mm_g_wo", also_bf16=True)
    dact = _matmul(dp3b, wo, "nt", BF16, 1024, 512, d, "mm_dact")
    dug, duu, du_t, dcwg, dcwu = _swiglu_bwd(u0, cw, dact)
    g_ffn_in, g_ffn_in16 = _ffn_win_grad(du_t, x2b)

    def wo_slabs(a):
        a = a.reshape(4, FF_SLAB_P, -1)[:, :FF_SLAB]
        return a.reshape(8, FF_SLAB // 2, a.shape[-1])

    def wo_own(me):
        half = FF_SLAB // 2
        return lax.dynamic_slice_in_dim(g_wo, FF_SLAB_P * (me // 2) + half * (me % 2), half, axis=0)

    sent = stages.grads_out("ffn", {"ffn_w_out": (wo_own, wo_slabs(g_wo16)), "ffn_w_in": (g_ffn_in, g_ffn_in16)})
    dx2 = _matmul(dug, w_ffn_t, "nn", F32, 512, 1024, FFP, "mm_dx2_g", resid=dp3, resid_scale=ALPHA, dep=sent)
    dx2 = _matmul(duu, w_ffn_t, "nn", F32, 512, 1024, FFP, "mm_dx2_u", resid=dx2, b_k_off=1)

    dp2, dp2b, dg2, db2 = _ln_bwd(x1, ca_out, small["ln2_g"], small["ln2_b"], dx2, False, "ln2_bwd")
    g_cao, g_cao16 = _matmul(o_ct, dp2b, "nn", F32, 512, 1024, t, "mm_g_cao", also_bf16=True)
    do_c = _matmul(dp2b, wts["ca_wo"], "nt", BF16, 1024, 1024, d, "mm_do_c")
    dq_ca, dmemkv = _ca_bwd(q_ca, memkv, do_c)
    g_caq, g_caq16 = _matmul(x1t, dq_ca, "nn", F32, 512, 1024, t, "mm_g_caq", also_bf16=True)
    g_cakv, g_cakv16 = _matmul(memb, dmemkv.astype(BF16), "tn", F32, 512, kvw, mem.shape[0], "mm_g_cakv",
                               out_slabs=True, also_bf16=True)
    dx1 = _matmul(dq_ca, wts["ca_wq"], "nt", F32, 1024, 1024, d, "mm_dx1", resid=dp2, resid_scale=ALPHA)

    dp1, dp1b, dg1, db1 = _ln_bwd(x, mix, small["ln1_g"], small["ln1_b"], dx1, False, "ln1_bwd")
    g_wout, g_wout16 = _matmul(mixin, dp1b, "tn", F32, 512, 1024, t, "mm_g_wout", also_bf16=True)

    def row_slabs(a):
        return a.reshape(8, a.shape[0] // 8, a.shape[1])

    sent = stages.grads_out("attn", {"ca_wo": (row_slabs(g_cao), row_slabs(g_cao16)),
                                     "ca_wq": (row_slabs(g_caq), row_slabs(g_caq16)), "ca_wkv": (g_cakv, g_cakv16),
                                     "w_out": (row_slabs(g_wout), row_slabs(g_wout16))})
    dmix = _matmul(dp1b, wts["w_out"], "nt", F32, 1024, 1024, d, "mm_dmix", dep=sent)
    dh_a, dw2, dgate_b, dnorm_g = _gla_bwd(h_a, w2p, small["gla_gate_b"], small["gla_norm_g"], o_raw, s_before, dmix)
    small_parts = {
        "gla_gate_b": dgate_b, "gla_norm_g": dnorm_g, "ln1_g": dg1, "ln1_b": db1, "ln2_g": dg2, "ln2_b": db2,
        "ln3_g": dg3, "ln3_b": db3,
        "conv": jnp.concatenate([_unpad_ff(dcwg), _unpad_ff(dcwu)], axis=1),
        "gla_gate_w2": dw2[:GLA_RANK],
    }
    sent = stages.small_out(small_parts)
    dq_d, dk_d, dv_d = _dil_bwd_all(qr, kr, h_b, dmix, o_d, lse_tot)
    dh_b = _dil_dh(dq_d, dk_d, dv_d, tabs)
    g_wa_t, g_wa16 = _matmul(dh_a, xb, "tn", F32, 640, 1024, t, "mm_g_wa", also_bf16=True, dep=sent)
    g_wb_t, g_wb16 = _matmul(dh_b, xb, "tn", F32, 512, 1024, t, "mm_g_wb", also_bf16=True)

    def w_in_slabs(a, b):
        full = jnp.concatenate([a[:N_GLR], b], axis=0)
        return full.reshape(8, full.shape[0] // 8, full.shape[1])

    def w_in_own(me):
        rows = (N_GLR + g_wb_t.shape[0]) // 8
        full = jnp.concatenate([g_wa_t[:N_GLR], g_wb_t], axis=0)
        return lax.dynamic_slice_in_dim(full, me * rows, rows, axis=0)

    sent = stages.grads_out("w_in", {"w_in": (w_in_own, w_in_slabs(g_wa16, g_wb16))})
    dx = _matmul(dh_a, w_a_t, "nn", F32, 512, 1024, HA_W, "mm_dx_a", resid=dp1, resid_scale=ALPHA, dep=sent)
    dx = _matmul(dh_b, w_b_t, "nn", F32, 512, 1024, HB_W, "mm_dx_b", resid=dx)

    grads = {"w_a_t": g_wa_t, "w_b_t": g_wb_t, "w_out": g_wout, "ca_wq": g_caq, "ca_wkv": g_cakv, "ca_wo": g_cao,
             "ffn_w_in": g_ffn_in, "wo": g_wo}
    return loss_part, dx, grads, small_parts


BIG = ("w_in", "w_out", "ca_wq", "ca_wkv", "ca_wo", "ffn_w_in", "ffn_w_out")
COL_SHARDED = ("w_in", "ca_wkv", "ffn_w_in")
SMALL_ORDER = ("gla_gate_b", "gla_norm_g", "ln1_g", "ln1_b", "ln2_g", "ln2_b", "ln3_g", "ln3_b")


def _gathered_full(name, g):
    if name in COL_SHARDED:
        return g.transpose(1, 0, 2).reshape(g.shape[1], 8 * g.shape[2])
    return g.reshape(8 * g.shape[1], g.shape[2])


def _to_slabs(name, full):
    if name in COL_SHARDED:
        r, cc = full.shape
        s = full.reshape(r, 8, cc // 8).transpose(1, 0, 2)
    else:
        rr, c = full.shape
        s = full.reshape(8, rr // 8, c)
    return s.reshape((4, 2) + s.shape[1:])


def kernel(x, mem, positions, w_in, gla_gate_w2, gla_gate_b, gla_norm_g, w_out, ln1_g, ln1_b, ca_wq, ca_wkv, ca_wo, ln2_g, ln2_b, ffn_w_in, ffn_conv_w, ffn_conv_b, ffn_w_out, ln3_g, ln3_b, loss_target, m_w_in, m_gla_gate_w2, m_gla_gate_b, m_gla_norm_g, m_w_out, m_ln1_g, m_ln1_b, m_ca_wq, m_ca_wkv, m_ca_wo, m_ln2_g, m_ln2_b, m_ffn_w_in, m_ffn_conv_w, m_ffn_conv_b, m_ffn_w_out, m_ln3_g, m_ln3_b, v_w_in, v_gla_gate_w2, v_gla_gate_b, v_gla_norm_g, v_w_out, v_ln1_g, v_ln1_b, v_ca_wq, v_ca_wkv, v_ca_wo, v_ln2_g, v_ln2_b, v_ffn_w_in, v_ffn_conv_w, v_ffn_conv_b, v_ffn_w_out, v_ln3_g, v_ln3_b):
    weights = dict(w_in=w_in, gla_gate_w2=gla_gate_w2, gla_gate_b=gla_gate_b, gla_norm_g=gla_norm_g, w_out=w_out,
                   ln1_g=ln1_g, ln1_b=ln1_b, ca_wq=ca_wq, ca_wkv=ca_wkv, ca_wo=ca_wo, ln2_g=ln2_g, ln2_b=ln2_b,
                   ffn_w_in=ffn_w_in, ffn_conv_w=ffn_conv_w, ffn_conv_b=ffn_conv_b, ffn_w_out=ffn_w_out,
                   ln3_g=ln3_g, ln3_b=ln3_b)
    moms = dict(w_in=(m_w_in, v_w_in), gla_gate_w2=(m_gla_gate_w2, v_gla_gate_w2), gla_gate_b=(m_gla_gate_b, v_gla_gate_b),
                gla_norm_g=(m_gla_norm_g, v_gla_norm_g), w_out=(m_w_out, v_w_out), ln1_g=(m_ln1_g, v_ln1_g),
                ln1_b=(m_ln1_b, v_ln1_b), ca_wq=(m_ca_wq, v_ca_wq), ca_wkv=(m_ca_wkv, v_ca_wkv), ca_wo=(m_ca_wo, v_ca_wo),
                ln2_g=(m_ln2_g, v_ln2_g), ln2_b=(m_ln2_b, v_ln2_b), ffn_w_in=(m_ffn_w_in, v_ffn_w_in),
                ffn_conv_w=(m_ffn_conv_w, v_ffn_conv_w), ffn_conv_b=(m_ffn_conv_b, v_ffn_conv_b),
                ffn_w_out=(m_ffn_w_out, v_ffn_w_out), ln3_g=(m_ln3_g, v_ln3_g), ln3_b=(m_ln3_b, v_ln3_b))
    order = list(weights)
    xi, yi, ci = lax.axis_index("x"), lax.axis_index("y"), lax.axis_index("c")
    me = 4 * xi + 2 * yi + ci

    def travel(n, a):
        return jnp.swapaxes(a, 1, 2) if n in TRANSPOSED else a

    shard = {n: travel(n, weights[n]).astype(BF16)[0] for n in BIG}
    first = _all_gather([shard["w_in"], gla_gate_w2.astype(BF16)[0], ffn_conv_w[0]], "ag_first")
    p = _prepare_sub1({"w_in": first[0], "gla_gate_w2": first[1]})
    p["cw"] = _prepare_conv(first[2], ffn_conv_b)
    later = ("w_out", "ca_wq", "ca_wkv", "ca_wo", "ffn_w_in", "ffn_w_out")
    srcs = [shard[n] for n in later]
    lands = [_landing(shard[n].shape, BF16, shard[n], me) for n in later]
    send, recv, srcs, lands, token = _spread_start(srcs, lands, first[0], True, "ag_rest_start")

    class stages:
        pass

    stages.token = token

    def arrived(lo, hi, after, name):
        return _spread_wait(send[lo:hi], recv[lo:hi], srcs[lo:hi], lands[lo:hi], after, True, name)

    def sub2(after):
        g = dict(zip(later[:4], arrived(0, 4, after, "ag_wait_attn")))
        w = {n: _gathered_full(n, g[n]) for n in ("w_out", "ca_wq", "ca_wo")}
        w["ca_wkv"] = g["ca_wkv"]
        return w

    stages.sub2 = sub2
    stages.ffn_in = lambda after: _prepare_ffn_in(arrived(4, 5, after, "ag_wait_ffn_in")[0])
    stages.ffn_out = lambda after: _prepare_ffn_out(arrived(5, 6, after, "ag_wait_ffn_out")[0])
    sent = {}

    def grads_out(group, slabs):
        names = list(slabs)
        srcs16 = [slabs[n][1] for n in names]
        zones = [_landing(s.shape[1:], BF16, jnp.zeros(s.shape[1:], BF16), me) for s in srcs16]
        snd, rcv, s_thru, l_thru, tok = _spread_start(srcs16, zones, srcs16[0], False, f"rs_{group}_start")
        own32 = [slabs[n][0](me) if callable(slabs[n][0]) else slabs[n][0] for n in names]
        sent[group] = (names, own32, (snd, rcv, s_thru, l_thru))
        return tok

    stages.grads_out = grads_out
    small_sent = []

    def small_out(parts):
        packed = jnp.concatenate([parts[n] for n in SMALL_ORDER] + [parts["conv"],
                                 parts["gla_gate_w2"].reshape(SUBLANES, -1)], axis=1)
        packed = jnp.pad(packed, ((0, 0), (0, (-packed.shape[1]) % 2048)))
        zone = _landing(packed.shape, F32, packed, me)
        snd, rcv, s_thru, l_thru, tok = _spread_start([packed], [zone], packed, True, "ag_small_start")
        small_sent.append((snd, rcv, s_thru, l_thru))
        return tok

    stages.small_out = small_out
    small = dict(gla_gate_b=gla_gate_b, gla_norm_g=gla_norm_g, ln1_g=ln1_g, ln1_b=ln1_b, ln2_g=ln2_g, ln2_b=ln2_b,
                 ln3_g=ln3_g, ln3_b=ln3_b)

    loss_part, dx, grads, small_parts = _local_step(x[0], mem[0], positions[0], loss_target[0], p, small, stages)
    loss = lax.psum(jnp.sum(loss_part), ("x", "y", "c"))

    out = {}
    (allp,) = _spread_wait(*small_sent[0], dx, True, "ag_small_wait")
    dev_sum, row_sum = _small_reduce(allp)

    me1 = me.reshape(1).astype(jnp.int32)

    def finish_group(group, after):
        names, own32, handles = sent[group]
        landed = _spread_wait(*handles, after, False, f"rs_{group}_wait")
        for n, own, land in zip(names, own32, landed):
            m_, v_ = moms[n]
            res4 = _adamw_direct(travel(n, weights[n]), travel(n, m_), travel(n, v_), own, land, me1, f"adamw_{n}")
            out[n] = [travel(n, a) for a in res4]

    finish_group("ffn", dx)
    finish_group("attn", dx)
    off = 0
    for n in SMALL_ORDER:
        width = weights[n].shape[1]
        g = row_sum[0:1, off:off + width]
        off += width
        m_, v_ = moms[n]
        out[n] = _adamw(weights[n], m_, v_, g, f"adamw_{n}")
    conv_g = dev_sum[:, off:off + 2 * D_FF]
    off += 2 * D_FF
    g_cb = conv_g[3:4]
    out["ffn_conv_b"] = _adamw(ffn_conv_b, m_ffn_conv_b, v_ffn_conv_b, g_cb, "adamw_ffn_conv_b")
    wsh = ffn_conv_w.shape[2]
    g_cw = lax.dynamic_slice_in_dim(conv_g[0:3], me * wsh, wsh, axis=1)
    out["ffn_conv_w"] = _adamw(ffn_conv_w[0], m_ffn_conv_w[0], v_ffn_conv_w[0], g_cw, "adamw_ffn_conv_w")
    w2_g = dev_sum[:, off:off + GLA_RANK * GLA_HEADS * GLA_DK // SUBLANES].reshape(GLA_RANK, GLA_HEADS * GLA_DK)
    wsh2 = gla_gate_w2.shape[2]
    g_w2 = lax.dynamic_slice_in_dim(w2_g, me * wsh2, wsh2, axis=1)
    out["gla_gate_w2"] = _adamw(gla_gate_w2[0], m_gla_gate_w2[0], v_gla_gate_w2[0], g_w2, "adamw_gla_gate_w2")
    finish_group("w_in", [o[1] for o in out.values()])

    def shaped(n, a):
        return a.reshape(weights[n].shape)

    res = [loss, dx[None]]
    for k in range(4):
        res += [shaped(n, out[n][k]) for n in order]
    return tuple(res)


def _adamw_direct(w, m, v, own, land, me, name):
    _, r, c = w.shape
    tr, tc = _tile2d(r, c)
    blk = pl.BlockSpec((None, tr, tc), lambda i, j, s: (0, i, j))
    if own.ndim == 2:
        mine = pl.BlockSpec((tr, tc), lambda i, j, s: (i, j))
    else:
        mine = pl.BlockSpec((None, tr, tc), lambda i, j, s: (s[0], i, j))
    slots = [pl.BlockSpec((None, tr, tc), lambda i, j, s, k=k: (k, i, j)) for k in range(8)]

    def body(s_ref, w_ref, m_ref, v_ref, p_ref, *rest):
        slot_refs, (g_ref, d_ref, nm_ref, nv_ref) = rest[:8], rest[8:]
        g = p_ref[...]
        for sr in slot_refs:
            g = g + sr[...].astype(F32)
        d_ref[...], nm_ref[...], nv_ref[...] = _adamw_math(w_ref[...], m_ref[...], v_ref[...], g)
        g_ref[...] = g

    gs = pltpu.PrefetchScalarGridSpec(num_scalar_prefetch=1, grid=(r // tr, c // tc),
                                      in_specs=[blk, blk, blk, mine] + slots, out_specs=[blk] * 4)
    return pl.pallas_call(body, name=name, grid_spec=gs, out_shape=[jax.ShapeDtypeStruct((1, r, c), F32)] * 4,
                          compiler_params=_params(("parallel", "parallel")))(me, w, m, v, own, *([land] * 8))
```

```python
import jax
import jax.numpy as jnp
from jax import lax
from jax.experimental import pallas as pl
from jax.experimental.pallas import tpu as pltpu

F32 = jnp.float32
BF16 = jnp.bfloat16
MESH = pl.DeviceIdType.MESH

D_MODEL = 2048
LN_EPS = 1e-5
GLA_HEADS = 4
GLA_DV = 256
GLA_DK = 128
GLA_RANK = 16
GLA_TAU = 16.0
GLA_CHUNK = 64
DIL_HD = 128
DIL_HEADS = 8
DIL_BAND = 128
DIL_DILATIONS = (1, 4, 16)
ROPE_THETA = 500000.0
ROPE_DIMS = 32
CA_HEADS = 4
CA_HD = 512
D_FF = 5504
ALPHA = 2.0 ** 0.25
ADAM_LR = 0.001
ADAM_B1 = 0.9
ADAM_B2 = 0.999
ADAM_EPS = 1e-08
ADAM_WD = 0.01
ADAM_STEP = 10

LANES = 128
SUBLANES = 8
VMEM_LIMIT = 56 * 1024 * 1024

GLA_W = 2 * GLA_HEADS * GLA_DK + 2 * GLA_HEADS * GLA_DV
HA_W = GLA_W + LANES
HB_W = 3 * DIL_HEADS * DIL_HD
FFP = 5632
NEG = -1e30


def _params(sem):
    return pltpu.CompilerParams(dimension_semantics=sem, vmem_limit_bytes=VMEM_LIMIT)


def _sigmoid(x):
    return 1.0 / (1.0 + jnp.exp(-x))


def _dot(a, b, dn, precision=None):
    return lax.dot_general(a, b, (dn, ((), ())), preferred_element_type=F32, precision=precision)


NN = ((1,), (0,))
NT = ((1,), (1,))
TN = ((0,), (0,))


def _bf(v):
    return v if v.dtype == BF16 else v.astype(BF16)


def _matmul(a, b, kind, out_dtype, tm, tn, tk, name, resid=None, resid_scale=1.0, b_k_off=0, b_slabs=False,
            out_slabs=False, also_bf16=False, dep=None):
    if b_slabs:
        assert kind != "nt" and b.shape[2] == tn
        k2, n = b.shape[1], b.shape[0] * tn
    elif kind == "nt":
        n, k2 = b.shape
    else:
        k2, n = b.shape
    (k, m) = a.shape if kind == "tn" else a.shape[::-1]
    assert k2 >= k and (k2 == k or not b_slabs) and m % tm == 0 and n % tn == 0 and k % tk == 0, \
        (name, a.shape, b.shape, tm, tn, tk)
    nk = k // tk
    dn = {"nn": NN, "nt": NT, "tn": TN}[kind]
    a_spec = pl.BlockSpec((tk, tm), lambda i, j, kk: (kk, i)) if kind == "tn" else pl.BlockSpec((tm, tk), lambda i, j, kk: (i, kk))
    if b_slabs:
        b_spec = pl.BlockSpec((None, tk, tn), lambda i, j, kk: (j, kk, 0))
    elif kind == "nt":
        b_spec = pl.BlockSpec((tn, tk), lambda i, j, kk: (j, kk + b_k_off))
    else:
        b_spec = pl.BlockSpec((tk, tn), lambda i, j, kk: (kk + b_k_off, j))
    if out_slabs:
        o_spec = pl.BlockSpec((None, tm, tn), lambda i, j, kk: (j, i, 0))
        o_shape = (n // tn, m, tn)
    else:
        o_spec = pl.BlockSpec((tm, tn), lambda i, j, kk: (i, j))
        o_shape = (m, n)
    has_resid = resid is not None

    n_in = 2 + int(has_resid) + int(dep is not None)

    def body(*refs):
        a_ref, b_ref = refs[:2]
        r_ref = refs[2] if has_resid else None
        o_ref = refs[n_in]
        ob_ref = refs[n_in + 1] if also_bf16 else None
        part = _dot(_bf(a_ref[...]), _bf(b_ref[...]), dn)

        def finish(acc):
            if has_resid:
                acc = acc + resid_scale * r_ref[...].astype(F32)
            o_ref[...] = acc.astype(out_dtype)
            if also_bf16:
                ob_ref[...] = acc.astype(BF16)

        if nk == 1:
            finish(part)
        else:
            acc_ref = refs[-1]
            kk = pl.program_id(2)

            @pl.when(kk == 0)
            def _():
                acc_ref[...] = part

            @pl.when(kk > 0)
            def _():
                acc_ref[...] += part

            @pl.when(kk == nk - 1)
            def _():
                finish(acc_ref[...])

    in_specs = [a_spec, b_spec] + ([o_spec] if has_resid else [])
    args = (a, b) + ((resid,) if has_resid else ())
    if dep is not None:
        in_specs.append(pl.BlockSpec((SUBLANES, LANES), lambda i, j, kk: (0, 0)))
        args += (dep,)
    o_struct = jax.ShapeDtypeStruct(o_shape, out_dtype)
    return pl.pallas_call(
        body, name=name, out_shape=[o_struct, jax.ShapeDtypeStruct(o_shape, BF16)] if also_bf16 else o_struct,
        grid=(m // tm, n // tn, nk), in_specs=in_specs, out_specs=[o_spec, o_spec] if also_bf16 else o_spec,
        scratch_shapes=[pltpu.VMEM((tm, tn), F32)] if nk > 1 else [],
        compiler_params=_params(("parallel", "parallel", "arbitrary")),
    )(*args)


def _ln_core(xres, f):
    p = ALPHA * xres + f
    mu = jnp.mean(p, axis=-1, keepdims=True)
    xc = p - mu
    var = jnp.mean(xc * xc, axis=-1, keepdims=True)
    rstd = lax.rsqrt(var + LN_EPS)
    return xc * rstd, rstd


def _rows8(v):
    r, c = v.shape
    return jnp.sum(v.reshape(r // SUBLANES, SUBLANES, c), axis=0)


def _ln_fwd(xres, f, g, b, name, transposed, tr=256):
    t, d = xres.shape
    row = pl.BlockSpec((tr, d), lambda i: (i, 0))
    vec = pl.BlockSpec((1, d), lambda i: (0, 0))

    def body(x_ref, f_ref, g_ref, b_ref, y_ref, yb_ref, *yt_ref):
        xhat, _ = _ln_core(x_ref[...], f_ref[...])
        y = xhat * g_ref[...] + b_ref[...]
        y_ref[...] = y
        yb = y.astype(BF16)
        yb_ref[...] = yb
        if transposed:
            yt_ref[0][...] = yb.T

    out_specs = [row, row] + ([pl.BlockSpec((d, tr), lambda i: (0, i))] if transposed else [])
    out_shape = [jax.ShapeDtypeStruct((t, d), F32), jax.ShapeDtypeStruct((t, d), BF16)] \
        + ([jax.ShapeDtypeStruct((d, t), BF16)] if transposed else [])
    return pl.pallas_call(
        body, name=name, grid=(t // tr,), in_specs=[row, row, vec, vec], out_specs=out_specs, out_shape=out_shape,
        compiler_params=_params(("parallel",)),
    )(xres, f, g, b)


def _ln_bwd(xres, f, g, b, dy_or_target, loss_head, name, tr=256):
    t, d = xres.shape
    row = pl.BlockSpec((tr, d), lambda i: (i, 0))
    vec = pl.BlockSpec((1, d), lambda i: (0, 0))
    acc = pl.BlockSpec((SUBLANES, d), lambda i: (0, 0))
    lacc = pl.BlockSpec((SUBLANES, LANES), lambda i: (0, 0))

    def body(x_ref, f_ref, g_ref, b_ref, t_ref, dp_ref, dpb_ref, dg_ref, db_ref, *rest):
        i = pl.program_id(0)
        xhat, rstd = _ln_core(x_ref[...], f_ref[...])
        if loss_head:
            err = xhat * g_ref[...] + b_ref[...] - t_ref[...]
            dy = err * (1.0 / d)
            sq = err * err
            lanes = sq[:, :LANES]
            for kk in range(1, d // LANES):
                lanes = lanes + sq[:, kk * LANES:(kk + 1) * LANES]
            lpart = _rows8(lanes) * (0.5 / d)
        else:
            dy = t_ref[...]
        dxh = dy * g_ref[...]
        m1 = jnp.mean(dxh, axis=-1, keepdims=True)
        m2 = jnp.mean(dxh * xhat, axis=-1, keepdims=True)
        dp = rstd * (dxh - m1 - xhat * m2)
        dp_ref[...] = dp
        dpb_ref[...] = dp.astype(BF16)
        dgp = _rows8(dy * xhat)
        dbp = _rows8(dy)

        @pl.when(i == 0)
        def _():
            dg_ref[...] = dgp
            db_ref[...] = dbp
            if loss_head:
                rest[0][...] = lpart

        @pl.when(i > 0)
        def _():
            dg_ref[...] += dgp
            db_ref[...] += dbp
            if loss_head:
                rest[0][...] += lpart

    out_shape = [jax.ShapeDtypeStruct((t, d), F32), jax.ShapeDtypeStruct((t, d), BF16),
                 jax.ShapeDtypeStruct((SUBLANES, d), F32), jax.ShapeDtypeStruct((SUBLANES, d), F32)]
    out_specs = [row, row, acc, acc]
    if loss_head:
        out_shape.append(jax.ShapeDtypeStruct((SUBLANES, LANES), F32))
        out_specs.append(lacc)
    return pl.pallas_call(
        body, name=name, grid=(t // tr,), in_specs=[row, row, vec, vec, row], out_specs=out_specs,
        out_shape=out_shape, compiler_params=_params(("arbitrary",)),
    )(xres, f, g, b, dy_or_target)


def _gla_gates(glr, w2, gb):
    z = _dot(_bf(glr), w2, NN) + gb
    lg = (jnp.minimum(z, 0.0) - jnp.log(1.0 + jnp.exp(-jnp.abs(z)))) * (1.0 / GLA_TAU)
    c = z.shape[0]
    ri = lax.broadcasted_iota(jnp.int32, (c, c), 0)
    ci = lax.broadcasted_iota(jnp.int32, (c, c), 1)
    tri = (ci <= ri).astype(F32)
    bcum = _dot(tri, lg, NN, precision=lax.Precision.HIGHEST)
    blast = jnp.sum(lg, axis=0, keepdims=True)
    return z, bcum, blast, tri


def _gla_specs(t):
    c = GLA_CHUNK
    return c, t // c


def _gla_fwd(h_a, w2p, gate_b, norm_g):
    t = h_a.shape[0]
    c, n = _gla_specs(t)
    hk, hv = GLA_HEADS * GLA_DK, GLA_HEADS * GLA_DV
    scale = GLA_DK ** -0.5

    def body(q_ref, k_ref, v_ref, r_ref, glr_ref, w2_ref, gb_ref, ng_ref, og_ref, oraw_ref, sb_ref, st_ref):
        i = pl.program_id(0)

        @pl.when(i == 0)
        def _():
            st_ref[...] = jnp.zeros_like(st_ref)

        _, bcum, blast, _ = _gla_gates(glr_ref[...], w2_ref[...], gb_ref[...])
        ri = lax.broadcasted_iota(jnp.int32, (c, c), 0)
        ci = lax.broadcasted_iota(jnp.int32, (c, c), 1)
        causal = ci <= ri
        for h in range(GLA_HEADS):
            ks = slice(h * GLA_DK, (h + 1) * GLA_DK)
            vs = slice(h * GLA_DV, (h + 1) * GLA_DV)
            b_h, bl_h = bcum[:, ks], blast[:, ks]
            q_h, k_h = q_ref[:, ks], k_ref[:, ks]
            v_h = _bf(v_ref[:, vs])
            qi = _bf(q_h * scale * jnp.exp(b_h))
            ki = _bf(k_h * jnp.exp(-b_h))
            ke = _bf(k_h * jnp.exp(bl_h - b_h))
            st = st_ref[h]
            sb_ref[0, h] = st
            a = jnp.where(causal, _dot(qi, ki, NT), 0.0)
            o = _dot(_bf(a), v_h, NN) + _dot(qi, _bf(st), NT)
            st_ref[h] = st * jnp.exp(bl_h) + _dot(v_h, ke, TN)
            oraw_ref[:, vs] = o
            mu = jnp.mean(o, axis=-1, keepdims=True)
            oc = o - mu
            var = jnp.mean(oc * oc, axis=-1, keepdims=True)
            xh = oc * lax.rsqrt(var + LN_EPS)
            r_h = r_ref[:, vs]
            og_ref[:, vs] = (xh * ng_ref[:, vs] * (r_h * _sigmoid(r_h))).astype(BF16)

    return pl.pallas_call(
        body, name="gla_fwd", grid=(n,),
        in_specs=[pl.BlockSpec((c, hk), lambda i: (i, 0)), pl.BlockSpec((c, hk), lambda i: (i, 1)),
                  pl.BlockSpec((c, hv), lambda i: (i, 1)), pl.BlockSpec((c, hv), lambda i: (i, 2)),
                  pl.BlockSpec((c, LANES), lambda i: (i, GLA_W // LANES)),
                  pl.BlockSpec((LANES, hk), lambda i: (0, 0)), pl.BlockSpec((1, hk), lambda i: (0, 0)),
                  pl.BlockSpec((1, hv), lambda i: (0, 0))],
        out_specs=[pl.BlockSpec((c, hv), lambda i: (i, 0)), pl.BlockSpec((c, hv), lambda i: (i, 0)),
                   pl.BlockSpec((1, GLA_HEADS, GLA_DV, GLA_DK), lambda i: (i, 0, 0, 0))],
        out_shape=[jax.ShapeDtypeStruct((t, hv), BF16), jax.ShapeDtypeStruct((t, hv), F32),
                   jax.ShapeDtypeStruct((n, GLA_HEADS, GLA_DV, GLA_DK), F32)],
        scratch_shapes=[pltpu.VMEM((GLA_HEADS, GLA_DV, GLA_DK), F32)],
        compiler_params=_params(("arbitrary",)),
    )(h_a, h_a, h_a, h_a, h_a, w2p, gate_b, norm_g)


def _gla_bwd(h_a, w2p, gate_b, norm_g, o_raw, s_before, dmix):
    t = h_a.shape[0]
    c, n = _gla_specs(t)
    hk, hv = GLA_HEADS * GLA_DK, GLA_HEADS * GLA_DV
    scale = GLA_DK ** -0.5
    rev = lambda i: n - 1 - i

    def body(q_ref, k_ref, v_ref, r_ref, glr_ref, w2_ref, gb_ref, ng_ref, oraw_ref, sb_ref, do_ref,
             dh_ref, dw2_ref, dgb_ref, dng_ref, dst_ref):
        i = pl.program_id(0)

        @pl.when(i == 0)
        def _():
            dst_ref[...] = jnp.zeros_like(dst_ref)

        glr = glr_ref[...]
        z, bcum, blast, tri = _gla_gates(glr, w2_ref[...], gb_ref[...])
        ri = lax.broadcasted_iota(jnp.int32, (c, c), 0)
        ci = lax.broadcasted_iota(jnp.int32, (c, c), 1)
        causal = ci <= ri
        dlg_parts = []
        dng_parts = []
        for h in range(GLA_HEADS):
            ks = slice(h * GLA_DK, (h + 1) * GLA_DK)
            vs = slice(h * GLA_DV, (h + 1) * GLA_DV)
            o = oraw_ref[:, vs]
            mu = jnp.mean(o, axis=-1, keepdims=True)
            oc = o - mu
            var = jnp.mean(oc * oc, axis=-1, keepdims=True)
            rstd = lax.rsqrt(var + LN_EPS)
            xh = oc * rstd
            r_h = r_ref[:, vs]
            sg = _sigmoid(r_h)
            silu = r_h * sg
            dout = do_ref[:, vs]
            ng = ng_ref[:, vs]
            dng_parts.append(_rows8(dout * xh * silu))
            dr = dout * xh * ng * (sg * (1.0 + r_h * (1.0 - sg)))
            dxh = dout * ng * silu
            m1 = jnp.mean(dxh, axis=-1, keepdims=True)
            m2 = jnp.mean(dxh * xh, axis=-1, keepdims=True)
            do_raw = _bf(rstd * (dxh - m1 - xh * m2))
            b_h, bl_h = bcum[:, ks], blast[:, ks]
            q_h, k_h = q_ref[:, ks], k_ref[:, ks]
            v_h = _bf(v_ref[:, vs])
            eb, enb, eend = jnp.exp(b_h), jnp.exp(-b_h), jnp.exp(bl_h - b_h)
            decay = jnp.exp(bl_h)
            qi_f, ki_f, ke_f = q_h * scale * eb, k_h * enb, k_h * eend
            qi, ki, ke = _bf(qi_f), _bf(ki_f), _bf(ke_f)
            st = sb_ref[0, h]
            dst = dst_ref[h]
            dst_b = _bf(dst)
            a = _bf(jnp.where(causal, _dot(qi, ki, NT), 0.0))
            da = _bf(jnp.where(causal, _dot(do_raw, v_h, NT), 0.0))
            dv = _dot(a, do_raw, TN) + _dot(ke, dst_b, NT)
            dqi = _dot(da, ki, NN) + _dot(do_raw, _bf(st), NN)
            dki = _dot(da, qi, TN)
            dke = _dot(v_h, dst_b, NN)
            dst_ref[h] = _dot(do_raw, qi, TN) + dst * decay
            dbl = decay * jnp.sum(st * dst, axis=0, keepdims=True) + jnp.sum(dke * ke_f, axis=0, keepdims=True)
            dbc = dqi * qi_f - dki * ki_f - dke * ke_f
            dlg_parts.append(_dot(tri, dbc, TN, precision=lax.Precision.HIGHEST) + dbl)
            dh_ref[:, ks] = (dqi * eb * scale).astype(BF16)
            dh_ref[:, hk + h * GLA_DK: hk + (h + 1) * GLA_DK] = (dki * enb + dke * eend).astype(BF16)
            dh_ref[:, 2 * hk + h * GLA_DV: 2 * hk + (h + 1) * GLA_DV] = dv.astype(BF16)
            dh_ref[:, 2 * hk + hv + h * GLA_DV: 2 * hk + hv + (h + 1) * GLA_DV] = dr.astype(BF16)
        dlg = jnp.concatenate(dlg_parts, axis=1)
        dz = dlg * (1.0 / GLA_TAU) * _sigmoid(-z)
        dz_b = _bf(dz)
        dh_ref[:, GLA_W:] = _dot(dz_b, w2_ref[...], NT).astype(BF16)
        dw2p = _dot(_bf(glr), dz_b, TN)
        dgbp = _rows8(dz)
        dngp = jnp.concatenate(dng_parts, axis=1)

        @pl.when(i == 0)
        def _():
            dw2_ref[...] = dw2p
            dgb_ref[...] = dgbp
            dng_ref[...] = dngp

        @pl.when(i > 0)
        def _():
            dw2_ref[...] += dw2p
            dgb_ref[...] += dgbp
            dng_ref[...] += dngp

    return pl.pallas_call(
        body, name="gla_bwd", grid=(n,),
        in_specs=[pl.BlockSpec((c, hk), lambda i: (rev(i), 0)), pl.BlockSpec((c, hk), lambda i: (rev(i), 1)),
                  pl.BlockSpec((c, hv), lambda i: (rev(i), 1)), pl.BlockSpec((c, hv), lambda i: (rev(i), 2)),
                  pl.BlockSpec((c, LANES), lambda i: (rev(i), GLA_W // LANES)),
                  pl.BlockSpec((LANES, hk), lambda i: (0, 0)), pl.BlockSpec((1, hk), lambda i: (0, 0)),
                  pl.BlockSpec((1, hv), lambda i: (0, 0)),
                  pl.BlockSpec((c, hv), lambda i: (rev(i), 0)),
                  pl.BlockSpec((1, GLA_HEADS, GLA_DV, GLA_DK), lambda i: (rev(i), 0, 0, 0)),
                  pl.BlockSpec((c, hv), lambda i: (rev(i), 0))],
        out_specs=[pl.BlockSpec((c, HA_W), lambda i: (rev(i), 0)),
                   pl.BlockSpec((LANES, hk), lambda i: (0, 0)),
                   pl.BlockSpec((SUBLANES, hk), lambda i: (0, 0)),
                   pl.BlockSpec((SUBLANES, hv), lambda i: (0, 0))],
        out_shape=[jax.ShapeDtypeStruct((t, HA_W), BF16), jax.ShapeDtypeStruct((LANES, hk), F32),
                   jax.ShapeDtypeStruct((SUBLANES, hk), F32), jax.ShapeDtypeStruct((SUBLANES, hv), F32)],
        scratch_shapes=[pltpu.VMEM((GLA_HEADS, GLA_DV, GLA_DK), F32)],
        compiler_params=_params(("arbitrary",)),
    )(h_a, h_a, h_a, h_a, h_a, w2p, gate_b, norm_g, o_raw, s_before, dmix)


def _rope_tables(positions):
    half = ROPE_DIMS // 2
    inv_freq = ROPE_THETA ** (-jnp.arange(0, ROPE_DIMS, 2, dtype=F32) / ROPE_DIMS)
    ang = positions.astype(F32).reshape(-1, 1) * inv_freq
    cos, sin = jnp.cos(ang), jnp.sin(ang)
    t = cos.shape[0]
    one = jnp.ones((t, DIL_HD - ROPE_DIMS), F32)
    zero = jnp.zeros((t, DIL_HD - ROPE_DIMS), F32)
    zh = jnp.zeros((t, half), F32)
    return (jnp.concatenate([cos, cos, one], axis=1), jnp.concatenate([-sin, zh, zero], axis=1),
            jnp.concatenate([zh, sin, zero], axis=1))


def _rope_apply(x, c, s1, s2):
    half = ROPE_DIMS // 2
    return x * c + pltpu.roll(x, DIL_HD - half, 1) * s1 + pltpu.roll(x, half, 1) * s2


def _rope_apply_t(dy, c, s1, s2):
    half = ROPE_DIMS // 2
    return dy * c + pltpu.roll(dy * s1, half, 1) + pltpu.roll(dy * s2, DIL_HD - half, 1)


def _rope_fwd(h_b, tabs, tr=256):
    t = h_b.shape[0]
    w = DIL_HEADS * DIL_HD
    scale = DIL_HD ** -0.5
    tab = pl.BlockSpec((tr, DIL_HD), lambda i: (i, 0))
    outb = pl.BlockSpec((tr, w), lambda i: (i, 0))

    def body(q_ref, k_ref, c_ref, s1_ref, s2_ref, qo_ref, ko_ref):
        c, s1, s2 = c_ref[...], s1_ref[...], s2_ref[...]
        for h in range(DIL_HEADS):
            hs = slice(h * DIL_HD, (h + 1) * DIL_HD)
            qo_ref[:, hs] = _rope_apply(q_ref[:, hs] * scale, c, s1, s2)
            ko_ref[:, hs] = _rope_apply(k_ref[:, hs], c, s1, s2)

    return pl.pallas_call(
        body, name="rope_fwd", grid=(t // tr,),
        in_specs=[pl.BlockSpec((tr, w), lambda i: (i, 0)), pl.BlockSpec((tr, w), lambda i: (i, 1)), tab, tab, tab],
        out_specs=[outb, outb],
        out_shape=[jax.ShapeDtypeStruct((t, w), F32)] * 2,
        compiler_params=_params(("parallel",)),
    )(h_b, h_b, *tabs)


def _dil_dh(dq, dk, dv, tabs, tr=256):
    t, w = dq.shape
    scale = DIL_HD ** -0.5
    tab = pl.BlockSpec((tr, DIL_HD), lambda i: (i, 0))
    inb = pl.BlockSpec((tr, w), lambda i: (i, 0))

    def body(dq_ref, dk_ref, dv_ref, c_ref, s1_ref, s2_ref, o_ref):
        c, s1, s2 = c_ref[...], s1_ref[...], s2_ref[...]
        for h in range(DIL_HEADS):
            hs = slice(h * DIL_HD, (h + 1) * DIL_HD)
            o_ref[:, h * DIL_HD:(h + 1) * DIL_HD] = (_rope_apply_t(dq_ref[:, hs], c, s1, s2) * scale).astype(BF16)
            o_ref[:, w + h * DIL_HD: w + (h + 1) * DIL_HD] = _rope_apply_t(dk_ref[:, hs], c, s1, s2).astype(BF16)
        o_ref[:, 2 * w:] = dv_ref[...].astype(BF16)

    return pl.pallas_call(
        body, name="dil_dh", grid=(t // tr,), in_specs=[inb] * 3 + [tab] * 3,
        out_specs=pl.BlockSpec((tr, 3 * w), lambda i: (i, 0)),
        out_shape=jax.ShapeDtypeStruct((t, 3 * w), BF16), compiler_params=_params(("parallel",)),
    )(dq, dk, dv, *tabs)


def _band_masks(not_first):
    r = lax.broadcasted_iota(jnp.int32, (DIL_BAND, 2 * DIL_BAND), 0)
    c = lax.broadcasted_iota(jnp.int32, (DIL_BAND, 2 * DIL_BAND), 1)
    nf = jnp.full((DIL_BAND, 2 * DIL_BAND), not_first, jnp.int32)
    look_back = jnp.logical_and(jnp.logical_and(c < DIL_BAND, c >= r), nf > 0)
    own_band = jnp.logical_and(c >= DIL_BAND, (c - DIL_BAND) <= r)
    return jnp.logical_or(look_back, own_band)


def _gather_rows(dst_ref, src_ref, t, d, cast=None):
    n = t // d
    for r in range(d):
        v = src_ref[pl.ds(r, n, stride=d), :] if d > 1 else src_ref[...]
        dst_ref[r * n:(r + 1) * n, :] = v if cast is None else v.astype(cast)


def _tri_mask():
    r = lax.broadcasted_iota(jnp.int32, (DIL_BAND, DIL_BAND), 0)
    c = lax.broadcasted_iota(jnp.int32, (DIL_BAND, DIL_BAND), 1)
    return c <= r


def _dil_fwd_all(qr, kr, h_b):
    t = qr.shape[0]
    nbands = t // DIL_BAND
    nbr = len(DIL_DILATIONS)
    hoff = DIL_HEADS

    def col(off):
        return pl.BlockSpec((t, DIL_HD), lambda h: (0, off + h), pipeline_mode=pl.Buffered(1))

    outb = pl.BlockSpec((t, DIL_HD), lambda h: (0, h))

    def body(q_ref, k_ref, v_ref, ob_ref, of_ref, lt_ref, qs, ks, vs, os_, ls_, *br):
        obr, lbr = br[:nbr], br[nbr:]
        for bi, d in enumerate(DIL_DILATIONS):
            n = t // d
            nb = n // DIL_BAND
            _gather_rows(qs, q_ref, t, d, BF16)
            _gather_rows(ks, k_ref, t, d, BF16)
            _gather_rows(vs, v_ref, t, d, BF16)
            s = jnp.where(_tri_mask(), _dot(qs[0:DIL_BAND, :], ks[0:DIL_BAND, :], NT), NEG)
            m = jnp.max(s, axis=-1, keepdims=True)
            pr = jnp.exp(s - m)
            den = jnp.sum(pr, axis=-1, keepdims=True)
            os_[0:DIL_BAND, :] = _dot(_bf(pr), vs[0:DIL_BAND, :], NN) / den
            ls_[0:DIL_BAND, :] = jnp.broadcast_to(m + jnp.log(den), (DIL_BAND, DIL_HD))

            def band(b, carry, nb=nb):
                st = pl.multiple_of((b - 1) * DIL_BAND, DIL_BAND)
                cur = pl.ds(st + DIL_BAND, DIL_BAND)
                both = pl.ds(st, 2 * DIL_BAND)
                not_first = ((b % nb) != 0).astype(jnp.int32)
                s = jnp.where(_band_masks(not_first), _dot(qs[cur, :], ks[both, :], NT), NEG)
                m = jnp.max(s, axis=-1, keepdims=True)
                pr = jnp.exp(s - m)
                den = jnp.sum(pr, axis=-1, keepdims=True)
                os_[cur, :] = _dot(_bf(pr), vs[both, :], NN) / den
                ls_[cur, :] = jnp.broadcast_to(m + jnp.log(den), (DIL_BAND, DIL_HD))
                return carry

            lax.fori_loop(1, nbands, band, 0, unroll=16)
            for r in range(d):
                dst = pl.ds(r, n, stride=d) if d > 1 else slice(None)
                obr[bi][dst, :] = os_[r * n:(r + 1) * n, :]
                lbr[bi][dst, :] = ls_[r * n:(r + 1) * n, :]
        rows = 512
        for c0 in range(0, t, rows):
            sl = slice(c0, c0 + rows)
            la, lb, lc = lbr[0][sl, :], lbr[1][sl, :], lbr[2][sl, :]
            m = jnp.maximum(jnp.maximum(la, lb), lc)
            ea, eb, ec = jnp.exp(la - m), jnp.exp(lb - m), jnp.exp(lc - m)
            den = ea + eb + ec
            o = (ea * obr[0][sl, :] + eb * obr[1][sl, :] + ec * obr[2][sl, :]) / den
            ob_ref[sl, :] = o.astype(BF16)
            of_ref[sl, :] = o
            lt_ref[sl, :] = m + jnp.log(den)

    w = DIL_HEADS * DIL_HD
    vm = lambda dt: pltpu.VMEM((t, DIL_HD), dt)
    return pl.pallas_call(
        body, name="dil_fwd", grid=(DIL_HEADS,), in_specs=[col(0), col(0), col(2 * hoff)],
        out_specs=[outb, outb, outb],
        out_shape=[jax.ShapeDtypeStruct((t, w), BF16), jax.ShapeDtypeStruct((t, w), F32),
                   jax.ShapeDtypeStruct((t, w), F32)],
        scratch_shapes=[vm(BF16)] * 3 + [vm(F32)] * 2 + [vm(F32)] * (2 * nbr),
        compiler_params=_params(("parallel",)),
    )(qr, kr, h_b)


def _dil_bwd_all(qr, kr, h_b, dmix, o_d, lse_tot):
    t = qr.shape[0]
    nbands = t // DIL_BAND
    hoff = DIL_HEADS

    def col(off):
        return pl.BlockSpec((t, DIL_HD), lambda h: (0, off + h), pipeline_mode=pl.Buffered(1))

    outb = pl.BlockSpec((t, DIL_HD), lambda h: (0, h))

    def body(q_ref, k_ref, v_ref, do_ref, o_ref, l_ref, dq_ref, dk_ref, dv_ref,
             qs, ks, vs, dos, lss, dds, dqs, acck, accv, ddt):
        rows = 512
        for c0 in range(0, t, rows):
            prod = do_ref[c0:c0 + rows, :] * o_ref[c0:c0 + rows, :]
            ddt[c0:c0 + rows, :] = jnp.broadcast_to(jnp.sum(prod, axis=-1, keepdims=True), (rows, DIL_HD))
        for bi, d in enumerate(DIL_DILATIONS):
            n = t // d
            nb = n // DIL_BAND
            _gather_rows(qs, q_ref, t, d, BF16)
            _gather_rows(ks, k_ref, t, d, BF16)
            _gather_rows(vs, v_ref, t, d, BF16)
            _gather_rows(dos, do_ref, t, d, BF16)
            _gather_rows(lss, l_ref, t, d)
            _gather_rows(dds, ddt, t, d)
            b0 = slice(0, DIL_BAND)
            s = jnp.where(_tri_mask(), _dot(qs[b0, :], ks[b0, :], NT), NEG)
            pr = jnp.exp(s - lss[b0, :])
            ds = _bf(pr * (_dot(dos[b0, :], vs[b0, :], NT) - dds[b0, :]))
            dqs[b0, :] = _dot(ds, ks[b0, :], NN)
            acck[DIL_BAND:2 * DIL_BAND, :] = _dot(ds, qs[b0, :], TN)
            accv[DIL_BAND:2 * DIL_BAND, :] = _dot(_bf(pr), dos[b0, :], TN)

            def band(b, carry, nb=nb):
                st = pl.multiple_of((b - 1) * DIL_BAND, DIL_BAND)
                cur = pl.ds(st + DIL_BAND, DIL_BAND)
                both = pl.ds(st, 2 * DIL_BAND)
                back_rows = pl.ds(st + DIL_BAND, DIL_BAND)
                own_rows = pl.ds(st + 2 * DIL_BAND, DIL_BAND)
                not_first = ((b % nb) != 0).astype(jnp.int32)
                qb, dob, lb, ddb = qs[cur, :], dos[cur, :], lss[cur, :], dds[cur, :]
                kcat, vcat = ks[both, :], vs[both, :]
                s = jnp.where(_band_masks(not_first), _dot(qb, kcat, NT), NEG)
                pr = jnp.exp(s - jnp.concatenate([lb, lb], axis=1))
                ds = _bf(pr * (_dot(dob, vcat, NT) - jnp.concatenate([ddb, ddb], axis=1)))
                dqs[cur, :] = _dot(ds, kcat, NN)
                dkk = _dot(ds, qb, TN)
                dvv = _dot(_bf(pr), dob, TN)
                acck[back_rows, :] += dkk[:DIL_BAND]
                accv[back_rows, :] += dvv[:DIL_BAND]
                acck[own_rows, :] = dkk[DIL_BAND:]
                accv[own_rows, :] = dvv[DIL_BAND:]
                return carry

            lax.fori_loop(1, nbands, band, 0, unroll=8)
            for r in range(d):
                lo = r * n
                if d == 1:
                    dq_ref[...] = dqs[...]
                    dk_ref[...] = acck[DIL_BAND:DIL_BAND + t, :]
                    dv_ref[...] = accv[DIL_BAND:DIL_BAND + t, :]
                else:
                    dst = pl.ds(r, n, stride=d)
                    dq_ref[dst, :] = dq_ref[dst, :] + dqs[lo:lo + n, :]
                    dk_ref[dst, :] = dk_ref[dst, :] + acck[DIL_BAND + lo:DIL_BAND + lo + n, :]
                    dv_ref[dst, :] = dv_ref[dst, :] + accv[DIL_BAND + lo:DIL_BAND + lo + n, :]

    w = DIL_HEADS * DIL_HD
    vm = lambda dt, extra=0: pltpu.VMEM((t + extra, DIL_HD), dt)
    return pl.pallas_call(
        body, name="dil_bwd", grid=(DIL_HEADS,),
        in_specs=[col(0), col(0), col(2 * hoff), col(hoff), col(0), col(0)], out_specs=[outb] * 3,
        out_shape=[jax.ShapeDtypeStruct((t, w), F32)] * 3,
        scratch_shapes=[vm(BF16)] * 4 + [vm(F32)] * 3 + [vm(F32, DIL_BAND)] * 2 + [vm(F32)],
        compiler_params=_params(("parallel",)),
    )(qr, kr, h_b, dmix, o_d, lse_tot)


def _ca_fwd(q, memkv, tq=512):
    t, d = q.shape
    m = memkv.shape[0]
    scale = CA_HD ** -0.5

    def body(q_ref, k_ref, v_ref, o_ref, ot_ref):
        for h in range(CA_HEADS):
            hs = slice(h * CA_HD, (h + 1) * CA_HD)
            s = _dot(q_ref[:, hs], k_ref[:, hs], NT) * scale
            p = jnp.exp(s - jnp.max(s, axis=-1, keepdims=True))
            p = p / jnp.sum(p, axis=-1, keepdims=True)
            o = _dot(_bf(p), v_ref[:, hs], NN).astype(BF16)
            o_ref[:, hs] = o
            ot_ref[hs, :] = o.T

    return pl.pallas_call(
        body, name="ca_fwd", grid=(t // tq,),
        in_specs=[pl.BlockSpec((tq, d), lambda i: (i, 0)), pl.BlockSpec((m, d), lambda i: (0, 0)),
                  pl.BlockSpec((m, d), lambda i: (0, 1))],
        out_specs=[pl.BlockSpec((tq, d), lambda i: (i, 0)), pl.BlockSpec((d, tq), lambda i: (0, i))],
        out_shape=[jax.ShapeDtypeStruct((t, d), BF16), jax.ShapeDtypeStruct((d, t), BF16)],
        compiler_params=_params(("parallel",)),
    )(q, memkv, memkv)


def _ca_bwd(q, memkv, do, tq=512):
    t, d = q.shape
    m = memkv.shape[0]
    scale = CA_HD ** -0.5

    def body(q_ref, k_ref, v_ref, do_ref, dq_ref, dkv_ref):
        i = pl.program_id(0)

        @pl.when(i == 0)
        def _():
            dkv_ref[...] = jnp.zeros_like(dkv_ref)

        for h in range(CA_HEADS):
            hs = slice(h * CA_HD, (h + 1) * CA_HD)
            q_h, k_h, v_h, do_h = q_ref[:, hs], k_ref[:, hs], v_ref[:, hs], do_ref[:, hs]
            s = _dot(q_h, k_h, NT) * scale
            p = jnp.exp(s - jnp.max(s, axis=-1, keepdims=True))
            p = p / jnp.sum(p, axis=-1, keepdims=True)
            dp = _dot(do_h, v_h, NT)
            ds = _bf(p * (dp - jnp.sum(p * dp, axis=-1, keepdims=True)) * scale)
            dq_ref[:, hs] = _dot(ds, k_h, NN).astype(BF16)
            dkv_ref[:, hs] += _dot(ds, q_h, TN)
            dkv_ref[:, d + h * CA_HD: d + (h + 1) * CA_HD] += _dot(_bf(p), do_h, TN)

    return pl.pallas_call(
        body, name="ca_bwd", grid=(t // tq,),
        in_specs=[pl.BlockSpec((tq, d), lambda i: (i, 0)), pl.BlockSpec((m, d), lambda i: (0, 0)),
                  pl.BlockSpec((m, d), lambda i: (0, 1)), pl.BlockSpec((tq, d), lambda i: (i, 0))],
        out_specs=[pl.BlockSpec((tq, d), lambda i: (i, 0)), pl.BlockSpec((m, 2 * d), lambda i: (0, 0))],
        out_shape=[jax.ShapeDtypeStruct((t, d), BF16), jax.ShapeDtypeStruct((m, 2 * d), F32)],
        compiler_params=_params(("arbitrary",)),
    )(q, memkv, memkv, do)


STRIP = 256


def _shift_down(u, n, row):
    return jnp.where(row >= n, pltpu.roll(u, n, 0), 0.0)


def _shift_up(u, n, row):
    t = u.shape[0]
    return jnp.where(row < t - n, pltpu.roll(u, t - n, 0), 0.0)


def _conv(u, cw_ref, row):
    return ((cw_ref[3:4, :] + cw_ref[0:1, :] * _shift_down(u, 2, row)) + cw_ref[1:2, :] * _shift_down(u, 1, row)) \
        + cw_ref[2:3, :] * u


def _swiglu_fwd(u0, cw):
    t, w = u0.shape[0], u0.shape[1] // 2
    ns = w // STRIP
    col = pl.BlockSpec((t, STRIP), lambda j: (0, j))
    col_up = pl.BlockSpec((t, STRIP), lambda j: (0, ns + j))
    cws = pl.BlockSpec((SUBLANES, STRIP), lambda j: (0, j))
    cws_up = pl.BlockSpec((SUBLANES, STRIP), lambda j: (0, ns + j))

    def body(g_ref, u_ref, cg_ref, cu_ref, a_ref, at_ref):
        row = lax.broadcasted_iota(jnp.int32, (t, STRIP), 0)
        gate = _conv(g_ref[...].astype(F32), cg_ref, row)
        up = _conv(u_ref[...].astype(F32), cu_ref, row)
        act = (gate * _sigmoid(gate) * up).astype(BF16)
        a_ref[...] = act
        at_ref[...] = act.T

    return pl.pallas_call(
        body, name="swiglu_fwd", grid=(ns,), in_specs=[col, col_up, cws, cws_up],
        out_specs=[col, pl.BlockSpec((STRIP, t), lambda j: (j, 0))],
        out_shape=[jax.ShapeDtypeStruct((t, w), BF16), jax.ShapeDtypeStruct((w, t), BF16)],
        compiler_params=_params(("parallel",)),
    )(u0, u0, cw, cw)


def _swiglu_bwd(u0, cw, da):
    t, w = u0.shape[0], u0.shape[1] // 2
    ns = w // STRIP
    col = pl.BlockSpec((t, STRIP), lambda j: (0, j))
    col_up = pl.BlockSpec((t, STRIP), lambda j: (0, ns + j))
    cws = pl.BlockSpec((SUBLANES, STRIP), lambda j: (0, j))
    cws_up = pl.BlockSpec((SUBLANES, STRIP), lambda j: (0, ns + j))

    def conv_bwd(du, u0, cw_ref, row, du0_ref, du0t_ref, dcw_ref):
        du1, du2 = _shift_up(du, 1, row), _shift_up(du, 2, row)
        du0 = ((cw_ref[2:3, :] * du + cw_ref[1:2, :] * du1) + cw_ref[0:1, :] * du2).astype(BF16)
        du0_ref[...] = du0
        du0t_ref[...] = du0.T
        dcw_ref[0:1, :] = jnp.sum(du2 * u0, axis=0, keepdims=True)
        dcw_ref[1:2, :] = jnp.sum(du1 * u0, axis=0, keepdims=True)
        dcw_ref[2:3, :] = jnp.sum(du * u0, axis=0, keepdims=True)
        dcw_ref[3:4, :] = jnp.sum(du, axis=0, keepdims=True)
        dcw_ref[4:8, :] = jnp.zeros((4, STRIP), F32)

    def body(g_ref, u_ref, cg_ref, cu_ref, da_ref, dg0_ref, du0_ref, dut_ref, dcg_ref, dcu_ref):
        row = lax.broadcasted_iota(jnp.int32, (t, STRIP), 0)
        g0, up0 = g_ref[...].astype(F32), u_ref[...].astype(F32)
        gate = _conv(g0, cg_ref, row)
        up = _conv(up0, cu_ref, row)
        sg = _sigmoid(gate)
        da = da_ref[...].astype(F32)
        dgate = da * up * (sg * (1.0 + gate * (1.0 - sg)))
        dup = da * (gate * sg)
        conv_bwd(dgate, g0, cg_ref, row, dg0_ref, dut_ref.at[0], dcg_ref)
        conv_bwd(dup, up0, cu_ref, row, du0_ref, dut_ref.at[1], dcu_ref)

    return pl.pallas_call(
        body, name="swiglu_bwd", grid=(ns,), in_specs=[col, col_up, cws, cws_up, col],
        out_specs=[col, col, pl.BlockSpec((2, STRIP, t), lambda j: (0, j, 0)), cws, cws],
        out_shape=[jax.ShapeDtypeStruct((t, w), BF16), jax.ShapeDtypeStruct((t, w), BF16),
                   jax.ShapeDtypeStruct((2, w, t), BF16),
                   jax.ShapeDtypeStruct((SUBLANES, w), F32), jax.ShapeDtypeStruct((SUBLANES, w), F32)],
        compiler_params=_params(("parallel",)),
    )(u0, u0, cw, cw, da)


def _ffn_win_grad(dut, x2b, tn=512):
    t, d = x2b.shape
    sp, sw = FF_SLAB_P, FF_SLAB

    def body(a_ref, b_ref, o_ref, ob_ref):
        res = _dot(a_ref[...], b_ref[...], NN)
        o_ref[...] = res[:sw, :]
        ob_ref[...] = res[:sw, :].astype(BF16)

    o_spec = pl.BlockSpec((None, sw, tn), lambda j, n: (j, 0, n))
    return pl.pallas_call(
        body, name="mm_g_ffn_in", grid=(8, d // tn),
        in_specs=[pl.BlockSpec((None, sp, t), lambda j, n: (j // 4, j % 4, 0)),
                  pl.BlockSpec((t, tn), lambda j, n: (0, n))],
        out_specs=[o_spec, o_spec],
        out_shape=[jax.ShapeDtypeStruct((8, sw, d), F32), jax.ShapeDtypeStruct((8, sw, d), BF16)],
        compiler_params=_params(("parallel", "parallel")),
    )(dut, x2b)


def _tile2d(r, c, limit=1 << 20):
    tr, tc = r, c
    while tr * tc * 4 > limit:
        if tr % (2 * SUBLANES) == 0:
            tr //= 2
        elif tc % (2 * LANES) == 0:
            tc //= 2
        else:
            break
    return tr, tc


def _adamw_math(w, m, v, g):
    c1 = 1.0 - ADAM_B1 ** ADAM_STEP
    c2 = 1.0 - ADAM_B2 ** ADAM_STEP
    mm = ADAM_B1 * m + (1.0 - ADAM_B1) * g
    vv = ADAM_B2 * v + (1.0 - ADAM_B2) * (g * g)
    delta = -ADAM_LR * ((mm / c1) / (jnp.sqrt(vv / c2) + ADAM_EPS) + ADAM_WD * w)
    return delta, mm, vv


def _adamw(w, m, v, g, name):
    r, c = w.shape
    blk = pl.BlockSpec((r, c), lambda i: (0, 0))

    def body(w_ref, m_ref, v_ref, gi_ref, g_ref, d_ref, nm_ref, nv_ref):
        g = gi_ref[...]
        d_ref[...], nm_ref[...], nv_ref[...] = _adamw_math(w_ref[...], m_ref[...], v_ref[...], g)
        g_ref[...] = g

    return pl.pallas_call(body, name=name, grid=(1,), in_specs=[blk] * 4, out_specs=[blk] * 4,
                          out_shape=[jax.ShapeDtypeStruct((r, c), F32)] * 4,
                          compiler_params=_params(("arbitrary",)))(w, m, v, g)


def _small_reduce(gathered):
    nd, r, n = gathered.shape
    tn = 2048 if n % 2048 == 0 else n
    def body(g_ref, s_ref, t_ref):
        s = g_ref[0]
        for dv in range(1, nd):
            s = s + g_ref[dv]
        s_ref[...] = s
        t_ref[...] = jnp.broadcast_to(jnp.sum(s, axis=0, keepdims=True), (r, tn))

    return pl.pallas_call(
        body, name="small_reduce", grid=(n // tn,),
        in_specs=[pl.BlockSpec((nd, r, tn), lambda j: (0, 0, j))],
        out_specs=[pl.BlockSpec((r, tn), lambda j: (0, j))] * 2,
        out_shape=[jax.ShapeDtypeStruct((r, n), F32)] * 2, compiler_params=_params(("parallel",)),
    )(gathered)


HBM = pl.BlockSpec(memory_space=pltpu.HBM)


def _all_gather(arrs, name):
    n = len(arrs)

    def body(*refs):
        ins, outs = refs[:n], refs[n:2 * n]
        send, recv, lsem = refs[2 * n:]
        x, y, c = lax.axis_index("x"), lax.axis_index("y"), lax.axis_index("c")
        me, sib = (x, y, c), (x, y, 1 - c)
        chips = [(1 - x, y), (x, 1 - y), (1 - x, 1 - y)]

        def slot(w, p):
            return outs[w].at[4 * p[0] + 2 * p[1] + p[2]]

        def cp(w, k, block, to, src=None):
            return pltpu.make_async_remote_copy(
                src_ref=slot(w, block) if src is None else src, dst_ref=slot(w, block),
                send_sem=send.at[w * 7 + k], recv_sem=recv.at[w * 7 + k], device_id=to, device_id_type=MESH)

        mine = [pltpu.make_async_copy(ins[w], slot(w, me), lsem.at[w]) for w in range(n)]
        for m in mine:
            m.start()
        first = []
        for w in range(n):
            first.append(cp(w, 0, me, sib, src=ins[w]))
            first += [cp(w, 1 + j, me, (*chip, c), src=ins[w]) for j, chip in enumerate(chips)]
        for f in first:
            f.start()
        passed = []
        for j, chip in enumerate(chips):
            for w in range(n):
                cp(w, 1 + j, (*chip, c), me).wait_recv()
                fwd = cp(w, 4 + j, (*chip, c), sib)
                fwd.start()
                passed.append(fwd)
        for w in range(n):
            cp(w, 0, sib, me).wait_recv()
            for j, chip in enumerate(chips):
                cp(w, 4 + j, (*chip, 1 - c), me).wait_recv()
        for f in first + passed:
            f.wait_send()
        for m in mine:
            m.wait()

    return pl.pallas_call(
        body, name=name, in_specs=[HBM] * n, out_specs=[HBM] * n,
        out_shape=[jax.ShapeDtypeStruct((8,) + a.shape, a.dtype) for a in arrs],
        scratch_shapes=[pltpu.SemaphoreType.DMA((7 * n,)), pltpu.SemaphoreType.DMA((7 * n,)),
                        pltpu.SemaphoreType.DMA((n,))],
    )(*arrs)


SEM = pl.BlockSpec(memory_space=pltpu.SEMAPHORE)
ANY = pl.BlockSpec(memory_space=pl.ANY)
EFFECT = pltpu.SideEffectType.DATAFLOW_SIDE_EFFECTING
N_PEERS = 7


def _peers(x, y, c):
    return [((1 - x) if k & 4 else x, (1 - y) if k & 2 else y, (1 - c) if k & 1 else c) for k in range(1, 8)]


def _spread_copies(src_refs, land_refs, send, recv, gather):
    x, y, c = lax.axis_index("x"), lax.axis_index("y"), lax.axis_index("c")
    me = 4 * x + 2 * y + c
    copies = []
    for w in range(len(src_refs)):
        for k, (px, py, pc) in enumerate(_peers(x, y, c)):
            p = 4 * px + 2 * py + pc
            copies.append((pltpu.make_async_remote_copy(
                src_ref=src_refs[w] if gather else src_refs[w].at[p], dst_ref=land_refs[w].at[me],
                send_sem=send[w].at[k], recv_sem=recv[w].at[k], device_id=(px, py, pc), device_id_type=MESH),
                pltpu.make_async_remote_copy(
                src_ref=src_refs[w] if gather else src_refs[w].at[p], dst_ref=land_refs[w].at[p],
                send_sem=send[w].at[k], recv_sem=recv[w].at[k], device_id=(px, py, pc), device_id_type=MESH)))
    return copies


def _hbm(a):
    return pltpu.with_memory_space_constraint(a, pltpu.HBM)


def _spread_start(srcs, lands, after, gather, name):
    n = len(srcs)

    def body(*refs):
        src_refs, land_refs = refs[:n], refs[n:2 * n]
        outs = refs[2 * n + 1:]
        send, recv, token = outs[:n], outs[n:2 * n], outs[4 * n]
        for start, _ in _spread_copies(src_refs, land_refs, send, recv, gather):
            start.start()
        token[...] = jnp.zeros_like(token)

    res = pl.pallas_call(
        body, name=name,
        out_shape=tuple([pltpu.SemaphoreType.DMA((N_PEERS,))] * (2 * n)
                        + [pltpu.HBM(a.shape, a.dtype) for a in srcs] + [pltpu.HBM(a.shape, a.dtype) for a in lands]
                        + [jax.ShapeDtypeStruct((SUBLANES, LANES), F32)]),
        in_specs=[HBM] * (2 * n) + [ANY],
        out_specs=tuple([SEM] * (2 * n) + [HBM] * (2 * n) + [pl.BlockSpec(memory_space=pltpu.VMEM)]),
        input_output_aliases={i: 2 * n + i for i in range(2 * n)},
        compiler_params=pltpu.CompilerParams(has_side_effects=EFFECT),
    )(*[_hbm(a) for a in srcs], *[_hbm(a) for a in lands], after)
    return res[:n], res[n:2 * n], res[2 * n:3 * n], res[3 * n:4 * n], res[4 * n]


def _spread_wait(send, recv, srcs, lands, after, gather, name):
    n = len(srcs)
    after = list(after) if isinstance(after, (list, tuple)) else [after]

    def body(*refs):
        src_refs, land_refs = refs[:n], refs[n:2 * n]
        send_refs, recv_refs = refs[2 * n:3 * n], refs[3 * n:4 * n]
        for _, arrive in _spread_copies(src_refs, land_refs, send_refs, recv_refs, gather):
            arrive.wait_send()
            arrive.wait_recv()

    res = pl.pallas_call(
        body, name=name,
        out_shape=tuple([pltpu.HBM(a.shape, a.dtype) for a in srcs] + [pltpu.HBM(a.shape, a.dtype) for a in lands]),
        in_specs=[HBM] * (2 * n) + [SEM] * (2 * n) + [ANY] * len(after),
        out_specs=tuple([HBM] * (2 * n)),
        input_output_aliases={i: i for i in range(2 * n)},
        compiler_params=pltpu.CompilerParams(has_side_effects=EFFECT),
    )(*srcs, *lands, *send, *recv, *after)
    return res[n:]


def _landing(shape, dtype, own, me):
    return lax.dynamic_update_index_in_dim(lax.empty((8,) + shape, dtype), own, me, 0)


N_GLR = GLA_W + GLA_RANK
FF_SLAB = D_FF // 4
FF_SLAB_P = FFP // 4


TRANSPOSED = ("w_in", "ffn_w_in")


def _prepare_sub1(gath):
    w_in_t = gath["w_in"].reshape(-1, gath["w_in"].shape[2])
    w2 = jnp.concatenate([gath["gla_gate_w2"][s] for s in range(8)], axis=1)
    return {"w_a_t": jnp.pad(w_in_t[:N_GLR], ((0, HA_W - N_GLR), (0, 0))), "w_b_t": w_in_t[N_GLR:],
            "w2p": jnp.pad(w2, ((0, LANES - GLA_RANK), (0, 0)))}


def _prepare_ffn_in(g):
    f = jnp.pad(g, ((0, 0), (0, FF_SLAB_P - FF_SLAB), (0, 0)))
    return f.reshape(2 * FFP, f.shape[2])


def _prepare_ffn_out(g):
    return jnp.pad(g.reshape(4, FF_SLAB, -1), ((0, 0), (0, FF_SLAB_P - FF_SLAB), (0, 0))).reshape(FFP, -1)


def _prepare_conv(g, conv_b):
    padc = FF_SLAB_P - FF_SLAB
    cw = jnp.pad(g, ((0, 0), (0, 0), (0, padc)))
    cb = jnp.pad(conv_b.reshape(8, 1, FF_SLAB), ((0, 0), (0, 0), (0, padc)))
    rows = jnp.concatenate([cw, cb, jnp.zeros((8, 4, FF_SLAB_P), F32)], axis=1)
    return jnp.concatenate([rows[s] for s in range(8)], axis=1)


def _prepare_ffn(gath, conv_b):
    return {"w_ffn_t": _prepare_ffn_in(gath["ffn_w_in"]), "wo": _prepare_ffn_out(gath["ffn_w_out"]),
            "cw": _prepare_conv(gath["ffn_conv_w"], conv_b)}


def _unpad_ff(a):
    r = a.shape[0]
    return a.reshape(r, 4, FF_SLAB_P)[:, :, :FF_SLAB].reshape(r, D_FF)


def _grad_slabs(g):
    w_in_t = jnp.concatenate([g["w_a_t"][:N_GLR], g["w_b_t"]], axis=0)
    s = {"w_in": w_in_t.reshape(4, 2, w_in_t.shape[0] // 8, w_in_t.shape[1])}
    for n in ("w_out", "ca_wq", "ca_wo"):
        s[n] = _to_slabs(n, g[n])
    for n in ("ca_wkv", "ffn_w_in"):
        s[n] = g[n].reshape((4, 2) + g[n].shape[1:])
    wo = g["wo"].reshape(4, FF_SLAB_P, -1)[:, :FF_SLAB]
    s["ffn_w_out"] = wo.reshape(4, 2, FF_SLAB // 2, wo.shape[-1])
    return s


class _AtHand:
    def __init__(self, p):
        self.p = p
        self.token = None

    def sub2(self, after):
        return self.p

    def ffn_in(self, after):
        return self.p["w_ffn_t"]

    def ffn_out(self, after):
        return self.p["wo"]

    def grads_out(self, group, slabs):
        pass

    def small_out(self, parts):
        pass


def _local_step(x, mem, positions, target, p, small, stages=None):
    t, d = x.shape
    stages = _AtHand(p) if stages is None else stages
    w_a_t, w_b_t, w2p, cw = p["w_a_t"], p["w_b_t"], p["w2p"], p["cw"]
    tabs = _rope_tables(positions)
    xb = x.astype(BF16) if stages.token is None else (x + stages.token[0, 0]).astype(BF16)
    memb = mem.astype(BF16)

    h_a = _matmul(xb, w_a_t, "nt", F32, 1024, 640, d, "mm_h_a")
    h_b = _matmul(xb, w_b_t, "nt", F32, 1024, 1024, d, "mm_h_b")
    o_g, o_raw, s_before = _gla_fwd(h_a, w2p, small["gla_gate_b"], small["gla_norm_g"])
    qr, kr = _rope_fwd(h_b, tabs)
    o_d_b, o_d, lse_tot = _dil_fwd_all(qr, kr, h_b)
    mixin = jnp.concatenate([o_g, o_d_b], axis=1)
    wts = stages.sub2(mixin)
    mix = _matmul(mixin, wts["w_out"], "nn", F32, 1024, 1024, d, "mm_mix")
    x1, x1b, x1t = _ln_fwd(x, mix, small["ln1_g"], small["ln1_b"], "ln1_fwd", True)

    q_ca = _matmul(x1b, wts["ca_wq"], "nn", BF16, 1024, 1024, d, "mm_caq")
    kvw = wts["ca_wkv"].shape[2]
    memkv = _matmul(memb, wts["ca_wkv"], "nn", BF16, mem.shape[0], kvw, d, "mm_memkv", b_slabs=True)
    o_c, o_ct = _ca_fwd(q_ca, memkv)
    ca_out = _matmul(o_c, wts["ca_wo"], "nn", F32, 1024, 1024, d, "mm_cao")
    x2, x2b = _ln_fwd(x1, ca_out, small["ln2_g"], small["ln2_b"], "ln2_fwd", False)

    w_ffn_t = stages.ffn_in(x2b)
    u0 = _matmul(x2b, w_ffn_t, "nt", BF16, 1024, 1024, d, "mm_u0")
    act, act_t = _swiglu_fwd(u0, cw)
    wo = stages.ffn_out(act)
    ffn = _matmul(act, wo, "nn", F32, 512, 1024, FFP, "mm_ffn")

    dp3, dp3b, dg3, db3, loss_part = _ln_bwd(x2, ffn, small["ln3_g"], small["ln3_b"], target, True, "ln3_bwd")
    g_wo, g_wo16 = _matmul(act_t, dp3b, "nn", F32, FFP // 8, 1024, t, "mm_g_wo", also_bf16=True)
    dact = _matmul(dp3b, wo, "nt", BF16, 1024, 512, d, "mm_dact")
    dug, duu, du_t, dcwg, dcwu = _swiglu_bwd(u0, cw, dact)
    g_ffn_in, g_ffn_in16 = _ffn_win_grad(du_t, x2b)

    def wo_slabs(a):
        a = a.reshape(4, FF_SLAB_P, -1)[:, :FF_SLAB]
        return a.reshape(8, FF_SLAB // 2, a.shape[-1])

    def wo_own(me):
        half = FF_SLAB // 2
        return lax.dynamic_slice_in_dim(g_wo, FF_SLAB_P * (me // 2) + half * (me % 2), half, axis=0)

    sent = stages.grads_out("ffn", {"ffn_w_out": (wo_own, wo_slabs(g_wo16)), "ffn_w_in": (g_ffn_in, g_ffn_in16)})
    dx2 = _matmul(dug, w_ffn_t, "nn", F32, 1024, 512, FFP, "mm_dx2_g", resid=dp3, resid_scale=ALPHA, dep=sent)
    dx2 = _matmul(duu, w_ffn_t, "nn", F32, 1024, 512, FFP, "mm_dx2_u", resid=dx2, b_k_off=1)

    dp2, dp2b, dg2, db2 = _ln_bwd(x1, ca_out, small["ln2_g"], small["ln2_b"], dx2, False, "ln2_bwd")
    g_cao, g_cao16 = _matmul(o_ct, dp2b, "nn", F32, 512, 1024, t, "mm_g_cao", also_bf16=True)
    do_c = _matmul(dp2b, wts["ca_wo"], "nt", BF16, 1024, 1024, d, "mm_do_c")
    dq_ca, dmemkv = _ca_bwd(q_ca, memkv, do_c)
    g_caq, g_caq16 = _matmul(x1t, dq_ca, "nn", F32, 512, 1024, t, "mm_g_caq", also_bf16=True)
    g_cakv, g_cakv16 = _matmul(memb, dmemkv.astype(BF16), "tn", F32, 512, kvw, mem.shape[0], "mm_g_cakv",
                               out_slabs=True, also_bf16=True)
    dx1 = _matmul(dq_ca, wts["ca_wq"], "nt", F32, 1024, 1024, d, "mm_dx1", resid=dp2, resid_scale=ALPHA)

    dp1, dp1b, dg1, db1 = _ln_bwd(x, mix, small["ln1_g"], small["ln1_b"], dx1, False, "ln1_bwd")
    g_wout, g_wout16 = _matmul(mixin, dp1b, "tn", F32, 512, 1024, t, "mm_g_wout", also_bf16=True)

    def row_slabs(a):
        return a.reshape(8, a.shape[0] // 8, a.shape[1])

    sent = stages.grads_out("attn", {"ca_wo": (row_slabs(g_cao), row_slabs(g_cao16)),
                                     "ca_wq": (row_slabs(g_caq), row_slabs(g_caq16)), "ca_wkv": (g_cakv, g_cakv16),
                                     "w_out": (row_slabs(g_wout), row_slabs(g_wout16))})
    dmix = _matmul(dp1b, wts["w_out"], "nt", F32, 1024, 1024, d, "mm_dmix", dep=sent)
    dh_a, dw2, dgate_b, dnorm_g = _gla_bwd(h_a, w2p, small["gla_gate_b"], small["gla_norm_g"], o_raw, s_before, dmix)
    small_parts = {
        "gla_gate_b": dgate_b, "gla_norm_g": dnorm_g, "ln1_g": dg1, "ln1_b": db1, "ln2_g": dg2, "ln2_b": db2,
        "ln3_g": dg3, "ln3_b": db3,
        "conv": jnp.concatenate([_unpad_ff(dcwg), _unpad_ff(dcwu)], axis=1),
        "gla_gate_w2": dw2[:GLA_RANK],
    }
    sent = stages.small_out(small_parts)
    dq_d, dk_d, dv_d = _dil_bwd_all(qr, kr, h_b, dmix, o_d, lse_tot)
    dh_b = _dil_dh(dq_d, dk_d, dv_d, tabs)
    g_wa_t, g_wa16 = _matmul(dh_a, xb, "tn", F32, 640, 1024, t, "mm_g_wa", also_bf16=True, dep=sent)
    g_wb_t, g_wb16 = _matmul(dh_b, xb, "tn", F32, 512, 1024, t, "mm_g_wb", also_bf16=True)

    def w_in_slabs(a, b):
        full = jnp.concatenate([a[:N_GLR], b], axis=0)
        return full.reshape(8, full.shape[0] // 8, full.shape[1])

    def w_in_own(me):
        rows = (N_GLR + g_wb_t.shape[0]) // 8
        full = jnp.concatenate([g_wa_t[:N_GLR], g_wb_t], axis=0)
        return lax.dynamic_slice_in_dim(full, me * rows, rows, axis=0)

    sent = stages.grads_out("w_in", {"w_in": (w_in_own, w_in_slabs(g_wa16, g_wb16))})
    dx = _matmul(dh_a, w_a_t, "nn", F32, 1024, 1024, HA_W, "mm_dx_a", resid=dp1, resid_scale=ALPHA, dep=sent)
    dx = _matmul(dh_b, w_b_t, "nn", F32, 1024, 1024, HB_W, "mm_dx_b", resid=dx)

    grads = {"w_a_t": g_wa_t, "w_b_t": g_wb_t, "w_out": g_wout, "ca_wq": g_caq, "ca_wkv": g_cakv, "ca_wo": g_cao,
             "ffn_w_in": g_ffn_in, "wo": g_wo}
    return loss_part, dx, grads, small_parts


BIG = ("w_in", "w_out", "ca_wq", "ca_wkv", "ca_wo", "ffn_w_in", "ffn_w_out")
COL_SHARDED = ("w_in", "ca_wkv", "ffn_w_in")
SMALL_ORDER = ("gla_gate_b", "gla_norm_g", "ln1_g", "ln1_b", "ln2_g", "ln2_b", "ln3_g", "ln3_b")


def _gathered_full(name, g):
    if name in COL_SHARDED:
        return g.transpose(1, 0, 2).reshape(g.shape[1], 8 * g.shape[2])
    return g.reshape(8 * g.shape[1], g.shape[2])


def _to_slabs(name, full):
    if name in COL_SHARDED:
        r, cc = full.shape
        s = full.reshape(r, 8, cc // 8).transpose(1, 0, 2)
    else:
        rr, c = full.shape
        s = full.reshape(8, rr // 8, c)
    return s.reshape((4, 2) + s.shape[1:])


def kernel(x, mem, positions, w_in, gla_gate_w2, gla_gate_b, gla_norm_g, w_out, ln1_g, ln1_b, ca_wq, ca_wkv, ca_wo, ln2_g, ln2_b, ffn_w_in, ffn_conv_w, ffn_conv_b, ffn_w_out, ln3_g, ln3_b, loss_target, m_w_in, m_gla_gate_w2, m_gla_gate_b, m_gla_norm_g, m_w_out, m_ln1_g, m_ln1_b, m_ca_wq, m_ca_wkv, m_ca_wo, m_ln2_g, m_ln2_b, m_ffn_w_in, m_ffn_conv_w, m_ffn_conv_b, m_ffn_w_out, m_ln3_g, m_ln3_b, v_w_in, v_gla_gate_w2, v_gla_gate_b, v_gla_norm_g, v_w_out, v_ln1_g, v_ln1_b, v_ca_wq, v_ca_wkv, v_ca_wo, v_ln2_g, v_ln2_b, v_ffn_w_in, v_ffn_conv_w, v_ffn_conv_b, v_ffn_w_out, v_ln3_g, v_ln3_b):
    weights = dict(w_in=w_in, gla_gate_w2=gla_gate_w2, gla_gate_b=gla_gate_b, gla_norm_g=gla_norm_g, w_out=w_out,
                   ln1_g=ln1_g, ln1_b=ln1_b, ca_wq=ca_wq, ca_wkv=ca_wkv, ca_wo=ca_wo, ln2_g=ln2_g, ln2_b=ln2_b,
                   ffn_w_in=ffn_w_in, ffn_conv_w=ffn_conv_w, ffn_conv_b=ffn_conv_b, ffn_w_out=ffn_w_out,
                   ln3_g=ln3_g, ln3_b=ln3_b)
    moms = dict(w_in=(m_w_in, v_w_in), gla_gate_w2=(m_gla_gate_w2, v_gla_gate_w2), gla_gate_b=(m_gla_gate_b, v_gla_gate_b),
                gla_norm_g=(m_gla_norm_g, v_gla_norm_g), w_out=(m_w_out, v_w_out), ln1_g=(m_ln1_g, v_ln1_g),
                ln1_b=(m_ln1_b, v_ln1_b), ca_wq=(m_ca_wq, v_ca_wq), ca_wkv=(m_ca_wkv, v_ca_wkv), ca_wo=(m_ca_wo, v_ca_wo),
                ln2_g=(m_ln2_g, v_ln2_g), ln2_b=(m_ln2_b, v_ln2_b), ffn_w_in=(m_ffn_w_in, v_ffn_w_in),
                ffn_conv_w=(m_ffn_conv_w, v_ffn_conv_w), ffn_conv_b=(m_ffn_conv_b, v_ffn_conv_b),
                ffn_w_out=(m_ffn_w_out, v_ffn_w_out), ln3_g=(m_ln3_g, v_ln3_g), ln3_b=(m_ln3_b, v_ln3_b))
    order = list(weights)
    xi, yi, ci = lax.axis_index("x"), lax.axis_index("y"), lax.axis_index("c")
    me = 4 * xi + 2 * yi + ci

    def travel(n, a):
        return jnp.swapaxes(a, 1, 2) if n in TRANSPOSED else a

    shard = {n: travel(n, weights[n]).astype(BF16)[0] for n in BIG}
    first = _all_gather([shard["w_in"], gla_gate_w2.astype(BF16)[0], ffn_conv_w[0]], "ag_first")
    p = _prepare_sub1({"w_in": first[0], "gla_gate_w2": first[1]})
    p["cw"] = _prepare_conv(first[2], ffn_conv_b)
    later = ("w_out", "ca_wq", "ca_wkv", "ca_wo", "ffn_w_in", "ffn_w_out")
    srcs = [shard[n] for n in later]
    lands = [_landing(shard[n].shape, BF16, shard[n], me) for n in later]
    send, recv, srcs, lands, token = _spread_start(srcs, lands, first[0], True, "ag_rest_start")

    class stages:
        pass

    stages.token = token

    def arrived(lo, hi, after, name):
        return _spread_wait(send[lo:hi], recv[lo:hi], srcs[lo:hi], lands[lo:hi], after, True, name)

    def sub2(after):
        g = dict(zip(later[:4], arrived(0, 4, after, "ag_wait_attn")))
        w = {n: _gathered_full(n, g[n]) for n in ("w_out", "ca_wq", "ca_wo")}
        w["ca_wkv"] = g["ca_wkv"]
        return w

    stages.sub2 = sub2
    stages.ffn_in = lambda after: _prepare_ffn_in(arrived(4, 5, after, "ag_wait_ffn_in")[0])
    stages.ffn_out = lambda after: _prepare_ffn_out(arrived(5, 6, after, "ag_wait_ffn_out")[0])
    sent = {}

    def grads_out(group, slabs):
        names = list(slabs)
        srcs16 = [slabs[n][1] for n in names]
        zones = [_landing(s.shape[1:], BF16, jnp.zeros(s.shape[1:], BF16), me) for s in srcs16]
        snd, rcv, s_thru, l_thru, tok = _spread_start(srcs16, zones, srcs16[0], False, f"rs_{group}_start")
        own32 = [slabs[n][0](me) if callable(slabs[n][0]) else slabs[n][0] for n in names]
        sent[group] = (names, own32, (snd, rcv, s_thru, l_thru))
        return tok

    stages.grads_out = grads_out
    small_sent = []

    def small_out(parts):
        packed = jnp.concatenate([parts[n] for n in SMALL_ORDER] + [parts["conv"],
                                 parts["gla_gate_w2"].reshape(SUBLANES, -1)], axis=1)
        packed = jnp.pad(packed, ((0, 0), (0, (-packed.shape[1]) % 2048)))
        zone = _landing(packed.shape, F32, packed, me)
        snd, rcv, s_thru, l_thru, tok = _spread_start([packed], [zone], packed, True, "ag_small_start")
        small_sent.append((snd, rcv, s_thru, l_thru))
        return tok

    stages.small_out = small_out
    small = dict(gla_gate_b=gla_gate_b, gla_norm_g=gla_norm_g, ln1_g=ln1_g, ln1_b=ln1_b, ln2_g=ln2_g, ln2_b=ln2_b,
                 ln3_g=ln3_g, ln3_b=ln3_b)

    loss_part, dx, grads, small_parts = _local_step(x[0], mem[0], positions[0], loss_target[0], p, small, stages)
    loss = lax.psum(jnp.sum(loss_part), ("x", "y", "c"))

    out = {}
    (allp,) = _spread_wait(*small_sent[0], dx, True, "ag_small_wait")
    dev_sum, row_sum = _small_reduce(allp)

    me1 = me.reshape(1).astype(jnp.int32)

    def finish_group(group, after):
        names, own32, handles = sent[group]
        landed = _spread_wait(*handles, after, False, f"rs_{group}_wait")
        for n, own, land in zip(names, own32, landed):
            m_, v_ = moms[n]
            res4 = _adamw_direct(travel(n, weights[n]), travel(n, m_), travel(n, v_), own, land, me1, f"adamw_{n}")
            out[n] = [travel(n, a) for a in res4]

    finish_group("ffn", dx)
    finish_group("attn", dx)
    off = 0
    for n in SMALL_ORDER:
        width = weights[n].shape[1]
        g = row_sum[0:1, off:off + width]
        off += width
        m_, v_ = moms[n]
        out[n] = _adamw(weights[n], m_, v_, g, f"adamw_{n}")
    conv_g = dev_sum[:, off:off + 2 * D_FF]
    off += 2 * D_FF
    g_cb = conv_g[3:4]
    out["ffn_conv_b"] = _adamw(ffn_conv_b, m_ffn_conv_b, v_ffn_conv_b, g_cb, "adamw_ffn_conv_b")
    wsh = ffn_conv_w.shape[2]
    g_cw = lax.dynamic_slice_in_dim(conv_g[0:3], me * wsh, wsh, axis=1)
    out["ffn_conv_w"] = _adamw(ffn_conv_w[0], m_ffn_conv_w[0], v_ffn_conv_w[0], g_cw, "adamw_ffn_conv_w")
    w2_g = dev_sum[:, off:off + GLA_RANK * GLA_HEADS * GLA_DK // SUBLANES].reshape(GLA_RANK, GLA_HEADS * GLA_DK)
    wsh2 = gla_gate_w2.shape[2]
    g_w2 = lax.dynamic_slice_in_dim(w2_g, me * wsh2, wsh2, axis=1)
    out["gla_gate_w2"] = _adamw(gla_gate_w2[0], m_gla_gate_w2[0], v_gla_gate_w2[0], g_w2, "adamw_gla_gate_w2")
    finish_group("w_in", [o[1] for o in out.values()])

    def shaped(n, a):
        return a.reshape(weights[n].shape)

    res = [loss, dx[None]]
    for k in range(4):
        res += [shaped(n, out[n][k]) for n in order]
    return tuple(res)


def _adamw_direct(w, m, v, own, land, me, name):
    _, r, c = w.shape
    tr, tc = _tile2d(r, c)
    blk = pl.BlockSpec((None, tr, tc), lambda i, j, s: (0, i, j))
    if own.ndim == 2:
        mine = pl.BlockSpec((tr, tc), lambda i, j, s: (i, j))
    else:
        mine = pl.BlockSpec((None, tr, tc), lambda i, j, s: (s[0], i, j))
    slots = [pl.BlockSpec((None, tr, tc), lambda i, j, s, k=k: (k, i, j)) for k in range(8)]

    def body(s_ref, w_ref, m_ref, v_ref, p_ref, *rest):
        slot_refs, (g_ref, d_ref, nm_ref, nv_ref) = rest[:8], rest[8:]
        g = p_ref[...]
        for sr in slot_refs:
            g = g + sr[...].astype(F32)
        d_ref[...], nm_ref[...], nv_ref[...] = _adamw_math(w_ref[...], m_ref[...], v_ref[...], g)
        g_ref[...] = g

    gs = pltpu.PrefetchScalarGridSpec(num_scalar_prefetch=1, grid=(r // tr, c // tc),
                                      in_specs=[blk, blk, blk, mine] + slots, out_specs=[blk] * 4)
    return pl.pallas_call(body, name=name, grid_spec=gs, out_shape=[jax.ShapeDtypeStruct((1, r, c), F32)] * 4,
                          compiler_params=_params(("parallel", "parallel")))(me, w, m, v, own, *([land] * 8))
```

```python
import jax
import jax.numpy as jnp
from jax import lax
from jax.experimental import pallas as pl
from jax.experimental.pallas import tpu as pltpu

F32 = jnp.float32
BF16 = jnp.bfloat16
MESH = pl.DeviceIdType.MESH

D_MODEL = 2048
LN_EPS = 1e-5
GLA_HEADS = 4
GLA_DV = 256
GLA_DK = 128
GLA_RANK = 16
GLA_TAU = 16.0
GLA_CHUNK = 64
DIL_HD = 128
DIL_HEADS = 8
DIL_BAND = 128
DIL_DILATIONS = (1, 4, 16)
ROPE_THETA = 500000.0
ROPE_DIMS = 32
CA_HEADS = 4
CA_HD = 512
D_FF = 5504
ALPHA = 2.0 ** 0.25
ADAM_LR = 0.001
ADAM_B1 = 0.9
ADAM_B2 = 0.999
ADAM_EPS = 1e-08
ADAM_WD = 0.01
ADAM_STEP = 10

LANES = 128
SUBLANES = 8
VMEM_LIMIT = 56 * 1024 * 1024

GLA_W = 2 * GLA_HEADS * GLA_DK + 2 * GLA_HEADS * GLA_DV
HA_W = GLA_W + LANES
HB_W = 3 * DIL_HEADS * DIL_HD
FFP = 5632
NEG = -1e30


def _params(sem):
    return pltpu.CompilerParams(dimension_semantics=sem, vmem_limit_bytes=VMEM_LIMIT)


def _sigmoid(x):
    return 1.0 / (1.0 + jnp.exp(-x))


def _dot(a, b, dn, precision=None):
    return lax.dot_general(a, b, (dn, ((), ())), preferred_element_type=F32, precision=precision)


NN = ((1,), (0,))
NT = ((1,), (1,))
TN = ((0,), (0,))


def _bf(v):
    return v if v.dtype == BF16 else v.astype(BF16)


def _matmul(a, b, kind, out_dtype, tm, tn, tk, name, resid=None, resid_scale=1.0, b_k_off=0, b_slabs=False,
            out_slabs=False, also_bf16=False, dep=None):
    if b_slabs:
        assert kind != "nt" and b.shape[2] == tn
        k2, n = b.shape[1], b.shape[0] * tn
    elif kind == "nt":
        n, k2 = b.shape
    else:
        k2, n = b.shape
    (k, m) = a.shape if kind == "tn" else a.shape[::-1]
    assert k2 >= k and (k2 == k or not b_slabs) and m % tm == 0 and n % tn == 0 and k % tk == 0, \
        (name, a.shape, b.shape, tm, tn, tk)
    nk = k // tk
    dn = {"nn": NN, "nt": NT, "tn": TN}[kind]
    a_spec = pl.BlockSpec((tk, tm), lambda i, j, kk: (kk, i)) if kind == "tn" else pl.BlockSpec((tm, tk), lambda i, j, kk: (i, kk))
    if b_slabs:
        b_spec = pl.BlockSpec((None, tk, tn), lambda i, j, kk: (j, kk, 0))
    elif kind == "nt":
        b_spec = pl.BlockSpec((tn, tk), lambda i, j, kk: (j, kk + b_k_off))
    else:
        b_spec = pl.BlockSpec((tk, tn), lambda i, j, kk: (kk + b_k_off, j))
    if out_slabs:
        o_spec = pl.BlockSpec((None, tm, tn), lambda i, j, kk: (j, i, 0))
        o_shape = (n // tn, m, tn)
    else:
        o_spec = pl.BlockSpec((tm, tn), lambda i, j, kk: (i, j))
        o_shape = (m, n)
    has_resid = resid is not None

    n_in = 2 + int(has_resid) + int(dep is not None)

    def body(*refs):
        a_ref, b_ref = refs[:2]
        r_ref = refs[2] if has_resid else None
        o_ref = refs[n_in]
        ob_ref = refs[n_in + 1] if also_bf16 else None
        part = _dot(_bf(a_ref[...]), _bf(b_ref[...]), dn)

        def finish(acc):
            if has_resid:
                acc = acc + resid_scale * r_ref[...].astype(F32)
            o_ref[...] = acc.astype(out_dtype)
            if also_bf16:
                ob_ref[...] = acc.astype(BF16)

        if nk == 1:
            finish(part)
        else:
            acc_ref = refs[-1]
            kk = pl.program_id(2)

            @pl.when(kk == 0)
            def _():
                acc_ref[...] = part

            @pl.when(kk > 0)
            def _():
                acc_ref[...] += part

            @pl.when(kk == nk - 1)
            def _():
                finish(acc_ref[...])

    in_specs = [a_spec, b_spec] + ([o_spec] if has_resid else [])
    args = (a, b) + ((resid,) if has_resid else ())
    if dep is not None:
        in_specs.append(pl.BlockSpec((SUBLANES, LANES), lambda i, j, kk: (0, 0)))
        args += (dep,)
    o_struct = jax.ShapeDtypeStruct(o_shape, out_dtype)
    return pl.pallas_call(
        body, name=name, out_shape=[o_struct, jax.ShapeDtypeStruct(o_shape, BF16)] if also_bf16 else o_struct,
        grid=(m // tm, n // tn, nk), in_specs=in_specs, out_specs=[o_spec, o_spec] if also_bf16 else o_spec,
        scratch_shapes=[pltpu.VMEM((tm, tn), F32)] if nk > 1 else [],
        compiler_params=_params(("parallel", "parallel", "arbitrary")),
    )(*args)


def _ln_core(xres, f):
    p = ALPHA * xres + f
    mu = jnp.mean(p, axis=-1, keepdims=True)
    xc = p - mu
    var = jnp.mean(xc * xc, axis=-1, keepdims=True)
    rstd = lax.rsqrt(var + LN_EPS)
    return xc * rstd, rstd


def _rows8(v):
    r, c = v.shape
    return jnp.sum(v.reshape(r // SUBLANES, SUBLANES, c), axis=0)


def _ln_fwd(xres, f, g, b, name, transposed, tr=256):
    t, d = xres.shape
    row = pl.BlockSpec((tr, d), lambda i: (i, 0))
    vec = pl.BlockSpec((1, d), lambda i: (0, 0))

    def body(x_ref, f_ref, g_ref, b_ref, y_ref, yb_ref, *yt_ref):
        xhat, _ = _ln_core(x_ref[...], f_ref[...])
        y = xhat * g_ref[...] + b_ref[...]
        y_ref[...] = y
        yb = y.astype(BF16)
        yb_ref[...] = yb
        if transposed:
            yt_ref[0][...] = yb.T

    out_specs = [row, row] + ([pl.BlockSpec((d, tr), lambda i: (0, i))] if transposed else [])
    out_shape = [jax.ShapeDtypeStruct((t, d), F32), jax.ShapeDtypeStruct((t, d), BF16)] \
        + ([jax.ShapeDtypeStruct((d, t), BF16)] if transposed else [])
    return pl.pallas_call(
        body, name=name, grid=(t // tr,), in_specs=[row, row, vec, vec], out_specs=out_specs, out_shape=out_shape,
        compiler_params=_params(("parallel",)),
    )(xres, f, g, b)


def _ln_bwd(xres, f, g, b, dy_or_target, loss_head, name, tr=256):
    t, d = xres.shape
    row = pl.BlockSpec((tr, d), lambda i: (i, 0))
    vec = pl.BlockSpec((1, d), lambda i: (0, 0))
    acc = pl.BlockSpec((SUBLANES, d), lambda i: (0, 0))
    lacc = pl.BlockSpec((SUBLANES, LANES), lambda i: (0, 0))

    def body(x_ref, f_ref, g_ref, b_ref, t_ref, dp_ref, dpb_ref, dg_ref, db_ref, *rest):
        i = pl.program_id(0)
        xhat, rstd = _ln_core(x_ref[...], f_ref[...])
        if loss_head:
            err = xhat * g_ref[...] + b_ref[...] - t_ref[...]
            dy = err * (1.0 / d)
            sq = err * err
            lanes = sq[:, :LANES]
            for kk in range(1, d // LANES):
                lanes = lanes + sq[:, kk * LANES:(kk + 1) * LANES]
            lpart = _rows8(lanes) * (0.5 / d)
        else:
            dy = t_ref[...]
        dxh = dy * g_ref[...]
        m1 = jnp.mean(dxh, axis=-1, keepdims=True)
        m2 = jnp.mean(dxh * xhat, axis=-1, keepdims=True)
        dp = rstd * (dxh - m1 - xhat * m2)
        dp_ref[...] = dp
        dpb_ref[...] = dp.astype(BF16)
        dgp = _rows8(dy * xhat)
        dbp = _rows8(dy)

        @pl.when(i == 0)
        def _():
            dg_ref[...] = dgp
            db_ref[...] = dbp
            if loss_head:
                rest[0][...] = lpart

        @pl.when(i > 0)
        def _():
            dg_ref[...] += dgp
            db_ref[...] += dbp
            if loss_head:
                rest[0][...] += lpart

    out_shape = [jax.ShapeDtypeStruct((t, d), F32), jax.ShapeDtypeStruct((t, d), BF16),
                 jax.ShapeDtypeStruct((SUBLANES, d), F32), jax.ShapeDtypeStruct((SUBLANES, d), F32)]
    out_specs = [row, row, acc, acc]
    if loss_head:
        out_shape.append(jax.ShapeDtypeStruct((SUBLANES, LANES), F32))
        out_specs.append(lacc)
    return pl.pallas_call(
        body, name=name, grid=(t // tr,), in_specs=[row, row, vec, vec, row], out_specs=out_specs,
        out_shape=out_shape, compiler_params=_params(("arbitrary",)),
    )(xres, f, g, b, dy_or_target)


def _gla_gates(glr, w2, gb):
    z = _dot(_bf(glr), w2, NN) + gb
    lg = (jnp.minimum(z, 0.0) - jnp.log(1.0 + jnp.exp(-jnp.abs(z)))) * (1.0 / GLA_TAU)
    c = z.shape[0]
    ri = lax.broadcasted_iota(jnp.int32, (c, c), 0)
    ci = lax.broadcasted_iota(jnp.int32, (c, c), 1)
    tri = (ci <= ri).astype(F32)
    bcum = _dot(tri, lg, NN, precision=lax.Precision.HIGHEST)
    blast = jnp.sum(lg, axis=0, keepdims=True)
    return z, bcum, blast, tri


def _gla_specs(t):
    c = GLA_CHUNK
    return c, t // c


def _gla_fwd(h_a, w2p, gate_b, norm_g):
    t = h_a.shape[0]
    c, n = _gla_specs(t)
    hk, hv = GLA_HEADS * GLA_DK, GLA_HEADS * GLA_DV
    scale = GLA_DK ** -0.5

    def body(q_ref, k_ref, v_ref, r_ref, glr_ref, w2_ref, gb_ref, ng_ref, og_ref, oraw_ref, sb_ref, st_ref):
        i = pl.program_id(0)

        @pl.when(i == 0)
        def _():
            st_ref[...] = jnp.zeros_like(st_ref)

        _, bcum, blast, _ = _gla_gates(glr_ref[...], w2_ref[...], gb_ref[...])
        ri = lax.broadcasted_iota(jnp.int32, (c, c), 0)
        ci = lax.broadcasted_iota(jnp.int32, (c, c), 1)
        causal = ci <= ri
        for h in range(GLA_HEADS):
            ks = slice(h * GLA_DK, (h + 1) * GLA_DK)
            vs = slice(h * GLA_DV, (h + 1) * GLA_DV)
            b_h, bl_h = bcum[:, ks], blast[:, ks]
            q_h, k_h = q_ref[:, ks], k_ref[:, ks]
            v_h = _bf(v_ref[:, vs])
            qi = _bf(q_h * scale * jnp.exp(b_h))
            ki = _bf(k_h * jnp.exp(-b_h))
            ke = _bf(k_h * jnp.exp(bl_h - b_h))
            st = st_ref[h]
            sb_ref[0, h] = st
            a = jnp.where(causal, _dot(qi, ki, NT), 0.0)
            o = _dot(_bf(a), v_h, NN) + _dot(qi, _bf(st), NT)
            st_ref[h] = st * jnp.exp(bl_h) + _dot(v_h, ke, TN)
            oraw_ref[:, vs] = o
            mu = jnp.mean(o, axis=-1, keepdims=True)
            oc = o - mu
            var = jnp.mean(oc * oc, axis=-1, keepdims=True)
            xh = oc * lax.rsqrt(var + LN_EPS)
            r_h = r_ref[:, vs]
            og_ref[:, vs] = (xh * ng_ref[:, vs] * (r_h * _sigmoid(r_h))).astype(BF16)

    return pl.pallas_call(
        body, name="gla_fwd", grid=(n,),
        in_specs=[pl.BlockSpec((c, hk), lambda i: (i, 0)), pl.BlockSpec((c, hk), lambda i: (i, 1)),
                  pl.BlockSpec((c, hv), lambda i: (i, 1)), pl.BlockSpec((c, hv), lambda i: (i, 2)),
                  pl.BlockSpec((c, LANES), lambda i: (i, GLA_W // LANES)),
                  pl.BlockSpec((LANES, hk), lambda i: (0, 0)), pl.BlockSpec((1, hk), lambda i: (0, 0)),
                  pl.BlockSpec((1, hv), lambda i: (0, 0))],
        out_specs=[pl.BlockSpec((c, hv), lambda i: (i, 0)), pl.BlockSpec((c, hv), lambda i: (i, 0)),
                   pl.BlockSpec((1, GLA_HEADS, GLA_DV, GLA_DK), lambda i: (i, 0, 0, 0))],
        out_shape=[jax.ShapeDtypeStruct((t, hv), BF16), jax.ShapeDtypeStruct((t, hv), F32),
                   jax.ShapeDtypeStruct((n, GLA_HEADS, GLA_DV, GLA_DK), F32)],
        scratch_shapes=[pltpu.VMEM((GLA_HEADS, GLA_DV, GLA_DK), F32)],
        compiler_params=_params(("arbitrary",)),
    )(h_a, h_a, h_a, h_a, h_a, w2p, gate_b, norm_g)


def _gla_bwd(h_a, w2p, gate_b, norm_g, o_raw, s_before, dmix):
    t = h_a.shape[0]
    c, n = _gla_specs(t)
    hk, hv = GLA_HEADS * GLA_DK, GLA_HEADS * GLA_DV
    scale = GLA_DK ** -0.5
    rev = lambda i: n - 1 - i

    def body(q_ref, k_ref, v_ref, r_ref, glr_ref, w2_ref, gb_ref, ng_ref, oraw_ref, sb_ref, do_ref,
             dh_ref, dw2_ref, dgb_ref, dng_ref, dst_ref):
        i = pl.program_id(0)

        @pl.when(i == 0)
        def _():
            dst_ref[...] = jnp.zeros_like(dst_ref)

        glr = glr_ref[...]
        z, bcum, blast, tri = _gla_gates(glr, w2_ref[...], gb_ref[...])
        ri = lax.broadcasted_iota(jnp.int32, (c, c), 0)
        ci = lax.broadcasted_iota(jnp.int32, (c, c), 1)
        causal = ci <= ri
        dlg_parts = []
        dng_parts = []
        for h in range(GLA_HEADS):
            ks = slice(h * GLA_DK, (h + 1) * GLA_DK)
            vs = slice(h * GLA_DV, (h + 1) * GLA_DV)
            o = oraw_ref[:, vs]
            mu = jnp.mean(o, axis=-1, keepdims=True)
            oc = o - mu
            var = jnp.mean(oc * oc, axis=-1, keepdims=True)
            rstd = lax.rsqrt(var + LN_EPS)
            xh = oc * rstd
            r_h = r_ref[:, vs]
            sg = _sigmoid(r_h)
            silu = r_h * sg
            dout = do_ref[:, vs]
            ng = ng_ref[:, vs]
            dng_parts.append(_rows8(dout * xh * silu))
            dr = dout * xh * ng * (sg * (1.0 + r_h * (1.0 - sg)))
            dxh = dout * ng * silu
            m1 = jnp.mean(dxh, axis=-1, keepdims=True)
            m2 = jnp.mean(dxh * xh, axis=-1, keepdims=True)
            do_raw = _bf(rstd * (dxh - m1 - xh * m2))
            b_h, bl_h = bcum[:, ks], blast[:, ks]
            q_h, k_h = q_ref[:, ks], k_ref[:, ks]
            v_h = _bf(v_ref[:, vs])
            eb, enb, eend = jnp.exp(b_h), jnp.exp(-b_h), jnp.exp(bl_h - b_h)
            decay = jnp.exp(bl_h)
            qi_f, ki_f, ke_f = q_h * scale * eb, k_h * enb, k_h * eend
            qi, ki, ke = _bf(qi_f), _bf(ki_f), _bf(ke_f)
            st = sb_ref[0, h]
            dst = dst_ref[h]
            dst_b = _bf(dst)
            a = _bf(jnp.where(causal, _dot(qi, ki, NT), 0.0))
            da = _bf(jnp.where(causal, _dot(do_raw, v_h, NT), 0.0))
            dv = _dot(a, do_raw, TN) + _dot(ke, dst_b, NT)
            dqi = _dot(da, ki, NN) + _dot(do_raw, _bf(st), NN)
            dki = _dot(da, qi, TN)
            dke = _dot(v_h, dst_b, NN)
            dst_ref[h] = _dot(do_raw, qi, TN) + dst * decay
            dbl = decay * jnp.sum(st * dst, axis=0, keepdims=True) + jnp.sum(dke * ke_f, axis=0, keepdims=True)
            dbc = dqi * qi_f - dki * ki_f - dke * ke_f
            dlg_parts.append(_dot(tri, dbc, TN, precision=lax.Precision.HIGHEST) + dbl)
            dh_ref[:, ks] = (dqi * eb * scale).astype(BF16)
            dh_ref[:, hk + h * GLA_DK: hk + (h + 1) * GLA_DK] = (dki * enb + dke * eend).astype(BF16)
            dh_ref[:, 2 * hk + h * GLA_DV: 2 * hk + (h + 1) * GLA_DV] = dv.astype(BF16)
            dh_ref[:, 2 * hk + hv + h * GLA_DV: 2 * hk + hv + (h + 1) * GLA_DV] = dr.astype(BF16)
        dlg = jnp.concatenate(dlg_parts, axis=1)
        dz = dlg * (1.0 / GLA_TAU) * _sigmoid(-z)
        dz_b = _bf(dz)
        dh_ref[:, GLA_W:] = _dot(dz_b, w2_ref[...], NT).astype(BF16)
        dw2p = _dot(_bf(glr), dz_b, TN)
        dgbp = _rows8(dz)
        dngp = jnp.concatenate(dng_parts, axis=1)

        @pl.when(i == 0)
        def _():
            dw2_ref[...] = dw2p
            dgb_ref[...] = dgbp
            dng_ref[...] = dngp

        @pl.when(i > 0)
        def _():
            dw2_ref[...] += dw2p
            dgb_ref[...] += dgbp
            dng_ref[...] += dngp

    return pl.pallas_call(
        body, name="gla_bwd", grid=(n,),
        in_specs=[pl.BlockSpec((c, hk), lambda i: (rev(i), 0)), pl.BlockSpec((c, hk), lambda i: (rev(i), 1)),
                  pl.BlockSpec((c, hv), lambda i: (rev(i), 1)), pl.BlockSpec((c, hv), lambda i: (rev(i), 2)),
                  pl.BlockSpec((c, LANES), lambda i: (rev(i), GLA_W // LANES)),
                  pl.BlockSpec((LANES, hk), lambda i: (0, 0)), pl.BlockSpec((1, hk), lambda i: (0, 0)),
                  pl.BlockSpec((1, hv), lambda i: (0, 0)),
                  pl.BlockSpec((c, hv), lambda i: (rev(i), 0)),
                  pl.BlockSpec((1, GLA_HEADS, GLA_DV, GLA_DK), lambda i: (rev(i), 0, 0, 0)),
                  pl.BlockSpec((c, hv), lambda i: (rev(i), 0))],
        out_specs=[pl.BlockSpec((c, HA_W), lambda i: (rev(i), 0)),
                   pl.BlockSpec((LANES, hk), lambda i: (0, 0)),
                   pl.BlockSpec((SUBLANES, hk), lambda i: (0, 0)),
                   pl.BlockSpec((SUBLANES, hv), lambda i: (0, 0))],
        out_shape=[jax.ShapeDtypeStruct((t, HA_W), BF16), jax.ShapeDtypeStruct((LANES, hk), F32),
                   jax.ShapeDtypeStruct((SUBLANES, hk), F32), jax.ShapeDtypeStruct((SUBLANES, hv), F32)],
        scratch_shapes=[pltpu.VMEM((GLA_HEADS, GLA_DV, GLA_DK), F32)],
        compiler_params=_params(("arbitrary",)),
    )(h_a, h_a, h_a, h_a, h_a, w2p, gate_b, norm_g, o_raw, s_before, dmix)


def _rope_tables(positions):
    half = ROPE_DIMS // 2
    inv_freq = ROPE_THETA ** (-jnp.arange(0, ROPE_DIMS, 2, dtype=F32) / ROPE_DIMS)
    ang = positions.astype(F32).reshape(-1, 1) * inv_freq
    cos, sin = jnp.cos(ang), jnp.sin(ang)
    t = cos.shape[0]
    one = jnp.ones((t, DIL_HD - ROPE_DIMS), F32)
    zero = jnp.zeros((t, DIL_HD - ROPE_DIMS), F32)
    zh = jnp.zeros((t, half), F32)
    return (jnp.concatenate([cos, cos, one], axis=1), jnp.concatenate([-sin, zh, zero], axis=1),
            jnp.concatenate([zh, sin, zero], axis=1))


def _rope_apply(x, c, s1, s2):
    half = ROPE_DIMS // 2
    return x * c + pltpu.roll(x, DIL_HD - half, 1) * s1 + pltpu.roll(x, half, 1) * s2


def _rope_apply_t(dy, c, s1, s2):
    half = ROPE_DIMS // 2
    return dy * c + pltpu.roll(dy * s1, half, 1) + pltpu.roll(dy * s2, DIL_HD - half, 1)


def _rope_fwd(h_b, tabs, tr=256):
    t = h_b.shape[0]
    w = DIL_HEADS * DIL_HD
    scale = DIL_HD ** -0.5
    tab = pl.BlockSpec((tr, DIL_HD), lambda i: (i, 0))
    outb = pl.BlockSpec((tr, w), lambda i: (i, 0))

    def body(q_ref, k_ref, c_ref, s1_ref, s2_ref, qo_ref, ko_ref):
        c, s1, s2 = c_ref[...], s1_ref[...], s2_ref[...]
        for h in range(DIL_HEADS):
            hs = slice(h * DIL_HD, (h + 1) * DIL_HD)
            qo_ref[:, hs] = _rope_apply(q_ref[:, hs] * scale, c, s1, s2)
            ko_ref[:, hs] = _rope_apply(k_ref[:, hs], c, s1, s2)

    return pl.pallas_call(
        body, name="rope_fwd", grid=(t // tr,),
        in_specs=[pl.BlockSpec((tr, w), lambda i: (i, 0)), pl.BlockSpec((tr, w), lambda i: (i, 1)), tab, tab, tab],
        out_specs=[outb, outb],
        out_shape=[jax.ShapeDtypeStruct((t, w), F32)] * 2,
        compiler_params=_params(("parallel",)),
    )(h_b, h_b, *tabs)


def _dil_dh(dq, dk, dv, tabs, tr=256):
    t, w = dq.shape
    scale = DIL_HD ** -0.5
    tab = pl.BlockSpec((tr, DIL_HD), lambda i: (i, 0))
    inb = pl.BlockSpec((tr, w), lambda i: (i, 0))

    def body(dq_ref, dk_ref, dv_ref, c_ref, s1_ref, s2_ref, o_ref):
        c, s1, s2 = c_ref[...], s1_ref[...], s2_ref[...]
        for h in range(DIL_HEADS):
            hs = slice(h * DIL_HD, (h + 1) * DIL_HD)
            o_ref[:, h * DIL_HD:(h + 1) * DIL_HD] = (_rope_apply_t(dq_ref[:, hs], c, s1, s2) * scale).astype(BF16)
            o_ref[:, w + h * DIL_HD: w + (h + 1) * DIL_HD] = _rope_apply_t(dk_ref[:, hs], c, s1, s2).astype(BF16)
        o_ref[:, 2 * w:] = dv_ref[...].astype(BF16)

    return pl.pallas_call(
        body, name="dil_dh", grid=(t // tr,), in_specs=[inb] * 3 + [tab] * 3,
        out_specs=pl.BlockSpec((tr, 3 * w), lambda i: (i, 0)),
        out_shape=jax.ShapeDtypeStruct((t, 3 * w), BF16), compiler_params=_params(("parallel",)),
    )(dq, dk, dv, *tabs)


def _band_masks(not_first):
    r = lax.broadcasted_iota(jnp.int32, (DIL_BAND, 2 * DIL_BAND), 0)
    c = lax.broadcasted_iota(jnp.int32, (DIL_BAND, 2 * DIL_BAND), 1)
    nf = jnp.full((DIL_BAND, 2 * DIL_BAND), not_first, jnp.int32)
    look_back = jnp.logical_and(jnp.logical_and(c < DIL_BAND, c >= r), nf > 0)
    own_band = jnp.logical_and(c >= DIL_BAND, (c - DIL_BAND) <= r)
    return jnp.logical_or(look_back, own_band)


def _gather_rows(dst_ref, src_ref, t, d, cast=None):
    n = t // d
    for r in range(d):
        v = src_ref[pl.ds(r, n, stride=d), :] if d > 1 else src_ref[...]
        dst_ref[r * n:(r + 1) * n, :] = v if cast is None else v.astype(cast)


def _tri_mask():
    r = lax.broadcasted_iota(jnp.int32, (DIL_BAND, DIL_BAND), 0)
    c = lax.broadcasted_iota(jnp.int32, (DIL_BAND, DIL_BAND), 1)
    return c <= r


def _dil_fwd_all(qr, kr, h_b):
    t = qr.shape[0]
    nbands = t // DIL_BAND
    nbr = len(DIL_DILATIONS)
    hoff = DIL_HEADS

    def col(off):
        return pl.BlockSpec((t, DIL_HD), lambda h: (0, off + h), pipeline_mode=pl.Buffered(1))

    outb = pl.BlockSpec((t, DIL_HD), lambda h: (0, h))

    def body(q_ref, k_ref, v_ref, ob_ref, of_ref, lt_ref, qs, ks, vs, os_, ls_, *br):
        obr, lbr = br[:nbr], br[nbr:]
        for bi, d in enumerate(DIL_DILATIONS):
            n = t // d
            nb = n // DIL_BAND
            _gather_rows(qs, q_ref, t, d, BF16)
            _gather_rows(ks, k_ref, t, d, BF16)
            _gather_rows(vs, v_ref, t, d, BF16)
            s = jnp.where(_tri_mask(), _dot(qs[0:DIL_BAND, :], ks[0:DIL_BAND, :], NT), NEG)
            m = jnp.max(s, axis=-1, keepdims=True)
            pr = jnp.exp(s - m)
            den = jnp.sum(pr, axis=-1, keepdims=True)
            os_[0:DIL_BAND, :] = _dot(_bf(pr), vs[0:DIL_BAND, :], NN) / den
            ls_[0:DIL_BAND, :] = jnp.broadcast_to(m + jnp.log(den), (DIL_BAND, DIL_HD))

            def band(b, carry, nb=nb):
                st = pl.multiple_of((b - 1) * DIL_BAND, DIL_BAND)
                cur = pl.ds(st + DIL_BAND, DIL_BAND)
                both = pl.ds(st, 2 * DIL_BAND)
                not_first = ((b % nb) != 0).astype(jnp.int32)
                s = jnp.where(_band_masks(not_first), _dot(qs[cur, :], ks[both, :], NT), NEG)
                m = jnp.max(s, axis=-1, keepdims=True)
                pr = jnp.exp(s - m)
                den = jnp.sum(pr, axis=-1, keepdims=True)
                os_[cur, :] = _dot(_bf(pr), vs[both, :], NN) / den
                ls_[cur, :] = jnp.broadcast_to(m + jnp.log(den), (DIL_BAND, DIL_HD))
                return carry

            lax.fori_loop(1, nbands, band, 0, unroll=16)
            for r in range(d):
                dst = pl.ds(r, n, stride=d) if d > 1 else slice(None)
                obr[bi][dst, :] = os_[r * n:(r + 1) * n, :]
                lbr[bi][dst, :] = ls_[r * n:(r + 1) * n, :]
        rows = 512
        for c0 in range(0, t, rows):
            sl = slice(c0, c0 + rows)
            la, lb, lc = lbr[0][sl, :], lbr[1][sl, :], lbr[2][sl, :]
            m = jnp.maximum(jnp.maximum(la, lb), lc)
            ea, eb, ec = jnp.exp(la - m), jnp.exp(lb - m), jnp.exp(lc - m)
            den = ea + eb + ec
            o = (ea * obr[0][sl, :] + eb * obr[1][sl, :] + ec * obr[2][sl, :]) / den
            ob_ref[sl, :] = o.astype(BF16)
            of_ref[sl, :] = o
            lt_ref[sl, :] = m + jnp.log(den)

    w = DIL_HEADS * DIL_HD
    vm = lambda dt: pltpu.VMEM((t, DIL_HD), dt)
    return pl.pallas_call(
        body, name="dil_fwd", grid=(DIL_HEADS,), in_specs=[col(0), col(0), col(2 * hoff)],
        out_specs=[outb, outb, outb],
        out_shape=[jax.ShapeDtypeStruct((t, w), BF16), jax.ShapeDtypeStruct((t, w), F32),
                   jax.ShapeDtypeStruct((t, w), F32)],
        scratch_shapes=[vm(BF16)] * 3 + [vm(F32)] * 2 + [vm(F32)] * (2 * nbr),
        compiler_params=_params(("parallel",)),
    )(qr, kr, h_b)


def _dil_bwd_all(qr, kr, h_b, dmix, o_d, lse_tot):
    t = qr.shape[0]
    nbands = t // DIL_BAND
    hoff = DIL_HEADS

    def col(off):
        return pl.BlockSpec((t, DIL_HD), lambda h: (0, off + h), pipeline_mode=pl.Buffered(1))

    outb = pl.BlockSpec((t, DIL_HD), lambda h: (0, h))

    def body(q_ref, k_ref, v_ref, do_ref, o_ref, l_ref, dq_ref, dk_ref, dv_ref,
             qs, ks, vs, dos, lss, dds, dqs, acck, accv, ddt):
        rows = 512
        for c0 in range(0, t, rows):
            prod = do_ref[c0:c0 + rows, :] * o_ref[c0:c0 + rows, :]
            ddt[c0:c0 + rows, :] = jnp.broadcast_to(jnp.sum(prod, axis=-1, keepdims=True), (rows, DIL_HD))
        for bi, d in enumerate(DIL_DILATIONS):
            n = t // d
            nb = n // DIL_BAND
            _gather_rows(qs, q_ref, t, d, BF16)
            _gather_rows(ks, k_ref, t, d, BF16)
            _gather_rows(vs, v_ref, t, d, BF16)
            _gather_rows(dos, do_ref, t, d, BF16)
            _gather_rows(lss, l_ref, t, d)
            _gather_rows(dds, ddt, t, d)
            b0 = slice(0, DIL_BAND)
            s = jnp.where(_tri_mask(), _dot(qs[b0, :], ks[b0, :], NT), NEG)
            pr = jnp.exp(s - lss[b0, :])
            ds = _bf(pr * (_dot(dos[b0, :], vs[b0, :], NT) - dds[b0, :]))
            dqs[b0, :] = _dot(ds, ks[b0, :], NN)
            acck[DIL_BAND:2 * DIL_BAND, :] = _dot(ds, qs[b0, :], TN)
            accv[DIL_BAND:2 * DIL_BAND, :] = _dot(_bf(pr), dos[b0, :], TN)

            def band(b, carry, nb=nb):
                st = pl.multiple_of((b - 1) * DIL_BAND, DIL_BAND)
                cur = pl.ds(st + DIL_BAND, DIL_BAND)
                both = pl.ds(st, 2 * DIL_BAND)
                back_rows = pl.ds(st + DIL_BAND, DIL_BAND)
                own_rows = pl.ds(st + 2 * DIL_BAND, DIL_BAND)
                not_first = ((b % nb) != 0).astype(jnp.int32)
                qb, dob, lb, ddb = qs[cur, :], dos[cur, :], lss[cur, :], dds[cur, :]
                kcat, vcat = ks[both, :], vs[both, :]
                s = jnp.where(_band_masks(not_first), _dot(qb, kcat, NT), NEG)
                pr = jnp.exp(s - jnp.concatenate([lb, lb], axis=1))
                ds = _bf(pr * (_dot(dob, vcat, NT) - jnp.concatenate([ddb, ddb], axis=1)))
                dqs[cur, :] = _dot(ds, kcat, NN)
                dkk = _dot(ds, qb, TN)
                dvv = _dot(_bf(pr), dob, TN)
                acck[back_rows, :] += dkk[:DIL_BAND]
                accv[back_rows, :] += dvv[:DIL_BAND]
                acck[own_rows, :] = dkk[DIL_BAND:]
                accv[own_rows, :] = dvv[DIL_BAND:]
                return carry

            lax.fori_loop(1, nbands, band, 0, unroll=8)
            for r in range(d):
                lo = r * n
                if d == 1:
                    dq_ref[...] = dqs[...]
                    dk_ref[...] = acck[DIL_BAND:DIL_BAND + t, :]
                    dv_ref[...] = accv[DIL_BAND:DIL_BAND + t, :]
                else:
                    dst = pl.ds(r, n, stride=d)
                    dq_ref[dst, :] = dq_ref[dst, :] + dqs[lo:lo + n, :]
                    dk_ref[dst, :] = dk_ref[dst, :] + acck[DIL_BAND + lo:DIL_BAND + lo + n, :]
                    dv_ref[dst, :] = dv_ref[dst, :] + accv[DIL_BAND + lo:DIL_BAND + lo + n, :]

    w = DIL_HEADS * DIL_HD
    vm = lambda dt, extra=0: pltpu.VMEM((t + extra, DIL_HD), dt)
    return pl.pallas_call(
        body, name="dil_bwd", grid=(DIL_HEADS,),
        in_specs=[col(0), col(0), col(2 * hoff), col(hoff), col(0), col(0)], out_specs=[outb] * 3,
        out_shape=[jax.ShapeDtypeStruct((t, w), F32)] * 3,
        scratch_shapes=[vm(BF16)] * 4 + [vm(F32)] * 3 + [vm(F32, DIL_BAND)] * 2 + [vm(F32)],
        compiler_params=_params(("parallel",)),
    )(qr, kr, h_b, dmix, o_d, lse_tot)


def _ca_fwd(q, memkv, tq=512):
    t, d = q.shape
    m = memkv.shape[0]
    scale = CA_HD ** -0.5

    def body(q_ref, k_ref, v_ref, o_ref, ot_ref):
        for h in range(CA_HEADS):
            hs = slice(h * CA_HD, (h + 1) * CA_HD)
            s = _dot(q_ref[:, hs], k_ref[:, hs], NT) * scale
            p = jnp.exp(s - jnp.max(s, axis=-1, keepdims=True))
            p = p / jnp.sum(p, axis=-1, keepdims=True)
            o = _dot(_bf(p), v_ref[:, hs], NN).astype(BF16)
            o_ref[:, hs] = o
            ot_ref[hs, :] = o.T

    return pl.pallas_call(
        body, name="ca_fwd", grid=(t // tq,),
        in_specs=[pl.BlockSpec((tq, d), lambda i: (i, 0)), pl.BlockSpec((m, d), lambda i: (0, 0)),
                  pl.BlockSpec((m, d), lambda i: (0, 1))],
        out_specs=[pl.BlockSpec((tq, d), lambda i: (i, 0)), pl.BlockSpec((d, tq), lambda i: (0, i))],
        out_shape=[jax.ShapeDtypeStruct((t, d), BF16), jax.ShapeDtypeStruct((d, t), BF16)],
        compiler_params=_params(("parallel",)),
    )(q, memkv, memkv)


def _ca_bwd(q, memkv, do, tq=512):
    t, d = q.shape
    m = memkv.shape[0]
    scale = CA_HD ** -0.5

    def body(q_ref, k_ref, v_ref, do_ref, dq_ref, dkv_ref):
        i = pl.program_id(0)

        @pl.when(i == 0)
        def _():
            dkv_ref[...] = jnp.zeros_like(dkv_ref)

        for h in range(CA_HEADS):
            hs = slice(h * CA_HD, (h + 1) * CA_HD)
            q_h, k_h, v_h, do_h = q_ref[:, hs], k_ref[:, hs], v_ref[:, hs], do_ref[:, hs]
            s = _dot(q_h, k_h, NT) * scale
            p = jnp.exp(s - jnp.max(s, axis=-1, keepdims=True))
            p = p / jnp.sum(p, axis=-1, keepdims=True)
            dp = _dot(do_h, v_h, NT)
            ds = _bf(p * (dp - jnp.sum(p * dp, axis=-1, keepdims=True)) * scale)
            dq_ref[:, hs] = _dot(ds, k_h, NN).astype(BF16)
            dkv_ref[:, hs] += _dot(ds, q_h, TN)
            dkv_ref[:, d + h * CA_HD: d + (h + 1) * CA_HD] += _dot(_bf(p), do_h, TN)

    return pl.pallas_call(
        body, name="ca_bwd", grid=(t // tq,),
        in_specs=[pl.BlockSpec((tq, d), lambda i: (i, 0)), pl.BlockSpec((m, d), lambda i: (0, 0)),
                  pl.BlockSpec((m, d), lambda i: (0, 1)), pl.BlockSpec((tq, d), lambda i: (i, 0))],
        out_specs=[pl.BlockSpec((tq, d), lambda i: (i, 0)), pl.BlockSpec((m, 2 * d), lambda i: (0, 0))],
        out_shape=[jax.ShapeDtypeStruct((t, d), BF16), jax.ShapeDtypeStruct((m, 2 * d), F32)],
        compiler_params=_params(("arbitrary",)),
    )(q, memkv, memkv, do)


STRIP = 256


def _shift_down(u, n, row):
    return jnp.where(row >= n, pltpu.roll(u, n, 0), 0.0)


def _shift_up(u, n, row):
    t = u.shape[0]
    return jnp.where(row < t - n, pltpu.roll(u, t - n, 0), 0.0)


def _conv(u, cw_ref, row):
    return ((cw_ref[3:4, :] + cw_ref[0:1, :] * _shift_down(u, 2, row)) + cw_ref[1:2, :] * _shift_down(u, 1, row)) \
        + cw_ref[2:3, :] * u


def _swiglu_fwd(u0, cw):
    t, w = u0.shape[0], u0.shape[1] // 2
    ns = w // STRIP
    col = pl.BlockSpec((t, STRIP), lambda j: (0, j))
    col_up = pl.BlockSpec((t, STRIP), lambda j: (0, ns + j))
    cws = pl.BlockSpec((SUBLANES, STRIP), lambda j: (0, j))
    cws_up = pl.BlockSpec((SUBLANES, STRIP), lambda j: (0, ns + j))

    def body(g_ref, u_ref, cg_ref, cu_ref, a_ref, at_ref):
        row = lax.broadcasted_iota(jnp.int32, (t, STRIP), 0)
        gate = _conv(g_ref[...].astype(F32), cg_ref, row)
        up = _conv(u_ref[...].astype(F32), cu_ref, row)
        act = (gate * _sigmoid(gate) * up).astype(BF16)
        a_ref[...] = act
        at_ref[...] = act.T

    return pl.pallas_call(
        body, name="swiglu_fwd", grid=(ns,), in_specs=[col, col_up, cws, cws_up],
        out_specs=[col, pl.BlockSpec((STRIP, t), lambda j: (j, 0))],
        out_shape=[jax.ShapeDtypeStruct((t, w), BF16), jax.ShapeDtypeStruct((w, t), BF16)],
        compiler_params=_params(("parallel",)),
    )(u0, u0, cw, cw)


def _swiglu_bwd(u0, cw, da):
    t, w = u0.shape[0], u0.shape[1] // 2
    ns = w // STRIP
    col = pl.BlockSpec((t, STRIP), lambda j: (0, j))
    col_up = pl.BlockSpec((t, STRIP), lambda j: (0, ns + j))
    cws = pl.BlockSpec((SUBLANES, STRIP), lambda j: (0, j))
    cws_up = pl.BlockSpec((SUBLANES, STRIP), lambda j: (0, ns + j))

    def conv_bwd(du, u0, cw_ref, row, du0_ref, du0t_ref, dcw_ref):
        du1, du2 = _shift_up(du, 1, row), _shift_up(du, 2, row)
        du0 = ((cw_ref[2:3, :] * du + cw_ref[1:2, :] * du1) + cw_ref[0:1, :] * du2).astype(BF16)
        du0_ref[...] = du0
        du0t_ref[...] = du0.T
        dcw_ref[0:1, :] = jnp.sum(du2 * u0, axis=0, keepdims=True)
        dcw_ref[1:2, :] = jnp.sum(du1 * u0, axis=0, keepdims=True)
        dcw_ref[2:3, :] = jnp.sum(du * u0, axis=0, keepdims=True)
        dcw_ref[3:4, :] = jnp.sum(du, axis=0, keepdims=True)
        dcw_ref[4:8, :] = jnp.zeros((4, STRIP), F32)

    def body(g_ref, u_ref, cg_ref, cu_ref, da_ref, dg0_ref, du0_ref, dut_ref, dcg_ref, dcu_ref):
        row = lax.broadcasted_iota(jnp.int32, (t, STRIP), 0)
        g0, up0 = g_ref[...].astype(F32), u_ref[...].astype(F32)
        gate = _conv(g0, cg_ref, row)
        up = _conv(up0, cu_ref, row)
        sg = _sigmoid(gate)
        da = da_ref[...].astype(F32)
        dgate = da * up * (sg * (1.0 + gate * (1.0 - sg)))
        dup = da * (gate * sg)
        conv_bwd(dgate, g0, cg_ref, row, dg0_ref, dut_ref.at[0], dcg_ref)
        conv_bwd(dup, up0, cu_ref, row, du0_ref, dut_ref.at[1], dcu_ref)

    return pl.pallas_call(
        body, name="swiglu_bwd", grid=(ns,), in_specs=[col, col_up, cws, cws_up, col],
        out_specs=[col, col, pl.BlockSpec((2, STRIP, t), lambda j: (0, j, 0)), cws, cws],
        out_shape=[jax.ShapeDtypeStruct((t, w), BF16), jax.ShapeDtypeStruct((t, w), BF16),
                   jax.ShapeDtypeStruct((2, w, t), BF16),
                   jax.ShapeDtypeStruct((SUBLANES, w), F32), jax.ShapeDtypeStruct((SUBLANES, w), F32)],
        compiler_params=_params(("parallel",)),
    )(u0, u0, cw, cw, da)


def _ffn_win_grad(dut, x2b, tn=512):
    t, d = x2b.shape
    sp, sw = FF_SLAB_P, FF_SLAB

    def body(a_ref, b_ref, o_ref, ob_ref):
        res = _dot(a_ref[...], b_ref[...], NN)
        o_ref[...] = res[:sw, :]
        ob_ref[...] = res[:sw, :].astype(BF16)

    o_spec = pl.BlockSpec((None, sw, tn), lambda j, n: (j, 0, n))
    return pl.pallas_call(
        body, name="mm_g_ffn_in", grid=(8, d // tn),
        in_specs=[pl.BlockSpec((None, sp, t), lambda j, n: (j // 4, j % 4, 0)),
                  pl.BlockSpec((t, tn), lambda j, n: (0, n))],
        out_specs=[o_spec, o_spec],
        out_shape=[jax.ShapeDtypeStruct((8, sw, d), F32), jax.ShapeDtypeStruct((8, sw, d), BF16)],
        compiler_params=_params(("parallel", "parallel")),
    )(dut, x2b)


def _tile2d(r, c, limit=1 << 20):
    tr, tc = r, c
    while tr * tc * 4 > limit:
        if tr % (2 * SUBLANES) == 0:
            tr //= 2
        elif tc % (2 * LANES) == 0:
            tc //= 2
        else:
            break
    return tr, tc


def _adamw_math(w, m, v, g):
    c1 = 1.0 - ADAM_B1 ** ADAM_STEP
    c2 = 1.0 - ADAM_B2 ** ADAM_STEP
    mm = ADAM_B1 * m + (1.0 - ADAM_B1) * g
    vv = ADAM_B2 * v + (1.0 - ADAM_B2) * (g * g)
    delta = -ADAM_LR * ((mm / c1) / (jnp.sqrt(vv / c2) + ADAM_EPS) + ADAM_WD * w)
    return delta, mm, vv


def _adamw(w, m, v, g, name):
    r, c = w.shape
    blk = pl.BlockSpec((r, c), lambda i: (0, 0))

    def body(w_ref, m_ref, v_ref, gi_ref, g_ref, d_ref, nm_ref, nv_ref):
        g = gi_ref[...]
        d_ref[...], nm_ref[...], nv_ref[...] = _adamw_math(w_ref[...], m_ref[...], v_ref[...], g)
        g_ref[...] = g

    return pl.pallas_call(body, name=name, grid=(1,), in_specs=[blk] * 4, out_specs=[blk] * 4,
                          out_shape=[jax.ShapeDtypeStruct((r, c), F32)] * 4,
                          compiler_params=_params(("arbitrary",)))(w, m, v, g)


def _small_reduce(gathered):
    nd, r, n = gathered.shape
    tn = 2048 if n % 2048 == 0 else n
    def body(g_ref, s_ref, t_ref):
        s = g_ref[0]
        for dv in range(1, nd):
            s = s + g_ref[dv]
        s_ref[...] = s
        t_ref[...] = jnp.broadcast_to(jnp.sum(s, axis=0, keepdims=True), (r, tn))

    return pl.pallas_call(
        body, name="small_reduce", grid=(n // tn,),
        in_specs=[pl.BlockSpec((nd, r, tn), lambda j: (0, 0, j))],
        out_specs=[pl.BlockSpec((r, tn), lambda j: (0, j))] * 2,
        out_shape=[jax.ShapeDtypeStruct((r, n), F32)] * 2, compiler_params=_params(("parallel",)),
    )(gathered)


HBM = pl.BlockSpec(memory_space=pltpu.HBM)


def _all_gather(arrs, name):
    n = len(arrs)

    def body(*refs):
        ins, outs = refs[:n], refs[n:2 * n]
        send, recv, lsem = refs[2 * n:]
        x, y, c = lax.axis_index("x"), lax.axis_index("y"), lax.axis_index("c")
        me, sib = (x, y, c), (x, y, 1 - c)
        chips = [(1 - x, y), (x, 1 - y), (1 - x, 1 - y)]

        def slot(w, p):
            return outs[w].at[4 * p[0] + 2 * p[1] + p[2]]

        def cp(w, k, block, to, src=None):
            return pltpu.make_async_remote_copy(
                src_ref=slot(w, block) if src is None else src, dst_ref=slot(w, block),
                send_sem=send.at[w * 7 + k], recv_sem=recv.at[w * 7 + k], device_id=to, device_id_type=MESH)

        mine = [pltpu.make_async_copy(ins[w], slot(w, me), lsem.at[w]) for w in range(n)]
        for m in mine:
            m.start()
        first = []
        for w in range(n):
            first.append(cp(w, 0, me, sib, src=ins[w]))
            first += [cp(w, 1 + j, me, (*chip, c), src=ins[w]) for j, chip in enumerate(chips)]
        for f in first:
            f.start()
        passed = []
        for j, chip in enumerate(chips):
            for w in range(n):
                cp(w, 1 + j, (*chip, c), me).wait_recv()
                fwd = cp(w, 4 + j, (*chip, c), sib)
                fwd.start()
                passed.append(fwd)
        for w in range(n):
            cp(w, 0, sib, me).wait_recv()
            for j, chip in enumerate(chips):
                cp(w, 4 + j, (*chip, 1 - c), me).wait_recv()
        for f in first + passed:
            f.wait_send()
        for m in mine:
            m.wait()

    return pl.pallas_call(
        body, name=name, in_specs=[HBM] * n, out_specs=[HBM] * n,
        out_shape=[jax.ShapeDtypeStruct((8,) + a.shape, a.dtype) for a in arrs],
        scratch_shapes=[pltpu.SemaphoreType.DMA((7 * n,)), pltpu.SemaphoreType.DMA((7 * n,)),
                        pltpu.SemaphoreType.DMA((n,))],
    )(*arrs)


SEM = pl.BlockSpec(memory_space=pltpu.SEMAPHORE)
ANY = pl.BlockSpec(memory_space=pl.ANY)
EFFECT = pltpu.SideEffectType.DATAFLOW_SIDE_EFFECTING
N_PEERS = 7


def _peers(x, y, c):
    return [((1 - x) if k & 4 else x, (1 - y) if k & 2 else y, (1 - c) if k & 1 else c) for k in range(1, 8)]


def _spread_copies(src_refs, land_refs, send, recv, gather):
    x, y, c = lax.axis_index("x"), lax.axis_index("y"), lax.axis_index("c")
    me = 4 * x + 2 * y + c
    copies = []
    for w in range(len(src_refs)):
        for k, (px, py, pc) in enumerate(_peers(x, y, c)):
            p = 4 * px + 2 * py + pc
            copies.append((pltpu.make_async_remote_copy(
                src_ref=src_refs[w] if gather else src_refs[w].at[p], dst_ref=land_refs[w].at[me],
                send_sem=send[w].at[k], recv_sem=recv[w].at[k], device_id=(px, py, pc), device_id_type=MESH),
                pltpu.make_async_remote_copy(
                src_ref=src_refs[w] if gather else src_refs[w].at[p], dst_ref=land_refs[w].at[p],
                send_sem=send[w].at[k], recv_sem=recv[w].at[k], device_id=(px, py, pc), device_id_type=MESH)))
    return copies


def _hbm(a):
    return pltpu.with_memory_space_constraint(a, pltpu.HBM)


def _spread_start(srcs, lands, after, gather, name):
    n = len(srcs)

    def body(*refs):
        src_refs, land_refs = refs[:n], refs[n:2 * n]
        outs = refs[2 * n + 1:]
        send, recv, token = outs[:n], outs[n:2 * n], outs[4 * n]
        for start, _ in _spread_copies(src_refs, land_refs, send, recv, gather):
            start.start()
        token[...] = jnp.zeros_like(token)

    res = pl.pallas_call(
        body, name=name,
        out_shape=tuple([pltpu.SemaphoreType.DMA((N_PEERS,))] * (2 * n)
                        + [pltpu.HBM(a.shape, a.dtype) for a in srcs] + [pltpu.HBM(a.shape, a.dtype) for a in lands]
                        + [jax.ShapeDtypeStruct((SUBLANES, LANES), F32)]),
        in_specs=[HBM] * (2 * n) + [ANY],
        out_specs=tuple([SEM] * (2 * n) + [HBM] * (2 * n) + [pl.BlockSpec(memory_space=pltpu.VMEM)]),
        input_output_aliases={i: 2 * n + i for i in range(2 * n)},
        compiler_params=pltpu.CompilerParams(has_side_effects=EFFECT),
    )(*[_hbm(a) for a in srcs], *[_hbm(a) for a in lands], after)
    return res[:n], res[n:2 * n], res[2 * n:3 * n], res[3 * n:4 * n], res[4 * n]


def _spread_wait(send, recv, srcs, lands, after, gather, name):
    n = len(srcs)
    after = list(after) if isinstance(after, (list, tuple)) else [after]

    def body(*refs):
        src_refs, land_refs = refs[:n], refs[n:2 * n]
        send_refs, recv_refs = refs[2 * n:3 * n], refs[3 * n:4 * n]
        for _, arrive in _spread_copies(src_refs, land_refs, send_refs, recv_refs, gather):
            arrive.wait_send()
            arrive.wait_recv()

    res = pl.pallas_call(
        body, name=name,
        out_shape=tuple([pltpu.HBM(a.shape, a.dtype) for a in srcs] + [pltpu.HBM(a.shape, a.dtype) for a in lands]),
        in_specs=[HBM] * (2 * n) + [SEM] * (2 * n) + [ANY] * len(after),
        out_specs=tuple([HBM] * (2 * n)),
        input_output_aliases={i: i for i in range(2 * n)},
        compiler_params=pltpu.CompilerParams(has_side_effects=EFFECT),
    )(*srcs, *lands, *send, *recv, *after)
    return res[n:]


def _landing(shape, dtype, own, me):
    return lax.dynamic_update_index_in_dim(lax.empty((8,) + shape, dtype), own, me, 0)


N_GLR = GLA_W + GLA_RANK
FF_SLAB = D_FF // 4
FF_SLAB_P = FFP // 4


TRANSPOSED = ("w_in", "ffn_w_in")


def _prepare_sub1(gath):
    w_in_t = gath["w_in"].reshape(-1, gath["w_in"].shape[2])
    w2 = jnp.concatenate([gath["gla_gate_w2"][s] for s in range(8)], axis=1)
    return {"w_a_t": jnp.pad(w_in_t[:N_GLR], ((0, HA_W - N_GLR), (0, 0))), "w_b_t": w_in_t[N_GLR:],
            "w2p": jnp.pad(w2, ((0, LANES - GLA_RANK), (0, 0)))}


def _prepare_ffn_in(g):
    f = jnp.pad(g, ((0, 0), (0, FF_SLAB_P - FF_SLAB), (0, 0)))
    return f.reshape(2 * FFP, f.shape[2])


def _prepare_ffn_out(g):
    return jnp.pad(g.reshape(4, FF_SLAB, -1), ((0, 0), (0, FF_SLAB_P - FF_SLAB), (0, 0))).reshape(FFP, -1)


def _prepare_conv(g, conv_b):
    padc = FF_SLAB_P - FF_SLAB
    cw = jnp.pad(g, ((0, 0), (0, 0), (0, padc)))
    cb = jnp.pad(conv_b.reshape(8, 1, FF_SLAB), ((0, 0), (0, 0), (0, padc)))
    rows = jnp.concatenate([cw, cb, jnp.zeros((8, 4, FF_SLAB_P), F32)], axis=1)
    return jnp.concatenate([rows[s] for s in range(8)], axis=1)


def _prepare_ffn(gath, conv_b):
    return {"w_ffn_t": _prepare_ffn_in(gath["ffn_w_in"]), "wo": _prepare_ffn_out(gath["ffn_w_out"]),
            "cw": _prepare_conv(gath["ffn_conv_w"], conv_b)}


def _unpad_ff(a):
    r = a.shape[0]
    return a.reshape(r, 4, FF_SLAB_P)[:, :, :FF_SLAB].reshape(r, D_FF)


def _grad_slabs(g):
    w_in_t = jnp.concatenate([g["w_a_t"][:N_GLR], g["w_b_t"]], axis=0)
    s = {"w_in": w_in_t.reshape(4, 2, w_in_t.shape[0] // 8, w_in_t.shape[1])}
    for n in ("w_out", "ca_wq", "ca_wo"):
        s[n] = _to_slabs(n, g[n])
    for n in ("ca_wkv", "ffn_w_in"):
        s[n] = g[n].reshape((4, 2) + g[n].shape[1:])
    wo = g["wo"].reshape(4, FF_SLAB_P, -1)[:, :FF_SLAB]
    s["ffn_w_out"] = wo.reshape(4, 2, FF_SLAB // 2, wo.shape[-1])
    return s


class _AtHand:
    def __init__(self, p):
        self.p = p
        self.token = None

    def sub2(self, after):
        return self.p

    def ffn_in(self, after):
        return self.p["w_ffn_t"]

    def ffn_out(self, after):
        return self.p["wo"]

    def grads_out(self, group, slabs):
        pass

    def small_out(self, parts):
        pass


def _local_step(x, mem, positions, target, p, small, stages=None):
    t, d = x.shape
    stages = _AtHand(p) if stages is None else stages
    w_a_t, w_b_t, w2p, cw = p["w_a_t"], p["w_b_t"], p["w2p"], p["cw"]
    tabs = _rope_tables(positions)
    xb = x.astype(BF16) if stages.token is None else (x + stages.token[0, 0]).astype(BF16)
    memb = mem.astype(BF16)

    h_a = _matmul(xb, w_a_t, "nt", F32, 1024, 640, d, "mm_h_a")
    h_b = _matmul(xb, w_b_t, "nt", F32, 1024, 1024, d, "mm_h_b")
    o_g, o_raw, s_before = _gla_fwd(h_a, w2p, small["gla_gate_b"], small["gla_norm_g"])
    qr, kr = _rope_fwd(h_b, tabs)
    o_d_b, o_d, lse_tot = _dil_fwd_all(qr, kr, h_b)
    mixin = jnp.concatenate([o_g, o_d_b], axis=1)
    wts = stages.sub2(mixin)
    mix = _matmul(mixin, wts["w_out"], "nn", F32, 1024, 1024, d, "mm_mix")
    x1, x1b, x1t = _ln_fwd(x, mix, small["ln1_g"], small["ln1_b"], "ln1_fwd", True)

    q_ca = _matmul(x1b, wts["ca_wq"], "nn", BF16, 1024, 1024, d, "mm_caq")
    kvw = wts["ca_wkv"].shape[2]
    memkv = _matmul(memb, wts["ca_wkv"], "nn", BF16, mem.shape[0], kvw, d, "mm_memkv", b_slabs=True)
    o_c, o_ct = _ca_fwd(q_ca, memkv)
    ca_out = _matmul(o_c, wts["ca_wo"], "nn", F32, 1024, 1024, d, "mm_cao")
    x2, x2b = _ln_fwd(x1, ca_out, small["ln2_g"], small["ln2_b"], "ln2_fwd", False)

    w_ffn_t = stages.ffn_in(x2b)
    u0 = _matmul(x2b, w_ffn_t, "nt", BF16, 1024, 1024, d, "mm_u0")
    act, act_t = _swiglu_fwd(u0, cw)
    wo = stages.ffn_out(act)
    ffn = _matmul(act, wo, "nn", F32, 1024, 512, FFP, "mm_ffn")

    dp3, dp3b, dg3, db3, loss_part = _ln_bwd(x2, ffn, small["ln3_g"], small["ln3_b"], target, True, "ln3_bwd")
    g_wo, g_wo16 = _matmul(act_t, dp3b, "nn", F32, FFP // 8, 1024, t, "mm_g_wo", also_bf16=True)
    dact = _matmul(dp3b, wo, "nt", BF16, 1024, 512, d, "mm_dact")
    dug, duu, du_t, dcwg, dcwu = _swiglu_bwd(u0, cw, dact)
    g_ffn_in, g_ffn_in16 = _ffn_win_grad(du_t, x2b)

    def wo_slabs(a):
        a = a.reshape(4, FF_SLAB_P, -1)[:, :FF_SLAB]
        return a.reshape(8, FF_SLAB // 2, a.shape[-1])

    def wo_own(me):
        half = FF_SLAB // 2
        return lax.dynamic_slice_in_dim(g_wo, FF_SLAB_P * (me // 2) + half * (me % 2), half, axis=0)

    sent = stages.grads_out("ffn", {"ffn_w_out": (wo_own, wo_slabs(g_wo16)), "ffn_w_in": (g_ffn_in, g_ffn_in16)})
    dx2 = _matmul(dug, w_ffn_t, "nn", F32, 1024, 512, FFP, "mm_dx2_g", resid=dp3, resid_scale=ALPHA, dep=sent)
    dx2 = _matmul(duu, w_ffn_t, "nn", F32, 1024, 512, FFP, "mm_dx2_u", resid=dx2, b_k_off=1)

    dp2, dp2b, dg2, db2 = _ln_bwd(x1, ca_out, small["ln2_g"], small["ln2_b"], dx2, False, "ln2_bwd")
    g_cao, g_cao16 = _matmul(o_ct, dp2b, "nn", F32, 1024, 1024, t, "mm_g_cao", also_bf16=True)
    do_c = _matmul(dp2b, wts["ca_wo"], "nt", BF16, 1024, 1024, d, "mm_do_c")
    dq_ca, dmemkv = _ca_bwd(q_ca, memkv, do_c)
    g_caq, g_caq16 = _matmul(x1t, dq_ca, "nn", F32, 1024, 1024, t, "mm_g_caq", also_bf16=True)
    g_cakv, g_cakv16 = _matmul(memb, dmemkv.astype(BF16), "tn", F32, 512, kvw, mem.shape[0], "mm_g_cakv",
                               out_slabs=True, also_bf16=True)
    dx1 = _matmul(dq_ca, wts["ca_wq"], "nt", F32, 1024, 1024, d, "mm_dx1", resid=dp2, resid_scale=ALPHA)

    dp1, dp1b, dg1, db1 = _ln_bwd(x, mix, small["ln1_g"], small["ln1_b"], dx1, False, "ln1_bwd")
    g_wout, g_wout16 = _matmul(mixin, dp1b, "tn", F32, 1024, 1024, t, "mm_g_wout", also_bf16=True)

    def row_slabs(a):
        return a.reshape(8, a.shape[0] // 8, a.shape[1])

    sent = stages.grads_out("attn", {"ca_wo": (row_slabs(g_cao), row_slabs(g_cao16)),
                                     "ca_wq": (row_slabs(g_caq), row_slabs(g_caq16)), "ca_wkv": (g_cakv, g_cakv16),
                                     "w_out": (row_slabs(g_wout), row_slabs(g_wout16))})
    dmix = _matmul(dp1b, wts["w_out"], "nt", F32, 1024, 1024, d, "mm_dmix", dep=sent)
    dh_a, dw2, dgate_b, dnorm_g = _gla_bwd(h_a, w2p, small["gla_gate_b"], small["gla_norm_g"], o_raw, s_before, dmix)
    small_parts = {
        "gla_gate_b": dgate_b, "gla_norm_g": dnorm_g, "ln1_g": dg1, "ln1_b": db1, "ln2_g": dg2, "ln2_b": db2,
        "ln3_g": dg3, "ln3_b": db3,
        "conv": jnp.concatenate([_unpad_ff(dcwg), _unpad_ff(dcwu)], axis=1),
        "gla_gate_w2": dw2[:GLA_RANK],
    }
    sent = stages.small_out(small_parts)
    dq_d, dk_d, dv_d = _dil_bwd_all(qr, kr, h_b, dmix, o_d, lse_tot)
    dh_b = _dil_dh(dq_d, dk_d, dv_d, tabs)
    g_wa_t, g_wa16 = _matmul(dh_a, xb, "tn", F32, 640, 1024, t, "mm_g_wa", also_bf16=True, dep=sent)
    g_wb_t, g_wb16 = _matmul(dh_b, xb, "tn", F32, 1024, 1024, t, "mm_g_wb", also_bf16=True)

    def w_in_slabs(a, b):
        full = jnp.concatenate([a[:N_GLR], b], axis=0)
        return full.reshape(8, full.shape[0] // 8, full.shape[1])

    def w_in_own(me):
        rows = (N_GLR + g_wb_t.shape[0]) // 8
        full = jnp.concatenate([g_wa_t[:N_GLR], g_wb_t], axis=0)
        return lax.dynamic_slice_in_dim(full, me * rows, rows, axis=0)

    sent = stages.grads_out("w_in", {"w_in": (w_in_own, w_in_slabs(g_wa16, g_wb16))})
    dx = _matmul(dh_a, w_a_t, "nn", F32, 1024, 1024, HA_W, "mm_dx_a", resid=dp1, resid_scale=ALPHA, dep=sent)
    dx = _matmul(dh_b, w_b_t, "nn", F32, 1024, 1024, HB_W, "mm_dx_b", resid=dx)

    grads = {"w_a_t": g_wa_t, "w_b_t": g_wb_t, "w_out": g_wout, "ca_wq": g_caq, "ca_wkv": g_cakv, "ca_wo": g_cao,
             "ffn_w_in": g_ffn_in, "wo": g_wo}
    return loss_part, dx, grads, small_parts


BIG = ("w_in", "w_out", "ca_wq", "ca_wkv", "ca_wo", "ffn_w_in", "ffn_w_out")
COL_SHARDED = ("w_in", "ca_wkv", "ffn_w_in")
SMALL_ORDER = ("gla_gate_b", "gla_norm_g", "ln1_g", "ln1_b", "ln2_g", "ln2_b", "ln3_g", "ln3_b")


def _gathered_full(name, g):
    if name in COL_SHARDED:
        return g.transpose(1, 0, 2).reshape(g.shape[1], 8 * g.shape[2])
    return g.reshape(8 * g.shape[1], g.shape[2])


def _to_slabs(name, full):
    if name in COL_SHARDED:
        r, cc = full.shape
        s = full.reshape(r, 8, cc // 8).transpose(1, 0, 2)
    else:
        rr, c = full.shape
        s = full.reshape(8, rr // 8, c)
    return s.reshape((4, 2) + s.shape[1:])


def kernel(x, mem, positions, w_in, gla_gate_w2, gla_gate_b, gla_norm_g, w_out, ln1_g, ln1_b, ca_wq, ca_wkv, ca_wo, ln2_g, ln2_b, ffn_w_in, ffn_conv_w, ffn_conv_b, ffn_w_out, ln3_g, ln3_b, loss_target, m_w_in, m_gla_gate_w2, m_gla_gate_b, m_gla_norm_g, m_w_out, m_ln1_g, m_ln1_b, m_ca_wq, m_ca_wkv, m_ca_wo, m_ln2_g, m_ln2_b, m_ffn_w_in, m_ffn_conv_w, m_ffn_conv_b, m_ffn_w_out, m_ln3_g, m_ln3_b, v_w_in, v_gla_gate_w2, v_gla_gate_b, v_gla_norm_g, v_w_out, v_ln1_g, v_ln1_b, v_ca_wq, v_ca_wkv, v_ca_wo, v_ln2_g, v_ln2_b, v_ffn_w_in, v_ffn_conv_w, v_ffn_conv_b, v_ffn_w_out, v_ln3_g, v_ln3_b):
    weights = dict(w_in=w_in, gla_gate_w2=gla_gate_w2, gla_gate_b=gla_gate_b, gla_norm_g=gla_norm_g, w_out=w_out,
                   ln1_g=ln1_g, ln1_b=ln1_b, ca_wq=ca_wq, ca_wkv=ca_wkv, ca_wo=ca_wo, ln2_g=ln2_g, ln2_b=ln2_b,
                   ffn_w_in=ffn_w_in, ffn_conv_w=ffn_conv_w, ffn_conv_b=ffn_conv_b, ffn_w_out=ffn_w_out,
                   ln3_g=ln3_g, ln3_b=ln3_b)
    moms = dict(w_in=(m_w_in, v_w_in), gla_gate_w2=(m_gla_gate_w2, v_gla_gate_w2), gla_gate_b=(m_gla_gate_b, v_gla_gate_b),
                gla_norm_g=(m_gla_norm_g, v_gla_norm_g), w_out=(m_w_out, v_w_out), ln1_g=(m_ln1_g, v_ln1_g),
                ln1_b=(m_ln1_b, v_ln1_b), ca_wq=(m_ca_wq, v_ca_wq), ca_wkv=(m_ca_wkv, v_ca_wkv), ca_wo=(m_ca_wo, v_ca_wo),
                ln2_g=(m_ln2_g, v_ln2_g), ln2_b=(m_ln2_b, v_ln2_b), ffn_w_in=(m_ffn_w_in, v_ffn_w_in),
                ffn_conv_w=(m_ffn_conv_w, v_ffn_conv_w), ffn_conv_b=(m_ffn_conv_b, v_ffn_conv_b),
                ffn_w_out=(m_ffn_w_out, v_ffn_w_out), ln3_g=(m_ln3_g, v_ln3_g), ln3_b=(m_ln3_b, v_ln3_b))
    order = list(weights)
    xi, yi, ci = lax.axis_index("x"), lax.axis_index("y"), lax.axis_index("c")
    me = 4 * xi + 2 * yi + ci

    def travel(n, a):
        return jnp.swapaxes(a, 1, 2) if n in TRANSPOSED else a

    shard = {n: travel(n, weights[n]).astype(BF16)[0] for n in BIG}
    first = _all_gather([shard["w_in"], gla_gate_w2.astype(BF16)[0], ffn_conv_w[0]], "ag_first")
    p = _prepare_sub1({"w_in": first[0], "gla_gate_w2": first[1]})
    p["cw"] = _prepare_conv(first[2], ffn_conv_b)
    later = ("w_out", "ca_wq", "ca_wkv", "ca_wo", "ffn_w_in", "ffn_w_out")
    srcs = [shard[n] for n in later]
    lands = [_landing(shard[n].shape, BF16, shard[n], me) for n in later]
    send, recv, srcs, lands, token = _spread_start(srcs, lands, first[0], True, "ag_rest_start")

    class stages:
        pass

    stages.token = token

    def arrived(lo, hi, after, name):
        return _spread_wait(send[lo:hi], recv[lo:hi], srcs[lo:hi], lands[lo:hi], after, True, name)

    def sub2(after):
        g = dict(zip(later[:4], arrived(0, 4, after, "ag_wait_attn")))
        w = {n: _gathered_full(n, g[n]) for n in ("w_out", "ca_wq", "ca_wo")}
        w["ca_wkv"] = g["ca_wkv"]
        return w

    stages.sub2 = sub2
    stages.ffn_in = lambda after: _prepare_ffn_in(arrived(4, 5, after, "ag_wait_ffn_in")[0])
    stages.ffn_out = lambda after: _prepare_ffn_out(arrived(5, 6, after, "ag_wait_ffn_out")[0])
    sent = {}

    def grads_out(group, slabs):
        names = list(slabs)
        srcs16 = [slabs[n][1] for n in names]
        zones = [_landing(s.shape[1:], BF16, jnp.zeros(s.shape[1:], BF16), me) for s in srcs16]
        snd, rcv, s_thru, l_thru, tok = _spread_start(srcs16, zones, srcs16[0], False, f"rs_{group}_start")
        own32 = [slabs[n][0](me) if callable(slabs[n][0]) else slabs[n][0] for n in names]
        sent[group] = (names, own32, (snd, rcv, s_thru, l_thru))
        return tok

    stages.grads_out = grads_out
    small_sent = []

    def small_out(parts):
        packed = jnp.concatenate([parts[n] for n in SMALL_ORDER] + [parts["conv"],
                                 parts["gla_gate_w2"].reshape(SUBLANES, -1)], axis=1)
        packed = jnp.pad(packed, ((0, 0), (0, (-packed.shape[1]) % 2048)))
        zone = _landing(packed.shape, F32, packed, me)
        snd, rcv, s_thru, l_thru, tok = _spread_start([packed], [zone], packed, True, "ag_small_start")
        small_sent.append((snd, rcv, s_thru, l_thru))
        return tok

    stages.small_out = small_out
    small = dict(gla_gate_b=gla_gate_b, gla_norm_g=gla_norm_g, ln1_g=ln1_g, ln1_b=ln1_b, ln2_g=ln2_g, ln2_b=ln2_b,
                 ln3_g=ln3_g, ln3_b=ln3_b)

    loss_part, dx, grads, small_parts = _local_step(x[0], mem[0], positions[0], loss_target[0], p, small, stages)
    loss = lax.psum(jnp.sum(loss_part), ("x", "y", "c"))

    out = {}
    (allp,) = _spread_wait(*small_sent[0], dx, True, "ag_small_wait")
    dev_sum, row_sum = _small_reduce(allp)

    me1 = me.reshape(1).astype(jnp.int32)

    def finish_group(group, after):
        names, own32, handles = sent[group]
        landed = _spread_wait(*handles, after, False, f"rs_{group}_wait")
        for n, own, land in zip(names, own32, landed):
            m_, v_ = moms[n]
            res4 = _adamw_direct(travel(n, weights[n]), travel(n, m_), travel(n, v_), own, land, me1, f"adamw_{n}")
            out[n] = [travel(n, a) for a in res4]

    finish_group("ffn", dx)
    finish_group("attn", dx)
    off = 0
    for n in SMALL_ORDER:
        width = weights[n].shape[1]
        g = row_sum[0:1, off:off + width]
        off += width
        m_, v_ = moms[n]
        out[n] = _adamw(weights[n], m_, v_, g, f"adamw_{n}")
    conv_g = dev_sum[:, off:off + 2 * D_FF]
    off += 2 * D_FF
    g_cb = conv_g[3:4]
    out["ffn_conv_b"] = _adamw(ffn_conv_b, m_ffn_conv_b, v_ffn_conv_b, g_cb, "adamw_ffn_conv_b")
    wsh = ffn_conv_w.shape[2]
    g_cw = lax.dynamic_slice_in_dim(conv_g[0:3], me * wsh, wsh, axis=1)
    out["ffn_conv_w"] = _adamw(ffn_conv_w[0], m_ffn_conv_w[0], v_ffn_conv_w[0], g_cw, "adamw_ffn_conv_w")
    w2_g = dev_sum[:, off:off + GLA_RANK * GLA_HEADS * GLA_DK // SUBLANES].reshape(GLA_RANK, GLA_HEADS * GLA_DK)
    wsh2 = gla_gate_w2.shape[2]
    g_w2 = lax.dynamic_slice_in_dim(w2_g, me * wsh2, wsh2, axis=1)
    out["gla_gate_w2"] = _adamw(gla_gate_w2[0], m_gla_gate_w2[0], v_gla_gate_w2[0], g_w2, "adamw_gla_gate_w2")
    finish_group("w_in", [o[1] for o in out.values()])

    def shaped(n, a):
        return a.reshape(weights[n].shape)

    res = [loss, dx[None]]
    for k in range(4):
        res += [shaped(n, out[n][k]) for n in order]
    return tuple(res)


def _adamw_direct(w, m, v, own, land, me, name):
    _, r, c = w.shape
    tr, tc = _tile2d(r, c)
    blk = pl.BlockSpec((None, tr, tc), lambda i, j, s: (0, i, j))
    if own.ndim == 2:
        mine = pl.BlockSpec((tr, tc), lambda i, j, s: (i, j))
    else:
        mine = pl.BlockSpec((None, tr, tc), lambda i, j, s: (s[0], i, j))
    slots = [pl.BlockSpec((None, tr, tc), lambda i, j, s, k=k: (k, i, j)) for k in range(8)]

    def body(s_ref, w_ref, m_ref, v_ref, p_ref, *rest):
        slot_refs, (g_ref, d_ref, nm_ref, nv_ref) = rest[:8], rest[8:]
        g = p_ref[...]
        for sr in slot_refs:
            g = g + sr[...].astype(F32)
        d_ref[...], nm_ref[...], nv_ref[...] = _adamw_math(w_ref[...], m_ref[...], v_ref[...], g)
        g_ref[...] = g

    gs = pltpu.PrefetchScalarGridSpec(num_scalar_prefetch=1, grid=(r // tr, c // tc),
                                      in_specs=[blk, blk, blk, mine] + slots, out_specs=[blk] * 4)
    return pl.pallas_call(body, name=name, grid_spec=gs, out_shape=[jax.ShapeDtypeStruct((1, r, c), F32)] * 4,
                          compiler_params=_params(("parallel", "parallel")))(me, w, m, v, own, *([land] * 8))
```

```python
import jax
import jax.numpy as jnp
from jax import lax
from jax.experimental import pallas as pl
from jax.experimental.pallas import tpu as pltpu

F32 = jnp.float32
BF16 = jnp.bfloat16
MESH = pl.DeviceIdType.MESH

D_MODEL = 2048
LN_EPS = 1e-5
GLA_HEADS = 4
GLA_DV = 256
GLA_DK = 128
GLA_RANK = 16
GLA_TAU = 16.0
GLA_CHUNK = 64
DIL_HD = 128
DIL_HEADS = 8
DIL_BAND = 128
DIL_DILATIONS = (1, 4, 16)
ROPE_THETA = 500000.0
ROPE_DIMS = 32
CA_HEADS = 4
CA_HD = 512
D_FF = 5504
ALPHA = 2.0 ** 0.25
ADAM_LR = 0.001
ADAM_B1 = 0.9
ADAM_B2 = 0.999
ADAM_EPS = 1e-08
ADAM_WD = 0.01
ADAM_STEP = 10

LANES = 128
SUBLANES = 8
VMEM_LIMIT = 56 * 1024 * 1024

GLA_W = 2 * GLA_HEADS * GLA_DK + 2 * GLA_HEADS * GLA_DV
HA_W = GLA_W + LANES
HB_W = 3 * DIL_HEADS * DIL_HD
FFP = 5632
NEG = -1e30


def _params(sem):
    return pltpu.CompilerParams(dimension_semantics=sem, vmem_limit_bytes=VMEM_LIMIT)


def _sigmoid(x):
    return 1.0 / (1.0 + jnp.exp(-x))


def _dot(a, b, dn, precision=None):
    return lax.dot_general(a, b, (dn, ((), ())), preferred_element_type=F32, precision=precision)


NN = ((1,), (0,))
NT = ((1,), (1,))
TN = ((0,), (0,))


def _bf(v):
    return v if v.dtype == BF16 else v.astype(BF16)


def _matmul(a, b, kind, out_dtype, tm, tn, tk, name, resid=None, resid_scale=1.0, b_k_off=0, b_slabs=False,
            out_slabs=False, also_bf16=False, dep=None):
    if b_slabs:
        assert kind != "nt" and b.shape[2] == tn
        k2, n = b.shape[1], b.shape[0] * tn
    elif kind == "nt":
        n, k2 = b.shape
    else:
        k2, n = b.shape
    (k, m) = a.shape if kind == "tn" else a.shape[::-1]
    assert k2 >= k and (k2 == k or not b_slabs) and m % tm == 0 and n % tn == 0 and k % tk == 0, \
        (name, a.shape, b.shape, tm, tn, tk)
    nk = k // tk
    dn = {"nn": NN, "nt": NT, "tn": TN}[kind]
    a_spec = pl.BlockSpec((tk, tm), lambda i, j, kk: (kk, i)) if kind == "tn" else pl.BlockSpec((tm, tk), lambda i, j, kk: (i, kk))
    if b_slabs:
        b_spec = pl.BlockSpec((None, tk, tn), lambda i, j, kk: (j, kk, 0))
    elif kind == "nt":
        b_spec = pl.BlockSpec((tn, tk), lambda i, j, kk: (j, kk + b_k_off))
    else:
        b_spec = pl.BlockSpec((tk, tn), lambda i, j, kk: (kk + b_k_off, j))
    if out_slabs:
        o_spec = pl.BlockSpec((None, tm, tn), lambda i, j, kk: (j, i, 0))
        o_shape = (n // tn, m, tn)
    else:
        o_spec = pl.BlockSpec((tm, tn), lambda i, j, kk: (i, j))
        o_shape = (m, n)
    has_resid = resid is not None

    n_in = 2 + int(has_resid) + int(dep is not None)

    def body(*refs):
        a_ref, b_ref = refs[:2]
        r_ref = refs[2] if has_resid else None
        o_ref = refs[n_in]
        ob_ref = refs[n_in + 1] if also_bf16 else None
        part = _dot(_bf(a_ref[...]), _bf(b_ref[...]), dn)

        def finish(acc):
            if has_resid:
                acc = acc + resid_scale * r_ref[...].astype(F32)
            o_ref[...] = acc.astype(out_dtype)
            if also_bf16:
                ob_ref[...] = acc.astype(BF16)

        if nk == 1:
            finish(part)
        else:
            acc_ref = refs[-1]
            kk = pl.program_id(2)

            @pl.when(kk == 0)
            def _():
                acc_ref[...] = part

            @pl.when(kk > 0)
            def _():
                acc_ref[...] += part

            @pl.when(kk == nk - 1)
            def _():
                finish(acc_ref[...])

    in_specs = [a_spec, b_spec] + ([o_spec] if has_resid else [])
    args = (a, b) + ((resid,) if has_resid else ())
    if dep is not None:
        in_specs.append(pl.BlockSpec((SUBLANES, LANES), lambda i, j, kk: (0, 0)))
        args += (dep,)
    o_struct = jax.ShapeDtypeStruct(o_shape, out_dtype)
    return pl.pallas_call(
        body, name=name, out_shape=[o_struct, jax.ShapeDtypeStruct(o_shape, BF16)] if also_bf16 else o_struct,
        grid=(m // tm, n // tn, nk), in_specs=in_specs, out_specs=[o_spec, o_spec] if also_bf16 else o_spec,
        scratch_shapes=[pltpu.VMEM((tm, tn), F32)] if nk > 1 else [],
        compiler_params=_params(("parallel", "parallel", "arbitrary")),
    )(*args)


def _ln_core(xres, f):
    p = ALPHA * xres + f
    mu = jnp.mean(p, axis=-1, keepdims=True)
    xc = p - mu
    var = jnp.mean(xc * xc, axis=-1, keepdims=True)
    rstd = lax.rsqrt(var + LN_EPS)
    return xc * rstd, rstd


def _rows8(v):
    r, c = v.shape
    return jnp.sum(v.reshape(r // SUBLANES, SUBLANES, c), axis=0)


def _ln_fwd(xres, f, g, b, name, transposed, tr=256):
    t, d = xres.shape
    row = pl.BlockSpec((tr, d), lambda i: (i, 0))
    vec = pl.BlockSpec((1, d), lambda i: (0, 0))

    def body(x_ref, f_ref, g_ref, b_ref, y_ref, yb_ref, *yt_ref):
        xhat, _ = _ln_core(x_ref[...], f_ref[...])
        y = xhat * g_ref[...] + b_ref[...]
        y_ref[...] = y
        yb = y.astype(BF16)
        yb_ref[...] = yb
        if transposed:
            yt_ref[0][...] = yb.T

    out_specs = [row, row] + ([pl.BlockSpec((d, tr), lambda i: (0, i))] if transposed else [])
    out_shape = [jax.ShapeDtypeStruct((t, d), F32), jax.ShapeDtypeStruct((t, d), BF16)] \
        + ([jax.ShapeDtypeStruct((d, t), BF16)] if transposed else [])
    return pl.pallas_call(
        body, name=name, grid=(t // tr,), in_specs=[row, row, vec, vec], out_specs=out_specs, out_shape=out_shape,
        compiler_params=_params(("parallel",)),
    )(xres, f, g, b)


def _ln_bwd(xres, f, g, b, dy_or_target, loss_head, name, tr=256):
    t, d = xres.shape
    row = pl.BlockSpec((tr, d), lambda i: (i, 0))
    vec = pl.BlockSpec((1, d), lambda i: (0, 0))
    acc = pl.BlockSpec((SUBLANES, d), lambda i: (0, 0))
    lacc = pl.BlockSpec((SUBLANES, LANES), lambda i: (0, 0))

    def body(x_ref, f_ref, g_ref, b_ref, t_ref, dp_ref, dpb_ref, dg_ref, db_ref, *rest):
        i = pl.program_id(0)
        xhat, rstd = _ln_core(x_ref[...], f_ref[...])
        if loss_head:
            err = xhat * g_ref[...] + b_ref[...] - t_ref[...]
            dy = err * (1.0 / d)
            sq = err * err
            lanes = sq[:, :LANES]
            for kk in range(1, d // LANES):
                lanes = lanes + sq[:, kk * LANES:(kk + 1) * LANES]
            lpart = _rows8(lanes) * (0.5 / d)
        else:
            dy = t_ref[...]
        dxh = dy * g_ref[...]
        m1 = jnp.mean(dxh, axis=-1, keepdims=True)
        m2 = jnp.mean(dxh * xhat, axis=-1, keepdims=True)
        dp = rstd * (dxh - m1 - xhat * m2)
        dp_ref[...] = dp
        dpb_ref[...] = dp.astype(BF16)
        dgp = _rows8(dy * xhat)
        dbp = _rows8(dy)

        @pl.when(i == 0)
        def _():
            dg_ref[...] = dgp
            db_ref[...] = dbp
            if loss_head:
                rest[0][...] = lpart

        @pl.when(i > 0)
        def _():
            dg_ref[...] += dgp
            db_ref[...] += dbp
            if loss_head:
                rest[0][...] += lpart

    out_shape = [jax.ShapeDtypeStruct((t, d), F32), jax.ShapeDtypeStruct((t, d), BF16),
                 jax.ShapeDtypeStruct((SUBLANES, d), F32), jax.ShapeDtypeStruct((SUBLANES, d), F32)]
    out_specs = [row, row, acc, acc]
    if loss_head:
        out_shape.append(jax.ShapeDtypeStruct((SUBLANES, LANES), F32))
        out_specs.append(lacc)
    return pl.pallas_call(
        body, name=name, grid=(t // tr,), in_specs=[row, row, vec, vec, row], out_specs=out_specs,
        out_shape=out_shape, compiler_params=_params(("arbitrary",)),
    )(xres, f, g, b, dy_or_target)


def _gla_gates(glr, w2, gb):
    z = _dot(_bf(glr), w2, NN) + gb
    lg = (jnp.minimum(z, 0.0) - jnp.log(1.0 + jnp.exp(-jnp.abs(z)))) * (1.0 / GLA_TAU)
    c = z.shape[0]
    ri = lax.broadcasted_iota(jnp.int32, (c, c), 0)
    ci = lax.broadcasted_iota(jnp.int32, (c, c), 1)
    tri = (ci <= ri).astype(F32)
    bcum = _dot(tri, lg, NN, precision=lax.Precision.HIGHEST)
    blast = jnp.sum(lg, axis=0, keepdims=True)
    return z, bcum, blast, tri


def _gla_specs(t):
    c = GLA_CHUNK
    return c, t // c


def _gla_fwd(h_a, w2p, gate_b, norm_g):
    t = h_a.shape[0]
    c, n = _gla_specs(t)
    hk, hv = GLA_HEADS * GLA_DK, GLA_HEADS * GLA_DV
    scale = GLA_DK ** -0.5

    def body(q_ref, k_ref, v_ref, r_ref, glr_ref, w2_ref, gb_ref, ng_ref, og_ref, oraw_ref, sb_ref, st_ref):
        i = pl.program_id(0)

        @pl.when(i == 0)
        def _():
            st_ref[...] = jnp.zeros_like(st_ref)

        _, bcum, blast, _ = _gla_gates(glr_ref[...], w2_ref[...], gb_ref[...])
        ri = lax.broadcasted_iota(jnp.int32, (c, c), 0)
        ci = lax.broadcasted_iota(jnp.int32, (c, c), 1)
        causal = ci <= ri
        for h in range(GLA_HEADS):
            ks = slice(h * GLA_DK, (h + 1) * GLA_DK)
            vs = slice(h * GLA_DV, (h + 1) * GLA_DV)
            b_h, bl_h = bcum[:, ks], blast[:, ks]
            q_h, k_h = q_ref[:, ks], k_ref[:, ks]
            v_h = _bf(v_ref[:, vs])
            qi = _bf(q_h * scale * jnp.exp(b_h))
            ki = _bf(k_h * jnp.exp(-b_h))
            ke = _bf(k_h * jnp.exp(bl_h - b_h))
            st = st_ref[h]
            sb_ref[0, h] = st
            a = jnp.where(causal, _dot(qi, ki, NT), 0.0)
            o = _dot(_bf(a), v_h, NN) + _dot(qi, _bf(st), NT)
            st_ref[h] = st * jnp.exp(bl_h) + _dot(v_h, ke, TN)
            oraw_ref[:, vs] = o
            mu = jnp.mean(o, axis=-1, keepdims=True)
            oc = o - mu
            var = jnp.mean(oc * oc, axis=-1, keepdims=True)
            xh = oc * lax.rsqrt(var + LN_EPS)
            r_h = r_ref[:, vs]
            og_ref[:, vs] = (xh * ng_ref[:, vs] * (r_h * _sigmoid(r_h))).astype(BF16)

    return pl.pallas_call(
        body, name="gla_fwd", grid=(n,),
        in_specs=[pl.BlockSpec((c, hk), lambda i: (i, 0)), pl.BlockSpec((c, hk), lambda i: (i, 1)),
                  pl.BlockSpec((c, hv), lambda i: (i, 1)), pl.BlockSpec((c, hv), lambda i: (i, 2)),
                  pl.BlockSpec((c, LANES), lambda i: (i, GLA_W // LANES)),
                  pl.BlockSpec((LANES, hk), lambda i: (0, 0)), pl.BlockSpec((1, hk), lambda i: (0, 0)),
                  pl.BlockSpec((1, hv), lambda i: (0, 0))],
        out_specs=[pl.BlockSpec((c, hv), lambda i: (i, 0)), pl.BlockSpec((c, hv), lambda i: (i, 0)),
                   pl.BlockSpec((1, GLA_HEADS, GLA_DV, GLA_DK), lambda i: (i, 0, 0, 0))],
        out_shape=[jax.ShapeDtypeStruct((t, hv), BF16), jax.ShapeDtypeStruct((t, hv), F32),
                   jax.ShapeDtypeStruct((n, GLA_HEADS, GLA_DV, GLA_DK), F32)],
        scratch_shapes=[pltpu.VMEM((GLA_HEADS, GLA_DV, GLA_DK), F32)],
        compiler_params=_params(("arbitrary",)),
    )(h_a, h_a, h_a, h_a, h_a, w2p, gate_b, norm_g)


def _gla_bwd(h_a, w2p, gate_b, norm_g, o_raw, s_before, dmix):
    t = h_a.shape[0]
    c, n = _gla_specs(t)
    hk, hv = GLA_HEADS * GLA_DK, GLA_HEADS * GLA_DV
    scale = GLA_DK ** -0.5
    rev = lambda i: n - 1 - i

    def body(q_ref, k_ref, v_ref, r_ref, glr_ref, w2_ref, gb_ref, ng_ref, oraw_ref, sb_ref, do_ref,
             dh_ref, dw2_ref, dgb_ref, dng_ref, dst_ref):
        i = pl.program_id(0)

        @pl.when(i == 0)
        def _():
            dst_ref[...] = jnp.zeros_like(dst_ref)

        glr = glr_ref[...]
        z, bcum, blast, tri = _gla_gates(glr, w2_ref[...], gb_ref[...])
        ri = lax.broadcasted_iota(jnp.int32, (c, c), 0)
        ci = lax.broadcasted_iota(jnp.int32, (c, c), 1)
        causal = ci <= ri
        dlg_parts = []
        dng_parts = []
        for h in range(GLA_HEADS):
            ks = slice(h * GLA_DK, (h + 1) * GLA_DK)
            vs = slice(h * GLA_DV, (h + 1) * GLA_DV)
            o = oraw_ref[:, vs]
            mu = jnp.mean(o, axis=-1, keepdims=True)
            oc = o - mu
            var = jnp.mean(oc * oc, axis=-1, keepdims=True)
            rstd = lax.rsqrt(var + LN_EPS)
            xh = oc * rstd
            r_h = r_ref[:, vs]
            sg = _sigmoid(r_h)
            silu = r_h * sg
            dout = do_ref[:, vs]
            ng = ng_ref[:, vs]
            dng_parts.append(_rows8(dout * xh * silu))
            dr = dout * xh * ng * (sg * (1.0 + r_h * (1.0 - sg)))
            dxh = dout * ng * silu
            m1 = jnp.mean(dxh, axis=-1, keepdims=True)
            m2 = jnp.mean(dxh * xh, axis=-1, keepdims=True)
            do_raw = _bf(rstd * (dxh - m1 - xh * m2))
            b_h, bl_h = bcum[:, ks], blast[:, ks]
            q_h, k_h = q_ref[:, ks], k_ref[:, ks]
            v_h = _bf(v_ref[:, vs])
            eb, enb, eend = jnp.exp(b_h), jnp.exp(-b_h), jnp.exp(bl_h - b_h)
            decay = jnp.exp(bl_h)
            qi_f, ki_f, ke_f = q_h * scale * eb, k_h * enb, k_h * eend
            qi, ki, ke = _bf(qi_f), _bf(ki_f), _bf(ke_f)
            st = sb_ref[0, h]
            dst = dst_ref[h]
            dst_b = _bf(dst)
            a = _bf(jnp.where(causal, _dot(qi, ki, NT), 0.0))
            da = _bf(jnp.where(causal, _dot(do_raw, v_h, NT), 0.0))
            dv = _dot(a, do_raw, TN) + _dot(ke, dst_b, NT)
            dqi = _dot(da, ki, NN) + _dot(do_raw, _bf(st), NN)
            dki = _dot(da, qi, TN)
            dke = _dot(v_h, dst_b, NN)
            dst_ref[h] = _dot(do_raw, qi, TN) + dst * decay
            dbl = decay * jnp.sum(st * dst, axis=0, keepdims=True) + jnp.sum(dke * ke_f, axis=0, keepdims=True)
            dbc = dqi * qi_f - dki * ki_f - dke * ke_f
            dlg_parts.append(_dot(tri, dbc, TN, precision=lax.Precision.HIGHEST) + dbl)
            dh_ref[:, ks] = (dqi * eb * scale).astype(BF16)
            dh_ref[:, hk + h * GLA_DK: hk + (h + 1) * GLA_DK] = (dki * enb + dke * eend).astype(BF16)
            dh_ref[:, 2 * hk + h * GLA_DV: 2 * hk + (h + 1) * GLA_DV] = dv.astype(BF16)
            dh_ref[:, 2 * hk + hv + h * GLA_DV: 2 * hk + hv + (h + 1) * GLA_DV] = dr.astype(BF16)
        dlg = jnp.concatenate(dlg_parts, axis=1)
        dz = dlg * (1.0 / GLA_TAU) * _sigmoid(-z)
        dz_b = _bf(dz)
        dh_ref[:, GLA_W:] = _dot(dz_b, w2_ref[...], NT).astype(BF16)
        dw2p = _dot(_bf(glr), dz_b, TN)
        dgbp = _rows8(dz)
        dngp = jnp.concatenate(dng_parts, axis=1)

        @pl.when(i == 0)
        def _():
            dw2_ref[...] = dw2p
            dgb_ref[...] = dgbp
            dng_ref[...] = dngp

        @pl.when(i > 0)
        def _():
            dw2_ref[...] += dw2p
            dgb_ref[...] += dgbp
            dng_ref[...] += dngp

    return pl.pallas_call(
        body, name="gla_bwd", grid=(n,),
        in_specs=[pl.BlockSpec((c, hk), lambda i: (rev(i), 0)), pl.BlockSpec((c, hk), lambda i: (rev(i), 1)),
                  pl.BlockSpec((c, hv), lambda i: (rev(i), 1)), pl.BlockSpec((c, hv), lambda i: (rev(i), 2)),
                  pl.BlockSpec((c, LANES), lambda i: (rev(i), GLA_W // LANES)),
                  pl.BlockSpec((LANES, hk), lambda i: (0, 0)), pl.BlockSpec((1, hk), lambda i: (0, 0)),
                  pl.BlockSpec((1, hv), lambda i: (0, 0)),
                  pl.BlockSpec((c, hv), lambda i: (rev(i), 0)),
                  pl.BlockSpec((1, GLA_HEADS, GLA_DV, GLA_DK), lambda i: (rev(i), 0, 0, 0)),
                  pl.BlockSpec((c, hv), lambda i: (rev(i), 0))],
        out_specs=[pl.BlockSpec((c, HA_W), lambda i: (rev(i), 0)),
                   pl.BlockSpec((LANES, hk), lambda i: (0, 0)),
                   pl.BlockSpec((SUBLANES, hk), lambda i: (0, 0)),
                   pl.BlockSpec((SUBLANES, hv), lambda i: (0, 0))],
        out_shape=[jax.ShapeDtypeStruct((t, HA_W), BF16), jax.ShapeDtypeStruct((LANES, hk), F32),
                   jax.ShapeDtypeStruct((SUBLANES, hk), F32), jax.ShapeDtypeStruct((SUBLANES, hv), F32)],
        scratch_shapes=[pltpu.VMEM((GLA_HEADS, GLA_DV, GLA_DK), F32)],
        compiler_params=_params(("arbitrary",)),
    )(h_a, h_a, h_a, h_a, h_a, w2p, gate_b, norm_g, o_raw, s_before, dmix)


def _rope_tables(positions):
    half = ROPE_DIMS // 2
    inv_freq = ROPE_THETA ** (-jnp.arange(0, ROPE_DIMS, 2, dtype=F32) / ROPE_DIMS)
    ang = positions.astype(F32).reshape(-1, 1) * inv_freq
    cos, sin = jnp.cos(ang), jnp.sin(ang)
    t = cos.shape[0]
    one = jnp.ones((t, DIL_HD - ROPE_DIMS), F32)
    zero = jnp.zeros((t, DIL_HD - ROPE_DIMS), F32)
    zh = jnp.zeros((t, half), F32)
    return (jnp.concatenate([cos, cos, one], axis=1), jnp.concatenate([-sin, zh, zero], axis=1),
            jnp.concatenate([zh, sin, zero], axis=1))


def _rope_apply(x, c, s1, s2):
    half = ROPE_DIMS // 2
    return x * c + pltpu.roll(x, DIL_HD - half, 1) * s1 + pltpu.roll(x, half, 1) * s2


def _rope_apply_t(dy, c, s1, s2):
    half = ROPE_DIMS // 2
    return dy * c + pltpu.roll(dy * s1, half, 1) + pltpu.roll(dy * s2, DIL_HD - half, 1)


def _rope_fwd(h_b, tabs, tr=256):
    t = h_b.shape[0]
    w = DIL_HEADS * DIL_HD
    scale = DIL_HD ** -0.5
    tab = pl.BlockSpec((tr, DIL_HD), lambda i: (i, 0))
    outb = pl.BlockSpec((tr, w), lambda i: (i, 0))

    def body(q_ref, k_ref, c_ref, s1_ref, s2_ref, qo_ref, ko_ref):
        c, s1, s2 = c_ref[...], s1_ref[...], s2_ref[...]
        for h in range(DIL_HEADS):
            hs = slice(h * DIL_HD, (h + 1) * DIL_HD)
            qo_ref[:, hs] = _rope_apply(q_ref[:, hs] * scale, c, s1, s2)
            ko_ref[:, hs] = _rope_apply(k_ref[:, hs], c, s1, s2)

    return pl.pallas_call(
        body, name="rope_fwd", grid=(t // tr,),
        in_specs=[pl.BlockSpec((tr, w), lambda i: (i, 0)), pl.BlockSpec((tr, w), lambda i: (i, 1)), tab, tab, tab],
        out_specs=[outb, outb],
        out_shape=[jax.ShapeDtypeStruct((t, w), F32)] * 2,
        compiler_params=_params(("parallel",)),
    )(h_b, h_b, *tabs)


def _dil_dh(dq, dk, dv, tabs, tr=256):
    t, w = dq.shape
    scale = DIL_HD ** -0.5
    tab = pl.BlockSpec((tr, DIL_HD), lambda i: (i, 0))
    inb = pl.BlockSpec((tr, w), lambda i: (i, 0))

    def body(dq_ref, dk_ref, dv_ref, c_ref, s1_ref, s2_ref, o_ref):
        c, s1, s2 = c_ref[...], s1_ref[...], s2_ref[...]
        for h in range(DIL_HEADS):
            hs = slice(h * DIL_HD, (h + 1) * DIL_HD)
            o_ref[:, h * DIL_HD:(h + 1) * DIL_HD] = (_rope_apply_t(dq_ref[:, hs], c, s1, s2) * scale).astype(BF16)
            o_ref[:, w + h * DIL_HD: w + (h + 1) * DIL_HD] = _rope_apply_t(dk_ref[:, hs], c, s1, s2).astype(BF16)
        o_ref[:, 2 * w:] = dv_ref[...].astype(BF16)

    return pl.pallas_call(
        body, name="dil_dh", grid=(t // tr,), in_specs=[inb] * 3 + [tab] * 3,
        out_specs=pl.BlockSpec((tr, 3 * w), lambda i: (i, 0)),
        out_shape=jax.ShapeDtypeStruct((t, 3 * w), BF16), compiler_params=_params(("parallel",)),
    )(dq, dk, dv, *tabs)


def _band_masks(not_first):
    r = lax.broadcasted_iota(jnp.int32, (DIL_BAND, 2 * DIL_BAND), 0)
    c = lax.broadcasted_iota(jnp.int32, (DIL_BAND, 2 * DIL_BAND), 1)
    nf = jnp.full((DIL_BAND, 2 * DIL_BAND), not_first, jnp.int32)
    look_back = jnp.logical_and(jnp.logical_and(c < DIL_BAND, c >= r), nf > 0)
    own_band = jnp.logical_and(c >= DIL_BAND, (c - DIL_BAND) <= r)
    return jnp.logical_or(look_back, own_band)


def _gather_rows(dst_ref, src_ref, t, d, cast=None):
    n = t // d
    for r in range(d):
        v = src_ref[pl.ds(r, n, stride=d), :] if d > 1 else src_ref[...]
        dst_ref[r * n:(r + 1) * n, :] = v if cast is None else v.astype(cast)


def _tri_mask():
    r = lax.broadcasted_iota(jnp.int32, (DIL_BAND, DIL_BAND), 0)
    c = lax.broadcasted_iota(jnp.int32, (DIL_BAND, DIL_BAND), 1)
    return c <= r


def _dil_fwd_all(qr, kr, h_b):
    t = qr.shape[0]
    nbands = t // DIL_BAND
    nbr = len(DIL_DILATIONS)
    hoff = DIL_HEADS

    def col(off):
        return pl.BlockSpec((t, DIL_HD), lambda h: (0, off + h))

    outb = pl.BlockSpec((t, DIL_HD), lambda h: (0, h))

    def body(q_ref, k_ref, v_ref, ob_ref, of_ref, lt_ref, qs, ks, vs, os_, ls_, *br):
        obr, lbr = br[:nbr], br[nbr:]
        for bi, d in enumerate(DIL_DILATIONS):
            n = t // d
            nb = n // DIL_BAND
            _gather_rows(qs, q_ref, t, d, BF16)
            _gather_rows(ks, k_ref, t, d, BF16)
            _gather_rows(vs, v_ref, t, d, BF16)
            s = jnp.where(_tri_mask(), _dot(qs[0:DIL_BAND, :], ks[0:DIL_BAND, :], NT), NEG)
            m = jnp.max(s, axis=-1, keepdims=True)
            pr = jnp.exp(s - m)
            den = jnp.sum(pr, axis=-1, keepdims=True)
            os_[0:DIL_BAND, :] = _dot(_bf(pr), vs[0:DIL_BAND, :], NN) / den
            ls_[0:DIL_BAND, :] = jnp.broadcast_to(m + jnp.log(den), (DIL_BAND, DIL_HD))

            def band(b, carry, nb=nb):
                st = pl.multiple_of((b - 1) * DIL_BAND, DIL_BAND)
                cur = pl.ds(st + DIL_BAND, DIL_BAND)
                both = pl.ds(st, 2 * DIL_BAND)
                not_first = ((b % nb) != 0).astype(jnp.int32)
                s = jnp.where(_band_masks(not_first), _dot(qs[cur, :], ks[both, :], NT), NEG)
                m = jnp.max(s, axis=-1, keepdims=True)
                pr = jnp.exp(s - m)
                den = jnp.sum(pr, axis=-1, keepdims=True)
                os_[cur, :] = _dot(_bf(pr), vs[both, :], NN) / den
                ls_[cur, :] = jnp.broadcast_to(m + jnp.log(den), (DIL_BAND, DIL_HD))
                return carry

            lax.fori_loop(1, nbands, band, 0, unroll=16)
            for r in range(d):
                dst = pl.ds(r, n, stride=d) if d > 1 else slice(None)
                obr[bi][dst, :] = os_[r * n:(r + 1) * n, :]
                lbr[bi][dst, :] = ls_[r * n:(r + 1) * n, :]
        rows = 512
        for c0 in range(0, t, rows):
            sl = slice(c0, c0 + rows)
            la, lb, lc = lbr[0][sl, :], lbr[1][sl, :], lbr[2][sl, :]
            m = jnp.maximum(jnp.maximum(la, lb), lc)
            ea, eb, ec = jnp.exp(la - m), jnp.exp(lb - m), jnp.exp(lc - m)
            den = ea + eb + ec
            o = (ea * obr[0][sl, :] + eb * obr[1][sl, :] + ec * obr[2][sl, :]) / den
            ob_ref[sl, :] = o.astype(BF16)
            of_ref[sl, :] = o
            lt_ref[sl, :] = m + jnp.log(den)

    w = DIL_HEADS * DIL_HD
    vm = lambda dt: pltpu.VMEM((t, DIL_HD), dt)
    return pl.pallas_call(
        body, name="dil_fwd", grid=(DIL_HEADS,), in_specs=[col(0), col(0), col(2 * hoff)],
        out_specs=[outb, outb, outb],
        out_shape=[jax.ShapeDtypeStruct((t, w), BF16), jax.ShapeDtypeStruct((t, w), F32),
                   jax.ShapeDtypeStruct((t, w), F32)],
        scratch_shapes=[vm(BF16)] * 3 + [vm(F32)] * 2 + [vm(F32)] * (2 * nbr),
        compiler_params=_params(("parallel",)),
    )(qr, kr, h_b)


def _dil_bwd_all(qr, kr, h_b, dmix, o_d, lse_tot):
    t = qr.shape[0]
    nbands = t // DIL_BAND
    hoff = DIL_HEADS

    def col(off, double=False):
        if double:
            return pl.BlockSpec((t, DIL_HD), lambda h: (0, off + h))
        return pl.BlockSpec((t, DIL_HD), lambda h: (0, off + h), pipeline_mode=pl.Buffered(1))

    outb = pl.BlockSpec((t, DIL_HD), lambda h: (0, h))

    def body(q_ref, k_ref, v_ref, do_ref, o_ref, l_ref, dq_ref, dk_ref, dv_ref,
             qs, ks, vs, dos, lss, dds, dqs, acck, accv, ddt):
        rows = 512
        for c0 in range(0, t, rows):
            prod = do_ref[c0:c0 + rows, :] * o_ref[c0:c0 + rows, :]
            ddt[c0:c0 + rows, :] = jnp.broadcast_to(jnp.sum(prod, axis=-1, keepdims=True), (rows, DIL_HD))
        for bi, d in enumerate(DIL_DILATIONS):
            n = t // d
            nb = n // DIL_BAND
            _gather_rows(qs, q_ref, t, d, BF16)
            _gather_rows(ks, k_ref, t, d, BF16)
            _gather_rows(vs, v_ref, t, d, BF16)
            _gather_rows(dos, do_ref, t, d, BF16)
            _gather_rows(lss, l_ref, t, d)
            _gather_rows(dds, ddt, t, d)
            b0 = slice(0, DIL_BAND)
            s = jnp.where(_tri_mask(), _dot(qs[b0, :], ks[b0, :], NT), NEG)
            pr = jnp.exp(s - lss[b0, :])
            ds = _bf(pr * (_dot(dos[b0, :], vs[b0, :], NT) - dds[b0, :]))
            dqs[b0, :] = _dot(ds, ks[b0, :], NN)
            acck[DIL_BAND:2 * DIL_BAND, :] = _dot(ds, qs[b0, :], TN)
            accv[DIL_BAND:2 * DIL_BAND, :] = _dot(_bf(pr), dos[b0, :], TN)

            def band(b, carry, nb=nb):
                st = pl.multiple_of((b - 1) * DIL_BAND, DIL_BAND)
                cur = pl.ds(st + DIL_BAND, DIL_BAND)
                both = pl.ds(st, 2 * DIL_BAND)
                back_rows = pl.ds(st + DIL_BAND, DIL_BAND)
                own_rows = pl.ds(st + 2 * DIL_BAND, DIL_BAND)
                not_first = ((b % nb) != 0).astype(jnp.int32)
                qb, dob, lb, ddb = qs[cur, :], dos[cur, :], lss[cur, :], dds[cur, :]
                kcat, vcat = ks[both, :], vs[both, :]
                s = jnp.where(_band_masks(not_first), _dot(qb, kcat, NT), NEG)
                pr = jnp.exp(s - jnp.concatenate([lb, lb], axis=1))
                ds = _bf(pr * (_dot(dob, vcat, NT) - jnp.concatenate([ddb, ddb], axis=1)))
                dqs[cur, :] = _dot(ds, kcat, NN)
                dkk = _dot(ds, qb, TN)
                dvv = _dot(_bf(pr), dob, TN)
                acck[back_rows, :] += dkk[:DIL_BAND]
                accv[back_rows, :] += dvv[:DIL_BAND]
                acck[own_rows, :] = dkk[DIL_BAND:]
                accv[own_rows, :] = dvv[DIL_BAND:]
                return carry

            lax.fori_loop(1, nbands, band, 0, unroll=8)
            for r in range(d):
                lo = r * n
                if d == 1:
                    dq_ref[...] = dqs[...]
                    dk_ref[...] = acck[DIL_BAND:DIL_BAND + t, :]
                    dv_ref[...] = accv[DIL_BAND:DIL_BAND + t, :]
                else:
                    dst = pl.ds(r, n, stride=d)
                    dq_ref[dst, :] = dq_ref[dst, :] + dqs[lo:lo + n, :]
                    dk_ref[dst, :] = dk_ref[dst, :] + acck[DIL_BAND + lo:DIL_BAND + lo + n, :]
                    dv_ref[dst, :] = dv_ref[dst, :] + accv[DIL_BAND + lo:DIL_BAND + lo + n, :]

    w = DIL_HEADS * DIL_HD
    vm = lambda dt, extra=0: pltpu.VMEM((t + extra, DIL_HD), dt)
    return pl.pallas_call(
        body, name="dil_bwd", grid=(DIL_HEADS,),
        in_specs=[col(0, True), col(0, True), col(2 * hoff, True), col(hoff), col(0), col(0)], out_specs=[outb] * 3,
        out_shape=[jax.ShapeDtypeStruct((t, w), F32)] * 3,
        scratch_shapes=[vm(BF16)] * 4 + [vm(F32)] * 3 + [vm(F32, DIL_BAND)] * 2 + [vm(F32)],
        compiler_params=_params(("parallel",)),
    )(qr, kr, h_b, dmix, o_d, lse_tot)


def _ca_fwd(q, memkv, tq=512):
    t, d = q.shape
    m = memkv.shape[0]
    scale = CA_HD ** -0.5

    def body(q_ref, k_ref, v_ref, o_ref, ot_ref):
        for h in range(CA_HEADS):
            hs = slice(h * CA_HD, (h + 1) * CA_HD)
            s = _dot(q_ref[:, hs], k_ref[:, hs], NT) * scale
            p = jnp.exp(s - jnp.max(s, axis=-1, keepdims=True))
            p = p / jnp.sum(p, axis=-1, keepdims=True)
            o = _dot(_bf(p), v_ref[:, hs], NN).astype(BF16)
            o_ref[:, hs] = o
            ot_ref[hs, :] = o.T

    return pl.pallas_call(
        body, name="ca_fwd", grid=(t // tq,),
        in_specs=[pl.BlockSpec((tq, d), lambda i: (i, 0)), pl.BlockSpec((m, d), lambda i: (0, 0)),
                  pl.BlockSpec((m, d), lambda i: (0, 1))],
        out_specs=[pl.BlockSpec((tq, d), lambda i: (i, 0)), pl.BlockSpec((d, tq), lambda i: (0, i))],
        out_shape=[jax.ShapeDtypeStruct((t, d), BF16), jax.ShapeDtypeStruct((d, t), BF16)],
        compiler_params=_params(("parallel",)),
    )(q, memkv, memkv)


def _ca_bwd(q, memkv, do, tq=512):
    t, d = q.shape
    m = memkv.shape[0]
    scale = CA_HD ** -0.5

    def body(q_ref, k_ref, v_ref, do_ref, dq_ref, dkv_ref):
        i = pl.program_id(0)

        @pl.when(i == 0)
        def _():
            dkv_ref[...] = jnp.zeros_like(dkv_ref)

        for h in range(CA_HEADS):
            hs = slice(h * CA_HD, (h + 1) * CA_HD)
            q_h, k_h, v_h, do_h = q_ref[:, hs], k_ref[:, hs], v_ref[:, hs], do_ref[:, hs]
            s = _dot(q_h, k_h, NT) * scale
            p = jnp.exp(s - jnp.max(s, axis=-1, keepdims=True))
            p = p / jnp.sum(p, axis=-1, keepdims=True)
            dp = _dot(do_h, v_h, NT)
            ds = _bf(p * (dp - jnp.sum(p * dp, axis=-1, keepdims=True)) * scale)
            dq_ref[:, hs] = _dot(ds, k_h, NN).astype(BF16)
            dkv_ref[:, hs] += _dot(ds, q_h, TN)
            dkv_ref[:, d + h * CA_HD: d + (h + 1) * CA_HD] += _dot(_bf(p), do_h, TN)

    return pl.pallas_call(
        body, name="ca_bwd", grid=(t // tq,),
        in_specs=[pl.BlockSpec((tq, d), lambda i: (i, 0)), pl.BlockSpec((m, d), lambda i: (0, 0)),
                  pl.BlockSpec((m, d), lambda i: (0, 1)), pl.BlockSpec((tq, d), lambda i: (i, 0))],
        out_specs=[pl.BlockSpec((tq, d), lambda i: (i, 0)), pl.BlockSpec((m, 2 * d), lambda i: (0, 0))],
        out_shape=[jax.ShapeDtypeStruct((t, d), BF16), jax.ShapeDtypeStruct((m, 2 * d), F32)],
        compiler_params=_params(("arbitrary",)),
    )(q, memkv, memkv, do)


STRIP = 256


def _shift_down(u, n, row):
    return jnp.where(row >= n, pltpu.roll(u, n, 0), 0.0)


def _shift_up(u, n, row):
    t = u.shape[0]
    return jnp.where(row < t - n, pltpu.roll(u, t - n, 0), 0.0)


def _conv(u, cw_ref, row):
    return ((cw_ref[3:4, :] + cw_ref[0:1, :] * _shift_down(u, 2, row)) + cw_ref[1:2, :] * _shift_down(u, 1, row)) \
        + cw_ref[2:3, :] * u


def _swiglu_fwd(u0, cw):
    t, w = u0.shape[0], u0.shape[1] // 2
    ns = w // STRIP
    col = pl.BlockSpec((t, STRIP), lambda j: (0, j))
    col_up = pl.BlockSpec((t, STRIP), lambda j: (0, ns + j))
    cws = pl.BlockSpec((SUBLANES, STRIP), lambda j: (0, j))
    cws_up = pl.BlockSpec((SUBLANES, STRIP), lambda j: (0, ns + j))

    def body(g_ref, u_ref, cg_ref, cu_ref, a_ref, at_ref):
        row = lax.broadcasted_iota(jnp.int32, (t, STRIP), 0)
        gate = _conv(g_ref[...].astype(F32), cg_ref, row)
        up = _conv(u_ref[...].astype(F32), cu_ref, row)
        act = (gate * _sigmoid(gate) * up).astype(BF16)
        a_ref[...] = act
        at_ref[...] = act.T

    return pl.pallas_call(
        body, name="swiglu_fwd", grid=(ns,), in_specs=[col, col_up, cws, cws_up],
        out_specs=[col, pl.BlockSpec((STRIP, t), lambda j: (j, 0))],
        out_shape=[jax.ShapeDtypeStruct((t, w), BF16), jax.ShapeDtypeStruct((w, t), BF16)],
        compiler_params=_params(("parallel",)),
    )(u0, u0, cw, cw)


def _swiglu_bwd(u0, cw, da):
    t, w = u0.shape[0], u0.shape[1] // 2
    ns = w // STRIP
    col = pl.BlockSpec((t, STRIP), lambda j: (0, j))
    col_up = pl.BlockSpec((t, STRIP), lambda j: (0, ns + j))
    cws = pl.BlockSpec((SUBLANES, STRIP), lambda j: (0, j))
    cws_up = pl.BlockSpec((SUBLANES, STRIP), lambda j: (0, ns + j))

    def conv_bwd(du, u0, cw_ref, row, du0_ref, du0t_ref, dcw_ref):
        du1, du2 = _shift_up(du, 1, row), _shift_up(du, 2, row)
        du0 = ((cw_ref[2:3, :] * du + cw_ref[1:2, :] * du1) + cw_ref[0:1, :] * du2).astype(BF16)
        du0_ref[...] = du0
        du0t_ref[...] = du0.T
        dcw_ref[0:1, :] = jnp.sum(du2 * u0, axis=0, keepdims=True)
        dcw_ref[1:2, :] = jnp.sum(du1 * u0, axis=0, keepdims=True)
        dcw_ref[2:3, :] = jnp.sum(du * u0, axis=0, keepdims=True)
        dcw_ref[3:4, :] = jnp.sum(du, axis=0, keepdims=True)
        dcw_ref[4:8, :] = jnp.zeros((4, STRIP), F32)

    def body(g_ref, u_ref, cg_ref, cu_ref, da_ref, dg0_ref, du0_ref, dut_ref, dcg_ref, dcu_ref):
        row = lax.broadcasted_iota(jnp.int32, (t, STRIP), 0)
        g0, up0 = g_ref[...].astype(F32), u_ref[...].astype(F32)
        gate = _conv(g0, cg_ref, row)
        up = _conv(up0, cu_ref, row)
        sg = _sigmoid(gate)
        da = da_ref[...].astype(F32)
        dgate = da * up * (sg * (1.0 + gate * (1.0 - sg)))
        dup = da * (gate * sg)
        conv_bwd(dgate, g0, cg_ref, row, dg0_ref, dut_ref.at[0], dcg_ref)
        conv_bwd(dup, up0, cu_ref, row, du0_ref, dut_ref.at[1], dcu_ref)

    return pl.pallas_call(
        body, name="swiglu_bwd", grid=(ns,), in_specs=[col, col_up, cws, cws_up, col],
        out_specs=[col, col, pl.BlockSpec((2, STRIP, t), lambda j: (0, j, 0)), cws, cws],
        out_shape=[jax.ShapeDtypeStruct((t, w), BF16), jax.ShapeDtypeStruct((t, w), BF16),
                   jax.ShapeDtypeStruct((2, w, t), BF16),
                   jax.ShapeDtypeStruct((SUBLANES, w), F32), jax.ShapeDtypeStruct((SUBLANES, w), F32)],
        compiler_params=_params(("parallel",)),
    )(u0, u0, cw, cw, da)


def _ffn_win_grad(dut, x2b, tn=512):
    t, d = x2b.shape
    sp, sw = FF_SLAB_P, FF_SLAB

    def body(a_ref, b_ref, o_ref, ob_ref):
        res = _dot(a_ref[...], b_ref[...], NN)
        o_ref[...] = res[:sw, :]
        ob_ref[...] = res[:sw, :].astype(BF16)

    o_spec = pl.BlockSpec((None, sw, tn), lambda j, n: (j, 0, n))
    return pl.pallas_call(
        body, name="mm_g_ffn_in", grid=(8, d // tn),
        in_specs=[pl.BlockSpec((None, sp, t), lambda j, n: (j // 4, j % 4, 0)),
                  pl.BlockSpec((t, tn), lambda j, n: (0, n))],
        out_specs=[o_spec, o_spec],
        out_shape=[jax.ShapeDtypeStruct((8, sw, d), F32), jax.ShapeDtypeStruct((8, sw, d), BF16)],
        compiler_params=_params(("parallel", "parallel")),
    )(dut, x2b)


def _tile2d(r, c, limit=1 << 20):
    tr, tc = r, c
    while tr * tc * 4 > limit:
        if tr % (2 * SUBLANES) == 0:
            tr //= 2
        elif tc % (2 * LANES) == 0:
            tc //= 2
        else:
            break
    return tr, tc


def _adamw_math(w, m, v, g):
    c1 = 1.0 - ADAM_B1 ** ADAM_STEP
    c2 = 1.0 - ADAM_B2 ** ADAM_STEP
    mm = ADAM_B1 * m + (1.0 - ADAM_B1) * g
    vv = ADAM_B2 * v + (1.0 - ADAM_B2) * (g * g)
    delta = -ADAM_LR * ((mm / c1) / (jnp.sqrt(vv / c2) + ADAM_EPS) + ADAM_WD * w)
    return delta, mm, vv


def _adamw(w, m, v, g, name):
    r, c = w.shape
    blk = pl.BlockSpec((r, c), lambda i: (0, 0))

    def body(w_ref, m_ref, v_ref, gi_ref, g_ref, d_ref, nm_ref, nv_ref):
        g = gi_ref[...]
        d_ref[...], nm_ref[...], nv_ref[...] = _adamw_math(w_ref[...], m_ref[...], v_ref[...], g)
        g_ref[...] = g

    return pl.pallas_call(body, name=name, grid=(1,), in_specs=[blk] * 4, out_specs=[blk] * 4,
                          out_shape=[jax.ShapeDtypeStruct((r, c), F32)] * 4,
                          compiler_params=_params(("arbitrary",)))(w, m, v, g)


def _small_reduce(gathered):
    nd, r, n = gathered.shape
    tn = 2048 if n % 2048 == 0 else n
    def body(g_ref, s_ref, t_ref):
        s = g_ref[0]
        for dv in range(1, nd):
            s = s + g_ref[dv]
        s_ref[...] = s
        t_ref[...] = jnp.broadcast_to(jnp.sum(s, axis=0, keepdims=True), (r, tn))

    return pl.pallas_call(
        body, name="small_reduce", grid=(n // tn,),
        in_specs=[pl.BlockSpec((nd, r, tn), lambda j: (0, 0, j))],
        out_specs=[pl.BlockSpec((r, tn), lambda j: (0, j))] * 2,
        out_shape=[jax.ShapeDtypeStruct((r, n), F32)] * 2, compiler_params=_params(("parallel",)),
    )(gathered)


HBM = pl.BlockSpec(memory_space=pltpu.HBM)


def _all_gather(arrs, name):
    n = len(arrs)

    def body(*refs):
        ins, outs = refs[:n], refs[n:2 * n]
        send, recv, lsem = refs[2 * n:]
        x, y, c = lax.axis_index("x"), lax.axis_index("y"), lax.axis_index("c")
        me, sib = (x, y, c), (x, y, 1 - c)
        chips = [(1 - x, y), (x, 1 - y), (1 - x, 1 - y)]

        def slot(w, p):
            return outs[w].at[4 * p[0] + 2 * p[1] + p[2]]

        def cp(w, k, block, to, src=None):
            return pltpu.make_async_remote_copy(
                src_ref=slot(w, block) if src is None else src, dst_ref=slot(w, block),
                send_sem=send.at[w * 7 + k], recv_sem=recv.at[w * 7 + k], device_id=to, device_id_type=MESH)

        mine = [pltpu.make_async_copy(ins[w], slot(w, me), lsem.at[w]) for w in range(n)]
        for m in mine:
            m.start()
        first = []
        for w in range(n):
            first.append(cp(w, 0, me, sib, src=ins[w]))
            first += [cp(w, 1 + j, me, (*chip, c), src=ins[w]) for j, chip in enumerate(chips)]
        for f in first:
            f.start()
        passed = []
        for j, chip in enumerate(chips):
            for w in range(n):
                cp(w, 1 + j, (*chip, c), me).wait_recv()
                fwd = cp(w, 4 + j, (*chip, c), sib)
                fwd.start()
                passed.append(fwd)
        for w in range(n):
            cp(w, 0, sib, me).wait_recv()
            for j, chip in enumerate(chips):
                cp(w, 4 + j, (*chip, 1 - c), me).wait_recv()
        for f in first + passed:
            f.wait_send()
        for m in mine:
            m.wait()

    return pl.pallas_call(
        body, name=name, in_specs=[HBM] * n, out_specs=[HBM] * n,
        out_shape=[jax.ShapeDtypeStruct((8,) + a.shape, a.dtype) for a in arrs],
        scratch_shapes=[pltpu.SemaphoreType.DMA((7 * n,)), pltpu.SemaphoreType.DMA((7 * n,)),
                        pltpu.SemaphoreType.DMA((n,))],
    )(*arrs)


SEM = pl.BlockSpec(memory_space=pltpu.SEMAPHORE)
ANY = pl.BlockSpec(memory_space=pl.ANY)
EFFECT = pltpu.SideEffectType.DATAFLOW_SIDE_EFFECTING
N_PEERS = 7


def _peers(x, y, c):
    return [((1 - x) if k & 4 else x, (1 - y) if k & 2 else y, (1 - c) if k & 1 else c) for k in range(1, 8)]


def _spread_copies(src_refs, land_refs, send, recv, gather):
    x, y, c = lax.axis_index("x"), lax.axis_index("y"), lax.axis_index("c")
    me = 4 * x + 2 * y + c
    copies = []
    for w in range(len(src_refs)):
        for k, (px, py, pc) in enumerate(_peers(x, y, c)):
            p = 4 * px + 2 * py + pc
            copies.append((pltpu.make_async_remote_copy(
                src_ref=src_refs[w] if gather else src_refs[w].at[p], dst_ref=land_refs[w].at[me],
                send_sem=send[w].at[k], recv_sem=recv[w].at[k], device_id=(px, py, pc), device_id_type=MESH),
                pltpu.make_async_remote_copy(
                src_ref=src_refs[w] if gather else src_refs[w].at[p], dst_ref=land_refs[w].at[p],
                send_sem=send[w].at[k], recv_sem=recv[w].at[k], device_id=(px, py, pc), device_id_type=MESH)))
    return copies


def _hbm(a):
    return pltpu.with_memory_space_constraint(a, pltpu.HBM)


def _spread_start(srcs, lands, after, gather, name):
    n = len(srcs)

    def body(*refs):
        src_refs, land_refs = refs[:n], refs[n:2 * n]
        outs = refs[2 * n + 1:]
        send, recv, token = outs[:n], outs[n:2 * n], outs[4 * n]
        for start, _ in _spread_copies(src_refs, land_refs, send, recv, gather):
            start.start()
        token[...] = jnp.zeros_like(token)

    res = pl.pallas_call(
        body, name=name,
        out_shape=tuple([pltpu.SemaphoreType.DMA((N_PEERS,))] * (2 * n)
                        + [pltpu.HBM(a.shape, a.dtype) for a in srcs] + [pltpu.HBM(a.shape, a.dtype) for a in lands]
                        + [jax.ShapeDtypeStruct((SUBLANES, LANES), F32)]),
        in_specs=[HBM] * (2 * n) + [ANY],
        out_specs=tuple([SEM] * (2 * n) + [HBM] * (2 * n) + [pl.BlockSpec(memory_space=pltpu.VMEM)]),
        input_output_aliases={i: 2 * n + i for i in range(2 * n)},
        compiler_params=pltpu.CompilerParams(has_side_effects=EFFECT),
    )(*[_hbm(a) for a in srcs], *[_hbm(a) for a in lands], after)
    return res[:n], res[n:2 * n], res[2 * n:3 * n], res[3 * n:4 * n], res[4 * n]


def _spread_wait(send, recv, srcs, lands, after, gather, name):
    n = len(srcs)
    after = list(after) if isinstance(after, (list, tuple)) else [after]

    def body(*refs):
        src_refs, land_refs = refs[:n], refs[n:2 * n]
        send_refs, recv_refs = refs[2 * n:3 * n], refs[3 * n:4 * n]
        for _, arrive in _spread_copies(src_refs, land_refs, send_refs, recv_refs, gather):
            arrive.wait_send()
            arrive.wait_recv()

    res = pl.pallas_call(
        body, name=name,
        out_shape=tuple([pltpu.HBM(a.shape, a.dtype) for a in srcs] + [pltpu.HBM(a.shape, a.dtype) for a in lands]),
        in_specs=[HBM] * (2 * n) + [SEM] * (2 * n) + [ANY] * len(after),
        out_specs=tuple([HBM] * (2 * n)),
        input_output_aliases={i: i for i in range(2 * n)},
        compiler_params=pltpu.CompilerParams(has_side_effects=EFFECT),
    )(*srcs, *lands, *send, *recv, *after)
    return res[n:]


def _landing(shape, dtype, own, me):
    return lax.dynamic_update_index_in_dim(lax.empty((8,) + shape, dtype), own, me, 0)


N_GLR = GLA_W + GLA_RANK
FF_SLAB = D_FF // 4
FF_SLAB_P = FFP // 4


TRANSPOSED = ("w_in", "ffn_w_in")


def _prepare_sub1(gath):
    w_in_t = gath["w_in"].reshape(-1, gath["w_in"].shape[2])
    w2 = jnp.concatenate([gath["gla_gate_w2"][s] for s in range(8)], axis=1)
    return {"w_a_t": jnp.pad(w_in_t[:N_GLR], ((0, HA_W - N_GLR), (0, 0))), "w_b_t": w_in_t[N_GLR:],
            "w2p": jnp.pad(w2, ((0, LANES - GLA_RANK), (0, 0)))}


def _prepare_ffn_in(g):
    f = jnp.pad(g, ((0, 0), (0, FF_SLAB_P - FF_SLAB), (0, 0)))
    return f.reshape(2 * FFP, f.shape[2])


def _prepare_ffn_out(g):
    return jnp.pad(g.reshape(4, FF_SLAB, -1), ((0, 0), (0, FF_SLAB_P - FF_SLAB), (0, 0))).reshape(FFP, -1)


def _prepare_conv(g, conv_b):
    padc = FF_SLAB_P - FF_SLAB
    cw = jnp.pad(g, ((0, 0), (0, 0), (0, padc)))
    cb = jnp.pad(conv_b.reshape(8, 1, FF_SLAB), ((0, 0), (0, 0), (0, padc)))
    rows = jnp.concatenate([cw, cb, jnp.zeros((8, 4, FF_SLAB_P), F32)], axis=1)
    return jnp.concatenate([rows[s] for s in range(8)], axis=1)


def _prepare_ffn(gath, conv_b):
    return {"w_ffn_t": _prepare_ffn_in(gath["ffn_w_in"]), "wo": _prepare_ffn_out(gath["ffn_w_out"]),
            "cw": _prepare_conv(gath["ffn_conv_w"], conv_b)}


def _unpad_ff(a):
    r = a.shape[0]
    return a.reshape(r, 4, FF_SLAB_P)[:, :, :FF_SLAB].reshape(r, D_FF)


def _grad_slabs(g):
    w_in_t = jnp.concatenate([g["w_a_t"][:N_GLR], g["w_b_t"]], axis=0)
    s = {"w_in": w_in_t.reshape(4, 2, w_in_t.shape[0] // 8, w_in_t.shape[1])}
    for n in ("w_out", "ca_wq", "ca_wo"):
        s[n] = _to_slabs(n, g[n])
    for n in ("ca_wkv", "ffn_w_in"):
        s[n] = g[n].reshape((4, 2) + g[n].shape[1:])
    wo = g["wo"].reshape(4, FF_SLAB_P, -1)[:, :FF_SLAB]
    s["ffn_w_out"] = wo.reshape(4, 2, FF_SLAB // 2, wo.shape[-1])
    return s


class _AtHand:
    def __init__(self, p):
        self.p = p
        self.token = None

    def sub2(self, after):
        return self.p

    def ffn_in(self, after):
        return self.p["w_ffn_t"]

    def ffn_out(self, after):
        return self.p["wo"]

    def grads_out(self, group, slabs):
        pass

    def small_out(self, parts):
        pass


def _local_step(x, mem, positions, target, p, small, stages=None):
    t, d = x.shape
    stages = _AtHand(p) if stages is None else stages
    w_a_t, w_b_t, w2p, cw = p["w_a_t"], p["w_b_t"], p["w2p"], p["cw"]
    tabs = _rope_tables(positions)
    xb = x.astype(BF16) if stages.token is None else (x + stages.token[0, 0]).astype(BF16)
    memb = mem.astype(BF16)

    h_a = _matmul(xb, w_a_t, "nt", F32, 1024, 640, d, "mm_h_a")
    h_b = _matmul(xb, w_b_t, "nt", F32, 1024, 1024, d, "mm_h_b")
    o_g, o_raw, s_before = _gla_fwd(h_a, w2p, small["gla_gate_b"], small["gla_norm_g"])
    qr, kr = _rope_fwd(h_b, tabs)
    o_d_b, o_d, lse_tot = _dil_fwd_all(qr, kr, h_b)
    mixin = jnp.concatenate([o_g, o_d_b], axis=1)
    wts = stages.sub2(mixin)
    mix = _matmul(mixin, wts["w_out"], "nn", F32, 1024, 1024, d, "mm_mix")
    x1, x1b, x1t = _ln_fwd(x, mix, small["ln1_g"], small["ln1_b"], "ln1_fwd", True)

    q_ca = _matmul(x1b, wts["ca_wq"], "nn", BF16, 1024, 1024, d, "mm_caq")
    kvw = wts["ca_wkv"].shape[2]
    memkv = _matmul(memb, wts["ca_wkv"], "nn", BF16, mem.shape[0], kvw, d, "mm_memkv", b_slabs=True)
    o_c, o_ct = _ca_fwd(q_ca, memkv)
    ca_out = _matmul(o_c, wts["ca_wo"], "nn", F32, 1024, 1024, d, "mm_cao")
    x2, x2b = _ln_fwd(x1, ca_out, small["ln2_g"], small["ln2_b"], "ln2_fwd", False)

    w_ffn_t = stages.ffn_in(x2b)
    u0 = _matmul(x2b, w_ffn_t, "nt", BF16, 1024, 1024, d, "mm_u0")
    act, act_t = _swiglu_fwd(u0, cw)
    wo = stages.ffn_out(act)
    ffn = _matmul(act, wo, "nn", F32, 1024, 512, FFP, "mm_ffn")

    dp3, dp3b, dg3, db3, loss_part = _ln_bwd(x2, ffn, small["ln3_g"], small["ln3_b"], target, True, "ln3_bwd")
    g_wo, g_wo16 = _matmul(act_t, dp3b, "nn", F32, FFP // 8, 1024, t, "mm_g_wo", also_bf16=True)
    dact = _matmul(dp3b, wo, "nt", BF16, 1024, 512, d, "mm_dact")
    dug, duu, du_t, dcwg, dcwu = _swiglu_bwd(u0, cw, dact)
    g_ffn_in, g_ffn_in16 = _ffn_win_grad(du_t, x2b)

    def wo_slabs(a):
        a = a.reshape(4, FF_SLAB_P, -1)[:, :FF_SLAB]
        return a.reshape(8, FF_SLAB // 2, a.shape[-1])

    def wo_own(me):
        half = FF_SLAB // 2
        return lax.dynamic_slice_in_dim(g_wo, FF_SLAB_P * (me // 2) + half * (me % 2), half, axis=0)

    sent = stages.grads_out("ffn", {"ffn_w_out": (wo_own, wo_slabs(g_wo16)), "ffn_w_in": (g_ffn_in, g_ffn_in16)})
    dx2 = _matmul(dug, w_ffn_t, "nn", F32, 1024, 512, FFP, "mm_dx2_g", resid=dp3, resid_scale=ALPHA, dep=sent)
    dx2 = _matmul(duu, w_ffn_t, "nn", F32, 1024, 512, FFP, "mm_dx2_u", resid=dx2, b_k_off=1)

    dp2, dp2b, dg2, db2 = _ln_bwd(x1, ca_out, small["ln2_g"], small["ln2_b"], dx2, False, "ln2_bwd")
    g_cao, g_cao16 = _matmul(o_ct, dp2b, "nn", F32, 1024, 1024, t, "mm_g_cao", also_bf16=True)
    do_c = _matmul(dp2b, wts["ca_wo"], "nt", BF16, 1024, 1024, d, "mm_do_c")
    dq_ca, dmemkv = _ca_bwd(q_ca, memkv, do_c)
    g_caq, g_caq16 = _matmul(x1t, dq_ca, "nn", F32, 1024, 1024, t, "mm_g_caq", also_bf16=True)
    g_cakv, g_cakv16 = _matmul(memb, dmemkv.astype(BF16), "tn", F32, 512, kvw, mem.shape[0], "mm_g_cakv",
                               out_slabs=True, also_bf16=True)
    dx1 = _matmul(dq_ca, wts["ca_wq"], "nt", F32, 1024, 1024, d, "mm_dx1", resid=dp2, resid_scale=ALPHA)

    dp1, dp1b, dg1, db1 = _ln_bwd(x, mix, small["ln1_g"], small["ln1_b"], dx1, False, "ln1_bwd")
    g_wout, g_wout16 = _matmul(mixin, dp1b, "tn", F32, 1024, 1024, t, "mm_g_wout", also_bf16=True)

    def row_slabs(a):
        return a.reshape(8, a.shape[0] // 8, a.shape[1])

    sent = stages.grads_out("attn", {"ca_wo": (row_slabs(g_cao), row_slabs(g_cao16)),
                                     "ca_wq": (row_slabs(g_caq), row_slabs(g_caq16)), "ca_wkv": (g_cakv, g_cakv16),
                                     "w_out": (row_slabs(g_wout), row_slabs(g_wout16))})
    dmix = _matmul(dp1b, wts["w_out"], "nt", F32, 1024, 1024, d, "mm_dmix", dep=sent)
    dh_a, dw2, dgate_b, dnorm_g = _gla_bwd(h_a, w2p, small["gla_gate_b"], small["gla_norm_g"], o_raw, s_before, dmix)
    small_parts = {
        "gla_gate_b": dgate_b, "gla_norm_g": dnorm_g, "ln1_g": dg1, "ln1_b": db1, "ln2_g": dg2, "ln2_b": db2,
        "ln3_g": dg3, "ln3_b": db3,
        "conv": jnp.concatenate([_unpad_ff(dcwg), _unpad_ff(dcwu)], axis=1),
        "gla_gate_w2": dw2[:GLA_RANK],
    }
    sent = stages.small_out(small_parts)
    dq_d, dk_d, dv_d = _dil_bwd_all(qr, kr, h_b, dmix, o_d, lse_tot)
    dh_b = _dil_dh(dq_d, dk_d, dv_d, tabs)
    g_wa_t, g_wa16 = _matmul(dh_a, xb, "tn", F32, 640, 1024, t, "mm_g_wa", also_bf16=True, dep=sent)
    g_wb_t, g_wb16 = _matmul(dh_b, xb, "tn", F32, 1024, 1024, t, "mm_g_wb", also_bf16=True)

    def w_in_slabs(a, b):
        full = jnp.concatenate([a[:N_GLR], b], axis=0)
        return full.reshape(8, full.shape[0] // 8, full.shape[1])

    def w_in_own(me):
        rows = (N_GLR + g_wb_t.shape[0]) // 8
        full = jnp.concatenate([g_wa_t[:N_GLR], g_wb_t], axis=0)
        return lax.dynamic_slice_in_dim(full, me * rows, rows, axis=0)

    sent = stages.grads_out("w_in", {"w_in": (w_in_own, w_in_slabs(g_wa16, g_wb16))})
    dx = _matmul(dh_a, w_a_t, "nn", F32, 1024, 1024, HA_W, "mm_dx_a", resid=dp1, resid_scale=ALPHA, dep=sent)
    dx = _matmul(dh_b, w_b_t, "nn", F32, 1024, 1024, HB_W, "mm_dx_b", resid=dx)

    grads = {"w_a_t": g_wa_t, "w_b_t": g_wb_t, "w_out": g_wout, "ca_wq": g_caq, "ca_wkv": g_cakv, "ca_wo": g_cao,
             "ffn_w_in": g_ffn_in, "wo": g_wo}
    return loss_part, dx, grads, small_parts


BIG = ("w_in", "w_out", "ca_wq", "ca_wkv", "ca_wo", "ffn_w_in", "ffn_w_out")
COL_SHARDED = ("w_in", "ca_wkv", "ffn_w_in")
SMALL_ORDER = ("gla_gate_b", "gla_norm_g", "ln1_g", "ln1_b", "ln2_g", "ln2_b", "ln3_g", "ln3_b")


def _gathered_full(name, g):
    if name in COL_SHARDED:
        return g.transpose(1, 0, 2).reshape(g.shape[1], 8 * g.shape[2])
    return g.reshape(8 * g.shape[1], g.shape[2])


def _to_slabs(name, full):
    if name in COL_SHARDED:
        r, cc = full.shape
        s = full.reshape(r, 8, cc // 8).transpose(1, 0, 2)
    else:
        rr, c = full.shape
        s = full.reshape(8, rr // 8, c)
    return s.reshape((4, 2) + s.shape[1:])


def kernel(x, mem, positions, w_in, gla_gate_w2, gla_gate_b, gla_norm_g, w_out, ln1_g, ln1_b, ca_wq, ca_wkv, ca_wo, ln2_g, ln2_b, ffn_w_in, ffn_conv_w, ffn_conv_b, ffn_w_out, ln3_g, ln3_b, loss_target, m_w_in, m_gla_gate_w2, m_gla_gate_b, m_gla_norm_g, m_w_out, m_ln1_g, m_ln1_b, m_ca_wq, m_ca_wkv, m_ca_wo, m_ln2_g, m_ln2_b, m_ffn_w_in, m_ffn_conv_w, m_ffn_conv_b, m_ffn_w_out, m_ln3_g, m_ln3_b, v_w_in, v_gla_gate_w2, v_gla_gate_b, v_gla_norm_g, v_w_out, v_ln1_g, v_ln1_b, v_ca_wq, v_ca_wkv, v_ca_wo, v_ln2_g, v_ln2_b, v_ffn_w_in, v_ffn_conv_w, v_ffn_conv_b, v_ffn_w_out, v_ln3_g, v_ln3_b):
    weights = dict(w_in=w_in, gla_gate_w2=gla_gate_w2, gla_gate_b=gla_gate_b, gla_norm_g=gla_norm_g, w_out=w_out,
                   ln1_g=ln1_g, ln1_b=ln1_b, ca_wq=ca_wq, ca_wkv=ca_wkv, ca_wo=ca_wo, ln2_g=ln2_g, ln2_b=ln2_b,
                   ffn_w_in=ffn_w_in, ffn_conv_w=ffn_conv_w, ffn_conv_b=ffn_conv_b, ffn_w_out=ffn_w_out,
                   ln3_g=ln3_g, ln3_b=ln3_b)
    moms = dict(w_in=(m_w_in, v_w_in), gla_gate_w2=(m_gla_gate_w2, v_gla_gate_w2), gla_gate_b=(m_gla_gate_b, v_gla_gate_b),
                gla_norm_g=(m_gla_norm_g, v_gla_norm_g), w_out=(m_w_out, v_w_out), ln1_g=(m_ln1_g, v_ln1_g),
                ln1_b=(m_ln1_b, v_ln1_b), ca_wq=(m_ca_wq, v_ca_wq), ca_wkv=(m_ca_wkv, v_ca_wkv), ca_wo=(m_ca_wo, v_ca_wo),
                ln2_g=(m_ln2_g, v_ln2_g), ln2_b=(m_ln2_b, v_ln2_b), ffn_w_in=(m_ffn_w_in, v_ffn_w_in),
                ffn_conv_w=(m_ffn_conv_w, v_ffn_conv_w), ffn_conv_b=(m_ffn_conv_b, v_ffn_conv_b),
                ffn_w_out=(m_ffn_w_out, v_ffn_w_out), ln3_g=(m_ln3_g, v_ln3_g), ln3_b=(m_ln3_b, v_ln3_b))
    order = list(weights)
    xi, yi, ci = lax.axis_index("x"), lax.axis_index("y"), lax.axis_index("c")
    me = 4 * xi + 2 * yi + ci

    def travel(n, a):
        return jnp.swapaxes(a, 1, 2) if n in TRANSPOSED else a

    shard = {n: travel(n, weights[n]).astype(BF16)[0] for n in BIG}
    first = _all_gather([shard["w_in"], gla_gate_w2.astype(BF16)[0], ffn_conv_w[0]], "ag_first")
    p = _prepare_sub1({"w_in": first[0], "gla_gate_w2": first[1]})
    p["cw"] = _prepare_conv(first[2], ffn_conv_b)
    later = ("w_out", "ca_wq", "ca_wkv", "ca_wo", "ffn_w_in", "ffn_w_out")
    srcs = [shard[n] for n in later]
    lands = [_landing(shard[n].shape, BF16, shard[n], me) for n in later]
    send, recv, srcs, lands, token = _spread_start(srcs, lands, first[0], True, "ag_rest_start")

    class stages:
        pass

    stages.token = token

    def arrived(lo, hi, after, name):
        return _spread_wait(send[lo:hi], recv[lo:hi], srcs[lo:hi], lands[lo:hi], after, True, name)

    def sub2(after):
        g = dict(zip(later[:4], arrived(0, 4, after, "ag_wait_attn")))
        w = {n: _gathered_full(n, g[n]) for n in ("w_out", "ca_wq", "ca_wo")}
        w["ca_wkv"] = g["ca_wkv"]
        return w

    stages.sub2 = sub2
    stages.ffn_in = lambda after: _prepare_ffn_in(arrived(4, 5, after, "ag_wait_ffn_in")[0])
    stages.ffn_out = lambda after: _prepare_ffn_out(arrived(5, 6, after, "ag_wait_ffn_out")[0])
    sent = {}

    def grads_out(group, slabs):
        names = list(slabs)
        srcs16 = [slabs[n][1] for n in names]
        zones = [_landing(s.shape[1:], BF16, jnp.zeros(s.shape[1:], BF16), me) for s in srcs16]
        snd, rcv, s_thru, l_thru, tok = _spread_start(srcs16, zones, srcs16[0], False, f"rs_{group}_start")
        own32 = [slabs[n][0](me) if callable(slabs[n][0]) else slabs[n][0] for n in names]
        sent[group] = (names, own32, (snd, rcv, s_thru, l_thru))
        return tok

    stages.grads_out = grads_out
    small_sent = []

    def small_out(parts):
        packed = jnp.concatenate([parts[n] for n in SMALL_ORDER] + [parts["conv"],
                                 parts["gla_gate_w2"].reshape(SUBLANES, -1)], axis=1)
        packed = jnp.pad(packed, ((0, 0), (0, (-packed.shape[1]) % 2048)))
        zone = _landing(packed.shape, F32, packed, me)
        snd, rcv, s_thru, l_thru, tok = _spread_start([packed], [zone], packed, True, "ag_small_start")
        small_sent.append((snd, rcv, s_thru, l_thru))
        return tok

    stages.small_out = small_out
    small = dict(gla_gate_b=gla_gate_b, gla_norm_g=gla_norm_g, ln1_g=ln1_g, ln1_b=ln1_b, ln2_g=ln2_g, ln2_b=ln2_b,
                 ln3_g=ln3_g, ln3_b=ln3_b)

    loss_part, dx, grads, small_parts = _local_step(x[0], mem[0], positions[0], loss_target[0], p, small, stages)
    loss = lax.psum(jnp.sum(loss_part), ("x", "y", "c"))

    out = {}
    (allp,) = _spread_wait(*small_sent[0], dx, True, "ag_small_wait")
    dev_sum, row_sum = _small_reduce(allp)

    me1 = me.reshape(1).astype(jnp.int32)

    def finish_group(group, after):
        names, own32, handles = sent[group]
        landed = _spread_wait(*handles, after, False, f"rs_{group}_wait")
        for n, own, land in zip(names, own32, landed):
            m_, v_ = moms[n]
            res4 = _adamw_direct(travel(n, weights[n]), travel(n, m_), travel(n, v_), own, land, me1, f"adamw_{n}")
            out[n] = [travel(n, a) for a in res4]

    finish_group("ffn", dx)
    finish_group("attn", dx)
    off = 0
    for n in SMALL_ORDER:
        width = weights[n].shape[1]
        g = row_sum[0:1, off:off + width]
        off += width
        m_, v_ = moms[n]
        out[n] = _adamw(weights[n], m_, v_, g, f"adamw_{n}")
    conv_g = dev_sum[:, off:off + 2 * D_FF]
    off += 2 * D_FF
    g_cb = conv_g[3:4]
    out["ffn_conv_b"] = _adamw(ffn_conv_b, m_ffn_conv_b, v_ffn_conv_b, g_cb, "adamw_ffn_conv_b")
    wsh = ffn_conv_w.shape[2]
    g_cw = lax.dynamic_slice_in_dim(conv_g[0:3], me * wsh, wsh, axis=1)
    out["ffn_conv_w"] = _adamw(ffn_conv_w[0], m_ffn_conv_w[0], v_ffn_conv_w[0], g_cw, "adamw_ffn_conv_w")
    w2_g = dev_sum[:, off:off + GLA_RANK * GLA_HEADS * GLA_DK // SUBLANES].reshape(GLA_RANK, GLA_HEADS * GLA_DK)
    wsh2 = gla_gate_w2.shape[2]
    g_w2 = lax.dynamic_slice_in_dim(w2_g, me * wsh2, wsh2, axis=1)
    out["gla_gate_w2"] = _adamw(gla_gate_w2[0], m_gla_gate_w2[0], v_gla_gate_w2[0], g_w2, "adamw_gla_gate_w2")
    finish_group("w_in", [o[1] for o in out.values()])

    def shaped(n, a):
        return a.reshape(weights[n].shape)

    res = [loss, dx[None]]
    for k in range(4):
        res += [shaped(n, out[n][k]) for n in order]
    return tuple(res)


def _adamw_direct(w, m, v, own, land, me, name):
    _, r, c = w.shape
    tr, tc = _tile2d(r, c)
    blk = pl.BlockSpec((None, tr, tc), lambda i, j, s: (0, i, j))
    if own.ndim == 2:
        mine = pl.BlockSpec((tr, tc), lambda i, j, s: (i, j))
    else:
        mine = pl.BlockSpec((None, tr, tc), lambda i, j, s: (s[0], i, j))
    slots = [pl.BlockSpec((None, tr, tc), lambda i, j, s, k=k: (k, i, j)) for k in range(8)]

    def body(s_ref, w_ref, m_ref, v_ref, p_ref, *rest):
        slot_refs, (g_ref, d_ref, nm_ref, nv_ref) = rest[:8], rest[8:]
        g = p_ref[...]
        for sr in slot_refs:
            g = g + sr[...].astype(F32)
        d_ref[...], nm_ref[...], nv_ref[...] = _adamw_math(w_ref[...], m_ref[...], v_ref[...], g)
        g_ref[...] = g

    gs = pltpu.PrefetchScalarGridSpec(num_scalar_prefetch=1, grid=(r // tr, c // tc),
                                      in_specs=[blk, blk, blk, mine] + slots, out_specs=[blk] * 4)
    return pl.pallas_call(body, name=name, grid_spec=gs, out_shape=[jax.ShapeDtypeStruct((1, r, c), F32)] * 4,
                          compiler_params=_params(("parallel", "parallel")))(me, w, m, v, own, *([land] * 8))
```
